```python
import jax, jax.numpy as jnp
from jax import lax
import numpy as np

D_MODEL = 1024
BATCH = 16
SEQ = 2048
DEPTH = 4

N_MIXERS = 2
N_HEADS = 16
N_KV_HEADS = 4
HEAD_DIM = D_MODEL // N_HEADS
GROUP = N_HEADS // N_KV_HEADS
QKV_DIM = (N_HEADS + 2 * N_KV_HEADS) * HEAD_DIM
WINDOW = 128
BLOCK = 128
ROPE_THETA = 10000.0
CONV_WIDTH = 3
D_FF = 2816
N_SUBLAYERS = 3
N_ADA = 3 * N_SUBLAYERS
EPS = 1e-6
N_ATTN_LAYERS = (DEPTH + 1) // 2
N_CONV_LAYERS = DEPTH // 2

kernel_name = "hybrid_swa_sink_shortconv_macaron_adaln"


def rms_norm(x, gain):
    xf = x.astype(jnp.float32)
    y = xf * lax.rsqrt(jnp.mean(xf * xf, axis=-1, keepdims=True) + EPS)
    return (y * gain.astype(jnp.float32)).astype(x.dtype)


def modulate(h, shift, scale):
    return h * (1.0 + scale[:, None, :]) + shift[:, None, :]


def swiglu(h, w_up, w_down):
    gate, up = jnp.split(h @ w_up, 2, axis=-1)
    return (jax.nn.silu(gate) * up) @ w_down


def rope_tables(positions, dtype):
    inv_freq = ROPE_THETA ** (-jnp.arange(0, HEAD_DIM, 2, dtype=jnp.float32) / HEAD_DIM)
    ang = positions.astype(jnp.float32)[..., None] * inv_freq
    return jnp.cos(ang)[:, :, None, :].astype(dtype), jnp.sin(ang)[:, :, None, :].astype(dtype)


def apply_rope(t, cos, sin):
    t1, t2 = jnp.split(t, 2, axis=-1)
    return jnp.concatenate([t1 * cos - t2 * sin, t2 * cos + t1 * sin], axis=-1)


def swa_sink_attention(h, cos, sin, w_qkv, b_qkv, q_gain, k_gain, sinks, w_o, b_o):
    b, s, _ = h.shape
    nb = s // BLOCK
    qkv = h @ w_qkv + b_qkv
    q, k, v = jnp.split(qkv, [N_HEADS * HEAD_DIM, (N_HEADS + N_KV_HEADS) * HEAD_DIM], axis=-1)
    q = q.reshape(b, s, N_HEADS, HEAD_DIM)
    k = k.reshape(b, s, N_KV_HEADS, HEAD_DIM)
    v = v.reshape(b, s, N_KV_HEADS, HEAD_DIM)
    q = apply_rope(rms_norm(q, q_gain), cos, sin)
    k = apply_rope(rms_norm(k, k_gain), cos, sin)
    qb = q.reshape(b, nb, BLOCK, N_KV_HEADS, GROUP, HEAD_DIM)

    def banded(t):
        tb = t.reshape(b, nb, BLOCK, N_KV_HEADS, HEAD_DIM)
        prev = jnp.pad(tb[:, :-1], ((0, 0), (1, 0), (0, 0), (0, 0), (0, 0)))
        return jnp.concatenate([prev, tb], axis=2)

    kb, vb = banded(k), banded(v)
    scores = jnp.einsum('bnqkgd,bnskd->bnkgqs', qb, kb,
                        preferred_element_type=jnp.float32) * (HEAD_DIM ** -0.5)
    blk = jnp.arange(nb)[:, None, None] * BLOCK
    q_pos = blk + jnp.arange(BLOCK)[None, :, None]
    k_pos = blk - BLOCK + jnp.arange(2 * BLOCK)[None, None, :]
    rel = q_pos - k_pos
    mask = (rel >= 0) & (rel < WINDOW) & (k_pos >= 0)
    scores = jnp.where(mask[None, :, None, None], scores, -jnp.inf)
    sink = jnp.broadcast_to(sinks.astype(jnp.float32).reshape(1, 1, N_KV_HEADS, GROUP, 1, 1),
                            scores.shape[:-1] + (1,))
    probs = jax.nn.softmax(jnp.concatenate([scores, sink], axis=-1), axis=-1)[..., :-1]
    out = jnp.einsum('bnkgqs,bnskd->bnqkgd', probs.astype(vb.dtype), vb)
    return out.reshape(b, s, N_HEADS * HEAD_DIM) @ w_o + b_o


def short_gated_conv(h, w_in, conv_w, w_out):
    gb, gc, v = jnp.split(h @ w_in, 3, axis=-1)
    u = gc * v
    conv = lax.conv_general_dilated(
        u, conv_w[:, None, :], window_strides=(1,), padding=[(CONV_WIDTH - 1, 0)],
        dimension_numbers=('NWC', 'WIO', 'NWC'), feature_group_count=D_MODEL)
    return (gb * conv) @ w_out


def _fwd_setup_inputs(seed: int = 0) -> dict:
    key = jax.random.key(seed)
    ks = jax.random.split(key, 20)
    f32 = jnp.float32
    d = D_MODEL
    nrm = lambda k, shape, fan_in: jax.random.normal(k, shape, f32) * (fan_in ** -0.5)
    x = jax.random.normal(ks[0], (BATCH, SEQ, d), f32)
    c = jax.random.normal(ks[1], (BATCH, d), f32)
    positions = (jnp.arange(SEQ, dtype=jnp.int32)[None, :]
                 + jax.random.randint(ks[2], (BATCH, 1), 0, 4096, dtype=jnp.int32))
    norm_gain = 1.0 + 0.05 * jax.random.normal(ks[3], (DEPTH, N_SUBLAYERS, d), f32)
    w_ada = nrm(ks[4], (DEPTH, d, N_ADA * d), d)
    b_ada = 0.02 * jax.random.normal(ks[5], (DEPTH, N_ADA * d), f32)
    w_ffn_up = nrm(ks[6], (DEPTH, 2, d, 2 * D_FF), d)
    w_ffn_down = nrm(ks[7], (DEPTH, 2, D_FF, d), D_FF)
    attn_w_qkv = nrm(ks[8], (N_ATTN_LAYERS, d, QKV_DIM), d)
    attn_b_qkv = 0.02 * jax.random.normal(ks[9], (N_ATTN_LAYERS, QKV_DIM), f32)
    attn_q_gain = 1.0 + 0.05 * jax.random.normal(ks[10], (N_ATTN_LAYERS, HEAD_DIM), f32)
    attn_k_gain = 1.0 + 0.05 * jax.random.normal(ks[11], (N_ATTN_LAYERS, HEAD_DIM), f32)
    attn_sinks = 0.5 * jax.random.normal(ks[12], (N_ATTN_LAYERS, N_HEADS), f32)
    attn_w_o = nrm(ks[13], (N_ATTN_LAYERS, N_HEADS * HEAD_DIM, d), N_HEADS * HEAD_DIM)
    attn_b_o = 0.02 * jax.random.normal(ks[14], (N_ATTN_LAYERS, d), f32)
    conv_w_in = nrm(ks[15], (N_CONV_LAYERS, d, 3 * d), d)
    conv_w = nrm(ks[16], (N_CONV_LAYERS, CONV_WIDTH, d), CONV_WIDTH)
    conv_w_out = nrm(ks[17], (N_CONV_LAYERS, d, d), d)
    return {"x": x, "c": c, "positions": positions, "norm_gain": norm_gain,
            "w_ada": w_ada, "b_ada": b_ada, "w_ffn_up": w_ffn_up, "w_ffn_down": w_ffn_down,
            "attn_w_qkv": attn_w_qkv, "attn_b_qkv": attn_b_qkv, "attn_q_gain": attn_q_gain,
            "attn_k_gain": attn_k_gain, "attn_sinks": attn_sinks, "attn_w_o": attn_w_o,
            "attn_b_o": attn_b_o, "conv_w_in": conv_w_in, "conv_w": conv_w,
            "conv_w_out": conv_w_out}


def _fwd_reference(x, c, positions, norm_gain, w_ada, b_ada, w_ffn_up, w_ffn_down,
              attn_w_qkv, attn_b_qkv, attn_q_gain, attn_k_gain, attn_sinks, attn_w_o,
              attn_b_o, conv_w_in, conv_w, conv_w_out):
    cos, sin = rope_tables(positions, x.dtype)
    c_act = jax.nn.silu(c)
    for i in range(DEPTH):
        mod = c_act @ w_ada[i] + b_ada[i]
        sh1, sc1, g1, sh2, sc2, g2, sh3, sc3, g3 = jnp.split(mod, N_ADA, axis=-1)
        h = modulate(rms_norm(x, norm_gain[i, 0]), sh1, sc1)
        x = x + 0.5 * g1[:, None, :] * swiglu(h, w_ffn_up[i, 0], w_ffn_down[i, 0])
        h = modulate(rms_norm(x, norm_gain[i, 1]), sh2, sc2)
        j = i // N_MIXERS
        if i % N_MIXERS == 0:
            y = swa_sink_attention(h, cos, sin, attn_w_qkv[j], attn_b_qkv[j], attn_q_gain[j],
                                   attn_k_gain[j], attn_sinks[j], attn_w_o[j], attn_b_o[j])
        else:
            y = short_gated_conv(h, conv_w_in[j], conv_w[j], conv_w_out[j])
        x = x + g2[:, None, :] * y
        h = modulate(rms_norm(x, norm_gain[i, 2]), sh3, sc3)
        x = x + 0.5 * g3[:, None, :] * swiglu(h, w_ffn_up[i, 1], w_ffn_down[i, 1])
    return x


import jax as _jax
import jax.numpy as _jnp

TWIN_FORMAT = 'train_step'
FWD_PARAMS = ['x', 'c', 'positions', 'norm_gain', 'w_ada', 'b_ada', 'w_ffn_up', 'w_ffn_down', 'attn_w_qkv', 'attn_b_qkv', 'attn_q_gain', 'attn_k_gain', 'attn_sinks', 'attn_w_o', 'attn_b_o', 'conv_w_in', 'conv_w', 'conv_w_out']
TWIN_WEIGHTS = ['norm_gain', 'w_ada', 'b_ada', 'w_ffn_up', 'w_ffn_down', 'attn_w_qkv', 'attn_b_qkv', 'attn_q_gain', 'attn_k_gain', 'attn_sinks', 'attn_w_o', 'attn_b_o', 'conv_w_in', 'conv_w', 'conv_w_out']
TWIN_DIFF_INPUT = 'x'
TWIN_INPUTS = ['x', 'c', 'positions', 'norm_gain', 'w_ada', 'b_ada', 'w_ffn_up', 'w_ffn_down', 'attn_w_qkv', 'attn_b_qkv', 'attn_q_gain', 'attn_k_gain', 'attn_sinks', 'attn_w_o', 'attn_b_o', 'conv_w_in', 'conv_w', 'conv_w_out', 'loss_target', 'm_norm_gain', 'm_w_ada', 'm_b_ada', 'm_w_ffn_up', 'm_w_ffn_down', 'm_attn_w_qkv', 'm_attn_b_qkv', 'm_attn_q_gain', 'm_attn_k_gain', 'm_attn_sinks', 'm_attn_w_o', 'm_attn_b_o', 'm_conv_w_in', 'm_conv_w', 'm_conv_w_out', 'v_norm_gain', 'v_w_ada', 'v_b_ada', 'v_w_ffn_up', 'v_w_ffn_down', 'v_attn_w_qkv', 'v_attn_b_qkv', 'v_attn_q_gain', 'v_attn_k_gain', 'v_attn_sinks', 'v_attn_w_o', 'v_attn_b_o', 'v_conv_w_in', 'v_conv_w', 'v_conv_w_out']
TWIN_OUTPUTS = ['loss', 'grad_x', 'grad_norm_gain', 'grad_w_ada', 'grad_b_ada', 'grad_w_ffn_up', 'grad_w_ffn_down', 'grad_attn_w_qkv', 'grad_attn_b_qkv', 'grad_attn_q_gain', 'grad_attn_k_gain', 'grad_attn_sinks', 'grad_attn_w_o', 'grad_attn_b_o', 'grad_conv_w_in', 'grad_conv_w', 'grad_conv_w_out', 'delta_norm_gain', 'delta_w_ada', 'delta_b_ada', 'delta_w_ffn_up', 'delta_w_ffn_down', 'delta_attn_w_qkv', 'delta_attn_b_qkv', 'delta_attn_q_gain', 'delta_attn_k_gain', 'delta_attn_sinks', 'delta_attn_w_o', 'delta_attn_b_o', 'delta_conv_w_in', 'delta_conv_w', 'delta_conv_w_out', 'new_m_norm_gain', 'new_m_w_ada', 'new_m_b_ada', 'new_m_w_ffn_up', 'new_m_w_ffn_down', 'new_m_attn_w_qkv', 'new_m_attn_b_qkv', 'new_m_attn_q_gain', 'new_m_attn_k_gain', 'new_m_attn_sinks', 'new_m_attn_w_o', 'new_m_attn_b_o', 'new_m_conv_w_in', 'new_m_conv_w', 'new_m_conv_w_out', 'new_v_norm_gain', 'new_v_w_ada', 'new_v_b_ada', 'new_v_w_ffn_up', 'new_v_w_ffn_down', 'new_v_attn_w_qkv', 'new_v_attn_b_qkv', 'new_v_attn_q_gain', 'new_v_attn_k_gain', 'new_v_attn_sinks', 'new_v_attn_w_o', 'new_v_attn_b_o', 'new_v_conv_w_in', 'new_v_conv_w', 'new_v_conv_w_out']
TWIN_LEAF_KINDS = {'loss': 'loss', 'grad_x': 'grad_x', 'grad_norm_gain': 'grad_w', 'grad_w_ada': 'grad_w', 'grad_b_ada': 'grad_w', 'grad_w_ffn_up': 'grad_w', 'grad_w_ffn_down': 'grad_w', 'grad_attn_w_qkv': 'grad_w', 'grad_attn_b_qkv': 'grad_w', 'grad_attn_q_gain': 'grad_w', 'grad_attn_k_gain': 'grad_w', 'grad_attn_sinks': 'grad_w', 'grad_attn_w_o': 'grad_w', 'grad_attn_b_o': 'grad_w', 'grad_conv_w_in': 'grad_w', 'grad_conv_w': 'grad_w', 'grad_conv_w_out': 'grad_w', 'delta_norm_gain': 'delta_w', 'delta_w_ada': 'delta_w', 'delta_b_ada': 'delta_w', 'delta_w_ffn_up': 'delta_w', 'delta_w_ffn_down': 'delta_w', 'delta_attn_w_qkv': 'delta_w', 'delta_attn_b_qkv': 'delta_w', 'delta_attn_q_gain': 'delta_w', 'delta_attn_k_gain': 'delta_w', 'delta_attn_sinks': 'delta_w', 'delta_attn_w_o': 'delta_w', 'delta_attn_b_o': 'delta_w', 'delta_conv_w_in': 'delta_w', 'delta_conv_w': 'delta_w', 'delta_conv_w_out': 'delta_w', 'new_m_norm_gain': 'new_m', 'new_m_w_ada': 'new_m', 'new_m_b_ada': 'new_m', 'new_m_w_ffn_up': 'new_m', 'new_m_w_ffn_down': 'new_m', 'new_m_attn_w_qkv': 'new_m', 'new_m_attn_b_qkv': 'new_m', 'new_m_attn_q_gain': 'new_m', 'new_m_attn_k_gain': 'new_m', 'new_m_attn_sinks': 'new_m', 'new_m_attn_w_o': 'new_m', 'new_m_attn_b_o': 'new_m', 'new_m_conv_w_in': 'new_m', 'new_m_conv_w': 'new_m', 'new_m_conv_w_out': 'new_m', 'new_v_norm_gain': 'new_v', 'new_v_w_ada': 'new_v', 'new_v_b_ada': 'new_v', 'new_v_w_ffn_up': 'new_v', 'new_v_w_ffn_down': 'new_v', 'new_v_attn_w_qkv': 'new_v', 'new_v_attn_b_qkv': 'new_v', 'new_v_attn_q_gain': 'new_v', 'new_v_attn_k_gain': 'new_v', 'new_v_attn_sinks': 'new_v', 'new_v_attn_w_o': 'new_v', 'new_v_attn_b_o': 'new_v', 'new_v_conv_w_in': 'new_v', 'new_v_conv_w': 'new_v', 'new_v_conv_w_out': 'new_v'}


def _forward(args):
    return _fwd_reference(*[args[k] for k in FWD_PARAMS])


def _output_shape():
    out = _jax.eval_shape(lambda: _forward(_fwd_setup_inputs(0)))
    return out.shape, out.dtype

N_MICROBATCH = 1
ADAM_LR = 0.001
ADAM_B1 = 0.9
ADAM_B2 = 0.999
ADAM_EPS = 1e-08
ADAM_WD = 0.01
ADAM_STEP = 10
PER_EXAMPLE_BATCH_AXIS = {'x': 0, 'c': 0, 'positions': 0, 'loss_target': 0}
SHARED_INPUTS = []
_WEIGHT_DTYPES = {'norm_gain': _jnp.float32, 'w_ada': _jnp.float32, 'b_ada': _jnp.float32, 'w_ffn_up': _jnp.float32, 'w_ffn_down': _jnp.float32, 'attn_w_qkv': _jnp.float32, 'attn_b_qkv': _jnp.float32, 'attn_q_gain': _jnp.float32, 'attn_k_gain': _jnp.float32, 'attn_sinks': _jnp.float32, 'attn_w_o': _jnp.float32, 'attn_b_o': _jnp.float32, 'conv_w_in': _jnp.float32, 'conv_w': _jnp.float32, 'conv_w_out': _jnp.float32}
MOMENT_SCALE = {'norm_gain': 8.222152e+01, 'w_ada': 2.135835e+01, 'b_ada': 4.286993e+01, 'w_ffn_up': 2.599908e+00, 'w_ffn_down': 4.898135e+00, 'attn_w_qkv': 1.896232e+01, 'attn_b_qkv': 2.930909e+01, 'attn_q_gain': 3.287605e+00, 'attn_k_gain': 3.494091e+00, 'attn_sinks': 1.410214e+00, 'attn_w_o': 2.311277e+01, 'attn_b_o': 3.655411e+01, 'conv_w_in': 1.703382e+01, 'conv_w': 4.348720e+01, 'conv_w_out': 7.538319e+00}


def _to_microbatches(a, axis):
    t = _jnp.moveaxis(a, axis, 0)
    t = t.reshape((N_MICROBATCH, t.shape[0] // N_MICROBATCH) + t.shape[1:])
    return _jnp.moveaxis(t, 1, axis + 1)


def setup_inputs(seed: int = 0) -> dict:
    inp = _fwd_setup_inputs(seed)
    key = _jax.random.fold_in(_jax.random.key(seed), 7919)
    shape, _ = _output_shape()
    out = dict(inp)
    out["loss_target"] = _jax.random.normal(_jax.random.fold_in(key, 0), shape, _jnp.float32)
    for i, name in enumerate(TWIN_WEIGHTS):
        w = inp[name].astype(_jnp.float32)
        if MOMENT_SCALE is None:
            s = _jnp.sqrt(_jnp.mean(_jnp.square(w)) + 1e-30)
        else:
            s = MOMENT_SCALE[name]
        km, kv = _jax.random.split(_jax.random.fold_in(key, i + 1))
        out[name] = w
        out["m_" + name] = s * _jax.random.normal(km, w.shape, _jnp.float32)
        out["v_" + name] = (s * s) * _jax.random.uniform(kv, w.shape, _jnp.float32, 0.5, 1.5)
    if N_MICROBATCH > 1:
        for name, axis in PER_EXAMPLE_BATCH_AXIS.items():
            out[name] = _to_microbatches(out[name], axis)
    return {'x': out['x'], 'c': out['c'], 'positions': out['positions'], 'norm_gain': out['norm_gain'], 'w_ada': out['w_ada'], 'b_ada': out['b_ada'], 'w_ffn_up': out['w_ffn_up'], 'w_ffn_down': out['w_ffn_down'], 'attn_w_qkv': out['attn_w_qkv'], 'attn_b_qkv': out['attn_b_qkv'], 'attn_q_gain': out['attn_q_gain'], 'attn_k_gain': out['attn_k_gain'], 'attn_sinks': out['attn_sinks'], 'attn_w_o': out['attn_w_o'], 'attn_b_o': out['attn_b_o'], 'conv_w_in': out['conv_w_in'], 'conv_w': out['conv_w'], 'conv_w_out': out['conv_w_out'], 'loss_target': out['loss_target'], 'm_norm_gain': out['m_norm_gain'], 'm_w_ada': out['m_w_ada'], 'm_b_ada': out['m_b_ada'], 'm_w_ffn_up': out['m_w_ffn_up'], 'm_w_ffn_down': out['m_w_ffn_down'], 'm_attn_w_qkv': out['m_attn_w_qkv'], 'm_attn_b_qkv': out['m_attn_b_qkv'], 'm_attn_q_gain': out['m_attn_q_gain'], 'm_attn_k_gain': out['m_attn_k_gain'], 'm_attn_sinks': out['m_attn_sinks'], 'm_attn_w_o': out['m_attn_w_o'], 'm_attn_b_o': out['m_attn_b_o'], 'm_conv_w_in': out['m_conv_w_in'], 'm_conv_w': out['m_conv_w'], 'm_conv_w_out': out['m_conv_w_out'], 'v_norm_gain': out['v_norm_gain'], 'v_w_ada': out['v_w_ada'], 'v_b_ada': out['v_b_ada'], 'v_w_ffn_up': out['v_w_ffn_up'], 'v_w_ffn_down': out['v_w_ffn_down'], 'v_attn_w_qkv': out['v_attn_w_qkv'], 'v_attn_b_qkv': out['v_attn_b_qkv'], 'v_attn_q_gain': out['v_attn_q_gain'], 'v_attn_k_gain': out['v_attn_k_gain'], 'v_attn_sinks': out['v_attn_sinks'], 'v_attn_w_o': out['v_attn_w_o'], 'v_attn_b_o': out['v_attn_b_o'], 'v_conv_w_in': out['v_conv_w_in'], 'v_conv_w': out['v_conv_w'], 'v_conv_w_out': out['v_conv_w_out']}


def _loss(weights, diff, rest, loss_target):
    with _jax.named_scope("forward"):
        args = {**rest, TWIN_DIFF_INPUT: diff, **{k: w.astype(_WEIGHT_DTYPES[k]) for k, w in weights.items()}}
        y = _forward(args)
    with _jax.named_scope("loss_head"):
        err = _jnp.square(y.astype(_jnp.float32) - loss_target)
        return 0.5 * _jnp.sum(_jnp.mean(err, axis=-1)) if err.ndim else 0.5 * err


def _adamw(w, g, m, v):
    m = ADAM_B1 * m + (1.0 - ADAM_B1) * g
    v = ADAM_B2 * v + (1.0 - ADAM_B2) * _jnp.square(g)
    m_hat = m / (1.0 - ADAM_B1 ** ADAM_STEP)
    v_hat = v / (1.0 - ADAM_B2 ** ADAM_STEP)
    delta = -ADAM_LR * (m_hat / (_jnp.sqrt(v_hat) + ADAM_EPS) + ADAM_WD * w)
    return delta, m, v


def reference(x, c, positions, norm_gain, w_ada, b_ada, w_ffn_up, w_ffn_down, attn_w_qkv, attn_b_qkv, attn_q_gain, attn_k_gain, attn_sinks, attn_w_o, attn_b_o, conv_w_in, conv_w, conv_w_out, loss_target, m_norm_gain, m_w_ada, m_b_ada, m_w_ffn_up, m_w_ffn_down, m_attn_w_qkv, m_attn_b_qkv, m_attn_q_gain, m_attn_k_gain, m_attn_sinks, m_attn_w_o, m_attn_b_o, m_conv_w_in, m_conv_w, m_conv_w_out, v_norm_gain, v_w_ada, v_b_ada, v_w_ffn_up, v_w_ffn_down, v_attn_w_qkv, v_attn_b_qkv, v_attn_q_gain, v_attn_k_gain, v_attn_sinks, v_attn_w_o, v_attn_b_o, v_conv_w_in, v_conv_w, v_conv_w_out):
    given = dict(x=x, c=c, positions=positions, norm_gain=norm_gain, w_ada=w_ada, b_ada=b_ada, w_ffn_up=w_ffn_up, w_ffn_down=w_ffn_down, attn_w_qkv=attn_w_qkv, attn_b_qkv=attn_b_qkv, attn_q_gain=attn_q_gain, attn_k_gain=attn_k_gain, attn_sinks=attn_sinks, attn_w_o=attn_w_o, attn_b_o=attn_b_o, conv_w_in=conv_w_in, conv_w=conv_w, conv_w_out=conv_w_out, loss_target=loss_target, m_norm_gain=m_norm_gain, m_w_ada=m_w_ada, m_b_ada=m_b_ada, m_w_ffn_up=m_w_ffn_up, m_w_ffn_down=m_w_ffn_down, m_attn_w_qkv=m_attn_w_qkv, m_attn_b_qkv=m_attn_b_qkv, m_attn_q_gain=m_attn_q_gain, m_attn_k_gain=m_attn_k_gain, m_attn_sinks=m_attn_sinks, m_attn_w_o=m_attn_w_o, m_attn_b_o=m_attn_b_o, m_conv_w_in=m_conv_w_in, m_conv_w=m_conv_w, m_conv_w_out=m_conv_w_out, v_norm_gain=v_norm_gain, v_w_ada=v_w_ada, v_b_ada=v_b_ada, v_w_ffn_up=v_w_ffn_up, v_w_ffn_down=v_w_ffn_down, v_attn_w_qkv=v_attn_w_qkv, v_attn_b_qkv=v_attn_b_qkv, v_attn_q_gain=v_attn_q_gain, v_attn_k_gain=v_attn_k_gain, v_attn_sinks=v_attn_sinks, v_attn_w_o=v_attn_w_o, v_attn_b_o=v_attn_b_o, v_conv_w_in=v_conv_w_in, v_conv_w=v_conv_w, v_conv_w_out=v_conv_w_out)
    weights = {n: given[n] for n in TWIN_WEIGHTS}
    shared = {n: given[n] for n in SHARED_INPUTS}
    per_example = {n: given[n] for n in ['x', 'c', 'positions']}
    grad_fn = _jax.value_and_grad(_loss, argnums=(0, 1))

    def one_microbatch(ex, loss_target):
        ex = dict(ex)
        diff = ex.pop(TWIN_DIFF_INPUT)
        return grad_fn(weights, diff, {**shared, **ex}, loss_target)

    if N_MICROBATCH == 1:
        loss, (grad_w, grad_x) = one_microbatch(per_example, given["loss_target"])
    else:
        def body(carry, xs):
            loss_sum, grad_sum = carry
            l_k, (gw_k, gx_k) = one_microbatch(xs[0], xs[1])
            with _jax.named_scope("update"):
                return (loss_sum + l_k, _jax.tree.map(_jnp.add, grad_sum, gw_k)), gx_k

        init = (_jnp.zeros((), _jnp.float32), _jax.tree.map(_jnp.zeros_like, weights))
        (loss, grad_w), grad_x = _jax.lax.scan(body, init, (per_example, given["loss_target"]))
    with _jax.named_scope("update"):
        delta_w, new_m, new_v = {}, {}, {}
        for n in TWIN_WEIGHTS:
            delta_w[n], new_m[n], new_v[n] = _adamw(weights[n], grad_w[n], given["m_" + n], given["v_" + n])
    return (loss, grad_x, *[grad_w[n] for n in TWIN_WEIGHTS], *[delta_w[n] for n in TWIN_WEIGHTS],
            *[new_m[n] for n in TWIN_WEIGHTS], *[new_v[n] for n in TWIN_WEIGHTS])
```

```python
import functools

import jax
import jax.numpy as jnp
from jax import lax
from jax.experimental import pallas as pl
from jax.experimental.pallas import tpu as pltpu

F32 = jnp.float32
BF16 = jnp.bfloat16

D = 1024
D_FF = 2816
N_HEADS = 16
N_KV = 4
HEAD_DIM = 64
GROUP = N_HEADS // N_KV
QK_DIM = (N_HEADS + N_KV) * HEAD_DIM
QKV_DIM = QK_DIM + N_KV * HEAD_DIM
BLOCK = 128
ROPE_THETA = 10000.0
EPS = 1e-6
DEPTH = 4
N_CHIPS = 4

ADAM_LR = 0.001
ADAM_B1 = 0.9
ADAM_B2 = 0.999
ADAM_EPS = 1e-08
ADAM_WD = 0.01
ADAM_STEP = 10

V7X_VMEM_BYTES = 64 * 1024 * 1024
FF_CHUNK = 1408
MESH = pl.DeviceIdType.MESH
ANY = pl.BlockSpec(memory_space=pl.ANY)


def _cparams(vmem_mb, n_grid):
    assert vmem_mb * 1024 * 1024 <= V7X_VMEM_BYTES
    return pltpu.CompilerParams(vmem_limit_bytes=vmem_mb * 1024 * 1024,
                                dimension_semantics=("arbitrary",) * n_grid)


def _resident(shape):
    nd = len(shape)
    return pl.BlockSpec(shape, lambda *_: (0,) * nd, pipeline_mode=pl.Buffered(1))


def _layer(w, l):
    return pl.BlockSpec((None,) + w.shape[1:], lambda *_: (l, 0, 0), pipeline_mode=pl.Buffered(1))


def _dot(a, b):
    return jnp.dot(a, b, preferred_element_type=F32)


def _dot_nt(a, b):
    return lax.dot_general(a, b, (((1,), (1,)), ((), ())), preferred_element_type=F32)


def _dot_tn(a, b):
    return lax.dot_general(a, b, (((0,), (0,)), ((), ())), preferred_element_type=F32)


def _dot_hilo(a, g):
    hi = a.astype(BF16)
    lo = (a - hi.astype(F32)).astype(BF16)
    return _dot(hi, g) + _dot(lo, g)


def _colsum(a):
    return jnp.sum(a, axis=0, keepdims=True)


def _norm_mod(x, gain, sc, sh):
    r = lax.rsqrt(jnp.mean(x * x, axis=-1, keepdims=True) + EPS)
    n = x * r * gain
    return r, n, n * (1.0 + sc) + sh


def _mod_rows(mod_ref, s):
    return (mod_ref[0, 3 * s:3 * s + 1, :], mod_ref[0, 3 * s + 1:3 * s + 2, :], mod_ref[0, 3 * s + 2:3 * s + 3, :])


def ffn_fwd(x, gain, mod, wup, wdn, l, s, tm=256):
    t = x.shape[0]
    tpe = t // tm // 2

    def body(x_ref, gain_ref, mod_ref, wup_ref, wdn_ref, xo_ref, u_ref, f_ref):
        xv = x_ref[...]
        sh, sc, g = _mod_rows(mod_ref, s)
        _, _, h = _norm_mod(xv, gain_ref[...], sc, sh)
        hb = h.astype(BF16)
        acc = jnp.zeros((tm, D), F32)
        for j in range(D_FF // FF_CHUNK):
            lo, hi = j * FF_CHUNK, (j + 1) * FF_CHUNK
            gate = _dot(hb, wup_ref[:, lo:hi])
            up = _dot(hb, wup_ref[:, D_FF + lo:D_FF + hi])
            u_ref[:, lo:hi] = gate.astype(BF16)
            u_ref[:, D_FF + lo:D_FF + hi] = up.astype(BF16)
            a = (gate * jax.nn.sigmoid(gate) * up).astype(BF16)
            acc = acc + _dot(a, wdn_ref[lo:hi, :])
        f_ref[...] = acc.astype(BF16)
        xo_ref[...] = xv + 0.5 * g * acc

    return pl.pallas_call(
        body, name="ffn_fwd", grid=(t // tm,),
        in_specs=[pl.BlockSpec((tm, D), lambda i: (i, 0)),
                  pl.BlockSpec((1, D), lambda i: (0, 0)),
                  pl.BlockSpec((1, 9, D), lambda i: (i // tpe, 0, 0)),
                  _layer(wup, l), _layer(wdn, l)],
        out_specs=[pl.BlockSpec((tm, D), lambda i: (i, 0)),
                   pl.BlockSpec((tm, 2 * D_FF), lambda i: (i, 0)),
                   pl.BlockSpec((tm, D), lambda i: (i, 0))],
        out_shape=[jax.ShapeDtypeStruct((t, D), F32), jax.ShapeDtypeStruct((t, 2 * D_FF), BF16),
                   jax.ShapeDtypeStruct((t, D), BF16)],
        compiler_params=_cparams(48, 1),
    )(x, gain, mod, wup, wdn)


def ffn_bwd_act(dy, u, f, mod, wdn, l, s, tm=256):
    t = dy.shape[0]
    tpe = t // tm // 2

    def body(dy_ref, u_ref, f_ref, mod_ref, wdn_ref, du_ref, a_ref, df_ref, sg_ref):
        i = pl.program_id(0)
        dyv = dy_ref[...]
        _, _, g = _mod_rows(mod_ref, s)
        dfb = (0.5 * g * dyv).astype(BF16)
        df_ref[...] = dfb

        @pl.when(i % tpe == 0)
        def _():
            sg_ref[...] = jnp.zeros_like(sg_ref)

        sg_ref[0, 0:1, :] += _colsum(0.5 * dyv * f_ref[...].astype(F32))
        for j in range(D_FF // FF_CHUNK):
            lo, hi = j * FF_CHUNK, (j + 1) * FF_CHUNK
            da = _dot_nt(dfb, wdn_ref[lo:hi, :])
            gate = u_ref[:, lo:hi].astype(F32)
            up = u_ref[:, D_FF + lo:D_FF + hi].astype(F32)
            sg = jax.nn.sigmoid(gate)
            silu = gate * sg
            a_ref[:, lo:hi] = (silu * up).astype(BF16)
            du_ref[:, lo:hi] = (da * up * (sg * (1.0 + gate * (1.0 - sg)))).astype(BF16)
            du_ref[:, D_FF + lo:D_FF + hi] = (da * silu).astype(BF16)

    return pl.pallas_call(
        body, name="ffn_bwd_act", grid=(t // tm,),
        in_specs=[pl.BlockSpec((tm, D), lambda i: (i, 0)),
                  pl.BlockSpec((tm, 2 * D_FF), lambda i: (i, 0)),
                  pl.BlockSpec((tm, D), lambda i: (i, 0)),
                  pl.BlockSpec((1, 9, D), lambda i: (i // tpe, 0, 0)),
                  _layer(wdn, l)],
        out_specs=[pl.BlockSpec((tm, 2 * D_FF), lambda i: (i, 0)),
                   pl.BlockSpec((tm, D_FF), lambda i: (i, 0)),
                   pl.BlockSpec((tm, D), lambda i: (i, 0)),
                   pl.BlockSpec((1, 8, D), lambda i: (i // tpe, 0, 0))],
        out_shape=[jax.ShapeDtypeStruct((t, 2 * D_FF), BF16), jax.ShapeDtypeStruct((t, D_FF), BF16),
                   jax.ShapeDtypeStruct((t, D), BF16), jax.ShapeDtypeStruct((2, 8, D), F32)],
        compiler_params=_cparams(40, 1),
    )(dy, u, f, mod, wdn)


def lin_bwd(dy, dz, w, l, x, gain, mod, s, want_db, tm=256):
    t = dy.shape[0]
    n = w.shape[2]
    tpe = t // tm // 2
    nck = -(-n // 1536)
    ck = n // nck

    def body(dy_ref, dz_ref, w_ref, x_ref, gain_ref, mod_ref, dx_ref, h_ref, se_ref, sa_ref, *db_ref):
        i = pl.program_id(0)
        xv = x_ref[...]
        gain_v = gain_ref[...]
        sh, sc, _ = _mod_rows(mod_ref, s)
        r, nrm, h = _norm_mod(xv, gain_v, sc, sh)
        h_ref[...] = h.astype(BF16)
        dh = jnp.zeros((tm, D), F32)
        for j in range(nck):
            dh = dh + _dot_nt(dz_ref[:, j * ck:(j + 1) * ck], w_ref[:, j * ck:(j + 1) * ck])
        dn = dh * (1.0 + sc)
        dxr = dn * gain_v
        m = jnp.mean(dxr * xv, axis=-1, keepdims=True)
        dx_ref[...] = dy_ref[...] + r * dxr - xv * (r * r * r) * m

        @pl.when(i % tpe == 0)
        def _():
            se_ref[...] = jnp.zeros_like(se_ref)

        @pl.when(i == 0)
        def _():
            sa_ref[...] = jnp.zeros_like(sa_ref)
            if want_db:
                db_ref[0][...] = jnp.zeros_like(db_ref[0])

        se_ref[0, 0:1, :] += _colsum(dh)
        se_ref[0, 1:2, :] += _colsum(dh * nrm)
        sa_ref[0:1, :] += _colsum(dn * xv * r)
        if want_db:
            db_ref[0][0:1, :] += _colsum(dz_ref[...].astype(F32))

    out_specs = [pl.BlockSpec((tm, D), lambda i: (i, 0)), pl.BlockSpec((tm, D), lambda i: (i, 0)),
                 pl.BlockSpec((1, 8, D), lambda i: (i // tpe, 0, 0)), pl.BlockSpec((8, D), lambda i: (0, 0))]
    out_shape = [jax.ShapeDtypeStruct((t, D), F32), jax.ShapeDtypeStruct((t, D), BF16),
                 jax.ShapeDtypeStruct((2, 8, D), F32), jax.ShapeDtypeStruct((8, D), F32)]
    if want_db:
        out_specs.append(pl.BlockSpec((8, n), lambda i: (0, 0)))
        out_shape.append(jax.ShapeDtypeStruct((8, n), F32))
    return pl.pallas_call(
        body, name="lin_bwd", grid=(t // tm,),
        in_specs=[pl.BlockSpec((tm, D), lambda i: (i, 0)),
                  pl.BlockSpec((tm, n), lambda i: (i, 0)),
                  _layer(w, l),
                  pl.BlockSpec((tm, D), lambda i: (i, 0)),
                  pl.BlockSpec((1, D), lambda i: (0, 0)),
                  pl.BlockSpec((1, 9, D), lambda i: (i // tpe, 0, 0))],
        out_specs=out_specs, out_shape=out_shape,
        compiler_params=_cparams(40, 1),
    )(dy, dz, w, x, gain, mod)


def wgrad(gstack, l, a, b, bm, bn, bt=512):
    t, m = a.shape
    n = b.shape[1]
    nt = t // bt

    def body(g_ref, a_ref, b_ref, o_ref, acc_ref):
        k = pl.program_id(2)

        @pl.when(k == 0)
        def _():
            acc_ref[...] = jnp.zeros_like(acc_ref)

        acc_ref[...] += _dot_tn(a_ref[...], b_ref[...])

        @pl.when(k == nt - 1)
        def _():
            o_ref[...] = acc_ref[...].astype(BF16)

    return pl.pallas_call(
        body, name="wgrad", grid=(m // bm, n // bn, nt),
        in_specs=[ANY, pl.BlockSpec((bt, bm), lambda i, j, k: (k, i)),
                  pl.BlockSpec((bt, bn), lambda i, j, k: (k, j))],
        out_specs=pl.BlockSpec((None, bm, bn), lambda i, j, k: (l, i, j)),
        out_shape=jax.ShapeDtypeStruct(gstack.shape, BF16),
        input_output_aliases={0: 0},
        scratch_shapes=[pltpu.VMEM((bm, bn), F32)],
        compiler_params=_cparams(40, 3),
    )(gstack, a, b)


def proj_res(x, o, w, l, b, mod, tm=512):
    t = x.shape[0]
    tpe = t // tm // 2

    def body(x_ref, o_ref, w_ref, b_ref, mod_ref, xo_ref, y_ref):
        _, _, g = _mod_rows(mod_ref, 1)
        y = _dot(o_ref[...], w_ref[...]) + b_ref[...]
        y_ref[...] = y.astype(BF16)
        xo_ref[...] = x_ref[...] + g * y

    return pl.pallas_call(
        body, name="proj_res", grid=(t // tm,),
        in_specs=[pl.BlockSpec((tm, D), lambda i: (i, 0)), pl.BlockSpec((tm, D), lambda i: (i, 0)),
                  _layer(w, l), pl.BlockSpec((1, D), lambda i: (0, 0)),
                  pl.BlockSpec((1, 9, D), lambda i: (i // tpe, 0, 0))],
        out_specs=[pl.BlockSpec((tm, D), lambda i: (i, 0)), pl.BlockSpec((tm, D), lambda i: (i, 0))],
        out_shape=[jax.ShapeDtypeStruct((t, D), F32), jax.ShapeDtypeStruct((t, D), BF16)],
        compiler_params=_cparams(32, 1),
    )(x, o, w, b, mod)


def proj_res_bwd(dy, y, w, l, mod, tm=512):
    t = dy.shape[0]
    tpe = t // tm // 2

    def body(dy_ref, y_ref, w_ref, mod_ref, dyy_ref, do_ref, se_ref, sa_ref):
        i = pl.program_id(0)
        _, _, g = _mod_rows(mod_ref, 1)
        dyv = dy_ref[...]
        dyy = g * dyv
        dyb = dyy.astype(BF16)
        dyy_ref[...] = dyb
        do_ref[...] = _dot_nt(dyb, w_ref[...]).astype(BF16)

        @pl.when(i % tpe == 0)
        def _():
            se_ref[...] = jnp.zeros_like(se_ref)

        @pl.when(i == 0)
        def _():
            sa_ref[...] = jnp.zeros_like(sa_ref)

        se_ref[0, 0:1, :] += _colsum(dyv * y_ref[...].astype(F32))
        sa_ref[0:1, :] += _colsum(dyy)

    return pl.pallas_call(
        body, name="proj_res_bwd", grid=(t // tm,),
        in_specs=[pl.BlockSpec((tm, D), lambda i: (i, 0)), pl.BlockSpec((tm, D), lambda i: (i, 0)),
                  _layer(w, l), pl.BlockSpec((1, 9, D), lambda i: (i // tpe, 0, 0))],
        out_specs=[pl.BlockSpec((tm, D), lambda i: (i, 0)), pl.BlockSpec((tm, D), lambda i: (i, 0)),
                   pl.BlockSpec((1, 8, D), lambda i: (i // tpe, 0, 0)), pl.BlockSpec((8, D), lambda i: (0, 0))],
        out_shape=[jax.ShapeDtypeStruct((t, D), BF16), jax.ShapeDtypeStruct((t, D), BF16),
                   jax.ShapeDtypeStruct((2, 8, D), F32), jax.ShapeDtypeStruct((8, D), F32)],
        compiler_params=_cparams(32, 1),
    )(dy, y, w, mod)


def lin_fwd(x, gain, mod, w, l, s, tm=512):
    t = x.shape[0]
    n = w.shape[2]
    tpe = t // tm // 2

    def body(x_ref, gain_ref, mod_ref, w_ref, z_ref):
        sh, sc, _ = _mod_rows(mod_ref, s)
        _, _, h = _norm_mod(x_ref[...], gain_ref[...], sc, sh)
        z_ref[...] = _dot(h.astype(BF16), w_ref[...]).astype(BF16)

    return pl.pallas_call(
        body, name="lin_fwd", grid=(t // tm,),
        in_specs=[pl.BlockSpec((tm, D), lambda i: (i, 0)), pl.BlockSpec((1, D), lambda i: (0, 0)),
                  pl.BlockSpec((1, 9, D), lambda i: (i // tpe, 0, 0)), _layer(w, l)],
        out_specs=pl.BlockSpec((tm, n), lambda i: (i, 0)),
        out_shape=jax.ShapeDtypeStruct((t, n), BF16),
        compiler_params=_cparams(40, 1),
    )(x, gain, mod, w)


def rope_tables(pos, invf):
    t = pos.shape[0]
    tm = 1024

    def body(pos_ref, invf_ref, c_ref, s_ref):
        ang = pos_ref[...].astype(F32) * invf_ref[...]
        lane = lax.broadcasted_iota(jnp.int32, (tm, 128), 1)
        sign = jnp.where(lane % HEAD_DIM < HEAD_DIM // 2, -1.0, 1.0)
        c_ref[...] = jnp.cos(ang)
        s_ref[...] = sign * jnp.sin(ang)

    return pl.pallas_call(
        body, name="rope_tables", grid=(t // tm,),
        in_specs=[pl.BlockSpec((tm, 1), lambda i: (i, 0)), pl.BlockSpec((1, 128), lambda i: (0, 0))],
        out_specs=[pl.BlockSpec((tm, 128), lambda i: (i, 0))] * 2,
        out_shape=[jax.ShapeDtypeStruct((t, 128), F32)] * 2,
        compiler_params=_cparams(16, 1),
    )(pos, invf)


def _swap_halves(v):
    lane = lax.broadcasted_iota(jnp.int32, v.shape, 1)
    return jnp.where(lane % HEAD_DIM < HEAD_DIM // 2, pltpu.roll(v, 128 - HEAD_DIM // 2, 1), pltpu.roll(v, HEAD_DIM // 2, 1))


def _rope(v, cos, sin):
    return jnp.concatenate(
        [v[:, j:j + 128] * cos + _swap_halves(v[:, j:j + 128]) * sin for j in range(0, v.shape[1], 128)], axis=1)


def _rope_t(dv, cos, sin):
    return jnp.concatenate(
        [dv[:, j:j + 128] * cos + _swap_halves(dv[:, j:j + 128] * sin) for j in range(0, dv.shape[1], 128)], axis=1)


def _head_stats(qk, g1, g2):
    rinv = lax.rsqrt(_dot_hilo(qk * qk, g1) + EPS)
    return rinv, _dot_hilo(rinv, g2)


def qkv_fwd(x, gain, mod, w, l, b, gqk, cos, sin, g1, g2, tm=256):
    t = x.shape[0]
    tpe = t // tm // 2

    def body(x_ref, gain_ref, mod_ref, w_ref, b_ref, gqk_ref, c_ref, s_ref, g1_ref, g2_ref, raw_ref, q_ref, k_ref):
        sh, sc, _ = _mod_rows(mod_ref, 1)
        _, _, h = _norm_mod(x_ref[...], gain_ref[...], sc, sh)
        qkv = _dot(h.astype(BF16), w_ref[...]) + b_ref[...]
        raw_ref[...] = qkv.astype(BF16)
        qk = qkv[:, :QK_DIM]
        _, rb = _head_stats(qk, g1_ref[...], g2_ref[...])
        qr = _rope(qk * rb * gqk_ref[...], c_ref[...], s_ref[...])
        q_ref[...] = qr[:, :D].astype(BF16)
        k_ref[...] = qr[:, D:].astype(BF16)

    return pl.pallas_call(
        body, name="qkv_fwd", grid=(t // tm,),
        in_specs=[pl.BlockSpec((tm, D), lambda i: (i, 0)), pl.BlockSpec((1, D), lambda i: (0, 0)),
                  pl.BlockSpec((1, 9, D), lambda i: (i // tpe, 0, 0)), _layer(w, l),
                  pl.BlockSpec((1, QKV_DIM), lambda i: (0, 0)), pl.BlockSpec((1, QK_DIM), lambda i: (0, 0)),
                  pl.BlockSpec((tm, 128), lambda i: (i, 0)), pl.BlockSpec((tm, 128), lambda i: (i, 0)),
                  _resident((QK_DIM, 128)), _resident((128, QK_DIM))],
        out_specs=[pl.BlockSpec((tm, QKV_DIM), lambda i: (i, 0)), pl.BlockSpec((tm, D), lambda i: (i, 0)),
                   pl.BlockSpec((tm, N_KV * HEAD_DIM), lambda i: (i, 0))],
        out_shape=[jax.ShapeDtypeStruct((t, QKV_DIM), BF16), jax.ShapeDtypeStruct((t, D), BF16),
                   jax.ShapeDtypeStruct((t, N_KV * HEAD_DIM), BF16)],
        compiler_params=_cparams(40, 1),
    )(x, gain, mod, w, b, gqk, cos, sin, g1, g2)


def qkv_bwd_pre(dq, dk, dv, raw, gqk, cos, sin, g1, g2, gsel, tm=256):
    t = dq.shape[0]

    def body(dq_ref, dk_ref, dv_ref, raw_ref, gqk_ref, c_ref, s_ref, g1_ref, g2_ref, gsel_ref, dz_ref, sa_ref):
        i = pl.program_id(0)
        dqk = jnp.concatenate([dq_ref[...].astype(F32), dk_ref[...]], axis=1)
        dqn = _rope_t(dqk, c_ref[...], s_ref[...])
        qk = raw_ref[:, :QK_DIM].astype(F32)
        g1v, g2v = g1_ref[...], g2_ref[...]
        rinv, rb = _head_stats(qk, g1v, g2v)
        dgq = jnp.broadcast_to(_colsum(dqn * qk * rb), (8, QK_DIM))
        dyh = dqn * gqk_ref[...]
        mh = _dot_hilo(dyh * qk, g1v)
        mb = _dot_hilo(mh * rinv * rinv * rinv, g2v)
        dz_ref[:, :QK_DIM] = (rb * dyh - qk * mb).astype(BF16)
        dz_ref[:, QK_DIM:] = dv_ref[...].astype(BF16)

        @pl.when(i == 0)
        def _():
            sa_ref[...] = jnp.zeros_like(sa_ref)

        sa_ref[...] += _dot_hilo(dgq, gsel_ref[...])

    kvw = N_KV * HEAD_DIM
    return pl.pallas_call(
        body, name="qkv_bwd_pre", grid=(t // tm,),
        in_specs=[pl.BlockSpec((tm, D), lambda i: (i, 0)), pl.BlockSpec((tm, kvw), lambda i: (i, 0)),
                  pl.BlockSpec((tm, kvw), lambda i: (i, 0)), pl.BlockSpec((tm, QKV_DIM), lambda i: (i, 0)),
                  pl.BlockSpec((1, QK_DIM), lambda i: (0, 0)),
                  pl.BlockSpec((tm, 128), lambda i: (i, 0)), pl.BlockSpec((tm, 128), lambda i: (i, 0)),
                  _resident((QK_DIM, 128)), _resident((128, QK_DIM)), _resident((QK_DIM, 128))],
        out_specs=[pl.BlockSpec((tm, QKV_DIM), lambda i: (i, 0)), pl.BlockSpec((8, 128), lambda i: (0, 0))],
        out_shape=[jax.ShapeDtypeStruct((t, QKV_DIM), BF16), jax.ShapeDtypeStruct((8, 128), F32)],
        compiler_params=_cparams(40, 1),
    )(dq, dk, dv, raw, gqk, cos, sin, g1, g2, gsel)


def _band_mask(n):
    row = lax.broadcasted_iota(jnp.int32, (GROUP * BLOCK, 2 * BLOCK), 0) % BLOCK
    col = lax.broadcasted_iota(jnp.int32, (GROUP * BLOCK, 2 * BLOCK), 1)
    rel = row + BLOCK - col
    return (rel >= 0) & (rel < BLOCK) & ((col >= BLOCK) | (n > 0))


def _stack_heads(v, g):
    base = g * GROUP * HEAD_DIM
    return jnp.concatenate([v[:, base + j * HEAD_DIM:base + (j + 1) * HEAD_DIM] for j in range(GROUP)], axis=0)


def _kv_cat(prev, cur, g):
    return jnp.concatenate([prev[:, g * HEAD_DIM:(g + 1) * HEAD_DIM], cur[:, g * HEAD_DIM:(g + 1) * HEAD_DIM]], axis=0)


def _sink_col(sink_ref, g):
    return jnp.concatenate([jnp.full((BLOCK, 1), sink_ref[0, g * GROUP + j], F32) for j in range(GROUP)], axis=0)


def _attn_specs(nb):
    kvw = N_KV * HEAD_DIM
    vcol = QK_DIM // kvw
    cur = lambda e, n: (e * nb + n, 0)
    prev = lambda e, n: (e * nb + jnp.maximum(n - 1, 0), 0)
    return [pl.BlockSpec((BLOCK, D), cur),
            pl.BlockSpec((BLOCK, kvw), cur), pl.BlockSpec((BLOCK, kvw), prev),
            pl.BlockSpec((BLOCK, kvw), lambda e, n: (e * nb + n, vcol)),
            pl.BlockSpec((BLOCK, kvw), lambda e, n: (e * nb + jnp.maximum(n - 1, 0), vcol)),
            pl.BlockSpec(memory_space=pltpu.SMEM)]


def attn_fwd(q, k, raw, sinks):
    t = q.shape[0]
    nb = t // 2 // BLOCK

    def body(q_ref, kc_ref, kp_ref, vc_ref, vp_ref, sink_ref, o_ref, lse_ref):
        n = pl.program_id(1)
        qv = q_ref[...]
        kc, kp, vc, vp = kc_ref[...], kp_ref[...], vc_ref[...], vp_ref[...]
        mask = _band_mask(n)
        outs, lses = [], []
        for g in range(N_KV):
            kk, vv = _kv_cat(kp, kc, g), _kv_cat(vp, vc, g)
            s = jnp.where(mask, _dot_nt(_stack_heads(qv, g), kk) * (HEAD_DIM ** -0.5), -1e30)
            sink = _sink_col(sink_ref, g)
            m = jnp.maximum(jnp.max(s, axis=1, keepdims=True), sink)
            p = jnp.exp(s - m)
            l = jnp.sum(p, axis=1, keepdims=True) + jnp.exp(sink - m)
            o = _dot(p.astype(BF16), vv) / l
            lse = m + jnp.log(l)
            for j in range(GROUP):
                outs.append(o[j * BLOCK:(j + 1) * BLOCK, :])
                lses.append(lse[j * BLOCK:(j + 1) * BLOCK, :])
        o_ref[...] = jnp.concatenate(outs, axis=1).astype(BF16)
        lse_ref[...] = jnp.concatenate(lses, axis=1)

    cur = lambda e, n: (e * nb + n, 0)
    return pl.pallas_call(
        body, name="attn_fwd", grid=(2, nb),
        in_specs=_attn_specs(nb),
        out_specs=[pl.BlockSpec((BLOCK, D), cur), pl.BlockSpec((BLOCK, N_HEADS), cur)],
        out_shape=[jax.ShapeDtypeStruct((t, D), BF16), jax.ShapeDtypeStruct((t, N_HEADS), F32)],
        compiler_params=_cparams(32, 2),
    )(q, k, k, raw, raw, sinks)


def attn_bwd(q, k, raw, sinks, o, do, lse):
    t = q.shape[0]
    s_len = t // 2
    nb = s_len // BLOCK
    kvw = N_KV * HEAD_DIM

    def body(q_ref, kc_ref, kp_ref, vc_ref, vp_ref, sink_ref, o_ref, do_ref, lse_ref, dq_ref, dk_ref, dv_ref, ds_ref):
        n = pl.program_id(1)

        @pl.when(n == 0)
        def _():
            dk_ref[...] = jnp.zeros_like(dk_ref)
            dv_ref[...] = jnp.zeros_like(dv_ref)

        @pl.when((n == 0) & (pl.program_id(0) == 0))
        def _():
            ds_ref[...] = jnp.zeros_like(ds_ref)

        qv, ov, dov, lsev = q_ref[...], o_ref[...], do_ref[...], lse_ref[...]
        kc, kp, vc, vp = kc_ref[...], kp_ref[...], vc_ref[...], vp_ref[...]
        mask = _band_mask(n)
        dqs, dks, dvs, dsk = [], [], [], []
        for g in range(N_KV):
            kk, vv = _kv_cat(kp, kc, g), _kv_cat(vp, vc, g)
            qg, og, dog = _stack_heads(qv, g), _stack_heads(ov, g), _stack_heads(dov, g)
            lse = jnp.concatenate([lsev[:, g * GROUP + j:g * GROUP + j + 1] for j in range(GROUP)], axis=0)
            s = jnp.where(mask, _dot_nt(qg, kk) * (HEAD_DIM ** -0.5), -1e30)
            p = jnp.exp(s - lse)
            dd = jnp.sum(dog.astype(F32) * og.astype(F32), axis=1, keepdims=True)
            ds = (p * (_dot_nt(dog, vv) - dd) * (HEAD_DIM ** -0.5)).astype(BF16)
            dqg = _dot(ds, kk)
            dks.append(_dot_tn(ds, qg))
            dvs.append(_dot_tn(p.astype(BF16), dog))
            wsink = jnp.exp(_sink_col(sink_ref, g) - lse) * dd
            for j in range(GROUP):
                dqs.append(dqg[j * BLOCK:(j + 1) * BLOCK, :])
                dsk.append(wsink[j * BLOCK:(j + 1) * BLOCK, :])
        dq_ref[...] = jnp.concatenate(dqs, axis=1).astype(BF16)
        dkk = jnp.concatenate(dks, axis=1)
        dvv = jnp.concatenate(dvs, axis=1)
        prev0 = pl.multiple_of(jnp.maximum(n - 1, 0) * BLOCK, BLOCK)
        cur0 = pl.multiple_of(n * BLOCK, BLOCK)
        dk_ref[pl.ds(prev0, BLOCK), :] += dkk[:BLOCK]
        dv_ref[pl.ds(prev0, BLOCK), :] += dvv[:BLOCK]
        dk_ref[pl.ds(cur0, BLOCK), :] += dkk[BLOCK:]
        dv_ref[pl.ds(cur0, BLOCK), :] += dvv[BLOCK:]
        ds_ref[0:1, :] -= _colsum(jnp.concatenate(dsk, axis=1))

    cur = lambda e, n: (e * nb + n, 0)
    return pl.pallas_call(
        body, name="attn_bwd", grid=(2, nb),
        in_specs=_attn_specs(nb) + [pl.BlockSpec((BLOCK, D), cur), pl.BlockSpec((BLOCK, D), cur),
                                    pl.BlockSpec((BLOCK, N_HEADS), cur)],
        out_specs=[pl.BlockSpec((BLOCK, D), cur), pl.BlockSpec((s_len, kvw), lambda e, n: (e, 0)),
                   pl.BlockSpec((s_len, kvw), lambda e, n: (e, 0)), pl.BlockSpec((8, N_HEADS), lambda e, n: (0, 0))],
        out_shape=[jax.ShapeDtypeStruct((t, D), BF16), jax.ShapeDtypeStruct((t, kvw), F32),
                   jax.ShapeDtypeStruct((t, kvw), F32), jax.ShapeDtypeStruct((8, N_HEADS), F32)],
        compiler_params=_cparams(32, 2),
    )(q, k, k, raw, raw, sinks, o, do, lse)


CONV_COLS = 256


def _conv_specs(s_len):
    nct = D // CONV_COLS
    return [pl.BlockSpec((s_len, CONV_COLS), lambda j, e: (e, j)),
            pl.BlockSpec((s_len, CONV_COLS), lambda j, e: (e, nct + j)),
            pl.BlockSpec((s_len, CONV_COLS), lambda j, e: (e, 2 * nct + j)),
            pl.BlockSpec((3, CONV_COLS), lambda j, e: (0, j))]


def _conv_taps(gc, v, w, s_len):
    u = gc * v
    row = lax.broadcasted_iota(jnp.int32, u.shape, 0)
    u1 = jnp.where(row >= 1, pltpu.roll(u, 1, 0), 0.0)
    u2 = jnp.where(row >= 2, pltpu.roll(u, 2, 0), 0.0)
    return u, u1, u2, w[2:3, :] * u + w[1:2, :] * u1 + w[0:1, :] * u2


def conv_fwd(z, w):
    t = z.shape[0]
    s_len = t // 2

    def body(gb_ref, gc_ref, v_ref, w_ref, p_ref):
        _, _, _, conv = _conv_taps(gc_ref[...].astype(F32), v_ref[...].astype(F32), w_ref[...], s_len)
        p_ref[...] = (gb_ref[...].astype(F32) * conv).astype(BF16)

    return pl.pallas_call(
        body, name="conv_fwd", grid=(D // CONV_COLS, 2),
        in_specs=_conv_specs(s_len),
        out_specs=pl.BlockSpec((s_len, CONV_COLS), lambda j, e: (e, j)),
        out_shape=jax.ShapeDtypeStruct((t, D), BF16),
        compiler_params=_cparams(40, 2),
    )(z, z, z, w)


def conv_bwd(z, w, dp):
    t = z.shape[0]
    s_len = t // 2

    def body(gb_ref, gc_ref, v_ref, w_ref, dp_ref, dgb_ref, dgc_ref, dv_ref, dw_ref):
        e = pl.program_id(1)
        gc, v, wv = gc_ref[...].astype(F32), v_ref[...].astype(F32), w_ref[...]
        u, u1, u2, conv = _conv_taps(gc, v, wv, s_len)
        dpv = dp_ref[...].astype(F32)
        dgb_ref[...] = (dpv * conv).astype(BF16)
        dc = dpv * gb_ref[...].astype(F32)
        row = lax.broadcasted_iota(jnp.int32, dc.shape, 0)
        dc1 = jnp.where(row <= s_len - 2, pltpu.roll(dc, s_len - 1, 0), 0.0)
        dc2 = jnp.where(row <= s_len - 3, pltpu.roll(dc, s_len - 2, 0), 0.0)
        du = wv[2:3, :] * dc + wv[1:2, :] * dc1 + wv[0:1, :] * dc2
        dgc_ref[...] = (du * v).astype(BF16)
        dv_ref[...] = (du * gc).astype(BF16)

        @pl.when(e == 0)
        def _():
            dw_ref[...] = jnp.zeros_like(dw_ref)

        dw_ref[0:1, :] += _colsum(dc * u2)
        dw_ref[1:2, :] += _colsum(dc * u1)
        dw_ref[2:3, :] += _colsum(dc * u)

    blk = pl.BlockSpec((s_len, CONV_COLS), lambda j, e: (e, j))
    return pl.pallas_call(
        body, name="conv_bwd", grid=(D // CONV_COLS, 2),
        in_specs=_conv_specs(s_len) + [blk],
        out_specs=[blk, blk, blk, pl.BlockSpec((8, CONV_COLS), lambda j, e: (0, j))],
        out_shape=[jax.ShapeDtypeStruct((t, D), BF16)] * 3 + [jax.ShapeDtypeStruct((8, D), F32)],
        compiler_params=_cparams(48, 2),
    )(z, z, z, w, dp)


def loss_grad(y, tgt, tm=512):
    t = y.shape[0]

    def body(y_ref, t_ref, dy_ref, l_ref):
        i = pl.program_id(0)
        d = y_ref[...] - t_ref[...]
        dy_ref[...] = d * (1.0 / D)

        @pl.when(i == 0)
        def _():
            l_ref[...] = jnp.zeros_like(l_ref)

        l_ref[...] += 0.5 / D * jnp.sum(d * d)

    return pl.pallas_call(
        body, name="loss_grad", grid=(t // tm,),
        in_specs=[pl.BlockSpec((tm, D), lambda i: (i, 0))] * 2,
        out_specs=[pl.BlockSpec((tm, D), lambda i: (i, 0)), pl.BlockSpec((8, 128), lambda i: (0, 0))],
        out_shape=[jax.ShapeDtypeStruct((t, D), F32), jax.ShapeDtypeStruct((8, 128), F32)],
        compiler_params=_cparams(32, 1),
    )(y, tgt)


ADA_COLS = 384


def ada_fwd(c_all, w):
    nl, _, n = w.shape
    nex = c_all.shape[0]

    def body(c_ref, w_ref, o_ref):
        cv = c_ref[...]
        ca = (cv * jax.nn.sigmoid(cv)).astype(BF16)
        o_ref[0] = _dot(ca, w_ref[0].astype(BF16))

    return pl.pallas_call(
        body, name="ada_fwd", grid=(nl, n // ADA_COLS),
        in_specs=[pl.BlockSpec((nex, D), lambda l, j: (0, 0)), pl.BlockSpec((1, D, ADA_COLS), lambda l, j: (l, 0, j))],
        out_specs=pl.BlockSpec((1, nex, ADA_COLS), lambda l, j: (l, 0, j)),
        out_shape=jax.ShapeDtypeStruct((nl, nex, n), F32),
        compiler_params=_cparams(32, 2),
    )(c_all, w)


def _adam_math(w, g, m, v):
    m = ADAM_B1 * m + (1.0 - ADAM_B1) * g
    v = ADAM_B2 * v + (1.0 - ADAM_B2) * (g * g)
    m_hat = m / (1.0 - ADAM_B1 ** ADAM_STEP)
    v_hat = v / (1.0 - ADAM_B2 ** ADAM_STEP)
    return -ADAM_LR * (m_hat / (jnp.sqrt(v_hat) + ADAM_EPS) + ADAM_WD * w), m, v


def ada_bwd_adam(c_all, dm, w, m, v):
    nl, _, n = w.shape
    nex = c_all.shape[0]

    def body(c_ref, dm_ref, w_ref, m_ref, v_ref, g_ref, d_ref, mo_ref, vo_ref):
        cv = c_ref[...]
        ca = (cv * jax.nn.sigmoid(cv)).astype(BF16)
        g = _dot_tn(ca, dm_ref[0].astype(BF16))
        g_ref[0] = g
        d_ref[0], mo_ref[0], vo_ref[0] = _adam_math(w_ref[0], g, m_ref[0], v_ref[0])

    wspec = pl.BlockSpec((1, D, ADA_COLS), lambda l, j: (l, 0, j))
    return pl.pallas_call(
        body, name="ada_bwd_adam", grid=(nl, n // ADA_COLS),
        in_specs=[pl.BlockSpec((nex, D), lambda l, j: (0, 0)), pl.BlockSpec((1, nex, ADA_COLS), lambda l, j: (l, 0, j)),
                  wspec, wspec, wspec],
        out_specs=[wspec] * 4,
        out_shape=[jax.ShapeDtypeStruct(w.shape, F32)] * 4,
        compiler_params=_cparams(40, 2),
    )(c_all, dm, w, m, v)


def adamw(w, g, m, v):
    shape = w.shape
    cols = shape[-1]
    rows = w.size // cols
    args = [a.reshape(rows, cols) for a in (w, g, m, v)]
    tr = rows
    while tr * cols * 4 > (1 << 20) and tr % 16 == 0:
        tr //= 2

    def body(w_ref, g_ref, m_ref, v_ref, d_ref, mo_ref, vo_ref):
        d_ref[...], mo_ref[...], vo_ref[...] = _adam_math(w_ref[...], g_ref[...], m_ref[...], v_ref[...])

    spec = pl.BlockSpec((tr, cols), lambda i: (i, 0))
    outs = pl.pallas_call(
        body, name="adamw", grid=(rows // tr,),
        in_specs=[spec] * 4, out_specs=[spec] * 3,
        out_shape=[jax.ShapeDtypeStruct((rows, cols), F32)] * 3,
        compiler_params=_cparams(32, 1),
    )(*args)
    return [o.reshape(shape) for o in outs]


def cast_bf16(w):
    nl, r, c = w.shape
    tr = r
    while tr * c * 4 > (2 << 20) and tr % 32 == 0:
        tr //= 2

    def body(w_ref, o_ref):
        o_ref[...] = w_ref[...].astype(BF16)

    spec = pl.BlockSpec((1, tr, c), lambda l, i: (l, i, 0))
    return pl.pallas_call(
        body, name="cast_bf16", grid=(nl, r // tr), in_specs=[spec], out_specs=spec,
        out_shape=jax.ShapeDtypeStruct(w.shape, BF16), compiler_params=_cparams(32, 2),
    )(w)


def add_bias(a, b):
    def body(a_ref, b_ref, o_ref):
        o_ref[...] = a_ref[...] + b_ref[...]

    return pl.pallas_call(body, name="add_bias", out_shape=jax.ShapeDtypeStruct(a.shape, F32))(a, b)


N_DMOD_ROWS = 40


def reduce_small(p_all):
    rows = p_all.shape[1]

    def body(p_ref, red_ref, ex_ref):
        acc = p_ref[0]
        for d in range(1, 8):
            acc = acc + p_ref[d]
        red_ref[...] = acc
        ex_ref[...] = acc[:N_DMOD_ROWS] + acc[N_DMOD_ROWS:2 * N_DMOD_ROWS]

    return pl.pallas_call(
        body, name="reduce_small",
        out_shape=[jax.ShapeDtypeStruct((rows, D), F32), jax.ShapeDtypeStruct((N_DMOD_ROWS, D), F32)],
        compiler_params=_cparams(32, 0),
    )(p_all)


def _place():
    return lax.axis_index("x"), lax.axis_index("y"), lax.axis_index("c")


def _other_chips(x, y):
    return [(1 - x, y), (x, 1 - y), (1 - x, 1 - y)]


def _sl(ref, axis, start, size):
    idx = [slice(None)] * len(ref.shape)
    idx[axis] = pl.ds(pl.multiple_of(start, 16), size)
    return ref.at[tuple(idx)]


def _rcopy(src, dst, send_sem, recv_sem, to):
    return pltpu.make_async_remote_copy(src_ref=src, dst_ref=dst, send_sem=send_sem, recv_sem=recv_sem,
                                        device_id=to, device_id_type=MESH)


def allgather_small(v, all_devices):
    rows, cols = v.shape
    flips = [(dx, dy, dc) for dx in (0, 1) for dy in (0, 1) for dc in (0, 1)
             if (dx, dy, dc) != (0, 0, 0) and (all_devices or dc == 0)]
    n_out = 8 if all_devices else 4

    def body(v_ref, o_ref, send_sems, recv_sems):
        x, y, c = _place()

        def slot(px, py, pc):
            return 4 * px + 2 * py + pc if all_devices else 2 * px + py

        peers = [(1 - x if dx else x, 1 - y if dy else y, 1 - c if dc else c) for dx, dy, dc in flips]
        sends = [_rcopy(v_ref, o_ref.at[slot(x, y, c)], send_sems.at[r], recv_sems.at[r], peer)
                 for r, peer in enumerate(peers)]
        for cp in sends:
            cp.start()
        o_ref[slot(x, y, c)] = v_ref[...]
        for r, peer in enumerate(peers):
            _rcopy(v_ref, o_ref.at[slot(*peer)], send_sems.at[r], recv_sems.at[r], peer).wait_recv()
        for cp in sends:
            cp.wait_send()

    vm = pl.BlockSpec(memory_space=pltpu.VMEM)
    return pl.pallas_call(
        body, name="allgather_small_all" if all_devices else "allgather_small_chips",
        in_specs=[vm], out_specs=vm,
        out_shape=jax.ShapeDtypeStruct((n_out, rows, cols), v.dtype),
        scratch_shapes=[pltpu.SemaphoreType.DMA((len(flips),)), pltpu.SemaphoreType.DMA((len(flips),))],
        compiler_params=pltpu.CompilerParams(vmem_limit_bytes=32 * 1024 * 1024),
    )(v)


def gather_weights(shards, axes):
    na = len(shards)
    fulls = [tuple(d * N_CHIPS if i == ax else d for i, d in enumerate(s.shape)) for s, ax in zip(shards, axes)]

    def body(*refs):
        ins, outs = refs[:na], refs[na:2 * na]
        ici_s, ici_r, d2d_s, d2d_r, loc = refs[2 * na:]
        x, y, c = _place()
        chips = _other_chips(x, y)

        def win(a, ref, chip):
            n = ins[a].shape[axes[a]]
            return _sl(ref, axes[a], (2 * chip[0] + chip[1]) * n, n)

        def half(a, ref, cc):
            ha = 3 - axes[a]
            hs = ins[a].shape[ha] // 2
            return _sl(ref, ha, cc * hs, hs)

        locs, sends, passed = [], [], []
        for a in range(na):
            mine = pltpu.make_async_copy(ins[a], win(a, outs[a], (x, y)), loc.at[a])
            mine.start()
            locs.append(mine)
            for j, chip in enumerate(chips):
                cp = _rcopy(half(a, ins[a], c), half(a, win(a, outs[a], (x, y)), c),
                            ici_s.at[3 * a + j], ici_r.at[3 * a + j], (*chip, c))
                cp.start()
                sends.append(cp)
        for a in range(na):
            for j, chip in enumerate(chips):
                got = half(a, win(a, outs[a], chip), c)
                _rcopy(got, got, ici_s.at[3 * a + j], ici_r.at[3 * a + j], (*chip, c)).wait_recv()
                cp = _rcopy(got, got, d2d_s.at[3 * a + j], d2d_r.at[3 * a + j], (x, y, 1 - c))
                cp.start()
                passed.append(cp)
        for a in range(na):
            for j, chip in enumerate(chips):
                got = half(a, win(a, outs[a], chip), 1 - c)
                _rcopy(got, got, d2d_s.at[3 * a + j], d2d_r.at[3 * a + j], (x, y, 1 - c)).wait_recv()
        for cp in sends + passed:
            cp.wait_send()
        for cp in locs:
            cp.wait()

    dma = pltpu.SemaphoreType.DMA
    return pl.pallas_call(
        body, name="gather_weights",
        in_specs=[ANY] * na, out_specs=[ANY] * na,
        out_shape=[jax.ShapeDtypeStruct(f, BF16) for f in fulls],
        scratch_shapes=[dma((3 * na,)), dma((3 * na,)), dma((3 * na,)), dma((3 * na,)), dma((na,))],
    )(*shards)


def exchange_halves(grads, axes):
    na = len(grads)

    def hshape(g, ax):
        ha = 3 - ax
        return tuple(d // 2 if i == ha else d for i, d in enumerate(g.shape))

    def body(*refs):
        ins, outs = refs[:na], refs[na:2 * na]
        send_sems, recv_sems = refs[2 * na:]
        x, y, c = _place()
        cps = []
        for a in range(na):
            ha = 3 - axes[a]
            hs = ins[a].shape[ha] // 2
            cp = _rcopy(_sl(ins[a], ha, (1 - c) * hs, hs), outs[a], send_sems.at[a], recv_sems.at[a], (x, y, 1 - c))
            cp.start()
            cps.append(cp)
        for cp in cps:
            cp.wait_recv()
        for cp in cps:
            cp.wait_send()

    dma = pltpu.SemaphoreType.DMA
    return pl.pallas_call(
        body, name="exchange_halves",
        in_specs=[ANY] * na, out_specs=[ANY] * na,
        out_shape=[jax.ShapeDtypeStruct(hshape(g, ax), BF16) for g, ax in zip(grads, axes)],
        scratch_shapes=[dma((na,)), dma((na,))],
    )(*grads)


def scatter_windows(halves, axes):
    na = len(halves)

    def pshape(h, ax):
        return (N_CHIPS,) + tuple(d // N_CHIPS if i == ax else d for i, d in enumerate(h.shape))

    def body(*refs):
        ins, outs = refs[:na], refs[na:2 * na]
        send_sems, recv_sems, loc = refs[2 * na:]
        x, y, c = _place()
        k = 2 * x + y
        chips = _other_chips(x, y)

        def win(a, chip):
            n = ins[a].shape[axes[a]] // N_CHIPS
            return _sl(ins[a], axes[a], (2 * chip[0] + chip[1]) * n, n)

        cps, locs = [], []
        for a in range(na):
            mine = pltpu.make_async_copy(win(a, (x, y)), outs[a].at[k], loc.at[a])
            mine.start()
            locs.append(mine)
            for j, chip in enumerate(chips):
                cp = _rcopy(win(a, chip), outs[a].at[k], send_sems.at[3 * a + j], recv_sems.at[3 * a + j], (*chip, c))
                cp.start()
                cps.append(cp)
        for a in range(na):
            for j, chip in enumerate(chips):
                slot = outs[a].at[2 * chip[0] + chip[1]]
                _rcopy(slot, slot, send_sems.at[3 * a + j], recv_sems.at[3 * a + j], (*chip, c)).wait_recv()
        for cp in cps:
            cp.wait_send()
        for cp in locs:
            cp.wait()

    dma = pltpu.SemaphoreType.DMA
    return pl.pallas_call(
        body, name="scatter_windows",
        in_specs=[ANY] * na, out_specs=[ANY] * na,
        out_shape=[jax.ShapeDtypeStruct(pshape(h, ax), BF16) for h, ax in zip(halves, axes)],
        scratch_shapes=[dma((3 * na,)), dma((3 * na,)), dma((na,))],
    )(*halves)


def join_halves(parts, axes):
    na = len(parts)

    def fshape(p, ax):
        ha = 3 - ax
        return tuple(d * 2 if i == ha else d for i, d in enumerate(p.shape))

    def body(*refs):
        ins, outs = refs[:na], refs[na:2 * na]
        send_sems, recv_sems, loc = refs[2 * na:]
        x, y, c = _place()
        cps, locs = [], []
        for a in range(na):
            ha = 3 - axes[a]
            hs = ins[a].shape[ha]
            mine = pltpu.make_async_copy(ins[a], _sl(outs[a], ha, c * hs, hs), loc.at[a])
            mine.start()
            locs.append(mine)
            cp = _rcopy(ins[a], _sl(outs[a], ha, c * hs, hs), send_sems.at[a], recv_sems.at[a], (x, y, 1 - c))
            cp.start()
            cps.append(cp)
        for a in range(na):
            ha = 3 - axes[a]
            hs = ins[a].shape[ha]
            theirs = _sl(outs[a], ha, (1 - c) * hs, hs)
            _rcopy(ins[a], theirs, send_sems.at[a], recv_sems.at[a], (x, y, 1 - c)).wait_recv()
        for cp in cps:
            cp.wait_send()
        for cp in locs:
            cp.wait()

    dma = pltpu.SemaphoreType.DMA
    return pl.pallas_call(
        body, name="join_halves",
        in_specs=[ANY] * na, out_specs=[ANY] * na,
        out_shape=[jax.ShapeDtypeStruct(fshape(p, ax), F32) for p, ax in zip(parts, axes)],
        scratch_shapes=[dma((na,)), dma((na,)), dma((na,))],
    )(*parts)


def _tile2(r, c, itemsize, limit):
    bc = c
    while bc > 1536:
        bc //= 2
    assert c % bc == 0 and bc % 128 == 0
    br = r
    while br * bc * itemsize > limit and br % 32 == 0:
        br //= 2
    assert r % br == 0 and br % 16 == 0
    return br, bc


def add_my_half(g, theirs, ax, cc):
    ha = 3 - ax
    nl, r, c = theirs.shape
    br, bc = _tile2(r, c, 2, 1 << 20)
    nrb, ncb = r // br, c // bc

    def body(cc_ref, g_ref, t_ref, o_ref):
        o_ref[...] = (g_ref[...].astype(F32) + t_ref[...].astype(F32)).astype(BF16)

    def gmap(l, i, j, cc_ref):
        return (l, cc_ref[0] * nrb + i, j) if ha == 1 else (l, i, cc_ref[0] * ncb + j)

    blk = pl.BlockSpec((1, br, bc), lambda l, i, j, cc_ref: (l, i, j))
    return pl.pallas_call(
        body, name="add_my_half",
        grid_spec=pltpu.PrefetchScalarGridSpec(
            num_scalar_prefetch=1, grid=(nl, nrb, ncb),
            in_specs=[pl.BlockSpec((1, br, bc), gmap), blk], out_specs=blk),
        out_shape=jax.ShapeDtypeStruct(theirs.shape, BF16),
        compiler_params=_cparams(32, 3),
    )(cc, g, theirs)


def sum_chips(parts):
    _, nl, r, c = parts.shape
    br, bc = _tile2(r, c, 2, 1 << 19)

    def body(p_ref, o_ref):
        acc = p_ref[0].astype(F32)
        for q in range(1, N_CHIPS):
            acc = acc + p_ref[q].astype(F32)
        o_ref[...] = acc

    return pl.pallas_call(
        body, name="sum_chips", grid=(nl, r // br, c // bc),
        in_specs=[pl.BlockSpec((N_CHIPS, 1, br, bc), lambda l, i, j: (0, l, i, j))],
        out_specs=pl.BlockSpec((1, br, bc), lambda l, i, j: (l, i, j)),
        out_shape=jax.ShapeDtypeStruct((nl, r, c), F32),
        compiler_params=_cparams(32, 3),
    )(parts)


BIG = ("w_ffn_up", "w_ffn_down", "attn_w_qkv", "attn_w_o", "conv_w_in", "conv_w_out")
BIG_AXIS = {"w_ffn_up": 2, "w_ffn_down": 1, "attn_w_qkv": 2, "attn_w_o": 1, "conv_w_in": 2, "conv_w_out": 1}
WEIGHTS = ("norm_gain", "w_ada", "b_ada", "w_ffn_up", "w_ffn_down", "attn_w_qkv", "attn_b_qkv", "attn_q_gain",
           "attn_k_gain", "attn_sinks", "attn_w_o", "attn_b_o", "conv_w_in", "conv_w", "conv_w_out")
N_SMALL_ROWS = 112


def _stack3(a):
    return a.reshape((-1,) + a.shape[-2:])


def _head_matrices():
    lane = jnp.arange(QK_DIM)
    head = lane // HEAD_DIM
    col = jnp.arange(128)
    g1 = jnp.where(head[:, None] == col[None, :], 1.0 / HEAD_DIM, 0.0).astype(BF16)
    g2 = jnp.where(col[:, None] == head[None, :], 1.0, 0.0).astype(BF16)
    fold = lane % HEAD_DIM + jnp.where(head >= N_HEADS, HEAD_DIM, 0)
    gsel = jnp.where(fold[:, None] == col[None, :], 1.0, 0.0).astype(BF16)
    return g1, g2, gsel


def _pad_cols(a, n):
    return jnp.pad(a, ((0, 0), (0, n - a.shape[1])))


def kernel(x, c, positions, norm_gain, w_ada, b_ada, w_ffn_up, w_ffn_down, attn_w_qkv, attn_b_qkv, attn_q_gain, attn_k_gain, attn_sinks, attn_w_o, attn_b_o, conv_w_in, conv_w, conv_w_out, loss_target, m_norm_gain, m_w_ada, m_b_ada, m_w_ffn_up, m_w_ffn_down, m_attn_w_qkv, m_attn_b_qkv, m_attn_q_gain, m_attn_k_gain, m_attn_sinks, m_attn_w_o, m_attn_b_o, m_conv_w_in, m_conv_w, m_conv_w_out, v_norm_gain, v_w_ada, v_b_ada, v_w_ffn_up, v_w_ffn_down, v_attn_w_qkv, v_attn_b_qkv, v_attn_q_gain, v_attn_k_gain, v_attn_sinks, v_attn_w_o, v_attn_b_o, v_conv_w_in, v_conv_w, v_conv_w_out):
    w = dict(norm_gain=norm_gain, w_ada=w_ada, b_ada=b_ada, w_ffn_up=w_ffn_up, w_ffn_down=w_ffn_down,
             attn_w_qkv=attn_w_qkv, attn_b_qkv=attn_b_qkv, attn_q_gain=attn_q_gain, attn_k_gain=attn_k_gain,
             attn_sinks=attn_sinks, attn_w_o=attn_w_o, attn_b_o=attn_b_o, conv_w_in=conv_w_in, conv_w=conv_w,
             conv_w_out=conv_w_out)
    mom = dict(norm_gain=m_norm_gain, w_ada=m_w_ada, b_ada=m_b_ada, w_ffn_up=m_w_ffn_up, w_ffn_down=m_w_ffn_down,
               attn_w_qkv=m_attn_w_qkv, attn_b_qkv=m_attn_b_qkv, attn_q_gain=m_attn_q_gain,
               attn_k_gain=m_attn_k_gain, attn_sinks=m_attn_sinks, attn_w_o=m_attn_w_o, attn_b_o=m_attn_b_o,
               conv_w_in=m_conv_w_in, conv_w=m_conv_w, conv_w_out=m_conv_w_out)
    var = dict(norm_gain=v_norm_gain, w_ada=v_w_ada, b_ada=v_b_ada, w_ffn_up=v_w_ffn_up, w_ffn_down=v_w_ffn_down,
               attn_w_qkv=v_attn_w_qkv, attn_b_qkv=v_attn_b_qkv, attn_q_gain=v_attn_q_gain,
               attn_k_gain=v_attn_k_gain, attn_sinks=v_attn_sinks, attn_w_o=v_attn_w_o, attn_b_o=v_attn_b_o,
               conv_w_in=v_conv_w_in, conv_w=v_conv_w, conv_w_out=v_conv_w_out)

    xi, yi, ci = _place()
    chip = 2 * xi + yi
    dev = 4 * xi + 2 * yi + ci
    nex, s_len, _ = x.shape
    t = nex * s_len
    n_attn, n_conv = attn_w_qkv.shape[0], conv_w_in.shape[0]
    axes = [BIG_AXIS[n] for n in BIG]

    c_all = allgather_small(jnp.pad(c, ((0, 8 - nex), (0, 0))), True)[:, :nex].reshape(8 * nex, D)
    ada_cols = w_ada.shape[2]
    modp = ada_fwd(c_all, w_ada)
    modg = allgather_small(modp.reshape(DEPTH * 8 * nex, ada_cols), False)
    modg = lax.dynamic_slice_in_dim(modg.reshape(N_CHIPS, DEPTH, 8 * nex, ada_cols), dev * nex, nex, axis=2)
    modg = modg.transpose(1, 2, 0, 3).reshape(DEPTH, nex, 9 * D)
    mod = add_bias(modg, b_ada.reshape(DEPTH, 1, 9 * D)).reshape(DEPTH, nex, 9, D)

    small = jnp.concatenate([norm_gain.reshape(DEPTH * 3, -1), conv_w.reshape(n_conv * 3, -1)], axis=0)
    small = jnp.pad(small, ((0, -small.shape[0] % 8), (0, 0)))
    small = allgather_small(small, False).transpose(1, 0, 2).reshape(small.shape[0], D)
    gain_full = small[:DEPTH * 3].reshape(DEPTH, 3, D)
    convw_full = small[DEPTH * 3:DEPTH * 3 + n_conv * 3].reshape(n_conv, 3, D)

    wup, wdn, wqkv, wo, win, wout = gather_weights([cast_bf16(_stack3(w[n])) for n in BIG], axes)

    invf = ROPE_THETA ** (-jnp.arange(0, HEAD_DIM, 2, dtype=F32) / HEAD_DIM)
    cos, sin = rope_tables(positions.reshape(t, 1), jnp.tile(invf, 4).reshape(1, 128))
    g1, g2, gsel = _head_matrices()
    gqk = [jnp.concatenate([jnp.tile(attn_q_gain[j], N_HEADS), jnp.tile(attn_k_gain[j], N_KV)]).reshape(1, QK_DIM)
           for j in range(n_attn)]
    zero_bias = jnp.zeros((1, D), F32)

    xs = x.reshape(t, D)
    saved = []
    for i in range(DEPTH):
        j = i // 2
        gn, md = gain_full[i], mod[i]
        x0 = xs
        xs, u1, f1 = ffn_fwd(x0, gn[0:1], md, wup, wdn, 2 * i, 0)
        x1 = xs
        if i % 2 == 0:
            raw, qr, kr = qkv_fwd(x1, gn[1:2], md, wqkv, j, attn_b_qkv[j:j + 1], gqk[j], cos, sin, g1, g2)
            o, lse = attn_fwd(qr, kr, raw, attn_sinks[j:j + 1])
            xs, ymix = proj_res(x1, o, wo, j, attn_b_o[j:j + 1], md)
            mix = (raw, qr, kr, o, lse)
        else:
            z = lin_fwd(x1, gn[1:2], md, win, j, 1)
            p = conv_fwd(z, convw_full[j])
            xs, ymix = proj_res(x1, p, wout, j, zero_bias, md)
            mix = (z, p)
        x2 = xs
        xs, u3, f3 = ffn_fwd(x2, gn[2:3], md, wup, wdn, 2 * i + 1, 2)
        saved.append((x0, u1, f1, x1, mix, ymix, x2, u3, f3))
    dy, lpart = loss_grad(xs, loss_target.reshape(t, D))

    gup = lax.empty(wup.shape, BF16)
    gdn = lax.empty(wdn.shape, BF16)
    gqkv = lax.empty(wqkv.shape, BF16)
    go = lax.empty(wo.shape, BF16)
    gin = lax.empty(win.shape, BF16)
    gout = lax.empty(wout.shape, BF16)
    dmod = [None] * DEPTH
    dgain = [None] * DEPTH
    db_qkv, dqk_gain, dsinks, db_o, dconv_w = ([None] * n_attn, [None] * n_attn, [None] * n_attn,
                                                [None] * n_attn, [None] * n_conv)
    for i in reversed(range(DEPTH)):
        j = i // 2
        gn, md = gain_full[i], mod[i]
        x0, u1, f1, x1, mix, ymix, x2, u3, f3 = saved[i]
        du, a, df, sg3 = ffn_bwd_act(dy, u3, f3, md, wdn, 2 * i + 1, 2)
        dy, h, se3, sa3 = lin_bwd(dy, du, wup, 2 * i + 1, x2, gn[2:3], md, 2, False)
        gup = wgrad(gup, 2 * i + 1, h, du, D, FF_CHUNK)
        gdn = wgrad(gdn, 2 * i + 1, a, df, FF_CHUNK, D)
        if i % 2 == 0:
            raw, qr, kr, o, lse = mix
            dyy, do, sp, sb = proj_res_bwd(dy, ymix, wo, j, md)
            go = wgrad(go, j, o, dyy, D, D)
            dq, dk, dv, dsinks[j] = attn_bwd(qr, kr, raw, attn_sinks[j:j + 1], o, do, lse)
            dz, dqk_gain[j] = qkv_bwd_pre(dq, dk, dv, raw, gqk[j], cos, sin, g1, g2, gsel)
            dy, h, se2, sa2, db_qkv[j] = lin_bwd(dy, dz, wqkv, j, x1, gn[1:2], md, 1, True)
            gqkv = wgrad(gqkv, j, h, dz, D, QKV_DIM)
            db_o[j] = sb
        else:
            z, p = mix
            dyy, dp, sp, _ = proj_res_bwd(dy, ymix, wout, j, md)
            gout = wgrad(gout, j, p, dyy, D, D)
            dgb, dgc, dvv, dconv_w[j] = conv_bwd(z, convw_full[j], dp)
            dz = jnp.concatenate([dgb, dgc, dvv], axis=1)
            dy, h, se2, sa2 = lin_bwd(dy, dz, win, j, x1, gn[1:2], md, 1, False)
            gin = wgrad(gin, j, h, dz, D, 1536)
        du, a, df, sg1 = ffn_bwd_act(dy, u1, f1, md, wdn, 2 * i, 0)
        dy, h, se1, sa1 = lin_bwd(dy, du, wup, 2 * i, x0, gn[0:1], md, 0, False)
        gup = wgrad(gup, 2 * i, h, du, D, FF_CHUNK)
        gdn = wgrad(gdn, 2 * i, a, df, FF_CHUNK, D)
        dmod[i] = jnp.stack([se1[:, 0], se1[:, 1], sg1[:, 0], se2[:, 0], se2[:, 1], sp[:, 0],
                             se3[:, 0], se3[:, 1], sg3[:, 0]], axis=1)
        dgain[i] = jnp.stack([sa1[0], sa2[0], sa3[0]], axis=0)
    grad_x = dy.reshape(x.shape)

    dmod_ex = jnp.stack(dmod, axis=1).reshape(nex, DEPTH * 9, D)
    dmod_ex = jnp.pad(dmod_ex, ((0, 0), (0, N_DMOD_ROWS - DEPTH * 9), (0, 0))).reshape(nex * N_DMOD_ROWS, D)
    misc = jnp.concatenate([dqk_gain[jj][0] for jj in range(n_attn)]
                           + [jnp.pad(dsinks[jj][0], (0, 128 - N_HEADS)) for jj in range(n_attn)]
                           + [lpart[0]])
    rows = [dmod_ex,
            jnp.concatenate(dgain, axis=0), jnp.zeros((4, D), F32),
            jnp.concatenate([_pad_cols(db_qkv[jj][0:1], 2 * D).reshape(2, D) for jj in range(n_attn)], axis=0),
            jnp.concatenate([db_o[jj][0:1] for jj in range(n_attn)], axis=0),
            jnp.concatenate([dconv_w[jj][0:3] for jj in range(n_conv)], axis=0),
            jnp.pad(misc, (0, D - misc.shape[0])).reshape(1, D)]
    packed = jnp.concatenate(rows, axis=0)
    packed = jnp.pad(packed, ((0, N_SMALL_ROWS - packed.shape[0]), (0, 0)))
    p_all = allgather_small(packed, True)
    red, exsum = reduce_small(p_all)

    r0 = nex * N_DMOD_ROWS
    grads = {}
    grads["b_ada"] = exsum[:DEPTH * 9].reshape(DEPTH, 9 * D)
    grads["norm_gain"] = lax.dynamic_slice_in_dim(red[r0:r0 + 12].reshape(DEPTH, 3, D), chip * (D // N_CHIPS),
                                                  D // N_CHIPS, axis=2)
    r1 = r0 + 16
    grads["attn_b_qkv"] = red[r1:r1 + 2 * n_attn].reshape(n_attn, 2 * D)[:, :QKV_DIM]
    r2 = r1 + 2 * n_attn
    grads["attn_b_o"] = red[r2:r2 + n_attn]
    r3 = r2 + n_attn
    grads["conv_w"] = lax.dynamic_slice_in_dim(red[r3:r3 + 3 * n_conv].reshape(n_conv, 3, D), chip * (D // N_CHIPS),
                                               D // N_CHIPS, axis=2)
    mrow = red[r3 + 3 * n_conv]
    grads["attn_q_gain"] = jnp.stack([mrow[128 * jj:128 * jj + HEAD_DIM] for jj in range(n_attn)])
    grads["attn_k_gain"] = jnp.stack([mrow[128 * jj + HEAD_DIM:128 * jj + 128] for jj in range(n_attn)])
    grads["attn_sinks"] = jnp.stack([mrow[128 * (n_attn + jj):128 * (n_attn + jj) + N_HEADS] for jj in range(n_attn)])
    loss = mrow[128 * 2 * n_attn]

    dm_all = p_all[:, :r0].reshape(8, nex, N_DMOD_ROWS, D)[:, :, :DEPTH * 9].reshape(8 * nex, DEPTH, 9 * D)
    dm_mine = lax.dynamic_slice_in_dim(dm_all.transpose(1, 0, 2), chip * ada_cols, ada_cols, axis=2)
    g_ada, d_ada, nm_ada, nv_ada = ada_bwd_adam(c_all, dm_mine, w_ada, m_w_ada, v_w_ada)

    gfull = [gup, gdn, gqkv, go, gin, gout]
    theirs = exchange_halves(gfull, axes)
    cc = ci.reshape(1).astype(jnp.int32)
    pair = [add_my_half(g, th, ax, cc) for g, th, ax in zip(gfull, theirs, axes)]
    parts = scatter_windows(pair, axes)
    reduced = join_halves([sum_chips(p) for p in parts], axes)
    for n, g in zip(BIG, reduced):
        grads[n] = g.reshape(w[n].shape)

    delta, new_m, new_v = {}, {}, {}
    for n in WEIGHTS:
        if n == "w_ada":
            grads[n], delta[n], new_m[n], new_v[n] = g_ada, d_ada, nm_ada, nv_ada
        else:
            delta[n], new_m[n], new_v[n] = adamw(w[n], grads[n], mom[n], var[n])

    return (loss, grad_x, *[grads[n] for n in WEIGHTS], *[delta[n] for n in WEIGHTS],
            *[new_m[n] for n in WEIGHTS], *[new_v[n] for n in WEIGHTS])
```

```python
import functools

import jax
import jax.numpy as jnp
from jax import lax
from jax.experimental import pallas as pl
from jax.experimental.pallas import tpu as pltpu

F32 = jnp.float32
BF16 = jnp.bfloat16

D = 1024
D_FF = 2816
N_HEADS = 16
N_KV = 4
HEAD_DIM = 64
GROUP = N_HEADS // N_KV
QK_DIM = (N_HEADS + N_KV) * HEAD_DIM
QKV_DIM = QK_DIM + N_KV * HEAD_DIM
BLOCK = 128
ROPE_THETA = 10000.0
EPS = 1e-6
DEPTH = 4
N_CHIPS = 4

ADAM_LR = 0.001
ADAM_B1 = 0.9
ADAM_B2 = 0.999
ADAM_EPS = 1e-08
ADAM_WD = 0.01
ADAM_STEP = 10

V7X_VMEM_BYTES = 64 * 1024 * 1024
FF_CHUNK = 1408
MESH = pl.DeviceIdType.MESH
ANY = pl.BlockSpec(memory_space=pl.ANY)


def _cparams(vmem_mb, n_grid):
    assert vmem_mb * 1024 * 1024 <= V7X_VMEM_BYTES
    return pltpu.CompilerParams(vmem_limit_bytes=vmem_mb * 1024 * 1024,
                                dimension_semantics=("arbitrary",) * n_grid)


def _resident(shape):
    nd = len(shape)
    return pl.BlockSpec(shape, lambda *_: (0,) * nd, pipeline_mode=pl.Buffered(1))


def _layer(w, l):
    return pl.BlockSpec((None,) + w.shape[1:], lambda *_: (l, 0, 0), pipeline_mode=pl.Buffered(1))


def _dot(a, b):
    return jnp.dot(a, b, preferred_element_type=F32)


def _dot_nt(a, b):
    return lax.dot_general(a, b, (((1,), (1,)), ((), ())), preferred_element_type=F32)


def _dot_tn(a, b):
    return lax.dot_general(a, b, (((0,), (0,)), ((), ())), preferred_element_type=F32)


def _dot_hilo(a, g):
    hi = a.astype(BF16)
    lo = (a - hi.astype(F32)).astype(BF16)
    return _dot(hi, g) + _dot(lo, g)


def _colsum(a):
    return jnp.sum(a, axis=0, keepdims=True)


def _norm_mod(x, gain, sc, sh):
    r = lax.rsqrt(jnp.mean(x * x, axis=-1, keepdims=True) + EPS)
    n = x * r * gain
    return r, n, n * (1.0 + sc) + sh


def _mod_rows(mod_ref, s):
    return (mod_ref[0, 3 * s:3 * s + 1, :], mod_ref[0, 3 * s + 1:3 * s + 2, :], mod_ref[0, 3 * s + 2:3 * s + 3, :])


def ffn_fwd(x, gain, mod, wup, wdn, l, s, tm=256):
    t = x.shape[0]
    tpe = t // tm // 2

    def body(x_ref, gain_ref, mod_ref, wup_ref, wdn_ref, xo_ref, u_ref, f_ref):
        xv = x_ref[...]
        sh, sc, g = _mod_rows(mod_ref, s)
        _, _, h = _norm_mod(xv, gain_ref[...], sc, sh)
        hb = h.astype(BF16)
        acc = jnp.zeros((tm, D), F32)
        for j in range(D_FF // FF_CHUNK):
            lo, hi = j * FF_CHUNK, (j + 1) * FF_CHUNK
            gate = _dot(hb, wup_ref[:, lo:hi])
            up = _dot(hb, wup_ref[:, D_FF + lo:D_FF + hi])
            u_ref[:, lo:hi] = gate.astype(BF16)
            u_ref[:, D_FF + lo:D_FF + hi] = up.astype(BF16)
            a = (gate * jax.nn.sigmoid(gate) * up).astype(BF16)
            acc = acc + _dot(a, wdn_ref[lo:hi, :])
        f_ref[...] = acc.astype(BF16)
        xo_ref[...] = xv + 0.5 * g * acc

    return pl.pallas_call(
        body, name="ffn_fwd", grid=(t // tm,),
        in_specs=[pl.BlockSpec((tm, D), lambda i: (i, 0)),
                  pl.BlockSpec((1, D), lambda i: (0, 0)),
                  pl.BlockSpec((1, 9, D), lambda i: (i // tpe, 0, 0)),
                  _layer(wup, l), _layer(wdn, l)],
        out_specs=[pl.BlockSpec((tm, D), lambda i: (i, 0)),
                   pl.BlockSpec((tm, 2 * D_FF), lambda i: (i, 0)),
                   pl.BlockSpec((tm, D), lambda i: (i, 0))],
        out_shape=[jax.ShapeDtypeStruct((t, D), F32), jax.ShapeDtypeStruct((t, 2 * D_FF), BF16),
                   jax.ShapeDtypeStruct((t, D), BF16)],
        compiler_params=_cparams(48, 1),
    )(x, gain, mod, wup, wdn)


def ffn_bwd_act(dy, u, f, mod, wdn, l, s, tm=256):
    t = dy.shape[0]
    tpe = t // tm // 2

    def body(dy_ref, u_ref, f_ref, mod_ref, wdn_ref, du_ref, a_ref, df_ref, sg_ref):
        i = pl.program_id(0)
        dyv = dy_ref[...]
        _, _, g = _mod_rows(mod_ref, s)
        dfb = (0.5 * g * dyv).astype(BF16)
        df_ref[...] = dfb

        @pl.when(i % tpe == 0)
        def _():
            sg_ref[...] = jnp.zeros_like(sg_ref)

        sg_ref[0, 0:1, :] += _colsum(0.5 * dyv * f_ref[...].astype(F32))
        for j in range(D_FF // FF_CHUNK):
            lo, hi = j * FF_CHUNK, (j + 1) * FF_CHUNK
            da = _dot_nt(dfb, wdn_ref[lo:hi, :])
            gate = u_ref[:, lo:hi].astype(F32)
            up = u_ref[:, D_FF + lo:D_FF + hi].astype(F32)
            sg = jax.nn.sigmoid(gate)
            silu = gate * sg
            a_ref[:, lo:hi] = (silu * up).astype(BF16)
            du_ref[:, lo:hi] = (da * up * (sg * (1.0 + gate * (1.0 - sg)))).astype(BF16)
            du_ref[:, D_FF + lo:D_FF + hi] = (da * silu).astype(BF16)

    return pl.pallas_call(
        body, name="ffn_bwd_act", grid=(t // tm,),
        in_specs=[pl.BlockSpec((tm, D), lambda i: (i, 0)),
                  pl.BlockSpec((tm, 2 * D_FF), lambda i: (i, 0)),
                  pl.BlockSpec((tm, D), lambda i: (i, 0)),
                  pl.BlockSpec((1, 9, D), lambda i: (i // tpe, 0, 0)),
                  _layer(wdn, l)],
        out_specs=[pl.BlockSpec((tm, 2 * D_FF), lambda i: (i, 0)),
                   pl.BlockSpec((tm, D_FF), lambda i: (i, 0)),
                   pl.BlockSpec((tm, D), lambda i: (i, 0)),
                   pl.BlockSpec((1, 8, D), lambda i: (i // tpe, 0, 0))],
        out_shape=[jax.ShapeDtypeStruct((t, 2 * D_FF), BF16), jax.ShapeDtypeStruct((t, D_FF), BF16),
                   jax.ShapeDtypeStruct((t, D), BF16), jax.ShapeDtypeStruct((2, 8, D), F32)],
        compiler_params=_cparams(40, 1),
    )(dy, u, f, mod, wdn)


def lin_bwd(dy, dz, w, l, x, gain, mod, s, want_db, tm=256):
    t = dy.shape[0]
    n = w.shape[2]
    tpe = t // tm // 2
    nck = -(-n // 1536)
    ck = n // nck

    def body(dy_ref, dz_ref, w_ref, x_ref, gain_ref, mod_ref, dx_ref, h_ref, se_ref, sa_ref, *db_ref):
        i = pl.program_id(0)
        xv = x_ref[...]
        gain_v = gain_ref[...]
        sh, sc, _ = _mod_rows(mod_ref, s)
        r, nrm, h = _norm_mod(xv, gain_v, sc, sh)
        h_ref[...] = h.astype(BF16)
        dh = jnp.zeros((tm, D), F32)
        for j in range(nck):
            dh = dh + _dot_nt(dz_ref[:, j * ck:(j + 1) * ck], w_ref[:, j * ck:(j + 1) * ck])
        dn = dh * (1.0 + sc)
        dxr = dn * gain_v
        m = jnp.mean(dxr * xv, axis=-1, keepdims=True)
        dx_ref[...] = dy_ref[...] + r * dxr - xv * (r * r * r) * m

        @pl.when(i % tpe == 0)
        def _():
            se_ref[...] = jnp.zeros_like(se_ref)

        @pl.when(i == 0)
        def _():
            sa_ref[...] = jnp.zeros_like(sa_ref)
            if want_db:
                db_ref[0][...] = jnp.zeros_like(db_ref[0])

        se_ref[0, 0:1, :] += _colsum(dh)
        se_ref[0, 1:2, :] += _colsum(dh * nrm)
        sa_ref[0:1, :] += _colsum(dn * xv * r)
        if want_db:
            db_ref[0][0:1, :] += _colsum(dz_ref[...].astype(F32))

    out_specs = [pl.BlockSpec((tm, D), lambda i: (i, 0)), pl.BlockSpec((tm, D), lambda i: (i, 0)),
                 pl.BlockSpec((1, 8, D), lambda i: (i // tpe, 0, 0)), pl.BlockSpec((8, D), lambda i: (0, 0))]
    out_shape = [jax.ShapeDtypeStruct((t, D), F32), jax.ShapeDtypeStruct((t, D), BF16),
                 jax.ShapeDtypeStruct((2, 8, D), F32), jax.ShapeDtypeStruct((8, D), F32)]
    if want_db:
        out_specs.append(pl.BlockSpec((8, n), lambda i: (0, 0)))
        out_shape.append(jax.ShapeDtypeStruct((8, n), F32))
    return pl.pallas_call(
        body, name="lin_bwd", grid=(t // tm,),
        in_specs=[pl.BlockSpec((tm, D), lambda i: (i, 0)),
                  pl.BlockSpec((tm, n), lambda i: (i, 0)),
                  _layer(w, l),
                  pl.BlockSpec((tm, D), lambda i: (i, 0)),
                  pl.BlockSpec((1, D), lambda i: (0, 0)),
                  pl.BlockSpec((1, 9, D), lambda i: (i // tpe, 0, 0))],
        out_specs=out_specs, out_shape=out_shape,
        compiler_params=_cparams(40, 1),
    )(dy, dz, w, x, gain, mod)


def wgrad(gstack, l, a, b, bm, bn, bt=512):
    t, m = a.shape
    n = b.shape[1]
    nt = t // bt

    def body(g_ref, a_ref, b_ref, o_ref, acc_ref):
        k = pl.program_id(2)

        @pl.when(k == 0)
        def _():
            acc_ref[...] = jnp.zeros_like(acc_ref)

        acc_ref[...] += _dot_tn(a_ref[...], b_ref[...])

        @pl.when(k == nt - 1)
        def _():
            o_ref[...] = acc_ref[...].astype(BF16)

    return pl.pallas_call(
        body, name="wgrad", grid=(m // bm, n // bn, nt),
        in_specs=[ANY, pl.BlockSpec((bt, bm), lambda i, j, k: (k, i)),
                  pl.BlockSpec((bt, bn), lambda i, j, k: (k, j))],
        out_specs=pl.BlockSpec((None, bm, bn), lambda i, j, k: (l, i, j)),
        out_shape=jax.ShapeDtypeStruct(gstack.shape, BF16),
        input_output_aliases={0: 0},
        scratch_shapes=[pltpu.VMEM((bm, bn), F32)],
        compiler_params=_cparams(40, 3),
    )(gstack, a, b)


def proj_res(x, o, w, l, b, mod, tm=512):
    t = x.shape[0]
    tpe = t // tm // 2

    def body(x_ref, o_ref, w_ref, b_ref, mod_ref, xo_ref, y_ref):
        _, _, g = _mod_rows(mod_ref, 1)
        y = _dot(o_ref[...], w_ref[...]) + b_ref[...]
        y_ref[...] = y.astype(BF16)
        xo_ref[...] = x_ref[...] + g * y

    return pl.pallas_call(
        body, name="proj_res", grid=(t // tm,),
        in_specs=[pl.BlockSpec((tm, D), lambda i: (i, 0)), pl.BlockSpec((tm, D), lambda i: (i, 0)),
                  _layer(w, l), pl.BlockSpec((1, D), lambda i: (0, 0)),
                  pl.BlockSpec((1, 9, D), lambda i: (i // tpe, 0, 0))],
        out_specs=[pl.BlockSpec((tm, D), lambda i: (i, 0)), pl.BlockSpec((tm, D), lambda i: (i, 0))],
        out_shape=[jax.ShapeDtypeStruct((t, D), F32), jax.ShapeDtypeStruct((t, D), BF16)],
        compiler_params=_cparams(32, 1),
    )(x, o, w, b, mod)


def proj_res_bwd(dy, y, w, l, mod, tm=512):
    t = dy.shape[0]
    tpe = t // tm // 2

    def body(dy_ref, y_ref, w_ref, mod_ref, dyy_ref, do_ref, se_ref, sa_ref):
        i = pl.program_id(0)
        _, _, g = _mod_rows(mod_ref, 1)
        dyv = dy_ref[...]
        dyy = g * dyv
        dyb = dyy.astype(BF16)
        dyy_ref[...] = dyb
        do_ref[...] = _dot_nt(dyb, w_ref[...]).astype(BF16)

        @pl.when(i % tpe == 0)
        def _():
            se_ref[...] = jnp.zeros_like(se_ref)

        @pl.when(i == 0)
        def _():
            sa_ref[...] = jnp.zeros_like(sa_ref)

        se_ref[0, 0:1, :] += _colsum(dyv * y_ref[...].astype(F32))
        sa_ref[0:1, :] += _colsum(dyy)

    return pl.pallas_call(
        body, name="proj_res_bwd", grid=(t // tm,),
        in_specs=[pl.BlockSpec((tm, D), lambda i: (i, 0)), pl.BlockSpec((tm, D), lambda i: (i, 0)),
                  _layer(w, l), pl.BlockSpec((1, 9, D), lambda i: (i // tpe, 0, 0))],
        out_specs=[pl.BlockSpec((tm, D), lambda i: (i, 0)), pl.BlockSpec((tm, D), lambda i: (i, 0)),
                   pl.BlockSpec((1, 8, D), lambda i: (i // tpe, 0, 0)), pl.BlockSpec((8, D), lambda i: (0, 0))],
        out_shape=[jax.ShapeDtypeStruct((t, D), BF16), jax.ShapeDtypeStruct((t, D), BF16),
                   jax.ShapeDtypeStruct((2, 8, D), F32), jax.ShapeDtypeStruct((8, D), F32)],
        compiler_params=_cparams(32, 1),
    )(dy, y, w, mod)


def lin_fwd(x, gain, mod, w, l, s, tm=512):
    t = x.shape[0]
    n = w.shape[2]
    tpe = t // tm // 2

    def body(x_ref, gain_ref, mod_ref, w_ref, z_ref):
        sh, sc, _ = _mod_rows(mod_ref, s)
        _, _, h = _norm_mod(x_ref[...], gain_ref[...], sc, sh)
        z_ref[...] = _dot(h.astype(BF16), w_ref[...]).astype(BF16)

    return pl.pallas_call(
        body, name="lin_fwd", grid=(t // tm,),
        in_specs=[pl.BlockSpec((tm, D), lambda i: (i, 0)), pl.BlockSpec((1, D), lambda i: (0, 0)),
                  pl.BlockSpec((1, 9, D), lambda i: (i // tpe, 0, 0)), _layer(w, l)],
        out_specs=pl.BlockSpec((tm, n), lambda i: (i, 0)),
        out_shape=jax.ShapeDtypeStruct((t, n), BF16),
        compiler_params=_cparams(40, 1),
    )(x, gain, mod, w)


def rope_tables(pos, invf):
    t = pos.shape[0]
    tm = 1024

    def body(pos_ref, invf_ref, c_ref, s_ref):
        ang = pos_ref[...].astype(F32) * invf_ref[...]
        lane = lax.broadcasted_iota(jnp.int32, (tm, 128), 1)
        sign = jnp.where(lane % HEAD_DIM < HEAD_DIM // 2, -1.0, 1.0)
        c_ref[...] = jnp.cos(ang)
        s_ref[...] = sign * jnp.sin(ang)

    return pl.pallas_call(
        body, name="rope_tables", grid=(t // tm,),
        in_specs=[pl.BlockSpec((tm, 1), lambda i: (i, 0)), pl.BlockSpec((1, 128), lambda i: (0, 0))],
        out_specs=[pl.BlockSpec((tm, 128), lambda i: (i, 0))] * 2,
        out_shape=[jax.ShapeDtypeStruct((t, 128), F32)] * 2,
        compiler_params=_cparams(16, 1),
    )(pos, invf)


def _swap_halves(v):
    lane = lax.broadcasted_iota(jnp.int32, v.shape, 1)
    return jnp.where(lane % HEAD_DIM < HEAD_DIM // 2, pltpu.roll(v, 128 - HEAD_DIM // 2, 1), pltpu.roll(v, HEAD_DIM // 2, 1))


def _rope(v, cos, sin):
    return jnp.concatenate(
        [v[:, j:j + 128] * cos + _swap_halves(v[:, j:j + 128]) * sin for j in range(0, v.shape[1], 128)], axis=1)


def _rope_t(dv, cos, sin):
    return jnp.concatenate(
        [dv[:, j:j + 128] * cos + _swap_halves(dv[:, j:j + 128] * sin) for j in range(0, dv.shape[1], 128)], axis=1)


def _head_stats(qk, g1, g2):
    rinv = lax.rsqrt(_dot_hilo(qk * qk, g1) + EPS)
    return rinv, _dot_hilo(rinv, g2)


def qkv_fwd(x, gain, mod, w, l, b, gqk, cos, sin, g1, g2, tm=256):
    t = x.shape[0]
    tpe = t // tm // 2

    def body(x_ref, gain_ref, mod_ref, w_ref, b_ref, gqk_ref, c_ref, s_ref, g1_ref, g2_ref, raw_ref, q_ref, k_ref):
        sh, sc, _ = _mod_rows(mod_ref, 1)
        _, _, h = _norm_mod(x_ref[...], gain_ref[...], sc, sh)
        qkv = _dot(h.astype(BF16), w_ref[...]) + b_ref[...]
        raw_ref[...] = qkv.astype(BF16)
        qk = qkv[:, :QK_DIM]
        _, rb = _head_stats(qk, g1_ref[...], g2_ref[...])
        qr = _rope(qk * rb * gqk_ref[...], c_ref[...], s_ref[...])
        q_ref[...] = qr[:, :D].astype(BF16)
        k_ref[...] = qr[:, D:].astype(BF16)

    return pl.pallas_call(
        body, name="qkv_fwd", grid=(t // tm,),
        in_specs=[pl.BlockSpec((tm, D), lambda i: (i, 0)), pl.BlockSpec((1, D), lambda i: (0, 0)),
                  pl.BlockSpec((1, 9, D), lambda i: (i // tpe, 0, 0)), _layer(w, l),
                  pl.BlockSpec((1, QKV_DIM), lambda i: (0, 0)), pl.BlockSpec((1, QK_DIM), lambda i: (0, 0)),
                  pl.BlockSpec((tm, 128), lambda i: (i, 0)), pl.BlockSpec((tm, 128), lambda i: (i, 0)),
                  _resident((QK_DIM, 128)), _resident((128, QK_DIM))],
        out_specs=[pl.BlockSpec((tm, QKV_DIM), lambda i: (i, 0)), pl.BlockSpec((tm, D), lambda i: (i, 0)),
                   pl.BlockSpec((tm, N_KV * HEAD_DIM), lambda i: (i, 0))],
        out_shape=[jax.ShapeDtypeStruct((t, QKV_DIM), BF16), jax.ShapeDtypeStruct((t, D), BF16),
                   jax.ShapeDtypeStruct((t, N_KV * HEAD_DIM), BF16)],
        compiler_params=_cparams(40, 1),
    )(x, gain, mod, w, b, gqk, cos, sin, g1, g2)


def qkv_bwd_pre(dq, dk, dv, raw, gqk, cos, sin, g1, g2, gsel, tm=256):
    t = dq.shape[0]

    def body(dq_ref, dk_ref, dv_ref, raw_ref, gqk_ref, c_ref, s_ref, g1_ref, g2_ref, gsel_ref, dz_ref, sa_ref):
        i = pl.program_id(0)
        dqk = jnp.concatenate([dq_ref[...].astype(F32), dk_ref[...]], axis=1)
        dqn = _rope_t(dqk, c_ref[...], s_ref[...])
        qk = raw_ref[:, :QK_DIM].astype(F32)
        g1v, g2v = g1_ref[...], g2_ref[...]
        rinv, rb = _head_stats(qk, g1v, g2v)
        dgq = jnp.broadcast_to(_colsum(dqn * qk * rb), (8, QK_DIM))
        dyh = dqn * gqk_ref[...]
        mh = _dot_hilo(dyh * qk, g1v)
        mb = _dot_hilo(mh * rinv * rinv * rinv, g2v)
        dz_ref[:, :QK_DIM] = (rb * dyh - qk * mb).astype(BF16)
        dz_ref[:, QK_DIM:] = dv_ref[...].astype(BF16)

        @pl.when(i == 0)
        def _():
            sa_ref[...] = jnp.zeros_like(sa_ref)

        sa_ref[...] += _dot_hilo(dgq, gsel_ref[...])

    kvw = N_KV * HEAD_DIM
    return pl.pallas_call(
        body, name="qkv_bwd_pre", grid=(t // tm,),
        in_specs=[pl.BlockSpec((tm, D), lambda i: (i, 0)), pl.BlockSpec((tm, kvw), lambda i: (i, 0)),
                  pl.BlockSpec((tm, kvw), lambda i: (i, 0)), pl.BlockSpec((tm, QKV_DIM), lambda i: (i, 0)),
                  pl.BlockSpec((1, QK_DIM), lambda i: (0, 0)),
                  pl.BlockSpec((tm, 128), lambda i: (i, 0)), pl.BlockSpec((tm, 128), lambda i: (i, 0)),
                  _resident((QK_DIM, 128)), _resident((128, QK_DIM)), _resident((QK_DIM, 128))],
        out_specs=[pl.BlockSpec((tm, QKV_DIM), lambda i: (i, 0)), pl.BlockSpec((8, 128), lambda i: (0, 0))],
        out_shape=[jax.ShapeDtypeStruct((t, QKV_DIM), BF16), jax.ShapeDtypeStruct((8, 128), F32)],
        compiler_params=_cparams(40, 1),
    )(dq, dk, dv, raw, gqk, cos, sin, g1, g2, gsel)


def _band_mask(n):
    row = lax.broadcasted_iota(jnp.int32, (GROUP * BLOCK, 2 * BLOCK), 0) % BLOCK
    col = lax.broadcasted_iota(jnp.int32, (GROUP * BLOCK, 2 * BLOCK), 1)
    rel = row + BLOCK - col
    return (rel >= 0) & (rel < BLOCK) & ((col >= BLOCK) | (n > 0))


def _stack_heads(v, g):
    base = g * GROUP * HEAD_DIM
    return jnp.concatenate([v[:, base + j * HEAD_DIM:base + (j + 1) * HEAD_DIM] for j in range(GROUP)], axis=0)


def _kv_cat(prev, cur, g):
    return jnp.concatenate([prev[:, g * HEAD_DIM:(g + 1) * HEAD_DIM], cur[:, g * HEAD_DIM:(g + 1) * HEAD_DIM]], axis=0)


def _sink_col(sink_ref, g):
    return jnp.concatenate([jnp.full((BLOCK, 1), sink_ref[0, g * GROUP + j], F32) for j in range(GROUP)], axis=0)


def _attn_specs(nb):
    kvw = N_KV * HEAD_DIM
    vcol = QK_DIM // kvw
    cur = lambda e, n: (e * nb + n, 0)
    prev = lambda e, n: (e * nb + jnp.maximum(n - 1, 0), 0)
    return [pl.BlockSpec((BLOCK, D), cur),
            pl.BlockSpec((BLOCK, kvw), cur), pl.BlockSpec((BLOCK, kvw), prev),
            pl.BlockSpec((BLOCK, kvw), lambda e, n: (e * nb + n, vcol)),
            pl.BlockSpec((BLOCK, kvw), lambda e, n: (e * nb + jnp.maximum(n - 1, 0), vcol)),
            pl.BlockSpec(memory_space=pltpu.SMEM)]


def attn_fwd(q, k, raw, sinks):
    t = q.shape[0]
    nb = t // 2 // BLOCK

    def body(q_ref, kc_ref, kp_ref, vc_ref, vp_ref, sink_ref, o_ref, lse_ref):
        n = pl.program_id(1)
        qv = q_ref[...]
        kc, kp, vc, vp = kc_ref[...], kp_ref[...], vc_ref[...], vp_ref[...]
        mask = _band_mask(n)
        outs, lses = [], []
        for g in range(N_KV):
            kk, vv = _kv_cat(kp, kc, g), _kv_cat(vp, vc, g)
            s = jnp.where(mask, _dot_nt(_stack_heads(qv, g), kk) * (HEAD_DIM ** -0.5), -1e30)
            sink = _sink_col(sink_ref, g)
            m = jnp.maximum(jnp.max(s, axis=1, keepdims=True), sink)
            p = jnp.exp(s - m)
            l = jnp.sum(p, axis=1, keepdims=True) + jnp.exp(sink - m)
            o = _dot(p.astype(BF16), vv) / l
            lse = m + jnp.log(l)
            for j in range(GROUP):
                outs.append(o[j * BLOCK:(j + 1) * BLOCK, :])
                lses.append(lse[j * BLOCK:(j + 1) * BLOCK, :])
        o_ref[...] = jnp.concatenate(outs, axis=1).astype(BF16)
        lse_ref[...] = jnp.concatenate(lses, axis=1)

    cur = lambda e, n: (e * nb + n, 0)
    return pl.pallas_call(
        body, name="attn_fwd", grid=(2, nb),
        in_specs=_attn_specs(nb),
        out_specs=[pl.BlockSpec((BLOCK, D), cur), pl.BlockSpec((BLOCK, N_HEADS), cur)],
        out_shape=[jax.ShapeDtypeStruct((t, D), BF16), jax.ShapeDtypeStruct((t, N_HEADS), F32)],
        compiler_params=_cparams(32, 2),
    )(q, k, k, raw, raw, sinks)


def attn_bwd(q, k, raw, sinks, o, do, lse):
    t = q.shape[0]
    s_len = t // 2
    nb = s_len // BLOCK
    kvw = N_KV * HEAD_DIM

    def body(q_ref, kc_ref, kp_ref, vc_ref, vp_ref, sink_ref, o_ref, do_ref, lse_ref, dq_ref, dk_ref, dv_ref, ds_ref):
        n = pl.program_id(1)

        @pl.when(n == 0)
        def _():
            dk_ref[...] = jnp.zeros_like(dk_ref)
            dv_ref[...] = jnp.zeros_like(dv_ref)

        @pl.when((n == 0) & (pl.program_id(0) == 0))
        def _():
            ds_ref[...] = jnp.zeros_like(ds_ref)

        qv, ov, dov, lsev = q_ref[...], o_ref[...], do_ref[...], lse_ref[...]
        kc, kp, vc, vp = kc_ref[...], kp_ref[...], vc_ref[...], vp_ref[...]
        mask = _band_mask(n)
        dqs, dks, dvs, dsk = [], [], [], []
        for g in range(N_KV):
            kk, vv = _kv_cat(kp, kc, g), _kv_cat(vp, vc, g)
            qg, og, dog = _stack_heads(qv, g), _stack_heads(ov, g), _stack_heads(dov, g)
            lse = jnp.concatenate([lsev[:, g * GROUP + j:g * GROUP + j + 1] for j in range(GROUP)], axis=0)
            s = jnp.where(mask, _dot_nt(qg, kk) * (HEAD_DIM ** -0.5), -1e30)
            p = jnp.exp(s - lse)
            dd = jnp.sum(dog.astype(F32) * og.astype(F32), axis=1, keepdims=True)
            ds = (p * (_dot_nt(dog, vv) - dd) * (HEAD_DIM ** -0.5)).astype(BF16)
            dqg = _dot(ds, kk)
            dks.append(_dot_tn(ds, qg))
            dvs.append(_dot_tn(p.astype(BF16), dog))
            wsink = jnp.exp(_sink_col(sink_ref, g) - lse) * dd
            for j in range(GROUP):
                dqs.append(dqg[j * BLOCK:(j + 1) * BLOCK, :])
                dsk.append(wsink[j * BLOCK:(j + 1) * BLOCK, :])
        dq_ref[...] = jnp.concatenate(dqs, axis=1).astype(BF16)
        dkk = jnp.concatenate(dks, axis=1)
        dvv = jnp.concatenate(dvs, axis=1)
        prev0 = pl.multiple_of(jnp.maximum(n - 1, 0) * BLOCK, BLOCK)
        cur0 = pl.multiple_of(n * BLOCK, BLOCK)
        dk_ref[pl.ds(prev0, BLOCK), :] += dkk[:BLOCK]
        dv_ref[pl.ds(prev0, BLOCK), :] += dvv[:BLOCK]
        dk_ref[pl.ds(cur0, BLOCK), :] += dkk[BLOCK:]
        dv_ref[pl.ds(cur0, BLOCK), :] += dvv[BLOCK:]
        ds_ref[0:1, :] -= _colsum(jnp.concatenate(dsk, axis=1))

    cur = lambda e, n: (e * nb + n, 0)
    return pl.pallas_call(
        body, name="attn_bwd", grid=(2, nb),
        in_specs=_attn_specs(nb) + [pl.BlockSpec((BLOCK, D), cur), pl.BlockSpec((BLOCK, D), cur),
                                    pl.BlockSpec((BLOCK, N_HEADS), cur)],
        out_specs=[pl.BlockSpec((BLOCK, D), cur), pl.BlockSpec((s_len, kvw), lambda e, n: (e, 0)),
                   pl.BlockSpec((s_len, kvw), lambda e, n: (e, 0)), pl.BlockSpec((8, N_HEADS), lambda e, n: (0, 0))],
        out_shape=[jax.ShapeDtypeStruct((t, D), BF16), jax.ShapeDtypeStruct((t, kvw), F32),
                   jax.ShapeDtypeStruct((t, kvw), F32), jax.ShapeDtypeStruct((8, N_HEADS), F32)],
        compiler_params=_cparams(32, 2),
    )(q, k, k, raw, raw, sinks, o, do, lse)


CONV_COLS = 256


def _conv_specs(s_len):
    nct = D // CONV_COLS
    return [pl.BlockSpec((s_len, CONV_COLS), lambda j, e: (e, j)),
            pl.BlockSpec((s_len, CONV_COLS), lambda j, e: (e, nct + j)),
            pl.BlockSpec((s_len, CONV_COLS), lambda j, e: (e, 2 * nct + j)),
            pl.BlockSpec((3, CONV_COLS), lambda j, e: (0, j))]


def _conv_taps(gc, v, w, s_len):
    u = gc * v
    row = lax.broadcasted_iota(jnp.int32, u.shape, 0)
    u1 = jnp.where(row >= 1, pltpu.roll(u, 1, 0), 0.0)
    u2 = jnp.where(row >= 2, pltpu.roll(u, 2, 0), 0.0)
    return u, u1, u2, w[2:3, :] * u + w[1:2, :] * u1 + w[0:1, :] * u2


def conv_fwd(z, w):
    t = z.shape[0]
    s_len = t // 2

    def body(gb_ref, gc_ref, v_ref, w_ref, p_ref):
        _, _, _, conv = _conv_taps(gc_ref[...].astype(F32), v_ref[...].astype(F32), w_ref[...], s_len)
        p_ref[...] = (gb_ref[...].astype(F32) * conv).astype(BF16)

    return pl.pallas_call(
        body, name="conv_fwd", grid=(D // CONV_COLS, 2),
        in_specs=_conv_specs(s_len),
        out_specs=pl.BlockSpec((s_len, CONV_COLS), lambda j, e: (e, j)),
        out_shape=jax.ShapeDtypeStruct((t, D), BF16),
        compiler_params=_cparams(40, 2),
    )(z, z, z, w)


def conv_bwd(z, w, dp):
    t = z.shape[0]
    s_len = t // 2

    def body(gb_ref, gc_ref, v_ref, w_ref, dp_ref, dgb_ref, dgc_ref, dv_ref, dw_ref):
        e = pl.program_id(1)
        gc, v, wv = gc_ref[...].astype(F32), v_ref[...].astype(F32), w_ref[...]
        u, u1, u2, conv = _conv_taps(gc, v, wv, s_len)
        dpv = dp_ref[...].astype(F32)
        dgb_ref[...] = (dpv * conv).astype(BF16)
        dc = dpv * gb_ref[...].astype(F32)
        row = lax.broadcasted_iota(jnp.int32, dc.shape, 0)
        dc1 = jnp.where(row <= s_len - 2, pltpu.roll(dc, s_len - 1, 0), 0.0)
        dc2 = jnp.where(row <= s_len - 3, pltpu.roll(dc, s_len - 2, 0), 0.0)
        du = wv[2:3, :] * dc + wv[1:2, :] * dc1 + wv[0:1, :] * dc2
        dgc_ref[...] = (du * v).astype(BF16)
        dv_ref[...] = (du * gc).astype(BF16)

        @pl.when(e == 0)
        def _():
            dw_ref[...] = jnp.zeros_like(dw_ref)

        dw_ref[0:1, :] += _colsum(dc * u2)
        dw_ref[1:2, :] += _colsum(dc * u1)
        dw_ref[2:3, :] += _colsum(dc * u)

    blk = pl.BlockSpec((s_len, CONV_COLS), lambda j, e: (e, j))
    return pl.pallas_call(
        body, name="conv_bwd", grid=(D // CONV_COLS, 2),
        in_specs=_conv_specs(s_len) + [blk],
        out_specs=[blk, blk, blk, pl.BlockSpec((8, CONV_COLS), lambda j, e: (0, j))],
        out_shape=[jax.ShapeDtypeStruct((t, D), BF16)] * 3 + [jax.ShapeDtypeStruct((8, D), F32)],
        compiler_params=_cparams(48, 2),
    )(z, z, z, w, dp)


def loss_grad(y, tgt, tm=512):
    t = y.shape[0]

    def body(y_ref, t_ref, dy_ref, l_ref):
        i = pl.program_id(0)
        d = y_ref[...] - t_ref[...]
        dy_ref[...] = d * (1.0 / D)

        @pl.when(i == 0)
        def _():
            l_ref[...] = jnp.zeros_like(l_ref)

        l_ref[...] += 0.5 / D * jnp.sum(d * d)

    return pl.pallas_call(
        body, name="loss_grad", grid=(t // tm,),
        in_specs=[pl.BlockSpec((tm, D), lambda i: (i, 0))] * 2,
        out_specs=[pl.BlockSpec((tm, D), lambda i: (i, 0)), pl.BlockSpec((8, 128), lambda i: (0, 0))],
        out_shape=[jax.ShapeDtypeStruct((t, D), F32), jax.ShapeDtypeStruct((8, 128), F32)],
        compiler_params=_cparams(32, 1),
    )(y, tgt)


ADA_COLS = 384


def ada_fwd(c_all, w):
    nl, _, n = w.shape
    nex = c_all.shape[0]

    def body(c_ref, w_ref, o_ref):
        cv = c_ref[...]
        ca = (cv * jax.nn.sigmoid(cv)).astype(BF16)
        o_ref[0] = _dot(ca, w_ref[0].astype(BF16))

    return pl.pallas_call(
        body, name="ada_fwd", grid=(nl, n // ADA_COLS),
        in_specs=[pl.BlockSpec((nex, D), lambda l, j: (0, 0)), pl.BlockSpec((1, D, ADA_COLS), lambda l, j: (l, 0, j))],
        out_specs=pl.BlockSpec((1, nex, ADA_COLS), lambda l, j: (l, 0, j)),
        out_shape=jax.ShapeDtypeStruct((nl, nex, n), F32),
        compiler_params=_cparams(32, 2),
    )(c_all, w)


def _adam_math(w, g, m, v):
    m = ADAM_B1 * m + (1.0 - ADAM_B1) * g
    v = ADAM_B2 * v + (1.0 - ADAM_B2) * (g * g)
    m_hat = m / (1.0 - ADAM_B1 ** ADAM_STEP)
    v_hat = v / (1.0 - ADAM_B2 ** ADAM_STEP)
    return -ADAM_LR * (m_hat / (jnp.sqrt(v_hat) + ADAM_EPS) + ADAM_WD * w), m, v


def ada_bwd_adam(c_all, dm, w, m, v):
    nl, _, n = w.shape
    nex = c_all.shape[0]

    def body(c_ref, dm_ref, w_ref, m_ref, v_ref, g_ref, d_ref, mo_ref, vo_ref):
        cv = c_ref[...]
        ca = (cv * jax.nn.sigmoid(cv)).astype(BF16)
        g = _dot_tn(ca, dm_ref[0].astype(BF16))
        g_ref[0] = g
        d_ref[0], mo_ref[0], vo_ref[0] = _adam_math(w_ref[0], g, m_ref[0], v_ref[0])

    wspec = pl.BlockSpec((1, D, ADA_COLS), lambda l, j: (l, 0, j))
    return pl.pallas_call(
        body, name="ada_bwd_adam", grid=(nl, n // ADA_COLS),
        in_specs=[pl.BlockSpec((nex, D), lambda l, j: (0, 0)), pl.BlockSpec((1, nex, ADA_COLS), lambda l, j: (l, 0, j)),
                  wspec, wspec, wspec],
        out_specs=[wspec] * 4,
        out_shape=[jax.ShapeDtypeStruct(w.shape, F32)] * 4,
        compiler_params=_cparams(40, 2),
    )(c_all, dm, w, m, v)


def adamw(w, g, m, v):
    shape = w.shape
    cols = shape[-1]
    rows = w.size // cols
    args = [a.reshape(rows, cols) for a in (w, g, m, v)]
    tr = rows
    while tr * cols * 4 > (1 << 20) and tr % 16 == 0:
        tr //= 2

    def body(w_ref, g_ref, m_ref, v_ref, d_ref, mo_ref, vo_ref):
        d_ref[...], mo_ref[...], vo_ref[...] = _adam_math(w_ref[...], g_ref[...], m_ref[...], v_ref[...])

    spec = pl.BlockSpec((tr, cols), lambda i: (i, 0))
    outs = pl.pallas_call(
        body, name="adamw", grid=(rows // tr,),
        in_specs=[spec] * 4, out_specs=[spec] * 3,
        out_shape=[jax.ShapeDtypeStruct((rows, cols), F32)] * 3,
        compiler_params=_cparams(32, 1),
    )(*args)
    return [o.reshape(shape) for o in outs]


def cast_into_window(w, l0, n, ax, chip):
    _, r, c = w.shape
    tr = r
    while tr * c * 4 > (2 << 20) and tr % 32 == 0:
        tr //= 2
    nrb = r // tr
    full = (n, r * N_CHIPS, c) if ax == 1 else (n, r, c * N_CHIPS)

    def body(chip_ref, w_ref, o_ref):
        o_ref[...] = w_ref[...].astype(BF16)

    def omap(l, i, chip_ref):
        return (l, chip_ref[0] * nrb + i, 0) if ax == 1 else (l, i, chip_ref[0])

    return pl.pallas_call(
        body, name="cast_into_window",
        grid_spec=pltpu.PrefetchScalarGridSpec(
            num_scalar_prefetch=1, grid=(n, nrb),
            in_specs=[pl.BlockSpec((1, tr, c), lambda l, i, chip_ref: (l0 + l, i, 0))],
            out_specs=pl.BlockSpec((1, tr, c), omap)),
        out_shape=jax.ShapeDtypeStruct(full, BF16), compiler_params=_cparams(32, 2),
    )(chip, w)


def add_bias(a, b):
    def body(a_ref, b_ref, o_ref):
        o_ref[...] = a_ref[...] + b_ref[...]

    return pl.pallas_call(body, name="add_bias", out_shape=jax.ShapeDtypeStruct(a.shape, F32))(a, b)


N_DMOD_ROWS = 40


def reduce_small(p_all):
    rows = p_all.shape[1]

    def body(p_ref, red_ref, ex_ref):
        acc = p_ref[0]
        for d in range(1, 8):
            acc = acc + p_ref[d]
        red_ref[...] = acc
        ex_ref[...] = acc[:N_DMOD_ROWS] + acc[N_DMOD_ROWS:2 * N_DMOD_ROWS]

    return pl.pallas_call(
        body, name="reduce_small",
        out_shape=[jax.ShapeDtypeStruct((rows, D), F32), jax.ShapeDtypeStruct((N_DMOD_ROWS, D), F32)],
        compiler_params=_cparams(32, 0),
    )(p_all)


def _place():
    return lax.axis_index("x"), lax.axis_index("y"), lax.axis_index("c")


def _other_chips(x, y):
    return [(1 - x, y), (x, 1 - y), (1 - x, 1 - y)]


def _sl(ref, axis, start, size):
    idx = [slice(None)] * len(ref.shape)
    idx[axis] = pl.ds(pl.multiple_of(start, 16), size)
    return ref.at[tuple(idx)]


def _rcopy(src, dst, send_sem, recv_sem, to):
    return pltpu.make_async_remote_copy(src_ref=src, dst_ref=dst, send_sem=send_sem, recv_sem=recv_sem,
                                        device_id=to, device_id_type=MESH)


def allgather_small(v, all_devices):
    rows, cols = v.shape
    flips = [(dx, dy, dc) for dx in (0, 1) for dy in (0, 1) for dc in (0, 1)
             if (dx, dy, dc) != (0, 0, 0) and (all_devices or dc == 0)]
    n_out = 8 if all_devices else 4

    def body(v_ref, o_ref, send_sems, recv_sems):
        x, y, c = _place()

        def slot(px, py, pc):
            return 4 * px + 2 * py + pc if all_devices else 2 * px + py

        peers = [(1 - x if dx else x, 1 - y if dy else y, 1 - c if dc else c) for dx, dy, dc in flips]
        sends = [_rcopy(v_ref, o_ref.at[slot(x, y, c)], send_sems.at[r], recv_sems.at[r], peer)
                 for r, peer in enumerate(peers)]
        for cp in sends:
            cp.start()
        o_ref[slot(x, y, c)] = v_ref[...]
        for r, peer in enumerate(peers):
            _rcopy(v_ref, o_ref.at[slot(*peer)], send_sems.at[r], recv_sems.at[r], peer).wait_recv()
        for cp in sends:
            cp.wait_send()

    vm = pl.BlockSpec(memory_space=pltpu.VMEM)
    return pl.pallas_call(
        body, name="allgather_small_all" if all_devices else "allgather_small_chips",
        in_specs=[vm], out_specs=vm,
        out_shape=jax.ShapeDtypeStruct((n_out, rows, cols), v.dtype),
        scratch_shapes=[pltpu.SemaphoreType.DMA((len(flips),)), pltpu.SemaphoreType.DMA((len(flips),))],
        compiler_params=pltpu.CompilerParams(vmem_limit_bytes=32 * 1024 * 1024),
    )(v)


def gather_weights(bufs, axes):
    na = len(bufs)

    def body(*refs):
        outs = refs[na:2 * na]
        ici_s, ici_r, d2d_s, d2d_r = refs[2 * na:]
        x, y, c = _place()
        chips = _other_chips(x, y)

        def win(a, ref, chip):
            n = ref.shape[axes[a]] // N_CHIPS
            return _sl(ref, axes[a], (2 * chip[0] + chip[1]) * n, n)

        def half(a, ref, cc):
            ha = 3 - axes[a]
            hs = ref.shape[ha] // 2
            return _sl(ref, ha, cc * hs, hs)

        sends, passed = [], []
        for a in range(na):
            mine = half(a, win(a, outs[a], (x, y)), c)
            for j, chip in enumerate(chips):
                cp = _rcopy(mine, mine, ici_s.at[3 * a + j], ici_r.at[3 * a + j], (*chip, c))
                cp.start()
                sends.append(cp)
        for a in range(na):
            for j, chip in enumerate(chips):
                got = half(a, win(a, outs[a], chip), c)
                _rcopy(got, got, ici_s.at[3 * a + j], ici_r.at[3 * a + j], (*chip, c)).wait_recv()
                cp = _rcopy(got, got, d2d_s.at[3 * a + j], d2d_r.at[3 * a + j], (x, y, 1 - c))
                cp.start()
                passed.append(cp)
        for a in range(na):
            for j, chip in enumerate(chips):
                got = half(a, win(a, outs[a], chip), 1 - c)
                _rcopy(got, got, d2d_s.at[3 * a + j], d2d_r.at[3 * a + j], (x, y, 1 - c)).wait_recv()
        for cp in sends + passed:
            cp.wait_send()

    dma = pltpu.SemaphoreType.DMA
    return pl.pallas_call(
        body, name="gather_weights",
        in_specs=[ANY] * na, out_specs=[ANY] * na,
        out_shape=[jax.ShapeDtypeStruct(b.shape, BF16) for b in bufs],
        input_output_aliases={a: a for a in range(na)},
        scratch_shapes=[dma((3 * na,)), dma((3 * na,)), dma((3 * na,)), dma((3 * na,))],
    )(*bufs)


def exchange_halves(grads, axes):
    na = len(grads)

    def hshape(g, ax):
        ha = 3 - ax
        return tuple(d // 2 if i == ha else d for i, d in enumerate(g.shape))

    def body(*refs):
        ins, outs = refs[:na], refs[na:2 * na]
        send_sems, recv_sems = refs[2 * na:]
        x, y, c = _place()
        cps = []
        for a in range(na):
            ha = 3 - axes[a]
            hs = ins[a].shape[ha] // 2
            cp = _rcopy(_sl(ins[a], ha, (1 - c) * hs, hs), outs[a], send_sems.at[a], recv_sems.at[a], (x, y, 1 - c))
            cp.start()
            cps.append(cp)
        for cp in cps:
            cp.wait_recv()
        for cp in cps:
            cp.wait_send()

    dma = pltpu.SemaphoreType.DMA
    return pl.pallas_call(
        body, name="exchange_halves",
        in_specs=[ANY] * na, out_specs=[ANY] * na,
        out_shape=[jax.ShapeDtypeStruct(hshape(g, ax), BF16) for g, ax in zip(grads, axes)],
        scratch_shapes=[dma((na,)), dma((na,))],
    )(*grads)


def scatter_windows(halves, axes):
    na = len(halves)

    def pshape(h, ax):
        return (N_CHIPS - 1,) + tuple(d // N_CHIPS if i == ax else d for i, d in enumerate(h.shape))

    def body(*refs):
        ins, outs = refs[:na], refs[na:2 * na]
        send_sems, recv_sems = refs[2 * na:]
        x, y, c = _place()
        chips = _other_chips(x, y)
        cps = []
        for a in range(na):
            n = ins[a].shape[axes[a]] // N_CHIPS
            for j, chip in enumerate(chips):
                cp = _rcopy(_sl(ins[a], axes[a], (2 * chip[0] + chip[1]) * n, n), outs[a].at[j],
                            send_sems.at[3 * a + j], recv_sems.at[3 * a + j], (*chip, c))
                cp.start()
                cps.append(cp)
        for cp in cps:
            cp.wait_recv()
        for cp in cps:
            cp.wait_send()

    dma = pltpu.SemaphoreType.DMA
    return pl.pallas_call(
        body, name="scatter_windows",
        in_specs=[ANY] * na, out_specs=[ANY] * na,
        out_shape=[jax.ShapeDtypeStruct(pshape(h, ax), BF16) for h, ax in zip(halves, axes)],
        scratch_shapes=[dma((3 * na,)), dma((3 * na,))],
    )(*halves)


def join_halves(gs, regions, axes):
    na = len(gs)

    def body(*refs):
        outs = refs[na:2 * na]
        send_sems, recv_sems = refs[2 * na:]
        x, y, c = _place()
        cps = []
        for a in range(na):
            ha = 3 - axes[a]
            hs = outs[a].shape[ha] // 2
            reg = outs[a].at[pl.ds(*regions[a])]
            mine = _sl(reg, ha, c * hs, hs)
            cp = _rcopy(mine, mine, send_sems.at[a], recv_sems.at[a], (x, y, 1 - c))
            cp.start()
            cps.append((cp, _sl(reg, ha, (1 - c) * hs, hs)))
        for a, (cp, theirs) in enumerate(cps):
            _rcopy(theirs, theirs, send_sems.at[a], recv_sems.at[a], (x, y, 1 - c)).wait_recv()
        for cp, _ in cps:
            cp.wait_send()

    dma = pltpu.SemaphoreType.DMA
    return pl.pallas_call(
        body, name="join_halves",
        in_specs=[ANY] * na, out_specs=[ANY] * na,
        out_shape=[jax.ShapeDtypeStruct(g.shape, F32) for g in gs],
        input_output_aliases={a: a for a in range(na)},
        scratch_shapes=[dma((na,)), dma((na,))],
    )(*gs)


def _tile2(r, c, itemsize, limit):
    bc = c
    while bc > 1536:
        bc //= 2
    assert c % bc == 0 and bc % 128 == 0
    br = r
    while br * bc * itemsize > limit and br % 32 == 0:
        br //= 2
    assert r % br == 0 and br % 16 == 0
    return br, bc


def add_my_half(g, theirs, ax, cc):
    ha = 3 - ax
    nl, r, c = theirs.shape
    br, bc = _tile2(r, c, 2, 1 << 20)
    nrb, ncb = r // br, c // bc

    def body(cc_ref, g_ref, t_ref, o_ref):
        o_ref[...] = (g_ref[...].astype(F32) + t_ref[...].astype(F32)).astype(BF16)

    def gmap(l, i, j, cc_ref):
        return (l, cc_ref[0] * nrb + i, j) if ha == 1 else (l, i, cc_ref[0] * ncb + j)

    blk = pl.BlockSpec((1, br, bc), lambda l, i, j, cc_ref: (l, i, j))
    return pl.pallas_call(
        body, name="add_my_half",
        grid_spec=pltpu.PrefetchScalarGridSpec(
            num_scalar_prefetch=1, grid=(nl, nrb, ncb),
            in_specs=[pl.BlockSpec((1, br, bc), gmap), blk], out_specs=blk),
        out_shape=jax.ShapeDtypeStruct(theirs.shape, BF16),
        compiler_params=_cparams(32, 3),
    )(cc, g, theirs)


def sum_chips(parts, pair, gstack, l0, ax, where):
    _, n, r, c = parts.shape
    ha = 3 - ax
    br, bc = _tile2(r, c, 2, 1 << 19)
    nrb, ncb = r // br, c // bc

    def body(w_ref, p_ref, own_ref, g_ref, o_ref):
        acc = own_ref[...].astype(F32)
        for q in range(N_CHIPS - 1):
            acc = acc + p_ref[q].astype(F32)
        o_ref[...] = acc

    def own_map(l, i, j, w_ref):
        return (l, w_ref[0] * nrb + i, j) if ax == 1 else (l, i, w_ref[0] * ncb + j)

    def out_map(l, i, j, w_ref):
        return (l0 + l, w_ref[1] * nrb + i, j) if ha == 1 else (l0 + l, i, w_ref[1] * ncb + j)

    return pl.pallas_call(
        body, name="sum_chips",
        grid_spec=pltpu.PrefetchScalarGridSpec(
            num_scalar_prefetch=1, grid=(n, nrb, ncb),
            in_specs=[pl.BlockSpec((N_CHIPS - 1, 1, br, bc), lambda l, i, j, w_ref: (0, l, i, j)),
                      pl.BlockSpec((1, br, bc), own_map), ANY],
            out_specs=pl.BlockSpec((1, br, bc), out_map)),
        out_shape=jax.ShapeDtypeStruct(gstack.shape, F32),
        input_output_aliases={3: 0},
        compiler_params=_cparams(32, 3),
    )(where, parts, pair, gstack)


BIG = ("w_ffn_up", "w_ffn_down", "attn_w_qkv", "attn_w_o", "conv_w_in", "conv_w_out")
BIG_AXIS = {"w_ffn_up": 2, "w_ffn_down": 1, "attn_w_qkv": 2, "attn_w_o": 1, "conv_w_in": 2, "conv_w_out": 1}
WEIGHTS = ("norm_gain", "w_ada", "b_ada", "w_ffn_up", "w_ffn_down", "attn_w_qkv", "attn_b_qkv", "attn_q_gain",
           "attn_k_gain", "attn_sinks", "attn_w_o", "attn_b_o", "conv_w_in", "conv_w", "conv_w_out")
N_SMALL_ROWS = 112


def _stack3(a):
    return a.reshape((-1,) + a.shape[-2:])


def _head_matrices():
    lane = jnp.arange(QK_DIM)
    head = lane // HEAD_DIM
    col = jnp.arange(128)
    g1 = jnp.where(head[:, None] == col[None, :], 1.0 / HEAD_DIM, 0.0).astype(BF16)
    g2 = jnp.where(col[:, None] == head[None, :], 1.0, 0.0).astype(BF16)
    fold = lane % HEAD_DIM + jnp.where(head >= N_HEADS, HEAD_DIM, 0)
    gsel = jnp.where(fold[:, None] == col[None, :], 1.0, 0.0).astype(BF16)
    return g1, g2, gsel


def _pad_cols(a, n):
    return jnp.pad(a, ((0, 0), (0, n - a.shape[1])))


def kernel(x, c, positions, norm_gain, w_ada, b_ada, w_ffn_up, w_ffn_down, attn_w_qkv, attn_b_qkv, attn_q_gain, attn_k_gain, attn_sinks, attn_w_o, attn_b_o, conv_w_in, conv_w, conv_w_out, loss_target, m_norm_gain, m_w_ada, m_b_ada, m_w_ffn_up, m_w_ffn_down, m_attn_w_qkv, m_attn_b_qkv, m_attn_q_gain, m_attn_k_gain, m_attn_sinks, m_attn_w_o, m_attn_b_o, m_conv_w_in, m_conv_w, m_conv_w_out, v_norm_gain, v_w_ada, v_b_ada, v_w_ffn_up, v_w_ffn_down, v_attn_w_qkv, v_attn_b_qkv, v_attn_q_gain, v_attn_k_gain, v_attn_sinks, v_attn_w_o, v_attn_b_o, v_conv_w_in, v_conv_w, v_conv_w_out):
    w = dict(norm_gain=norm_gain, w_ada=w_ada, b_ada=b_ada, w_ffn_up=w_ffn_up, w_ffn_down=w_ffn_down,
             attn_w_qkv=attn_w_qkv, attn_b_qkv=attn_b_qkv, attn_q_gain=attn_q_gain, attn_k_gain=attn_k_gain,
             attn_sinks=attn_sinks, attn_w_o=attn_w_o, attn_b_o=attn_b_o, conv_w_in=conv_w_in, conv_w=conv_w,
             conv_w_out=conv_w_out)
    mom = dict(norm_gain=m_norm_gain, w_ada=m_w_ada, b_ada=m_b_ada, w_ffn_up=m_w_ffn_up, w_ffn_down=m_w_ffn_down,
               attn_w_qkv=m_attn_w_qkv, attn_b_qkv=m_attn_b_qkv, attn_q_gain=m_attn_q_gain,
               attn_k_gain=m_attn_k_gain, attn_sinks=m_attn_sinks, attn_w_o=m_attn_w_o, attn_b_o=m_attn_b_o,
               conv_w_in=m_conv_w_in, conv_w=m_conv_w, conv_w_out=m_conv_w_out)
    var = dict(norm_gain=v_norm_gain, w_ada=v_w_ada, b_ada=v_b_ada, w_ffn_up=v_w_ffn_up, w_ffn_down=v_w_ffn_down,
               attn_w_qkv=v_attn_w_qkv, attn_b_qkv=v_attn_b_qkv, attn_q_gain=v_attn_q_gain,
               attn_k_gain=v_attn_k_gain, attn_sinks=v_attn_sinks, attn_w_o=v_attn_w_o, attn_b_o=v_attn_b_o,
               conv_w_in=v_conv_w_in, conv_w=v_conv_w, conv_w_out=v_conv_w_out)

    xi, yi, ci = _place()
    chip = 2 * xi + yi
    dev = 4 * xi + 2 * yi + ci
    nex, s_len, _ = x.shape
    t = nex * s_len
    n_attn, n_conv = attn_w_qkv.shape[0], conv_w_in.shape[0]
    axes = [BIG_AXIS[n] for n in BIG]

    c_all = allgather_small(jnp.pad(c, ((0, 8 - nex), (0, 0))), True)[:, :nex].reshape(8 * nex, D)
    ada_cols = w_ada.shape[2]
    modp = ada_fwd(c_all, w_ada)
    modg = allgather_small(modp.reshape(DEPTH * 8 * nex, ada_cols), False)
    modg = lax.dynamic_slice_in_dim(modg.reshape(N_CHIPS, DEPTH, 8 * nex, ada_cols), dev * nex, nex, axis=2)
    modg = modg.transpose(1, 2, 0, 3).reshape(DEPTH, nex, 9 * D)
    mod = add_bias(modg, b_ada.reshape(DEPTH, 1, 9 * D)).reshape(DEPTH, nex, 9, D)

    small = jnp.concatenate([norm_gain.reshape(DEPTH * 3, -1), conv_w.reshape(n_conv * 3, -1)], axis=0)
    small = jnp.pad(small, ((0, -small.shape[0] % 8), (0, 0)))
    small = allgather_small(small, False).transpose(1, 0, 2).reshape(small.shape[0], D)
    gain_full = small[:DEPTH * 3].reshape(DEPTH, 3, D)
    convw_full = small[DEPTH * 3:DEPTH * 3 + n_conv * 3].reshape(n_conv, 3, D)

    chip_arr = chip.reshape(1).astype(jnp.int32)
    where = jnp.stack([chip, ci]).astype(jnp.int32)
    stacks = [_stack3(w[n]) for n in BIG]
    wup, wdn, wqkv, wo, win, wout = gather_weights(
        [cast_into_window(s, 0, s.shape[0], ax, chip_arr) for s, ax in zip(stacks, axes)], axes)

    invf = ROPE_THETA ** (-jnp.arange(0, HEAD_DIM, 2, dtype=F32) / HEAD_DIM)
    cos, sin = rope_tables(positions.reshape(t, 1), jnp.tile(invf, 4).reshape(1, 128))
    g1, g2, gsel = _head_matrices()
    gqk = [jnp.concatenate([jnp.tile(attn_q_gain[j], N_HEADS), jnp.tile(attn_k_gain[j], N_KV)]).reshape(1, QK_DIM)
           for j in range(n_attn)]
    zero_bias = jnp.zeros((1, D), F32)

    xs = x.reshape(t, D)
    saved = []
    for i in range(DEPTH):
        j = i // 2
        gn, md = gain_full[i], mod[i]
        x0 = xs
        xs, u1, f1 = ffn_fwd(x0, gn[0:1], md, wup, wdn, 2 * i, 0)
        x1 = xs
        if i % 2 == 0:
            raw, qr, kr = qkv_fwd(x1, gn[1:2], md, wqkv, j, attn_b_qkv[j:j + 1], gqk[j], cos, sin, g1, g2)
            o, lse = attn_fwd(qr, kr, raw, attn_sinks[j:j + 1])
            xs, ymix = proj_res(x1, o, wo, j, attn_b_o[j:j + 1], md)
            mix = (raw, qr, kr, o, lse)
        else:
            z = lin_fwd(x1, gn[1:2], md, win, j, 1)
            p = conv_fwd(z, convw_full[j])
            xs, ymix = proj_res(x1, p, wout, j, zero_bias, md)
            mix = (z, p)
        x2 = xs
        xs, u3, f3 = ffn_fwd(x2, gn[2:3], md, wup, wdn, 2 * i + 1, 2)
        saved.append((x0, u1, f1, x1, mix, ymix, x2, u3, f3))
    dy, lpart = loss_grad(xs, loss_target.reshape(t, D))

    gup = lax.empty(wup.shape, BF16)
    gdn = lax.empty(wdn.shape, BF16)
    gqkv = lax.empty(wqkv.shape, BF16)
    go = lax.empty(wo.shape, BF16)
    gin = lax.empty(win.shape, BF16)
    gout = lax.empty(wout.shape, BF16)
    dmod = [None] * DEPTH
    dgain = [None] * DEPTH
    db_qkv, dqk_gain, dsinks, db_o, dconv_w = ([None] * n_attn, [None] * n_attn, [None] * n_attn,
                                                [None] * n_attn, [None] * n_conv)
    for i in reversed(range(DEPTH)):
        j = i // 2
        gn, md = gain_full[i], mod[i]
        x0, u1, f1, x1, mix, ymix, x2, u3, f3 = saved[i]
        du, a, df, sg3 = ffn_bwd_act(dy, u3, f3, md, wdn, 2 * i + 1, 2)
        dy, h, se3, sa3 = lin_bwd(dy, du, wup, 2 * i + 1, x2, gn[2:3], md, 2, False)
        gup = wgrad(gup, 2 * i + 1, h, du, D, FF_CHUNK)
        gdn = wgrad(gdn, 2 * i + 1, a, df, FF_CHUNK, D)
        if i % 2 == 0:
            raw, qr, kr, o, lse = mix
            dyy, do, sp, sb = proj_res_bwd(dy, ymix, wo, j, md)
            go = wgrad(go, j, o, dyy, D, D)
            dq, dk, dv, dsinks[j] = attn_bwd(qr, kr, raw, attn_sinks[j:j + 1], o, do, lse)
            dz, dqk_gain[j] = qkv_bwd_pre(dq, dk, dv, raw, gqk[j], cos, sin, g1, g2, gsel)
            dy, h, se2, sa2, db_qkv[j] = lin_bwd(dy, dz, wqkv, j, x1, gn[1:2], md, 1, True)
            gqkv = wgrad(gqkv, j, h, dz, D, QKV_DIM)
            db_o[j] = sb
        else:
            z, p = mix
            dyy, dp, sp, _ = proj_res_bwd(dy, ymix, wout, j, md)
            gout = wgrad(gout, j, p, dyy, D, D)
            dgb, dgc, dvv, dconv_w[j] = conv_bwd(z, convw_full[j], dp)
            dz = jnp.concatenate([dgb, dgc, dvv], axis=1)
            dy, h, se2, sa2 = lin_bwd(dy, dz, win, j, x1, gn[1:2], md, 1, False)
            gin = wgrad(gin, j, h, dz, D, 1536)
        du, a, df, sg1 = ffn_bwd_act(dy, u1, f1, md, wdn, 2 * i, 0)
        dy, h, se1, sa1 = lin_bwd(dy, du, wup, 2 * i, x0, gn[0:1], md, 0, False)
        gup = wgrad(gup, 2 * i, h, du, D, FF_CHUNK)
        gdn = wgrad(gdn, 2 * i, a, df, FF_CHUNK, D)
        dmod[i] = jnp.stack([se1[:, 0], se1[:, 1], sg1[:, 0], se2[:, 0], se2[:, 1], sp[:, 0],
                             se3[:, 0], se3[:, 1], sg3[:, 0]], axis=1)
        dgain[i] = jnp.stack([sa1[0], sa2[0], sa3[0]], axis=0)
    grad_x = dy.reshape(x.shape)

    dmod_ex = jnp.stack(dmod, axis=1).reshape(nex, DEPTH * 9, D)
    dmod_ex = jnp.pad(dmod_ex, ((0, 0), (0, N_DMOD_ROWS - DEPTH * 9), (0, 0))).reshape(nex * N_DMOD_ROWS, D)
    misc = jnp.concatenate([dqk_gain[jj][0] for jj in range(n_attn)]
                           + [jnp.pad(dsinks[jj][0], (0, 128 - N_HEADS)) for jj in range(n_attn)]
                           + [lpart[0]])
    rows = [dmod_ex,
            jnp.concatenate(dgain, axis=0), jnp.zeros((4, D), F32),
            jnp.concatenate([_pad_cols(db_qkv[jj][0:1], 2 * D).reshape(2, D) for jj in range(n_attn)], axis=0),
            jnp.concatenate([db_o[jj][0:1] for jj in range(n_attn)], axis=0),
            jnp.concatenate([dconv_w[jj][0:3] for jj in range(n_conv)], axis=0),
            jnp.pad(misc, (0, D - misc.shape[0])).reshape(1, D)]
    packed = jnp.concatenate(rows, axis=0)
    packed = jnp.pad(packed, ((0, N_SMALL_ROWS - packed.shape[0]), (0, 0)))
    p_all = allgather_small(packed, True)
    red, exsum = reduce_small(p_all)

    r0 = nex * N_DMOD_ROWS
    grads = {}
    grads["b_ada"] = exsum[:DEPTH * 9].reshape(DEPTH, 9 * D)
    grads["norm_gain"] = lax.dynamic_slice_in_dim(red[r0:r0 + 12].reshape(DEPTH, 3, D), chip * (D // N_CHIPS),
                                                  D // N_CHIPS, axis=2)
    r1 = r0 + 16
    grads["attn_b_qkv"] = red[r1:r1 + 2 * n_attn].reshape(n_attn, 2 * D)[:, :QKV_DIM]
    r2 = r1 + 2 * n_attn
    grads["attn_b_o"] = red[r2:r2 + n_attn]
    r3 = r2 + n_attn
    grads["conv_w"] = lax.dynamic_slice_in_dim(red[r3:r3 + 3 * n_conv].reshape(n_conv, 3, D), chip * (D // N_CHIPS),
                                               D // N_CHIPS, axis=2)
    mrow = red[r3 + 3 * n_conv]
    grads["attn_q_gain"] = jnp.stack([mrow[128 * jj:128 * jj + HEAD_DIM] for jj in range(n_attn)])
    grads["attn_k_gain"] = jnp.stack([mrow[128 * jj + HEAD_DIM:128 * jj + 128] for jj in range(n_attn)])
    grads["attn_sinks"] = jnp.stack([mrow[128 * (n_attn + jj):128 * (n_attn + jj) + N_HEADS] for jj in range(n_attn)])
    loss = mrow[128 * 2 * n_attn]

    dm_all = p_all[:, :r0].reshape(8, nex, N_DMOD_ROWS, D)[:, :, :DEPTH * 9].reshape(8 * nex, DEPTH, 9 * D)
    dm_mine = lax.dynamic_slice_in_dim(dm_all.transpose(1, 0, 2), chip * ada_cols, ada_cols, axis=2)
    g_ada, d_ada, nm_ada, nv_ada = ada_bwd_adam(c_all, dm_mine, w_ada, m_w_ada, v_w_ada)

    gfull = [gup, gdn, gqkv, go, gin, gout]
    theirs = exchange_halves(gfull, axes)
    cc = ci.reshape(1).astype(jnp.int32)
    pair = [add_my_half(g, th, ax, cc) for g, th, ax in zip(gfull, theirs, axes)]
    parts = scatter_windows(pair, axes)
    mine = [sum_chips(p, h, lax.empty(s.shape, F32), 0, ax, where) for p, h, s, ax in zip(parts, pair, stacks, axes)]
    reduced = join_halves(mine, [(0, s.shape[0]) for s in stacks], axes)
    for n, g in zip(BIG, reduced):
        grads[n] = g.reshape(w[n].shape)

    delta, new_m, new_v = {}, {}, {}
    for n in WEIGHTS:
        if n == "w_ada":
            grads[n], delta[n], new_m[n], new_v[n] = g_ada, d_ada, nm_ada, nv_ada
        else:
            delta[n], new_m[n], new_v[n] = adamw(w[n], grads[n], mom[n], var[n])

    return (loss, grad_x, *[grads[n] for n in WEIGHTS], *[delta[n] for n in WEIGHTS],
            *[new_m[n] for n in WEIGHTS], *[new_v[n] for n in WEIGHTS])
```

```python
import functools

import jax
import jax.numpy as jnp
from jax import lax
from jax.experimental import pallas as pl
from jax.experimental.pallas import tpu as pltpu

F32 = jnp.float32
BF16 = jnp.bfloat16

D = 1024
D_FF = 2816
N_HEADS = 16
N_KV = 4
HEAD_DIM = 64
GROUP = N_HEADS // N_KV
QK_DIM = (N_HEADS + N_KV) * HEAD_DIM
QKV_DIM = QK_DIM + N_KV * HEAD_DIM
BLOCK = 128
ROPE_THETA = 10000.0
EPS = 1e-6
DEPTH = 4
N_CHIPS = 4

ADAM_LR = 0.001
ADAM_B1 = 0.9
ADAM_B2 = 0.999
ADAM_EPS = 1e-08
ADAM_WD = 0.01
ADAM_STEP = 10

V7X_VMEM_BYTES = 64 * 1024 * 1024
FF_CHUNK = 1408
MESH = pl.DeviceIdType.MESH
ANY = pl.BlockSpec(memory_space=pl.ANY)


def _cparams(vmem_mb, n_grid):
    assert vmem_mb * 1024 * 1024 <= V7X_VMEM_BYTES
    return pltpu.CompilerParams(vmem_limit_bytes=vmem_mb * 1024 * 1024,
                                dimension_semantics=("arbitrary",) * n_grid)


def _resident(shape):
    nd = len(shape)
    return pl.BlockSpec(shape, lambda *_: (0,) * nd, pipeline_mode=pl.Buffered(1))


def _layer(w, l):
    return pl.BlockSpec((None,) + w.shape[1:], lambda *_: (l, 0, 0), pipeline_mode=pl.Buffered(1))


def _dot(a, b):
    return jnp.dot(a, b, preferred_element_type=F32)


def _dot_nt(a, b):
    return lax.dot_general(a, b, (((1,), (1,)), ((), ())), preferred_element_type=F32)


def _dot_tn(a, b):
    return lax.dot_general(a, b, (((0,), (0,)), ((), ())), preferred_element_type=F32)


def _dot_hilo(a, g):
    hi = a.astype(BF16)
    lo = (a - hi.astype(F32)).astype(BF16)
    return _dot(hi, g) + _dot(lo, g)


def _colsum(a):
    return jnp.sum(a, axis=0, keepdims=True)


def _norm_mod(x, gain, sc, sh):
    r = lax.rsqrt(jnp.mean(x * x, axis=-1, keepdims=True) + EPS)
    n = x * r * gain
    return r, n, n * (1.0 + sc) + sh


def _mod_rows(mod_ref, s):
    return (mod_ref[0, 3 * s:3 * s + 1, :], mod_ref[0, 3 * s + 1:3 * s + 2, :], mod_ref[0, 3 * s + 2:3 * s + 3, :])


def ffn_fwd(x, gain, mod, wup, wdn, l, s, tm=256, after=()):
    t = x.shape[0]
    tpe = t // tm // 2

    def body(x_ref, gain_ref, mod_ref, wup_ref, wdn_ref, *rest):
        xo_ref, u_ref, f_ref = rest[len(after):]
        xv = x_ref[...]
        sh, sc, g = _mod_rows(mod_ref, s)
        _, _, h = _norm_mod(xv, gain_ref[...], sc, sh)
        hb = h.astype(BF16)
        acc = jnp.zeros((tm, D), F32)
        for j in range(D_FF // FF_CHUNK):
            lo, hi = j * FF_CHUNK, (j + 1) * FF_CHUNK
            gate = _dot(hb, wup_ref[:, lo:hi])
            up = _dot(hb, wup_ref[:, D_FF + lo:D_FF + hi])
            u_ref[:, lo:hi] = gate.astype(BF16)
            u_ref[:, D_FF + lo:D_FF + hi] = up.astype(BF16)
            a = (gate * jax.nn.sigmoid(gate) * up).astype(BF16)
            acc = acc + _dot(a, wdn_ref[lo:hi, :])
        f_ref[...] = acc.astype(BF16)
        xo_ref[...] = xv + 0.5 * g * acc

    return pl.pallas_call(
        body, name="ffn_fwd", grid=(t // tm,),
        in_specs=[pl.BlockSpec((tm, D), lambda i: (i, 0)),
                  pl.BlockSpec((1, D), lambda i: (0, 0)),
                  pl.BlockSpec((1, 9, D), lambda i: (i // tpe, 0, 0)),
                  _layer(wup, l), _layer(wdn, l)] + [ANY] * len(after),
        out_specs=[pl.BlockSpec((tm, D), lambda i: (i, 0)),
                   pl.BlockSpec((tm, 2 * D_FF), lambda i: (i, 0)),
                   pl.BlockSpec((tm, D), lambda i: (i, 0))],
        out_shape=[jax.ShapeDtypeStruct((t, D), F32), jax.ShapeDtypeStruct((t, 2 * D_FF), BF16),
                   jax.ShapeDtypeStruct((t, D), BF16)],
        compiler_params=_cparams(48, 1),
    )(x, gain, mod, wup, wdn, *after)


def ffn_bwd_act(dy, u, f, mod, wdn, l, s, tm=256, after=()):
    t = dy.shape[0]
    tpe = t // tm // 2

    def body(dy_ref, u_ref, f_ref, mod_ref, wdn_ref, *rest):
        du_ref, a_ref, df_ref, sg_ref = rest[len(after):]
        i = pl.program_id(0)
        dyv = dy_ref[...]
        _, _, g = _mod_rows(mod_ref, s)
        dfb = (0.5 * g * dyv).astype(BF16)
        df_ref[...] = dfb

        @pl.when(i % tpe == 0)
        def _():
            sg_ref[...] = jnp.zeros_like(sg_ref)

        sg_ref[0, 0:1, :] += _colsum(0.5 * dyv * f_ref[...].astype(F32))
        for j in range(D_FF // FF_CHUNK):
            lo, hi = j * FF_CHUNK, (j + 1) * FF_CHUNK
            da = _dot_nt(dfb, wdn_ref[lo:hi, :])
            gate = u_ref[:, lo:hi].astype(F32)
            up = u_ref[:, D_FF + lo:D_FF + hi].astype(F32)
            sg = jax.nn.sigmoid(gate)
            silu = gate * sg
            a_ref[:, lo:hi] = (silu * up).astype(BF16)
            du_ref[:, lo:hi] = (da * up * (sg * (1.0 + gate * (1.0 - sg)))).astype(BF16)
            du_ref[:, D_FF + lo:D_FF + hi] = (da * silu).astype(BF16)

    return pl.pallas_call(
        body, name="ffn_bwd_act", grid=(t // tm,),
        in_specs=[pl.BlockSpec((tm, D), lambda i: (i, 0)),
                  pl.BlockSpec((tm, 2 * D_FF), lambda i: (i, 0)),
                  pl.BlockSpec((tm, D), lambda i: (i, 0)),
                  pl.BlockSpec((1, 9, D), lambda i: (i // tpe, 0, 0)),
                  _layer(wdn, l)] + [ANY] * len(after),
        out_specs=[pl.BlockSpec((tm, 2 * D_FF), lambda i: (i, 0)),
                   pl.BlockSpec((tm, D_FF), lambda i: (i, 0)),
                   pl.BlockSpec((tm, D), lambda i: (i, 0)),
                   pl.BlockSpec((1, 8, D), lambda i: (i // tpe, 0, 0))],
        out_shape=[jax.ShapeDtypeStruct((t, 2 * D_FF), BF16), jax.ShapeDtypeStruct((t, D_FF), BF16),
                   jax.ShapeDtypeStruct((t, D), BF16), jax.ShapeDtypeStruct((2, 8, D), F32)],
        compiler_params=_cparams(40, 1),
    )(dy, u, f, mod, wdn, *after)


def lin_bwd(dy, dz, w, l, x, gain, mod, s, want_db, tm=256):
    t = dy.shape[0]
    n = w.shape[2]
    tpe = t // tm // 2
    nck = -(-n // 1536)
    ck = n // nck

    def body(dy_ref, dz_ref, w_ref, x_ref, gain_ref, mod_ref, dx_ref, h_ref, se_ref, sa_ref, *db_ref):
        i = pl.program_id(0)
        xv = x_ref[...]
        gain_v = gain_ref[...]
        sh, sc, _ = _mod_rows(mod_ref, s)
        r, nrm, h = _norm_mod(xv, gain_v, sc, sh)
        h_ref[...] = h.astype(BF16)
        dh = jnp.zeros((tm, D), F32)
        for j in range(nck):
            dh = dh + _dot_nt(dz_ref[:, j * ck:(j + 1) * ck], w_ref[:, j * ck:(j + 1) * ck])
        dn = dh * (1.0 + sc)
        dxr = dn * gain_v
        m = jnp.mean(dxr * xv, axis=-1, keepdims=True)
        dx_ref[...] = dy_ref[...] + r * dxr - xv * (r * r * r) * m

        @pl.when(i % tpe == 0)
        def _():
            se_ref[...] = jnp.zeros_like(se_ref)

        @pl.when(i == 0)
        def _():
            sa_ref[...] = jnp.zeros_like(sa_ref)
            if want_db:
                db_ref[0][...] = jnp.zeros_like(db_ref[0])

        se_ref[0, 0:1, :] += _colsum(dh)
        se_ref[0, 1:2, :] += _colsum(dh * nrm)
        sa_ref[0:1, :] += _colsum(dn * xv * r)
        if want_db:
            db_ref[0][0:1, :] += _colsum(dz_ref[...].astype(F32))

    out_specs = [pl.BlockSpec((tm, D), lambda i: (i, 0)), pl.BlockSpec((tm, D), lambda i: (i, 0)),
                 pl.BlockSpec((1, 8, D), lambda i: (i // tpe, 0, 0)), pl.BlockSpec((8, D), lambda i: (0, 0))]
    out_shape = [jax.ShapeDtypeStruct((t, D), F32), jax.ShapeDtypeStruct((t, D), BF16),
                 jax.ShapeDtypeStruct((2, 8, D), F32), jax.ShapeDtypeStruct((8, D), F32)]
    if want_db:
        out_specs.append(pl.BlockSpec((8, n), lambda i: (0, 0)))
        out_shape.append(jax.ShapeDtypeStruct((8, n), F32))
    return pl.pallas_call(
        body, name="lin_bwd", grid=(t // tm,),
        in_specs=[pl.BlockSpec((tm, D), lambda i: (i, 0)),
                  pl.BlockSpec((tm, n), lambda i: (i, 0)),
                  _layer(w, l),
                  pl.BlockSpec((tm, D), lambda i: (i, 0)),
                  pl.BlockSpec((1, D), lambda i: (0, 0)),
                  pl.BlockSpec((1, 9, D), lambda i: (i // tpe, 0, 0))],
        out_specs=out_specs, out_shape=out_shape,
        compiler_params=_cparams(40, 1),
    )(dy, dz, w, x, gain, mod)


def wgrad(gstack, l, a, b, bm, bn, bt=512):
    t, m = a.shape
    n = b.shape[1]
    nt = t // bt

    def body(g_ref, a_ref, b_ref, o_ref, acc_ref):
        k = pl.program_id(2)

        @pl.when(k == 0)
        def _():
            acc_ref[...] = jnp.zeros_like(acc_ref)

        acc_ref[...] += _dot_tn(a_ref[...], b_ref[...])

        @pl.when(k == nt - 1)
        def _():
            o_ref[...] = acc_ref[...].astype(BF16)

    return pl.pallas_call(
        body, name="wgrad", grid=(m // bm, n // bn, nt),
        in_specs=[ANY, pl.BlockSpec((bt, bm), lambda i, j, k: (k, i)),
                  pl.BlockSpec((bt, bn), lambda i, j, k: (k, j))],
        out_specs=pl.BlockSpec((None, bm, bn), lambda i, j, k: (l, i, j)),
        out_shape=jax.ShapeDtypeStruct(gstack.shape, BF16),
        input_output_aliases={0: 0},
        scratch_shapes=[pltpu.VMEM((bm, bn), F32)],
        compiler_params=_cparams(40, 3),
    )(gstack, a, b)


def proj_res(x, o, w, l, b, mod, tm=512):
    t = x.shape[0]
    tpe = t // tm // 2

    def body(x_ref, o_ref, w_ref, b_ref, mod_ref, xo_ref, y_ref):
        _, _, g = _mod_rows(mod_ref, 1)
        y = _dot(o_ref[...], w_ref[...]) + b_ref[...]
        y_ref[...] = y.astype(BF16)
        xo_ref[...] = x_ref[...] + g * y

    return pl.pallas_call(
        body, name="proj_res", grid=(t // tm,),
        in_specs=[pl.BlockSpec((tm, D), lambda i: (i, 0)), pl.BlockSpec((tm, D), lambda i: (i, 0)),
                  _layer(w, l), pl.BlockSpec((1, D), lambda i: (0, 0)),
                  pl.BlockSpec((1, 9, D), lambda i: (i // tpe, 0, 0))],
        out_specs=[pl.BlockSpec((tm, D), lambda i: (i, 0)), pl.BlockSpec((tm, D), lambda i: (i, 0))],
        out_shape=[jax.ShapeDtypeStruct((t, D), F32), jax.ShapeDtypeStruct((t, D), BF16)],
        compiler_params=_cparams(32, 1),
    )(x, o, w, b, mod)


def proj_res_bwd(dy, y, w, l, mod, tm=512):
    t = dy.shape[0]
    tpe = t // tm // 2

    def body(dy_ref, y_ref, w_ref, mod_ref, dyy_ref, do_ref, se_ref, sa_ref):
        i = pl.program_id(0)
        _, _, g = _mod_rows(mod_ref, 1)
        dyv = dy_ref[...]
        dyy = g * dyv
        dyb = dyy.astype(BF16)
        dyy_ref[...] = dyb
        do_ref[...] = _dot_nt(dyb, w_ref[...]).astype(BF16)

        @pl.when(i % tpe == 0)
        def _():
            se_ref[...] = jnp.zeros_like(se_ref)

        @pl.when(i == 0)
        def _():
            sa_ref[...] = jnp.zeros_like(sa_ref)

        se_ref[0, 0:1, :] += _colsum(dyv * y_ref[...].astype(F32))
        sa_ref[0:1, :] += _colsum(dyy)

    return pl.pallas_call(
        body, name="proj_res_bwd", grid=(t // tm,),
        in_specs=[pl.BlockSpec((tm, D), lambda i: (i, 0)), pl.BlockSpec((tm, D), lambda i: (i, 0)),
                  _layer(w, l), pl.BlockSpec((1, 9, D), lambda i: (i // tpe, 0, 0))],
        out_specs=[pl.BlockSpec((tm, D), lambda i: (i, 0)), pl.BlockSpec((tm, D), lambda i: (i, 0)),
                   pl.BlockSpec((1, 8, D), lambda i: (i // tpe, 0, 0)), pl.BlockSpec((8, D), lambda i: (0, 0))],
        out_shape=[jax.ShapeDtypeStruct((t, D), BF16), jax.ShapeDtypeStruct((t, D), BF16),
                   jax.ShapeDtypeStruct((2, 8, D), F32), jax.ShapeDtypeStruct((8, D), F32)],
        compiler_params=_cparams(32, 1),
    )(dy, y, w, mod)


def lin_fwd(x, gain, mod, w, l, s, tm=512):
    t = x.shape[0]
    n = w.shape[2]
    tpe = t // tm // 2

    def body(x_ref, gain_ref, mod_ref, w_ref, z_ref):
        sh, sc, _ = _mod_rows(mod_ref, s)
        _, _, h = _norm_mod(x_ref[...], gain_ref[...], sc, sh)
        z_ref[...] = _dot(h.astype(BF16), w_ref[...]).astype(BF16)

    return pl.pallas_call(
        body, name="lin_fwd", grid=(t // tm,),
        in_specs=[pl.BlockSpec((tm, D), lambda i: (i, 0)), pl.BlockSpec((1, D), lambda i: (0, 0)),
                  pl.BlockSpec((1, 9, D), lambda i: (i // tpe, 0, 0)), _layer(w, l)],
        out_specs=pl.BlockSpec((tm, n), lambda i: (i, 0)),
        out_shape=jax.ShapeDtypeStruct((t, n), BF16),
        compiler_params=_cparams(40, 1),
    )(x, gain, mod, w)


def rope_tables(pos, invf):
    t = pos.shape[0]
    tm = 1024

    def body(pos_ref, invf_ref, c_ref, s_ref):
        ang = pos_ref[...].astype(F32) * invf_ref[...]
        lane = lax.broadcasted_iota(jnp.int32, (tm, 128), 1)
        sign = jnp.where(lane % HEAD_DIM < HEAD_DIM // 2, -1.0, 1.0)
        c_ref[...] = jnp.cos(ang)
        s_ref[...] = sign * jnp.sin(ang)

    return pl.pallas_call(
        body, name="rope_tables", grid=(t // tm,),
        in_specs=[pl.BlockSpec((tm, 1), lambda i: (i, 0)), pl.BlockSpec((1, 128), lambda i: (0, 0))],
        out_specs=[pl.BlockSpec((tm, 128), lambda i: (i, 0))] * 2,
        out_shape=[jax.ShapeDtypeStruct((t, 128), F32)] * 2,
        compiler_params=_cparams(16, 1),
    )(pos, invf)


def _swap_halves(v):
    lane = lax.broadcasted_iota(jnp.int32, v.shape, 1)
    return jnp.where(lane % HEAD_DIM < HEAD_DIM // 2, pltpu.roll(v, 128 - HEAD_DIM // 2, 1), pltpu.roll(v, HEAD_DIM // 2, 1))


def _rope(v, cos, sin):
    return jnp.concatenate(
        [v[:, j:j + 128] * cos + _swap_halves(v[:, j:j + 128]) * sin for j in range(0, v.shape[1], 128)], axis=1)


def _rope_t(dv, cos, sin):
    return jnp.concatenate(
        [dv[:, j:j + 128] * cos + _swap_halves(dv[:, j:j + 128] * sin) for j in range(0, dv.shape[1], 128)], axis=1)


def _head_stats(qk, g1, g2):
    rinv = lax.rsqrt(_dot_hilo(qk * qk, g1) + EPS)
    return rinv, _dot_hilo(rinv, g2)


def qkv_fwd(x, gain, mod, w, l, b, gqk, cos, sin, g1, g2, tm=256):
    t = x.shape[0]
    tpe = t // tm // 2

    def body(x_ref, gain_ref, mod_ref, w_ref, b_ref, gqk_ref, c_ref, s_ref, g1_ref, g2_ref, raw_ref, q_ref, k_ref):
        sh, sc, _ = _mod_rows(mod_ref, 1)
        _, _, h = _norm_mod(x_ref[...], gain_ref[...], sc, sh)
        qkv = _dot(h.astype(BF16), w_ref[...]) + b_ref[...]
        raw_ref[...] = qkv.astype(BF16)
        qk = qkv[:, :QK_DIM]
        _, rb = _head_stats(qk, g1_ref[...], g2_ref[...])
        qr = _rope(qk * rb * gqk_ref[...], c_ref[...], s_ref[...])
        q_ref[...] = qr[:, :D].astype(BF16)
        k_ref[...] = qr[:, D:].astype(BF16)

    return pl.pallas_call(
        body, name="qkv_fwd", grid=(t // tm,),
        in_specs=[pl.BlockSpec((tm, D), lambda i: (i, 0)), pl.BlockSpec((1, D), lambda i: (0, 0)),
                  pl.BlockSpec((1, 9, D), lambda i: (i // tpe, 0, 0)), _layer(w, l),
                  pl.BlockSpec((1, QKV_DIM), lambda i: (0, 0)), pl.BlockSpec((1, QK_DIM), lambda i: (0, 0)),
                  pl.BlockSpec((tm, 128), lambda i: (i, 0)), pl.BlockSpec((tm, 128), lambda i: (i, 0)),
                  _resident((QK_DIM, 128)), _resident((128, QK_DIM))],
        out_specs=[pl.BlockSpec((tm, QKV_DIM), lambda i: (i, 0)), pl.BlockSpec((tm, D), lambda i: (i, 0)),
                   pl.BlockSpec((tm, N_KV * HEAD_DIM), lambda i: (i, 0))],
        out_shape=[jax.ShapeDtypeStruct((t, QKV_DIM), BF16), jax.ShapeDtypeStruct((t, D), BF16),
                   jax.ShapeDtypeStruct((t, N_KV * HEAD_DIM), BF16)],
        compiler_params=_cparams(40, 1),
    )(x, gain, mod, w, b, gqk, cos, sin, g1, g2)


def qkv_bwd_pre(dq, dk, dv, raw, gqk, cos, sin, g1, g2, gsel, tm=256):
    t = dq.shape[0]

    def body(dq_ref, dk_ref, dv_ref, raw_ref, gqk_ref, c_ref, s_ref, g1_ref, g2_ref, gsel_ref, dz_ref, sa_ref):
        i = pl.program_id(0)
        dqk = jnp.concatenate([dq_ref[...].astype(F32), dk_ref[...]], axis=1)
        dqn = _rope_t(dqk, c_ref[...], s_ref[...])
        qk = raw_ref[:, :QK_DIM].astype(F32)
        g1v, g2v = g1_ref[...], g2_ref[...]
        rinv, rb = _head_stats(qk, g1v, g2v)
        dgq = jnp.broadcast_to(_colsum(dqn * qk * rb), (8, QK_DIM))
        dyh = dqn * gqk_ref[...]
        mh = _dot_hilo(dyh * qk, g1v)
        mb = _dot_hilo(mh * rinv * rinv * rinv, g2v)
        dz_ref[:, :QK_DIM] = (rb * dyh - qk * mb).astype(BF16)
        dz_ref[:, QK_DIM:] = dv_ref[...].astype(BF16)

        @pl.when(i == 0)
        def _():
            sa_ref[...] = jnp.zeros_like(sa_ref)

        sa_ref[...] += _dot_hilo(dgq, gsel_ref[...])

    kvw = N_KV * HEAD_DIM
    return pl.pallas_call(
        body, name="qkv_bwd_pre", grid=(t // tm,),
        in_specs=[pl.BlockSpec((tm, D), lambda i: (i, 0)), pl.BlockSpec((tm, kvw), lambda i: (i, 0)),
                  pl.BlockSpec((tm, kvw), lambda i: (i, 0)), pl.BlockSpec((tm, QKV_DIM), lambda i: (i, 0)),
                  pl.BlockSpec((1, QK_DIM), lambda i: (0, 0)),
                  pl.BlockSpec((tm, 128), lambda i: (i, 0)), pl.BlockSpec((tm, 128), lambda i: (i, 0)),
                  _resident((QK_DIM, 128)), _resident((128, QK_DIM)), _resident((QK_DIM, 128))],
        out_specs=[pl.BlockSpec((tm, QKV_DIM), lambda i: (i, 0)), pl.BlockSpec((8, 128), lambda i: (0, 0))],
        out_shape=[jax.ShapeDtypeStruct((t, QKV_DIM), BF16), jax.ShapeDtypeStruct((8, 128), F32)],
        compiler_params=_cparams(40, 1),
    )(dq, dk, dv, raw, gqk, cos, sin, g1, g2, gsel)


def _band_mask(n):
    row = lax.broadcasted_iota(jnp.int32, (GROUP * BLOCK, 2 * BLOCK), 0) % BLOCK
    col = lax.broadcasted_iota(jnp.int32, (GROUP * BLOCK, 2 * BLOCK), 1)
    rel = row + BLOCK - col
    return (rel >= 0) & (rel < BLOCK) & ((col >= BLOCK) | (n > 0))


def _stack_heads(v, g):
    base = g * GROUP * HEAD_DIM
    return jnp.concatenate([v[:, base + j * HEAD_DIM:base + (j + 1) * HEAD_DIM] for j in range(GROUP)], axis=0)


def _kv_cat(prev, cur, g):
    return jnp.concatenate([prev[:, g * HEAD_DIM:(g + 1) * HEAD_DIM], cur[:, g * HEAD_DIM:(g + 1) * HEAD_DIM]], axis=0)


def _sink_col(sink_ref, g):
    return jnp.concatenate([jnp.full((BLOCK, 1), sink_ref[0, g * GROUP + j], F32) for j in range(GROUP)], axis=0)


def _attn_specs(nb):
    kvw = N_KV * HEAD_DIM
    vcol = QK_DIM // kvw
    cur = lambda e, n: (e * nb + n, 0)
    prev = lambda e, n: (e * nb + jnp.maximum(n - 1, 0), 0)
    return [pl.BlockSpec((BLOCK, D), cur),
            pl.BlockSpec((BLOCK, kvw), cur), pl.BlockSpec((BLOCK, kvw), prev),
            pl.BlockSpec((BLOCK, kvw), lambda e, n: (e * nb + n, vcol)),
            pl.BlockSpec((BLOCK, kvw), lambda e, n: (e * nb + jnp.maximum(n - 1, 0), vcol)),
            pl.BlockSpec(memory_space=pltpu.SMEM)]


def attn_fwd(q, k, raw, sinks):
    t = q.shape[0]
    nb = t // 2 // BLOCK

    def body(q_ref, kc_ref, kp_ref, vc_ref, vp_ref, sink_ref, o_ref, lse_ref):
        n = pl.program_id(1)
        qv = q_ref[...]
        kc, kp, vc, vp = kc_ref[...], kp_ref[...], vc_ref[...], vp_ref[...]
        mask = _band_mask(n)
        outs, lses = [], []
        for g in range(N_KV):
            kk, vv = _kv_cat(kp, kc, g), _kv_cat(vp, vc, g)
            s = jnp.where(mask, _dot_nt(_stack_heads(qv, g), kk) * (HEAD_DIM ** -0.5), -1e30)
            sink = _sink_col(sink_ref, g)
            m = jnp.maximum(jnp.max(s, axis=1, keepdims=True), sink)
            p = jnp.exp(s - m)
            l = jnp.sum(p, axis=1, keepdims=True) + jnp.exp(sink - m)
            o = _dot(p.astype(BF16), vv) / l
            lse = m + jnp.log(l)
            for j in range(GROUP):
                outs.append(o[j * BLOCK:(j + 1) * BLOCK, :])
                lses.append(lse[j * BLOCK:(j + 1) * BLOCK, :])
        o_ref[...] = jnp.concatenate(outs, axis=1).astype(BF16)
        lse_ref[...] = jnp.concatenate(lses, axis=1)

    cur = lambda e, n: (e * nb + n, 0)
    return pl.pallas_call(
        body, name="attn_fwd", grid=(2, nb),
        in_specs=_attn_specs(nb),
        out_specs=[pl.BlockSpec((BLOCK, D), cur), pl.BlockSpec((BLOCK, N_HEADS), cur)],
        out_shape=[jax.ShapeDtypeStruct((t, D), BF16), jax.ShapeDtypeStruct((t, N_HEADS), F32)],
        compiler_params=_cparams(32, 2),
    )(q, k, k, raw, raw, sinks)


def attn_bwd(q, k, raw, sinks, o, do, lse):
    t = q.shape[0]
    s_len = t // 2
    nb = s_len // BLOCK
    kvw = N_KV * HEAD_DIM

    def body(q_ref, kc_ref, kp_ref, vc_ref, vp_ref, sink_ref, o_ref, do_ref, lse_ref, dq_ref, dk_ref, dv_ref, ds_ref):
        n = pl.program_id(1)

        @pl.when(n == 0)
        def _():
            dk_ref[...] = jnp.zeros_like(dk_ref)
            dv_ref[...] = jnp.zeros_like(dv_ref)

        @pl.when((n == 0) & (pl.program_id(0) == 0))
        def _():
            ds_ref[...] = jnp.zeros_like(ds_ref)

        qv, ov, dov, lsev = q_ref[...], o_ref[...], do_ref[...], lse_ref[...]
        kc, kp, vc, vp = kc_ref[...], kp_ref[...], vc_ref[...], vp_ref[...]
        mask = _band_mask(n)
        dqs, dks, dvs, dsk = [], [], [], []
        for g in range(N_KV):
            kk, vv = _kv_cat(kp, kc, g), _kv_cat(vp, vc, g)
            qg, og, dog = _stack_heads(qv, g), _stack_heads(ov, g), _stack_heads(dov, g)
            lse = jnp.concatenate([lsev[:, g * GROUP + j:g * GROUP + j + 1] for j in range(GROUP)], axis=0)
            s = jnp.where(mask, _dot_nt(qg, kk) * (HEAD_DIM ** -0.5), -1e30)
            p = jnp.exp(s - lse)
            dd = jnp.sum(dog.astype(F32) * og.astype(F32), axis=1, keepdims=True)
            ds = (p * (_dot_nt(dog, vv) - dd) * (HEAD_DIM ** -0.5)).astype(BF16)
            dqg = _dot(ds, kk)
            dks.append(_dot_tn(ds, qg))
            dvs.append(_dot_tn(p.astype(BF16), dog))
            wsink = jnp.exp(_sink_col(sink_ref, g) - lse) * dd
            for j in range(GROUP):
                dqs.append(dqg[j * BLOCK:(j + 1) * BLOCK, :])
                dsk.append(wsink[j * BLOCK:(j + 1) * BLOCK, :])
        dq_ref[...] = jnp.concatenate(dqs, axis=1).astype(BF16)
        dkk = jnp.concatenate(dks, axis=1)
        dvv = jnp.concatenate(dvs, axis=1)
        prev0 = pl.multiple_of(jnp.maximum(n - 1, 0) * BLOCK, BLOCK)
        cur0 = pl.multiple_of(n * BLOCK, BLOCK)
        dk_ref[pl.ds(prev0, BLOCK), :] += dkk[:BLOCK]
        dv_ref[pl.ds(prev0, BLOCK), :] += dvv[:BLOCK]
        dk_ref[pl.ds(cur0, BLOCK), :] += dkk[BLOCK:]
        dv_ref[pl.ds(cur0, BLOCK), :] += dvv[BLOCK:]
        ds_ref[0:1, :] -= _colsum(jnp.concatenate(dsk, axis=1))

    cur = lambda e, n: (e * nb + n, 0)
    return pl.pallas_call(
        body, name="attn_bwd", grid=(2, nb),
        in_specs=_attn_specs(nb) + [pl.BlockSpec((BLOCK, D), cur), pl.BlockSpec((BLOCK, D), cur),
                                    pl.BlockSpec((BLOCK, N_HEADS), cur)],
        out_specs=[pl.BlockSpec((BLOCK, D), cur), pl.BlockSpec((s_len, kvw), lambda e, n: (e, 0)),
                   pl.BlockSpec((s_len, kvw), lambda e, n: (e, 0)), pl.BlockSpec((8, N_HEADS), lambda e, n: (0, 0))],
        out_shape=[jax.ShapeDtypeStruct((t, D), BF16), jax.ShapeDtypeStruct((t, kvw), F32),
                   jax.ShapeDtypeStruct((t, kvw), F32), jax.ShapeDtypeStruct((8, N_HEADS), F32)],
        compiler_params=_cparams(32, 2),
    )(q, k, k, raw, raw, sinks, o, do, lse)


CONV_COLS = 256


def _conv_specs(s_len):
    nct = D // CONV_COLS
    return [pl.BlockSpec((s_len, CONV_COLS), lambda j, e: (e, j)),
            pl.BlockSpec((s_len, CONV_COLS), lambda j, e: (e, nct + j)),
            pl.BlockSpec((s_len, CONV_COLS), lambda j, e: (e, 2 * nct + j)),
            pl.BlockSpec((3, CONV_COLS), lambda j, e: (0, j))]


def _conv_taps(gc, v, w, s_len):
    u = gc * v
    row = lax.broadcasted_iota(jnp.int32, u.shape, 0)
    u1 = jnp.where(row >= 1, pltpu.roll(u, 1, 0), 0.0)
    u2 = jnp.where(row >= 2, pltpu.roll(u, 2, 0), 0.0)
    return u, u1, u2, w[2:3, :] * u + w[1:2, :] * u1 + w[0:1, :] * u2


def conv_fwd(z, w):
    t = z.shape[0]
    s_len = t // 2

    def body(gb_ref, gc_ref, v_ref, w_ref, p_ref):
        _, _, _, conv = _conv_taps(gc_ref[...].astype(F32), v_ref[...].astype(F32), w_ref[...], s_len)
        p_ref[...] = (gb_ref[...].astype(F32) * conv).astype(BF16)

    return pl.pallas_call(
        body, name="conv_fwd", grid=(D // CONV_COLS, 2),
        in_specs=_conv_specs(s_len),
        out_specs=pl.BlockSpec((s_len, CONV_COLS), lambda j, e: (e, j)),
        out_shape=jax.ShapeDtypeStruct((t, D), BF16),
        compiler_params=_cparams(40, 2),
    )(z, z, z, w)


def conv_bwd(z, w, dp):
    t = z.shape[0]
    s_len = t // 2

    def body(gb_ref, gc_ref, v_ref, w_ref, dp_ref, dgb_ref, dgc_ref, dv_ref, dw_ref):
        e = pl.program_id(1)
        gc, v, wv = gc_ref[...].astype(F32), v_ref[...].astype(F32), w_ref[...]
        u, u1, u2, conv = _conv_taps(gc, v, wv, s_len)
        dpv = dp_ref[...].astype(F32)
        dgb_ref[...] = (dpv * conv).astype(BF16)
        dc = dpv * gb_ref[...].astype(F32)
        row = lax.broadcasted_iota(jnp.int32, dc.shape, 0)
        dc1 = jnp.where(row <= s_len - 2, pltpu.roll(dc, s_len - 1, 0), 0.0)
        dc2 = jnp.where(row <= s_len - 3, pltpu.roll(dc, s_len - 2, 0), 0.0)
        du = wv[2:3, :] * dc + wv[1:2, :] * dc1 + wv[0:1, :] * dc2
        dgc_ref[...] = (du * v).astype(BF16)
        dv_ref[...] = (du * gc).astype(BF16)

        @pl.when(e == 0)
        def _():
            dw_ref[...] = jnp.zeros_like(dw_ref)

        dw_ref[0:1, :] += _colsum(dc * u2)
        dw_ref[1:2, :] += _colsum(dc * u1)
        dw_ref[2:3, :] += _colsum(dc * u)

    blk = pl.BlockSpec((s_len, CONV_COLS), lambda j, e: (e, j))
    return pl.pallas_call(
        body, name="conv_bwd", grid=(D // CONV_COLS, 2),
        in_specs=_conv_specs(s_len) + [blk],
        out_specs=[blk, blk, blk, pl.BlockSpec((8, CONV_COLS), lambda j, e: (0, j))],
        out_shape=[jax.ShapeDtypeStruct((t, D), BF16)] * 3 + [jax.ShapeDtypeStruct((8, D), F32)],
        compiler_params=_cparams(48, 2),
    )(z, z, z, w, dp)


def loss_grad(y, tgt, tm=512):
    t = y.shape[0]

    def body(y_ref, t_ref, dy_ref, l_ref):
        i = pl.program_id(0)
        d = y_ref[...] - t_ref[...]
        dy_ref[...] = d * (1.0 / D)

        @pl.when(i == 0)
        def _():
            l_ref[...] = jnp.zeros_like(l_ref)

        l_ref[...] += 0.5 / D * jnp.sum(d * d)

    return pl.pallas_call(
        body, name="loss_grad", grid=(t // tm,),
        in_specs=[pl.BlockSpec((tm, D), lambda i: (i, 0))] * 2,
        out_specs=[pl.BlockSpec((tm, D), lambda i: (i, 0)), pl.BlockSpec((8, 128), lambda i: (0, 0))],
        out_shape=[jax.ShapeDtypeStruct((t, D), F32), jax.ShapeDtypeStruct((8, 128), F32)],
        compiler_params=_cparams(32, 1),
    )(y, tgt)


ADA_COLS = 384


def ada_fwd(c_all, w):
    nl, _, n = w.shape
    nex = c_all.shape[0]

    def body(c_ref, w_ref, o_ref):
        cv = c_ref[...]
        ca = (cv * jax.nn.sigmoid(cv)).astype(BF16)
        o_ref[0] = _dot(ca, w_ref[0].astype(BF16))

    return pl.pallas_call(
        body, name="ada_fwd", grid=(nl, n // ADA_COLS),
        in_specs=[pl.BlockSpec((nex, D), lambda l, j: (0, 0)), pl.BlockSpec((1, D, ADA_COLS), lambda l, j: (l, 0, j))],
        out_specs=pl.BlockSpec((1, nex, ADA_COLS), lambda l, j: (l, 0, j)),
        out_shape=jax.ShapeDtypeStruct((nl, nex, n), F32),
        compiler_params=_cparams(32, 2),
    )(c_all, w)


def _adam_math(w, g, m, v):
    m = ADAM_B1 * m + (1.0 - ADAM_B1) * g
    v = ADAM_B2 * v + (1.0 - ADAM_B2) * (g * g)
    m_hat = m / (1.0 - ADAM_B1 ** ADAM_STEP)
    v_hat = v / (1.0 - ADAM_B2 ** ADAM_STEP)
    return -ADAM_LR * (m_hat / (jnp.sqrt(v_hat) + ADAM_EPS) + ADAM_WD * w), m, v


def ada_bwd_adam(c_all, dm, w, m, v):
    nl, _, n = w.shape
    nex = c_all.shape[0]

    def body(c_ref, dm_ref, w_ref, m_ref, v_ref, g_ref, d_ref, mo_ref, vo_ref):
        cv = c_ref[...]
        ca = (cv * jax.nn.sigmoid(cv)).astype(BF16)
        g = _dot_tn(ca, dm_ref[0].astype(BF16))
        g_ref[0] = g
        d_ref[0], mo_ref[0], vo_ref[0] = _adam_math(w_ref[0], g, m_ref[0], v_ref[0])

    wspec = pl.BlockSpec((1, D, ADA_COLS), lambda l, j: (l, 0, j))
    return pl.pallas_call(
        body, name="ada_bwd_adam", grid=(nl, n // ADA_COLS),
        in_specs=[pl.BlockSpec((nex, D), lambda l, j: (0, 0)), pl.BlockSpec((1, nex, ADA_COLS), lambda l, j: (l, 0, j)),
                  wspec, wspec, wspec],
        out_specs=[wspec] * 4,
        out_shape=[jax.ShapeDtypeStruct(w.shape, F32)] * 4,
        compiler_params=_cparams(40, 2),
    )(c_all, dm, w, m, v)


def adamw(w, g, m, v):
    shape = w.shape
    cols = shape[-1]
    rows = w.size // cols
    args = [a.reshape(rows, cols) for a in (w, g, m, v)]
    tr = rows
    while tr * cols * 4 > (1 << 20) and tr % 16 == 0:
        tr //= 2

    def body(w_ref, g_ref, m_ref, v_ref, d_ref, mo_ref, vo_ref):
        d_ref[...], mo_ref[...], vo_ref[...] = _adam_math(w_ref[...], g_ref[...], m_ref[...], v_ref[...])

    spec = pl.BlockSpec((tr, cols), lambda i: (i, 0))
    outs = pl.pallas_call(
        body, name="adamw", grid=(rows // tr,),
        in_specs=[spec] * 4, out_specs=[spec] * 3,
        out_shape=[jax.ShapeDtypeStruct((rows, cols), F32)] * 3,
        compiler_params=_cparams(32, 1),
    )(*args)
    return [o.reshape(shape) for o in outs]


def adamw_layers(w, g, m, v, prev, l0, n):
    _, r, c = w.shape
    tr = r
    while tr * c * 4 > (1 << 20) and tr % 16 == 0:
        tr //= 2

    def body(w_ref, g_ref, m_ref, v_ref, pd_ref, pm_ref, pv_ref, d_ref, mo_ref, vo_ref):
        d_ref[...], mo_ref[...], vo_ref[...] = _adam_math(w_ref[...], g_ref[...], m_ref[...], v_ref[...])

    spec = pl.BlockSpec((1, tr, c), lambda l, i: (l0 + l, i, 0))
    return pl.pallas_call(
        body, name="adamw_layers", grid=(n, r // tr),
        in_specs=[spec] * 4 + [ANY] * 3, out_specs=[spec] * 3,
        out_shape=[jax.ShapeDtypeStruct(w.shape, F32)] * 3,
        input_output_aliases={4: 0, 5: 1, 6: 2},
        compiler_params=_cparams(32, 2),
    )(w, g, m, v, *prev)


def cast_into_window(w, l0, n, ax, chip, after=()):
    _, r, c = w.shape
    tr = r
    while tr * c * 4 > (2 << 20) and tr % 32 == 0:
        tr //= 2
    nrb = r // tr
    full = (n, r * N_CHIPS, c) if ax == 1 else (n, r, c * N_CHIPS)

    def body(chip_ref, w_ref, *rest):
        o_ref = rest[len(after)]
        o_ref[...] = w_ref[...].astype(BF16)

    def omap(l, i, chip_ref):
        return (l, chip_ref[0] * nrb + i, 0) if ax == 1 else (l, i, chip_ref[0])

    return pl.pallas_call(
        body, name="cast_into_window",
        grid_spec=pltpu.PrefetchScalarGridSpec(
            num_scalar_prefetch=1, grid=(n, nrb),
            in_specs=[pl.BlockSpec((1, tr, c), lambda l, i, chip_ref: (l0 + l, i, 0))] + [ANY] * len(after),
            out_specs=pl.BlockSpec((1, tr, c), omap)),
        out_shape=jax.ShapeDtypeStruct(full, BF16), compiler_params=_cparams(32, 2),
    )(chip, w, *after)


def add_bias(a, b):
    def body(a_ref, b_ref, o_ref):
        o_ref[...] = a_ref[...] + b_ref[...]

    return pl.pallas_call(body, name="add_bias", out_shape=jax.ShapeDtypeStruct(a.shape, F32))(a, b)


N_DMOD_ROWS = 40


def reduce_small(p_all):
    rows = p_all.shape[1]

    def body(p_ref, red_ref, ex_ref):
        acc = p_ref[0]
        for d in range(1, 8):
            acc = acc + p_ref[d]
        red_ref[...] = acc
        ex_ref[...] = acc[:N_DMOD_ROWS] + acc[N_DMOD_ROWS:2 * N_DMOD_ROWS]

    return pl.pallas_call(
        body, name="reduce_small",
        out_shape=[jax.ShapeDtypeStruct((rows, D), F32), jax.ShapeDtypeStruct((N_DMOD_ROWS, D), F32)],
        compiler_params=_cparams(32, 0),
    )(p_all)


def _place():
    return lax.axis_index("x"), lax.axis_index("y"), lax.axis_index("c")


def _other_chips(x, y):
    return [(1 - x, y), (x, 1 - y), (1 - x, 1 - y)]


def _sl(ref, axis, start, size):
    idx = [slice(None)] * len(ref.shape)
    idx[axis] = pl.ds(pl.multiple_of(start, 16), size)
    return ref.at[tuple(idx)]


def _rcopy(src, dst, send_sem, recv_sem, to):
    return pltpu.make_async_remote_copy(src_ref=src, dst_ref=dst, send_sem=send_sem, recv_sem=recv_sem,
                                        device_id=to, device_id_type=MESH)


def allgather_small(v, all_devices):
    rows, cols = v.shape
    flips = [(dx, dy, dc) for dx in (0, 1) for dy in (0, 1) for dc in (0, 1)
             if (dx, dy, dc) != (0, 0, 0) and (all_devices or dc == 0)]
    n_out = 8 if all_devices else 4

    def body(v_ref, o_ref, send_sems, recv_sems):
        x, y, c = _place()

        def slot(px, py, pc):
            return 4 * px + 2 * py + pc if all_devices else 2 * px + py

        peers = [(1 - x if dx else x, 1 - y if dy else y, 1 - c if dc else c) for dx, dy, dc in flips]
        sends = [_rcopy(v_ref, o_ref.at[slot(x, y, c)], send_sems.at[r], recv_sems.at[r], peer)
                 for r, peer in enumerate(peers)]
        for cp in sends:
            cp.start()
        o_ref[slot(x, y, c)] = v_ref[...]
        for r, peer in enumerate(peers):
            _rcopy(v_ref, o_ref.at[slot(*peer)], send_sems.at[r], recv_sems.at[r], peer).wait_recv()
        for cp in sends:
            cp.wait_send()

    vm = pl.BlockSpec(memory_space=pltpu.VMEM)
    return pl.pallas_call(
        body, name="allgather_small_all" if all_devices else "allgather_small_chips",
        in_specs=[vm], out_specs=vm,
        out_shape=jax.ShapeDtypeStruct((n_out, rows, cols), v.dtype),
        scratch_shapes=[pltpu.SemaphoreType.DMA((len(flips),)), pltpu.SemaphoreType.DMA((len(flips),))],
        compiler_params=pltpu.CompilerParams(vmem_limit_bytes=32 * 1024 * 1024),
    )(v)


HBM = pl.BlockSpec(memory_space=pltpu.HBM)
SEM = pl.BlockSpec(memory_space=pltpu.SEMAPHORE)
SPLIT_COPY = pltpu.CompilerParams(has_side_effects=pltpu.SideEffectType.DATAFLOW_SIDE_EFFECTING)


def _in_hbm(a):
    return pltpu.with_memory_space_constraint(a, pltpu.HBM)


def _window(ref, ax, chip):
    n = ref.shape[ax] // N_CHIPS
    return _sl(ref, ax, (2 * chip[0] + chip[1]) * n, n)


def _half(ref, ax, cc):
    ha = 3 - ax
    hs = ref.shape[ha] // 2
    return _sl(ref, ha, cc * hs, hs)


def gather_start(bufs, axes, tag):
    na = len(bufs)

    def body(*refs):
        ins = refs[:na]
        send_sems, recv_sems = refs[na], refs[na + 1]
        token = refs[2 * na + 2]
        x, y, c = _place()
        for a in range(na):
            mine = _half(_window(ins[a], axes[a], (x, y)), axes[a], c)
            for j, chip in enumerate(_other_chips(x, y)):
                _rcopy(mine, mine, send_sems.at[3 * a + j], recv_sems.at[3 * a + j], (*chip, c)).start()
        token[...] = jnp.zeros_like(token)

    dma = pltpu.SemaphoreType.DMA
    outs = pl.pallas_call(
        body, name="gather_start_" + tag,
        in_specs=[HBM] * na,
        out_specs=(SEM, SEM, *[HBM] * na, pl.BlockSpec(memory_space=pltpu.VMEM)),
        out_shape=(dma((3 * na,)), dma((3 * na,)), *[pltpu.HBM(b.shape, b.dtype) for b in bufs],
                   jax.ShapeDtypeStruct((8, 128), F32)),
        input_output_aliases={a: 2 + a for a in range(na)},
        compiler_params=SPLIT_COPY,
    )(*[_in_hbm(b) for b in bufs])
    return outs[0], outs[1], list(outs[2:2 + na]), outs[2 + na]


def gather_wait(send_sems, recv_sems, bufs, axes, after, tag):
    na = len(bufs)

    def body(*refs):
        ins = refs[:na]
        send_sems, recv_sems = refs[na], refs[na + 1]
        x, y, c = _place()
        for a in range(na):
            for j, chip in enumerate(_other_chips(x, y)):
                got = _half(_window(ins[a], axes[a], chip), axes[a], c)
                _rcopy(got, got, send_sems.at[3 * a + j], recv_sems.at[3 * a + j], (*chip, c)).wait_recv()
        for a in range(na):
            mine = _half(_window(ins[a], axes[a], (x, y)), axes[a], c)
            for j, chip in enumerate(_other_chips(x, y)):
                _rcopy(mine, mine, send_sems.at[3 * a + j], recv_sems.at[3 * a + j], (*chip, c)).wait_send()

    return pl.pallas_call(
        body, name="gather_wait_" + tag,
        in_specs=[HBM] * na + [SEM, SEM, ANY],
        out_specs=[HBM] * na,
        out_shape=[pltpu.HBM(b.shape, b.dtype) for b in bufs],
        input_output_aliases={a: a for a in range(na)},
        compiler_params=SPLIT_COPY,
    )(*bufs, send_sems, recv_sems, after)


def gather_forward(bufs, axes):
    na = len(bufs)

    def body(*refs):
        outs = refs[na:2 * na]
        send_sems, recv_sems = refs[2 * na:]
        x, y, c = _place()
        chips = _other_chips(x, y)
        passed = []
        for a in range(na):
            for j, chip in enumerate(chips):
                got = _half(_window(outs[a], axes[a], chip), axes[a], c)
                cp = _rcopy(got, got, send_sems.at[3 * a + j], recv_sems.at[3 * a + j], (x, y, 1 - c))
                cp.start()
                passed.append(cp)
        for a in range(na):
            for j, chip in enumerate(chips):
                got = _half(_window(outs[a], axes[a], chip), axes[a], 1 - c)
                _rcopy(got, got, send_sems.at[3 * a + j], recv_sems.at[3 * a + j], (x, y, 1 - c)).wait_recv()
        for cp in passed:
            cp.wait_send()

    dma = pltpu.SemaphoreType.DMA
    return pl.pallas_call(
        body, name="gather_forward",
        in_specs=[ANY] * na, out_specs=[ANY] * na,
        out_shape=[jax.ShapeDtypeStruct(b.shape, BF16) for b in bufs],
        input_output_aliases={a: a for a in range(na)},
        scratch_shapes=[dma((3 * na,)), dma((3 * na,))],
    )(*bufs)


def exchange_halves(grads, axes):
    na = len(grads)

    def hshape(g, ax):
        ha = 3 - ax
        return tuple(d // 2 if i == ha else d for i, d in enumerate(g.shape))

    def body(*refs):
        ins, outs = refs[:na], refs[na:2 * na]
        send_sems, recv_sems = refs[2 * na:]
        x, y, c = _place()
        cps = []
        for a in range(na):
            ha = 3 - axes[a]
            hs = ins[a].shape[ha] // 2
            cp = _rcopy(_sl(ins[a], ha, (1 - c) * hs, hs), outs[a], send_sems.at[a], recv_sems.at[a], (x, y, 1 - c))
            cp.start()
            cps.append(cp)
        for cp in cps:
            cp.wait_recv()
        for cp in cps:
            cp.wait_send()

    dma = pltpu.SemaphoreType.DMA
    return pl.pallas_call(
        body, name="exchange_halves",
        in_specs=[ANY] * na, out_specs=[ANY] * na,
        out_shape=[jax.ShapeDtypeStruct(hshape(g, ax), BF16) for g, ax in zip(grads, axes)],
        scratch_shapes=[dma((na,)), dma((na,))],
    )(*grads)


def scatter_start(halves, axes, tag):
    na = len(halves)

    def pshape(h, ax):
        return (N_CHIPS - 1,) + tuple(d // N_CHIPS if i == ax else d for i, d in enumerate(h.shape))

    def body(*refs):
        ins, lands = refs[:na], refs[na:2 * na]
        send_sems, recv_sems = refs[2 * na], refs[2 * na + 1]
        token = refs[4 * na + 2]
        x, y, c = _place()
        for a in range(na):
            for j, chip in enumerate(_other_chips(x, y)):
                _rcopy(_window(ins[a], axes[a], chip), lands[a].at[j],
                       send_sems.at[3 * a + j], recv_sems.at[3 * a + j], (*chip, c)).start()
        token[...] = jnp.zeros_like(token)

    dma = pltpu.SemaphoreType.DMA
    lands = [lax.empty(pshape(h, ax), BF16) for h, ax in zip(halves, axes)]
    outs = pl.pallas_call(
        body, name="scatter_start_" + tag,
        in_specs=[HBM] * (2 * na),
        out_specs=(SEM, SEM, *[HBM] * (2 * na), pl.BlockSpec(memory_space=pltpu.VMEM)),
        out_shape=(dma((3 * na,)), dma((3 * na,)), *[pltpu.HBM(b.shape, b.dtype) for b in halves + lands],
                   jax.ShapeDtypeStruct((8, 128), F32)),
        input_output_aliases={a: 2 + a for a in range(2 * na)},
        compiler_params=SPLIT_COPY,
    )(*[_in_hbm(b) for b in halves + lands])
    return outs[0], outs[1], list(outs[2:2 + na]), list(outs[2 + na:2 + 2 * na]), outs[2 + 2 * na]


def scatter_wait(send_sems, recv_sems, halves, lands, axes, after, tag):
    na = len(halves)

    def body(*refs):
        ins, lands = refs[:na], refs[na:2 * na]
        send_sems, recv_sems = refs[2 * na], refs[2 * na + 1]
        x, y, c = _place()
        for a in range(na):
            for j, chip in enumerate(_other_chips(x, y)):
                _rcopy(_window(ins[a], axes[a], chip), lands[a].at[j],
                       send_sems.at[3 * a + j], recv_sems.at[3 * a + j], (*chip, c)).wait_recv()
        for a in range(na):
            for j, chip in enumerate(_other_chips(x, y)):
                _rcopy(_window(ins[a], axes[a], chip), lands[a].at[j],
                       send_sems.at[3 * a + j], recv_sems.at[3 * a + j], (*chip, c)).wait_send()

    outs = pl.pallas_call(
        body, name="scatter_wait_" + tag,
        in_specs=[HBM] * (2 * na) + [SEM, SEM, ANY],
        out_specs=[HBM] * (2 * na),
        out_shape=[pltpu.HBM(b.shape, b.dtype) for b in halves + lands],
        input_output_aliases={a: a for a in range(2 * na)},
        compiler_params=SPLIT_COPY,
    )(*halves, *lands, send_sems, recv_sems, after)
    return list(outs[:na]), list(outs[na:])


def join_halves(gs, regions, axes):
    na = len(gs)

    def body(*refs):
        outs = refs[na:2 * na]
        send_sems, recv_sems = refs[2 * na:]
        x, y, c = _place()
        cps = []
        for a in range(na):
            ha = 3 - axes[a]
            hs = outs[a].shape[ha] // 2
            reg = outs[a].at[pl.ds(*regions[a])]
            mine = _sl(reg, ha, c * hs, hs)
            cp = _rcopy(mine, mine, send_sems.at[a], recv_sems.at[a], (x, y, 1 - c))
            cp.start()
            cps.append((cp, _sl(reg, ha, (1 - c) * hs, hs)))
        for a, (cp, theirs) in enumerate(cps):
            _rcopy(theirs, theirs, send_sems.at[a], recv_sems.at[a], (x, y, 1 - c)).wait_recv()
        for cp, _ in cps:
            cp.wait_send()

    dma = pltpu.SemaphoreType.DMA
    return pl.pallas_call(
        body, name="join_halves",
        in_specs=[ANY] * na, out_specs=[ANY] * na,
        out_shape=[jax.ShapeDtypeStruct(g.shape, F32) for g in gs],
        input_output_aliases={a: a for a in range(na)},
        scratch_shapes=[dma((na,)), dma((na,))],
    )(*gs)


def _tile2(r, c, itemsize, limit):
    bc = c
    while bc > 1536:
        bc //= 2
    assert c % bc == 0 and bc % 128 == 0
    br = r
    while br * bc * itemsize > limit and br % 32 == 0:
        br //= 2
    assert r % br == 0 and br % 16 == 0
    return br, bc


def add_my_half(g, theirs, ax, cc):
    ha = 3 - ax
    nl, r, c = theirs.shape
    br, bc = _tile2(r, c, 2, 1 << 20)
    nrb, ncb = r // br, c // bc

    def body(cc_ref, g_ref, t_ref, o_ref):
        o_ref[...] = (g_ref[...].astype(F32) + t_ref[...].astype(F32)).astype(BF16)

    def gmap(l, i, j, cc_ref):
        return (l, cc_ref[0] * nrb + i, j) if ha == 1 else (l, i, cc_ref[0] * ncb + j)

    blk = pl.BlockSpec((1, br, bc), lambda l, i, j, cc_ref: (l, i, j))
    return pl.pallas_call(
        body, name="add_my_half",
        grid_spec=pltpu.PrefetchScalarGridSpec(
            num_scalar_prefetch=1, grid=(nl, nrb, ncb),
            in_specs=[pl.BlockSpec((1, br, bc), gmap), blk], out_specs=blk),
        out_shape=jax.ShapeDtypeStruct(theirs.shape, BF16),
        compiler_params=_cparams(32, 3),
    )(cc, g, theirs)


def sum_chips(parts, pair, gstack, l0, ax, where):
    _, n, r, c = parts.shape
    ha = 3 - ax
    br, bc = _tile2(r, c, 2, 1 << 19)
    nrb, ncb = r // br, c // bc

    def body(w_ref, p_ref, own_ref, g_ref, o_ref):
        acc = own_ref[...].astype(F32)
        for q in range(N_CHIPS - 1):
            acc = acc + p_ref[q].astype(F32)
        o_ref[...] = acc

    def own_map(l, i, j, w_ref):
        return (l, w_ref[0] * nrb + i, j) if ax == 1 else (l, i, w_ref[0] * ncb + j)

    def out_map(l, i, j, w_ref):
        return (l0 + l, w_ref[1] * nrb + i, j) if ha == 1 else (l0 + l, i, w_ref[1] * ncb + j)

    return pl.pallas_call(
        body, name="sum_chips",
        grid_spec=pltpu.PrefetchScalarGridSpec(
            num_scalar_prefetch=1, grid=(n, nrb, ncb),
            in_specs=[pl.BlockSpec((N_CHIPS - 1, 1, br, bc), lambda l, i, j, w_ref: (0, l, i, j)),
                      pl.BlockSpec((1, br, bc), own_map), ANY],
            out_specs=pl.BlockSpec((1, br, bc), out_map)),
        out_shape=jax.ShapeDtypeStruct(gstack.shape, F32),
        input_output_aliases={3: 0},
        compiler_params=_cparams(32, 3),
    )(where, parts, pair, gstack)


BIG = ("w_ffn_up", "w_ffn_down", "attn_w_qkv", "attn_w_o", "conv_w_in", "conv_w_out")
BIG_AXIS = {"w_ffn_up": 2, "w_ffn_down": 1, "attn_w_qkv": 2, "attn_w_o": 1, "conv_w_in": 2, "conv_w_out": 1}
WEIGHTS = ("norm_gain", "w_ada", "b_ada", "w_ffn_up", "w_ffn_down", "attn_w_qkv", "attn_b_qkv", "attn_q_gain",
           "attn_k_gain", "attn_sinks", "attn_w_o", "attn_b_o", "conv_w_in", "conv_w", "conv_w_out")
N_SMALL_ROWS = 112


def _stack3(a):
    return a.reshape((-1,) + a.shape[-2:])


def _head_matrices():
    lane = jnp.arange(QK_DIM)
    head = lane // HEAD_DIM
    col = jnp.arange(128)
    g1 = jnp.where(head[:, None] == col[None, :], 1.0 / HEAD_DIM, 0.0).astype(BF16)
    g2 = jnp.where(col[:, None] == head[None, :], 1.0, 0.0).astype(BF16)
    fold = lane % HEAD_DIM + jnp.where(head >= N_HEADS, HEAD_DIM, 0)
    gsel = jnp.where(fold[:, None] == col[None, :], 1.0, 0.0).astype(BF16)
    return g1, g2, gsel


def _pad_cols(a, n):
    return jnp.pad(a, ((0, 0), (0, n - a.shape[1])))


def kernel(x, c, positions, norm_gain, w_ada, b_ada, w_ffn_up, w_ffn_down, attn_w_qkv, attn_b_qkv, attn_q_gain, attn_k_gain, attn_sinks, attn_w_o, attn_b_o, conv_w_in, conv_w, conv_w_out, loss_target, m_norm_gain, m_w_ada, m_b_ada, m_w_ffn_up, m_w_ffn_down, m_attn_w_qkv, m_attn_b_qkv, m_attn_q_gain, m_attn_k_gain, m_attn_sinks, m_attn_w_o, m_attn_b_o, m_conv_w_in, m_conv_w, m_conv_w_out, v_norm_gain, v_w_ada, v_b_ada, v_w_ffn_up, v_w_ffn_down, v_attn_w_qkv, v_attn_b_qkv, v_attn_q_gain, v_attn_k_gain, v_attn_sinks, v_attn_w_o, v_attn_b_o, v_conv_w_in, v_conv_w, v_conv_w_out):
    w = dict(norm_gain=norm_gain, w_ada=w_ada, b_ada=b_ada, w_ffn_up=w_ffn_up, w_ffn_down=w_ffn_down,
             attn_w_qkv=attn_w_qkv, attn_b_qkv=attn_b_qkv, attn_q_gain=attn_q_gain, attn_k_gain=attn_k_gain,
             attn_sinks=attn_sinks, attn_w_o=attn_w_o, attn_b_o=attn_b_o, conv_w_in=conv_w_in, conv_w=conv_w,
             conv_w_out=conv_w_out)
    mom = dict(norm_gain=m_norm_gain, w_ada=m_w_ada, b_ada=m_b_ada, w_ffn_up=m_w_ffn_up, w_ffn_down=m_w_ffn_down,
               attn_w_qkv=m_attn_w_qkv, attn_b_qkv=m_attn_b_qkv, attn_q_gain=m_attn_q_gain,
               attn_k_gain=m_attn_k_gain, attn_sinks=m_attn_sinks, attn_w_o=m_attn_w_o, attn_b_o=m_attn_b_o,
               conv_w_in=m_conv_w_in, conv_w=m_conv_w, conv_w_out=m_conv_w_out)
    var = dict(norm_gain=v_norm_gain, w_ada=v_w_ada, b_ada=v_b_ada, w_ffn_up=v_w_ffn_up, w_ffn_down=v_w_ffn_down,
               attn_w_qkv=v_attn_w_qkv, attn_b_qkv=v_attn_b_qkv, attn_q_gain=v_attn_q_gain,
               attn_k_gain=v_attn_k_gain, attn_sinks=v_attn_sinks, attn_w_o=v_attn_w_o, attn_b_o=v_attn_b_o,
               conv_w_in=v_conv_w_in, conv_w=v_conv_w, conv_w_out=v_conv_w_out)

    xi, yi, ci = _place()
    chip = 2 * xi + yi
    dev = 4 * xi + 2 * yi + ci
    nex, s_len, _ = x.shape
    t = nex * s_len
    n_attn, n_conv = attn_w_qkv.shape[0], conv_w_in.shape[0]
    axes = [BIG_AXIS[n] for n in BIG]

    c_all = allgather_small(jnp.pad(c, ((0, 8 - nex), (0, 0))), True)[:, :nex].reshape(8 * nex, D)
    ada_cols = w_ada.shape[2]
    modp = ada_fwd(c_all, w_ada)
    modg = allgather_small(modp.reshape(DEPTH * 8 * nex, ada_cols), False)
    modg = lax.dynamic_slice_in_dim(modg.reshape(N_CHIPS, DEPTH, 8 * nex, ada_cols), dev * nex, nex, axis=2)
    modg = modg.transpose(1, 2, 0, 3).reshape(DEPTH, nex, 9 * D)
    mod = add_bias(modg, b_ada.reshape(DEPTH, 1, 9 * D)).reshape(DEPTH, nex, 9, D)

    small = jnp.concatenate([norm_gain.reshape(DEPTH * 3, -1), conv_w.reshape(n_conv * 3, -1)], axis=0)
    small = jnp.pad(small, ((0, -small.shape[0] % 8), (0, 0)))
    small = allgather_small(small, False).transpose(1, 0, 2).reshape(small.shape[0], D)
    gain_full = small[:DEPTH * 3].reshape(DEPTH, 3, D)
    convw_full = small[DEPTH * 3:DEPTH * 3 + n_conv * 3].reshape(n_conv, 3, D)

    chip_arr = chip.reshape(1).astype(jnp.int32)
    where = jnp.stack([chip, ci]).astype(jnp.int32)
    stacks = [_stack3(w[n]) for n in BIG]

    def groups(i):
        mix = (2, 3) if i % 2 == 0 else (4, 5)
        return [(0, 2 * i, 2), (1, 2 * i, 2), (mix[0], i // 2, 1), (mix[1], i // 2, 1)]

    lax4 = [2, 1, 2, 1]
    in_flight, token = [], ()
    for i in range(DEPTH):
        bufs = [cast_into_window(stacks[b], l0, n, axes[b], chip_arr, token) for b, l0, n in groups(i)]
        ssem, rsem, bufs, tok = gather_start(bufs, lax4, f"l{i}")
        in_flight.append((ssem, rsem, bufs))
        token = (tok,)

    invf = ROPE_THETA ** (-jnp.arange(0, HEAD_DIM, 2, dtype=F32) / HEAD_DIM)
    cos, sin = rope_tables(positions.reshape(t, 1), jnp.tile(invf, 4).reshape(1, 128))
    g1, g2, gsel = _head_matrices()
    gqk = [jnp.concatenate([jnp.tile(attn_q_gain[j], N_HEADS), jnp.tile(attn_k_gain[j], N_KV)]).reshape(1, QK_DIM)
           for j in range(n_attn)]
    zero_bias = jnp.zeros((1, D), F32)

    xs = x.reshape(t, D)
    saved, weights = [], []
    for i in range(DEPTH):
        j = i // 2
        gn, md = gain_full[i], mod[i]
        ssem, rsem, bufs = in_flight[i]
        bufs = gather_wait(ssem, rsem, bufs, lax4, token[0] if i == 0 else xs, f"l{i}")
        wup, wdn, wmi, wmo = gather_forward(bufs, lax4)
        weights.append((wup, wdn, wmi, wmo))
        x0 = xs
        xs, u1, f1 = ffn_fwd(x0, gn[0:1], md, wup, wdn, 0, 0)
        x1 = xs
        if i % 2 == 0:
            raw, qr, kr = qkv_fwd(x1, gn[1:2], md, wmi, 0, attn_b_qkv[j:j + 1], gqk[j], cos, sin, g1, g2)
            o, lse = attn_fwd(qr, kr, raw, attn_sinks[j:j + 1])
            xs, ymix = proj_res(x1, o, wmo, 0, attn_b_o[j:j + 1], md)
            mix = (raw, qr, kr, o, lse)
        else:
            z = lin_fwd(x1, gn[1:2], md, wmi, 0, 1)
            p = conv_fwd(z, convw_full[j])
            xs, ymix = proj_res(x1, p, wmo, 0, zero_bias, md)
            mix = (z, p)
        x2 = xs
        xs, u3, f3 = ffn_fwd(x2, gn[2:3], md, wup, wdn, 1, 2)
        saved.append((x0, u1, f1, x1, mix, ymix, x2, u3, f3))
    dy, lpart = loss_grad(xs, loss_target.reshape(t, D))

    cc = ci.reshape(1).astype(jnp.int32)
    gshard = [lax.empty(s.shape, F32) for s in stacks]
    upd = [[lax.empty(s.shape, F32) for _ in range(3)] for s in stacks]
    mstacks = [_stack3(mom[n]) for n in BIG]
    vstacks = [_stack3(var[n]) for n in BIG]

    def finish(i, flight, after):
        ssem, rsem, pair, lands = flight
        pair, lands = scatter_wait(ssem, rsem, pair, lands, lax4, after, f"l{i}")
        grp = groups(i)
        for a, (b, l0, n) in enumerate(grp):
            gshard[b] = sum_chips(lands[a], pair[a], gshard[b], l0, axes[b], where)
        joined = join_halves([gshard[b] for b, _, _ in grp], [(l0, n) for _, l0, n in grp], lax4)
        for (b, l0, n), g in zip(grp, joined):
            gshard[b] = g
            upd[b] = adamw_layers(stacks[b], g, mstacks[b], vstacks[b], upd[b], l0, n)

    dmod = [None] * DEPTH
    dgain = [None] * DEPTH
    db_qkv, dqk_gain, dsinks, db_o, dconv_w = ([None] * n_attn, [None] * n_attn, [None] * n_attn,
                                                [None] * n_attn, [None] * n_conv)
    flight, token = None, ()
    for i in reversed(range(DEPTH)):
        j = i // 2
        gn, md = gain_full[i], mod[i]
        wup, wdn, wmi, wmo = weights[i]
        x0, u1, f1, x1, mix, ymix, x2, u3, f3 = saved[i]
        gup, gdn, gmi, gmo = (lax.empty(a.shape, BF16) for a in weights[i])
        du, a, df, sg3 = ffn_bwd_act(dy, u3, f3, md, wdn, 1, 2, after=token)
        dy, h, se3, sa3 = lin_bwd(dy, du, wup, 1, x2, gn[2:3], md, 2, False)
        gup = wgrad(gup, 1, h, du, D, FF_CHUNK)
        gdn = wgrad(gdn, 1, a, df, FF_CHUNK, D)
        if i % 2 == 0:
            raw, qr, kr, o, lse = mix
            dyy, do, sp, sb = proj_res_bwd(dy, ymix, wmo, 0, md)
            gmo = wgrad(gmo, 0, o, dyy, D, D)
            dq, dk, dv, dsinks[j] = attn_bwd(qr, kr, raw, attn_sinks[j:j + 1], o, do, lse)
            dz, dqk_gain[j] = qkv_bwd_pre(dq, dk, dv, raw, gqk[j], cos, sin, g1, g2, gsel)
            dy, h, se2, sa2, db_qkv[j] = lin_bwd(dy, dz, wmi, 0, x1, gn[1:2], md, 1, True)
            gmi = wgrad(gmi, 0, h, dz, D, QKV_DIM)
            db_o[j] = sb
        else:
            z, p = mix
            dyy, dp, sp, _ = proj_res_bwd(dy, ymix, wmo, 0, md)
            gmo = wgrad(gmo, 0, p, dyy, D, D)
            dgb, dgc, dvv, dconv_w[j] = conv_bwd(z, convw_full[j], dp)
            dz = jnp.concatenate([dgb, dgc, dvv], axis=1)
            dy, h, se2, sa2 = lin_bwd(dy, dz, wmi, 0, x1, gn[1:2], md, 1, False)
            gmi = wgrad(gmi, 0, h, dz, D, 1536)
        du, a, df, sg1 = ffn_bwd_act(dy, u1, f1, md, wdn, 0, 0)
        dy, h, se1, sa1 = lin_bwd(dy, du, wup, 0, x0, gn[0:1], md, 0, False)
        gup = wgrad(gup, 0, h, du, D, FF_CHUNK)
        gdn = wgrad(gdn, 0, a, df, FF_CHUNK, D)
        dmod[i] = jnp.stack([se1[:, 0], se1[:, 1], sg1[:, 0], se2[:, 0], se2[:, 1], sp[:, 0],
                             se3[:, 0], se3[:, 1], sg3[:, 0]], axis=1)
        dgain[i] = jnp.stack([sa1[0], sa2[0], sa3[0]], axis=0)
        glayer = [gup, gdn, gmi, gmo]
        theirs = exchange_halves(glayer, lax4)
        pair = [add_my_half(g, th, ax, cc) for g, th, ax in zip(glayer, theirs, lax4)]
        ssem, rsem, pair, lands, tok = scatter_start(pair, lax4, f"l{i}")
        if flight is not None:
            finish(i + 1, flight, tok)
        flight, token = (ssem, rsem, pair, lands), (tok,)
    grad_x = dy.reshape(x.shape)

    dmod_ex = jnp.stack(dmod, axis=1).reshape(nex, DEPTH * 9, D)
    dmod_ex = jnp.pad(dmod_ex, ((0, 0), (0, N_DMOD_ROWS - DEPTH * 9), (0, 0))).reshape(nex * N_DMOD_ROWS, D)
    misc = jnp.concatenate([dqk_gain[jj][0] for jj in range(n_attn)]
                           + [jnp.pad(dsinks[jj][0], (0, 128 - N_HEADS)) for jj in range(n_attn)]
                           + [lpart[0]])
    rows = [dmod_ex,
            jnp.concatenate(dgain, axis=0), jnp.zeros((4, D), F32),
            jnp.concatenate([_pad_cols(db_qkv[jj][0:1], 2 * D).reshape(2, D) for jj in range(n_attn)], axis=0),
            jnp.concatenate([db_o[jj][0:1] for jj in range(n_attn)], axis=0),
            jnp.concatenate([dconv_w[jj][0:3] for jj in range(n_conv)], axis=0),
            jnp.pad(misc, (0, D - misc.shape[0])).reshape(1, D)]
    packed = jnp.concatenate(rows, axis=0)
    packed = jnp.pad(packed, ((0, N_SMALL_ROWS - packed.shape[0]), (0, 0)))
    p_all = allgather_small(packed, True)
    red, exsum = reduce_small(p_all)

    r0 = nex * N_DMOD_ROWS
    grads = {}
    grads["b_ada"] = exsum[:DEPTH * 9].reshape(DEPTH, 9 * D)
    grads["norm_gain"] = lax.dynamic_slice_in_dim(red[r0:r0 + 12].reshape(DEPTH, 3, D), chip * (D // N_CHIPS),
                                                  D // N_CHIPS, axis=2)
    r1 = r0 + 16
    grads["attn_b_qkv"] = red[r1:r1 + 2 * n_attn].reshape(n_attn, 2 * D)[:, :QKV_DIM]
    r2 = r1 + 2 * n_attn
    grads["attn_b_o"] = red[r2:r2 + n_attn]
    r3 = r2 + n_attn
    grads["conv_w"] = lax.dynamic_slice_in_dim(red[r3:r3 + 3 * n_conv].reshape(n_conv, 3, D), chip * (D // N_CHIPS),
                                               D // N_CHIPS, axis=2)
    mrow = red[r3 + 3 * n_conv]
    grads["attn_q_gain"] = jnp.stack([mrow[128 * jj:128 * jj + HEAD_DIM] for jj in range(n_attn)])
    grads["attn_k_gain"] = jnp.stack([mrow[128 * jj + HEAD_DIM:128 * jj + 128] for jj in range(n_attn)])
    grads["attn_sinks"] = jnp.stack([mrow[128 * (n_attn + jj):128 * (n_attn + jj) + N_HEADS] for jj in range(n_attn)])
    loss = mrow[128 * 2 * n_attn]

    dm_all = p_all[:, :r0].reshape(8, nex, N_DMOD_ROWS, D)[:, :, :DEPTH * 9].reshape(8 * nex, DEPTH, 9 * D)
    dm_mine = lax.dynamic_slice_in_dim(dm_all.transpose(1, 0, 2), chip * ada_cols, ada_cols, axis=2)
    g_ada, d_ada, nm_ada, nv_ada = ada_bwd_adam(c_all, dm_mine, w_ada, m_w_ada, v_w_ada)

    delta, new_m, new_v = {}, {}, {}
    for n in WEIGHTS:
        if n == "w_ada":
            grads[n], delta[n], new_m[n], new_v[n] = g_ada, d_ada, nm_ada, nv_ada
        elif n not in BIG:
            delta[n], new_m[n], new_v[n] = adamw(w[n], grads[n], mom[n], var[n])
    finish(0, flight, delta["conv_w"])
    for b, n in enumerate(BIG):
        grads[n] = gshard[b].reshape(w[n].shape)
        delta[n], new_m[n], new_v[n] = (u.reshape(w[n].shape) for u in upd[b])

    return (loss, grad_x, *[grads[n] for n in WEIGHTS], *[delta[n] for n in WEIGHTS],
            *[new_m[n] for n in WEIGHTS], *[new_v[n] for n in WEIGHTS])
```

```python
import functools

import jax
import jax.numpy as jnp
from jax import lax
from jax.experimental import pallas as pl
from jax.experimental.pallas import tpu as pltpu

F32 = jnp.float32
BF16 = jnp.bfloat16

D = 1024
D_FF = 2816
N_HEADS = 16
N_KV = 4
HEAD_DIM = 64
GROUP = N_HEADS // N_KV
QK_DIM = (N_HEADS + N_KV) * HEAD_DIM
QKV_DIM = QK_DIM + N_KV * HEAD_DIM
BLOCK = 128
ROPE_THETA = 10000.0
EPS = 1e-6
DEPTH = 4
N_CHIPS = 4

ADAM_LR = 0.001
ADAM_B1 = 0.9
ADAM_B2 = 0.999
ADAM_EPS = 1e-08
ADAM_WD = 0.01
ADAM_STEP = 10

V7X_VMEM_BYTES = 64 * 1024 * 1024
FF_CHUNK = 1408
MESH = pl.DeviceIdType.MESH
ANY = pl.BlockSpec(memory_space=pl.ANY)


def _cparams(vmem_mb, n_grid):
    assert vmem_mb * 1024 * 1024 <= V7X_VMEM_BYTES
    return pltpu.CompilerParams(vmem_limit_bytes=vmem_mb * 1024 * 1024,
                                dimension_semantics=("arbitrary",) * n_grid)


def _resident(shape):
    nd = len(shape)
    return pl.BlockSpec(shape, lambda *_: (0,) * nd, pipeline_mode=pl.Buffered(1))


def _layer(w, l):
    return pl.BlockSpec((None,) + w.shape[1:], lambda *_: (l, 0, 0), pipeline_mode=pl.Buffered(1))


def _dot(a, b):
    return jnp.dot(a, b, preferred_element_type=F32)


def _dot_nt(a, b):
    return lax.dot_general(a, b, (((1,), (1,)), ((), ())), preferred_element_type=F32)


def _dot_tn(a, b):
    return lax.dot_general(a, b, (((0,), (0,)), ((), ())), preferred_element_type=F32)


def _dot_hilo(a, g):
    hi = a.astype(BF16)
    lo = (a - hi.astype(F32)).astype(BF16)
    return _dot(hi, g) + _dot(lo, g)


def _colsum(a):
    return jnp.sum(a, axis=0, keepdims=True)


def _norm_mod(x, gain, sc, sh):
    r = lax.rsqrt(jnp.mean(x * x, axis=-1, keepdims=True) + EPS)
    n = x * r * gain
    return r, n, n * (1.0 + sc) + sh


def _mod_rows(mod_ref, s):
    return (mod_ref[0, 3 * s:3 * s + 1, :], mod_ref[0, 3 * s + 1:3 * s + 2, :], mod_ref[0, 3 * s + 2:3 * s + 3, :])


def ffn_fwd(x, gain, mod, wup, wdn, l, s, tm=256, after=()):
    t = x.shape[0]
    tpe = t // tm // 2

    def body(x_ref, gain_ref, mod_ref, wup_ref, wdn_ref, *rest):
        xo_ref, u_ref, f_ref = rest[len(after):]
        xv = x_ref[...]
        sh, sc, g = _mod_rows(mod_ref, s)
        _, _, h = _norm_mod(xv, gain_ref[...], sc, sh)
        hb = h.astype(BF16)
        acc = jnp.zeros((tm, D), F32)
        for j in range(D_FF // FF_CHUNK):
            lo, hi = j * FF_CHUNK, (j + 1) * FF_CHUNK
            gate = _dot(hb, wup_ref[:, lo:hi])
            up = _dot(hb, wup_ref[:, D_FF + lo:D_FF + hi])
            u_ref[:, lo:hi] = gate.astype(BF16)
            u_ref[:, D_FF + lo:D_FF + hi] = up.astype(BF16)
            a = (gate * jax.nn.sigmoid(gate) * up).astype(BF16)
            acc = acc + _dot(a, wdn_ref[lo:hi, :])
        f_ref[...] = acc.astype(BF16)
        xo_ref[...] = xv + 0.5 * g * acc

    return pl.pallas_call(
        body, name="ffn_fwd", grid=(t // tm,),
        in_specs=[pl.BlockSpec((tm, D), lambda i: (i, 0)),
                  pl.BlockSpec((1, D), lambda i: (0, 0)),
                  pl.BlockSpec((1, 9, D), lambda i: (i // tpe, 0, 0)),
                  _layer(wup, l), _layer(wdn, l)] + [ANY] * len(after),
        out_specs=[pl.BlockSpec((tm, D), lambda i: (i, 0)),
                   pl.BlockSpec((tm, 2 * D_FF), lambda i: (i, 0)),
                   pl.BlockSpec((tm, D), lambda i: (i, 0))],
        out_shape=[jax.ShapeDtypeStruct((t, D), F32), jax.ShapeDtypeStruct((t, 2 * D_FF), BF16),
                   jax.ShapeDtypeStruct((t, D), BF16)],
        compiler_params=_cparams(48, 1),
    )(x, gain, mod, wup, wdn, *after)


def ffn_bwd_act(dy, u, f, mod, wdn, l, s, tm=256, after=()):
    t = dy.shape[0]
    tpe = t // tm // 2

    def body(dy_ref, u_ref, f_ref, mod_ref, wdn_ref, *rest):
        du_ref, a_ref, df_ref, sg_ref = rest[len(after):]
        i = pl.program_id(0)
        dyv = dy_ref[...]
        _, _, g = _mod_rows(mod_ref, s)
        dfb = (0.5 * g * dyv).astype(BF16)
        df_ref[...] = dfb

        @pl.when(i % tpe == 0)
        def _():
            sg_ref[...] = jnp.zeros_like(sg_ref)

        sg_ref[0, 0:1, :] += _colsum(0.5 * dyv * f_ref[...].astype(F32))
        for j in range(D_FF // FF_CHUNK):
            lo, hi = j * FF_CHUNK, (j + 1) * FF_CHUNK
            da = _dot_nt(dfb, wdn_ref[lo:hi, :])
            gate = u_ref[:, lo:hi].astype(F32)
            up = u_ref[:, D_FF + lo:D_FF + hi].astype(F32)
            sg = jax.nn.sigmoid(gate)
            silu = gate * sg
            a_ref[:, lo:hi] = (silu * up).astype(BF16)
            du_ref[:, lo:hi] = (da * up * (sg * (1.0 + gate * (1.0 - sg)))).astype(BF16)
            du_ref[:, D_FF + lo:D_FF + hi] = (da * silu).astype(BF16)

    return pl.pallas_call(
        body, name="ffn_bwd_act", grid=(t // tm,),
        in_specs=[pl.BlockSpec((tm, D), lambda i: (i, 0)),
                  pl.BlockSpec((tm, 2 * D_FF), lambda i: (i, 0)),
                  pl.BlockSpec((tm, D), lambda i: (i, 0)),
                  pl.BlockSpec((1, 9, D), lambda i: (i // tpe, 0, 0)),
                  _layer(wdn, l)] + [ANY] * len(after),
        out_specs=[pl.BlockSpec((tm, 2 * D_FF), lambda i: (i, 0)),
                   pl.BlockSpec((tm, D_FF), lambda i: (i, 0)),
                   pl.BlockSpec((tm, D), lambda i: (i, 0)),
                   pl.BlockSpec((1, 8, D), lambda i: (i // tpe, 0, 0))],
        out_shape=[jax.ShapeDtypeStruct((t, 2 * D_FF), BF16), jax.ShapeDtypeStruct((t, D_FF), BF16),
                   jax.ShapeDtypeStruct((t, D), BF16), jax.ShapeDtypeStruct((2, 8, D), F32)],
        compiler_params=_cparams(40, 1),
    )(dy, u, f, mod, wdn, *after)


def lin_bwd(dy, dz, w, l, x, gain, mod, s, want_db, tm=256):
    t = dy.shape[0]
    n = w.shape[2]
    tpe = t // tm // 2
    nck = -(-n // 1536)
    ck = n // nck

    def body(dy_ref, dz_ref, w_ref, x_ref, gain_ref, mod_ref, dx_ref, h_ref, se_ref, sa_ref, *db_ref):
        i = pl.program_id(0)
        xv = x_ref[...]
        gain_v = gain_ref[...]
        sh, sc, _ = _mod_rows(mod_ref, s)
        r, nrm, h = _norm_mod(xv, gain_v, sc, sh)
        h_ref[...] = h.astype(BF16)
        dh = jnp.zeros((tm, D), F32)
        for j in range(nck):
            dh = dh + _dot_nt(dz_ref[:, j * ck:(j + 1) * ck], w_ref[:, j * ck:(j + 1) * ck])
        dn = dh * (1.0 + sc)
        dxr = dn * gain_v
        m = jnp.mean(dxr * xv, axis=-1, keepdims=True)
        dx_ref[...] = dy_ref[...] + r * dxr - xv * (r * r * r) * m

        @pl.when(i % tpe == 0)
        def _():
            se_ref[...] = jnp.zeros_like(se_ref)

        @pl.when(i == 0)
        def _():
            sa_ref[...] = jnp.zeros_like(sa_ref)
            if want_db:
                db_ref[0][...] = jnp.zeros_like(db_ref[0])

        se_ref[0, 0:1, :] += _colsum(dh)
        se_ref[0, 1:2, :] += _colsum(dh * nrm)
        sa_ref[0:1, :] += _colsum(dn * xv * r)
        if want_db:
            db_ref[0][0:1, :] += _colsum(dz_ref[...].astype(F32))

    out_specs = [pl.BlockSpec((tm, D), lambda i: (i, 0)), pl.BlockSpec((tm, D), lambda i: (i, 0)),
                 pl.BlockSpec((1, 8, D), lambda i: (i // tpe, 0, 0)), pl.BlockSpec((8, D), lambda i: (0, 0))]
    out_shape = [jax.ShapeDtypeStruct((t, D), F32), jax.ShapeDtypeStruct((t, D), BF16),
                 jax.ShapeDtypeStruct((2, 8, D), F32), jax.ShapeDtypeStruct((8, D), F32)]
    if want_db:
        out_specs.append(pl.BlockSpec((8, n), lambda i: (0, 0)))
        out_shape.append(jax.ShapeDtypeStruct((8, n), F32))
    return pl.pallas_call(
        body, name="lin_bwd", grid=(t // tm,),
        in_specs=[pl.BlockSpec((tm, D), lambda i: (i, 0)),
                  pl.BlockSpec((tm, n), lambda i: (i, 0)),
                  _layer(w, l),
                  pl.BlockSpec((tm, D), lambda i: (i, 0)),
                  pl.BlockSpec((1, D), lambda i: (0, 0)),
                  pl.BlockSpec((1, 9, D), lambda i: (i // tpe, 0, 0))],
        out_specs=out_specs, out_shape=out_shape,
        compiler_params=_cparams(40, 1),
    )(dy, dz, w, x, gain, mod)


def wgrad(gstack, l, a, b, bm, bn, bt=512):
    t, m = a.shape
    n = b.shape[1]
    nt = t // bt

    def body(g_ref, a_ref, b_ref, o_ref, acc_ref):
        k = pl.program_id(2)

        @pl.when(k == 0)
        def _():
            acc_ref[...] = jnp.zeros_like(acc_ref)

        acc_ref[...] += _dot_tn(a_ref[...], b_ref[...])

        @pl.when(k == nt - 1)
        def _():
            o_ref[...] = acc_ref[...].astype(BF16)

    return pl.pallas_call(
        body, name="wgrad", grid=(m // bm, n // bn, nt),
        in_specs=[ANY, pl.BlockSpec((bt, bm), lambda i, j, k: (k, i)),
                  pl.BlockSpec((bt, bn), lambda i, j, k: (k, j))],
        out_specs=pl.BlockSpec((None, bm, bn), lambda i, j, k: (l, i, j)),
        out_shape=jax.ShapeDtypeStruct(gstack.shape, BF16),
        input_output_aliases={0: 0},
        scratch_shapes=[pltpu.VMEM((bm, bn), F32)],
        compiler_params=_cparams(40, 3),
    )(gstack, a, b)


def proj_res(x, o, w, l, b, mod, tm=512):
    t = x.shape[0]
    tpe = t // tm // 2

    def body(x_ref, o_ref, w_ref, b_ref, mod_ref, xo_ref, y_ref):
        _, _, g = _mod_rows(mod_ref, 1)
        y = _dot(o_ref[...], w_ref[...]) + b_ref[...]
        y_ref[...] = y.astype(BF16)
        xo_ref[...] = x_ref[...] + g * y

    return pl.pallas_call(
        body, name="proj_res", grid=(t // tm,),
        in_specs=[pl.BlockSpec((tm, D), lambda i: (i, 0)), pl.BlockSpec((tm, D), lambda i: (i, 0)),
                  _layer(w, l), pl.BlockSpec((1, D), lambda i: (0, 0)),
                  pl.BlockSpec((1, 9, D), lambda i: (i // tpe, 0, 0))],
        out_specs=[pl.BlockSpec((tm, D), lambda i: (i, 0)), pl.BlockSpec((tm, D), lambda i: (i, 0))],
        out_shape=[jax.ShapeDtypeStruct((t, D), F32), jax.ShapeDtypeStruct((t, D), BF16)],
        compiler_params=_cparams(32, 1),
    )(x, o, w, b, mod)


def proj_res_bwd(dy, y, w, l, mod, tm=512):
    t = dy.shape[0]
    tpe = t // tm // 2

    def body(dy_ref, y_ref, w_ref, mod_ref, dyy_ref, do_ref, se_ref, sa_ref):
        i = pl.program_id(0)
        _, _, g = _mod_rows(mod_ref, 1)
        dyv = dy_ref[...]
        dyy = g * dyv
        dyb = dyy.astype(BF16)
        dyy_ref[...] = dyb
        do_ref[...] = _dot_nt(dyb, w_ref[...]).astype(BF16)

        @pl.when(i % tpe == 0)
        def _():
            se_ref[...] = jnp.zeros_like(se_ref)

        @pl.when(i == 0)
        def _():
            sa_ref[...] = jnp.zeros_like(sa_ref)

        se_ref[0, 0:1, :] += _colsum(dyv * y_ref[...].astype(F32))
        sa_ref[0:1, :] += _colsum(dyy)

    return pl.pallas_call(
        body, name="proj_res_bwd", grid=(t // tm,),
        in_specs=[pl.BlockSpec((tm, D), lambda i: (i, 0)), pl.BlockSpec((tm, D), lambda i: (i, 0)),
                  _layer(w, l), pl.BlockSpec((1, 9, D), lambda i: (i // tpe, 0, 0))],
        out_specs=[pl.BlockSpec((tm, D), lambda i: (i, 0)), pl.BlockSpec((tm, D), lambda i: (i, 0)),
                   pl.BlockSpec((1, 8, D), lambda i: (i // tpe, 0, 0)), pl.BlockSpec((8, D), lambda i: (0, 0))],
        out_shape=[jax.ShapeDtypeStruct((t, D), BF16), jax.ShapeDtypeStruct((t, D), BF16),
                   jax.ShapeDtypeStruct((2, 8, D), F32), jax.ShapeDtypeStruct((8, D), F32)],
        compiler_params=_cparams(32, 1),
    )(dy, y, w, mod)


def lin_fwd(x, gain, mod, w, l, s, tm=512):
    t = x.shape[0]
    n = w.shape[2]
    tpe = t // tm // 2

    def body(x_ref, gain_ref, mod_ref, w_ref, z_ref):
        sh, sc, _ = _mod_rows(mod_ref, s)
        _, _, h = _norm_mod(x_ref[...], gain_ref[...], sc, sh)
        z_ref[...] = _dot(h.astype(BF16), w_ref[...]).astype(BF16)

    return pl.pallas_call(
        body, name="lin_fwd", grid=(t // tm,),
        in_specs=[pl.BlockSpec((tm, D), lambda i: (i, 0)), pl.BlockSpec((1, D), lambda i: (0, 0)),
                  pl.BlockSpec((1, 9, D), lambda i: (i // tpe, 0, 0)), _layer(w, l)],
        out_specs=pl.BlockSpec((tm, n), lambda i: (i, 0)),
        out_shape=jax.ShapeDtypeStruct((t, n), BF16),
        compiler_params=_cparams(40, 1),
    )(x, gain, mod, w)


def rope_tables(pos, invf):
    t = pos.shape[0]
    tm = 1024

    def body(pos_ref, invf_ref, c_ref, s_ref):
        ang = pos_ref[...].astype(F32) * invf_ref[...]
        lane = lax.broadcasted_iota(jnp.int32, (tm, 128), 1)
        sign = jnp.where(lane % HEAD_DIM < HEAD_DIM // 2, -1.0, 1.0)
        c_ref[...] = jnp.cos(ang)
        s_ref[...] = sign * jnp.sin(ang)

    return pl.pallas_call(
        body, name="rope_tables", grid=(t // tm,),
        in_specs=[pl.BlockSpec((tm, 1), lambda i: (i, 0)), pl.BlockSpec((1, 128), lambda i: (0, 0))],
        out_specs=[pl.BlockSpec((tm, 128), lambda i: (i, 0))] * 2,
        out_shape=[jax.ShapeDtypeStruct((t, 128), F32)] * 2,
        compiler_params=_cparams(16, 1),
    )(pos, invf)


def _swap_halves(v):
    lane = lax.broadcasted_iota(jnp.int32, v.shape, 1)
    return jnp.where(lane % HEAD_DIM < HEAD_DIM // 2, pltpu.roll(v, 128 - HEAD_DIM // 2, 1), pltpu.roll(v, HEAD_DIM // 2, 1))


def _rope(v, cos, sin):
    return jnp.concatenate(
        [v[:, j:j + 128] * cos + _swap_halves(v[:, j:j + 128]) * sin for j in range(0, v.shape[1], 128)], axis=1)


def _rope_t(dv, cos, sin):
    return jnp.concatenate(
        [dv[:, j:j + 128] * cos + _swap_halves(dv[:, j:j + 128] * sin) for j in range(0, dv.shape[1], 128)], axis=1)


def _head_stats(qk, g1, g2):
    rinv = lax.rsqrt(_dot_hilo(qk * qk, g1) + EPS)
    return rinv, _dot_hilo(rinv, g2)


def qkv_fwd(x, gain, mod, w, l, b, gqk, cos, sin, g1, g2, tm=256):
    t = x.shape[0]
    tpe = t // tm // 2

    def body(x_ref, gain_ref, mod_ref, w_ref, b_ref, gqk_ref, c_ref, s_ref, g1_ref, g2_ref, raw_ref, q_ref, k_ref):
        sh, sc, _ = _mod_rows(mod_ref, 1)
        _, _, h = _norm_mod(x_ref[...], gain_ref[...], sc, sh)
        qkv = _dot(h.astype(BF16), w_ref[...]) + b_ref[...]
        raw_ref[...] = qkv.astype(BF16)
        qk = qkv[:, :QK_DIM]
        _, rb = _head_stats(qk, g1_ref[...], g2_ref[...])
        qr = _rope(qk * rb * gqk_ref[...], c_ref[...], s_ref[...])
        q_ref[...] = qr[:, :D].astype(BF16)
        k_ref[...] = qr[:, D:].astype(BF16)

    return pl.pallas_call(
        body, name="qkv_fwd", grid=(t // tm,),
        in_specs=[pl.BlockSpec((tm, D), lambda i: (i, 0)), pl.BlockSpec((1, D), lambda i: (0, 0)),
                  pl.BlockSpec((1, 9, D), lambda i: (i // tpe, 0, 0)), _layer(w, l),
                  pl.BlockSpec((1, QKV_DIM), lambda i: (0, 0)), pl.BlockSpec((1, QK_DIM), lambda i: (0, 0)),
                  pl.BlockSpec((tm, 128), lambda i: (i, 0)), pl.BlockSpec((tm, 128), lambda i: (i, 0)),
                  _resident((QK_DIM, 128)), _resident((128, QK_DIM))],
        out_specs=[pl.BlockSpec((tm, QKV_DIM), lambda i: (i, 0)), pl.BlockSpec((tm, D), lambda i: (i, 0)),
                   pl.BlockSpec((tm, N_KV * HEAD_DIM), lambda i: (i, 0))],
        out_shape=[jax.ShapeDtypeStruct((t, QKV_DIM), BF16), jax.ShapeDtypeStruct((t, D), BF16),
                   jax.ShapeDtypeStruct((t, N_KV * HEAD_DIM), BF16)],
        compiler_params=_cparams(40, 1),
    )(x, gain, mod, w, b, gqk, cos, sin, g1, g2)


def qkv_bwd_pre(dq, dk, dv, raw, gqk, cos, sin, g1, g2, gsel, tm=256):
    t = dq.shape[0]

    def body(dq_ref, dk_ref, dv_ref, raw_ref, gqk_ref, c_ref, s_ref, g1_ref, g2_ref, gsel_ref, dz_ref, sa_ref):
        i = pl.program_id(0)
        dqk = jnp.concatenate([dq_ref[...].astype(F32), dk_ref[...]], axis=1)
        dqn = _rope_t(dqk, c_ref[...], s_ref[...])
        qk = raw_ref[:, :QK_DIM].astype(F32)
        g1v, g2v = g1_ref[...], g2_ref[...]
        rinv, rb = _head_stats(qk, g1v, g2v)
        dgq = jnp.broadcast_to(_colsum(dqn * qk * rb), (8, QK_DIM))
        dyh = dqn * gqk_ref[...]
        mh = _dot_hilo(dyh * qk, g1v)
        mb = _dot_hilo(mh * rinv * rinv * rinv, g2v)
        dz_ref[:, :QK_DIM] = (rb * dyh - qk * mb).astype(BF16)
        dz_ref[:, QK_DIM:] = dv_ref[...].astype(BF16)

        @pl.when(i == 0)
        def _():
            sa_ref[...] = jnp.zeros_like(sa_ref)

        sa_ref[...] += _dot_hilo(dgq, gsel_ref[...])

    kvw = N_KV * HEAD_DIM
    return pl.pallas_call(
        body, name="qkv_bwd_pre", grid=(t // tm,),
        in_specs=[pl.BlockSpec((tm, D), lambda i: (i, 0)), pl.BlockSpec((tm, kvw), lambda i: (i, 0)),
                  pl.BlockSpec((tm, kvw), lambda i: (i, 0)), pl.BlockSpec((tm, QKV_DIM), lambda i: (i, 0)),
                  pl.BlockSpec((1, QK_DIM), lambda i: (0, 0)),
                  pl.BlockSpec((tm, 128), lambda i: (i, 0)), pl.BlockSpec((tm, 128), lambda i: (i, 0)),
                  _resident((QK_DIM, 128)), _resident((128, QK_DIM)), _resident((QK_DIM, 128))],
        out_specs=[pl.BlockSpec((tm, QKV_DIM), lambda i: (i, 0)), pl.BlockSpec((8, 128), lambda i: (0, 0))],
        out_shape=[jax.ShapeDtypeStruct((t, QKV_DIM), BF16), jax.ShapeDtypeStruct((8, 128), F32)],
        compiler_params=_cparams(40, 1),
    )(dq, dk, dv, raw, gqk, cos, sin, g1, g2, gsel)


def _band_mask(n):
    row = lax.broadcasted_iota(jnp.int32, (GROUP * BLOCK, 2 * BLOCK), 0) % BLOCK
    col = lax.broadcasted_iota(jnp.int32, (GROUP * BLOCK, 2 * BLOCK), 1)
    rel = row + BLOCK - col
    return (rel >= 0) & (rel < BLOCK) & ((col >= BLOCK) | (n > 0))


def _stack_heads(v, g):
    base = g * GROUP * HEAD_DIM
    return jnp.concatenate([v[:, base + j * HEAD_DIM:base + (j + 1) * HEAD_DIM] for j in range(GROUP)], axis=0)


def _kv_cat(prev, cur, g):
    return jnp.concatenate([prev[:, g * HEAD_DIM:(g + 1) * HEAD_DIM], cur[:, g * HEAD_DIM:(g + 1) * HEAD_DIM]], axis=0)


def _sink_col(sink_ref, g):
    return jnp.concatenate([jnp.full((BLOCK, 1), sink_ref[0, g * GROUP + j], F32) for j in range(GROUP)], axis=0)


def _attn_specs(nb):
    kvw = N_KV * HEAD_DIM
    vcol = QK_DIM // kvw
    cur = lambda e, n: (e * nb + n, 0)
    prev = lambda e, n: (e * nb + jnp.maximum(n - 1, 0), 0)
    return [pl.BlockSpec((BLOCK, D), cur),
            pl.BlockSpec((BLOCK, kvw), cur), pl.BlockSpec((BLOCK, kvw), prev),
            pl.BlockSpec((BLOCK, kvw), lambda e, n: (e * nb + n, vcol)),
            pl.BlockSpec((BLOCK, kvw), lambda e, n: (e * nb + jnp.maximum(n - 1, 0), vcol)),
            pl.BlockSpec(memory_space=pltpu.SMEM)]


def attn_fwd(q, k, raw, sinks):
    t = q.shape[0]
    nb = t // 2 // BLOCK

    def body(q_ref, kc_ref, kp_ref, vc_ref, vp_ref, sink_ref, o_ref, lse_ref):
        n = pl.program_id(1)
        qv = q_ref[...]
        kc, kp, vc, vp = kc_ref[...], kp_ref[...], vc_ref[...], vp_ref[...]
        mask = _band_mask(n)
        outs, lses = [], []
        for g in range(N_KV):
            kk, vv = _kv_cat(kp, kc, g), _kv_cat(vp, vc, g)
            s = jnp.where(mask, _dot_nt(_stack_heads(qv, g), kk) * (HEAD_DIM ** -0.5), -1e30)
            sink = _sink_col(sink_ref, g)
            m = jnp.maximum(jnp.max(s, axis=1, keepdims=True), sink)
            p = jnp.exp(s - m)
            l = jnp.sum(p, axis=1, keepdims=True) + jnp.exp(sink - m)
            o = _dot(p.astype(BF16), vv) / l
            lse = m + jnp.log(l)
            for j in range(GROUP):
                outs.append(o[j * BLOCK:(j + 1) * BLOCK, :])
                lses.append(lse[j * BLOCK:(j + 1) * BLOCK, :])
        o_ref[...] = jnp.concatenate(outs, axis=1).astype(BF16)
        lse_ref[...] = jnp.concatenate(lses, axis=1)

    cur = lambda e, n: (e * nb + n, 0)
    return pl.pallas_call(
        body, name="attn_fwd", grid=(2, nb),
        in_specs=_attn_specs(nb),
        out_specs=[pl.BlockSpec((BLOCK, D), cur), pl.BlockSpec((BLOCK, N_HEADS), cur)],
        out_shape=[jax.ShapeDtypeStruct((t, D), BF16), jax.ShapeDtypeStruct((t, N_HEADS), F32)],
        compiler_params=_cparams(32, 2),
    )(q, k, k, raw, raw, sinks)


def attn_bwd(q, k, raw, sinks, o, do, lse):
    t = q.shape[0]
    s_len = t // 2
    nb = s_len // BLOCK
    kvw = N_KV * HEAD_DIM

    def body(q_ref, kc_ref, kp_ref, vc_ref, vp_ref, sink_ref, o_ref, do_ref, lse_ref, dq_ref, dk_ref, dv_ref, ds_ref):
        n = pl.program_id(1)

        @pl.when(n == 0)
        def _():
            dk_ref[...] = jnp.zeros_like(dk_ref)
            dv_ref[...] = jnp.zeros_like(dv_ref)

        @pl.when((n == 0) & (pl.program_id(0) == 0))
        def _():
            ds_ref[...] = jnp.zeros_like(ds_ref)

        qv, ov, dov, lsev = q_ref[...], o_ref[...], do_ref[...], lse_ref[...]
        kc, kp, vc, vp = kc_ref[...], kp_ref[...], vc_ref[...], vp_ref[...]
        mask = _band_mask(n)
        dqs, dks, dvs, dsk = [], [], [], []
        for g in range(N_KV):
            kk, vv = _kv_cat(kp, kc, g), _kv_cat(vp, vc, g)
            qg, og, dog = _stack_heads(qv, g), _stack_heads(ov, g), _stack_heads(dov, g)
            lse = jnp.concatenate([lsev[:, g * GROUP + j:g * GROUP + j + 1] for j in range(GROUP)], axis=0)
            s = jnp.where(mask, _dot_nt(qg, kk) * (HEAD_DIM ** -0.5), -1e30)
            p = jnp.exp(s - lse)
            dd = jnp.sum(dog.astype(F32) * og.astype(F32), axis=1, keepdims=True)
            ds = (p * (_dot_nt(dog, vv) - dd) * (HEAD_DIM ** -0.5)).astype(BF16)
            dqg = _dot(ds, kk)
            dks.append(_dot_tn(ds, qg))
            dvs.append(_dot_tn(p.astype(BF16), dog))
            wsink = jnp.exp(_sink_col(sink_ref, g) - lse) * dd
            for j in range(GROUP):
                dqs.append(dqg[j * BLOCK:(j + 1) * BLOCK, :])
                dsk.append(wsink[j * BLOCK:(j + 1) * BLOCK, :])
        dq_ref[...] = jnp.concatenate(dqs, axis=1).astype(BF16)
        dkk = jnp.concatenate(dks, axis=1)
        dvv = jnp.concatenate(dvs, axis=1)
        prev0 = pl.multiple_of(jnp.maximum(n - 1, 0) * BLOCK, BLOCK)
        cur0 = pl.multiple_of(n * BLOCK, BLOCK)
        dk_ref[pl.ds(prev0, BLOCK), :] += dkk[:BLOCK]
        dv_ref[pl.ds(prev0, BLOCK), :] += dvv[:BLOCK]
        dk_ref[pl.ds(cur0, BLOCK), :] += dkk[BLOCK:]
        dv_ref[pl.ds(cur0, BLOCK), :] += dvv[BLOCK:]
        ds_ref[0:1, :] -= _colsum(jnp.concatenate(dsk, axis=1))

    cur = lambda e, n: (e * nb + n, 0)
    return pl.pallas_call(
        body, name="attn_bwd", grid=(2, nb),
        in_specs=_attn_specs(nb) + [pl.BlockSpec((BLOCK, D), cur), pl.BlockSpec((BLOCK, D), cur),
                                    pl.BlockSpec((BLOCK, N_HEADS), cur)],
        out_specs=[pl.BlockSpec((BLOCK, D), cur), pl.BlockSpec((s_len, kvw), lambda e, n: (e, 0)),
                   pl.BlockSpec((s_len, kvw), lambda e, n: (e, 0)), pl.BlockSpec((8, N_HEADS), lambda e, n: (0, 0))],
        out_shape=[jax.ShapeDtypeStruct((t, D), BF16), jax.ShapeDtypeStruct((t, kvw), F32),
                   jax.ShapeDtypeStruct((t, kvw), F32), jax.ShapeDtypeStruct((8, N_HEADS), F32)],
        compiler_params=_cparams(32, 2),
    )(q, k, k, raw, raw, sinks, o, do, lse)


CONV_COLS = 256


def _conv_specs(s_len):
    nct = D // CONV_COLS
    return [pl.BlockSpec((s_len, CONV_COLS), lambda j, e: (e, j)),
            pl.BlockSpec((s_len, CONV_COLS), lambda j, e: (e, nct + j)),
            pl.BlockSpec((s_len, CONV_COLS), lambda j, e: (e, 2 * nct + j)),
            pl.BlockSpec((3, CONV_COLS), lambda j, e: (0, j))]


def _conv_taps(gc, v, w, s_len):
    u = gc * v
    row = lax.broadcasted_iota(jnp.int32, u.shape, 0)
    u1 = jnp.where(row >= 1, pltpu.roll(u, 1, 0), 0.0)
    u2 = jnp.where(row >= 2, pltpu.roll(u, 2, 0), 0.0)
    return u, u1, u2, w[2:3, :] * u + w[1:2, :] * u1 + w[0:1, :] * u2


def conv_fwd(z, w):
    t = z.shape[0]
    s_len = t // 2

    def body(gb_ref, gc_ref, v_ref, w_ref, p_ref):
        _, _, _, conv = _conv_taps(gc_ref[...].astype(F32), v_ref[...].astype(F32), w_ref[...], s_len)
        p_ref[...] = (gb_ref[...].astype(F32) * conv).astype(BF16)

    return pl.pallas_call(
        body, name="conv_fwd", grid=(D // CONV_COLS, 2),
        in_specs=_conv_specs(s_len),
        out_specs=pl.BlockSpec((s_len, CONV_COLS), lambda j, e: (e, j)),
        out_shape=jax.ShapeDtypeStruct((t, D), BF16),
        compiler_params=_cparams(40, 2),
    )(z, z, z, w)


def conv_bwd(z, w, dp):
    t = z.shape[0]
    s_len = t // 2

    def body(gb_ref, gc_ref, v_ref, w_ref, dp_ref, dgb_ref, dgc_ref, dv_ref, dw_ref):
        e = pl.program_id(1)
        gc, v, wv = gc_ref[...].astype(F32), v_ref[...].astype(F32), w_ref[...]
        u, u1, u2, conv = _conv_taps(gc, v, wv, s_len)
        dpv = dp_ref[...].astype(F32)
        dgb_ref[...] = (dpv * conv).astype(BF16)
        dc = dpv * gb_ref[...].astype(F32)
        row = lax.broadcasted_iota(jnp.int32, dc.shape, 0)
        dc1 = jnp.where(row <= s_len - 2, pltpu.roll(dc, s_len - 1, 0), 0.0)
        dc2 = jnp.where(row <= s_len - 3, pltpu.roll(dc, s_len - 2, 0), 0.0)
        du = wv[2:3, :] * dc + wv[1:2, :] * dc1 + wv[0:1, :] * dc2
        dgc_ref[...] = (du * v).astype(BF16)
        dv_ref[...] = (du * gc).astype(BF16)

        @pl.when(e == 0)
        def _():
            dw_ref[...] = jnp.zeros_like(dw_ref)

        dw_ref[0:1, :] += _colsum(dc * u2)
        dw_ref[1:2, :] += _colsum(dc * u1)
        dw_ref[2:3, :] += _colsum(dc * u)

    blk = pl.BlockSpec((s_len, CONV_COLS), lambda j, e: (e, j))
    return pl.pallas_call(
        body, name="conv_bwd", grid=(D // CONV_COLS, 2),
        in_specs=_conv_specs(s_len) + [blk],
        out_specs=[blk, blk, blk, pl.BlockSpec((8, CONV_COLS), lambda j, e: (0, j))],
        out_shape=[jax.ShapeDtypeStruct((t, D), BF16)] * 3 + [jax.ShapeDtypeStruct((8, D), F32)],
        compiler_params=_cparams(48, 2),
    )(z, z, z, w, dp)


def loss_grad(y, tgt, tm=512):
    t = y.shape[0]

    def body(y_ref, t_ref, dy_ref, l_ref):
        i = pl.program_id(0)
        d = y_ref[...] - t_ref[...]
        dy_ref[...] = d * (1.0 / D)

        @pl.when(i == 0)
        def _():
            l_ref[...] = jnp.zeros_like(l_ref)

        l_ref[...] += 0.5 / D * jnp.sum(d * d)

    return pl.pallas_call(
        body, name="loss_grad", grid=(t // tm,),
        in_specs=[pl.BlockSpec((tm, D), lambda i: (i, 0))] * 2,
        out_specs=[pl.BlockSpec((tm, D), lambda i: (i, 0)), pl.BlockSpec((8, 128), lambda i: (0, 0))],
        out_shape=[jax.ShapeDtypeStruct((t, D), F32), jax.ShapeDtypeStruct((8, 128), F32)],
        compiler_params=_cparams(32, 1),
    )(y, tgt)


ADA_COLS = 384


def ada_fwd(c_all, w):
    nl, _, n = w.shape
    nex = c_all.shape[0]

    def body(c_ref, w_ref, o_ref):
        cv = c_ref[...]
        ca = (cv * jax.nn.sigmoid(cv)).astype(BF16)
        o_ref[0] = _dot(ca, w_ref[0].astype(BF16))

    return pl.pallas_call(
        body, name="ada_fwd", grid=(nl, n // ADA_COLS),
        in_specs=[pl.BlockSpec((nex, D), lambda l, j: (0, 0)), pl.BlockSpec((1, D, ADA_COLS), lambda l, j: (l, 0, j))],
        out_specs=pl.BlockSpec((1, nex, ADA_COLS), lambda l, j: (l, 0, j)),
        out_shape=jax.ShapeDtypeStruct((nl, nex, n), F32),
        compiler_params=_cparams(32, 2),
    )(c_all, w)


def _adam_math(w, g, m, v):
    m = ADAM_B1 * m + (1.0 - ADAM_B1) * g
    v = ADAM_B2 * v + (1.0 - ADAM_B2) * (g * g)
    m_hat = m / (1.0 - ADAM_B1 ** ADAM_STEP)
    v_hat = v / (1.0 - ADAM_B2 ** ADAM_STEP)
    return -ADAM_LR * (m_hat / (jnp.sqrt(v_hat) + ADAM_EPS) + ADAM_WD * w), m, v


def ada_bwd_adam(c_all, dm, w, m, v):
    nl, _, n = w.shape
    nex = c_all.shape[0]

    def body(c_ref, dm_ref, w_ref, m_ref, v_ref, g_ref, d_ref, mo_ref, vo_ref):
        cv = c_ref[...]
        ca = (cv * jax.nn.sigmoid(cv)).astype(BF16)
        g = _dot_tn(ca, dm_ref[0].astype(BF16))
        g_ref[0] = g
        d_ref[0], mo_ref[0], vo_ref[0] = _adam_math(w_ref[0], g, m_ref[0], v_ref[0])

    wspec = pl.BlockSpec((1, D, ADA_COLS), lambda l, j: (l, 0, j))
    return pl.pallas_call(
        body, name="ada_bwd_adam", grid=(nl, n // ADA_COLS),
        in_specs=[pl.BlockSpec((nex, D), lambda l, j: (0, 0)), pl.BlockSpec((1, nex, ADA_COLS), lambda l, j: (l, 0, j)),
                  wspec, wspec, wspec],
        out_specs=[wspec] * 4,
        out_shape=[jax.ShapeDtypeStruct(w.shape, F32)] * 4,
        compiler_params=_cparams(40, 2),
    )(c_all, dm, w, m, v)


def adamw(w, g, m, v):
    shape = w.shape
    cols = shape[-1]
    rows = w.size // cols
    args = [a.reshape(rows, cols) for a in (w, g, m, v)]
    tr = rows
    while tr * cols * 4 > (1 << 20) and tr % 16 == 0:
        tr //= 2

    def body(w_ref, g_ref, m_ref, v_ref, d_ref, mo_ref, vo_ref):
        d_ref[...], mo_ref[...], vo_ref[...] = _adam_math(w_ref[...], g_ref[...], m_ref[...], v_ref[...])

    spec = pl.BlockSpec((tr, cols), lambda i: (i, 0))
    outs = pl.pallas_call(
        body, name="adamw", grid=(rows // tr,),
        in_specs=[spec] * 4, out_specs=[spec] * 3,
        out_shape=[jax.ShapeDtypeStruct((rows, cols), F32)] * 3,
        compiler_params=_cparams(32, 1),
    )(*args)
    return [o.reshape(shape) for o in outs]


def adamw_layers(w, g, m, v, prev, l0, n):
    _, r, c = w.shape
    tr = r
    while tr * c * 4 > (1 << 20) and tr % 16 == 0:
        tr //= 2

    def body(w_ref, g_ref, m_ref, v_ref, pd_ref, pm_ref, pv_ref, d_ref, mo_ref, vo_ref):
        d_ref[...], mo_ref[...], vo_ref[...] = _adam_math(w_ref[...], g_ref[...], m_ref[...], v_ref[...])

    spec = pl.BlockSpec((1, tr, c), lambda l, i: (l0 + l, i, 0))
    return pl.pallas_call(
        body, name="adamw_layers", grid=(n, r // tr),
        in_specs=[spec] * 4 + [ANY] * 3, out_specs=[spec] * 3,
        out_shape=[jax.ShapeDtypeStruct(w.shape, F32)] * 3,
        input_output_aliases={4: 0, 5: 1, 6: 2},
        compiler_params=_cparams(32, 2),
    )(w, g, m, v, *prev)


def cast_into_window(w, l0, n, ax, chip, after=()):
    _, r, c = w.shape
    tr = r
    while tr * c * 4 > (2 << 20) and tr % 32 == 0:
        tr //= 2
    nrb = r // tr
    full = (n, r * N_CHIPS, c) if ax == 1 else (n, r, c * N_CHIPS)

    def body(chip_ref, w_ref, *rest):
        o_ref = rest[len(after)]
        o_ref[...] = w_ref[...].astype(BF16)

    def omap(l, i, chip_ref):
        return (l, chip_ref[0] * nrb + i, 0) if ax == 1 else (l, i, chip_ref[0])

    return pl.pallas_call(
        body, name="cast_into_window",
        grid_spec=pltpu.PrefetchScalarGridSpec(
            num_scalar_prefetch=1, grid=(n, nrb),
            in_specs=[pl.BlockSpec((1, tr, c), lambda l, i, chip_ref: (l0 + l, i, 0))] + [ANY] * len(after),
            out_specs=pl.BlockSpec((1, tr, c), omap)),
        out_shape=jax.ShapeDtypeStruct(full, BF16), compiler_params=_cparams(32, 2),
    )(chip, w, *after)


def add_bias(a, b):
    def body(a_ref, b_ref, o_ref):
        o_ref[...] = a_ref[...] + b_ref[...]

    return pl.pallas_call(body, name="add_bias", out_shape=jax.ShapeDtypeStruct(a.shape, F32))(a, b)


N_DMOD_ROWS = 40


def reduce_small(p_all):
    rows = p_all.shape[1]

    def body(p_ref, red_ref, ex_ref):
        acc = p_ref[0]
        for d in range(1, 8):
            acc = acc + p_ref[d]
        red_ref[...] = acc
        ex_ref[...] = acc[:N_DMOD_ROWS] + acc[N_DMOD_ROWS:2 * N_DMOD_ROWS]

    return pl.pallas_call(
        body, name="reduce_small",
        out_shape=[jax.ShapeDtypeStruct((rows, D), F32), jax.ShapeDtypeStruct((N_DMOD_ROWS, D), F32)],
        compiler_params=_cparams(32, 0),
    )(p_all)


def _place():
    return lax.axis_index("x"), lax.axis_index("y"), lax.axis_index("c")


def _other_chips(x, y):
    return [(1 - x, y), (x, 1 - y), (1 - x, 1 - y)]


def _sl(ref, axis, start, size):
    idx = [slice(None)] * len(ref.shape)
    idx[axis] = pl.ds(pl.multiple_of(start, 16), size)
    return ref.at[tuple(idx)]


def _rcopy(src, dst, send_sem, recv_sem, to):
    return pltpu.make_async_remote_copy(src_ref=src, dst_ref=dst, send_sem=send_sem, recv_sem=recv_sem,
                                        device_id=to, device_id_type=MESH)


def allgather_small(v, all_devices):
    rows, cols = v.shape
    flips = [(dx, dy, dc) for dx in (0, 1) for dy in (0, 1) for dc in (0, 1)
             if (dx, dy, dc) != (0, 0, 0) and (all_devices or dc == 0)]
    n_out = 8 if all_devices else 4

    def body(v_ref, o_ref, send_sems, recv_sems):
        x, y, c = _place()

        def slot(px, py, pc):
            return 4 * px + 2 * py + pc if all_devices else 2 * px + py

        peers = [(1 - x if dx else x, 1 - y if dy else y, 1 - c if dc else c) for dx, dy, dc in flips]
        sends = [_rcopy(v_ref, o_ref.at[slot(x, y, c)], send_sems.at[r], recv_sems.at[r], peer)
                 for r, peer in enumerate(peers)]
        for cp in sends:
            cp.start()
        o_ref[slot(x, y, c)] = v_ref[...]
        for r, peer in enumerate(peers):
            _rcopy(v_ref, o_ref.at[slot(*peer)], send_sems.at[r], recv_sems.at[r], peer).wait_recv()
        for cp in sends:
            cp.wait_send()

    vm = pl.BlockSpec(memory_space=pltpu.VMEM)
    return pl.pallas_call(
        body, name="allgather_small_all" if all_devices else "allgather_small_chips",
        in_specs=[vm], out_specs=vm,
        out_shape=jax.ShapeDtypeStruct((n_out, rows, cols), v.dtype),
        scratch_shapes=[pltpu.SemaphoreType.DMA((len(flips),)), pltpu.SemaphoreType.DMA((len(flips),))],
        compiler_params=pltpu.CompilerParams(vmem_limit_bytes=32 * 1024 * 1024),
    )(v)


HBM = pl.BlockSpec(memory_space=pltpu.HBM)
SEM = pl.BlockSpec(memory_space=pltpu.SEMAPHORE)
SPLIT_COPY = pltpu.CompilerParams(has_side_effects=pltpu.SideEffectType.DATAFLOW_SIDE_EFFECTING)


def _in_hbm(a):
    return pltpu.with_memory_space_constraint(a, pltpu.HBM)


def _window(ref, ax, chip):
    n = ref.shape[ax] // N_CHIPS
    return _sl(ref, ax, (2 * chip[0] + chip[1]) * n, n)


def _half(ref, ax, cc):
    ha = 3 - ax
    hs = ref.shape[ha] // 2
    return _sl(ref, ha, cc * hs, hs)


def gather_start(bufs, axes, tag):
    na = len(bufs)

    def body(*refs):
        ins = refs[:na]
        send_sems, recv_sems = refs[na], refs[na + 1]
        token = refs[2 * na + 2]
        x, y, c = _place()
        for a in range(na):
            mine = _half(_window(ins[a], axes[a], (x, y)), axes[a], c)
            for j, chip in enumerate(_other_chips(x, y)):
                _rcopy(mine, mine, send_sems.at[3 * a + j], recv_sems.at[3 * a + j], (*chip, c)).start()
        token[...] = jnp.zeros_like(token)

    dma = pltpu.SemaphoreType.DMA
    outs = pl.pallas_call(
        body, name="gather_start_" + tag,
        in_specs=[HBM] * na,
        out_specs=(SEM, SEM, *[HBM] * na, pl.BlockSpec(memory_space=pltpu.VMEM)),
        out_shape=(dma((3 * na,)), dma((3 * na,)), *[pltpu.HBM(b.shape, b.dtype) for b in bufs],
                   jax.ShapeDtypeStruct((8, 128), F32)),
        input_output_aliases={a: 2 + a for a in range(na)},
        compiler_params=SPLIT_COPY,
    )(*[_in_hbm(b) for b in bufs])
    return outs[0], outs[1], list(outs[2:2 + na]), outs[2 + na]


def gather_wait(send_sems, recv_sems, bufs, axes, after, tag):
    na = len(bufs)

    def body(*refs):
        ins = refs[:na]
        send_sems, recv_sems = refs[na], refs[na + 1]
        x, y, c = _place()
        for a in range(na):
            for j, chip in enumerate(_other_chips(x, y)):
                got = _half(_window(ins[a], axes[a], chip), axes[a], c)
                _rcopy(got, got, send_sems.at[3 * a + j], recv_sems.at[3 * a + j], (*chip, c)).wait_recv()
        for a in range(na):
            mine = _half(_window(ins[a], axes[a], (x, y)), axes[a], c)
            for j, chip in enumerate(_other_chips(x, y)):
                _rcopy(mine, mine, send_sems.at[3 * a + j], recv_sems.at[3 * a + j], (*chip, c)).wait_send()

    return pl.pallas_call(
        body, name="gather_wait_" + tag,
        in_specs=[HBM] * na + [SEM, SEM] + [ANY] * len(after),
        out_specs=[HBM] * na,
        out_shape=[pltpu.HBM(b.shape, b.dtype) for b in bufs],
        input_output_aliases={a: a for a in range(na)},
        compiler_params=SPLIT_COPY,
    )(*bufs, send_sems, recv_sems, *after)


def gather_forward(bufs, axes):
    na = len(bufs)

    def body(*refs):
        outs = refs[na:2 * na]
        send_sems, recv_sems = refs[2 * na:]
        x, y, c = _place()
        chips = _other_chips(x, y)
        passed = []
        for a in range(na):
            for j, chip in enumerate(chips):
                got = _half(_window(outs[a], axes[a], chip), axes[a], c)
                cp = _rcopy(got, got, send_sems.at[3 * a + j], recv_sems.at[3 * a + j], (x, y, 1 - c))
                cp.start()
                passed.append(cp)
        for a in range(na):
            for j, chip in enumerate(chips):
                got = _half(_window(outs[a], axes[a], chip), axes[a], 1 - c)
                _rcopy(got, got, send_sems.at[3 * a + j], recv_sems.at[3 * a + j], (x, y, 1 - c)).wait_recv()
        for cp in passed:
            cp.wait_send()

    dma = pltpu.SemaphoreType.DMA
    return pl.pallas_call(
        body, name="gather_forward",
        in_specs=[ANY] * na, out_specs=[ANY] * na,
        out_shape=[jax.ShapeDtypeStruct(b.shape, BF16) for b in bufs],
        input_output_aliases={a: a for a in range(na)},
        scratch_shapes=[dma((3 * na,)), dma((3 * na,))],
    )(*bufs)


def exchange_halves(grads, axes):
    na = len(grads)

    def hshape(g, ax):
        ha = 3 - ax
        return tuple(d // 2 if i == ha else d for i, d in enumerate(g.shape))

    def body(*refs):
        ins, outs = refs[:na], refs[na:2 * na]
        send_sems, recv_sems = refs[2 * na:]
        x, y, c = _place()
        cps = []
        for a in range(na):
            ha = 3 - axes[a]
            hs = ins[a].shape[ha] // 2
            cp = _rcopy(_sl(ins[a], ha, (1 - c) * hs, hs), outs[a], send_sems.at[a], recv_sems.at[a], (x, y, 1 - c))
            cp.start()
            cps.append(cp)
        for cp in cps:
            cp.wait_recv()
        for cp in cps:
            cp.wait_send()

    dma = pltpu.SemaphoreType.DMA
    return pl.pallas_call(
        body, name="exchange_halves",
        in_specs=[ANY] * na, out_specs=[ANY] * na,
        out_shape=[jax.ShapeDtypeStruct(hshape(g, ax), BF16) for g, ax in zip(grads, axes)],
        scratch_shapes=[dma((na,)), dma((na,))],
    )(*grads)


def scatter_start(halves, axes, tag):
    na = len(halves)

    def pshape(h, ax):
        return (N_CHIPS - 1,) + tuple(d // N_CHIPS if i == ax else d for i, d in enumerate(h.shape))

    def body(*refs):
        ins, lands = refs[:na], refs[na:2 * na]
        send_sems, recv_sems = refs[2 * na], refs[2 * na + 1]
        token = refs[4 * na + 2]
        x, y, c = _place()
        for a in range(na):
            for j, chip in enumerate(_other_chips(x, y)):
                _rcopy(_window(ins[a], axes[a], chip), lands[a].at[j],
                       send_sems.at[3 * a + j], recv_sems.at[3 * a + j], (*chip, c)).start()
        token[...] = jnp.zeros_like(token)

    dma = pltpu.SemaphoreType.DMA
    lands = [lax.empty(pshape(h, ax), BF16) for h, ax in zip(halves, axes)]
    outs = pl.pallas_call(
        body, name="scatter_start_" + tag,
        in_specs=[HBM] * (2 * na),
        out_specs=(SEM, SEM, *[HBM] * (2 * na), pl.BlockSpec(memory_space=pltpu.VMEM)),
        out_shape=(dma((3 * na,)), dma((3 * na,)), *[pltpu.HBM(b.shape, b.dtype) for b in halves + lands],
                   jax.ShapeDtypeStruct((8, 128), F32)),
        input_output_aliases={a: 2 + a for a in range(2 * na)},
        compiler_params=SPLIT_COPY,
    )(*[_in_hbm(b) for b in halves + lands])
    return outs[0], outs[1], list(outs[2:2 + na]), list(outs[2 + na:2 + 2 * na]), outs[2 + 2 * na]


def scatter_wait(send_sems, recv_sems, halves, lands, axes, after, tag):
    na = len(halves)

    def body(*refs):
        ins, lands = refs[:na], refs[na:2 * na]
        send_sems, recv_sems = refs[2 * na], refs[2 * na + 1]
        x, y, c = _place()
        for a in range(na):
            for j, chip in enumerate(_other_chips(x, y)):
                _rcopy(_window(ins[a], axes[a], chip), lands[a].at[j],
                       send_sems.at[3 * a + j], recv_sems.at[3 * a + j], (*chip, c)).wait_recv()
        for a in range(na):
            for j, chip in enumerate(_other_chips(x, y)):
                _rcopy(_window(ins[a], axes[a], chip), lands[a].at[j],
                       send_sems.at[3 * a + j], recv_sems.at[3 * a + j], (*chip, c)).wait_send()

    outs = pl.pallas_call(
        body, name="scatter_wait_" + tag,
        in_specs=[HBM] * (2 * na) + [SEM, SEM] + [ANY] * len(after),
        out_specs=[HBM] * (2 * na),
        out_shape=[pltpu.HBM(b.shape, b.dtype) for b in halves + lands],
        input_output_aliases={a: a for a in range(2 * na)},
        compiler_params=SPLIT_COPY,
    )(*halves, *lands, send_sems, recv_sems, *after)
    return list(outs[:na]), list(outs[na:])


def join_halves(gs, regions, axes):
    na = len(gs)

    def body(*refs):
        outs = refs[na:2 * na]
        send_sems, recv_sems = refs[2 * na:]
        x, y, c = _place()
        cps = []
        for a in range(na):
            ha = 3 - axes[a]
            hs = outs[a].shape[ha] // 2
            reg = outs[a].at[pl.ds(*regions[a])]
            mine = _sl(reg, ha, c * hs, hs)
            cp = _rcopy(mine, mine, send_sems.at[a], recv_sems.at[a], (x, y, 1 - c))
            cp.start()
            cps.append((cp, _sl(reg, ha, (1 - c) * hs, hs)))
        for a, (cp, theirs) in enumerate(cps):
            _rcopy(theirs, theirs, send_sems.at[a], recv_sems.at[a], (x, y, 1 - c)).wait_recv()
        for cp, _ in cps:
            cp.wait_send()

    dma = pltpu.SemaphoreType.DMA
    return pl.pallas_call(
        body, name="join_halves",
        in_specs=[ANY] * na, out_specs=[ANY] * na,
        out_shape=[jax.ShapeDtypeStruct(g.shape, F32) for g in gs],
        input_output_aliases={a: a for a in range(na)},
        scratch_shapes=[dma((na,)), dma((na,))],
    )(*gs)


def _tile2(r, c, itemsize, limit):
    bc = c
    while bc > 1536:
        bc //= 2
    assert c % bc == 0 and bc % 128 == 0
    br = r
    while br * bc * itemsize > limit and br % 32 == 0:
        br //= 2
    assert r % br == 0 and br % 16 == 0
    return br, bc


def add_my_half(g, theirs, ax, cc):
    ha = 3 - ax
    nl, r, c = theirs.shape
    br, bc = _tile2(r, c, 2, 1 << 20)
    nrb, ncb = r // br, c // bc

    def body(cc_ref, g_ref, t_ref, o_ref):
        o_ref[...] = (g_ref[...].astype(F32) + t_ref[...].astype(F32)).astype(BF16)

    def gmap(l, i, j, cc_ref):
        return (l, cc_ref[0] * nrb + i, j) if ha == 1 else (l, i, cc_ref[0] * ncb + j)

    blk = pl.BlockSpec((1, br, bc), lambda l, i, j, cc_ref: (l, i, j))
    return pl.pallas_call(
        body, name="add_my_half",
        grid_spec=pltpu.PrefetchScalarGridSpec(
            num_scalar_prefetch=1, grid=(nl, nrb, ncb),
            in_specs=[pl.BlockSpec((1, br, bc), gmap), blk], out_specs=blk),
        out_shape=jax.ShapeDtypeStruct(theirs.shape, BF16),
        compiler_params=_cparams(32, 3),
    )(cc, g, theirs)


def sum_chips(parts, pair, gstack, l0, ax, where):
    _, n, r, c = parts.shape
    ha = 3 - ax
    br, bc = _tile2(r, c, 2, 1 << 19)
    nrb, ncb = r // br, c // bc

    def body(w_ref, p_ref, own_ref, g_ref, o_ref):
        acc = own_ref[...].astype(F32)
        for q in range(N_CHIPS - 1):
            acc = acc + p_ref[q].astype(F32)
        o_ref[...] = acc

    def own_map(l, i, j, w_ref):
        return (l, w_ref[0] * nrb + i, j) if ax == 1 else (l, i, w_ref[0] * ncb + j)

    def out_map(l, i, j, w_ref):
        return (l0 + l, w_ref[1] * nrb + i, j) if ha == 1 else (l0 + l, i, w_ref[1] * ncb + j)

    return pl.pallas_call(
        body, name="sum_chips",
        grid_spec=pltpu.PrefetchScalarGridSpec(
            num_scalar_prefetch=1, grid=(n, nrb, ncb),
            in_specs=[pl.BlockSpec((N_CHIPS - 1, 1, br, bc), lambda l, i, j, w_ref: (0, l, i, j)),
                      pl.BlockSpec((1, br, bc), own_map), ANY],
            out_specs=pl.BlockSpec((1, br, bc), out_map)),
        out_shape=jax.ShapeDtypeStruct(gstack.shape, F32),
        input_output_aliases={3: 0},
        compiler_params=_cparams(32, 3),
    )(where, parts, pair, gstack)


BIG = ("w_ffn_up", "w_ffn_down", "attn_w_qkv", "attn_w_o", "conv_w_in", "conv_w_out")
BIG_AXIS = {"w_ffn_up": 2, "w_ffn_down": 1, "attn_w_qkv": 2, "attn_w_o": 1, "conv_w_in": 2, "conv_w_out": 1}
WEIGHTS = ("norm_gain", "w_ada", "b_ada", "w_ffn_up", "w_ffn_down", "attn_w_qkv", "attn_b_qkv", "attn_q_gain",
           "attn_k_gain", "attn_sinks", "attn_w_o", "attn_b_o", "conv_w_in", "conv_w", "conv_w_out")
N_SMALL_ROWS = 112


def _stack3(a):
    return a.reshape((-1,) + a.shape[-2:])


def _head_matrices():
    lane = jnp.arange(QK_DIM)
    head = lane // HEAD_DIM
    col = jnp.arange(128)
    g1 = jnp.where(head[:, None] == col[None, :], 1.0 / HEAD_DIM, 0.0).astype(BF16)
    g2 = jnp.where(col[:, None] == head[None, :], 1.0, 0.0).astype(BF16)
    fold = lane % HEAD_DIM + jnp.where(head >= N_HEADS, HEAD_DIM, 0)
    gsel = jnp.where(fold[:, None] == col[None, :], 1.0, 0.0).astype(BF16)
    return g1, g2, gsel


def _pad_cols(a, n):
    return jnp.pad(a, ((0, 0), (0, n - a.shape[1])))


def kernel(x, c, positions, norm_gain, w_ada, b_ada, w_ffn_up, w_ffn_down, attn_w_qkv, attn_b_qkv, attn_q_gain, attn_k_gain, attn_sinks, attn_w_o, attn_b_o, conv_w_in, conv_w, conv_w_out, loss_target, m_norm_gain, m_w_ada, m_b_ada, m_w_ffn_up, m_w_ffn_down, m_attn_w_qkv, m_attn_b_qkv, m_attn_q_gain, m_attn_k_gain, m_attn_sinks, m_attn_w_o, m_attn_b_o, m_conv_w_in, m_conv_w, m_conv_w_out, v_norm_gain, v_w_ada, v_b_ada, v_w_ffn_up, v_w_ffn_down, v_attn_w_qkv, v_attn_b_qkv, v_attn_q_gain, v_attn_k_gain, v_attn_sinks, v_attn_w_o, v_attn_b_o, v_conv_w_in, v_conv_w, v_conv_w_out):
    w = dict(norm_gain=norm_gain, w_ada=w_ada, b_ada=b_ada, w_ffn_up=w_ffn_up, w_ffn_down=w_ffn_down,
             attn_w_qkv=attn_w_qkv, attn_b_qkv=attn_b_qkv, attn_q_gain=attn_q_gain, attn_k_gain=attn_k_gain,
             attn_sinks=attn_sinks, attn_w_o=attn_w_o, attn_b_o=attn_b_o, conv_w_in=conv_w_in, conv_w=conv_w,
             conv_w_out=conv_w_out)
    mom = dict(norm_gain=m_norm_gain, w_ada=m_w_ada, b_ada=m_b_ada, w_ffn_up=m_w_ffn_up, w_ffn_down=m_w_ffn_down,
               attn_w_qkv=m_attn_w_qkv, attn_b_qkv=m_attn_b_qkv, attn_q_gain=m_attn_q_gain,
               attn_k_gain=m_attn_k_gain, attn_sinks=m_attn_sinks, attn_w_o=m_attn_w_o, attn_b_o=m_attn_b_o,
               conv_w_in=m_conv_w_in, conv_w=m_conv_w, conv_w_out=m_conv_w_out)
    var = dict(norm_gain=v_norm_gain, w_ada=v_w_ada, b_ada=v_b_ada, w_ffn_up=v_w_ffn_up, w_ffn_down=v_w_ffn_down,
               attn_w_qkv=v_attn_w_qkv, attn_b_qkv=v_attn_b_qkv, attn_q_gain=v_attn_q_gain,
               attn_k_gain=v_attn_k_gain, attn_sinks=v_attn_sinks, attn_w_o=v_attn_w_o, attn_b_o=v_attn_b_o,
               conv_w_in=v_conv_w_in, conv_w=v_conv_w, conv_w_out=v_conv_w_out)

    xi, yi, ci = _place()
    chip = 2 * xi + yi
    dev = 4 * xi + 2 * yi + ci
    nex, s_len, _ = x.shape
    t = nex * s_len
    n_attn, n_conv = attn_w_qkv.shape[0], conv_w_in.shape[0]
    axes = [BIG_AXIS[n] for n in BIG]

    c_all = allgather_small(jnp.pad(c, ((0, 8 - nex), (0, 0))), True)[:, :nex].reshape(8 * nex, D)
    ada_cols = w_ada.shape[2]
    modp = ada_fwd(c_all, w_ada)
    modg = allgather_small(modp.reshape(DEPTH * 8 * nex, ada_cols), False)
    modg = lax.dynamic_slice_in_dim(modg.reshape(N_CHIPS, DEPTH, 8 * nex, ada_cols), dev * nex, nex, axis=2)
    modg = modg.transpose(1, 2, 0, 3).reshape(DEPTH, nex, 9 * D)
    mod = add_bias(modg, b_ada.reshape(DEPTH, 1, 9 * D)).reshape(DEPTH, nex, 9, D)

    small = jnp.concatenate([norm_gain.reshape(DEPTH * 3, -1), conv_w.reshape(n_conv * 3, -1)], axis=0)
    small = jnp.pad(small, ((0, -small.shape[0] % 8), (0, 0)))
    small = allgather_small(small, False).transpose(1, 0, 2).reshape(small.shape[0], D)
    gain_full = small[:DEPTH * 3].reshape(DEPTH, 3, D)
    convw_full = small[DEPTH * 3:DEPTH * 3 + n_conv * 3].reshape(n_conv, 3, D)

    chip_arr = chip.reshape(1).astype(jnp.int32)
    where = jnp.stack([chip, ci]).astype(jnp.int32)
    stacks = [_stack3(w[n]) for n in BIG]

    def mixer(i):
        return (2, 3) if i % 2 == 0 else (4, 5)

    groups = [[(0, 0, 1), (1, 0, 1)], [(mixer(0)[0], 0, 1), (mixer(0)[1], 0, 1), (0, 1, 1), (1, 1, 1)]]
    groups += [[(0, 2 * i, 2), (1, 2 * i, 2), (mixer(i)[0], i // 2, 1), (mixer(i)[1], i // 2, 1)]
               for i in range(1, DEPTH)]
    gaxes = [[axes[b] for b, _, _ in grp] for grp in groups]
    where_is = {(b, l0 + k): (g, a, k) for g, grp in enumerate(groups) for a, (b, l0, n) in enumerate(grp)
                for k in range(n)}
    in_flight, token = [], (mod, small)
    for g, grp in enumerate(groups):
        bufs = [cast_into_window(stacks[b], l0, n, axes[b], chip_arr, token) for b, l0, n in grp]
        ssem, rsem, bufs, tok = gather_start(bufs, gaxes[g], f"g{g}")
        in_flight.append((ssem, rsem, bufs))
        token = (tok,)

    invf = ROPE_THETA ** (-jnp.arange(0, HEAD_DIM, 2, dtype=F32) / HEAD_DIM)
    cos, sin = rope_tables(positions.reshape(t, 1), jnp.tile(invf, 4).reshape(1, 128))
    g1, g2, gsel = _head_matrices()
    gqk = [jnp.concatenate([jnp.tile(attn_q_gain[j], N_HEADS), jnp.tile(attn_k_gain[j], N_KV)]).reshape(1, QK_DIM)
           for j in range(n_attn)]
    zero_bias = jnp.zeros((1, D), F32)

    xs = x.reshape(t, D)
    saved, ready = [], {}

    def weight(b, l, after):
        g, a, k = where_is[(b, l)]
        if g not in ready:
            ssem, rsem, bufs = in_flight[g]
            ready[g] = gather_forward(gather_wait(ssem, rsem, bufs, gaxes[g], after, f"g{g}"), gaxes[g])
        return ready[g][a], k

    for i in range(DEPTH):
        j = i // 2
        gn, md = gain_full[i], mod[i]
        x0 = xs
        wup, k = weight(0, 2 * i, token if i == 0 else (xs,))
        wdn, _ = weight(1, 2 * i, ())
        xs, u1, f1 = ffn_fwd(x0, gn[0:1], md, wup, wdn, k, 0)
        x1 = xs
        wmi, k = weight(mixer(i)[0], j, (xs,))
        wmo, _ = weight(mixer(i)[1], j, ())
        if i % 2 == 0:
            raw, qr, kr = qkv_fwd(x1, gn[1:2], md, wmi, k, attn_b_qkv[j:j + 1], gqk[j], cos, sin, g1, g2)
            o, lse = attn_fwd(qr, kr, raw, attn_sinks[j:j + 1])
            xs, ymix = proj_res(x1, o, wmo, k, attn_b_o[j:j + 1], md)
            mix = (raw, qr, kr, o, lse)
        else:
            z = lin_fwd(x1, gn[1:2], md, wmi, k, 1)
            p = conv_fwd(z, convw_full[j])
            xs, ymix = proj_res(x1, p, wmo, k, zero_bias, md)
            mix = (z, p)
        x2 = xs
        wup, k = weight(0, 2 * i + 1, (xs,))
        wdn, _ = weight(1, 2 * i + 1, ())
        xs, u3, f3 = ffn_fwd(x2, gn[2:3], md, wup, wdn, k, 2)
        saved.append((x0, u1, f1, x1, mix, ymix, x2, u3, f3))
    dy, lpart = loss_grad(xs, loss_target.reshape(t, D))

    cc = ci.reshape(1).astype(jnp.int32)
    gshard = [lax.empty(s.shape, F32) for s in stacks]
    upd = [[lax.empty(s.shape, F32) for _ in range(3)] for s in stacks]
    mstacks = [_stack3(mom[n]) for n in BIG]
    vstacks = [_stack3(var[n]) for n in BIG]

    def finish(g, copies, after):
        ssem, rsem, pair, lands = copies
        pair, lands = scatter_wait(ssem, rsem, pair, lands, gaxes[g], after, f"g{g}")
        grp = groups[g]
        for a, (b, l0, n) in enumerate(grp):
            gshard[b] = sum_chips(lands[a], pair[a], gshard[b], l0, axes[b], where)
        joined = join_halves([gshard[b] for b, _, _ in grp], [(l0, n) for _, l0, n in grp], gaxes[g])
        for (b, l0, n), gj in zip(grp, joined):
            gshard[b] = gj
            upd[b] = adamw_layers(stacks[b], gj, mstacks[b], vstacks[b], upd[b], l0, n)

    ggrad = {g: [lax.empty(buf.shape, BF16) for buf in ready[g]] for g in range(len(groups))}
    missing = {g: sum(n for _, _, n in grp) for g, grp in enumerate(groups)}
    state = dict(flight=None, token=())

    def put(b, l, lhs, rhs, bm, bn):
        g, a, k = where_is[(b, l)]
        ggrad[g][a] = wgrad(ggrad[g][a], k, lhs, rhs, bm, bn)
        missing[g] -= 1
        if missing[g] == 0:
            theirs = exchange_halves(ggrad[g], gaxes[g])
            pair = [add_my_half(gr, th, ax, cc) for gr, th, ax in zip(ggrad[g], theirs, gaxes[g])]
            ssem, rsem, pair, lands, tok = scatter_start(pair, gaxes[g], f"g{g}")
            if state["flight"] is not None:
                finish(*state["flight"], (tok,))
            state.update(flight=(g, (ssem, rsem, pair, lands)), token=(tok,))

    dmod = [None] * DEPTH
    dgain = [None] * DEPTH
    db_qkv, dqk_gain, dsinks, db_o, dconv_w = ([None] * n_attn, [None] * n_attn, [None] * n_attn,
                                                [None] * n_attn, [None] * n_conv)
    for i in reversed(range(DEPTH)):
        j = i // 2
        gn, md = gain_full[i], mod[i]
        x0, u1, f1, x1, mix, ymix, x2, u3, f3 = saved[i]
        (wup, k), (wdn, _) = weight(0, 2 * i + 1, ()), weight(1, 2 * i + 1, ())
        du, a, df, sg3 = ffn_bwd_act(dy, u3, f3, md, wdn, k, 2, after=state["token"])
        dy, h, se3, sa3 = lin_bwd(dy, du, wup, k, x2, gn[2:3], md, 2, False)
        put(0, 2 * i + 1, h, du, D, FF_CHUNK)
        put(1, 2 * i + 1, a, df, FF_CHUNK, D)
        (wmi, k), (wmo, _) = weight(mixer(i)[0], j, ()), weight(mixer(i)[1], j, ())
        if i % 2 == 0:
            raw, qr, kr, o, lse = mix
            dyy, do, sp, sb = proj_res_bwd(dy, ymix, wmo, k, md)
            put(mixer(i)[1], j, o, dyy, D, D)
            dq, dk, dv, dsinks[j] = attn_bwd(qr, kr, raw, attn_sinks[j:j + 1], o, do, lse)
            dz, dqk_gain[j] = qkv_bwd_pre(dq, dk, dv, raw, gqk[j], cos, sin, g1, g2, gsel)
            dy, h, se2, sa2, db_qkv[j] = lin_bwd(dy, dz, wmi, k, x1, gn[1:2], md, 1, True)
            put(mixer(i)[0], j, h, dz, D, QKV_DIM)
            db_o[j] = sb
        else:
            z, p = mix
            dyy, dp, sp, _ = proj_res_bwd(dy, ymix, wmo, k, md)
            put(mixer(i)[1], j, p, dyy, D, D)
            dgb, dgc, dvv, dconv_w[j] = conv_bwd(z, convw_full[j], dp)
            dz = jnp.concatenate([dgb, dgc, dvv], axis=1)
            dy, h, se2, sa2 = lin_bwd(dy, dz, wmi, k, x1, gn[1:2], md, 1, False)
            put(mixer(i)[0], j, h, dz, D, 1536)
        (wup, k), (wdn, _) = weight(0, 2 * i, ()), weight(1, 2 * i, ())
        du, a, df, sg1 = ffn_bwd_act(dy, u1, f1, md, wdn, k, 0, after=state["token"])
        dy, h, se1, sa1 = lin_bwd(dy, du, wup, k, x0, gn[0:1], md, 0, False)
        put(0, 2 * i, h, du, D, FF_CHUNK)
        put(1, 2 * i, a, df, FF_CHUNK, D)
        dmod[i] = jnp.stack([se1[:, 0], se1[:, 1], sg1[:, 0], se2[:, 0], se2[:, 1], sp[:, 0],
                             se3[:, 0], se3[:, 1], sg3[:, 0]], axis=1)
        dgain[i] = jnp.stack([sa1[0], sa2[0], sa3[0]], axis=0)
    grad_x = dy.reshape(x.shape)

    dmod_ex = jnp.stack(dmod, axis=1).reshape(nex, DEPTH * 9, D)
    dmod_ex = jnp.pad(dmod_ex, ((0, 0), (0, N_DMOD_ROWS - DEPTH * 9), (0, 0))).reshape(nex * N_DMOD_ROWS, D)
    misc = jnp.concatenate([dqk_gain[jj][0] for jj in range(n_attn)]
                           + [jnp.pad(dsinks[jj][0], (0, 128 - N_HEADS)) for jj in range(n_attn)]
                           + [lpart[0]])
    rows = [dmod_ex,
            jnp.concatenate(dgain, axis=0), jnp.zeros((4, D), F32),
            jnp.concatenate([_pad_cols(db_qkv[jj][0:1], 2 * D).reshape(2, D) for jj in range(n_attn)], axis=0),
            jnp.concatenate([db_o[jj][0:1] for jj in range(n_attn)], axis=0),
            jnp.concatenate([dconv_w[jj][0:3] for jj in range(n_conv)], axis=0),
            jnp.pad(misc, (0, D - misc.shape[0])).reshape(1, D)]
    packed = jnp.concatenate(rows, axis=0)
    packed = jnp.pad(packed, ((0, N_SMALL_ROWS - packed.shape[0]), (0, 0)))
    p_all = allgather_small(packed, True)
    red, exsum = reduce_small(p_all)

    r0 = nex * N_DMOD_ROWS
    grads = {}
    grads["b_ada"] = exsum[:DEPTH * 9].reshape(DEPTH, 9 * D)
    grads["norm_gain"] = lax.dynamic_slice_in_dim(red[r0:r0 + 12].reshape(DEPTH, 3, D), chip * (D // N_CHIPS),
                                                  D // N_CHIPS, axis=2)
    r1 = r0 + 16
    grads["attn_b_qkv"] = red[r1:r1 + 2 * n_attn].reshape(n_attn, 2 * D)[:, :QKV_DIM]
    r2 = r1 + 2 * n_attn
    grads["attn_b_o"] = red[r2:r2 + n_attn]
    r3 = r2 + n_attn
    grads["conv_w"] = lax.dynamic_slice_in_dim(red[r3:r3 + 3 * n_conv].reshape(n_conv, 3, D), chip * (D // N_CHIPS),
                                               D // N_CHIPS, axis=2)
    mrow = red[r3 + 3 * n_conv]
    grads["attn_q_gain"] = jnp.stack([mrow[128 * jj:128 * jj + HEAD_DIM] for jj in range(n_attn)])
    grads["attn_k_gain"] = jnp.stack([mrow[128 * jj + HEAD_DIM:128 * jj + 128] for jj in range(n_attn)])
    grads["attn_sinks"] = jnp.stack([mrow[128 * (n_attn + jj):128 * (n_attn + jj) + N_HEADS] for jj in range(n_attn)])
    loss = mrow[128 * 2 * n_attn]

    dm_all = p_all[:, :r0].reshape(8, nex, N_DMOD_ROWS, D)[:, :, :DEPTH * 9].reshape(8 * nex, DEPTH, 9 * D)
    dm_mine = lax.dynamic_slice_in_dim(dm_all.transpose(1, 0, 2), chip * ada_cols, ada_cols, axis=2)
    g_ada, d_ada, nm_ada, nv_ada = ada_bwd_adam(c_all, dm_mine, w_ada, m_w_ada, v_w_ada)

    delta, new_m, new_v = {}, {}, {}
    for n in WEIGHTS:
        if n == "w_ada":
            grads[n], delta[n], new_m[n], new_v[n] = g_ada, d_ada, nm_ada, nv_ada
        elif n not in BIG:
            delta[n], new_m[n], new_v[n] = adamw(w[n], grads[n], mom[n], var[n])
    finish(*state["flight"], (delta["conv_w"], d_ada))
    for b, n in enumerate(BIG):
        grads[n] = gshard[b].reshape(w[n].shape)
        delta[n], new_m[n], new_v[n] = (u.reshape(w[n].shape) for u in upd[b])

    return (loss, grad_x, *[grads[n] for n in WEIGHTS], *[delta[n] for n in WEIGHTS],
            *[new_m[n] for n in WEIGHTS], *[new_v[n] for n in WEIGHTS])
```

```python
import functools

import jax
import jax.numpy as jnp
from jax import lax
from jax.experimental import pallas as pl
from jax.experimental.pallas import tpu as pltpu

F32 = jnp.float32
BF16 = jnp.bfloat16

D = 1024
D_FF = 2816
N_HEADS = 16
N_KV = 4
HEAD_DIM = 64
GROUP = N_HEADS // N_KV
QK_DIM = (N_HEADS + N_KV) * HEAD_DIM
QKV_DIM = QK_DIM + N_KV * HEAD_DIM
BLOCK = 128
ROPE_THETA = 10000.0
EPS = 1e-6
DEPTH = 4
N_CHIPS = 4

ADAM_LR = 0.001
ADAM_B1 = 0.9
ADAM_B2 = 0.999
ADAM_EPS = 1e-08
ADAM_WD = 0.01
ADAM_STEP = 10

V7X_VMEM_BYTES = 64 * 1024 * 1024
FF_CHUNK = 1408
MESH = pl.DeviceIdType.MESH
ANY = pl.BlockSpec(memory_space=pl.ANY)


def _cparams(vmem_mb, n_grid):
    assert vmem_mb * 1024 * 1024 <= V7X_VMEM_BYTES
    return pltpu.CompilerParams(vmem_limit_bytes=vmem_mb * 1024 * 1024,
                                dimension_semantics=("arbitrary",) * n_grid)


def _resident(shape):
    nd = len(shape)
    return pl.BlockSpec(shape, lambda *_: (0,) * nd, pipeline_mode=pl.Buffered(1))


def _layer(w, l):
    return pl.BlockSpec((None,) + w.shape[1:], lambda *_: (l, 0, 0), pipeline_mode=pl.Buffered(1))


PIN_BYTES = 1 << 20


def _pin(*args):
    return [pltpu.with_memory_space_constraint(a, pltpu.HBM) if a.size * a.dtype.itemsize >= PIN_BYTES else a
            for a in args]


def _dot(a, b):
    return jnp.dot(a, b, preferred_element_type=F32)


def _dot_nt(a, b):
    return lax.dot_general(a, b, (((1,), (1,)), ((), ())), preferred_element_type=F32)


def _dot_tn(a, b):
    return lax.dot_general(a, b, (((0,), (0,)), ((), ())), preferred_element_type=F32)


def _dot_hilo(a, g):
    hi = a.astype(BF16)
    lo = (a - hi.astype(F32)).astype(BF16)
    return _dot(hi, g) + _dot(lo, g)


def _colsum(a):
    return jnp.sum(a, axis=0, keepdims=True)


def _norm_mod(x, gain, sc, sh):
    r = lax.rsqrt(jnp.mean(x * x, axis=-1, keepdims=True) + EPS)
    n = x * r * gain
    return r, n, n * (1.0 + sc) + sh


def _mod_rows(mod_ref, s):
    return (mod_ref[0, 3 * s:3 * s + 1, :], mod_ref[0, 3 * s + 1:3 * s + 2, :], mod_ref[0, 3 * s + 2:3 * s + 3, :])


def ffn_fwd(x, gain, mod, wup, wdn, l, s, tm=512, after=()):
    t = x.shape[0]
    tpe = t // tm // 2

    def body(x_ref, gain_ref, mod_ref, wup_ref, wdn_ref, *rest):
        xo_ref, u_ref, f_ref = rest[len(after):]
        xv = x_ref[...]
        sh, sc, g = _mod_rows(mod_ref, s)
        _, _, h = _norm_mod(xv, gain_ref[...], sc, sh)
        hb = h.astype(BF16)
        acc = jnp.zeros((tm, D), F32)
        for j in range(D_FF // FF_CHUNK):
            lo, hi = j * FF_CHUNK, (j + 1) * FF_CHUNK
            gate = _dot(hb, wup_ref[:, lo:hi])
            up = _dot(hb, wup_ref[:, D_FF + lo:D_FF + hi])
            u_ref[:, lo:hi] = gate.astype(BF16)
            u_ref[:, D_FF + lo:D_FF + hi] = up.astype(BF16)
            a = (gate * jax.nn.sigmoid(gate) * up).astype(BF16)
            acc = acc + _dot(a, wdn_ref[lo:hi, :])
        f_ref[...] = acc.astype(BF16)
        xo_ref[...] = xv + 0.5 * g * acc

    return pl.pallas_call(
        body, name="ffn_fwd", grid=(t // tm,),
        in_specs=[pl.BlockSpec((tm, D), lambda i: (i, 0)),
                  pl.BlockSpec((1, D), lambda i: (0, 0)),
                  pl.BlockSpec((1, 9, D), lambda i: (i // tpe, 0, 0)),
                  _layer(wup, l), _layer(wdn, l)] + [ANY] * len(after),
        out_specs=[pl.BlockSpec((tm, D), lambda i: (i, 0)),
                   pl.BlockSpec((tm, 2 * D_FF), lambda i: (i, 0)),
                   pl.BlockSpec((tm, D), lambda i: (i, 0))],
        out_shape=[jax.ShapeDtypeStruct((t, D), F32), jax.ShapeDtypeStruct((t, 2 * D_FF), BF16),
                   jax.ShapeDtypeStruct((t, D), BF16)],
        compiler_params=_cparams(60, 1),
    )(*_pin(x, gain, mod, wup, wdn, *after))


def ffn_bwd_act(dy, u, f, mod, wdn, l, s, tm=256, after=()):
    t = dy.shape[0]
    tpe = t // tm // 2

    def body(dy_ref, u_ref, f_ref, mod_ref, wdn_ref, *rest):
        du_ref, a_ref, df_ref, sg_ref = rest[len(after):]
        i = pl.program_id(0)
        dyv = dy_ref[...]
        _, _, g = _mod_rows(mod_ref, s)
        dfb = (0.5 * g * dyv).astype(BF16)
        df_ref[...] = dfb

        @pl.when(i % tpe == 0)
        def _():
            sg_ref[...] = jnp.zeros_like(sg_ref)

        sg_ref[0, 0:1, :] += _colsum(0.5 * dyv * f_ref[...].astype(F32))
        for j in range(D_FF // FF_CHUNK):
            lo, hi = j * FF_CHUNK, (j + 1) * FF_CHUNK
            da = _dot_nt(dfb, wdn_ref[lo:hi, :])
            gate = u_ref[:, lo:hi].astype(F32)
            up = u_ref[:, D_FF + lo:D_FF + hi].astype(F32)
            sg = jax.nn.sigmoid(gate)
            silu = gate * sg
            a_ref[:, lo:hi] = (silu * up).astype(BF16)
            du_ref[:, lo:hi] = (da * up * (sg * (1.0 + gate * (1.0 - sg)))).astype(BF16)
            du_ref[:, D_FF + lo:D_FF + hi] = (da * silu).astype(BF16)

    return pl.pallas_call(
        body, name="ffn_bwd_act", grid=(t // tm,),
        in_specs=[pl.BlockSpec((tm, D), lambda i: (i, 0)),
                  pl.BlockSpec((tm, 2 * D_FF), lambda i: (i, 0)),
                  pl.BlockSpec((tm, D), lambda i: (i, 0)),
                  pl.BlockSpec((1, 9, D), lambda i: (i // tpe, 0, 0)),
                  _layer(wdn, l)] + [ANY] * len(after),
        out_specs=[pl.BlockSpec((tm, 2 * D_FF), lambda i: (i, 0)),
                   pl.BlockSpec((tm, D_FF), lambda i: (i, 0)),
                   pl.BlockSpec((tm, D), lambda i: (i, 0)),
                   pl.BlockSpec((1, 8, D), lambda i: (i // tpe, 0, 0))],
        out_shape=[jax.ShapeDtypeStruct((t, 2 * D_FF), BF16), jax.ShapeDtypeStruct((t, D_FF), BF16),
                   jax.ShapeDtypeStruct((t, D), BF16), jax.ShapeDtypeStruct((2, 8, D), F32)],
        compiler_params=_cparams(40, 1),
    )(*_pin(dy, u, f, mod, wdn, *after))


def lin_bwd(dy, dz, w, l, x, gain, mod, s, want_db, tm=512):
    t = dy.shape[0]
    n = w.shape[2]
    tpe = t // tm // 2
    nck = -(-n // 1536)
    ck = n // nck

    def body(dy_ref, dz_ref, w_ref, x_ref, gain_ref, mod_ref, dx_ref, h_ref, se_ref, sa_ref, *db_ref):
        i = pl.program_id(0)
        xv = x_ref[...]
        gain_v = gain_ref[...]
        sh, sc, _ = _mod_rows(mod_ref, s)
        r, nrm, h = _norm_mod(xv, gain_v, sc, sh)
        h_ref[...] = h.astype(BF16)
        dh = jnp.zeros((tm, D), F32)
        for j in range(nck):
            dh = dh + _dot_nt(dz_ref[:, j * ck:(j + 1) * ck], w_ref[:, j * ck:(j + 1) * ck])
        dn = dh * (1.0 + sc)
        dxr = dn * gain_v
        m = jnp.mean(dxr * xv, axis=-1, keepdims=True)
        dx_ref[...] = dy_ref[...] + r * dxr - xv * (r * r * r) * m

        @pl.when(i % tpe == 0)
        def _():
            se_ref[...] = jnp.zeros_like(se_ref)

        @pl.when(i == 0)
        def _():
            sa_ref[...] = jnp.zeros_like(sa_ref)
            if want_db:
                db_ref[0][...] = jnp.zeros_like(db_ref[0])

        se_ref[0, 0:1, :] += _colsum(dh)
        se_ref[0, 1:2, :] += _colsum(dh * nrm)
        sa_ref[0:1, :] += _colsum(dn * xv * r)
        if want_db:
            db_ref[0][0:1, :] += _colsum(dz_ref[...].astype(F32))

    out_specs = [pl.BlockSpec((tm, D), lambda i: (i, 0)), pl.BlockSpec((tm, D), lambda i: (i, 0)),
                 pl.BlockSpec((1, 8, D), lambda i: (i // tpe, 0, 0)), pl.BlockSpec((8, D), lambda i: (0, 0))]
    out_shape = [jax.ShapeDtypeStruct((t, D), F32), jax.ShapeDtypeStruct((t, D), BF16),
                 jax.ShapeDtypeStruct((2, 8, D), F32), jax.ShapeDtypeStruct((8, D), F32)]
    if want_db:
        out_specs.append(pl.BlockSpec((8, n), lambda i: (0, 0)))
        out_shape.append(jax.ShapeDtypeStruct((8, n), F32))
    return pl.pallas_call(
        body, name="lin_bwd", grid=(t // tm,),
        in_specs=[pl.BlockSpec((tm, D), lambda i: (i, 0)),
                  pl.BlockSpec((tm, n), lambda i: (i, 0)),
                  _layer(w, l),
                  pl.BlockSpec((tm, D), lambda i: (i, 0)),
                  pl.BlockSpec((1, D), lambda i: (0, 0)),
                  pl.BlockSpec((1, 9, D), lambda i: (i // tpe, 0, 0))],
        out_specs=out_specs, out_shape=out_shape,
        compiler_params=_cparams(56, 1),
    )(*_pin(dy, dz, w, x, gain, mod))


def wgrad(gstack, l, a, b, bm, bn, bt=1024):
    t, m = a.shape
    n = b.shape[1]
    nt = t // bt

    def body(g_ref, a_ref, b_ref, o_ref, acc_ref):
        k = pl.program_id(2)

        @pl.when(k == 0)
        def _():
            acc_ref[...] = jnp.zeros_like(acc_ref)

        acc_ref[...] += _dot_tn(a_ref[...], b_ref[...])

        @pl.when(k == nt - 1)
        def _():
            o_ref[...] = acc_ref[...].astype(BF16)

    return pl.pallas_call(
        body, name="wgrad", grid=(m // bm, n // bn, nt),
        in_specs=[ANY, pl.BlockSpec((bt, bm), lambda i, j, k: (k, i)),
                  pl.BlockSpec((bt, bn), lambda i, j, k: (k, j))],
        out_specs=pl.BlockSpec((None, bm, bn), lambda i, j, k: (l, i, j)),
        out_shape=jax.ShapeDtypeStruct(gstack.shape, BF16),
        input_output_aliases={0: 0},
        scratch_shapes=[pltpu.VMEM((bm, bn), F32)],
        compiler_params=_cparams(48, 3),
    )(*_pin(gstack, a, b))


def proj_res(x, o, w, l, b, mod, tm=512):
    t = x.shape[0]
    tpe = t // tm // 2

    def body(x_ref, o_ref, w_ref, b_ref, mod_ref, xo_ref, y_ref):
        _, _, g = _mod_rows(mod_ref, 1)
        y = _dot(o_ref[...], w_ref[...]) + b_ref[...]
        y_ref[...] = y.astype(BF16)
        xo_ref[...] = x_ref[...] + g * y

    return pl.pallas_call(
        body, name="proj_res", grid=(t // tm,),
        in_specs=[pl.BlockSpec((tm, D), lambda i: (i, 0)), pl.BlockSpec((tm, D), lambda i: (i, 0)),
                  _layer(w, l), pl.BlockSpec((1, D), lambda i: (0, 0)),
                  pl.BlockSpec((1, 9, D), lambda i: (i // tpe, 0, 0))],
        out_specs=[pl.BlockSpec((tm, D), lambda i: (i, 0)), pl.BlockSpec((tm, D), lambda i: (i, 0))],
        out_shape=[jax.ShapeDtypeStruct((t, D), F32), jax.ShapeDtypeStruct((t, D), BF16)],
        compiler_params=_cparams(32, 1),
    )(*_pin(x, o, w, b, mod))


def proj_res_bwd(dy, y, w, l, mod, tm=512):
    t = dy.shape[0]
    tpe = t // tm // 2

    def body(dy_ref, y_ref, w_ref, mod_ref, dyy_ref, do_ref, se_ref, sa_ref):
        i = pl.program_id(0)
        _, _, g = _mod_rows(mod_ref, 1)
        dyv = dy_ref[...]
        dyy = g * dyv
        dyb = dyy.astype(BF16)
        dyy_ref[...] = dyb
        do_ref[...] = _dot_nt(dyb, w_ref[...]).astype(BF16)

        @pl.when(i % tpe == 0)
        def _():
            se_ref[...] = jnp.zeros_like(se_ref)

        @pl.when(i == 0)
        def _():
            sa_ref[...] = jnp.zeros_like(sa_ref)

        se_ref[0, 0:1, :] += _colsum(dyv * y_ref[...].astype(F32))
        sa_ref[0:1, :] += _colsum(dyy)

    return pl.pallas_call(
        body, name="proj_res_bwd", grid=(t // tm,),
        in_specs=[pl.BlockSpec((tm, D), lambda i: (i, 0)), pl.BlockSpec((tm, D), lambda i: (i, 0)),
                  _layer(w, l), pl.BlockSpec((1, 9, D), lambda i: (i // tpe, 0, 0))],
        out_specs=[pl.BlockSpec((tm, D), lambda i: (i, 0)), pl.BlockSpec((tm, D), lambda i: (i, 0)),
                   pl.BlockSpec((1, 8, D), lambda i: (i // tpe, 0, 0)), pl.BlockSpec((8, D), lambda i: (0, 0))],
        out_shape=[jax.ShapeDtypeStruct((t, D), BF16), jax.ShapeDtypeStruct((t, D), BF16),
                   jax.ShapeDtypeStruct((2, 8, D), F32), jax.ShapeDtypeStruct((8, D), F32)],
        compiler_params=_cparams(32, 1),
    )(*_pin(dy, y, w, mod))


def lin_fwd(x, gain, mod, w, l, s, tm=512):
    t = x.shape[0]
    n = w.shape[2]
    tpe = t // tm // 2

    def body(x_ref, gain_ref, mod_ref, w_ref, z_ref):
        sh, sc, _ = _mod_rows(mod_ref, s)
        _, _, h = _norm_mod(x_ref[...], gain_ref[...], sc, sh)
        z_ref[...] = _dot(h.astype(BF16), w_ref[...]).astype(BF16)

    return pl.pallas_call(
        body, name="lin_fwd", grid=(t // tm,),
        in_specs=[pl.BlockSpec((tm, D), lambda i: (i, 0)), pl.BlockSpec((1, D), lambda i: (0, 0)),
                  pl.BlockSpec((1, 9, D), lambda i: (i // tpe, 0, 0)), _layer(w, l)],
        out_specs=pl.BlockSpec((tm, n), lambda i: (i, 0)),
        out_shape=jax.ShapeDtypeStruct((t, n), BF16),
        compiler_params=_cparams(40, 1),
    )(*_pin(x, gain, mod, w))


def rope_tables(pos, invf):
    t = pos.shape[0]
    tm = 1024

    def body(pos_ref, invf_ref, c_ref, s_ref):
        ang = pos_ref[...].astype(F32) * invf_ref[...]
        lane = lax.broadcasted_iota(jnp.int32, (tm, 128), 1)
        sign = jnp.where(lane % HEAD_DIM < HEAD_DIM // 2, -1.0, 1.0)
        c_ref[...] = jnp.cos(ang)
        s_ref[...] = sign * jnp.sin(ang)

    return pl.pallas_call(
        body, name="rope_tables", grid=(t // tm,),
        in_specs=[pl.BlockSpec((tm, 1), lambda i: (i, 0)), pl.BlockSpec((1, 128), lambda i: (0, 0))],
        out_specs=[pl.BlockSpec((tm, 128), lambda i: (i, 0))] * 2,
        out_shape=[jax.ShapeDtypeStruct((t, 128), F32)] * 2,
        compiler_params=_cparams(16, 1),
    )(pos, invf)


def _swap_halves(v):
    lane = lax.broadcasted_iota(jnp.int32, v.shape, 1)
    return jnp.where(lane % HEAD_DIM < HEAD_DIM // 2, pltpu.roll(v, 128 - HEAD_DIM // 2, 1), pltpu.roll(v, HEAD_DIM // 2, 1))


def _rope(v, cos, sin):
    return jnp.concatenate(
        [v[:, j:j + 128] * cos + _swap_halves(v[:, j:j + 128]) * sin for j in range(0, v.shape[1], 128)], axis=1)


def _rope_t(dv, cos, sin):
    return jnp.concatenate(
        [dv[:, j:j + 128] * cos + _swap_halves(dv[:, j:j + 128] * sin) for j in range(0, dv.shape[1], 128)], axis=1)


def _head_stats(qk, g1, g2):
    rinv = lax.rsqrt(_dot_hilo(qk * qk, g1) + EPS)
    return rinv, _dot_hilo(rinv, g2)


def qkv_fwd(x, gain, mod, w, l, b, gqk, cos, sin, g1, g2, tm=256):
    t = x.shape[0]
    tpe = t // tm // 2

    def body(x_ref, gain_ref, mod_ref, w_ref, b_ref, gqk_ref, c_ref, s_ref, g1_ref, g2_ref, raw_ref, q_ref, k_ref):
        sh, sc, _ = _mod_rows(mod_ref, 1)
        _, _, h = _norm_mod(x_ref[...], gain_ref[...], sc, sh)
        qkv = _dot(h.astype(BF16), w_ref[...]) + b_ref[...]
        raw_ref[...] = qkv.astype(BF16)
        qk = qkv[:, :QK_DIM]
        _, rb = _head_stats(qk, g1_ref[...], g2_ref[...])
        qr = _rope(qk * rb * gqk_ref[...], c_ref[...], s_ref[...])
        q_ref[...] = qr[:, :D].astype(BF16)
        k_ref[...] = qr[:, D:].astype(BF16)

    return pl.pallas_call(
        body, name="qkv_fwd", grid=(t // tm,),
        in_specs=[pl.BlockSpec((tm, D), lambda i: (i, 0)), pl.BlockSpec((1, D), lambda i: (0, 0)),
                  pl.BlockSpec((1, 9, D), lambda i: (i // tpe, 0, 0)), _layer(w, l),
                  pl.BlockSpec((1, QKV_DIM), lambda i: (0, 0)), pl.BlockSpec((1, QK_DIM), lambda i: (0, 0)),
                  pl.BlockSpec((tm, 128), lambda i: (i, 0)), pl.BlockSpec((tm, 128), lambda i: (i, 0)),
                  _resident((QK_DIM, 128)), _resident((128, QK_DIM))],
        out_specs=[pl.BlockSpec((tm, QKV_DIM), lambda i: (i, 0)), pl.BlockSpec((tm, D), lambda i: (i, 0)),
                   pl.BlockSpec((tm, N_KV * HEAD_DIM), lambda i: (i, 0))],
        out_shape=[jax.ShapeDtypeStruct((t, QKV_DIM), BF16), jax.ShapeDtypeStruct((t, D), BF16),
                   jax.ShapeDtypeStruct((t, N_KV * HEAD_DIM), BF16)],
        compiler_params=_cparams(40, 1),
    )(*_pin(x, gain, mod, w, b, gqk, cos, sin, g1, g2))


def qkv_bwd_pre(dq, dk, dv, raw, gqk, cos, sin, g1, g2, gsel, tm=256):
    t = dq.shape[0]

    def body(dq_ref, dk_ref, dv_ref, raw_ref, gqk_ref, c_ref, s_ref, g1_ref, g2_ref, gsel_ref, dz_ref, sa_ref):
        i = pl.program_id(0)
        dqk = jnp.concatenate([dq_ref[...].astype(F32), dk_ref[...]], axis=1)
        dqn = _rope_t(dqk, c_ref[...], s_ref[...])
        qk = raw_ref[:, :QK_DIM].astype(F32)
        g1v, g2v = g1_ref[...], g2_ref[...]
        rinv, rb = _head_stats(qk, g1v, g2v)
        dgq = jnp.broadcast_to(_colsum(dqn * qk * rb), (8, QK_DIM))
        dyh = dqn * gqk_ref[...]
        mh = _dot_hilo(dyh * qk, g1v)
        mb = _dot_hilo(mh * rinv * rinv * rinv, g2v)
        dz_ref[:, :QK_DIM] = (rb * dyh - qk * mb).astype(BF16)
        dz_ref[:, QK_DIM:] = dv_ref[...].astype(BF16)

        @pl.when(i == 0)
        def _():
            sa_ref[...] = jnp.zeros_like(sa_ref)

        sa_ref[...] += _dot_hilo(dgq, gsel_ref[...])

    kvw = N_KV * HEAD_DIM
    return pl.pallas_call(
        body, name="qkv_bwd_pre", grid=(t // tm,),
        in_specs=[pl.BlockSpec((tm, D), lambda i: (i, 0)), pl.BlockSpec((tm, kvw), lambda i: (i, 0)),
                  pl.BlockSpec((tm, kvw), lambda i: (i, 0)), pl.BlockSpec((tm, QKV_DIM), lambda i: (i, 0)),
                  pl.BlockSpec((1, QK_DIM), lambda i: (0, 0)),
                  pl.BlockSpec((tm, 128), lambda i: (i, 0)), pl.BlockSpec((tm, 128), lambda i: (i, 0)),
                  _resident((QK_DIM, 128)), _resident((128, QK_DIM)), _resident((QK_DIM, 128))],
        out_specs=[pl.BlockSpec((tm, QKV_DIM), lambda i: (i, 0)), pl.BlockSpec((8, 128), lambda i: (0, 0))],
        out_shape=[jax.ShapeDtypeStruct((t, QKV_DIM), BF16), jax.ShapeDtypeStruct((8, 128), F32)],
        compiler_params=_cparams(40, 1),
    )(*_pin(dq, dk, dv, raw, gqk, cos, sin, g1, g2, gsel))


def _band_mask(n):
    row = lax.broadcasted_iota(jnp.int32, (GROUP * BLOCK, 2 * BLOCK), 0) % BLOCK
    col = lax.broadcasted_iota(jnp.int32, (GROUP * BLOCK, 2 * BLOCK), 1)
    rel = row + BLOCK - col
    return (rel >= 0) & (rel < BLOCK) & ((col >= BLOCK) | (n > 0))


def _stack_heads(v, g):
    base = g * GROUP * HEAD_DIM
    return jnp.concatenate([v[:, base + j * HEAD_DIM:base + (j + 1) * HEAD_DIM] for j in range(GROUP)], axis=0)


def _kv_cat(prev, cur, g):
    return jnp.concatenate([prev[:, g * HEAD_DIM:(g + 1) * HEAD_DIM], cur[:, g * HEAD_DIM:(g + 1) * HEAD_DIM]], axis=0)


def _sink_col(sink_ref, g):
    return jnp.concatenate([jnp.full((BLOCK, 1), sink_ref[0, g * GROUP + j], F32) for j in range(GROUP)], axis=0)


def _attn_specs(nb):
    kvw = N_KV * HEAD_DIM
    vcol = QK_DIM // kvw
    cur = lambda e, n: (e * nb + n, 0)
    prev = lambda e, n: (e * nb + jnp.maximum(n - 1, 0), 0)
    return [pl.BlockSpec((BLOCK, D), cur),
            pl.BlockSpec((BLOCK, kvw), cur), pl.BlockSpec((BLOCK, kvw), prev),
            pl.BlockSpec((BLOCK, kvw), lambda e, n: (e * nb + n, vcol)),
            pl.BlockSpec((BLOCK, kvw), lambda e, n: (e * nb + jnp.maximum(n - 1, 0), vcol)),
            pl.BlockSpec(memory_space=pltpu.SMEM)]


def attn_fwd(q, k, raw, sinks):
    t = q.shape[0]
    nb = t // 2 // BLOCK

    def body(q_ref, kc_ref, kp_ref, vc_ref, vp_ref, sink_ref, o_ref, lse_ref):
        n = pl.program_id(1)
        qv = q_ref[...]
        kc, kp, vc, vp = kc_ref[...], kp_ref[...], vc_ref[...], vp_ref[...]
        mask = _band_mask(n)
        outs, lses = [], []
        for g in range(N_KV):
            kk, vv = _kv_cat(kp, kc, g), _kv_cat(vp, vc, g)
            s = jnp.where(mask, _dot_nt(_stack_heads(qv, g), kk) * (HEAD_DIM ** -0.5), -1e30)
            sink = _sink_col(sink_ref, g)
            m = jnp.maximum(jnp.max(s, axis=1, keepdims=True), sink)
            p = jnp.exp(s - m)
            l = jnp.sum(p, axis=1, keepdims=True) + jnp.exp(sink - m)
            o = _dot(p.astype(BF16), vv) / l
            lse = m + jnp.log(l)
            for j in range(GROUP):
                outs.append(o[j * BLOCK:(j + 1) * BLOCK, :])
                lses.append(lse[j * BLOCK:(j + 1) * BLOCK, :])
        o_ref[...] = jnp.concatenate(outs, axis=1).astype(BF16)
        lse_ref[...] = jnp.concatenate(lses, axis=1)

    cur = lambda e, n: (e * nb + n, 0)
    return pl.pallas_call(
        body, name="attn_fwd", grid=(2, nb),
        in_specs=_attn_specs(nb),
        out_specs=[pl.BlockSpec((BLOCK, D), cur), pl.BlockSpec((BLOCK, N_HEADS), cur)],
        out_shape=[jax.ShapeDtypeStruct((t, D), BF16), jax.ShapeDtypeStruct((t, N_HEADS), F32)],
        compiler_params=_cparams(32, 2),
    )(*_pin(q, k, k, raw, raw), sinks)


def attn_bwd(q, k, raw, sinks, o, do, lse):
    t = q.shape[0]
    s_len = t // 2
    nb = s_len // BLOCK
    kvw = N_KV * HEAD_DIM

    def body(q_ref, kc_ref, kp_ref, vc_ref, vp_ref, sink_ref, o_ref, do_ref, lse_ref, dq_ref, dk_ref, dv_ref, ds_ref):
        n = pl.program_id(1)

        @pl.when(n == 0)
        def _():
            dk_ref[...] = jnp.zeros_like(dk_ref)
            dv_ref[...] = jnp.zeros_like(dv_ref)

        @pl.when((n == 0) & (pl.program_id(0) == 0))
        def _():
            ds_ref[...] = jnp.zeros_like(ds_ref)

        qv, ov, dov, lsev = q_ref[...], o_ref[...], do_ref[...], lse_ref[...]
        kc, kp, vc, vp = kc_ref[...], kp_ref[...], vc_ref[...], vp_ref[...]
        mask = _band_mask(n)
        dqs, dks, dvs, dsk = [], [], [], []
        for g in range(N_KV):
            kk, vv = _kv_cat(kp, kc, g), _kv_cat(vp, vc, g)
            qg, og, dog = _stack_heads(qv, g), _stack_heads(ov, g), _stack_heads(dov, g)
            lse = jnp.concatenate([lsev[:, g * GROUP + j:g * GROUP + j + 1] for j in range(GROUP)], axis=0)
            s = jnp.where(mask, _dot_nt(qg, kk) * (HEAD_DIM ** -0.5), -1e30)
            p = jnp.exp(s - lse)
            dd = jnp.sum(dog.astype(F32) * og.astype(F32), axis=1, keepdims=True)
            ds = (p * (_dot_nt(dog, vv) - dd) * (HEAD_DIM ** -0.5)).astype(BF16)
            dqg = _dot(ds, kk)
            dks.append(_dot_tn(ds, qg))
            dvs.append(_dot_tn(p.astype(BF16), dog))
            wsink = jnp.exp(_sink_col(sink_ref, g) - lse) * dd
            for j in range(GROUP):
                dqs.append(dqg[j * BLOCK:(j + 1) * BLOCK, :])
                dsk.append(wsink[j * BLOCK:(j + 1) * BLOCK, :])
        dq_ref[...] = jnp.concatenate(dqs, axis=1).astype(BF16)
        dkk = jnp.concatenate(dks, axis=1)
        dvv = jnp.concatenate(dvs, axis=1)
        prev0 = pl.multiple_of(jnp.maximum(n - 1, 0) * BLOCK, BLOCK)
        cur0 = pl.multiple_of(n * BLOCK, BLOCK)
        dk_ref[pl.ds(prev0, BLOCK), :] += dkk[:BLOCK]
        dv_ref[pl.ds(prev0, BLOCK), :] += dvv[:BLOCK]
        dk_ref[pl.ds(cur0, BLOCK), :] += dkk[BLOCK:]
        dv_ref[pl.ds(cur0, BLOCK), :] += dvv[BLOCK:]
        ds_ref[0:1, :] -= _colsum(jnp.concatenate(dsk, axis=1))

    cur = lambda e, n: (e * nb + n, 0)
    return pl.pallas_call(
        body, name="attn_bwd", grid=(2, nb),
        in_specs=_attn_specs(nb) + [pl.BlockSpec((BLOCK, D), cur), pl.BlockSpec((BLOCK, D), cur),
                                    pl.BlockSpec((BLOCK, N_HEADS), cur)],
        out_specs=[pl.BlockSpec((BLOCK, D), cur), pl.BlockSpec((s_len, kvw), lambda e, n: (e, 0)),
                   pl.BlockSpec((s_len, kvw), lambda e, n: (e, 0)), pl.BlockSpec((8, N_HEADS), lambda e, n: (0, 0))],
        out_shape=[jax.ShapeDtypeStruct((t, D), BF16), jax.ShapeDtypeStruct((t, kvw), F32),
                   jax.ShapeDtypeStruct((t, kvw), F32), jax.ShapeDtypeStruct((8, N_HEADS), F32)],
        compiler_params=_cparams(32, 2),
    )(*_pin(q, k, k, raw, raw), sinks, *_pin(o, do, lse))


CONV_COLS = 256


def _conv_specs(s_len):
    nct = D // CONV_COLS
    return [pl.BlockSpec((s_len, CONV_COLS), lambda j, e: (e, j)),
            pl.BlockSpec((s_len, CONV_COLS), lambda j, e: (e, nct + j)),
            pl.BlockSpec((s_len, CONV_COLS), lambda j, e: (e, 2 * nct + j)),
            pl.BlockSpec((3, CONV_COLS), lambda j, e: (0, j))]


def _conv_taps(gc, v, w, s_len):
    u = gc * v
    row = lax.broadcasted_iota(jnp.int32, u.shape, 0)
    u1 = jnp.where(row >= 1, pltpu.roll(u, 1, 0), 0.0)
    u2 = jnp.where(row >= 2, pltpu.roll(u, 2, 0), 0.0)
    return u, u1, u2, w[2:3, :] * u + w[1:2, :] * u1 + w[0:1, :] * u2


def conv_fwd(z, w):
    t = z.shape[0]
    s_len = t // 2

    def body(gb_ref, gc_ref, v_ref, w_ref, p_ref):
        _, _, _, conv = _conv_taps(gc_ref[...].astype(F32), v_ref[...].astype(F32), w_ref[...], s_len)
        p_ref[...] = (gb_ref[...].astype(F32) * conv).astype(BF16)

    return pl.pallas_call(
        body, name="conv_fwd", grid=(D // CONV_COLS, 2),
        in_specs=_conv_specs(s_len),
        out_specs=pl.BlockSpec((s_len, CONV_COLS), lambda j, e: (e, j)),
        out_shape=jax.ShapeDtypeStruct((t, D), BF16),
        compiler_params=_cparams(40, 2),
    )(*_pin(z, z, z, w))


def conv_bwd(z, w, dp):
    t = z.shape[0]
    s_len = t // 2

    def body(gb_ref, gc_ref, v_ref, w_ref, dp_ref, dgb_ref, dgc_ref, dv_ref, dw_ref):
        e = pl.program_id(1)
        gc, v, wv = gc_ref[...].astype(F32), v_ref[...].astype(F32), w_ref[...]
        u, u1, u2, conv = _conv_taps(gc, v, wv, s_len)
        dpv = dp_ref[...].astype(F32)
        dgb_ref[...] = (dpv * conv).astype(BF16)
        dc = dpv * gb_ref[...].astype(F32)
        row = lax.broadcasted_iota(jnp.int32, dc.shape, 0)
        dc1 = jnp.where(row <= s_len - 2, pltpu.roll(dc, s_len - 1, 0), 0.0)
        dc2 = jnp.where(row <= s_len - 3, pltpu.roll(dc, s_len - 2, 0), 0.0)
        du = wv[2:3, :] * dc + wv[1:2, :] * dc1 + wv[0:1, :] * dc2
        dgc_ref[...] = (du * v).astype(BF16)
        dv_ref[...] = (du * gc).astype(BF16)

        @pl.when(e == 0)
        def _():
            dw_ref[...] = jnp.zeros_like(dw_ref)

        dw_ref[0:1, :] += _colsum(dc * u2)
        dw_ref[1:2, :] += _colsum(dc * u1)
        dw_ref[2:3, :] += _colsum(dc * u)

    blk = pl.BlockSpec((s_len, CONV_COLS), lambda j, e: (e, j))
    return pl.pallas_call(
        body, name="conv_bwd", grid=(D // CONV_COLS, 2),
        in_specs=_conv_specs(s_len) + [blk],
        out_specs=[blk, blk, blk, pl.BlockSpec((8, CONV_COLS), lambda j, e: (0, j))],
        out_shape=[jax.ShapeDtypeStruct((t, D), BF16)] * 3 + [jax.ShapeDtypeStruct((8, D), F32)],
        compiler_params=_cparams(48, 2),
    )(*_pin(z, z, z, w, dp))


def loss_grad(y, tgt, tm=512):
    t = y.shape[0]

    def body(y_ref, t_ref, dy_ref, l_ref):
        i = pl.program_id(0)
        d = y_ref[...] - t_ref[...]
        dy_ref[...] = d * (1.0 / D)

        @pl.when(i == 0)
        def _():
            l_ref[...] = jnp.zeros_like(l_ref)

        l_ref[...] += 0.5 / D * jnp.sum(d * d)

    return pl.pallas_call(
        body, name="loss_grad", grid=(t // tm,),
        in_specs=[pl.BlockSpec((tm, D), lambda i: (i, 0))] * 2,
        out_specs=[pl.BlockSpec((tm, D), lambda i: (i, 0)), pl.BlockSpec((8, 128), lambda i: (0, 0))],
        out_shape=[jax.ShapeDtypeStruct((t, D), F32), jax.ShapeDtypeStruct((8, 128), F32)],
        compiler_params=_cparams(32, 1),
    )(*_pin(y, tgt))


ADA_COLS = 384


def ada_fwd(c_all, w):
    nl, _, n = w.shape
    nex = c_all.shape[0]

    def body(c_ref, w_ref, o_ref):
        cv = c_ref[...]
        ca = (cv * jax.nn.sigmoid(cv)).astype(BF16)
        o_ref[0] = _dot(ca, w_ref[0].astype(BF16))

    return pl.pallas_call(
        body, name="ada_fwd", grid=(nl, n // ADA_COLS),
        in_specs=[pl.BlockSpec((nex, D), lambda l, j: (0, 0)), pl.BlockSpec((1, D, ADA_COLS), lambda l, j: (l, 0, j))],
        out_specs=pl.BlockSpec((1, nex, ADA_COLS), lambda l, j: (l, 0, j)),
        out_shape=jax.ShapeDtypeStruct((nl, nex, n), F32),
        compiler_params=_cparams(32, 2),
    )(*_pin(c_all, w))


def _adam_math(w, g, m, v):
    m = ADAM_B1 * m + (1.0 - ADAM_B1) * g
    v = ADAM_B2 * v + (1.0 - ADAM_B2) * (g * g)
    m_hat = m / (1.0 - ADAM_B1 ** ADAM_STEP)
    v_hat = v / (1.0 - ADAM_B2 ** ADAM_STEP)
    return -ADAM_LR * (m_hat / (jnp.sqrt(v_hat) + ADAM_EPS) + ADAM_WD * w), m, v


def ada_bwd_adam(c_all, dm, w, m, v):
    nl, _, n = w.shape
    nex = c_all.shape[0]

    def body(c_ref, dm_ref, w_ref, m_ref, v_ref, g_ref, d_ref, mo_ref, vo_ref):
        cv = c_ref[...]
        ca = (cv * jax.nn.sigmoid(cv)).astype(BF16)
        g = _dot_tn(ca, dm_ref[0].astype(BF16))
        g_ref[0] = g
        d_ref[0], mo_ref[0], vo_ref[0] = _adam_math(w_ref[0], g, m_ref[0], v_ref[0])

    wspec = pl.BlockSpec((1, D, ADA_COLS), lambda l, j: (l, 0, j))
    return pl.pallas_call(
        body, name="ada_bwd_adam", grid=(nl, n // ADA_COLS),
        in_specs=[pl.BlockSpec((nex, D), lambda l, j: (0, 0)), pl.BlockSpec((1, nex, ADA_COLS), lambda l, j: (l, 0, j)),
                  wspec, wspec, wspec],
        out_specs=[wspec] * 4,
        out_shape=[jax.ShapeDtypeStruct(w.shape, F32)] * 4,
        compiler_params=_cparams(40, 2),
    )(*_pin(c_all, dm, w, m, v))


def adamw(w, g, m, v):
    shape = w.shape
    cols = shape[-1]
    rows = w.size // cols
    args = [a.reshape(rows, cols) for a in (w, g, m, v)]
    tr = rows
    while tr * cols * 4 > (1 << 20) and tr % 16 == 0:
        tr //= 2

    def body(w_ref, g_ref, m_ref, v_ref, d_ref, mo_ref, vo_ref):
        d_ref[...], mo_ref[...], vo_ref[...] = _adam_math(w_ref[...], g_ref[...], m_ref[...], v_ref[...])

    spec = pl.BlockSpec((tr, cols), lambda i: (i, 0))
    outs = pl.pallas_call(
        body, name="adamw", grid=(rows // tr,),
        in_specs=[spec] * 4, out_specs=[spec] * 3,
        out_shape=[jax.ShapeDtypeStruct((rows, cols), F32)] * 3,
        compiler_params=_cparams(32, 1),
    )(*args)
    return [o.reshape(shape) for o in outs]


def adamw_layers(w, g, m, v, prev, l0, n):
    _, r, c = w.shape
    tr = r
    while tr * c * 4 > (1 << 20) and tr % 16 == 0:
        tr //= 2

    def body(w_ref, g_ref, m_ref, v_ref, pd_ref, pm_ref, pv_ref, d_ref, mo_ref, vo_ref):
        d_ref[...], mo_ref[...], vo_ref[...] = _adam_math(w_ref[...], g_ref[...], m_ref[...], v_ref[...])

    spec = pl.BlockSpec((1, tr, c), lambda l, i: (l0 + l, i, 0))
    return pl.pallas_call(
        body, name="adamw_layers", grid=(n, r // tr),
        in_specs=[spec] * 4 + [ANY] * 3, out_specs=[spec] * 3,
        out_shape=[jax.ShapeDtypeStruct(w.shape, F32)] * 3,
        input_output_aliases={4: 0, 5: 1, 6: 2},
        compiler_params=_cparams(32, 2),
    )(*_pin(w, g, m, v, *prev))


def cast_into_window(w, l0, n, ax, chip, after=()):
    _, r, c = w.shape
    tr = r
    while tr * c * 4 > (2 << 20) and tr % 32 == 0:
        tr //= 2
    nrb = r // tr
    full = (n, r * N_CHIPS, c) if ax == 1 else (n, r, c * N_CHIPS)

    def body(chip_ref, w_ref, *rest):
        o_ref = rest[len(after)]
        o_ref[...] = w_ref[...].astype(BF16)

    def omap(l, i, chip_ref):
        return (l, chip_ref[0] * nrb + i, 0) if ax == 1 else (l, i, chip_ref[0])

    return pl.pallas_call(
        body, name="cast_into_window",
        grid_spec=pltpu.PrefetchScalarGridSpec(
            num_scalar_prefetch=1, grid=(n, nrb),
            in_specs=[pl.BlockSpec((1, tr, c), lambda l, i, chip_ref: (l0 + l, i, 0))] + [ANY] * len(after),
            out_specs=pl.BlockSpec((1, tr, c), omap)),
        out_shape=jax.ShapeDtypeStruct(full, BF16), compiler_params=_cparams(32, 2),
    )(chip, *_pin(w, *after))


def add_bias(a, b):
    def body(a_ref, b_ref, o_ref):
        o_ref[...] = a_ref[...] + b_ref[...]

    return pl.pallas_call(body, name="add_bias", out_shape=jax.ShapeDtypeStruct(a.shape, F32))(a, b)


N_DMOD_ROWS = 40


def reduce_small(p_all):
    rows = p_all.shape[1]

    def body(p_ref, red_ref, ex_ref):
        acc = p_ref[0]
        for d in range(1, 8):
            acc = acc + p_ref[d]
        red_ref[...] = acc
        ex_ref[...] = acc[:N_DMOD_ROWS] + acc[N_DMOD_ROWS:2 * N_DMOD_ROWS]

    return pl.pallas_call(
        body, name="reduce_small",
        out_shape=[jax.ShapeDtypeStruct((rows, D), F32), jax.ShapeDtypeStruct((N_DMOD_ROWS, D), F32)],
        compiler_params=_cparams(32, 0),
    )(p_all)


def _place():
    return lax.axis_index("x"), lax.axis_index("y"), lax.axis_index("c")


def _other_chips(x, y):
    return [(1 - x, y), (x, 1 - y), (1 - x, 1 - y)]


def _sl(ref, axis, start, size):
    idx = [slice(None)] * len(ref.shape)
    idx[axis] = pl.ds(pl.multiple_of(start, 16), size)
    return ref.at[tuple(idx)]


def _rcopy(src, dst, send_sem, recv_sem, to):
    return pltpu.make_async_remote_copy(src_ref=src, dst_ref=dst, send_sem=send_sem, recv_sem=recv_sem,
                                        device_id=to, device_id_type=MESH)


def allgather_small(v, all_devices):
    rows, cols = v.shape
    flips = [(dx, dy, dc) for dx in (0, 1) for dy in (0, 1) for dc in (0, 1)
             if (dx, dy, dc) != (0, 0, 0) and (all_devices or dc == 0)]
    n_out = 8 if all_devices else 4

    def body(v_ref, o_ref, send_sems, recv_sems):
        x, y, c = _place()

        def slot(px, py, pc):
            return 4 * px + 2 * py + pc if all_devices else 2 * px + py

        peers = [(1 - x if dx else x, 1 - y if dy else y, 1 - c if dc else c) for dx, dy, dc in flips]
        sends = [_rcopy(v_ref, o_ref.at[slot(x, y, c)], send_sems.at[r], recv_sems.at[r], peer)
                 for r, peer in enumerate(peers)]
        for cp in sends:
            cp.start()
        o_ref[slot(x, y, c)] = v_ref[...]
        for r, peer in enumerate(peers):
            _rcopy(v_ref, o_ref.at[slot(*peer)], send_sems.at[r], recv_sems.at[r], peer).wait_recv()
        for cp in sends:
            cp.wait_send()

    vm = pl.BlockSpec(memory_space=pltpu.VMEM)
    return pl.pallas_call(
        body, name="allgather_small_all" if all_devices else "allgather_small_chips",
        in_specs=[vm], out_specs=vm,
        out_shape=jax.ShapeDtypeStruct((n_out, rows, cols), v.dtype),
        scratch_shapes=[pltpu.SemaphoreType.DMA((len(flips),)), pltpu.SemaphoreType.DMA((len(flips),))],
        compiler_params=pltpu.CompilerParams(vmem_limit_bytes=32 * 1024 * 1024),
    )(v)


HBM = pl.BlockSpec(memory_space=pltpu.HBM)
SEM = pl.BlockSpec(memory_space=pltpu.SEMAPHORE)
SPLIT_COPY = pltpu.CompilerParams(has_side_effects=pltpu.SideEffectType.DATAFLOW_SIDE_EFFECTING)


def _in_hbm(a):
    return pltpu.with_memory_space_constraint(a, pltpu.HBM)


def _window(ref, ax, chip):
    n = ref.shape[ax] // N_CHIPS
    return _sl(ref, ax, (2 * chip[0] + chip[1]) * n, n)


def _half(ref, ax, cc):
    ha = 3 - ax
    hs = ref.shape[ha] // 2
    return _sl(ref, ha, cc * hs, hs)


def gather_start(bufs, axes, tag):
    na = len(bufs)

    def body(*refs):
        ins = refs[:na]
        send_sems, recv_sems = refs[na], refs[na + 1]
        token = refs[2 * na + 2]
        x, y, c = _place()
        for a in range(na):
            mine = _half(_window(ins[a], axes[a], (x, y)), axes[a], c)
            for j, chip in enumerate(_other_chips(x, y)):
                _rcopy(mine, mine, send_sems.at[3 * a + j], recv_sems.at[3 * a + j], (*chip, c)).start()
        token[...] = jnp.zeros_like(token)

    dma = pltpu.SemaphoreType.DMA
    outs = pl.pallas_call(
        body, name="gather_start_" + tag,
        in_specs=[HBM] * na,
        out_specs=(SEM, SEM, *[HBM] * na, pl.BlockSpec(memory_space=pltpu.VMEM)),
        out_shape=(dma((3 * na,)), dma((3 * na,)), *[pltpu.HBM(b.shape, b.dtype) for b in bufs],
                   jax.ShapeDtypeStruct((8, 128), F32)),
        input_output_aliases={a: 2 + a for a in range(na)},
        compiler_params=SPLIT_COPY,
    )(*[_in_hbm(b) for b in bufs])
    return outs[0], outs[1], list(outs[2:2 + na]), outs[2 + na]


def gather_wait(send_sems, recv_sems, bufs, axes, after, tag):
    na = len(bufs)

    def body(*refs):
        ins = refs[:na]
        send_sems, recv_sems = refs[na], refs[na + 1]
        x, y, c = _place()
        for a in range(na):
            for j, chip in enumerate(_other_chips(x, y)):
                got = _half(_window(ins[a], axes[a], chip), axes[a], c)
                _rcopy(got, got, send_sems.at[3 * a + j], recv_sems.at[3 * a + j], (*chip, c)).wait_recv()
        for a in range(na):
            mine = _half(_window(ins[a], axes[a], (x, y)), axes[a], c)
            for j, chip in enumerate(_other_chips(x, y)):
                _rcopy(mine, mine, send_sems.at[3 * a + j], recv_sems.at[3 * a + j], (*chip, c)).wait_send()

    return pl.pallas_call(
        body, name="gather_wait_" + tag,
        in_specs=[HBM] * na + [SEM, SEM] + [ANY] * len(after),
        out_specs=[HBM] * na,
        out_shape=[pltpu.HBM(b.shape, b.dtype) for b in bufs],
        input_output_aliases={a: a for a in range(na)},
        compiler_params=SPLIT_COPY,
    )(*bufs, send_sems, recv_sems, *after)


def gather_forward(bufs, axes):
    na = len(bufs)

    def body(*refs):
        outs = refs[na:2 * na]
        send_sems, recv_sems = refs[2 * na:]
        x, y, c = _place()
        chips = _other_chips(x, y)
        passed = []
        for a in range(na):
            for j, chip in enumerate(chips):
                got = _half(_window(outs[a], axes[a], chip), axes[a], c)
                cp = _rcopy(got, got, send_sems.at[3 * a + j], recv_sems.at[3 * a + j], (x, y, 1 - c))
                cp.start()
                passed.append(cp)
        for a in range(na):
            for j, chip in enumerate(chips):
                got = _half(_window(outs[a], axes[a], chip), axes[a], 1 - c)
                _rcopy(got, got, send_sems.at[3 * a + j], recv_sems.at[3 * a + j], (x, y, 1 - c)).wait_recv()
        for cp in passed:
            cp.wait_send()

    dma = pltpu.SemaphoreType.DMA
    return pl.pallas_call(
        body, name="gather_forward",
        in_specs=[ANY] * na, out_specs=[ANY] * na,
        out_shape=[jax.ShapeDtypeStruct(b.shape, BF16) for b in bufs],
        input_output_aliases={a: a for a in range(na)},
        scratch_shapes=[dma((3 * na,)), dma((3 * na,))],
    )(*bufs)


def exchange_halves(grads, axes):
    na = len(grads)

    def hshape(g, ax):
        ha = 3 - ax
        return tuple(d // 2 if i == ha else d for i, d in enumerate(g.shape))

    def body(*refs):
        ins, outs = refs[:na], refs[na:2 * na]
        send_sems, recv_sems = refs[2 * na:]
        x, y, c = _place()
        cps = []
        for a in range(na):
            ha = 3 - axes[a]
            hs = ins[a].shape[ha] // 2
            cp = _rcopy(_sl(ins[a], ha, (1 - c) * hs, hs), outs[a], send_sems.at[a], recv_sems.at[a], (x, y, 1 - c))
            cp.start()
            cps.append(cp)
        for cp in cps:
            cp.wait_recv()
        for cp in cps:
            cp.wait_send()

    dma = pltpu.SemaphoreType.DMA
    return pl.pallas_call(
        body, name="exchange_halves",
        in_specs=[ANY] * na, out_specs=[ANY] * na,
        out_shape=[jax.ShapeDtypeStruct(hshape(g, ax), BF16) for g, ax in zip(grads, axes)],
        scratch_shapes=[dma((na,)), dma((na,))],
    )(*grads)


def scatter_start(halves, axes, tag):
    na = len(halves)

    def pshape(h, ax):
        return (N_CHIPS - 1,) + tuple(d // N_CHIPS if i == ax else d for i, d in enumerate(h.shape))

    def body(*refs):
        ins, lands = refs[:na], refs[na:2 * na]
        send_sems, recv_sems = refs[2 * na], refs[2 * na + 1]
        token = refs[4 * na + 2]
        x, y, c = _place()
        for a in range(na):
            for j, chip in enumerate(_other_chips(x, y)):
                _rcopy(_window(ins[a], axes[a], chip), lands[a].at[j],
                       send_sems.at[3 * a + j], recv_sems.at[3 * a + j], (*chip, c)).start()
        token[...] = jnp.zeros_like(token)

    dma = pltpu.SemaphoreType.DMA
    lands = [lax.empty(pshape(h, ax), BF16) for h, ax in zip(halves, axes)]
    outs = pl.pallas_call(
        body, name="scatter_start_" + tag,
        in_specs=[HBM] * (2 * na),
        out_specs=(SEM, SEM, *[HBM] * (2 * na), pl.BlockSpec(memory_space=pltpu.VMEM)),
        out_shape=(dma((3 * na,)), dma((3 * na,)), *[pltpu.HBM(b.shape, b.dtype) for b in halves + lands],
                   jax.ShapeDtypeStruct((8, 128), F32)),
        input_output_aliases={a: 2 + a for a in range(2 * na)},
        compiler_params=SPLIT_COPY,
    )(*[_in_hbm(b) for b in halves + lands])
    return outs[0], outs[1], list(outs[2:2 + na]), list(outs[2 + na:2 + 2 * na]), outs[2 + 2 * na]


def scatter_wait(send_sems, recv_sems, halves, lands, axes, after, tag):
    na = len(halves)

    def body(*refs):
        ins, lands = refs[:na], refs[na:2 * na]
        send_sems, recv_sems = refs[2 * na], refs[2 * na + 1]
        x, y, c = _place()
        for a in range(na):
            for j, chip in enumerate(_other_chips(x, y)):
                _rcopy(_window(ins[a], axes[a], chip), lands[a].at[j],
                       send_sems.at[3 * a + j], recv_sems.at[3 * a + j], (*chip, c)).wait_recv()
        for a in range(na):
            for j, chip in enumerate(_other_chips(x, y)):
                _rcopy(_window(ins[a], axes[a], chip), lands[a].at[j],
                       send_sems.at[3 * a + j], recv_sems.at[3 * a + j], (*chip, c)).wait_send()

    outs = pl.pallas_call(
        body, name="scatter_wait_" + tag,
        in_specs=[HBM] * (2 * na) + [SEM, SEM] + [ANY] * len(after),
        out_specs=[HBM] * (2 * na),
        out_shape=[pltpu.HBM(b.shape, b.dtype) for b in halves + lands],
        input_output_aliases={a: a for a in range(2 * na)},
        compiler_params=SPLIT_COPY,
    )(*halves, *lands, send_sems, recv_sems, *after)
    return list(outs[:na]), list(outs[na:])


def join_halves(gs, regions, axes):
    na = len(gs)

    def body(*refs):
        outs = refs[na:2 * na]
        send_sems, recv_sems = refs[2 * na:]
        x, y, c = _place()
        cps = []
        for a in range(na):
            ha = 3 - axes[a]
            hs = outs[a].shape[ha] // 2
            reg = outs[a].at[pl.ds(*regions[a])]
            mine = _sl(reg, ha, c * hs, hs)
            cp = _rcopy(mine, mine, send_sems.at[a], recv_sems.at[a], (x, y, 1 - c))
            cp.start()
            cps.append((cp, _sl(reg, ha, (1 - c) * hs, hs)))
        for a, (cp, theirs) in enumerate(cps):
            _rcopy(theirs, theirs, send_sems.at[a], recv_sems.at[a], (x, y, 1 - c)).wait_recv()
        for cp, _ in cps:
            cp.wait_send()

    dma = pltpu.SemaphoreType.DMA
    return pl.pallas_call(
        body, name="join_halves",
        in_specs=[ANY] * na, out_specs=[ANY] * na,
        out_shape=[jax.ShapeDtypeStruct(g.shape, F32) for g in gs],
        input_output_aliases={a: a for a in range(na)},
        scratch_shapes=[dma((na,)), dma((na,))],
    )(*gs)


def _tile2(r, c, itemsize, limit):
    bc = c
    while bc > 1536:
        bc //= 2
    assert c % bc == 0 and bc % 128 == 0
    br = r
    while br * bc * itemsize > limit and br % 32 == 0:
        br //= 2
    assert r % br == 0 and br % 16 == 0
    return br, bc


def add_my_half(g, theirs, ax, cc):
    ha = 3 - ax
    nl, r, c = theirs.shape
    br, bc = _tile2(r, c, 2, 1 << 20)
    nrb, ncb = r // br, c // bc

    def body(cc_ref, g_ref, t_ref, o_ref):
        o_ref[...] = (g_ref[...].astype(F32) + t_ref[...].astype(F32)).astype(BF16)

    def gmap(l, i, j, cc_ref):
        return (l, cc_ref[0] * nrb + i, j) if ha == 1 else (l, i, cc_ref[0] * ncb + j)

    blk = pl.BlockSpec((1, br, bc), lambda l, i, j, cc_ref: (l, i, j))
    return pl.pallas_call(
        body, name="add_my_half",
        grid_spec=pltpu.PrefetchScalarGridSpec(
            num_scalar_prefetch=1, grid=(nl, nrb, ncb),
            in_specs=[pl.BlockSpec((1, br, bc), gmap), blk], out_specs=blk),
        out_shape=jax.ShapeDtypeStruct(theirs.shape, BF16),
        compiler_params=_cparams(32, 3),
    )(cc, *_pin(g, theirs))


def sum_chips(parts, pair, gstack, l0, ax, where):
    _, n, r, c = parts.shape
    ha = 3 - ax
    br, bc = _tile2(r, c, 2, 1 << 19)
    nrb, ncb = r // br, c // bc

    def body(w_ref, p_ref, own_ref, g_ref, o_ref):
        acc = own_ref[...].astype(F32)
        for q in range(N_CHIPS - 1):
            acc = acc + p_ref[q].astype(F32)
        o_ref[...] = acc

    def own_map(l, i, j, w_ref):
        return (l, w_ref[0] * nrb + i, j) if ax == 1 else (l, i, w_ref[0] * ncb + j)

    def out_map(l, i, j, w_ref):
        return (l0 + l, w_ref[1] * nrb + i, j) if ha == 1 else (l0 + l, i, w_ref[1] * ncb + j)

    return pl.pallas_call(
        body, name="sum_chips",
        grid_spec=pltpu.PrefetchScalarGridSpec(
            num_scalar_prefetch=1, grid=(n, nrb, ncb),
            in_specs=[pl.BlockSpec((N_CHIPS - 1, 1, br, bc), lambda l, i, j, w_ref: (0, l, i, j)),
                      pl.BlockSpec((1, br, bc), own_map), ANY],
            out_specs=pl.BlockSpec((1, br, bc), out_map)),
        out_shape=jax.ShapeDtypeStruct(gstack.shape, F32),
        input_output_aliases={3: 0},
        compiler_params=_cparams(32, 3),
    )(where, *_pin(parts, pair, gstack))


BIG = ("w_ffn_up", "w_ffn_down", "attn_w_qkv", "attn_w_o", "conv_w_in", "conv_w_out")
BIG_AXIS = {"w_ffn_up": 2, "w_ffn_down": 1, "attn_w_qkv": 2, "attn_w_o": 1, "conv_w_in": 2, "conv_w_out": 1}
WEIGHTS = ("norm_gain", "w_ada", "b_ada", "w_ffn_up", "w_ffn_down", "attn_w_qkv", "attn_b_qkv", "attn_q_gain",
           "attn_k_gain", "attn_sinks", "attn_w_o", "attn_b_o", "conv_w_in", "conv_w", "conv_w_out")
N_SMALL_ROWS = 112


def _stack3(a):
    return a.reshape((-1,) + a.shape[-2:])


def _head_matrices():
    lane = jnp.arange(QK_DIM)
    head = lane // HEAD_DIM
    col = jnp.arange(128)
    g1 = jnp.where(head[:, None] == col[None, :], 1.0 / HEAD_DIM, 0.0).astype(BF16)
    g2 = jnp.where(col[:, None] == head[None, :], 1.0, 0.0).astype(BF16)
    fold = lane % HEAD_DIM + jnp.where(head >= N_HEADS, HEAD_DIM, 0)
    gsel = jnp.where(fold[:, None] == col[None, :], 1.0, 0.0).astype(BF16)
    return g1, g2, gsel


def _pad_cols(a, n):
    return jnp.pad(a, ((0, 0), (0, n - a.shape[1])))


def kernel(x, c, positions, norm_gain, w_ada, b_ada, w_ffn_up, w_ffn_down, attn_w_qkv, attn_b_qkv, attn_q_gain, attn_k_gain, attn_sinks, attn_w_o, attn_b_o, conv_w_in, conv_w, conv_w_out, loss_target, m_norm_gain, m_w_ada, m_b_ada, m_w_ffn_up, m_w_ffn_down, m_attn_w_qkv, m_attn_b_qkv, m_attn_q_gain, m_attn_k_gain, m_attn_sinks, m_attn_w_o, m_attn_b_o, m_conv_w_in, m_conv_w, m_conv_w_out, v_norm_gain, v_w_ada, v_b_ada, v_w_ffn_up, v_w_ffn_down, v_attn_w_qkv, v_attn_b_qkv, v_attn_q_gain, v_attn_k_gain, v_attn_sinks, v_attn_w_o, v_attn_b_o, v_conv_w_in, v_conv_w, v_conv_w_out):
    w = dict(norm_gain=norm_gain, w_ada=w_ada, b_ada=b_ada, w_ffn_up=w_ffn_up, w_ffn_down=w_ffn_down,
             attn_w_qkv=attn_w_qkv, attn_b_qkv=attn_b_qkv, attn_q_gain=attn_q_gain, attn_k_gain=attn_k_gain,
             attn_sinks=attn_sinks, attn_w_o=attn_w_o, attn_b_o=attn_b_o, conv_w_in=conv_w_in, conv_w=conv_w,
             conv_w_out=conv_w_out)
    mom = dict(norm_gain=m_norm_gain, w_ada=m_w_ada, b_ada=m_b_ada, w_ffn_up=m_w_ffn_up, w_ffn_down=m_w_ffn_down,
               attn_w_qkv=m_attn_w_qkv, attn_b_qkv=m_attn_b_qkv, attn_q_gain=m_attn_q_gain,
               attn_k_gain=m_attn_k_gain, attn_sinks=m_attn_sinks, attn_w_o=m_attn_w_o, attn_b_o=m_attn_b_o,
               conv_w_in=m_conv_w_in, conv_w=m_conv_w, conv_w_out=m_conv_w_out)
    var = dict(norm_gain=v_norm_gain, w_ada=v_w_ada, b_ada=v_b_ada, w_ffn_up=v_w_ffn_up, w_ffn_down=v_w_ffn_down,
               attn_w_qkv=v_attn_w_qkv, attn_b_qkv=v_attn_b_qkv, attn_q_gain=v_attn_q_gain,
               attn_k_gain=v_attn_k_gain, attn_sinks=v_attn_sinks, attn_w_o=v_attn_w_o, attn_b_o=v_attn_b_o,
               conv_w_in=v_conv_w_in, conv_w=v_conv_w, conv_w_out=v_conv_w_out)

    xi, yi, ci = _place()
    chip = 2 * xi + yi
    dev = 4 * xi + 2 * yi + ci
    nex, s_len, _ = x.shape
    t = nex * s_len
    n_attn, n_conv = attn_w_qkv.shape[0], conv_w_in.shape[0]
    axes = [BIG_AXIS[n] for n in BIG]

    c_all = allgather_small(jnp.pad(c, ((0, 8 - nex), (0, 0))), True)[:, :nex].reshape(8 * nex, D)
    ada_cols = w_ada.shape[2]
    modp = ada_fwd(c_all, w_ada)
    modg = allgather_small(modp.reshape(DEPTH * 8 * nex, ada_cols), False)
    modg = lax.dynamic_slice_in_dim(modg.reshape(N_CHIPS, DEPTH, 8 * nex, ada_cols), dev * nex, nex, axis=2)
    modg = modg.transpose(1, 2, 0, 3).reshape(DEPTH, nex, 9 * D)
    mod = add_bias(modg, b_ada.reshape(DEPTH, 1, 9 * D)).reshape(DEPTH, nex, 9, D)

    small = jnp.concatenate([norm_gain.reshape(DEPTH * 3, -1), conv_w.reshape(n_conv * 3, -1)], axis=0)
    small = jnp.pad(small, ((0, -small.shape[0] % 8), (0, 0)))
    small = allgather_small(small, False).transpose(1, 0, 2).reshape(small.shape[0], D)
    gain_full = small[:DEPTH * 3].reshape(DEPTH, 3, D)
    convw_full = small[DEPTH * 3:DEPTH * 3 + n_conv * 3].reshape(n_conv, 3, D)

    chip_arr = chip.reshape(1).astype(jnp.int32)
    where = jnp.stack([chip, ci]).astype(jnp.int32)
    stacks = [_stack3(w[n]) for n in BIG]

    def mixer(i):
        return (2, 3) if i % 2 == 0 else (4, 5)

    groups = [[(0, 0, 1), (1, 0, 1)], [(mixer(0)[0], 0, 1), (mixer(0)[1], 0, 1), (0, 1, 1), (1, 1, 1)]]
    groups += [[(0, 2 * i, 2), (1, 2 * i, 2), (mixer(i)[0], i // 2, 1), (mixer(i)[1], i // 2, 1)]
               for i in range(1, DEPTH)]
    gaxes = [[axes[b] for b, _, _ in grp] for grp in groups]
    where_is = {(b, l0 + k): (g, a, k) for g, grp in enumerate(groups) for a, (b, l0, n) in enumerate(grp)
                for k in range(n)}
    in_flight, token = [], (mod, small)
    for g, grp in enumerate(groups):
        bufs = [cast_into_window(stacks[b], l0, n, axes[b], chip_arr, token) for b, l0, n in grp]
        ssem, rsem, bufs, tok = gather_start(bufs, gaxes[g], f"g{g}")
        in_flight.append((ssem, rsem, bufs))
        token = (tok,)

    invf = ROPE_THETA ** (-jnp.arange(0, HEAD_DIM, 2, dtype=F32) / HEAD_DIM)
    cos, sin = rope_tables(positions.reshape(t, 1), jnp.tile(invf, 4).reshape(1, 128))
    g1, g2, gsel = _head_matrices()
    gqk = [jnp.concatenate([jnp.tile(attn_q_gain[j], N_HEADS), jnp.tile(attn_k_gain[j], N_KV)]).reshape(1, QK_DIM)
           for j in range(n_attn)]
    zero_bias = jnp.zeros((1, D), F32)

    xs = x.reshape(t, D)
    saved, ready = [], {}

    def weight(b, l, after):
        g, a, k = where_is[(b, l)]
        if g not in ready:
            ssem, rsem, bufs = in_flight[g]
            ready[g] = gather_forward(gather_wait(ssem, rsem, bufs, gaxes[g], after, f"g{g}"), gaxes[g])
        return ready[g][a], k

    for i in range(DEPTH):
        j = i // 2
        gn, md = gain_full[i], mod[i]
        x0 = xs
        wup, k = weight(0, 2 * i, token if i == 0 else (xs,))
        wdn, _ = weight(1, 2 * i, ())
        xs, u1, f1 = ffn_fwd(x0, gn[0:1], md, wup, wdn, k, 0)
        x1 = xs
        wmi, k = weight(mixer(i)[0], j, (xs,))
        wmo, _ = weight(mixer(i)[1], j, ())
        if i % 2 == 0:
            raw, qr, kr = qkv_fwd(x1, gn[1:2], md, wmi, k, attn_b_qkv[j:j + 1], gqk[j], cos, sin, g1, g2)
            o, lse = attn_fwd(qr, kr, raw, attn_sinks[j:j + 1])
            xs, ymix = proj_res(x1, o, wmo, k, attn_b_o[j:j + 1], md)
            mix = (raw, qr, kr, o, lse)
        else:
            z = lin_fwd(x1, gn[1:2], md, wmi, k, 1)
            p = conv_fwd(z, convw_full[j])
            xs, ymix = proj_res(x1, p, wmo, k, zero_bias, md)
            mix = (z, p)
        x2 = xs
        wup, k = weight(0, 2 * i + 1, (xs,))
        wdn, _ = weight(1, 2 * i + 1, ())
        xs, u3, f3 = ffn_fwd(x2, gn[2:3], md, wup, wdn, k, 2)
        saved.append((x0, u1, f1, x1, mix, ymix, x2, u3, f3))
    dy, lpart = loss_grad(xs, loss_target.reshape(t, D))

    cc = ci.reshape(1).astype(jnp.int32)
    gshard = [lax.empty(s.shape, F32) for s in stacks]
    upd = [[lax.empty(s.shape, F32) for _ in range(3)] for s in stacks]
    mstacks = [_stack3(mom[n]) for n in BIG]
    vstacks = [_stack3(var[n]) for n in BIG]

    def finish(g, copies, after):
        ssem, rsem, pair, lands = copies
        pair, lands = scatter_wait(ssem, rsem, pair, lands, gaxes[g], after, f"g{g}")
        grp = groups[g]
        for a, (b, l0, n) in enumerate(grp):
            gshard[b] = sum_chips(lands[a], pair[a], gshard[b], l0, axes[b], where)
        joined = join_halves([gshard[b] for b, _, _ in grp], [(l0, n) for _, l0, n in grp], gaxes[g])
        for (b, l0, n), gj in zip(grp, joined):
            gshard[b] = gj
            upd[b] = adamw_layers(stacks[b], gj, mstacks[b], vstacks[b], upd[b], l0, n)

    ggrad = {g: [lax.empty(buf.shape, BF16) for buf in ready[g]] for g in range(len(groups))}
    missing = {g: sum(n for _, _, n in grp) for g, grp in enumerate(groups)}
    state = dict(flight=None, token=())

    def put(b, l, lhs, rhs, bm, bn):
        g, a, k = where_is[(b, l)]
        ggrad[g][a] = wgrad(ggrad[g][a], k, lhs, rhs, bm, bn)
        missing[g] -= 1
        if missing[g] == 0:
            theirs = exchange_halves(ggrad[g], gaxes[g])
            pair = [add_my_half(gr, th, ax, cc) for gr, th, ax in zip(ggrad[g], theirs, gaxes[g])]
            ssem, rsem, pair, lands, tok = scatter_start(pair, gaxes[g], f"g{g}")
            if state["flight"] is not None:
                finish(*state["flight"], (tok,))
            state.update(flight=(g, (ssem, rsem, pair, lands)), token=(tok,))

    dmod = [None] * DEPTH
    dgain = [None] * DEPTH
    db_qkv, dqk_gain, dsinks, db_o, dconv_w = ([None] * n_attn, [None] * n_attn, [None] * n_attn,
                                                [None] * n_attn, [None] * n_conv)
    for i in reversed(range(DEPTH)):
        j = i // 2
        gn, md = gain_full[i], mod[i]
        x0, u1, f1, x1, mix, ymix, x2, u3, f3 = saved[i]
        (wup, k), (wdn, _) = weight(0, 2 * i + 1, ()), weight(1, 2 * i + 1, ())
        du, a, df, sg3 = ffn_bwd_act(dy, u3, f3, md, wdn, k, 2, after=state["token"])
        dy, h, se3, sa3 = lin_bwd(dy, du, wup, k, x2, gn[2:3], md, 2, False)
        put(0, 2 * i + 1, h, du, D, FF_CHUNK)
        put(1, 2 * i + 1, a, df, FF_CHUNK, D)
        (wmi, k), (wmo, _) = weight(mixer(i)[0], j, ()), weight(mixer(i)[1], j, ())
        if i % 2 == 0:
            raw, qr, kr, o, lse = mix
            dyy, do, sp, sb = proj_res_bwd(dy, ymix, wmo, k, md)
            put(mixer(i)[1], j, o, dyy, D, D)
            dq, dk, dv, dsinks[j] = attn_bwd(qr, kr, raw, attn_sinks[j:j + 1], o, do, lse)
            dz, dqk_gain[j] = qkv_bwd_pre(dq, dk, dv, raw, gqk[j], cos, sin, g1, g2, gsel)
            dy, h, se2, sa2, db_qkv[j] = lin_bwd(dy, dz, wmi, k, x1, gn[1:2], md, 1, True)
            put(mixer(i)[0], j, h, dz, D, QKV_DIM)
            db_o[j] = sb
        else:
            z, p = mix
            dyy, dp, sp, _ = proj_res_bwd(dy, ymix, wmo, k, md)
            put(mixer(i)[1], j, p, dyy, D, D)
            dgb, dgc, dvv, dconv_w[j] = conv_bwd(z, convw_full[j], dp)
            dz = jnp.concatenate([dgb, dgc, dvv], axis=1)
            dy, h, se2, sa2 = lin_bwd(dy, dz, wmi, k, x1, gn[1:2], md, 1, False)
            put(mixer(i)[0], j, h, dz, D, 1536)
        (wup, k), (wdn, _) = weight(0, 2 * i, ()), weight(1, 2 * i, ())
        du, a, df, sg1 = ffn_bwd_act(dy, u1, f1, md, wdn, k, 0, after=state["token"])
        dy, h, se1, sa1 = lin_bwd(dy, du, wup, k, x0, gn[0:1], md, 0, False)
        put(0, 2 * i, h, du, D, FF_CHUNK)
        put(1, 2 * i, a, df, FF_CHUNK, D)
        dmod[i] = jnp.stack([se1[:, 0], se1[:, 1], sg1[:, 0], se2[:, 0], se2[:, 1], sp[:, 0],
                             se3[:, 0], se3[:, 1], sg3[:, 0]], axis=1)
        dgain[i] = jnp.stack([sa1[0], sa2[0], sa3[0]], axis=0)
    grad_x = dy.reshape(x.shape)

    dmod_ex = jnp.stack(dmod, axis=1).reshape(nex, DEPTH * 9, D)
    dmod_ex = jnp.pad(dmod_ex, ((0, 0), (0, N_DMOD_ROWS - DEPTH * 9), (0, 0))).reshape(nex * N_DMOD_ROWS, D)
    misc = jnp.concatenate([dqk_gain[jj][0] for jj in range(n_attn)]
                           + [jnp.pad(dsinks[jj][0], (0, 128 - N_HEADS)) for jj in range(n_attn)]
                           + [lpart[0]])
    rows = [dmod_ex,
            jnp.concatenate(dgain, axis=0), jnp.zeros((4, D), F32),
            jnp.concatenate([_pad_cols(db_qkv[jj][0:1], 2 * D).reshape(2, D) for jj in range(n_attn)], axis=0),
            jnp.concatenate([db_o[jj][0:1] for jj in range(n_attn)], axis=0),
            jnp.concatenate([dconv_w[jj][0:3] for jj in range(n_conv)], axis=0),
            jnp.pad(misc, (0, D - misc.shape[0])).reshape(1, D)]
    packed = jnp.concatenate(rows, axis=0)
    packed = jnp.pad(packed, ((0, N_SMALL_ROWS - packed.shape[0]), (0, 0)))
    p_all = allgather_small(packed, True)
    red, exsum = reduce_small(p_all)

    r0 = nex * N_DMOD_ROWS
    grads = {}
    grads["b_ada"] = exsum[:DEPTH * 9].reshape(DEPTH, 9 * D)
    grads["norm_gain"] = lax.dynamic_slice_in_dim(red[r0:r0 + 12].reshape(DEPTH, 3, D), chip * (D // N_CHIPS),
                                                  D // N_CHIPS, axis=2)
    r1 = r0 + 16
    grads["attn_b_qkv"] = red[r1:r1 + 2 * n_attn].reshape(n_attn, 2 * D)[:, :QKV_DIM]
    r2 = r1 + 2 * n_attn
    grads["attn_b_o"] = red[r2:r2 + n_attn]
    r3 = r2 + n_attn
    grads["conv_w"] = lax.dynamic_slice_in_dim(red[r3:r3 + 3 * n_conv].reshape(n_conv, 3, D), chip * (D // N_CHIPS),
                                               D // N_CHIPS, axis=2)
    mrow = red[r3 + 3 * n_conv]
    grads["attn_q_gain"] = jnp.stack([mrow[128 * jj:128 * jj + HEAD_DIM] for jj in range(n_attn)])
    grads["attn_k_gain"] = jnp.stack([mrow[128 * jj + HEAD_DIM:128 * jj + 128] for jj in range(n_attn)])
    grads["attn_sinks"] = jnp.stack([mrow[128 * (n_attn + jj):128 * (n_attn + jj) + N_HEADS] for jj in range(n_attn)])
    loss = mrow[128 * 2 * n_attn]

    dm_all = p_all[:, :r0].reshape(8, nex, N_DMOD_ROWS, D)[:, :, :DEPTH * 9].reshape(8 * nex, DEPTH, 9 * D)
    dm_mine = lax.dynamic_slice_in_dim(dm_all.transpose(1, 0, 2), chip * ada_cols, ada_cols, axis=2)
    g_ada, d_ada, nm_ada, nv_ada = ada_bwd_adam(c_all, dm_mine, w_ada, m_w_ada, v_w_ada)

    delta, new_m, new_v = {}, {}, {}
    for n in WEIGHTS:
        if n == "w_ada":
            grads[n], delta[n], new_m[n], new_v[n] = g_ada, d_ada, nm_ada, nv_ada
        elif n not in BIG:
            delta[n], new_m[n], new_v[n] = adamw(w[n], grads[n], mom[n], var[n])
    finish(*state["flight"], (delta["conv_w"], d_ada))
    for b, n in enumerate(BIG):
        grads[n] = gshard[b].reshape(w[n].shape)
        delta[n], new_m[n], new_v[n] = (u.reshape(w[n].shape) for u in upd[b])

    return (loss, grad_x, *[grads[n] for n in WEIGHTS], *[delta[n] for n in WEIGHTS],
            *[new_m[n] for n in WEIGHTS], *[new_v[n] for n in WEIGHTS])
```

```python
import functools

import jax
import jax.numpy as jnp
from jax import lax
from jax.experimental import pallas as pl
from jax.experimental.pallas import tpu as pltpu

F32 = jnp.float32
BF16 = jnp.bfloat16

D = 1024
D_FF = 2816
N_HEADS = 16
N_KV = 4
HEAD_DIM = 64
GROUP = N_HEADS // N_KV
QK_DIM = (N_HEADS + N_KV) * HEAD_DIM
QKV_DIM = QK_DIM + N_KV * HEAD_DIM
BLOCK = 128
ROPE_THETA = 10000.0
EPS = 1e-6
DEPTH = 4
N_CHIPS = 4

ADAM_LR = 0.001
ADAM_B1 = 0.9
ADAM_B2 = 0.999
ADAM_EPS = 1e-08
ADAM_WD = 0.01
ADAM_STEP = 10

V7X_VMEM_BYTES = 64 * 1024 * 1024
FF_CHUNK = 1408
MESH = pl.DeviceIdType.MESH
ANY = pl.BlockSpec(memory_space=pl.ANY)


def _cparams(vmem_mb, n_grid):
    assert vmem_mb * 1024 * 1024 <= V7X_VMEM_BYTES
    return pltpu.CompilerParams(vmem_limit_bytes=vmem_mb * 1024 * 1024,
                                dimension_semantics=("arbitrary",) * n_grid)


def _resident(shape):
    nd = len(shape)
    return pl.BlockSpec(shape, lambda *_: (0,) * nd, pipeline_mode=pl.Buffered(1))


def _layer(w, l):
    return pl.BlockSpec((None,) + w.shape[1:], lambda *_: (l, 0, 0), pipeline_mode=pl.Buffered(1))


PIN_BYTES = 1 << 20


def _pin(*args):
    return [pltpu.with_memory_space_constraint(a, pltpu.HBM) if a.size * a.dtype.itemsize >= PIN_BYTES else a
            for a in args]


def _mid(shape, dtype):
    n = 1
    for d in shape:
        n *= d
    if n * jnp.dtype(dtype).itemsize >= PIN_BYTES:
        return pltpu.HBM(tuple(shape), dtype)
    return jax.ShapeDtypeStruct(tuple(shape), dtype)


def _dot(a, b):
    return jnp.dot(a, b, preferred_element_type=F32)


def _dot_nt(a, b):
    return lax.dot_general(a, b, (((1,), (1,)), ((), ())), preferred_element_type=F32)


def _dot_tn(a, b):
    return lax.dot_general(a, b, (((0,), (0,)), ((), ())), preferred_element_type=F32)


def _dot_hilo(a, g):
    hi = a.astype(BF16)
    lo = (a - hi.astype(F32)).astype(BF16)
    return _dot(hi, g) + _dot(lo, g)


def _colsum(a):
    return jnp.sum(a, axis=0, keepdims=True)


def _norm_mod(x, gain, sc, sh):
    r = lax.rsqrt(jnp.mean(x * x, axis=-1, keepdims=True) + EPS)
    n = x * r * gain
    return r, n, n * (1.0 + sc) + sh


def _mod_rows(mod_ref, s):
    return (mod_ref[0, 3 * s:3 * s + 1, :], mod_ref[0, 3 * s + 1:3 * s + 2, :], mod_ref[0, 3 * s + 2:3 * s + 3, :])


def ffn_fwd(x, gain, mod, wup, wdn, l, s, tm=512, after=()):
    t = x.shape[0]
    tpe = t // tm // 2

    def body(x_ref, gain_ref, mod_ref, wup_ref, wdn_ref, *rest):
        xo_ref, u_ref, f_ref = rest[len(after):]
        xv = x_ref[...]
        sh, sc, g = _mod_rows(mod_ref, s)
        _, _, h = _norm_mod(xv, gain_ref[...], sc, sh)
        hb = h.astype(BF16)
        acc = jnp.zeros((tm, D), F32)
        for j in range(D_FF // FF_CHUNK):
            lo, hi = j * FF_CHUNK, (j + 1) * FF_CHUNK
            gate = _dot(hb, wup_ref[:, lo:hi])
            up = _dot(hb, wup_ref[:, D_FF + lo:D_FF + hi])
            u_ref[:, lo:hi] = gate.astype(BF16)
            u_ref[:, D_FF + lo:D_FF + hi] = up.astype(BF16)
            a = (gate * jax.nn.sigmoid(gate) * up).astype(BF16)
            acc = acc + _dot(a, wdn_ref[lo:hi, :])
        f_ref[...] = acc.astype(BF16)
        xo_ref[...] = xv + 0.5 * g * acc

    return pl.pallas_call(
        body, name="ffn_fwd", grid=(t // tm,),
        in_specs=[pl.BlockSpec((tm, D), lambda i: (i, 0)),
                  pl.BlockSpec((1, D), lambda i: (0, 0)),
                  pl.BlockSpec((1, 9, D), lambda i: (i // tpe, 0, 0)),
                  _layer(wup, l), _layer(wdn, l)] + [ANY] * len(after),
        out_specs=[pl.BlockSpec((tm, D), lambda i: (i, 0)),
                   pl.BlockSpec((tm, 2 * D_FF), lambda i: (i, 0)),
                   pl.BlockSpec((tm, D), lambda i: (i, 0))],
        out_shape=[_mid((t, D), F32), _mid((t, 2 * D_FF), BF16), _mid((t, D), BF16)],
        compiler_params=_cparams(60, 1),
    )(*_pin(x, gain, mod, wup, wdn, *after))


def ffn_bwd_act(dy, u, f, mod, wdn, l, s, tm=256, after=()):
    t = dy.shape[0]
    tpe = t // tm // 2

    def body(dy_ref, u_ref, f_ref, mod_ref, wdn_ref, *rest):
        du_ref, a_ref, df_ref, sg_ref = rest[len(after):]
        i = pl.program_id(0)
        dyv = dy_ref[...]
        _, _, g = _mod_rows(mod_ref, s)
        dfb = (0.5 * g * dyv).astype(BF16)
        df_ref[...] = dfb

        @pl.when(i % tpe == 0)
        def _():
            sg_ref[...] = jnp.zeros_like(sg_ref)

        sg_ref[0, 0:1, :] += _colsum(0.5 * dyv * f_ref[...].astype(F32))
        for j in range(D_FF // FF_CHUNK):
            lo, hi = j * FF_CHUNK, (j + 1) * FF_CHUNK
            da = _dot_nt(dfb, wdn_ref[lo:hi, :])
            gate = u_ref[:, lo:hi].astype(F32)
            up = u_ref[:, D_FF + lo:D_FF + hi].astype(F32)
            sg = jax.nn.sigmoid(gate)
            silu = gate * sg
            a_ref[:, lo:hi] = (silu * up).astype(BF16)
            du_ref[:, lo:hi] = (da * up * (sg * (1.0 + gate * (1.0 - sg)))).astype(BF16)
            du_ref[:, D_FF + lo:D_FF + hi] = (da * silu).astype(BF16)

    return pl.pallas_call(
        body, name="ffn_bwd_act", grid=(t // tm,),
        in_specs=[pl.BlockSpec((tm, D), lambda i: (i, 0)),
                  pl.BlockSpec((tm, 2 * D_FF), lambda i: (i, 0)),
                  pl.BlockSpec((tm, D), lambda i: (i, 0)),
                  pl.BlockSpec((1, 9, D), lambda i: (i // tpe, 0, 0)),
                  _layer(wdn, l)] + [ANY] * len(after),
        out_specs=[pl.BlockSpec((tm, 2 * D_FF), lambda i: (i, 0)),
                   pl.BlockSpec((tm, D_FF), lambda i: (i, 0)),
                   pl.BlockSpec((tm, D), lambda i: (i, 0)),
                   pl.BlockSpec((1, 8, D), lambda i: (i // tpe, 0, 0))],
        out_shape=[_mid((t, 2 * D_FF), BF16), _mid((t, D_FF), BF16), _mid((t, D), BF16), _mid((2, 8, D), F32)],
        compiler_params=_cparams(40, 1),
    )(*_pin(dy, u, f, mod, wdn, *after))


def lin_bwd(dy, dz, w, l, x, gain, mod, s, want_db, tm=512):
    t = dy.shape[0]
    n = w.shape[2]
    tpe = t // tm // 2
    nck = -(-n // 1536)
    ck = n // nck

    def body(dy_ref, dz_ref, w_ref, x_ref, gain_ref, mod_ref, dx_ref, h_ref, se_ref, sa_ref, *db_ref):
        i = pl.program_id(0)
        xv = x_ref[...]
        gain_v = gain_ref[...]
        sh, sc, _ = _mod_rows(mod_ref, s)
        r, nrm, h = _norm_mod(xv, gain_v, sc, sh)
        h_ref[...] = h.astype(BF16)
        dh = jnp.zeros((tm, D), F32)
        for j in range(nck):
            dh = dh + _dot_nt(dz_ref[:, j * ck:(j + 1) * ck], w_ref[:, j * ck:(j + 1) * ck])
        dn = dh * (1.0 + sc)
        dxr = dn * gain_v
        m = jnp.mean(dxr * xv, axis=-1, keepdims=True)
        dx_ref[...] = dy_ref[...] + r * dxr - xv * (r * r * r) * m

        @pl.when(i % tpe == 0)
        def _():
            se_ref[...] = jnp.zeros_like(se_ref)

        @pl.when(i == 0)
        def _():
            sa_ref[...] = jnp.zeros_like(sa_ref)
            if want_db:
                db_ref[0][...] = jnp.zeros_like(db_ref[0])

        se_ref[0, 0:1, :] += _colsum(dh)
        se_ref[0, 1:2, :] += _colsum(dh * nrm)
        sa_ref[0:1, :] += _colsum(dn * xv * r)
        if want_db:
            db_ref[0][0:1, :] += _colsum(dz_ref[...].astype(F32))

    out_specs = [pl.BlockSpec((tm, D), lambda i: (i, 0)), pl.BlockSpec((tm, D), lambda i: (i, 0)),
                 pl.BlockSpec((1, 8, D), lambda i: (i // tpe, 0, 0)), pl.BlockSpec((8, D), lambda i: (0, 0))]
    out_shape = [_mid((t, D), F32), _mid((t, D), BF16),
                 jax.ShapeDtypeStruct((2, 8, D), F32), jax.ShapeDtypeStruct((8, D), F32)]
    if want_db:
        out_specs.append(pl.BlockSpec((8, n), lambda i: (0, 0)))
        out_shape.append(jax.ShapeDtypeStruct((8, n), F32))
    return pl.pallas_call(
        body, name="lin_bwd", grid=(t // tm,),
        in_specs=[pl.BlockSpec((tm, D), lambda i: (i, 0)),
                  pl.BlockSpec((tm, n), lambda i: (i, 0)),
                  _layer(w, l),
                  pl.BlockSpec((tm, D), lambda i: (i, 0)),
                  pl.BlockSpec((1, D), lambda i: (0, 0)),
                  pl.BlockSpec((1, 9, D), lambda i: (i // tpe, 0, 0))],
        out_specs=out_specs, out_shape=out_shape,
        compiler_params=_cparams(56, 1),
    )(*_pin(dy, dz, w, x, gain, mod))


def wgrad(gstack, l, a, b, bm, bn, bt=1024):
    t, m = a.shape
    n = b.shape[1]
    nt = t // bt

    def body(g_ref, a_ref, b_ref, o_ref, acc_ref):
        k = pl.program_id(2)

        @pl.when(k == 0)
        def _():
            acc_ref[...] = jnp.zeros_like(acc_ref)

        acc_ref[...] += _dot_tn(a_ref[...], b_ref[...])

        @pl.when(k == nt - 1)
        def _():
            o_ref[...] = acc_ref[...].astype(BF16)

    return pl.pallas_call(
        body, name="wgrad", grid=(m // bm, n // bn, nt),
        in_specs=[ANY, pl.BlockSpec((bt, bm), lambda i, j, k: (k, i)),
                  pl.BlockSpec((bt, bn), lambda i, j, k: (k, j))],
        out_specs=pl.BlockSpec((None, bm, bn), lambda i, j, k: (l, i, j)),
        out_shape=_mid(gstack.shape, BF16),
        input_output_aliases={0: 0},
        scratch_shapes=[pltpu.VMEM((bm, bn), F32)],
        compiler_params=_cparams(48, 3),
    )(*_pin(gstack, a, b))


def proj_res(x, o, w, l, b, mod, tm=512):
    t = x.shape[0]
    tpe = t // tm // 2

    def body(x_ref, o_ref, w_ref, b_ref, mod_ref, xo_ref, y_ref):
        _, _, g = _mod_rows(mod_ref, 1)
        y = _dot(o_ref[...], w_ref[...]) + b_ref[...]
        y_ref[...] = y.astype(BF16)
        xo_ref[...] = x_ref[...] + g * y

    return pl.pallas_call(
        body, name="proj_res", grid=(t // tm,),
        in_specs=[pl.BlockSpec((tm, D), lambda i: (i, 0)), pl.BlockSpec((tm, D), lambda i: (i, 0)),
                  _layer(w, l), pl.BlockSpec((1, D), lambda i: (0, 0)),
                  pl.BlockSpec((1, 9, D), lambda i: (i // tpe, 0, 0))],
        out_specs=[pl.BlockSpec((tm, D), lambda i: (i, 0)), pl.BlockSpec((tm, D), lambda i: (i, 0))],
        out_shape=[_mid((t, D), F32), _mid((t, D), BF16)],
        compiler_params=_cparams(32, 1),
    )(*_pin(x, o, w, b, mod))


def proj_res_bwd(dy, y, w, l, mod, tm=512):
    t = dy.shape[0]
    tpe = t // tm // 2

    def body(dy_ref, y_ref, w_ref, mod_ref, dyy_ref, do_ref, se_ref, sa_ref):
        i = pl.program_id(0)
        _, _, g = _mod_rows(mod_ref, 1)
        dyv = dy_ref[...]
        dyy = g * dyv
        dyb = dyy.astype(BF16)
        dyy_ref[...] = dyb
        do_ref[...] = _dot_nt(dyb, w_ref[...]).astype(BF16)

        @pl.when(i % tpe == 0)
        def _():
            se_ref[...] = jnp.zeros_like(se_ref)

        @pl.when(i == 0)
        def _():
            sa_ref[...] = jnp.zeros_like(sa_ref)

        se_ref[0, 0:1, :] += _colsum(dyv * y_ref[...].astype(F32))
        sa_ref[0:1, :] += _colsum(dyy)

    return pl.pallas_call(
        body, name="proj_res_bwd", grid=(t // tm,),
        in_specs=[pl.BlockSpec((tm, D), lambda i: (i, 0)), pl.BlockSpec((tm, D), lambda i: (i, 0)),
                  _layer(w, l), pl.BlockSpec((1, 9, D), lambda i: (i // tpe, 0, 0))],
        out_specs=[pl.BlockSpec((tm, D), lambda i: (i, 0)), pl.BlockSpec((tm, D), lambda i: (i, 0)),
                   pl.BlockSpec((1, 8, D), lambda i: (i // tpe, 0, 0)), pl.BlockSpec((8, D), lambda i: (0, 0))],
        out_shape=[_mid((t, D), BF16), _mid((t, D), BF16), _mid((2, 8, D), F32), _mid((8, D), F32)],
        compiler_params=_cparams(32, 1),
    )(*_pin(dy, y, w, mod))


def lin_fwd(x, gain, mod, w, l, s, tm=512):
    t = x.shape[0]
    n = w.shape[2]
    tpe = t // tm // 2

    def body(x_ref, gain_ref, mod_ref, w_ref, z_ref):
        sh, sc, _ = _mod_rows(mod_ref, s)
        _, _, h = _norm_mod(x_ref[...], gain_ref[...], sc, sh)
        z_ref[...] = _dot(h.astype(BF16), w_ref[...]).astype(BF16)

    return pl.pallas_call(
        body, name="lin_fwd", grid=(t // tm,),
        in_specs=[pl.BlockSpec((tm, D), lambda i: (i, 0)), pl.BlockSpec((1, D), lambda i: (0, 0)),
                  pl.BlockSpec((1, 9, D), lambda i: (i // tpe, 0, 0)), _layer(w, l)],
        out_specs=pl.BlockSpec((tm, n), lambda i: (i, 0)),
        out_shape=_mid((t, n), BF16),
        compiler_params=_cparams(40, 1),
    )(*_pin(x, gain, mod, w))


def rope_tables(pos, invf):
    t = pos.shape[0]
    tm = 1024

    def body(pos_ref, invf_ref, c_ref, s_ref):
        ang = pos_ref[...].astype(F32) * invf_ref[...]
        lane = lax.broadcasted_iota(jnp.int32, (tm, 128), 1)
        sign = jnp.where(lane % HEAD_DIM < HEAD_DIM // 2, -1.0, 1.0)
        c_ref[...] = jnp.cos(ang)
        s_ref[...] = sign * jnp.sin(ang)

    return pl.pallas_call(
        body, name="rope_tables", grid=(t // tm,),
        in_specs=[pl.BlockSpec((tm, 1), lambda i: (i, 0)), pl.BlockSpec((1, 128), lambda i: (0, 0))],
        out_specs=[pl.BlockSpec((tm, 128), lambda i: (i, 0))] * 2,
        out_shape=[jax.ShapeDtypeStruct((t, 128), F32)] * 2,
        compiler_params=_cparams(16, 1),
    )(pos, invf)


def _swap_halves(v):
    lane = lax.broadcasted_iota(jnp.int32, v.shape, 1)
    return jnp.where(lane % HEAD_DIM < HEAD_DIM // 2, pltpu.roll(v, 128 - HEAD_DIM // 2, 1), pltpu.roll(v, HEAD_DIM // 2, 1))


def _rope(v, cos, sin):
    return jnp.concatenate(
        [v[:, j:j + 128] * cos + _swap_halves(v[:, j:j + 128]) * sin for j in range(0, v.shape[1], 128)], axis=1)


def _rope_t(dv, cos, sin):
    return jnp.concatenate(
        [dv[:, j:j + 128] * cos + _swap_halves(dv[:, j:j + 128] * sin) for j in range(0, dv.shape[1], 128)], axis=1)


def _head_stats(qk, g1, g2):
    rinv = lax.rsqrt(_dot_hilo(qk * qk, g1) + EPS)
    return rinv, _dot_hilo(rinv, g2)


def qkv_fwd(x, gain, mod, w, l, b, gqk, cos, sin, g1, g2, tm=256):
    t = x.shape[0]
    tpe = t // tm // 2

    def body(x_ref, gain_ref, mod_ref, w_ref, b_ref, gqk_ref, c_ref, s_ref, g1_ref, g2_ref, raw_ref, q_ref, k_ref):
        sh, sc, _ = _mod_rows(mod_ref, 1)
        _, _, h = _norm_mod(x_ref[...], gain_ref[...], sc, sh)
        qkv = _dot(h.astype(BF16), w_ref[...]) + b_ref[...]
        raw_ref[...] = qkv.astype(BF16)
        qk = qkv[:, :QK_DIM]
        _, rb = _head_stats(qk, g1_ref[...], g2_ref[...])
        qr = _rope(qk * rb * gqk_ref[...], c_ref[...], s_ref[...])
        q_ref[...] = qr[:, :D].astype(BF16)
        k_ref[...] = qr[:, D:].astype(BF16)

    return pl.pallas_call(
        body, name="qkv_fwd", grid=(t // tm,),
        in_specs=[pl.BlockSpec((tm, D), lambda i: (i, 0)), pl.BlockSpec((1, D), lambda i: (0, 0)),
                  pl.BlockSpec((1, 9, D), lambda i: (i // tpe, 0, 0)), _layer(w, l),
                  pl.BlockSpec((1, QKV_DIM), lambda i: (0, 0)), pl.BlockSpec((1, QK_DIM), lambda i: (0, 0)),
                  pl.BlockSpec((tm, 128), lambda i: (i, 0)), pl.BlockSpec((tm, 128), lambda i: (i, 0)),
                  _resident((QK_DIM, 128)), _resident((128, QK_DIM))],
        out_specs=[pl.BlockSpec((tm, QKV_DIM), lambda i: (i, 0)), pl.BlockSpec((tm, D), lambda i: (i, 0)),
                   pl.BlockSpec((tm, N_KV * HEAD_DIM), lambda i: (i, 0))],
        out_shape=[_mid((t, QKV_DIM), BF16), _mid((t, D), BF16), _mid((t, N_KV * HEAD_DIM), BF16)],
        compiler_params=_cparams(40, 1),
    )(*_pin(x, gain, mod, w, b, gqk, cos, sin, g1, g2))


def qkv_bwd_pre(dq, dk, dv, raw, gqk, cos, sin, g1, g2, gsel, tm=256):
    t = dq.shape[0]

    def body(dq_ref, dk_ref, dv_ref, raw_ref, gqk_ref, c_ref, s_ref, g1_ref, g2_ref, gsel_ref, dz_ref, sa_ref):
        i = pl.program_id(0)
        dqk = jnp.concatenate([dq_ref[...].astype(F32), dk_ref[...]], axis=1)
        dqn = _rope_t(dqk, c_ref[...], s_ref[...])
        qk = raw_ref[:, :QK_DIM].astype(F32)
        g1v, g2v = g1_ref[...], g2_ref[...]
        rinv, rb = _head_stats(qk, g1v, g2v)
        dgq = jnp.broadcast_to(_colsum(dqn * qk * rb), (8, QK_DIM))
        dyh = dqn * gqk_ref[...]
        mh = _dot_hilo(dyh * qk, g1v)
        mb = _dot_hilo(mh * rinv * rinv * rinv, g2v)
        dz_ref[:, :QK_DIM] = (rb * dyh - qk * mb).astype(BF16)
        dz_ref[:, QK_DIM:] = dv_ref[...].astype(BF16)

        @pl.when(i == 0)
        def _():
            sa_ref[...] = jnp.zeros_like(sa_ref)

        sa_ref[...] += _dot_hilo(dgq, gsel_ref[...])

    kvw = N_KV * HEAD_DIM
    return pl.pallas_call(
        body, name="qkv_bwd_pre", grid=(t // tm,),
        in_specs=[pl.BlockSpec((tm, D), lambda i: (i, 0)), pl.BlockSpec((tm, kvw), lambda i: (i, 0)),
                  pl.BlockSpec((tm, kvw), lambda i: (i, 0)), pl.BlockSpec((tm, QKV_DIM), lambda i: (i, 0)),
                  pl.BlockSpec((1, QK_DIM), lambda i: (0, 0)),
                  pl.BlockSpec((tm, 128), lambda i: (i, 0)), pl.BlockSpec((tm, 128), lambda i: (i, 0)),
                  _resident((QK_DIM, 128)), _resident((128, QK_DIM)), _resident((QK_DIM, 128))],
        out_specs=[pl.BlockSpec((tm, QKV_DIM), lambda i: (i, 0)), pl.BlockSpec((8, 128), lambda i: (0, 0))],
        out_shape=[_mid((t, QKV_DIM), BF16), _mid((8, 128), F32)],
        compiler_params=_cparams(40, 1),
    )(*_pin(dq, dk, dv, raw, gqk, cos, sin, g1, g2, gsel))


def _band_mask(n):
    row = lax.broadcasted_iota(jnp.int32, (GROUP * BLOCK, 2 * BLOCK), 0) % BLOCK
    col = lax.broadcasted_iota(jnp.int32, (GROUP * BLOCK, 2 * BLOCK), 1)
    rel = row + BLOCK - col
    return (rel >= 0) & (rel < BLOCK) & ((col >= BLOCK) | (n > 0))


def _stack_heads(v, g):
    base = g * GROUP * HEAD_DIM
    return jnp.concatenate([v[:, base + j * HEAD_DIM:base + (j + 1) * HEAD_DIM] for j in range(GROUP)], axis=0)


def _kv_cat(prev, cur, g):
    return jnp.concatenate([prev[:, g * HEAD_DIM:(g + 1) * HEAD_DIM], cur[:, g * HEAD_DIM:(g + 1) * HEAD_DIM]], axis=0)


def _sink_col(sink_ref, g):
    return jnp.concatenate([jnp.full((BLOCK, 1), sink_ref[0, g * GROUP + j], F32) for j in range(GROUP)], axis=0)


def _attn_specs(nb):
    kvw = N_KV * HEAD_DIM
    vcol = QK_DIM // kvw
    cur = lambda e, n: (e * nb + n, 0)
    prev = lambda e, n: (e * nb + jnp.maximum(n - 1, 0), 0)
    return [pl.BlockSpec((BLOCK, D), cur),
            pl.BlockSpec((BLOCK, kvw), cur), pl.BlockSpec((BLOCK, kvw), prev),
            pl.BlockSpec((BLOCK, kvw), lambda e, n: (e * nb + n, vcol)),
            pl.BlockSpec((BLOCK, kvw), lambda e, n: (e * nb + jnp.maximum(n - 1, 0), vcol)),
            pl.BlockSpec(memory_space=pltpu.SMEM)]


def attn_fwd(q, k, raw, sinks):
    t = q.shape[0]
    nb = t // 2 // BLOCK

    def body(q_ref, kc_ref, kp_ref, vc_ref, vp_ref, sink_ref, o_ref, lse_ref):
        n = pl.program_id(1)
        qv = q_ref[...]
        kc, kp, vc, vp = kc_ref[...], kp_ref[...], vc_ref[...], vp_ref[...]
        mask = _band_mask(n)
        outs, lses = [], []
        for g in range(N_KV):
            kk, vv = _kv_cat(kp, kc, g), _kv_cat(vp, vc, g)
            s = jnp.where(mask, _dot_nt(_stack_heads(qv, g), kk) * (HEAD_DIM ** -0.5), -1e30)
            sink = _sink_col(sink_ref, g)
            m = jnp.maximum(jnp.max(s, axis=1, keepdims=True), sink)
            p = jnp.exp(s - m)
            l = jnp.sum(p, axis=1, keepdims=True) + jnp.exp(sink - m)
            o = _dot(p.astype(BF16), vv) / l
            lse = m + jnp.log(l)
            for j in range(GROUP):
                outs.append(o[j * BLOCK:(j + 1) * BLOCK, :])
                lses.append(lse[j * BLOCK:(j + 1) * BLOCK, :])
        o_ref[...] = jnp.concatenate(outs, axis=1).astype(BF16)
        lse_ref[...] = jnp.concatenate(lses, axis=1)

    cur = lambda e, n: (e * nb + n, 0)
    return pl.pallas_call(
        body, name="attn_fwd", grid=(2, nb),
        in_specs=_attn_specs(nb),
        out_specs=[pl.BlockSpec((BLOCK, D), cur), pl.BlockSpec((BLOCK, N_HEADS), cur)],
        out_shape=[_mid((t, D), BF16), _mid((t, N_HEADS), F32)],
        compiler_params=_cparams(32, 2),
    )(*_pin(q, k, k, raw, raw), sinks)


def attn_bwd(q, k, raw, sinks, o, do, lse):
    t = q.shape[0]
    s_len = t // 2
    nb = s_len // BLOCK
    kvw = N_KV * HEAD_DIM

    def body(q_ref, kc_ref, kp_ref, vc_ref, vp_ref, sink_ref, o_ref, do_ref, lse_ref, dq_ref, dk_ref, dv_ref, ds_ref):
        n = pl.program_id(1)

        @pl.when(n == 0)
        def _():
            dk_ref[...] = jnp.zeros_like(dk_ref)
            dv_ref[...] = jnp.zeros_like(dv_ref)

        @pl.when((n == 0) & (pl.program_id(0) == 0))
        def _():
            ds_ref[...] = jnp.zeros_like(ds_ref)

        qv, ov, dov, lsev = q_ref[...], o_ref[...], do_ref[...], lse_ref[...]
        kc, kp, vc, vp = kc_ref[...], kp_ref[...], vc_ref[...], vp_ref[...]
        mask = _band_mask(n)
        dqs, dks, dvs, dsk = [], [], [], []
        for g in range(N_KV):
            kk, vv = _kv_cat(kp, kc, g), _kv_cat(vp, vc, g)
            qg, og, dog = _stack_heads(qv, g), _stack_heads(ov, g), _stack_heads(dov, g)
            lse = jnp.concatenate([lsev[:, g * GROUP + j:g * GROUP + j + 1] for j in range(GROUP)], axis=0)
            s = jnp.where(mask, _dot_nt(qg, kk) * (HEAD_DIM ** -0.5), -1e30)
            p = jnp.exp(s - lse)
            dd = jnp.sum(dog.astype(F32) * og.astype(F32), axis=1, keepdims=True)
            ds = (p * (_dot_nt(dog, vv) - dd) * (HEAD_DIM ** -0.5)).astype(BF16)
            dqg = _dot(ds, kk)
            dks.append(_dot_tn(ds, qg))
            dvs.append(_dot_tn(p.astype(BF16), dog))
            wsink = jnp.exp(_sink_col(sink_ref, g) - lse) * dd
            for j in range(GROUP):
                dqs.append(dqg[j * BLOCK:(j + 1) * BLOCK, :])
                dsk.append(wsink[j * BLOCK:(j + 1) * BLOCK, :])
        dq_ref[...] = jnp.concatenate(dqs, axis=1).astype(BF16)
        dkk = jnp.concatenate(dks, axis=1)
        dvv = jnp.concatenate(dvs, axis=1)
        prev0 = pl.multiple_of(jnp.maximum(n - 1, 0) * BLOCK, BLOCK)
        cur0 = pl.multiple_of(n * BLOCK, BLOCK)
        dk_ref[pl.ds(prev0, BLOCK), :] += dkk[:BLOCK]
        dv_ref[pl.ds(prev0, BLOCK), :] += dvv[:BLOCK]
        dk_ref[pl.ds(cur0, BLOCK), :] += dkk[BLOCK:]
        dv_ref[pl.ds(cur0, BLOCK), :] += dvv[BLOCK:]
        ds_ref[0:1, :] -= _colsum(jnp.concatenate(dsk, axis=1))

    cur = lambda e, n: (e * nb + n, 0)
    return pl.pallas_call(
        body, name="attn_bwd", grid=(2, nb),
        in_specs=_attn_specs(nb) + [pl.BlockSpec((BLOCK, D), cur), pl.BlockSpec((BLOCK, D), cur),
                                    pl.BlockSpec((BLOCK, N_HEADS), cur)],
        out_specs=[pl.BlockSpec((BLOCK, D), cur), pl.BlockSpec((s_len, kvw), lambda e, n: (e, 0)),
                   pl.BlockSpec((s_len, kvw), lambda e, n: (e, 0)), pl.BlockSpec((8, N_HEADS), lambda e, n: (0, 0))],
        out_shape=[_mid((t, D), BF16), _mid((t, kvw), F32), _mid((t, kvw), F32), _mid((8, N_HEADS), F32)],
        compiler_params=_cparams(32, 2),
    )(*_pin(q, k, k, raw, raw), sinks, *_pin(o, do, lse))


CONV_COLS = 256


def _conv_specs(s_len):
    nct = D // CONV_COLS
    return [pl.BlockSpec((s_len, CONV_COLS), lambda j, e, *_: (e, j)),
            pl.BlockSpec((s_len, CONV_COLS), lambda j, e, *_: (e, nct + j)),
            pl.BlockSpec((s_len, CONV_COLS), lambda j, e, *_: (e, 2 * nct + j)),
            pl.BlockSpec((3, CONV_COLS), lambda j, e, *_: (0, j))]


def _conv_taps(gc, v, w, s_len):
    u = gc * v
    row = lax.broadcasted_iota(jnp.int32, u.shape, 0)
    u1 = jnp.where(row >= 1, pltpu.roll(u, 1, 0), 0.0)
    u2 = jnp.where(row >= 2, pltpu.roll(u, 2, 0), 0.0)
    return u, u1, u2, w[2:3, :] * u + w[1:2, :] * u1 + w[0:1, :] * u2


def conv_fwd(z, w):
    t = z.shape[0]
    s_len = t // 2

    def body(gb_ref, gc_ref, v_ref, w_ref, p_ref):
        _, _, _, conv = _conv_taps(gc_ref[...].astype(F32), v_ref[...].astype(F32), w_ref[...], s_len)
        p_ref[...] = (gb_ref[...].astype(F32) * conv).astype(BF16)

    return pl.pallas_call(
        body, name="conv_fwd", grid=(D // CONV_COLS, 2),
        in_specs=_conv_specs(s_len),
        out_specs=pl.BlockSpec((s_len, CONV_COLS), lambda j, e: (e, j)),
        out_shape=_mid((t, D), BF16),
        compiler_params=_cparams(40, 2),
    )(*_pin(z, z, z, w))


def conv_bwd(z, w, dp):
    t = z.shape[0]
    s_len = t // 2
    nct = D // CONV_COLS

    def body(gb_ref, gc_ref, v_ref, w_ref, dp_ref, dz_ref, dw_ref, parts_ref):
        e, part = pl.program_id(1), pl.program_id(2)

        @pl.when(part == 0)
        def _():
            gc, v, wv = gc_ref[...].astype(F32), v_ref[...].astype(F32), w_ref[...]
            u, u1, u2, conv = _conv_taps(gc, v, wv, s_len)
            dpv = dp_ref[...].astype(F32)
            parts_ref[0] = (dpv * conv).astype(BF16)
            dc = dpv * gb_ref[...].astype(F32)
            row = lax.broadcasted_iota(jnp.int32, dc.shape, 0)
            dc1 = jnp.where(row <= s_len - 2, pltpu.roll(dc, s_len - 1, 0), 0.0)
            dc2 = jnp.where(row <= s_len - 3, pltpu.roll(dc, s_len - 2, 0), 0.0)
            du = wv[2:3, :] * dc + wv[1:2, :] * dc1 + wv[0:1, :] * dc2
            parts_ref[1] = (du * v).astype(BF16)
            parts_ref[2] = (du * gc).astype(BF16)

            @pl.when(e == 0)
            def _():
                dw_ref[...] = jnp.zeros_like(dw_ref)

            dw_ref[0:1, :] += _colsum(dc * u2)
            dw_ref[1:2, :] += _colsum(dc * u1)
            dw_ref[2:3, :] += _colsum(dc * u)

        dz_ref[...] = parts_ref[part]

    return pl.pallas_call(
        body, name="conv_bwd", grid=(nct, 2, 3),
        in_specs=_conv_specs(s_len) + [pl.BlockSpec((s_len, CONV_COLS), lambda j, e, part: (e, j))],
        out_specs=[pl.BlockSpec((s_len, CONV_COLS), lambda j, e, part: (e, part * nct + j)),
                   pl.BlockSpec((8, CONV_COLS), lambda j, e, part: (0, j))],
        out_shape=[_mid((t, 3 * D), BF16), jax.ShapeDtypeStruct((8, D), F32)],
        scratch_shapes=[pltpu.VMEM((3, s_len, CONV_COLS), BF16)],
        compiler_params=_cparams(48, 3),
    )(*_pin(z, z, z, w, dp))


def loss_grad(y, tgt, tm=512):
    t = y.shape[0]

    def body(y_ref, t_ref, dy_ref, l_ref):
        i = pl.program_id(0)
        d = y_ref[...] - t_ref[...]
        dy_ref[...] = d * (1.0 / D)

        @pl.when(i == 0)
        def _():
            l_ref[...] = jnp.zeros_like(l_ref)

        l_ref[...] += 0.5 / D * jnp.sum(d * d)

    return pl.pallas_call(
        body, name="loss_grad", grid=(t // tm,),
        in_specs=[pl.BlockSpec((tm, D), lambda i: (i, 0))] * 2,
        out_specs=[pl.BlockSpec((tm, D), lambda i: (i, 0)), pl.BlockSpec((8, 128), lambda i: (0, 0))],
        out_shape=[_mid((t, D), F32), _mid((8, 128), F32)],
        compiler_params=_cparams(32, 1),
    )(*_pin(y, tgt))


ADA_COLS = 384


def ada_fwd(c_all, w):
    nl, _, n = w.shape
    nex = c_all.shape[0]

    def body(c_ref, w_ref, o_ref):
        cv = c_ref[...]
        ca = (cv * jax.nn.sigmoid(cv)).astype(BF16)
        o_ref[0] = _dot(ca, w_ref[0].astype(BF16))

    return pl.pallas_call(
        body, name="ada_fwd", grid=(nl, n // ADA_COLS),
        in_specs=[pl.BlockSpec((nex, D), lambda l, j: (0, 0)), pl.BlockSpec((1, D, ADA_COLS), lambda l, j: (l, 0, j))],
        out_specs=pl.BlockSpec((1, nex, ADA_COLS), lambda l, j: (l, 0, j)),
        out_shape=jax.ShapeDtypeStruct((nl, nex, n), F32),
        compiler_params=_cparams(32, 2),
    )(*_pin(c_all, w))


def _adam_math(w, g, m, v):
    m = ADAM_B1 * m + (1.0 - ADAM_B1) * g
    v = ADAM_B2 * v + (1.0 - ADAM_B2) * (g * g)
    m_hat = m / (1.0 - ADAM_B1 ** ADAM_STEP)
    v_hat = v / (1.0 - ADAM_B2 ** ADAM_STEP)
    return -ADAM_LR * (m_hat / (jnp.sqrt(v_hat) + ADAM_EPS) + ADAM_WD * w), m, v


def ada_bwd_adam(c_all, dm, w, m, v):
    nl, _, n = w.shape
    nex = c_all.shape[0]

    def body(c_ref, dm_ref, w_ref, m_ref, v_ref, g_ref, d_ref, mo_ref, vo_ref):
        cv = c_ref[...]
        ca = (cv * jax.nn.sigmoid(cv)).astype(BF16)
        g = _dot_tn(ca, dm_ref[0].astype(BF16))
        g_ref[0] = g
        d_ref[0], mo_ref[0], vo_ref[0] = _adam_math(w_ref[0], g, m_ref[0], v_ref[0])

    wspec = pl.BlockSpec((1, D, ADA_COLS), lambda l, j: (l, 0, j))
    return pl.pallas_call(
        body, name="ada_bwd_adam", grid=(nl, n // ADA_COLS),
        in_specs=[pl.BlockSpec((nex, D), lambda l, j: (0, 0)), pl.BlockSpec((1, nex, ADA_COLS), lambda l, j: (l, 0, j)),
                  wspec, wspec, wspec],
        out_specs=[wspec] * 4,
        out_shape=[jax.ShapeDtypeStruct(w.shape, F32)] * 4,
        compiler_params=_cparams(40, 2),
    )(*_pin(c_all, dm, w, m, v))


def adamw(w, g, m, v):
    shape = w.shape
    cols = shape[-1]
    rows = w.size // cols
    args = [a.reshape(rows, cols) for a in (w, g, m, v)]
    tr = rows
    while tr * cols * 4 > (1 << 20) and tr % 16 == 0:
        tr //= 2

    def body(w_ref, g_ref, m_ref, v_ref, d_ref, mo_ref, vo_ref):
        d_ref[...], mo_ref[...], vo_ref[...] = _adam_math(w_ref[...], g_ref[...], m_ref[...], v_ref[...])

    spec = pl.BlockSpec((tr, cols), lambda i: (i, 0))
    outs = pl.pallas_call(
        body, name="adamw", grid=(rows // tr,),
        in_specs=[spec] * 4, out_specs=[spec] * 3,
        out_shape=[jax.ShapeDtypeStruct((rows, cols), F32)] * 3,
        compiler_params=_cparams(32, 1),
    )(*args)
    return [o.reshape(shape) for o in outs]


def adamw_layers(w, g, m, v, prev, l0, n):
    _, r, c = w.shape
    tr = r
    while tr * c * 4 > (1 << 20) and tr % 16 == 0:
        tr //= 2

    def body(w_ref, g_ref, m_ref, v_ref, pd_ref, pm_ref, pv_ref, d_ref, mo_ref, vo_ref):
        d_ref[...], mo_ref[...], vo_ref[...] = _adam_math(w_ref[...], g_ref[...], m_ref[...], v_ref[...])

    spec = pl.BlockSpec((1, tr, c), lambda l, i: (l0 + l, i, 0))
    return pl.pallas_call(
        body, name="adamw_layers", grid=(n, r // tr),
        in_specs=[spec] * 4 + [ANY] * 3, out_specs=[spec] * 3,
        out_shape=[jax.ShapeDtypeStruct(w.shape, F32)] * 3,
        input_output_aliases={4: 0, 5: 1, 6: 2},
        compiler_params=_cparams(32, 2),
    )(*_pin(w, g, m, v, *prev))


def cast_into_window(w, l0, n, ax, chip, after=()):
    _, r, c = w.shape
    tr = r
    while tr * c * 4 > (2 << 20) and tr % 32 == 0:
        tr //= 2
    nrb = r // tr
    full = (n, r * N_CHIPS, c) if ax == 1 else (n, r, c * N_CHIPS)

    def body(chip_ref, w_ref, *rest):
        o_ref = rest[len(after)]
        o_ref[...] = w_ref[...].astype(BF16)

    def omap(l, i, chip_ref):
        return (l, chip_ref[0] * nrb + i, 0) if ax == 1 else (l, i, chip_ref[0])

    return pl.pallas_call(
        body, name="cast_into_window",
        grid_spec=pltpu.PrefetchScalarGridSpec(
            num_scalar_prefetch=1, grid=(n, nrb),
            in_specs=[pl.BlockSpec((1, tr, c), lambda l, i, chip_ref: (l0 + l, i, 0))] + [ANY] * len(after),
            out_specs=pl.BlockSpec((1, tr, c), omap)),
        out_shape=_mid(full, BF16), compiler_params=_cparams(32, 2),
    )(chip, *_pin(w, *after))


def add_bias(a, b):
    def body(a_ref, b_ref, o_ref):
        o_ref[...] = a_ref[...] + b_ref[...]

    return pl.pallas_call(body, name="add_bias", out_shape=jax.ShapeDtypeStruct(a.shape, F32))(a, b)


N_DMOD_ROWS = 40


def reduce_small(p_all):
    rows = p_all.shape[1]

    def body(p_ref, red_ref, ex_ref):
        acc = p_ref[0]
        for d in range(1, 8):
            acc = acc + p_ref[d]
        red_ref[...] = acc
        ex_ref[...] = acc[:N_DMOD_ROWS] + acc[N_DMOD_ROWS:2 * N_DMOD_ROWS]

    return pl.pallas_call(
        body, name="reduce_small",
        out_shape=[jax.ShapeDtypeStruct((rows, D), F32), jax.ShapeDtypeStruct((N_DMOD_ROWS, D), F32)],
        compiler_params=_cparams(32, 0),
    )(p_all)


def _place():
    return lax.axis_index("x"), lax.axis_index("y"), lax.axis_index("c")


def _other_chips(x, y):
    return [(1 - x, y), (x, 1 - y), (1 - x, 1 - y)]


def _sl(ref, axis, start, size):
    idx = [slice(None)] * len(ref.shape)
    idx[axis] = pl.ds(pl.multiple_of(start, 16), size)
    return ref.at[tuple(idx)]


def _rcopy(src, dst, send_sem, recv_sem, to):
    return pltpu.make_async_remote_copy(src_ref=src, dst_ref=dst, send_sem=send_sem, recv_sem=recv_sem,
                                        device_id=to, device_id_type=MESH)


def allgather_small(v, all_devices):
    rows, cols = v.shape
    flips = [(dx, dy, dc) for dx in (0, 1) for dy in (0, 1) for dc in (0, 1)
             if (dx, dy, dc) != (0, 0, 0) and (all_devices or dc == 0)]
    n_out = 8 if all_devices else 4

    def body(v_ref, o_ref, send_sems, recv_sems):
        x, y, c = _place()

        def slot(px, py, pc):
            return 4 * px + 2 * py + pc if all_devices else 2 * px + py

        peers = [(1 - x if dx else x, 1 - y if dy else y, 1 - c if dc else c) for dx, dy, dc in flips]
        sends = [_rcopy(v_ref, o_ref.at[slot(x, y, c)], send_sems.at[r], recv_sems.at[r], peer)
                 for r, peer in enumerate(peers)]
        for cp in sends:
            cp.start()
        o_ref[slot(x, y, c)] = v_ref[...]
        for r, peer in enumerate(peers):
            _rcopy(v_ref, o_ref.at[slot(*peer)], send_sems.at[r], recv_sems.at[r], peer).wait_recv()
        for cp in sends:
            cp.wait_send()

    vm = pl.BlockSpec(memory_space=pltpu.VMEM)
    return pl.pallas_call(
        body, name="allgather_small_all" if all_devices else "allgather_small_chips",
        in_specs=[vm], out_specs=vm,
        out_shape=jax.ShapeDtypeStruct((n_out, rows, cols), v.dtype),
        scratch_shapes=[pltpu.SemaphoreType.DMA((len(flips),)), pltpu.SemaphoreType.DMA((len(flips),))],
        compiler_params=pltpu.CompilerParams(vmem_limit_bytes=32 * 1024 * 1024),
    )(v)


HBM = pl.BlockSpec(memory_space=pltpu.HBM)
SEM = pl.BlockSpec(memory_space=pltpu.SEMAPHORE)
SPLIT_COPY = pltpu.CompilerParams(has_side_effects=pltpu.SideEffectType.DATAFLOW_SIDE_EFFECTING)


def _in_hbm(a):
    return pltpu.with_memory_space_constraint(a, pltpu.HBM)


def _window(ref, ax, chip):
    n = ref.shape[ax] // N_CHIPS
    return _sl(ref, ax, (2 * chip[0] + chip[1]) * n, n)


def _half(ref, ax, cc):
    ha = 3 - ax
    hs = ref.shape[ha] // 2
    return _sl(ref, ha, cc * hs, hs)


def gather_start(bufs, axes, tag):
    na = len(bufs)

    def body(*refs):
        ins = refs[:na]
        send_sems, recv_sems = refs[na], refs[na + 1]
        token = refs[2 * na + 2]
        x, y, c = _place()
        for a in range(na):
            mine = _half(_window(ins[a], axes[a], (x, y)), axes[a], c)
            for j, chip in enumerate(_other_chips(x, y)):
                _rcopy(mine, mine, send_sems.at[3 * a + j], recv_sems.at[3 * a + j], (*chip, c)).start()
        token[...] = jnp.zeros_like(token)

    dma = pltpu.SemaphoreType.DMA
    outs = pl.pallas_call(
        body, name="gather_start_" + tag,
        in_specs=[HBM] * na,
        out_specs=(SEM, SEM, *[HBM] * na, pl.BlockSpec(memory_space=pltpu.VMEM)),
        out_shape=(dma((3 * na,)), dma((3 * na,)), *[pltpu.HBM(b.shape, b.dtype) for b in bufs],
                   jax.ShapeDtypeStruct((8, 128), F32)),
        input_output_aliases={a: 2 + a for a in range(na)},
        compiler_params=SPLIT_COPY,
    )(*[_in_hbm(b) for b in bufs])
    return outs[0], outs[1], list(outs[2:2 + na]), outs[2 + na]


def gather_wait(send_sems, recv_sems, bufs, axes, after, tag):
    na = len(bufs)

    def body(*refs):
        ins = refs[:na]
        send_sems, recv_sems = refs[na], refs[na + 1]
        x, y, c = _place()
        for a in range(na):
            for j, chip in enumerate(_other_chips(x, y)):
                got = _half(_window(ins[a], axes[a], chip), axes[a], c)
                _rcopy(got, got, send_sems.at[3 * a + j], recv_sems.at[3 * a + j], (*chip, c)).wait_recv()
        for a in range(na):
            mine = _half(_window(ins[a], axes[a], (x, y)), axes[a], c)
            for j, chip in enumerate(_other_chips(x, y)):
                _rcopy(mine, mine, send_sems.at[3 * a + j], recv_sems.at[3 * a + j], (*chip, c)).wait_send()

    return pl.pallas_call(
        body, name="gather_wait_" + tag,
        in_specs=[HBM] * na + [SEM, SEM] + [ANY] * len(after),
        out_specs=[HBM] * na,
        out_shape=[pltpu.HBM(b.shape, b.dtype) for b in bufs],
        input_output_aliases={a: a for a in range(na)},
        compiler_params=SPLIT_COPY,
    )(*bufs, send_sems, recv_sems, *after)


def gather_forward(bufs, axes):
    na = len(bufs)

    def body(*refs):
        outs = refs[na:2 * na]
        send_sems, recv_sems = refs[2 * na:]
        x, y, c = _place()
        chips = _other_chips(x, y)
        passed = []
        for a in range(na):
            for j, chip in enumerate(chips):
                got = _half(_window(outs[a], axes[a], chip), axes[a], c)
                cp = _rcopy(got, got, send_sems.at[3 * a + j], recv_sems.at[3 * a + j], (x, y, 1 - c))
                cp.start()
                passed.append(cp)
        for a in range(na):
            for j, chip in enumerate(chips):
                got = _half(_window(outs[a], axes[a], chip), axes[a], 1 - c)
                _rcopy(got, got, send_sems.at[3 * a + j], recv_sems.at[3 * a + j], (x, y, 1 - c)).wait_recv()
        for cp in passed:
            cp.wait_send()

    dma = pltpu.SemaphoreType.DMA
    return pl.pallas_call(
        body, name="gather_forward",
        in_specs=[ANY] * na, out_specs=[ANY] * na,
        out_shape=[jax.ShapeDtypeStruct(b.shape, BF16) for b in bufs],
        input_output_aliases={a: a for a in range(na)},
        scratch_shapes=[dma((3 * na,)), dma((3 * na,))],
    )(*bufs)


def exchange_halves(grads, axes):
    na = len(grads)

    def hshape(g, ax):
        ha = 3 - ax
        return tuple(d // 2 if i == ha else d for i, d in enumerate(g.shape))

    def body(*refs):
        ins, outs = refs[:na], refs[na:2 * na]
        send_sems, recv_sems = refs[2 * na:]
        x, y, c = _place()
        cps = []
        for a in range(na):
            ha = 3 - axes[a]
            hs = ins[a].shape[ha] // 2
            cp = _rcopy(_sl(ins[a], ha, (1 - c) * hs, hs), outs[a], send_sems.at[a], recv_sems.at[a], (x, y, 1 - c))
            cp.start()
            cps.append(cp)
        for cp in cps:
            cp.wait_recv()
        for cp in cps:
            cp.wait_send()

    dma = pltpu.SemaphoreType.DMA
    return pl.pallas_call(
        body, name="exchange_halves",
        in_specs=[ANY] * na, out_specs=[ANY] * na,
        out_shape=[jax.ShapeDtypeStruct(hshape(g, ax), BF16) for g, ax in zip(grads, axes)],
        scratch_shapes=[dma((na,)), dma((na,))],
    )(*grads)


def scatter_start(halves, axes, tag):
    na = len(halves)

    def pshape(h, ax):
        return (N_CHIPS - 1,) + tuple(d // N_CHIPS if i == ax else d for i, d in enumerate(h.shape))

    def body(*refs):
        ins, lands = refs[:na], refs[na:2 * na]
        send_sems, recv_sems = refs[2 * na], refs[2 * na + 1]
        token = refs[4 * na + 2]
        x, y, c = _place()
        for a in range(na):
            for j, chip in enumerate(_other_chips(x, y)):
                _rcopy(_window(ins[a], axes[a], chip), lands[a].at[j],
                       send_sems.at[3 * a + j], recv_sems.at[3 * a + j], (*chip, c)).start()
        token[...] = jnp.zeros_like(token)

    dma = pltpu.SemaphoreType.DMA
    lands = [lax.empty(pshape(h, ax), BF16) for h, ax in zip(halves, axes)]
    outs = pl.pallas_call(
        body, name="scatter_start_" + tag,
        in_specs=[HBM] * (2 * na),
        out_specs=(SEM, SEM, *[HBM] * (2 * na), pl.BlockSpec(memory_space=pltpu.VMEM)),
        out_shape=(dma((3 * na,)), dma((3 * na,)), *[pltpu.HBM(b.shape, b.dtype) for b in halves + lands],
                   jax.ShapeDtypeStruct((8, 128), F32)),
        input_output_aliases={a: 2 + a for a in range(2 * na)},
        compiler_params=SPLIT_COPY,
    )(*[_in_hbm(b) for b in halves + lands])
    return outs[0], outs[1], list(outs[2:2 + na]), list(outs[2 + na:2 + 2 * na]), outs[2 + 2 * na]


def scatter_wait(send_sems, recv_sems, halves, lands, axes, after, tag):
    na = len(halves)

    def body(*refs):
        ins, lands = refs[:na], refs[na:2 * na]
        send_sems, recv_sems = refs[2 * na], refs[2 * na + 1]
        x, y, c = _place()
        for a in range(na):
            for j, chip in enumerate(_other_chips(x, y)):
                _rcopy(_window(ins[a], axes[a], chip), lands[a].at[j],
                       send_sems.at[3 * a + j], recv_sems.at[3 * a + j], (*chip, c)).wait_recv()
        for a in range(na):
            for j, chip in enumerate(_other_chips(x, y)):
                _rcopy(_window(ins[a], axes[a], chip), lands[a].at[j],
                       send_sems.at[3 * a + j], recv_sems.at[3 * a + j], (*chip, c)).wait_send()

    outs = pl.pallas_call(
        body, name="scatter_wait_" + tag,
        in_specs=[HBM] * (2 * na) + [SEM, SEM] + [ANY] * len(after),
        out_specs=[HBM] * (2 * na),
        out_shape=[pltpu.HBM(b.shape, b.dtype) for b in halves + lands],
        input_output_aliases={a: a for a in range(2 * na)},
        compiler_params=SPLIT_COPY,
    )(*halves, *lands, send_sems, recv_sems, *after)
    return list(outs[:na]), list(outs[na:])


def join_halves(gs, regions, axes):
    na = len(gs)

    def body(*refs):
        outs = refs[na:2 * na]
        send_sems, recv_sems = refs[2 * na:]
        x, y, c = _place()
        cps = []
        for a in range(na):
            ha = 3 - axes[a]
            hs = outs[a].shape[ha] // 2
            reg = outs[a].at[pl.ds(*regions[a])]
            mine = _sl(reg, ha, c * hs, hs)
            cp = _rcopy(mine, mine, send_sems.at[a], recv_sems.at[a], (x, y, 1 - c))
            cp.start()
            cps.append((cp, _sl(reg, ha, (1 - c) * hs, hs)))
        for a, (cp, theirs) in enumerate(cps):
            _rcopy(theirs, theirs, send_sems.at[a], recv_sems.at[a], (x, y, 1 - c)).wait_recv()
        for cp, _ in cps:
            cp.wait_send()

    dma = pltpu.SemaphoreType.DMA
    return pl.pallas_call(
        body, name="join_halves",
        in_specs=[ANY] * na, out_specs=[ANY] * na,
        out_shape=[jax.ShapeDtypeStruct(g.shape, F32) for g in gs],
        input_output_aliases={a: a for a in range(na)},
        scratch_shapes=[dma((na,)), dma((na,))],
    )(*gs)


def _tile2(r, c, itemsize, limit):
    bc = c
    while bc > 1536:
        bc //= 2
    assert c % bc == 0 and bc % 128 == 0
    br = r
    while br * bc * itemsize > limit and br % 32 == 0:
        br //= 2
    assert r % br == 0 and br % 16 == 0
    return br, bc


def add_my_half(g, theirs, ax, cc):
    ha = 3 - ax
    nl, r, c = theirs.shape
    br, bc = _tile2(r, c, 2, 1 << 20)
    nrb, ncb = r // br, c // bc

    def body(cc_ref, g_ref, t_ref, o_ref):
        o_ref[...] = (g_ref[...].astype(F32) + t_ref[...].astype(F32)).astype(BF16)

    def gmap(l, i, j, cc_ref):
        return (l, cc_ref[0] * nrb + i, j) if ha == 1 else (l, i, cc_ref[0] * ncb + j)

    blk = pl.BlockSpec((1, br, bc), lambda l, i, j, cc_ref: (l, i, j))
    return pl.pallas_call(
        body, name="add_my_half",
        grid_spec=pltpu.PrefetchScalarGridSpec(
            num_scalar_prefetch=1, grid=(nl, nrb, ncb),
            in_specs=[pl.BlockSpec((1, br, bc), gmap), blk], out_specs=blk),
        out_shape=_mid(theirs.shape, BF16),
        compiler_params=_cparams(32, 3),
    )(cc, *_pin(g, theirs))


def sum_chips(parts, pair, gstack, l0, ax, where):
    _, n, r, c = parts.shape
    ha = 3 - ax
    br, bc = _tile2(r, c, 2, 1 << 19)
    nrb, ncb = r // br, c // bc

    def body(w_ref, p_ref, own_ref, g_ref, o_ref):
        acc = own_ref[...].astype(F32)
        for q in range(N_CHIPS - 1):
            acc = acc + p_ref[q].astype(F32)
        o_ref[...] = acc

    def own_map(l, i, j, w_ref):
        return (l, w_ref[0] * nrb + i, j) if ax == 1 else (l, i, w_ref[0] * ncb + j)

    def out_map(l, i, j, w_ref):
        return (l0 + l, w_ref[1] * nrb + i, j) if ha == 1 else (l0 + l, i, w_ref[1] * ncb + j)

    return pl.pallas_call(
        body, name="sum_chips",
        grid_spec=pltpu.PrefetchScalarGridSpec(
            num_scalar_prefetch=1, grid=(n, nrb, ncb),
            in_specs=[pl.BlockSpec((N_CHIPS - 1, 1, br, bc), lambda l, i, j, w_ref: (0, l, i, j)),
                      pl.BlockSpec((1, br, bc), own_map), ANY],
            out_specs=pl.BlockSpec((1, br, bc), out_map)),
        out_shape=jax.ShapeDtypeStruct(gstack.shape, F32),
        input_output_aliases={3: 0},
        compiler_params=_cparams(32, 3),
    )(where, *_pin(parts, pair, gstack))


BIG = ("w_ffn_up", "w_ffn_down", "attn_w_qkv", "attn_w_o", "conv_w_in", "conv_w_out")
BIG_AXIS = {"w_ffn_up": 2, "w_ffn_down": 1, "attn_w_qkv": 2, "attn_w_o": 1, "conv_w_in": 2, "conv_w_out": 1}
WEIGHTS = ("norm_gain", "w_ada", "b_ada", "w_ffn_up", "w_ffn_down", "attn_w_qkv", "attn_b_qkv", "attn_q_gain",
           "attn_k_gain", "attn_sinks", "attn_w_o", "attn_b_o", "conv_w_in", "conv_w", "conv_w_out")
N_SMALL_ROWS = 112


def _stack3(a):
    return a.reshape((-1,) + a.shape[-2:])


def _head_matrices():
    lane = jnp.arange(QK_DIM)
    head = lane // HEAD_DIM
    col = jnp.arange(128)
    g1 = jnp.where(head[:, None] == col[None, :], 1.0 / HEAD_DIM, 0.0).astype(BF16)
    g2 = jnp.where(col[:, None] == head[None, :], 1.0, 0.0).astype(BF16)
    fold = lane % HEAD_DIM + jnp.where(head >= N_HEADS, HEAD_DIM, 0)
    gsel = jnp.where(fold[:, None] == col[None, :], 1.0, 0.0).astype(BF16)
    return g1, g2, gsel


def _pad_cols(a, n):
    return jnp.pad(a, ((0, 0), (0, n - a.shape[1])))


def kernel(x, c, positions, norm_gain, w_ada, b_ada, w_ffn_up, w_ffn_down, attn_w_qkv, attn_b_qkv, attn_q_gain, attn_k_gain, attn_sinks, attn_w_o, attn_b_o, conv_w_in, conv_w, conv_w_out, loss_target, m_norm_gain, m_w_ada, m_b_ada, m_w_ffn_up, m_w_ffn_down, m_attn_w_qkv, m_attn_b_qkv, m_attn_q_gain, m_attn_k_gain, m_attn_sinks, m_attn_w_o, m_attn_b_o, m_conv_w_in, m_conv_w, m_conv_w_out, v_norm_gain, v_w_ada, v_b_ada, v_w_ffn_up, v_w_ffn_down, v_attn_w_qkv, v_attn_b_qkv, v_attn_q_gain, v_attn_k_gain, v_attn_sinks, v_attn_w_o, v_attn_b_o, v_conv_w_in, v_conv_w, v_conv_w_out):
    w = dict(norm_gain=norm_gain, w_ada=w_ada, b_ada=b_ada, w_ffn_up=w_ffn_up, w_ffn_down=w_ffn_down,
             attn_w_qkv=attn_w_qkv, attn_b_qkv=attn_b_qkv, attn_q_gain=attn_q_gain, attn_k_gain=attn_k_gain,
             attn_sinks=attn_sinks, attn_w_o=attn_w_o, attn_b_o=attn_b_o, conv_w_in=conv_w_in, conv_w=conv_w,
             conv_w_out=conv_w_out)
    mom = dict(norm_gain=m_norm_gain, w_ada=m_w_ada, b_ada=m_b_ada, w_ffn_up=m_w_ffn_up, w_ffn_down=m_w_ffn_down,
               attn_w_qkv=m_attn_w_qkv, attn_b_qkv=m_attn_b_qkv, attn_q_gain=m_attn_q_gain,
               attn_k_gain=m_attn_k_gain, attn_sinks=m_attn_sinks, attn_w_o=m_attn_w_o, attn_b_o=m_attn_b_o,
               conv_w_in=m_conv_w_in, conv_w=m_conv_w, conv_w_out=m_conv_w_out)
    var = dict(norm_gain=v_norm_gain, w_ada=v_w_ada, b_ada=v_b_ada, w_ffn_up=v_w_ffn_up, w_ffn_down=v_w_ffn_down,
               attn_w_qkv=v_attn_w_qkv, attn_b_qkv=v_attn_b_qkv, attn_q_gain=v_attn_q_gain,
               attn_k_gain=v_attn_k_gain, attn_sinks=v_attn_sinks, attn_w_o=v_attn_w_o, attn_b_o=v_attn_b_o,
               conv_w_in=v_conv_w_in, conv_w=v_conv_w, conv_w_out=v_conv_w_out)

    xi, yi, ci = _place()
    chip = 2 * xi + yi
    dev = 4 * xi + 2 * yi + ci
    nex, s_len, _ = x.shape
    t = nex * s_len
    n_attn, n_conv = attn_w_qkv.shape[0], conv_w_in.shape[0]
    axes = [BIG_AXIS[n] for n in BIG]

    c_all = allgather_small(jnp.pad(c, ((0, 8 - nex), (0, 0))), True)[:, :nex].reshape(8 * nex, D)
    ada_cols = w_ada.shape[2]
    modp = ada_fwd(c_all, w_ada)
    modg = allgather_small(modp.reshape(DEPTH * 8 * nex, ada_cols), False)
    modg = lax.dynamic_slice_in_dim(modg.reshape(N_CHIPS, DEPTH, 8 * nex, ada_cols), dev * nex, nex, axis=2)
    modg = modg.transpose(1, 2, 0, 3).reshape(DEPTH, nex, 9 * D)
    mod = add_bias(modg, b_ada.reshape(DEPTH, 1, 9 * D)).reshape(DEPTH, nex, 9, D)

    small = jnp.concatenate([norm_gain.reshape(DEPTH * 3, -1), conv_w.reshape(n_conv * 3, -1)], axis=0)
    small = jnp.pad(small, ((0, -small.shape[0] % 8), (0, 0)))
    small = allgather_small(small, False).transpose(1, 0, 2).reshape(small.shape[0], D)
    gain_full = small[:DEPTH * 3].reshape(DEPTH, 3, D)
    convw_full = small[DEPTH * 3:DEPTH * 3 + n_conv * 3].reshape(n_conv, 3, D)

    chip_arr = chip.reshape(1).astype(jnp.int32)
    where = jnp.stack([chip, ci]).astype(jnp.int32)
    stacks = [_stack3(w[n]) for n in BIG]

    def mixer(i):
        return (2, 3) if i % 2 == 0 else (4, 5)

    groups = [[(0, 0, 1), (1, 0, 1)], [(mixer(0)[0], 0, 1), (mixer(0)[1], 0, 1), (0, 1, 1), (1, 1, 1)]]
    groups += [[(0, 2 * i, 2), (1, 2 * i, 2), (mixer(i)[0], i // 2, 1), (mixer(i)[1], i // 2, 1)]
               for i in range(1, DEPTH)]
    gaxes = [[axes[b] for b, _, _ in grp] for grp in groups]
    where_is = {(b, l0 + k): (g, a, k) for g, grp in enumerate(groups) for a, (b, l0, n) in enumerate(grp)
                for k in range(n)}
    in_flight, token = [], (mod, small)
    for g, grp in enumerate(groups):
        bufs = [cast_into_window(stacks[b], l0, n, axes[b], chip_arr, token) for b, l0, n in grp]
        ssem, rsem, bufs, tok = gather_start(bufs, gaxes[g], f"g{g}")
        in_flight.append((ssem, rsem, bufs))
        token = (tok,)

    invf = ROPE_THETA ** (-jnp.arange(0, HEAD_DIM, 2, dtype=F32) / HEAD_DIM)
    cos, sin = rope_tables(positions.reshape(t, 1), jnp.tile(invf, 4).reshape(1, 128))
    g1, g2, gsel = _head_matrices()
    gqk = [jnp.concatenate([jnp.tile(attn_q_gain[j], N_HEADS), jnp.tile(attn_k_gain[j], N_KV)]).reshape(1, QK_DIM)
           for j in range(n_attn)]
    zero_bias = jnp.zeros((1, D), F32)

    xs = x.reshape(t, D)
    saved, ready = [], {}

    def weight(b, l, after):
        g, a, k = where_is[(b, l)]
        if g not in ready:
            ssem, rsem, bufs = in_flight[g]
            ready[g] = gather_forward(gather_wait(ssem, rsem, bufs, gaxes[g], after, f"g{g}"), gaxes[g])
        return ready[g][a], k

    for i in range(DEPTH):
        j = i // 2
        gn, md = gain_full[i], mod[i]
        x0 = xs
        wup, k = weight(0, 2 * i, token if i == 0 else (xs,))
        wdn, _ = weight(1, 2 * i, ())
        xs, u1, f1 = ffn_fwd(x0, gn[0:1], md, wup, wdn, k, 0)
        x1 = xs
        wmi, k = weight(mixer(i)[0], j, (xs,))
        wmo, _ = weight(mixer(i)[1], j, ())
        if i % 2 == 0:
            raw, qr, kr = qkv_fwd(x1, gn[1:2], md, wmi, k, attn_b_qkv[j:j + 1], gqk[j], cos, sin, g1, g2)
            o, lse = attn_fwd(qr, kr, raw, attn_sinks[j:j + 1])
            xs, ymix = proj_res(x1, o, wmo, k, attn_b_o[j:j + 1], md)
            mix = (raw, qr, kr, o, lse)
        else:
            z = lin_fwd(x1, gn[1:2], md, wmi, k, 1)
            p = conv_fwd(z, convw_full[j])
            xs, ymix = proj_res(x1, p, wmo, k, zero_bias, md)
            mix = (z, p)
        x2 = xs
        wup, k = weight(0, 2 * i + 1, (xs,))
        wdn, _ = weight(1, 2 * i + 1, ())
        xs, u3, f3 = ffn_fwd(x2, gn[2:3], md, wup, wdn, k, 2)
        saved.append((x0, u1, f1, x1, mix, ymix, x2, u3, f3))
    dy, lpart = loss_grad(xs, loss_target.reshape(t, D))

    cc = ci.reshape(1).astype(jnp.int32)
    gshard = [lax.empty(s.shape, F32) for s in stacks]
    upd = [[lax.empty(s.shape, F32) for _ in range(3)] for s in stacks]
    mstacks = [_stack3(mom[n]) for n in BIG]
    vstacks = [_stack3(var[n]) for n in BIG]

    def finish(g, copies, after):
        ssem, rsem, pair, lands = copies
        pair, lands = scatter_wait(ssem, rsem, pair, lands, gaxes[g], after, f"g{g}")
        grp = groups[g]
        for a, (b, l0, n) in enumerate(grp):
            gshard[b] = sum_chips(lands[a], pair[a], gshard[b], l0, axes[b], where)
        joined = join_halves([gshard[b] for b, _, _ in grp], [(l0, n) for _, l0, n in grp], gaxes[g])
        for (b, l0, n), gj in zip(grp, joined):
            gshard[b] = gj
            upd[b] = adamw_layers(stacks[b], gj, mstacks[b], vstacks[b], upd[b], l0, n)

    ggrad = {g: [lax.empty(buf.shape, BF16) for buf in ready[g]] for g in range(len(groups))}
    missing = {g: sum(n for _, _, n in grp) for g, grp in enumerate(groups)}
    state = dict(flight=None, token=())

    def put(b, l, lhs, rhs, bm, bn):
        g, a, k = where_is[(b, l)]
        ggrad[g][a] = wgrad(ggrad[g][a], k, lhs, rhs, bm, bn)
        missing[g] -= 1
        if missing[g] == 0:
            theirs = exchange_halves(ggrad[g], gaxes[g])
            pair = [add_my_half(gr, th, ax, cc) for gr, th, ax in zip(ggrad[g], theirs, gaxes[g])]
            ssem, rsem, pair, lands, tok = scatter_start(pair, gaxes[g], f"g{g}")
            if state["flight"] is not None:
                finish(*state["flight"], (tok,))
            state.update(flight=(g, (ssem, rsem, pair, lands)), token=(tok,))

    dmod = [None] * DEPTH
    dgain = [None] * DEPTH
    db_qkv, dqk_gain, dsinks, db_o, dconv_w = ([None] * n_attn, [None] * n_attn, [None] * n_attn,
                                                [None] * n_attn, [None] * n_conv)
    for i in reversed(range(DEPTH)):
        j = i // 2
        gn, md = gain_full[i], mod[i]
        x0, u1, f1, x1, mix, ymix, x2, u3, f3 = saved[i]
        (wup, k), (wdn, _) = weight(0, 2 * i + 1, ()), weight(1, 2 * i + 1, ())
        du, a, df, sg3 = ffn_bwd_act(dy, u3, f3, md, wdn, k, 2, after=state["token"])
        dy, h, se3, sa3 = lin_bwd(dy, du, wup, k, x2, gn[2:3], md, 2, False)
        put(0, 2 * i + 1, h, du, D, FF_CHUNK)
        put(1, 2 * i + 1, a, df, FF_CHUNK, D)
        (wmi, k), (wmo, _) = weight(mixer(i)[0], j, ()), weight(mixer(i)[1], j, ())
        if i % 2 == 0:
            raw, qr, kr, o, lse = mix
            dyy, do, sp, sb = proj_res_bwd(dy, ymix, wmo, k, md)
            put(mixer(i)[1], j, o, dyy, D, D)
            dq, dk, dv, dsinks[j] = attn_bwd(qr, kr, raw, attn_sinks[j:j + 1], o, do, lse)
            dz, dqk_gain[j] = qkv_bwd_pre(dq, dk, dv, raw, gqk[j], cos, sin, g1, g2, gsel)
            dy, h, se2, sa2, db_qkv[j] = lin_bwd(dy, dz, wmi, k, x1, gn[1:2], md, 1, True)
            put(mixer(i)[0], j, h, dz, D, QKV_DIM)
            db_o[j] = sb
        else:
            z, p = mix
            dyy, dp, sp, _ = proj_res_bwd(dy, ymix, wmo, k, md)
            put(mixer(i)[1], j, p, dyy, D, D)
            dz, dconv_w[j] = conv_bwd(z, convw_full[j], dp)
            dy, h, se2, sa2 = lin_bwd(dy, dz, wmi, k, x1, gn[1:2], md, 1, False)
            put(mixer(i)[0], j, h, dz, D, 1536)
        (wup, k), (wdn, _) = weight(0, 2 * i, ()), weight(1, 2 * i, ())
        du, a, df, sg1 = ffn_bwd_act(dy, u1, f1, md, wdn, k, 0, after=state["token"])
        dy, h, se1, sa1 = lin_bwd(dy, du, wup, k, x0, gn[0:1], md, 0, False)
        put(0, 2 * i, h, du, D, FF_CHUNK)
        put(1, 2 * i, a, df, FF_CHUNK, D)
        dmod[i] = jnp.stack([se1[:, 0], se1[:, 1], sg1[:, 0], se2[:, 0], se2[:, 1], sp[:, 0],
                             se3[:, 0], se3[:, 1], sg3[:, 0]], axis=1)
        dgain[i] = jnp.stack([sa1[0], sa2[0], sa3[0]], axis=0)
    grad_x = dy.reshape(x.shape)

    dmod_ex = jnp.stack(dmod, axis=1).reshape(nex, DEPTH * 9, D)
    dmod_ex = jnp.pad(dmod_ex, ((0, 0), (0, N_DMOD_ROWS - DEPTH * 9), (0, 0))).reshape(nex * N_DMOD_ROWS, D)
    misc = jnp.concatenate([dqk_gain[jj][0] for jj in range(n_attn)]
                           + [jnp.pad(dsinks[jj][0], (0, 128 - N_HEADS)) for jj in range(n_attn)]
                           + [lpart[0]])
    rows = [dmod_ex,
            jnp.concatenate(dgain, axis=0), jnp.zeros((4, D), F32),
            jnp.concatenate([_pad_cols(db_qkv[jj][0:1], 2 * D).reshape(2, D) for jj in range(n_attn)], axis=0),
            jnp.concatenate([db_o[jj][0:1] for jj in range(n_attn)], axis=0),
            jnp.concatenate([dconv_w[jj][0:3] for jj in range(n_conv)], axis=0),
            jnp.pad(misc, (0, D - misc.shape[0])).reshape(1, D)]
    packed = jnp.concatenate(rows, axis=0)
    packed = jnp.pad(packed, ((0, N_SMALL_ROWS - packed.shape[0]), (0, 0)))
    p_all = allgather_small(packed, True)
    red, exsum = reduce_small(p_all)

    r0 = nex * N_DMOD_ROWS
    grads = {}
    grads["b_ada"] = exsum[:DEPTH * 9].reshape(DEPTH, 9 * D)
    grads["norm_gain"] = lax.dynamic_slice_in_dim(red[r0:r0 + 12].reshape(DEPTH, 3, D), chip * (D // N_CHIPS),
                                                  D // N_CHIPS, axis=2)
    r1 = r0 + 16
    grads["attn_b_qkv"] = red[r1:r1 + 2 * n_attn].reshape(n_attn, 2 * D)[:, :QKV_DIM]
    r2 = r1 + 2 * n_attn
    grads["attn_b_o"] = red[r2:r2 + n_attn]
    r3 = r2 + n_attn
    grads["conv_w"] = lax.dynamic_slice_in_dim(red[r3:r3 + 3 * n_conv].reshape(n_conv, 3, D), chip * (D // N_CHIPS),
                                               D // N_CHIPS, axis=2)
    mrow = red[r3 + 3 * n_conv]
    grads["attn_q_gain"] = jnp.stack([mrow[128 * jj:128 * jj + HEAD_DIM] for jj in range(n_attn)])
    grads["attn_k_gain"] = jnp.stack([mrow[128 * jj + HEAD_DIM:128 * jj + 128] for jj in range(n_attn)])
    grads["attn_sinks"] = jnp.stack([mrow[128 * (n_attn + jj):128 * (n_attn + jj) + N_HEADS] for jj in range(n_attn)])
    loss = mrow[128 * 2 * n_attn]

    dm_all = p_all[:, :r0].reshape(8, nex, N_DMOD_ROWS, D)[:, :, :DEPTH * 9].reshape(8 * nex, DEPTH, 9 * D)
    dm_mine = lax.dynamic_slice_in_dim(dm_all.transpose(1, 0, 2), chip * ada_cols, ada_cols, axis=2)
    g_ada, d_ada, nm_ada, nv_ada = ada_bwd_adam(c_all, dm_mine, w_ada, m_w_ada, v_w_ada)

    delta, new_m, new_v = {}, {}, {}
    for n in WEIGHTS:
        if n == "w_ada":
            grads[n], delta[n], new_m[n], new_v[n] = g_ada, d_ada, nm_ada, nv_ada
        elif n not in BIG:
            delta[n], new_m[n], new_v[n] = adamw(w[n], grads[n], mom[n], var[n])
    finish(*state["flight"], (delta["conv_w"], d_ada))
    for b, n in enumerate(BIG):
        grads[n] = gshard[b].reshape(w[n].shape)
        delta[n], new_m[n], new_v[n] = (u.reshape(w[n].shape) for u in upd[b])

    return (loss, grad_x, *[grads[n] for n in WEIGHTS], *[delta[n] for n in WEIGHTS],
            *[new_m[n] for n in WEIGHTS], *[new_v[n] for n in WEIGHTS])
```

```python
import functools

import jax
import jax.numpy as jnp
from jax import lax
from jax.experimental import pallas as pl
from jax.experimental.pallas import tpu as pltpu

F32 = jnp.float32
BF16 = jnp.bfloat16

D = 1024
D_FF = 2816
N_HEADS = 16
N_KV = 4
HEAD_DIM = 64
GROUP = N_HEADS // N_KV
QK_DIM = (N_HEADS + N_KV) * HEAD_DIM
QKV_DIM = QK_DIM + N_KV * HEAD_DIM
BLOCK = 128
ROPE_THETA = 10000.0
EPS = 1e-6
DEPTH = 4
N_CHIPS = 4

ADAM_LR = 0.001
ADAM_B1 = 0.9
ADAM_B2 = 0.999
ADAM_EPS = 1e-08
ADAM_WD = 0.01
ADAM_STEP = 10

V7X_VMEM_BYTES = 64 * 1024 * 1024
V7X_MXU_DIM = 256
FF_CHUNKS = ((0, 1536), (1536, D_FF))
assert all((hi - lo) % V7X_MXU_DIM == 0 for lo, hi in FF_CHUNKS)
MESH = pl.DeviceIdType.MESH
ANY = pl.BlockSpec(memory_space=pl.ANY)


def _cparams(vmem_mb, n_grid):
    assert vmem_mb * 1024 * 1024 <= V7X_VMEM_BYTES
    return pltpu.CompilerParams(vmem_limit_bytes=vmem_mb * 1024 * 1024,
                                dimension_semantics=("arbitrary",) * n_grid)


def _resident(shape):
    nd = len(shape)
    return pl.BlockSpec(shape, lambda *_: (0,) * nd, pipeline_mode=pl.Buffered(1))


def _layer(w, l):
    return pl.BlockSpec((None,) + w.shape[1:], lambda *_: (l, 0, 0), pipeline_mode=pl.Buffered(1))


PIN_BYTES = 1 << 20


def _pin(*args):
    return [pltpu.with_memory_space_constraint(a, pltpu.HBM) if a.size * a.dtype.itemsize >= PIN_BYTES else a
            for a in args]


def _mid(shape, dtype):
    n = 1
    for d in shape:
        n *= d
    if n * jnp.dtype(dtype).itemsize >= PIN_BYTES:
        return pltpu.HBM(tuple(shape), dtype)
    return jax.ShapeDtypeStruct(tuple(shape), dtype)


def _dot(a, b):
    return jnp.dot(a, b, preferred_element_type=F32)


def _dot_nt(a, b):
    return lax.dot_general(a, b, (((1,), (1,)), ((), ())), preferred_element_type=F32)


def _dot_tn(a, b):
    return lax.dot_general(a, b, (((0,), (0,)), ((), ())), preferred_element_type=F32)


def _dot_hilo(a, g):
    hi = a.astype(BF16)
    lo = (a - hi.astype(F32)).astype(BF16)
    return _dot(hi, g) + _dot(lo, g)


def _colsum(a):
    return jnp.sum(a, axis=0, keepdims=True)


def _norm_mod(x, gain, sc, sh):
    r = lax.rsqrt(jnp.mean(x * x, axis=-1, keepdims=True) + EPS)
    n = x * r * gain
    return r, n, n * (1.0 + sc) + sh


def _mod_rows(mod_ref, s):
    return (mod_ref[0, 3 * s:3 * s + 1, :], mod_ref[0, 3 * s + 1:3 * s + 2, :], mod_ref[0, 3 * s + 2:3 * s + 3, :])


def ffn_fwd(x, gain, mod, wup, wdn, l, s, tm=512, after=()):
    t = x.shape[0]
    tpe = t // tm // 2

    def body(x_ref, gain_ref, mod_ref, wup_ref, wdn_ref, *rest):
        xo_ref, u_ref, f_ref = rest[len(after):]
        xv = x_ref[...]
        sh, sc, g = _mod_rows(mod_ref, s)
        _, _, h = _norm_mod(xv, gain_ref[...], sc, sh)
        hb = h.astype(BF16)
        acc = jnp.zeros((tm, D), F32)
        for lo, hi in FF_CHUNKS:
            gate = _dot(hb, wup_ref[:, lo:hi])
            up = _dot(hb, wup_ref[:, D_FF + lo:D_FF + hi])
            u_ref[:, lo:hi] = gate.astype(BF16)
            u_ref[:, D_FF + lo:D_FF + hi] = up.astype(BF16)
            a = (gate * jax.nn.sigmoid(gate) * up).astype(BF16)
            acc = acc + _dot(a, wdn_ref[lo:hi, :])
        f_ref[...] = acc.astype(BF16)
        xo_ref[...] = xv + 0.5 * g * acc

    return pl.pallas_call(
        body, name="ffn_fwd", grid=(t // tm,),
        in_specs=[pl.BlockSpec((tm, D), lambda i: (i, 0)),
                  pl.BlockSpec((1, D), lambda i: (0, 0)),
                  pl.BlockSpec((1, 9, D), lambda i: (i // tpe, 0, 0)),
                  _layer(wup, l), _layer(wdn, l)] + [ANY] * len(after),
        out_specs=[pl.BlockSpec((tm, D), lambda i: (i, 0)),
                   pl.BlockSpec((tm, 2 * D_FF), lambda i: (i, 0)),
                   pl.BlockSpec((tm, D), lambda i: (i, 0))],
        out_shape=[_mid((t, D), F32), _mid((t, 2 * D_FF), BF16), _mid((t, D), BF16)],
        compiler_params=_cparams(60, 1),
    )(*_pin(x, gain, mod, wup, wdn, *after))


def ffn_bwd_act(dy, u, f, mod, wdn, l, s, tm=256, after=()):
    t = dy.shape[0]
    tpe = t // tm // 2

    def body(dy_ref, u_ref, f_ref, mod_ref, wdn_ref, *rest):
        du_ref, a_ref, df_ref, sg_ref = rest[len(after):]
        i = pl.program_id(0)
        dyv = dy_ref[...]
        _, _, g = _mod_rows(mod_ref, s)
        dfb = (0.5 * g * dyv).astype(BF16)
        df_ref[...] = dfb

        @pl.when(i % tpe == 0)
        def _():
            sg_ref[...] = jnp.zeros_like(sg_ref)

        sg_ref[0, 0:1, :] += _colsum(0.5 * dyv * f_ref[...].astype(F32))
        for lo, hi in FF_CHUNKS:
            da = _dot_nt(dfb, wdn_ref[lo:hi, :])
            gate = u_ref[:, lo:hi].astype(F32)
            up = u_ref[:, D_FF + lo:D_FF + hi].astype(F32)
            sg = jax.nn.sigmoid(gate)
            silu = gate * sg
            a_ref[:, lo:hi] = (silu * up).astype(BF16)
            du_ref[:, lo:hi] = (da * up * (sg * (1.0 + gate * (1.0 - sg)))).astype(BF16)
            du_ref[:, D_FF + lo:D_FF + hi] = (da * silu).astype(BF16)

    return pl.pallas_call(
        body, name="ffn_bwd_act", grid=(t // tm,),
        in_specs=[pl.BlockSpec((tm, D), lambda i: (i, 0)),
                  pl.BlockSpec((tm, 2 * D_FF), lambda i: (i, 0)),
                  pl.BlockSpec((tm, D), lambda i: (i, 0)),
                  pl.BlockSpec((1, 9, D), lambda i: (i // tpe, 0, 0)),
                  _layer(wdn, l)] + [ANY] * len(after),
        out_specs=[pl.BlockSpec((tm, 2 * D_FF), lambda i: (i, 0)),
                   pl.BlockSpec((tm, D_FF), lambda i: (i, 0)),
                   pl.BlockSpec((tm, D), lambda i: (i, 0)),
                   pl.BlockSpec((1, 8, D), lambda i: (i // tpe, 0, 0))],
        out_shape=[_mid((t, 2 * D_FF), BF16), _mid((t, D_FF), BF16), _mid((t, D), BF16), _mid((2, 8, D), F32)],
        compiler_params=_cparams(40, 1),
    )(*_pin(dy, u, f, mod, wdn, *after))


def lin_bwd(dy, dz, w, l, x, gain, mod, s, want_db, tm=512):
    t = dy.shape[0]
    n = w.shape[2]
    tpe = t // tm // 2

    def body(dy_ref, dz_ref, w_ref, x_ref, gain_ref, mod_ref, dx_ref, h_ref, se_ref, sa_ref, *db_ref):
        i = pl.program_id(0)
        xv = x_ref[...]
        gain_v = gain_ref[...]
        sh, sc, _ = _mod_rows(mod_ref, s)
        r, nrm, h = _norm_mod(xv, gain_v, sc, sh)
        h_ref[...] = h.astype(BF16)
        dh = _dot_nt(dz_ref[...], w_ref[...])
        dn = dh * (1.0 + sc)
        dxr = dn * gain_v
        m = jnp.mean(dxr * xv, axis=-1, keepdims=True)
        dx_ref[...] = dy_ref[...] + r * dxr - xv * (r * r * r) * m

        @pl.when(i % tpe == 0)
        def _():
            se_ref[...] = jnp.zeros_like(se_ref)

        @pl.when(i == 0)
        def _():
            sa_ref[...] = jnp.zeros_like(sa_ref)
            if want_db:
                db_ref[0][...] = jnp.zeros_like(db_ref[0])

        se_ref[0, 0:1, :] += _colsum(dh)
        se_ref[0, 1:2, :] += _colsum(dh * nrm)
        sa_ref[0:1, :] += _colsum(dn * xv * r)
        if want_db:
            db_ref[0][0:1, :] += _colsum(dz_ref[...].astype(F32))

    out_specs = [pl.BlockSpec((tm, D), lambda i: (i, 0)), pl.BlockSpec((tm, D), lambda i: (i, 0)),
                 pl.BlockSpec((1, 8, D), lambda i: (i // tpe, 0, 0)), pl.BlockSpec((8, D), lambda i: (0, 0))]
    out_shape = [_mid((t, D), F32), _mid((t, D), BF16),
                 jax.ShapeDtypeStruct((2, 8, D), F32), jax.ShapeDtypeStruct((8, D), F32)]
    if want_db:
        out_specs.append(pl.BlockSpec((8, n), lambda i: (0, 0)))
        out_shape.append(jax.ShapeDtypeStruct((8, n), F32))
    return pl.pallas_call(
        body, name="lin_bwd", grid=(t // tm,),
        in_specs=[pl.BlockSpec((tm, D), lambda i: (i, 0)),
                  pl.BlockSpec((tm, n), lambda i: (i, 0)),
                  _layer(w, l),
                  pl.BlockSpec((tm, D), lambda i: (i, 0)),
                  pl.BlockSpec((1, D), lambda i: (0, 0)),
                  pl.BlockSpec((1, 9, D), lambda i: (i // tpe, 0, 0))],
        out_specs=out_specs, out_shape=out_shape,
        compiler_params=_cparams(56, 1),
    )(*_pin(dy, dz, w, x, gain, mod))


def wgrad(gstack, l, a, b, bm, bn, bt=1024):
    t, m = a.shape
    n = b.shape[1]
    nt = t // bt

    def body(g_ref, a_ref, b_ref, o_ref, acc_ref):
        k = pl.program_id(2)

        @pl.when(k == 0)
        def _():
            acc_ref[...] = jnp.zeros_like(acc_ref)

        acc_ref[...] += _dot_tn(a_ref[...], b_ref[...])

        @pl.when(k == nt - 1)
        def _():
            o_ref[...] = acc_ref[...].astype(BF16)

    return pl.pallas_call(
        body, name="wgrad", grid=(m // bm, n // bn, nt),
        in_specs=[ANY, pl.BlockSpec((bt, bm), lambda i, j, k: (k, i)),
                  pl.BlockSpec((bt, bn), lambda i, j, k: (k, j))],
        out_specs=pl.BlockSpec((None, bm, bn), lambda i, j, k: (l, i, j)),
        out_shape=_mid(gstack.shape, BF16),
        input_output_aliases={0: 0},
        scratch_shapes=[pltpu.VMEM((bm, bn), F32)],
        compiler_params=_cparams(48, 3),
    )(*_pin(gstack, a, b))


def proj_res(x, o, w, l, b, mod, tm=512):
    t = x.shape[0]
    tpe = t // tm // 2

    def body(x_ref, o_ref, w_ref, b_ref, mod_ref, xo_ref, y_ref):
        _, _, g = _mod_rows(mod_ref, 1)
        y = _dot(o_ref[...], w_ref[...]) + b_ref[...]
        y_ref[...] = y.astype(BF16)
        xo_ref[...] = x_ref[...] + g * y

    return pl.pallas_call(
        body, name="proj_res", grid=(t // tm,),
        in_specs=[pl.BlockSpec((tm, D), lambda i: (i, 0)), pl.BlockSpec((tm, D), lambda i: (i, 0)),
                  _layer(w, l), pl.BlockSpec((1, D), lambda i: (0, 0)),
                  pl.BlockSpec((1, 9, D), lambda i: (i // tpe, 0, 0))],
        out_specs=[pl.BlockSpec((tm, D), lambda i: (i, 0)), pl.BlockSpec((tm, D), lambda i: (i, 0))],
        out_shape=[_mid((t, D), F32), _mid((t, D), BF16)],
        compiler_params=_cparams(32, 1),
    )(*_pin(x, o, w, b, mod))


def proj_res_bwd(dy, y, w, l, mod, tm=512):
    t = dy.shape[0]
    tpe = t // tm // 2

    def body(dy_ref, y_ref, w_ref, mod_ref, dyy_ref, do_ref, se_ref, sa_ref):
        i = pl.program_id(0)
        _, _, g = _mod_rows(mod_ref, 1)
        dyv = dy_ref[...]
        dyy = g * dyv
        dyb = dyy.astype(BF16)
        dyy_ref[...] = dyb
        do_ref[...] = _dot_nt(dyb, w_ref[...]).astype(BF16)

        @pl.when(i % tpe == 0)
        def _():
            se_ref[...] = jnp.zeros_like(se_ref)

        @pl.when(i == 0)
        def _():
            sa_ref[...] = jnp.zeros_like(sa_ref)

        se_ref[0, 0:1, :] += _colsum(dyv * y_ref[...].astype(F32))
        sa_ref[0:1, :] += _colsum(dyy)

    return pl.pallas_call(
        body, name="proj_res_bwd", grid=(t // tm,),
        in_specs=[pl.BlockSpec((tm, D), lambda i: (i, 0)), pl.BlockSpec((tm, D), lambda i: (i, 0)),
                  _layer(w, l), pl.BlockSpec((1, 9, D), lambda i: (i // tpe, 0, 0))],
        out_specs=[pl.BlockSpec((tm, D), lambda i: (i, 0)), pl.BlockSpec((tm, D), lambda i: (i, 0)),
                   pl.BlockSpec((1, 8, D), lambda i: (i // tpe, 0, 0)), pl.BlockSpec((8, D), lambda i: (0, 0))],
        out_shape=[_mid((t, D), BF16), _mid((t, D), BF16), _mid((2, 8, D), F32), _mid((8, D), F32)],
        compiler_params=_cparams(32, 1),
    )(*_pin(dy, y, w, mod))


def lin_fwd(x, gain, mod, w, l, s, tm=512):
    t = x.shape[0]
    n = w.shape[2]
    tpe = t // tm // 2

    def body(x_ref, gain_ref, mod_ref, w_ref, z_ref):
        sh, sc, _ = _mod_rows(mod_ref, s)
        _, _, h = _norm_mod(x_ref[...], gain_ref[...], sc, sh)
        z_ref[...] = _dot(h.astype(BF16), w_ref[...]).astype(BF16)

    return pl.pallas_call(
        body, name="lin_fwd", grid=(t // tm,),
        in_specs=[pl.BlockSpec((tm, D), lambda i: (i, 0)), pl.BlockSpec((1, D), lambda i: (0, 0)),
                  pl.BlockSpec((1, 9, D), lambda i: (i // tpe, 0, 0)), _layer(w, l)],
        out_specs=pl.BlockSpec((tm, n), lambda i: (i, 0)),
        out_shape=_mid((t, n), BF16),
        compiler_params=_cparams(40, 1),
    )(*_pin(x, gain, mod, w))


def rope_tables(pos, invf):
    t = pos.shape[0]
    tm = 1024

    def body(pos_ref, invf_ref, c_ref, s_ref):
        ang = pos_ref[...].astype(F32) * invf_ref[...]
        lane = lax.broadcasted_iota(jnp.int32, (tm, 128), 1)
        sign = jnp.where(lane % HEAD_DIM < HEAD_DIM // 2, -1.0, 1.0)
        c_ref[...] = jnp.cos(ang)
        s_ref[...] = sign * jnp.sin(ang)

    return pl.pallas_call(
        body, name="rope_tables", grid=(t // tm,),
        in_specs=[pl.BlockSpec((tm, 1), lambda i: (i, 0)), pl.BlockSpec((1, 128), lambda i: (0, 0))],
        out_specs=[pl.BlockSpec((tm, 128), lambda i: (i, 0))] * 2,
        out_shape=[jax.ShapeDtypeStruct((t, 128), F32)] * 2,
        compiler_params=_cparams(16, 1),
    )(pos, invf)


def _swap_halves(v):
    lane = lax.broadcasted_iota(jnp.int32, v.shape, 1)
    return jnp.where(lane % HEAD_DIM < HEAD_DIM // 2, pltpu.roll(v, 128 - HEAD_DIM // 2, 1), pltpu.roll(v, HEAD_DIM // 2, 1))


def _rope(v, cos, sin):
    return jnp.concatenate(
        [v[:, j:j + 128] * cos + _swap_halves(v[:, j:j + 128]) * sin for j in range(0, v.shape[1], 128)], axis=1)


def _rope_t(dv, cos, sin):
    return jnp.concatenate(
        [dv[:, j:j + 128] * cos + _swap_halves(dv[:, j:j + 128] * sin) for j in range(0, dv.shape[1], 128)], axis=1)


def _head_stats(qk, g1, g2):
    rinv = lax.rsqrt(_dot_hilo(qk * qk, g1) + EPS)
    return rinv, _dot_hilo(rinv, g2)


def qkv_fwd(x, gain, mod, w, l, b, gqk, cos, sin, g1, g2, tm=256):
    t = x.shape[0]
    tpe = t // tm // 2

    def body(x_ref, gain_ref, mod_ref, w_ref, b_ref, gqk_ref, c_ref, s_ref, g1_ref, g2_ref, raw_ref, q_ref, k_ref):
        sh, sc, _ = _mod_rows(mod_ref, 1)
        _, _, h = _norm_mod(x_ref[...], gain_ref[...], sc, sh)
        qkv = _dot(h.astype(BF16), w_ref[...]) + b_ref[...]
        raw_ref[...] = qkv.astype(BF16)
        qk = qkv[:, :QK_DIM]
        _, rb = _head_stats(qk, g1_ref[...], g2_ref[...])
        qr = _rope(qk * rb * gqk_ref[...], c_ref[...], s_ref[...])
        q_ref[...] = qr[:, :D].astype(BF16)
        k_ref[...] = qr[:, D:].astype(BF16)

    return pl.pallas_call(
        body, name="qkv_fwd", grid=(t // tm,),
        in_specs=[pl.BlockSpec((tm, D), lambda i: (i, 0)), pl.BlockSpec((1, D), lambda i: (0, 0)),
                  pl.BlockSpec((1, 9, D), lambda i: (i // tpe, 0, 0)), _layer(w, l),
                  pl.BlockSpec((1, QKV_DIM), lambda i: (0, 0)), pl.BlockSpec((1, QK_DIM), lambda i: (0, 0)),
                  pl.BlockSpec((tm, 128), lambda i: (i, 0)), pl.BlockSpec((tm, 128), lambda i: (i, 0)),
                  _resident((QK_DIM, 128)), _resident((128, QK_DIM))],
        out_specs=[pl.BlockSpec((tm, QKV_DIM), lambda i: (i, 0)), pl.BlockSpec((tm, D), lambda i: (i, 0)),
                   pl.BlockSpec((tm, N_KV * HEAD_DIM), lambda i: (i, 0))],
        out_shape=[_mid((t, QKV_DIM), BF16), _mid((t, D), BF16), _mid((t, N_KV * HEAD_DIM), BF16)],
        compiler_params=_cparams(40, 1),
    )(*_pin(x, gain, mod, w, b, gqk, cos, sin, g1, g2))


def qkv_bwd_pre(dq, dk, dv, raw, gqk, cos, sin, g1, g2, gsel, tm=256):
    t = dq.shape[0]

    def body(dq_ref, dk_ref, dv_ref, raw_ref, gqk_ref, c_ref, s_ref, g1_ref, g2_ref, gsel_ref, dz_ref, sa_ref):
        i = pl.program_id(0)
        dqk = jnp.concatenate([dq_ref[...].astype(F32), dk_ref[...]], axis=1)
        dqn = _rope_t(dqk, c_ref[...], s_ref[...])
        qk = raw_ref[:, :QK_DIM].astype(F32)
        g1v, g2v = g1_ref[...], g2_ref[...]
        rinv, rb = _head_stats(qk, g1v, g2v)
        dgq = jnp.broadcast_to(_colsum(dqn * qk * rb), (8, QK_DIM))
        dyh = dqn * gqk_ref[...]
        mh = _dot_hilo(dyh * qk, g1v)
        mb = _dot_hilo(mh * rinv * rinv * rinv, g2v)
        dz_ref[:, :QK_DIM] = (rb * dyh - qk * mb).astype(BF16)
        dz_ref[:, QK_DIM:] = dv_ref[...].astype(BF16)

        @pl.when(i == 0)
        def _():
            sa_ref[...] = jnp.zeros_like(sa_ref)

        sa_ref[...] += _dot_hilo(dgq, gsel_ref[...])

    kvw = N_KV * HEAD_DIM
    return pl.pallas_call(
        body, name="qkv_bwd_pre", grid=(t // tm,),
        in_specs=[pl.BlockSpec((tm, D), lambda i: (i, 0)), pl.BlockSpec((tm, kvw), lambda i: (i, 0)),
                  pl.BlockSpec((tm, kvw), lambda i: (i, 0)), pl.BlockSpec((tm, QKV_DIM), lambda i: (i, 0)),
                  pl.BlockSpec((1, QK_DIM), lambda i: (0, 0)),
                  pl.BlockSpec((tm, 128), lambda i: (i, 0)), pl.BlockSpec((tm, 128), lambda i: (i, 0)),
                  _resident((QK_DIM, 128)), _resident((128, QK_DIM)), _resident((QK_DIM, 128))],
        out_specs=[pl.BlockSpec((tm, QKV_DIM), lambda i: (i, 0)), pl.BlockSpec((8, 128), lambda i: (0, 0))],
        out_shape=[_mid((t, QKV_DIM), BF16), _mid((8, 128), F32)],
        compiler_params=_cparams(40, 1),
    )(*_pin(dq, dk, dv, raw, gqk, cos, sin, g1, g2, gsel))


def _band_mask(n):
    row = lax.broadcasted_iota(jnp.int32, (GROUP * BLOCK, 2 * BLOCK), 0) % BLOCK
    col = lax.broadcasted_iota(jnp.int32, (GROUP * BLOCK, 2 * BLOCK), 1)
    rel = row + BLOCK - col
    return (rel >= 0) & (rel < BLOCK) & ((col >= BLOCK) | (n > 0))


def _stack_heads(v, g):
    base = g * GROUP * HEAD_DIM
    return jnp.concatenate([v[:, base + j * HEAD_DIM:base + (j + 1) * HEAD_DIM] for j in range(GROUP)], axis=0)


def _kv_cat(prev, cur, g):
    return jnp.concatenate([prev[:, g * HEAD_DIM:(g + 1) * HEAD_DIM], cur[:, g * HEAD_DIM:(g + 1) * HEAD_DIM]], axis=0)


def _sink_col(sink_ref, g):
    return jnp.concatenate([jnp.full((BLOCK, 1), sink_ref[0, g * GROUP + j], F32) for j in range(GROUP)], axis=0)


def _attn_specs(nb):
    kvw = N_KV * HEAD_DIM
    vcol = QK_DIM // kvw
    cur = lambda e, n: (e * nb + n, 0)
    prev = lambda e, n: (e * nb + jnp.maximum(n - 1, 0), 0)
    return [pl.BlockSpec((BLOCK, D), cur),
            pl.BlockSpec((BLOCK, kvw), cur), pl.BlockSpec((BLOCK, kvw), prev),
            pl.BlockSpec((BLOCK, kvw), lambda e, n: (e * nb + n, vcol)),
            pl.BlockSpec((BLOCK, kvw), lambda e, n: (e * nb + jnp.maximum(n - 1, 0), vcol)),
            pl.BlockSpec(memory_space=pltpu.SMEM)]


def attn_fwd(q, k, raw, sinks):
    t = q.shape[0]
    nb = t // 2 // BLOCK

    def body(q_ref, kc_ref, kp_ref, vc_ref, vp_ref, sink_ref, o_ref, lse_ref):
        n = pl.program_id(1)
        qv = q_ref[...]
        kc, kp, vc, vp = kc_ref[...], kp_ref[...], vc_ref[...], vp_ref[...]
        mask = _band_mask(n)
        outs, lses = [], []
        for g in range(N_KV):
            kk, vv = _kv_cat(kp, kc, g), _kv_cat(vp, vc, g)
            s = jnp.where(mask, _dot_nt(_stack_heads(qv, g), kk) * (HEAD_DIM ** -0.5), -1e30)
            sink = _sink_col(sink_ref, g)
            m = jnp.maximum(jnp.max(s, axis=1, keepdims=True), sink)
            p = jnp.exp(s - m)
            l = jnp.sum(p, axis=1, keepdims=True) + jnp.exp(sink - m)
            o = _dot(p.astype(BF16), vv) / l
            lse = m + jnp.log(l)
            for j in range(GROUP):
                outs.append(o[j * BLOCK:(j + 1) * BLOCK, :])
                lses.append(lse[j * BLOCK:(j + 1) * BLOCK, :])
        o_ref[...] = jnp.concatenate(outs, axis=1).astype(BF16)
        lse_ref[...] = jnp.concatenate(lses, axis=1)

    cur = lambda e, n: (e * nb + n, 0)
    return pl.pallas_call(
        body, name="attn_fwd", grid=(2, nb),
        in_specs=_attn_specs(nb),
        out_specs=[pl.BlockSpec((BLOCK, D), cur), pl.BlockSpec((BLOCK, N_HEADS), cur)],
        out_shape=[_mid((t, D), BF16), _mid((t, N_HEADS), F32)],
        compiler_params=_cparams(32, 2),
    )(*_pin(q, k, k, raw, raw), sinks)


def attn_bwd(q, k, raw, sinks, o, do, lse):
    t = q.shape[0]
    s_len = t // 2
    nb = s_len // BLOCK
    kvw = N_KV * HEAD_DIM

    def body(q_ref, kc_ref, kp_ref, vc_ref, vp_ref, sink_ref, o_ref, do_ref, lse_ref, dq_ref, dk_ref, dv_ref, ds_ref):
        n = pl.program_id(1)

        @pl.when(n == 0)
        def _():
            dk_ref[...] = jnp.zeros_like(dk_ref)
            dv_ref[...] = jnp.zeros_like(dv_ref)

        @pl.when((n == 0) & (pl.program_id(0) == 0))
        def _():
            ds_ref[...] = jnp.zeros_like(ds_ref)

        qv, ov, dov, lsev = q_ref[...], o_ref[...], do_ref[...], lse_ref[...]
        kc, kp, vc, vp = kc_ref[...], kp_ref[...], vc_ref[...], vp_ref[...]
        mask = _band_mask(n)
        dqs, dks, dvs, dsk = [], [], [], []
        for g in range(N_KV):
            kk, vv = _kv_cat(kp, kc, g), _kv_cat(vp, vc, g)
            qg, og, dog = _stack_heads(qv, g), _stack_heads(ov, g), _stack_heads(dov, g)
            lse = jnp.concatenate([lsev[:, g * GROUP + j:g * GROUP + j + 1] for j in range(GROUP)], axis=0)
            s = jnp.where(mask, _dot_nt(qg, kk) * (HEAD_DIM ** -0.5), -1e30)
            p = jnp.exp(s - lse)
            dd = jnp.sum(dog.astype(F32) * og.astype(F32), axis=1, keepdims=True)
            ds = (p * (_dot_nt(dog, vv) - dd) * (HEAD_DIM ** -0.5)).astype(BF16)
            dqg = _dot(ds, kk)
            dks.append(_dot_tn(ds, qg))
            dvs.append(_dot_tn(p.astype(BF16), dog))
            wsink = jnp.exp(_sink_col(sink_ref, g) - lse) * dd
            for j in range(GROUP):
                dqs.append(dqg[j * BLOCK:(j + 1) * BLOCK, :])
                dsk.append(wsink[j * BLOCK:(j + 1) * BLOCK, :])
        dq_ref[...] = jnp.concatenate(dqs, axis=1).astype(BF16)
        dkk = jnp.concatenate(dks, axis=1)
        dvv = jnp.concatenate(dvs, axis=1)
        prev0 = pl.multiple_of(jnp.maximum(n - 1, 0) * BLOCK, BLOCK)
        cur0 = pl.multiple_of(n * BLOCK, BLOCK)
        dk_ref[pl.ds(prev0, BLOCK), :] += dkk[:BLOCK]
        dv_ref[pl.ds(prev0, BLOCK), :] += dvv[:BLOCK]
        dk_ref[pl.ds(cur0, BLOCK), :] += dkk[BLOCK:]
        dv_ref[pl.ds(cur0, BLOCK), :] += dvv[BLOCK:]
        ds_ref[0:1, :] -= _colsum(jnp.concatenate(dsk, axis=1))

    cur = lambda e, n: (e * nb + n, 0)
    return pl.pallas_call(
        body, name="attn_bwd", grid=(2, nb),
        in_specs=_attn_specs(nb) + [pl.BlockSpec((BLOCK, D), cur), pl.BlockSpec((BLOCK, D), cur),
                                    pl.BlockSpec((BLOCK, N_HEADS), cur)],
        out_specs=[pl.BlockSpec((BLOCK, D), cur), pl.BlockSpec((s_len, kvw), lambda e, n: (e, 0)),
                   pl.BlockSpec((s_len, kvw), lambda e, n: (e, 0)), pl.BlockSpec((8, N_HEADS), lambda e, n: (0, 0))],
        out_shape=[_mid((t, D), BF16), _mid((t, kvw), F32), _mid((t, kvw), F32), _mid((8, N_HEADS), F32)],
        compiler_params=_cparams(32, 2),
    )(*_pin(q, k, k, raw, raw), sinks, *_pin(o, do, lse))


CONV_COLS = 256


def _conv_specs(s_len):
    nct = D // CONV_COLS
    return [pl.BlockSpec((s_len, CONV_COLS), lambda j, e, *_: (e, j)),
            pl.BlockSpec((s_len, CONV_COLS), lambda j, e, *_: (e, nct + j)),
            pl.BlockSpec((s_len, CONV_COLS), lambda j, e, *_: (e, 2 * nct + j)),
            pl.BlockSpec((3, CONV_COLS), lambda j, e, *_: (0, j))]


def _conv_taps(gc, v, w, s_len):
    u = gc * v
    row = lax.broadcasted_iota(jnp.int32, u.shape, 0)
    u1 = jnp.where(row >= 1, pltpu.roll(u, 1, 0), 0.0)
    u2 = jnp.where(row >= 2, pltpu.roll(u, 2, 0), 0.0)
    return u, u1, u2, w[2:3, :] * u + w[1:2, :] * u1 + w[0:1, :] * u2


def conv_fwd(z, w):
    t = z.shape[0]
    s_len = t // 2

    def body(gb_ref, gc_ref, v_ref, w_ref, p_ref):
        _, _, _, conv = _conv_taps(gc_ref[...].astype(F32), v_ref[...].astype(F32), w_ref[...], s_len)
        p_ref[...] = (gb_ref[...].astype(F32) * conv).astype(BF16)

    return pl.pallas_call(
        body, name="conv_fwd", grid=(D // CONV_COLS, 2),
        in_specs=_conv_specs(s_len),
        out_specs=pl.BlockSpec((s_len, CONV_COLS), lambda j, e: (e, j)),
        out_shape=_mid((t, D), BF16),
        compiler_params=_cparams(40, 2),
    )(*_pin(z, z, z, w))


def conv_bwd(z, w, dp):
    t = z.shape[0]
    s_len = t // 2
    nct = D // CONV_COLS

    def body(gb_ref, gc_ref, v_ref, w_ref, dp_ref, dz_ref, dw_ref, parts_ref):
        e, part = pl.program_id(1), pl.program_id(2)

        @pl.when(part == 0)
        def _():
            gc, v, wv = gc_ref[...].astype(F32), v_ref[...].astype(F32), w_ref[...]
            u, u1, u2, conv = _conv_taps(gc, v, wv, s_len)
            dpv = dp_ref[...].astype(F32)
            parts_ref[0] = (dpv * conv).astype(BF16)
            dc = dpv * gb_ref[...].astype(F32)
            row = lax.broadcasted_iota(jnp.int32, dc.shape, 0)
            dc1 = jnp.where(row <= s_len - 2, pltpu.roll(dc, s_len - 1, 0), 0.0)
            dc2 = jnp.where(row <= s_len - 3, pltpu.roll(dc, s_len - 2, 0), 0.0)
            du = wv[2:3, :] * dc + wv[1:2, :] * dc1 + wv[0:1, :] * dc2
            parts_ref[1] = (du * v).astype(BF16)
            parts_ref[2] = (du * gc).astype(BF16)

            @pl.when(e == 0)
            def _():
                dw_ref[...] = jnp.zeros_like(dw_ref)

            dw_ref[0:1, :] += _colsum(dc * u2)
            dw_ref[1:2, :] += _colsum(dc * u1)
            dw_ref[2:3, :] += _colsum(dc * u)

        dz_ref[...] = parts_ref[part]

    return pl.pallas_call(
        body, name="conv_bwd", grid=(nct, 2, 3),
        in_specs=_conv_specs(s_len) + [pl.BlockSpec((s_len, CONV_COLS), lambda j, e, part: (e, j))],
        out_specs=[pl.BlockSpec((s_len, CONV_COLS), lambda j, e, part: (e, part * nct + j)),
                   pl.BlockSpec((8, CONV_COLS), lambda j, e, part: (0, j))],
        out_shape=[_mid((t, 3 * D), BF16), jax.ShapeDtypeStruct((8, D), F32)],
        scratch_shapes=[pltpu.VMEM((3, s_len, CONV_COLS), BF16)],
        compiler_params=_cparams(48, 3),
    )(*_pin(z, z, z, w, dp))


def loss_grad(y, tgt, tm=512):
    t = y.shape[0]

    def body(y_ref, t_ref, dy_ref, l_ref):
        i = pl.program_id(0)
        d = y_ref[...] - t_ref[...]
        dy_ref[...] = d * (1.0 / D)

        @pl.when(i == 0)
        def _():
            l_ref[...] = jnp.zeros_like(l_ref)

        l_ref[...] += 0.5 / D * jnp.sum(d * d)

    return pl.pallas_call(
        body, name="loss_grad", grid=(t // tm,),
        in_specs=[pl.BlockSpec((tm, D), lambda i: (i, 0))] * 2,
        out_specs=[pl.BlockSpec((tm, D), lambda i: (i, 0)), pl.BlockSpec((8, 128), lambda i: (0, 0))],
        out_shape=[_mid((t, D), F32), _mid((8, 128), F32)],
        compiler_params=_cparams(32, 1),
    )(*_pin(y, tgt))


ADA_COLS = 384


def ada_fwd(c_all, w):
    nl, _, n = w.shape
    nex = c_all.shape[0]

    def body(c_ref, w_ref, o_ref):
        cv = c_ref[...]
        ca = (cv * jax.nn.sigmoid(cv)).astype(BF16)
        o_ref[0] = _dot(ca, w_ref[0].astype(BF16))

    return pl.pallas_call(
        body, name="ada_fwd", grid=(nl, n // ADA_COLS),
        in_specs=[pl.BlockSpec((nex, D), lambda l, j: (0, 0)), pl.BlockSpec((1, D, ADA_COLS), lambda l, j: (l, 0, j))],
        out_specs=pl.BlockSpec((1, nex, ADA_COLS), lambda l, j: (l, 0, j)),
        out_shape=jax.ShapeDtypeStruct((nl, nex, n), F32),
        compiler_params=_cparams(32, 2),
    )(*_pin(c_all, w))


def _adam_math(w, g, m, v):
    m = ADAM_B1 * m + (1.0 - ADAM_B1) * g
    v = ADAM_B2 * v + (1.0 - ADAM_B2) * (g * g)
    m_hat = m / (1.0 - ADAM_B1 ** ADAM_STEP)
    v_hat = v / (1.0 - ADAM_B2 ** ADAM_STEP)
    return -ADAM_LR * (m_hat / (jnp.sqrt(v_hat) + ADAM_EPS) + ADAM_WD * w), m, v


def ada_bwd_adam(c_all, dm, w, m, v):
    nl, _, n = w.shape
    nex = c_all.shape[0]

    def body(c_ref, dm_ref, w_ref, m_ref, v_ref, g_ref, d_ref, mo_ref, vo_ref):
        cv = c_ref[...]
        ca = (cv * jax.nn.sigmoid(cv)).astype(BF16)
        g = _dot_tn(ca, dm_ref[0].astype(BF16))
        g_ref[0] = g
        d_ref[0], mo_ref[0], vo_ref[0] = _adam_math(w_ref[0], g, m_ref[0], v_ref[0])

    wspec = pl.BlockSpec((1, D, ADA_COLS), lambda l, j: (l, 0, j))
    return pl.pallas_call(
        body, name="ada_bwd_adam", grid=(nl, n // ADA_COLS),
        in_specs=[pl.BlockSpec((nex, D), lambda l, j: (0, 0)), pl.BlockSpec((1, nex, ADA_COLS), lambda l, j: (l, 0, j)),
                  wspec, wspec, wspec],
        out_specs=[wspec] * 4,
        out_shape=[jax.ShapeDtypeStruct(w.shape, F32)] * 4,
        compiler_params=_cparams(40, 2),
    )(*_pin(c_all, dm, w, m, v))


def adamw(w, g, m, v):
    shape = w.shape
    cols = shape[-1]
    rows = w.size // cols
    args = [a.reshape(rows, cols) for a in (w, g, m, v)]
    tr = rows
    while tr * cols * 4 > (1 << 20) and tr % 16 == 0:
        tr //= 2

    def body(w_ref, g_ref, m_ref, v_ref, d_ref, mo_ref, vo_ref):
        d_ref[...], mo_ref[...], vo_ref[...] = _adam_math(w_ref[...], g_ref[...], m_ref[...], v_ref[...])

    spec = pl.BlockSpec((tr, cols), lambda i: (i, 0))
    outs = pl.pallas_call(
        body, name="adamw", grid=(rows // tr,),
        in_specs=[spec] * 4, out_specs=[spec] * 3,
        out_shape=[jax.ShapeDtypeStruct((rows, cols), F32)] * 3,
        compiler_params=_cparams(32, 1),
    )(*args)
    return [o.reshape(shape) for o in outs]


def adamw_layers(w, g, m, v, prev, l0, n):
    _, r, c = w.shape
    tr = r
    while tr * c * 4 > (2 << 20) and tr % 16 == 0:
        tr //= 2

    def body(w_ref, g_ref, m_ref, v_ref, pd_ref, pm_ref, pv_ref, d_ref, mo_ref, vo_ref):
        d_ref[...], mo_ref[...], vo_ref[...] = _adam_math(w_ref[...], g_ref[...], m_ref[...], v_ref[...])

    spec = pl.BlockSpec((1, tr, c), lambda l, i: (l0 + l, i, 0))
    return pl.pallas_call(
        body, name="adamw_layers", grid=(n, r // tr),
        in_specs=[spec] * 4 + [ANY] * 3, out_specs=[spec] * 3,
        out_shape=[jax.ShapeDtypeStruct(w.shape, F32)] * 3,
        input_output_aliases={4: 0, 5: 1, 6: 2},
        compiler_params=_cparams(40, 2),
    )(*_pin(w, g, m, v, *prev))


def cast_into_window(w, l0, n, ax, chip, after=()):
    _, r, c = w.shape
    tr = r
    while tr * c * 4 > (4 << 20) and tr % 32 == 0:
        tr //= 2
    nrb = r // tr
    full = (n, r * N_CHIPS, c) if ax == 1 else (n, r, c * N_CHIPS)

    def body(chip_ref, w_ref, *rest):
        o_ref = rest[len(after)]
        o_ref[...] = w_ref[...].astype(BF16)

    def omap(l, i, chip_ref):
        return (l, chip_ref[0] * nrb + i, 0) if ax == 1 else (l, i, chip_ref[0])

    return pl.pallas_call(
        body, name="cast_into_window",
        grid_spec=pltpu.PrefetchScalarGridSpec(
            num_scalar_prefetch=1, grid=(n, nrb),
            in_specs=[pl.BlockSpec((1, tr, c), lambda l, i, chip_ref: (l0 + l, i, 0))] + [ANY] * len(after),
            out_specs=pl.BlockSpec((1, tr, c), omap)),
        out_shape=_mid(full, BF16), compiler_params=_cparams(32, 2),
    )(chip, *_pin(w, *after))


def add_bias(a, b):
    def body(a_ref, b_ref, o_ref):
        o_ref[...] = a_ref[...] + b_ref[...]

    return pl.pallas_call(body, name="add_bias", out_shape=jax.ShapeDtypeStruct(a.shape, F32))(a, b)


N_DMOD_ROWS = 40


def reduce_small(p_all):
    rows = p_all.shape[1]

    def body(p_ref, red_ref, ex_ref):
        acc = p_ref[0]
        for d in range(1, 8):
            acc = acc + p_ref[d]
        red_ref[...] = acc
        ex_ref[...] = acc[:N_DMOD_ROWS] + acc[N_DMOD_ROWS:2 * N_DMOD_ROWS]

    return pl.pallas_call(
        body, name="reduce_small",
        out_shape=[jax.ShapeDtypeStruct((rows, D), F32), jax.ShapeDtypeStruct((N_DMOD_ROWS, D), F32)],
        compiler_params=_cparams(32, 0),
    )(p_all)


def _place():
    return lax.axis_index("x"), lax.axis_index("y"), lax.axis_index("c")


def _other_chips(x, y):
    return [(1 - x, y), (x, 1 - y), (1 - x, 1 - y)]


def _sl(ref, axis, start, size):
    idx = [slice(None)] * len(ref.shape)
    idx[axis] = pl.ds(pl.multiple_of(start, 16), size)
    return ref.at[tuple(idx)]


def _rcopy(src, dst, send_sem, recv_sem, to):
    return pltpu.make_async_remote_copy(src_ref=src, dst_ref=dst, send_sem=send_sem, recv_sem=recv_sem,
                                        device_id=to, device_id_type=MESH)


def allgather_small(v, all_devices):
    rows, cols = v.shape
    flips = [(dx, dy, dc) for dx in (0, 1) for dy in (0, 1) for dc in (0, 1)
             if (dx, dy, dc) != (0, 0, 0) and (all_devices or dc == 0)]
    n_out = 8 if all_devices else 4

    def body(v_ref, o_ref, send_sems, recv_sems):
        x, y, c = _place()

        def slot(px, py, pc):
            return 4 * px + 2 * py + pc if all_devices else 2 * px + py

        peers = [(1 - x if dx else x, 1 - y if dy else y, 1 - c if dc else c) for dx, dy, dc in flips]
        sends = [_rcopy(v_ref, o_ref.at[slot(x, y, c)], send_sems.at[r], recv_sems.at[r], peer)
                 for r, peer in enumerate(peers)]
        for cp in sends:
            cp.start()
        o_ref[slot(x, y, c)] = v_ref[...]
        for r, peer in enumerate(peers):
            _rcopy(v_ref, o_ref.at[slot(*peer)], send_sems.at[r], recv_sems.at[r], peer).wait_recv()
        for cp in sends:
            cp.wait_send()

    vm = pl.BlockSpec(memory_space=pltpu.VMEM)
    return pl.pallas_call(
        body, name="allgather_small_all" if all_devices else "allgather_small_chips",
        in_specs=[vm], out_specs=vm,
        out_shape=jax.ShapeDtypeStruct((n_out, rows, cols), v.dtype),
        scratch_shapes=[pltpu.SemaphoreType.DMA((len(flips),)), pltpu.SemaphoreType.DMA((len(flips),))],
        compiler_params=pltpu.CompilerParams(vmem_limit_bytes=32 * 1024 * 1024),
    )(v)


HBM = pl.BlockSpec(memory_space=pltpu.HBM)
SEM = pl.BlockSpec(memory_space=pltpu.SEMAPHORE)
SPLIT_COPY = pltpu.CompilerParams(has_side_effects=pltpu.SideEffectType.DATAFLOW_SIDE_EFFECTING)


def _in_hbm(a):
    return pltpu.with_memory_space_constraint(a, pltpu.HBM)


def _window(ref, ax, chip):
    n = ref.shape[ax] // N_CHIPS
    return _sl(ref, ax, (2 * chip[0] + chip[1]) * n, n)


def _half(ref, ax, cc):
    ha = 3 - ax
    hs = ref.shape[ha] // 2
    return _sl(ref, ha, cc * hs, hs)


def gather_start(bufs, axes, tag):
    na = len(bufs)

    def body(*refs):
        ins = refs[:na]
        send_sems, recv_sems = refs[na], refs[na + 1]
        token = refs[2 * na + 2]
        x, y, c = _place()
        for a in range(na):
            mine = _half(_window(ins[a], axes[a], (x, y)), axes[a], c)
            for j, chip in enumerate(_other_chips(x, y)):
                _rcopy(mine, mine, send_sems.at[3 * a + j], recv_sems.at[3 * a + j], (*chip, c)).start()
        token[...] = jnp.zeros_like(token)

    dma = pltpu.SemaphoreType.DMA
    outs = pl.pallas_call(
        body, name="gather_start_" + tag,
        in_specs=[HBM] * na,
        out_specs=(SEM, SEM, *[HBM] * na, pl.BlockSpec(memory_space=pltpu.VMEM)),
        out_shape=(dma((3 * na,)), dma((3 * na,)), *[pltpu.HBM(b.shape, b.dtype) for b in bufs],
                   jax.ShapeDtypeStruct((8, 128), F32)),
        input_output_aliases={a: 2 + a for a in range(na)},
        compiler_params=SPLIT_COPY,
    )(*[_in_hbm(b) for b in bufs])
    return outs[0], outs[1], list(outs[2:2 + na]), outs[2 + na]


def gather_wait(send_sems, recv_sems, bufs, axes, after, tag):
    na = len(bufs)

    def body(*refs):
        ins = refs[:na]
        send_sems, recv_sems = refs[na], refs[na + 1]
        x, y, c = _place()
        for a in range(na):
            for j, chip in enumerate(_other_chips(x, y)):
                got = _half(_window(ins[a], axes[a], chip), axes[a], c)
                _rcopy(got, got, send_sems.at[3 * a + j], recv_sems.at[3 * a + j], (*chip, c)).wait_recv()
        for a in range(na):
            mine = _half(_window(ins[a], axes[a], (x, y)), axes[a], c)
            for j, chip in enumerate(_other_chips(x, y)):
                _rcopy(mine, mine, send_sems.at[3 * a + j], recv_sems.at[3 * a + j], (*chip, c)).wait_send()

    return pl.pallas_call(
        body, name="gather_wait_" + tag,
        in_specs=[HBM] * na + [SEM, SEM] + [ANY] * len(after),
        out_specs=[HBM] * na,
        out_shape=[pltpu.HBM(b.shape, b.dtype) for b in bufs],
        input_output_aliases={a: a for a in range(na)},
        compiler_params=SPLIT_COPY,
    )(*bufs, send_sems, recv_sems, *after)


def gather_forward(bufs, axes):
    na = len(bufs)

    def body(*refs):
        outs = refs[na:2 * na]
        send_sems, recv_sems = refs[2 * na:]
        x, y, c = _place()
        chips = _other_chips(x, y)
        passed = []
        for a in range(na):
            for j, chip in enumerate(chips):
                got = _half(_window(outs[a], axes[a], chip), axes[a], c)
                cp = _rcopy(got, got, send_sems.at[3 * a + j], recv_sems.at[3 * a + j], (x, y, 1 - c))
                cp.start()
                passed.append(cp)
        for a in range(na):
            for j, chip in enumerate(chips):
                got = _half(_window(outs[a], axes[a], chip), axes[a], 1 - c)
                _rcopy(got, got, send_sems.at[3 * a + j], recv_sems.at[3 * a + j], (x, y, 1 - c)).wait_recv()
        for cp in passed:
            cp.wait_send()

    dma = pltpu.SemaphoreType.DMA
    return pl.pallas_call(
        body, name="gather_forward",
        in_specs=[ANY] * na, out_specs=[ANY] * na,
        out_shape=[jax.ShapeDtypeStruct(b.shape, BF16) for b in bufs],
        input_output_aliases={a: a for a in range(na)},
        scratch_shapes=[dma((3 * na,)), dma((3 * na,))],
    )(*bufs)


def exchange_halves(grads, axes):
    na = len(grads)

    def hshape(g, ax):
        ha = 3 - ax
        return tuple(d // 2 if i == ha else d for i, d in enumerate(g.shape))

    def body(*refs):
        ins, outs = refs[:na], refs[na:2 * na]
        send_sems, recv_sems = refs[2 * na:]
        x, y, c = _place()
        cps = []
        for a in range(na):
            ha = 3 - axes[a]
            hs = ins[a].shape[ha] // 2
            cp = _rcopy(_sl(ins[a], ha, (1 - c) * hs, hs), outs[a], send_sems.at[a], recv_sems.at[a], (x, y, 1 - c))
            cp.start()
            cps.append(cp)
        for cp in cps:
            cp.wait_recv()
        for cp in cps:
            cp.wait_send()

    dma = pltpu.SemaphoreType.DMA
    return pl.pallas_call(
        body, name="exchange_halves",
        in_specs=[ANY] * na, out_specs=[ANY] * na,
        out_shape=[jax.ShapeDtypeStruct(hshape(g, ax), BF16) for g, ax in zip(grads, axes)],
        scratch_shapes=[dma((na,)), dma((na,))],
    )(*grads)


def scatter_start(halves, axes, tag):
    na = len(halves)

    def pshape(h, ax):
        return (N_CHIPS - 1,) + tuple(d // N_CHIPS if i == ax else d for i, d in enumerate(h.shape))

    def body(*refs):
        ins, lands = refs[:na], refs[na:2 * na]
        send_sems, recv_sems = refs[2 * na], refs[2 * na + 1]
        token = refs[4 * na + 2]
        x, y, c = _place()
        for a in range(na):
            for j, chip in enumerate(_other_chips(x, y)):
                _rcopy(_window(ins[a], axes[a], chip), lands[a].at[j],
                       send_sems.at[3 * a + j], recv_sems.at[3 * a + j], (*chip, c)).start()
        token[...] = jnp.zeros_like(token)

    dma = pltpu.SemaphoreType.DMA
    lands = [lax.empty(pshape(h, ax), BF16) for h, ax in zip(halves, axes)]
    outs = pl.pallas_call(
        body, name="scatter_start_" + tag,
        in_specs=[HBM] * (2 * na),
        out_specs=(SEM, SEM, *[HBM] * (2 * na), pl.BlockSpec(memory_space=pltpu.VMEM)),
        out_shape=(dma((3 * na,)), dma((3 * na,)), *[pltpu.HBM(b.shape, b.dtype) for b in halves + lands],
                   jax.ShapeDtypeStruct((8, 128), F32)),
        input_output_aliases={a: 2 + a for a in range(2 * na)},
        compiler_params=SPLIT_COPY,
    )(*[_in_hbm(b) for b in halves + lands])
    return outs[0], outs[1], list(outs[2:2 + na]), list(outs[2 + na:2 + 2 * na]), outs[2 + 2 * na]


def scatter_wait(send_sems, recv_sems, halves, lands, axes, after, tag):
    na = len(halves)

    def body(*refs):
        ins, lands = refs[:na], refs[na:2 * na]
        send_sems, recv_sems = refs[2 * na], refs[2 * na + 1]
        x, y, c = _place()
        for a in range(na):
            for j, chip in enumerate(_other_chips(x, y)):
                _rcopy(_window(ins[a], axes[a], chip), lands[a].at[j],
                       send_sems.at[3 * a + j], recv_sems.at[3 * a + j], (*chip, c)).wait_recv()
        for a in range(na):
            for j, chip in enumerate(_other_chips(x, y)):
                _rcopy(_window(ins[a], axes[a], chip), lands[a].at[j],
                       send_sems.at[3 * a + j], recv_sems.at[3 * a + j], (*chip, c)).wait_send()

    outs = pl.pallas_call(
        body, name="scatter_wait_" + tag,
        in_specs=[HBM] * (2 * na) + [SEM, SEM] + [ANY] * len(after),
        out_specs=[HBM] * (2 * na),
        out_shape=[pltpu.HBM(b.shape, b.dtype) for b in halves + lands],
        input_output_aliases={a: a for a in range(2 * na)},
        compiler_params=SPLIT_COPY,
    )(*halves, *lands, send_sems, recv_sems, *after)
    return list(outs[:na]), list(outs[na:])


def join_halves(gs, regions, axes):
    na = len(gs)

    def body(*refs):
        outs = refs[na:2 * na]
        send_sems, recv_sems = refs[2 * na:]
        x, y, c = _place()
        cps = []
        for a in range(na):
            ha = 3 - axes[a]
            hs = outs[a].shape[ha] // 2
            reg = outs[a].at[pl.ds(*regions[a])]
            mine = _sl(reg, ha, c * hs, hs)
            cp = _rcopy(mine, mine, send_sems.at[a], recv_sems.at[a], (x, y, 1 - c))
            cp.start()
            cps.append((cp, _sl(reg, ha, (1 - c) * hs, hs)))
        for a, (cp, theirs) in enumerate(cps):
            _rcopy(theirs, theirs, send_sems.at[a], recv_sems.at[a], (x, y, 1 - c)).wait_recv()
        for cp, _ in cps:
            cp.wait_send()

    dma = pltpu.SemaphoreType.DMA
    return pl.pallas_call(
        body, name="join_halves",
        in_specs=[ANY] * na, out_specs=[ANY] * na,
        out_shape=[jax.ShapeDtypeStruct(g.shape, F32) for g in gs],
        input_output_aliases={a: a for a in range(na)},
        scratch_shapes=[dma((na,)), dma((na,))],
    )(*gs)


def _tile2(r, c, itemsize, limit):
    bc = c
    while bc > 1536:
        bc //= 2
    assert c % bc == 0 and bc % 128 == 0
    br = r
    while br * bc * itemsize > limit and br % 32 == 0:
        br //= 2
    assert r % br == 0 and br % 16 == 0
    return br, bc


def add_my_half(g, theirs, ax, cc):
    ha = 3 - ax
    nl, r, c = theirs.shape
    br, bc = _tile2(r, c, 2, 2 << 20)
    nrb, ncb = r // br, c // bc

    def body(cc_ref, g_ref, t_ref, o_ref):
        o_ref[...] = (g_ref[...].astype(F32) + t_ref[...].astype(F32)).astype(BF16)

    def gmap(l, i, j, cc_ref):
        return (l, cc_ref[0] * nrb + i, j) if ha == 1 else (l, i, cc_ref[0] * ncb + j)

    blk = pl.BlockSpec((1, br, bc), lambda l, i, j, cc_ref: (l, i, j))
    return pl.pallas_call(
        body, name="add_my_half",
        grid_spec=pltpu.PrefetchScalarGridSpec(
            num_scalar_prefetch=1, grid=(nl, nrb, ncb),
            in_specs=[pl.BlockSpec((1, br, bc), gmap), blk], out_specs=blk),
        out_shape=_mid(theirs.shape, BF16),
        compiler_params=_cparams(32, 3),
    )(cc, *_pin(g, theirs))


def sum_chips(parts, pair, gstack, l0, ax, where):
    _, n, r, c = parts.shape
    ha = 3 - ax
    br, bc = _tile2(r, c, 2, 1 << 20)
    nrb, ncb = r // br, c // bc

    def body(w_ref, p_ref, own_ref, g_ref, o_ref):
        acc = own_ref[...].astype(F32)
        for q in range(N_CHIPS - 1):
            acc = acc + p_ref[q].astype(F32)
        o_ref[...] = acc

    def own_map(l, i, j, w_ref):
        return (l, w_ref[0] * nrb + i, j) if ax == 1 else (l, i, w_ref[0] * ncb + j)

    def out_map(l, i, j, w_ref):
        return (l0 + l, w_ref[1] * nrb + i, j) if ha == 1 else (l0 + l, i, w_ref[1] * ncb + j)

    return pl.pallas_call(
        body, name="sum_chips",
        grid_spec=pltpu.PrefetchScalarGridSpec(
            num_scalar_prefetch=1, grid=(n, nrb, ncb),
            in_specs=[pl.BlockSpec((N_CHIPS - 1, 1, br, bc), lambda l, i, j, w_ref: (0, l, i, j)),
                      pl.BlockSpec((1, br, bc), own_map), ANY],
            out_specs=pl.BlockSpec((1, br, bc), out_map)),
        out_shape=jax.ShapeDtypeStruct(gstack.shape, F32),
        input_output_aliases={3: 0},
        compiler_params=_cparams(32, 3),
    )(where, *_pin(parts, pair, gstack))


BIG = ("w_ffn_up", "w_ffn_down", "attn_w_qkv", "attn_w_o", "conv_w_in", "conv_w_out")
BIG_AXIS = {"w_ffn_up": 2, "w_ffn_down": 1, "attn_w_qkv": 2, "attn_w_o": 1, "conv_w_in": 2, "conv_w_out": 1}
WEIGHTS = ("norm_gain", "w_ada", "b_ada", "w_ffn_up", "w_ffn_down", "attn_w_qkv", "attn_b_qkv", "attn_q_gain",
           "attn_k_gain", "attn_sinks", "attn_w_o", "attn_b_o", "conv_w_in", "conv_w", "conv_w_out")
N_SMALL_ROWS = 112


def _stack3(a):
    return a.reshape((-1,) + a.shape[-2:])


def _head_matrices():
    lane = jnp.arange(QK_DIM)
    head = lane // HEAD_DIM
    col = jnp.arange(128)
    g1 = jnp.where(head[:, None] == col[None, :], 1.0 / HEAD_DIM, 0.0).astype(BF16)
    g2 = jnp.where(col[:, None] == head[None, :], 1.0, 0.0).astype(BF16)
    fold = lane % HEAD_DIM + jnp.where(head >= N_HEADS, HEAD_DIM, 0)
    gsel = jnp.where(fold[:, None] == col[None, :], 1.0, 0.0).astype(BF16)
    return g1, g2, gsel


def _pad_cols(a, n):
    return jnp.pad(a, ((0, 0), (0, n - a.shape[1])))


def kernel(x, c, positions, norm_gain, w_ada, b_ada, w_ffn_up, w_ffn_down, attn_w_qkv, attn_b_qkv, attn_q_gain, attn_k_gain, attn_sinks, attn_w_o, attn_b_o, conv_w_in, conv_w, conv_w_out, loss_target, m_norm_gain, m_w_ada, m_b_ada, m_w_ffn_up, m_w_ffn_down, m_attn_w_qkv, m_attn_b_qkv, m_attn_q_gain, m_attn_k_gain, m_attn_sinks, m_attn_w_o, m_attn_b_o, m_conv_w_in, m_conv_w, m_conv_w_out, v_norm_gain, v_w_ada, v_b_ada, v_w_ffn_up, v_w_ffn_down, v_attn_w_qkv, v_attn_b_qkv, v_attn_q_gain, v_attn_k_gain, v_attn_sinks, v_attn_w_o, v_attn_b_o, v_conv_w_in, v_conv_w, v_conv_w_out):
    w = dict(norm_gain=norm_gain, w_ada=w_ada, b_ada=b_ada, w_ffn_up=w_ffn_up, w_ffn_down=w_ffn_down,
             attn_w_qkv=attn_w_qkv, attn_b_qkv=attn_b_qkv, attn_q_gain=attn_q_gain, attn_k_gain=attn_k_gain,
             attn_sinks=attn_sinks, attn_w_o=attn_w_o, attn_b_o=attn_b_o, conv_w_in=conv_w_in, conv_w=conv_w,
             conv_w_out=conv_w_out)
    mom = dict(norm_gain=m_norm_gain, w_ada=m_w_ada, b_ada=m_b_ada, w_ffn_up=m_w_ffn_up, w_ffn_down=m_w_ffn_down,
               attn_w_qkv=m_attn_w_qkv, attn_b_qkv=m_attn_b_qkv, attn_q_gain=m_attn_q_gain,
               attn_k_gain=m_attn_k_gain, attn_sinks=m_attn_sinks, attn_w_o=m_attn_w_o, attn_b_o=m_attn_b_o,
               conv_w_in=m_conv_w_in, conv_w=m_conv_w, conv_w_out=m_conv_w_out)
    var = dict(norm_gain=v_norm_gain, w_ada=v_w_ada, b_ada=v_b_ada, w_ffn_up=v_w_ffn_up, w_ffn_down=v_w_ffn_down,
               attn_w_qkv=v_attn_w_qkv, attn_b_qkv=v_attn_b_qkv, attn_q_gain=v_attn_q_gain,
               attn_k_gain=v_attn_k_gain, attn_sinks=v_attn_sinks, attn_w_o=v_attn_w_o, attn_b_o=v_attn_b_o,
               conv_w_in=v_conv_w_in, conv_w=v_conv_w, conv_w_out=v_conv_w_out)

    xi, yi, ci = _place()
    chip = 2 * xi + yi
    dev = 4 * xi + 2 * yi + ci
    nex, s_len, _ = x.shape
    t = nex * s_len
    n_attn, n_conv = attn_w_qkv.shape[0], conv_w_in.shape[0]
    axes = [BIG_AXIS[n] for n in BIG]

    c_all = allgather_small(jnp.pad(c, ((0, 8 - nex), (0, 0))), True)[:, :nex].reshape(8 * nex, D)
    ada_cols = w_ada.shape[2]
    modp = ada_fwd(c_all, w_ada)
    modg = allgather_small(modp.reshape(DEPTH * 8 * nex, ada_cols), False)
    modg = lax.dynamic_slice_in_dim(modg.reshape(N_CHIPS, DEPTH, 8 * nex, ada_cols), dev * nex, nex, axis=2)
    modg = modg.transpose(1, 2, 0, 3).reshape(DEPTH, nex, 9 * D)
    mod = add_bias(modg, b_ada.reshape(DEPTH, 1, 9 * D)).reshape(DEPTH, nex, 9, D)

    small = jnp.concatenate([norm_gain.reshape(DEPTH * 3, -1), conv_w.reshape(n_conv * 3, -1)], axis=0)
    small = jnp.pad(small, ((0, -small.shape[0] % 8), (0, 0)))
    small = allgather_small(small, False).transpose(1, 0, 2).reshape(small.shape[0], D)
    gain_full = small[:DEPTH * 3].reshape(DEPTH, 3, D)
    convw_full = small[DEPTH * 3:DEPTH * 3 + n_conv * 3].reshape(n_conv, 3, D)

    chip_arr = chip.reshape(1).astype(jnp.int32)
    where = jnp.stack([chip, ci]).astype(jnp.int32)
    stacks = [_stack3(w[n]) for n in BIG]

    def mixer(i):
        return (2, 3) if i % 2 == 0 else (4, 5)

    groups = [[(0, 0, 1), (1, 0, 1)], [(mixer(0)[0], 0, 1), (mixer(0)[1], 0, 1), (0, 1, 1), (1, 1, 1)]]
    groups += [[(0, 2 * i, 2), (1, 2 * i, 2), (mixer(i)[0], i // 2, 1), (mixer(i)[1], i // 2, 1)]
               for i in range(1, DEPTH)]
    gaxes = [[axes[b] for b, _, _ in grp] for grp in groups]
    where_is = {(b, l0 + k): (g, a, k) for g, grp in enumerate(groups) for a, (b, l0, n) in enumerate(grp)
                for k in range(n)}
    in_flight, token = [], (mod, small)
    for g, grp in enumerate(groups):
        bufs = [cast_into_window(stacks[b], l0, n, axes[b], chip_arr, token) for b, l0, n in grp]
        ssem, rsem, bufs, tok = gather_start(bufs, gaxes[g], f"g{g}")
        in_flight.append((ssem, rsem, bufs))
        token = (tok,)

    invf = ROPE_THETA ** (-jnp.arange(0, HEAD_DIM, 2, dtype=F32) / HEAD_DIM)
    cos, sin = rope_tables(positions.reshape(t, 1), jnp.tile(invf, 4).reshape(1, 128))
    g1, g2, gsel = _head_matrices()
    gqk = [jnp.concatenate([jnp.tile(attn_q_gain[j], N_HEADS), jnp.tile(attn_k_gain[j], N_KV)]).reshape(1, QK_DIM)
           for j in range(n_attn)]
    zero_bias = jnp.zeros((1, D), F32)

    xs = x.reshape(t, D)
    saved, ready = [], {}

    def weight(b, l, after):
        g, a, k = where_is[(b, l)]
        if g not in ready:
            ssem, rsem, bufs = in_flight[g]
            ready[g] = gather_forward(gather_wait(ssem, rsem, bufs, gaxes[g], after, f"g{g}"), gaxes[g])
        return ready[g][a], k

    for i in range(DEPTH):
        j = i // 2
        gn, md = gain_full[i], mod[i]
        x0 = xs
        wup, k = weight(0, 2 * i, token if i == 0 else (xs,))
        wdn, _ = weight(1, 2 * i, ())
        xs, u1, f1 = ffn_fwd(x0, gn[0:1], md, wup, wdn, k, 0)
        x1 = xs
        wmi, k = weight(mixer(i)[0], j, (xs,))
        wmo, _ = weight(mixer(i)[1], j, ())
        if i % 2 == 0:
            raw, qr, kr = qkv_fwd(x1, gn[1:2], md, wmi, k, attn_b_qkv[j:j + 1], gqk[j], cos, sin, g1, g2)
            o, lse = attn_fwd(qr, kr, raw, attn_sinks[j:j + 1])
            xs, ymix = proj_res(x1, o, wmo, k, attn_b_o[j:j + 1], md)
            mix = (raw, qr, kr, o, lse)
        else:
            z = lin_fwd(x1, gn[1:2], md, wmi, k, 1)
            p = conv_fwd(z, convw_full[j])
            xs, ymix = proj_res(x1, p, wmo, k, zero_bias, md)
            mix = (z, p)
        x2 = xs
        wup, k = weight(0, 2 * i + 1, (xs,))
        wdn, _ = weight(1, 2 * i + 1, ())
        xs, u3, f3 = ffn_fwd(x2, gn[2:3], md, wup, wdn, k, 2)
        saved.append((x0, u1, f1, x1, mix, ymix, x2, u3, f3))
    dy, lpart = loss_grad(xs, loss_target.reshape(t, D))

    cc = ci.reshape(1).astype(jnp.int32)
    gshard = [lax.empty(s.shape, F32) for s in stacks]
    upd = [[lax.empty(s.shape, F32) for _ in range(3)] for s in stacks]
    mstacks = [_stack3(mom[n]) for n in BIG]
    vstacks = [_stack3(var[n]) for n in BIG]

    def finish(g, copies, after):
        ssem, rsem, pair, lands = copies
        pair, lands = scatter_wait(ssem, rsem, pair, lands, gaxes[g], after, f"g{g}")
        grp = groups[g]
        for a, (b, l0, n) in enumerate(grp):
            gshard[b] = sum_chips(lands[a], pair[a], gshard[b], l0, axes[b], where)
        joined = join_halves([gshard[b] for b, _, _ in grp], [(l0, n) for _, l0, n in grp], gaxes[g])
        for (b, l0, n), gj in zip(grp, joined):
            gshard[b] = gj
            upd[b] = adamw_layers(stacks[b], gj, mstacks[b], vstacks[b], upd[b], l0, n)

    ggrad = {g: [lax.empty(buf.shape, BF16) for buf in ready[g]] for g in range(len(groups))}
    missing = {g: sum(n for _, _, n in grp) for g, grp in enumerate(groups)}
    state = dict(flight=None, token=())

    def put(b, l, lhs, rhs, bm, bn):
        g, a, k = where_is[(b, l)]
        ggrad[g][a] = wgrad(ggrad[g][a], k, lhs, rhs, bm, bn)
        missing[g] -= 1
        if missing[g] == 0:
            theirs = exchange_halves(ggrad[g], gaxes[g])
            pair = [add_my_half(gr, th, ax, cc) for gr, th, ax in zip(ggrad[g], theirs, gaxes[g])]
            ssem, rsem, pair, lands, tok = scatter_start(pair, gaxes[g], f"g{g}")
            if state["flight"] is not None:
                finish(*state["flight"], (tok,))
            state.update(flight=(g, (ssem, rsem, pair, lands)), token=(tok,))

    dmod = [None] * DEPTH
    dgain = [None] * DEPTH
    db_qkv, dqk_gain, dsinks, db_o, dconv_w = ([None] * n_attn, [None] * n_attn, [None] * n_attn,
                                                [None] * n_attn, [None] * n_conv)
    for i in reversed(range(DEPTH)):
        j = i // 2
        gn, md = gain_full[i], mod[i]
        x0, u1, f1, x1, mix, ymix, x2, u3, f3 = saved[i]
        (wup, k), (wdn, _) = weight(0, 2 * i + 1, ()), weight(1, 2 * i + 1, ())
        du, a, df, sg3 = ffn_bwd_act(dy, u3, f3, md, wdn, k, 2, after=state["token"])
        dy, h, se3, sa3 = lin_bwd(dy, du, wup, k, x2, gn[2:3], md, 2, False)
        put(0, 2 * i + 1, h, du, D // 2, D_FF)
        put(1, 2 * i + 1, a, df, D_FF // 2, D)
        (wmi, k), (wmo, _) = weight(mixer(i)[0], j, ()), weight(mixer(i)[1], j, ())
        if i % 2 == 0:
            raw, qr, kr, o, lse = mix
            dyy, do, sp, sb = proj_res_bwd(dy, ymix, wmo, k, md)
            put(mixer(i)[1], j, o, dyy, D, D)
            dq, dk, dv, dsinks[j] = attn_bwd(qr, kr, raw, attn_sinks[j:j + 1], o, do, lse)
            dz, dqk_gain[j] = qkv_bwd_pre(dq, dk, dv, raw, gqk[j], cos, sin, g1, g2, gsel)
            dy, h, se2, sa2, db_qkv[j] = lin_bwd(dy, dz, wmi, k, x1, gn[1:2], md, 1, True)
            put(mixer(i)[0], j, h, dz, D, QKV_DIM)
            db_o[j] = sb
        else:
            z, p = mix
            dyy, dp, sp, _ = proj_res_bwd(dy, ymix, wmo, k, md)
            put(mixer(i)[1], j, p, dyy, D, D)
            dz, dconv_w[j] = conv_bwd(z, convw_full[j], dp)
            dy, h, se2, sa2 = lin_bwd(dy, dz, wmi, k, x1, gn[1:2], md, 1, False)
            put(mixer(i)[0], j, h, dz, D, 1536)
        (wup, k), (wdn, _) = weight(0, 2 * i, ()), weight(1, 2 * i, ())
        du, a, df, sg1 = ffn_bwd_act(dy, u1, f1, md, wdn, k, 0, after=state["token"])
        dy, h, se1, sa1 = lin_bwd(dy, du, wup, k, x0, gn[0:1], md, 0, False)
        put(0, 2 * i, h, du, D // 2, D_FF)
        put(1, 2 * i, a, df, D_FF // 2, D)
        dmod[i] = jnp.stack([se1[:, 0], se1[:, 1], sg1[:, 0], se2[:, 0], se2[:, 1], sp[:, 0],
                             se3[:, 0], se3[:, 1], sg3[:, 0]], axis=1)
        dgain[i] = jnp.stack([sa1[0], sa2[0], sa3[0]], axis=0)
    grad_x = dy.reshape(x.shape)

    dmod_ex = jnp.stack(dmod, axis=1).reshape(nex, DEPTH * 9, D)
    dmod_ex = jnp.pad(dmod_ex, ((0, 0), (0, N_DMOD_ROWS - DEPTH * 9), (0, 0))).reshape(nex * N_DMOD_ROWS, D)
    misc = jnp.concatenate([dqk_gain[jj][0] for jj in range(n_attn)]
                           + [jnp.pad(dsinks[jj][0], (0, 128 - N_HEADS)) for jj in range(n_attn)]
                           + [lpart[0]])
    rows = [dmod_ex,
            jnp.concatenate(dgain, axis=0), jnp.zeros((4, D), F32),
            jnp.concatenate([_pad_cols(db_qkv[jj][0:1], 2 * D).reshape(2, D) for jj in range(n_attn)], axis=0),
            jnp.concatenate([db_o[jj][0:1] for jj in range(n_attn)], axis=0),
            jnp.concatenate([dconv_w[jj][0:3] for jj in range(n_conv)], axis=0),
            jnp.pad(misc, (0, D - misc.shape[0])).reshape(1, D)]
    packed = jnp.concatenate(rows, axis=0)
    packed = jnp.pad(packed, ((0, N_SMALL_ROWS - packed.shape[0]), (0, 0)))
    p_all = allgather_small(packed, True)
    red, exsum = reduce_small(p_all)

    r0 = nex * N_DMOD_ROWS
    grads = {}
    grads["b_ada"] = exsum[:DEPTH * 9].reshape(DEPTH, 9 * D)
    grads["norm_gain"] = lax.dynamic_slice_in_dim(red[r0:r0 + 12].reshape(DEPTH, 3, D), chip * (D // N_CHIPS),
                                                  D // N_CHIPS, axis=2)
    r1 = r0 + 16
    grads["attn_b_qkv"] = red[r1:r1 + 2 * n_attn].reshape(n_attn, 2 * D)[:, :QKV_DIM]
    r2 = r1 + 2 * n_attn
    grads["attn_b_o"] = red[r2:r2 + n_attn]
    r3 = r2 + n_attn
    grads["conv_w"] = lax.dynamic_slice_in_dim(red[r3:r3 + 3 * n_conv].reshape(n_conv, 3, D), chip * (D // N_CHIPS),
                                               D // N_CHIPS, axis=2)
    mrow = red[r3 + 3 * n_conv]
    grads["attn_q_gain"] = jnp.stack([mrow[128 * jj:128 * jj + HEAD_DIM] for jj in range(n_attn)])
    grads["attn_k_gain"] = jnp.stack([mrow[128 * jj + HEAD_DIM:128 * jj + 128] for jj in range(n_attn)])
    grads["attn_sinks"] = jnp.stack([mrow[128 * (n_attn + jj):128 * (n_attn + jj) + N_HEADS] for jj in range(n_attn)])
    loss = mrow[128 * 2 * n_attn]

    dm_all = p_all[:, :r0].reshape(8, nex, N_DMOD_ROWS, D)[:, :, :DEPTH * 9].reshape(8 * nex, DEPTH, 9 * D)
    dm_mine = lax.dynamic_slice_in_dim(dm_all.transpose(1, 0, 2), chip * ada_cols, ada_cols, axis=2)
    g_ada, d_ada, nm_ada, nv_ada = ada_bwd_adam(c_all, dm_mine, w_ada, m_w_ada, v_w_ada)

    delta, new_m, new_v = {}, {}, {}
    for n in WEIGHTS:
        if n == "w_ada":
            grads[n], delta[n], new_m[n], new_v[n] = g_ada, d_ada, nm_ada, nv_ada
        elif n not in BIG:
            delta[n], new_m[n], new_v[n] = adamw(w[n], grads[n], mom[n], var[n])
    finish(*state["flight"], (delta["conv_w"], d_ada))
    for b, n in enumerate(BIG):
        grads[n] = gshard[b].reshape(w[n].shape)
        delta[n], new_m[n], new_v[n] = (u.reshape(w[n].shape) for u in upd[b])

    return (loss, grad_x, *[grads[n] for n in WEIGHTS], *[delta[n] for n in WEIGHTS],
            *[new_m[n] for n in WEIGHTS], *[new_v[n] for n in WEIGHTS])
```

```python
import functools

import jax
import jax.numpy as jnp
from jax import lax
from jax.experimental import pallas as pl
from jax.experimental.pallas import tpu as pltpu

F32 = jnp.float32
BF16 = jnp.bfloat16

D = 1024
D_FF = 2816
N_HEADS = 16
N_KV = 4
HEAD_DIM = 64
GROUP = N_HEADS // N_KV
QK_DIM = (N_HEADS + N_KV) * HEAD_DIM
QKV_DIM = QK_DIM + N_KV * HEAD_DIM
BLOCK = 128
ROPE_THETA = 10000.0
EPS = 1e-6
DEPTH = 4
N_CHIPS = 4

ADAM_LR = 0.001
ADAM_B1 = 0.9
ADAM_B2 = 0.999
ADAM_EPS = 1e-08
ADAM_WD = 0.01
ADAM_STEP = 10

V7X_VMEM_BYTES = 64 * 1024 * 1024
V7X_MXU_DIM = 256
FF_CHUNKS = ((0, 1536), (1536, D_FF))
assert all((hi - lo) % V7X_MXU_DIM == 0 for lo, hi in FF_CHUNKS)
MESH = pl.DeviceIdType.MESH
ANY = pl.BlockSpec(memory_space=pl.ANY)


def _cparams(vmem_mb, n_grid):
    assert vmem_mb * 1024 * 1024 <= V7X_VMEM_BYTES
    return pltpu.CompilerParams(vmem_limit_bytes=vmem_mb * 1024 * 1024,
                                dimension_semantics=("arbitrary",) * n_grid)


def _resident(shape):
    nd = len(shape)
    return pl.BlockSpec(shape, lambda *_: (0,) * nd, pipeline_mode=pl.Buffered(1))


def _layer(w, l):
    return pl.BlockSpec((None,) + w.shape[1:], lambda *_: (l, 0, 0), pipeline_mode=pl.Buffered(1))


PIN_BYTES = 1 << 20


def _pin(*args):
    return [pltpu.with_memory_space_constraint(a, pltpu.HBM) if a.size * a.dtype.itemsize >= PIN_BYTES else a
            for a in args]


def _mid(shape, dtype):
    n = 1
    for d in shape:
        n *= d
    if n * jnp.dtype(dtype).itemsize >= PIN_BYTES:
        return pltpu.HBM(tuple(shape), dtype)
    return jax.ShapeDtypeStruct(tuple(shape), dtype)


def _dot(a, b):
    return jnp.dot(a, b, preferred_element_type=F32)


def _dot_nt(a, b):
    return lax.dot_general(a, b, (((1,), (1,)), ((), ())), preferred_element_type=F32)


def _dot_tn(a, b):
    return lax.dot_general(a, b, (((0,), (0,)), ((), ())), preferred_element_type=F32)


def _dot_hilo(a, g):
    hi = a.astype(BF16)
    lo = (a - hi.astype(F32)).astype(BF16)
    return _dot(hi, g) + _dot(lo, g)


def _sigmoid(x):
    return 0.5 * jnp.tanh(0.5 * x) + 0.5


def _colsum(a):
    return jnp.sum(a, axis=0, keepdims=True)


def _norm_mod(x, gain, sc, sh):
    r = lax.rsqrt(jnp.mean(x * x, axis=-1, keepdims=True) + EPS)
    n = x * r * gain
    return r, n, n * (1.0 + sc) + sh


def _mod_rows(mod_ref, s):
    return (mod_ref[0, 3 * s:3 * s + 1, :], mod_ref[0, 3 * s + 1:3 * s + 2, :], mod_ref[0, 3 * s + 2:3 * s + 3, :])


def ffn_fwd(x, gain, mod, wup, wdn, l, s, tm=512, after=()):
    t = x.shape[0]
    tpe = t // tm // 2

    def body(x_ref, gain_ref, mod_ref, wup_ref, wdn_ref, *rest):
        xo_ref, u_ref, f_ref = rest[len(after):]
        xv = x_ref[...]
        sh, sc, g = _mod_rows(mod_ref, s)
        _, _, h = _norm_mod(xv, gain_ref[...], sc, sh)
        hb = h.astype(BF16)
        acc = jnp.zeros((tm, D), F32)
        for lo, hi in FF_CHUNKS:
            gate = _dot(hb, wup_ref[:, lo:hi])
            up = _dot(hb, wup_ref[:, D_FF + lo:D_FF + hi])
            u_ref[:, lo:hi] = gate.astype(BF16)
            u_ref[:, D_FF + lo:D_FF + hi] = up.astype(BF16)
            a = (gate * _sigmoid(gate) * up).astype(BF16)
            acc = acc + _dot(a, wdn_ref[lo:hi, :])
        f_ref[...] = acc.astype(BF16)
        xo_ref[...] = xv + 0.5 * g * acc

    return pl.pallas_call(
        body, name="ffn_fwd", grid=(t // tm,),
        in_specs=[pl.BlockSpec((tm, D), lambda i: (i, 0)),
                  pl.BlockSpec((1, D), lambda i: (0, 0)),
                  pl.BlockSpec((1, 9, D), lambda i: (i // tpe, 0, 0)),
                  _layer(wup, l), _layer(wdn, l)] + [ANY] * len(after),
        out_specs=[pl.BlockSpec((tm, D), lambda i: (i, 0)),
                   pl.BlockSpec((tm, 2 * D_FF), lambda i: (i, 0)),
                   pl.BlockSpec((tm, D), lambda i: (i, 0))],
        out_shape=[_mid((t, D), F32), _mid((t, 2 * D_FF), BF16), _mid((t, D), BF16)],
        compiler_params=_cparams(60, 1),
    )(*_pin(x, gain, mod, wup, wdn, *after))


def ffn_bwd_act(dy, u, f, mod, wdn, l, s, tm=256, after=()):
    t = dy.shape[0]
    tpe = t // tm // 2

    def body(dy_ref, u_ref, f_ref, mod_ref, wdn_ref, *rest):
        du_ref, a_ref, df_ref, sg_ref = rest[len(after):]
        i = pl.program_id(0)
        dyv = dy_ref[...]
        _, _, g = _mod_rows(mod_ref, s)
        dfb = (0.5 * g * dyv).astype(BF16)
        df_ref[...] = dfb

        @pl.when(i % tpe == 0)
        def _():
            sg_ref[...] = jnp.zeros_like(sg_ref)

        sg_ref[0, 0:1, :] += _colsum(0.5 * dyv * f_ref[...].astype(F32))
        for lo, hi in FF_CHUNKS:
            da = _dot_nt(dfb, wdn_ref[lo:hi, :])
            gate = u_ref[:, lo:hi].astype(F32)
            up = u_ref[:, D_FF + lo:D_FF + hi].astype(F32)
            sg = _sigmoid(gate)
            silu = gate * sg
            a_ref[:, lo:hi] = (silu * up).astype(BF16)
            du_ref[:, lo:hi] = (da * up * (sg + silu * (1.0 - sg))).astype(BF16)
            du_ref[:, D_FF + lo:D_FF + hi] = (da * silu).astype(BF16)

    return pl.pallas_call(
        body, name="ffn_bwd_act", grid=(t // tm,),
        in_specs=[pl.BlockSpec((tm, D), lambda i: (i, 0)),
                  pl.BlockSpec((tm, 2 * D_FF), lambda i: (i, 0)),
                  pl.BlockSpec((tm, D), lambda i: (i, 0)),
                  pl.BlockSpec((1, 9, D), lambda i: (i // tpe, 0, 0)),
                  _layer(wdn, l)] + [ANY] * len(after),
        out_specs=[pl.BlockSpec((tm, 2 * D_FF), lambda i: (i, 0)),
                   pl.BlockSpec((tm, D_FF), lambda i: (i, 0)),
                   pl.BlockSpec((tm, D), lambda i: (i, 0)),
                   pl.BlockSpec((1, 8, D), lambda i: (i // tpe, 0, 0))],
        out_shape=[_mid((t, 2 * D_FF), BF16), _mid((t, D_FF), BF16), _mid((t, D), BF16), _mid((2, 8, D), F32)],
        compiler_params=_cparams(40, 1),
    )(*_pin(dy, u, f, mod, wdn, *after))


def lin_bwd(dy, dz, w, l, x, gain, mod, s, want_db, tm=512):
    t = dy.shape[0]
    n = w.shape[2]
    tpe = t // tm // 2

    def body(dy_ref, dz_ref, w_ref, x_ref, gain_ref, mod_ref, dx_ref, h_ref, se_ref, sa_ref, *db_ref):
        i = pl.program_id(0)
        xv = x_ref[...]
        gain_v = gain_ref[...]
        sh, sc, _ = _mod_rows(mod_ref, s)
        r, nrm, h = _norm_mod(xv, gain_v, sc, sh)
        h_ref[...] = h.astype(BF16)
        dh = _dot_nt(dz_ref[...], w_ref[...])
        dn = dh * (1.0 + sc)
        dxr = dn * gain_v
        m = jnp.mean(dxr * xv, axis=-1, keepdims=True)
        dx_ref[...] = dy_ref[...] + r * dxr - xv * (r * r * r) * m

        @pl.when(i % tpe == 0)
        def _():
            se_ref[...] = jnp.zeros_like(se_ref)

        @pl.when(i == 0)
        def _():
            sa_ref[...] = jnp.zeros_like(sa_ref)
            if want_db:
                db_ref[0][...] = jnp.zeros_like(db_ref[0])

        se_ref[0, 0:1, :] += _colsum(dh)
        se_ref[0, 1:2, :] += _colsum(dh * nrm)
        sa_ref[0:1, :] += _colsum(dn * xv * r)
        if want_db:
            db_ref[0][0:1, :] += _colsum(dz_ref[...].astype(F32))

    out_specs = [pl.BlockSpec((tm, D), lambda i: (i, 0)), pl.BlockSpec((tm, D), lambda i: (i, 0)),
                 pl.BlockSpec((1, 8, D), lambda i: (i // tpe, 0, 0)), pl.BlockSpec((8, D), lambda i: (0, 0))]
    out_shape = [_mid((t, D), F32), _mid((t, D), BF16),
                 jax.ShapeDtypeStruct((2, 8, D), F32), jax.ShapeDtypeStruct((8, D), F32)]
    if want_db:
        out_specs.append(pl.BlockSpec((8, n), lambda i: (0, 0)))
        out_shape.append(jax.ShapeDtypeStruct((8, n), F32))
    return pl.pallas_call(
        body, name="lin_bwd", grid=(t // tm,),
        in_specs=[pl.BlockSpec((tm, D), lambda i: (i, 0)),
                  pl.BlockSpec((tm, n), lambda i: (i, 0)),
                  _layer(w, l),
                  pl.BlockSpec((tm, D), lambda i: (i, 0)),
                  pl.BlockSpec((1, D), lambda i: (0, 0)),
                  pl.BlockSpec((1, 9, D), lambda i: (i // tpe, 0, 0))],
        out_specs=out_specs, out_shape=out_shape,
        compiler_params=_cparams(56, 1),
    )(*_pin(dy, dz, w, x, gain, mod))


def wgrad(gstack, l, a, b, bm, bn, bt=1024):
    t, m = a.shape
    n = b.shape[1]
    nt = t // bt

    def body(g_ref, a_ref, b_ref, o_ref, acc_ref):
        k = pl.program_id(2)

        @pl.when(k == 0)
        def _():
            acc_ref[...] = jnp.zeros_like(acc_ref)

        acc_ref[...] += _dot_tn(a_ref[...], b_ref[...])

        @pl.when(k == nt - 1)
        def _():
            o_ref[...] = acc_ref[...].astype(BF16)

    return pl.pallas_call(
        body, name="wgrad", grid=(m // bm, n // bn, nt),
        in_specs=[ANY, pl.BlockSpec((bt, bm), lambda i, j, k: (k, i)),
                  pl.BlockSpec((bt, bn), lambda i, j, k: (k, j))],
        out_specs=pl.BlockSpec((None, bm, bn), lambda i, j, k: (l, i, j)),
        out_shape=_mid(gstack.shape, BF16),
        input_output_aliases={0: 0},
        scratch_shapes=[pltpu.VMEM((bm, bn), F32)],
        compiler_params=_cparams(48, 3),
    )(*_pin(gstack, a, b))


def proj_res(x, o, w, l, b, mod, tm=512):
    t = x.shape[0]
    tpe = t // tm // 2

    def body(x_ref, o_ref, w_ref, b_ref, mod_ref, xo_ref, y_ref):
        _, _, g = _mod_rows(mod_ref, 1)
        y = _dot(o_ref[...], w_ref[...]) + b_ref[...]
        y_ref[...] = y.astype(BF16)
        xo_ref[...] = x_ref[...] + g * y

    return pl.pallas_call(
        body, name="proj_res", grid=(t // tm,),
        in_specs=[pl.BlockSpec((tm, D), lambda i: (i, 0)), pl.BlockSpec((tm, D), lambda i: (i, 0)),
                  _layer(w, l), pl.BlockSpec((1, D), lambda i: (0, 0)),
                  pl.BlockSpec((1, 9, D), lambda i: (i // tpe, 0, 0))],
        out_specs=[pl.BlockSpec((tm, D), lambda i: (i, 0)), pl.BlockSpec((tm, D), lambda i: (i, 0))],
        out_shape=[_mid((t, D), F32), _mid((t, D), BF16)],
        compiler_params=_cparams(32, 1),
    )(*_pin(x, o, w, b, mod))


def proj_res_bwd(dy, y, w, l, mod, tm=512):
    t = dy.shape[0]
    tpe = t // tm // 2

    def body(dy_ref, y_ref, w_ref, mod_ref, dyy_ref, do_ref, se_ref, sa_ref):
        i = pl.program_id(0)
        _, _, g = _mod_rows(mod_ref, 1)
        dyv = dy_ref[...]
        dyy = g * dyv
        dyb = dyy.astype(BF16)
        dyy_ref[...] = dyb
        do_ref[...] = _dot_nt(dyb, w_ref[...]).astype(BF16)

        @pl.when(i % tpe == 0)
        def _():
            se_ref[...] = jnp.zeros_like(se_ref)

        @pl.when(i == 0)
        def _():
            sa_ref[...] = jnp.zeros_like(sa_ref)

        se_ref[0, 0:1, :] += _colsum(dyv * y_ref[...].astype(F32))
        sa_ref[0:1, :] += _colsum(dyy)

    return pl.pallas_call(
        body, name="proj_res_bwd", grid=(t // tm,),
        in_specs=[pl.BlockSpec((tm, D), lambda i: (i, 0)), pl.BlockSpec((tm, D), lambda i: (i, 0)),
                  _layer(w, l), pl.BlockSpec((1, 9, D), lambda i: (i // tpe, 0, 0))],
        out_specs=[pl.BlockSpec((tm, D), lambda i: (i, 0)), pl.BlockSpec((tm, D), lambda i: (i, 0)),
                   pl.BlockSpec((1, 8, D), lambda i: (i // tpe, 0, 0)), pl.BlockSpec((8, D), lambda i: (0, 0))],
        out_shape=[_mid((t, D), BF16), _mid((t, D), BF16), _mid((2, 8, D), F32), _mid((8, D), F32)],
        compiler_params=_cparams(32, 1),
    )(*_pin(dy, y, w, mod))


def lin_fwd(x, gain, mod, w, l, s, tm=512):
    t = x.shape[0]
    n = w.shape[2]
    tpe = t // tm // 2

    def body(x_ref, gain_ref, mod_ref, w_ref, z_ref):
        sh, sc, _ = _mod_rows(mod_ref, s)
        _, _, h = _norm_mod(x_ref[...], gain_ref[...], sc, sh)
        z_ref[...] = _dot(h.astype(BF16), w_ref[...]).astype(BF16)

    return pl.pallas_call(
        body, name="lin_fwd", grid=(t // tm,),
        in_specs=[pl.BlockSpec((tm, D), lambda i: (i, 0)), pl.BlockSpec((1, D), lambda i: (0, 0)),
                  pl.BlockSpec((1, 9, D), lambda i: (i // tpe, 0, 0)), _layer(w, l)],
        out_specs=pl.BlockSpec((tm, n), lambda i: (i, 0)),
        out_shape=_mid((t, n), BF16),
        compiler_params=_cparams(40, 1),
    )(*_pin(x, gain, mod, w))


def rope_tables(pos, invf):
    t = pos.shape[0]
    tm = 1024

    def body(pos_ref, invf_ref, c_ref, s_ref):
        ang = pos_ref[...].astype(F32) * invf_ref[...]
        lane = lax.broadcasted_iota(jnp.int32, (tm, 128), 1)
        sign = jnp.where(lane % HEAD_DIM < HEAD_DIM // 2, -1.0, 1.0)
        c_ref[...] = jnp.cos(ang)
        s_ref[...] = sign * jnp.sin(ang)

    return pl.pallas_call(
        body, name="rope_tables", grid=(t // tm,),
        in_specs=[pl.BlockSpec((tm, 1), lambda i: (i, 0)), pl.BlockSpec((1, 128), lambda i: (0, 0))],
        out_specs=[pl.BlockSpec((tm, 128), lambda i: (i, 0))] * 2,
        out_shape=[jax.ShapeDtypeStruct((t, 128), F32)] * 2,
        compiler_params=_cparams(16, 1),
    )(pos, invf)


def _swap_halves(v):
    lane = lax.broadcasted_iota(jnp.int32, v.shape, 1)
    return jnp.where(lane % HEAD_DIM < HEAD_DIM // 2, pltpu.roll(v, 128 - HEAD_DIM // 2, 1), pltpu.roll(v, HEAD_DIM // 2, 1))


def _rope(v, cos, sin):
    return jnp.concatenate(
        [v[:, j:j + 128] * cos + _swap_halves(v[:, j:j + 128]) * sin for j in range(0, v.shape[1], 128)], axis=1)


def _rope_t(dv, cos, sin):
    return jnp.concatenate(
        [dv[:, j:j + 128] * cos + _swap_halves(dv[:, j:j + 128] * sin) for j in range(0, dv.shape[1], 128)], axis=1)


def _head_stats(qk, g1, g2):
    rinv = lax.rsqrt(_dot_hilo(qk * qk, g1) + EPS)
    return rinv, _dot_hilo(rinv, g2)


def qkv_fwd(x, gain, mod, w, l, b, gqk, cos, sin, g1, g2, tm=256):
    t = x.shape[0]
    tpe = t // tm // 2

    def body(x_ref, gain_ref, mod_ref, w_ref, b_ref, gqk_ref, c_ref, s_ref, g1_ref, g2_ref, raw_ref, q_ref, k_ref):
        sh, sc, _ = _mod_rows(mod_ref, 1)
        _, _, h = _norm_mod(x_ref[...], gain_ref[...], sc, sh)
        qkv = _dot(h.astype(BF16), w_ref[...]) + b_ref[...]
        raw_ref[...] = qkv.astype(BF16)
        qk = qkv[:, :QK_DIM]
        _, rb = _head_stats(qk, g1_ref[...], g2_ref[...])
        qr = _rope(qk * rb * gqk_ref[...], c_ref[...], s_ref[...])
        q_ref[...] = qr[:, :D].astype(BF16)
        k_ref[...] = qr[:, D:].astype(BF16)

    return pl.pallas_call(
        body, name="qkv_fwd", grid=(t // tm,),
        in_specs=[pl.BlockSpec((tm, D), lambda i: (i, 0)), pl.BlockSpec((1, D), lambda i: (0, 0)),
                  pl.BlockSpec((1, 9, D), lambda i: (i // tpe, 0, 0)), _layer(w, l),
                  pl.BlockSpec((1, QKV_DIM), lambda i: (0, 0)), pl.BlockSpec((1, QK_DIM), lambda i: (0, 0)),
                  pl.BlockSpec((tm, 128), lambda i: (i, 0)), pl.BlockSpec((tm, 128), lambda i: (i, 0)),
                  _resident((QK_DIM, 128)), _resident((128, QK_DIM))],
        out_specs=[pl.BlockSpec((tm, QKV_DIM), lambda i: (i, 0)), pl.BlockSpec((tm, D), lambda i: (i, 0)),
                   pl.BlockSpec((tm, N_KV * HEAD_DIM), lambda i: (i, 0))],
        out_shape=[_mid((t, QKV_DIM), BF16), _mid((t, D), BF16), _mid((t, N_KV * HEAD_DIM), BF16)],
        compiler_params=_cparams(40, 1),
    )(*_pin(x, gain, mod, w, b, gqk, cos, sin, g1, g2))


def qkv_bwd_pre(dq, dk, dv, raw, gqk, cos, sin, g1, g2, gsel, tm=256):
    t = dq.shape[0]

    def body(dq_ref, dk_ref, dv_ref, raw_ref, gqk_ref, c_ref, s_ref, g1_ref, g2_ref, gsel_ref, dz_ref, sa_ref):
        i = pl.program_id(0)
        dqk = jnp.concatenate([dq_ref[...].astype(F32), dk_ref[...]], axis=1)
        dqn = _rope_t(dqk, c_ref[...], s_ref[...])
        qk = raw_ref[:, :QK_DIM].astype(F32)
        g1v, g2v = g1_ref[...], g2_ref[...]
        rinv, rb = _head_stats(qk, g1v, g2v)
        dgq = jnp.broadcast_to(_colsum(dqn * qk * rb), (8, QK_DIM))
        dyh = dqn * gqk_ref[...]
        mh = _dot_hilo(dyh * qk, g1v)
        mb = _dot_hilo(mh * rinv * rinv * rinv, g2v)
        dz_ref[:, :QK_DIM] = (rb * dyh - qk * mb).astype(BF16)
        dz_ref[:, QK_DIM:] = dv_ref[...].astype(BF16)

        @pl.when(i == 0)
        def _():
            sa_ref[...] = jnp.zeros_like(sa_ref)

        sa_ref[...] += _dot_hilo(dgq, gsel_ref[...])

    kvw = N_KV * HEAD_DIM
    return pl.pallas_call(
        body, name="qkv_bwd_pre", grid=(t // tm,),
        in_specs=[pl.BlockSpec((tm, D), lambda i: (i, 0)), pl.BlockSpec((tm, kvw), lambda i: (i, 0)),
                  pl.BlockSpec((tm, kvw), lambda i: (i, 0)), pl.BlockSpec((tm, QKV_DIM), lambda i: (i, 0)),
                  pl.BlockSpec((1, QK_DIM), lambda i: (0, 0)),
                  pl.BlockSpec((tm, 128), lambda i: (i, 0)), pl.BlockSpec((tm, 128), lambda i: (i, 0)),
                  _resident((QK_DIM, 128)), _resident((128, QK_DIM)), _resident((QK_DIM, 128))],
        out_specs=[pl.BlockSpec((tm, QKV_DIM), lambda i: (i, 0)), pl.BlockSpec((8, 128), lambda i: (0, 0))],
        out_shape=[_mid((t, QKV_DIM), BF16), _mid((8, 128), F32)],
        compiler_params=_cparams(40, 1),
    )(*_pin(dq, dk, dv, raw, gqk, cos, sin, g1, g2, gsel))


def _band_mask(n):
    row = lax.broadcasted_iota(jnp.int32, (GROUP * BLOCK, 2 * BLOCK), 0) % BLOCK
    col = lax.broadcasted_iota(jnp.int32, (GROUP * BLOCK, 2 * BLOCK), 1)
    rel = row + BLOCK - col
    return (rel >= 0) & (rel < BLOCK) & ((col >= BLOCK) | (n > 0))


def _stack_heads(v, g):
    base = g * GROUP * HEAD_DIM
    return jnp.concatenate([v[:, base + j * HEAD_DIM:base + (j + 1) * HEAD_DIM] for j in range(GROUP)], axis=0)


def _kv_cat(prev, cur, g):
    return jnp.concatenate([prev[:, g * HEAD_DIM:(g + 1) * HEAD_DIM], cur[:, g * HEAD_DIM:(g + 1) * HEAD_DIM]], axis=0)


def _sink_col(sink_ref, g):
    return jnp.concatenate([jnp.full((BLOCK, 1), sink_ref[0, g * GROUP + j], F32) for j in range(GROUP)], axis=0)


def _attn_specs(nb):
    kvw = N_KV * HEAD_DIM
    vcol = QK_DIM // kvw
    cur = lambda e, n: (e * nb + n, 0)
    prev = lambda e, n: (e * nb + jnp.maximum(n - 1, 0), 0)
    return [pl.BlockSpec((BLOCK, D), cur),
            pl.BlockSpec((BLOCK, kvw), cur), pl.BlockSpec((BLOCK, kvw), prev),
            pl.BlockSpec((BLOCK, kvw), lambda e, n: (e * nb + n, vcol)),
            pl.BlockSpec((BLOCK, kvw), lambda e, n: (e * nb + jnp.maximum(n - 1, 0), vcol)),
            pl.BlockSpec(memory_space=pltpu.SMEM)]


def attn_fwd(q, k, raw, sinks):
    t = q.shape[0]
    nb = t // 2 // BLOCK

    def body(q_ref, kc_ref, kp_ref, vc_ref, vp_ref, sink_ref, o_ref, lse_ref):
        n = pl.program_id(1)
        qv = q_ref[...]
        kc, kp, vc, vp = kc_ref[...], kp_ref[...], vc_ref[...], vp_ref[...]
        mask = _band_mask(n)
        ones = jnp.ones((2 * BLOCK, HEAD_DIM), BF16)
        outs, ms, ls = [], [], []
        for g in range(N_KV):
            kk, vv = _kv_cat(kp, kc, g), _kv_cat(vp, vc, g)
            s = jnp.where(mask, _dot_nt(_stack_heads(qv, g), kk) * (HEAD_DIM ** -0.5), -1e30)
            sink = _sink_col(sink_ref, g)
            m = jnp.maximum(jnp.max(s, axis=1, keepdims=True), sink)
            p = jnp.exp(s - m).astype(BF16)
            pv = _dot(p, jnp.concatenate([vv, ones], axis=1))
            l = pv[:, HEAD_DIM:HEAD_DIM + 1] + jnp.exp(sink - m)
            o = pv[:, :HEAD_DIM] * (1.0 / l)
            for j in range(GROUP):
                outs.append(o[j * BLOCK:(j + 1) * BLOCK, :])
                ms.append(m[j * BLOCK:(j + 1) * BLOCK, :])
                ls.append(l[j * BLOCK:(j + 1) * BLOCK, :])
        o_ref[...] = jnp.concatenate(outs, axis=1).astype(BF16)
        lse_ref[...] = jnp.concatenate(ms, axis=1) + jnp.log(jnp.concatenate(ls, axis=1))

    cur = lambda e, n: (e * nb + n, 0)
    return pl.pallas_call(
        body, name="attn_fwd", grid=(2, nb),
        in_specs=_attn_specs(nb),
        out_specs=[pl.BlockSpec((BLOCK, D), cur), pl.BlockSpec((BLOCK, N_HEADS), cur)],
        out_shape=[_mid((t, D), BF16), _mid((t, N_HEADS), F32)],
        compiler_params=_cparams(32, 2),
    )(*_pin(q, k, k, raw, raw), sinks)


def attn_bwd(q, k, raw, sinks, o, do, lse):
    t = q.shape[0]
    s_len = t // 2
    nb = s_len // BLOCK
    kvw = N_KV * HEAD_DIM

    def body(q_ref, kc_ref, kp_ref, vc_ref, vp_ref, sink_ref, o_ref, do_ref, lse_ref, dq_ref, dk_ref, dv_ref, ds_ref):
        n = pl.program_id(1)

        @pl.when(n == 0)
        def _():
            dk_ref[...] = jnp.zeros_like(dk_ref)
            dv_ref[...] = jnp.zeros_like(dv_ref)

        @pl.when((n == 0) & (pl.program_id(0) == 0))
        def _():
            ds_ref[...] = jnp.zeros_like(ds_ref)

        qv, ov, dov, lsev = q_ref[...], o_ref[...], do_ref[...], lse_ref[...]
        kc, kp, vc, vp = kc_ref[...], kp_ref[...], vc_ref[...], vp_ref[...]
        mask = _band_mask(n)
        dqs, dks, dvs, dsk = [], [], [], []
        for g in range(N_KV):
            kk, vv = _kv_cat(kp, kc, g), _kv_cat(vp, vc, g)
            qg, og, dog = _stack_heads(qv, g), _stack_heads(ov, g), _stack_heads(dov, g)
            lse = jnp.concatenate([lsev[:, g * GROUP + j:g * GROUP + j + 1] for j in range(GROUP)], axis=0)
            s = jnp.where(mask, _dot_nt(qg, kk) * (HEAD_DIM ** -0.5), -1e30)
            p = jnp.exp(s - lse)
            dd = _dot_hilo(dog.astype(F32) * og.astype(F32), jnp.ones((HEAD_DIM, 128), BF16))[:, :1]
            ds = (p * (_dot_nt(dog, vv) - dd) * (HEAD_DIM ** -0.5)).astype(BF16)
            dqg = _dot(ds, kk)
            dks.append(_dot_tn(ds, qg))
            dvs.append(_dot_tn(p.astype(BF16), dog))
            wsink = jnp.exp(_sink_col(sink_ref, g) - lse) * dd
            for j in range(GROUP):
                dqs.append(dqg[j * BLOCK:(j + 1) * BLOCK, :])
                dsk.append(wsink[j * BLOCK:(j + 1) * BLOCK, :])
        dq_ref[...] = jnp.concatenate(dqs, axis=1).astype(BF16)
        dkk = jnp.concatenate(dks, axis=1)
        dvv = jnp.concatenate(dvs, axis=1)
        prev0 = pl.multiple_of(jnp.maximum(n - 1, 0) * BLOCK, BLOCK)
        cur0 = pl.multiple_of(n * BLOCK, BLOCK)
        dk_ref[pl.ds(prev0, BLOCK), :] += dkk[:BLOCK]
        dv_ref[pl.ds(prev0, BLOCK), :] += dvv[:BLOCK]
        dk_ref[pl.ds(cur0, BLOCK), :] += dkk[BLOCK:]
        dv_ref[pl.ds(cur0, BLOCK), :] += dvv[BLOCK:]
        ds_ref[0:1, :] -= _colsum(jnp.concatenate(dsk, axis=1))

    cur = lambda e, n: (e * nb + n, 0)
    return pl.pallas_call(
        body, name="attn_bwd", grid=(2, nb),
        in_specs=_attn_specs(nb) + [pl.BlockSpec((BLOCK, D), cur), pl.BlockSpec((BLOCK, D), cur),
                                    pl.BlockSpec((BLOCK, N_HEADS), cur)],
        out_specs=[pl.BlockSpec((BLOCK, D), cur), pl.BlockSpec((s_len, kvw), lambda e, n: (e, 0)),
                   pl.BlockSpec((s_len, kvw), lambda e, n: (e, 0)), pl.BlockSpec((8, N_HEADS), lambda e, n: (0, 0))],
        out_shape=[_mid((t, D), BF16), _mid((t, kvw), F32), _mid((t, kvw), F32), _mid((8, N_HEADS), F32)],
        compiler_params=_cparams(32, 2),
    )(*_pin(q, k, k, raw, raw), sinks, *_pin(o, do, lse))


CONV_COLS = 256


def _conv_specs(s_len):
    nct = D // CONV_COLS
    return [pl.BlockSpec((s_len, CONV_COLS), lambda j, e, *_: (e, j)),
            pl.BlockSpec((s_len, CONV_COLS), lambda j, e, *_: (e, nct + j)),
            pl.BlockSpec((s_len, CONV_COLS), lambda j, e, *_: (e, 2 * nct + j)),
            pl.BlockSpec((3, CONV_COLS), lambda j, e, *_: (0, j))]


def _conv_taps(gc, v, w, s_len):
    u = gc * v
    row = lax.broadcasted_iota(jnp.int32, u.shape, 0)
    u1 = jnp.where(row >= 1, pltpu.roll(u, 1, 0), 0.0)
    u2 = jnp.where(row >= 2, pltpu.roll(u, 2, 0), 0.0)
    return u, u1, u2, w[2:3, :] * u + w[1:2, :] * u1 + w[0:1, :] * u2


def conv_fwd(z, w):
    t = z.shape[0]
    s_len = t // 2

    def body(gb_ref, gc_ref, v_ref, w_ref, p_ref):
        _, _, _, conv = _conv_taps(gc_ref[...].astype(F32), v_ref[...].astype(F32), w_ref[...], s_len)
        p_ref[...] = (gb_ref[...].astype(F32) * conv).astype(BF16)

    return pl.pallas_call(
        body, name="conv_fwd", grid=(D // CONV_COLS, 2),
        in_specs=_conv_specs(s_len),
        out_specs=pl.BlockSpec((s_len, CONV_COLS), lambda j, e: (e, j)),
        out_shape=_mid((t, D), BF16),
        compiler_params=_cparams(40, 2),
    )(*_pin(z, z, z, w))


def conv_bwd(z, w, dp):
    t = z.shape[0]
    s_len = t // 2
    nct = D // CONV_COLS

    def body(gb_ref, gc_ref, v_ref, w_ref, dp_ref, dz_ref, dw_ref, parts_ref):
        e, part = pl.program_id(1), pl.program_id(2)

        @pl.when(part == 0)
        def _():
            gc, v, wv = gc_ref[...].astype(F32), v_ref[...].astype(F32), w_ref[...]
            u, u1, u2, conv = _conv_taps(gc, v, wv, s_len)
            dpv = dp_ref[...].astype(F32)
            parts_ref[0] = (dpv * conv).astype(BF16)
            dc = dpv * gb_ref[...].astype(F32)
            row = lax.broadcasted_iota(jnp.int32, dc.shape, 0)
            dc1 = jnp.where(row <= s_len - 2, pltpu.roll(dc, s_len - 1, 0), 0.0)
            dc2 = jnp.where(row <= s_len - 3, pltpu.roll(dc, s_len - 2, 0), 0.0)
            du = wv[2:3, :] * dc + wv[1:2, :] * dc1 + wv[0:1, :] * dc2
            parts_ref[1] = (du * v).astype(BF16)
            parts_ref[2] = (du * gc).astype(BF16)

            @pl.when(e == 0)
            def _():
                dw_ref[...] = jnp.zeros_like(dw_ref)

            dw_ref[0:1, :] += _colsum(dc * u2)
            dw_ref[1:2, :] += _colsum(dc * u1)
            dw_ref[2:3, :] += _colsum(dc * u)

        dz_ref[...] = parts_ref[part]

    return pl.pallas_call(
        body, name="conv_bwd", grid=(nct, 2, 3),
        in_specs=_conv_specs(s_len) + [pl.BlockSpec((s_len, CONV_COLS), lambda j, e, part: (e, j))],
        out_specs=[pl.BlockSpec((s_len, CONV_COLS), lambda j, e, part: (e, part * nct + j)),
                   pl.BlockSpec((8, CONV_COLS), lambda j, e, part: (0, j))],
        out_shape=[_mid((t, 3 * D), BF16), jax.ShapeDtypeStruct((8, D), F32)],
        scratch_shapes=[pltpu.VMEM((3, s_len, CONV_COLS), BF16)],
        compiler_params=_cparams(48, 3),
    )(*_pin(z, z, z, w, dp))


def loss_grad(y, tgt, tm=512):
    t = y.shape[0]

    def body(y_ref, t_ref, dy_ref, l_ref):
        i = pl.program_id(0)
        d = y_ref[...] - t_ref[...]
        dy_ref[...] = d * (1.0 / D)

        @pl.when(i == 0)
        def _():
            l_ref[...] = jnp.zeros_like(l_ref)

        l_ref[...] += 0.5 / D * jnp.sum(d * d)

    return pl.pallas_call(
        body, name="loss_grad", grid=(t // tm,),
        in_specs=[pl.BlockSpec((tm, D), lambda i: (i, 0))] * 2,
        out_specs=[pl.BlockSpec((tm, D), lambda i: (i, 0)), pl.BlockSpec((8, 128), lambda i: (0, 0))],
        out_shape=[_mid((t, D), F32), _mid((8, 128), F32)],
        compiler_params=_cparams(32, 1),
    )(*_pin(y, tgt))


ADA_COLS = 384


def ada_fwd(c_all, w):
    nl, _, n = w.shape
    nex = c_all.shape[0]

    def body(c_ref, w_ref, o_ref):
        cv = c_ref[...]
        ca = (cv * jax.nn.sigmoid(cv)).astype(BF16)
        o_ref[0] = _dot(ca, w_ref[0].astype(BF16))

    return pl.pallas_call(
        body, name="ada_fwd", grid=(nl, n // ADA_COLS),
        in_specs=[pl.BlockSpec((nex, D), lambda l, j: (0, 0)), pl.BlockSpec((1, D, ADA_COLS), lambda l, j: (l, 0, j))],
        out_specs=pl.BlockSpec((1, nex, ADA_COLS), lambda l, j: (l, 0, j)),
        out_shape=jax.ShapeDtypeStruct((nl, nex, n), F32),
        compiler_params=_cparams(32, 2),
    )(*_pin(c_all, w))


def _adam_math(w, g, m, v):
    m = ADAM_B1 * m + (1.0 - ADAM_B1) * g
    v = ADAM_B2 * v + (1.0 - ADAM_B2) * (g * g)
    m_hat = m / (1.0 - ADAM_B1 ** ADAM_STEP)
    v_hat = v / (1.0 - ADAM_B2 ** ADAM_STEP)
    return -ADAM_LR * (m_hat / (jnp.sqrt(v_hat) + ADAM_EPS) + ADAM_WD * w), m, v


def ada_bwd_adam(c_all, dm, w, m, v):
    nl, _, n = w.shape
    nex = c_all.shape[0]

    def body(c_ref, dm_ref, w_ref, m_ref, v_ref, g_ref, d_ref, mo_ref, vo_ref):
        cv = c_ref[...]
        ca = (cv * jax.nn.sigmoid(cv)).astype(BF16)
        g = _dot_tn(ca, dm_ref[0].astype(BF16))
        g_ref[0] = g
        d_ref[0], mo_ref[0], vo_ref[0] = _adam_math(w_ref[0], g, m_ref[0], v_ref[0])

    wspec = pl.BlockSpec((1, D, ADA_COLS), lambda l, j: (l, 0, j))
    return pl.pallas_call(
        body, name="ada_bwd_adam", grid=(nl, n // ADA_COLS),
        in_specs=[pl.BlockSpec((nex, D), lambda l, j: (0, 0)), pl.BlockSpec((1, nex, ADA_COLS), lambda l, j: (l, 0, j)),
                  wspec, wspec, wspec],
        out_specs=[wspec] * 4,
        out_shape=[jax.ShapeDtypeStruct(w.shape, F32)] * 4,
        compiler_params=_cparams(40, 2),
    )(*_pin(c_all, dm, w, m, v))


def adamw(w, g, m, v):
    shape = w.shape
    cols = shape[-1]
    rows = w.size // cols
    args = [a.reshape(rows, cols) for a in (w, g, m, v)]
    tr = rows
    while tr * cols * 4 > (1 << 20) and tr % 16 == 0:
        tr //= 2

    def body(w_ref, g_ref, m_ref, v_ref, d_ref, mo_ref, vo_ref):
        d_ref[...], mo_ref[...], vo_ref[...] = _adam_math(w_ref[...], g_ref[...], m_ref[...], v_ref[...])

    spec = pl.BlockSpec((tr, cols), lambda i: (i, 0))
    outs = pl.pallas_call(
        body, name="adamw", grid=(rows // tr,),
        in_specs=[spec] * 4, out_specs=[spec] * 3,
        out_shape=[jax.ShapeDtypeStruct((rows, cols), F32)] * 3,
        compiler_params=_cparams(32, 1),
    )(*args)
    return [o.reshape(shape) for o in outs]


def adamw_layers(w, g, m, v, prev, l0, n):
    _, r, c = w.shape
    tr = r
    while tr * c * 4 > (2 << 20) and tr % 16 == 0:
        tr //= 2

    def body(w_ref, g_ref, m_ref, v_ref, pd_ref, pm_ref, pv_ref, d_ref, mo_ref, vo_ref):
        d_ref[...], mo_ref[...], vo_ref[...] = _adam_math(w_ref[...], g_ref[...], m_ref[...], v_ref[...])

    spec = pl.BlockSpec((1, tr, c), lambda l, i: (l0 + l, i, 0))
    return pl.pallas_call(
        body, name="adamw_layers", grid=(n, r // tr),
        in_specs=[spec] * 4 + [ANY] * 3, out_specs=[spec] * 3,
        out_shape=[jax.ShapeDtypeStruct(w.shape, F32)] * 3,
        input_output_aliases={4: 0, 5: 1, 6: 2},
        compiler_params=_cparams(40, 2),
    )(*_pin(w, g, m, v, *prev))


def cast_into_window(w, l0, n, ax, chip, after=()):
    _, r, c = w.shape
    tr = r
    while tr * c * 4 > (4 << 20) and tr % 32 == 0:
        tr //= 2
    nrb = r // tr
    full = (n, r * N_CHIPS, c) if ax == 1 else (n, r, c * N_CHIPS)

    def body(chip_ref, w_ref, *rest):
        o_ref = rest[len(after)]
        o_ref[...] = w_ref[...].astype(BF16)

    def omap(l, i, chip_ref):
        return (l, chip_ref[0] * nrb + i, 0) if ax == 1 else (l, i, chip_ref[0])

    return pl.pallas_call(
        body, name="cast_into_window",
        grid_spec=pltpu.PrefetchScalarGridSpec(
            num_scalar_prefetch=1, grid=(n, nrb),
            in_specs=[pl.BlockSpec((1, tr, c), lambda l, i, chip_ref: (l0 + l, i, 0))] + [ANY] * len(after),
            out_specs=pl.BlockSpec((1, tr, c), omap)),
        out_shape=_mid(full, BF16), compiler_params=_cparams(32, 2),
    )(chip, *_pin(w, *after))


def add_bias(a, b):
    def body(a_ref, b_ref, o_ref):
        o_ref[...] = a_ref[...] + b_ref[...]

    return pl.pallas_call(body, name="add_bias", out_shape=jax.ShapeDtypeStruct(a.shape, F32))(a, b)


N_DMOD_ROWS = 40


def reduce_small(p_all):
    rows = p_all.shape[1]

    def body(p_ref, red_ref, ex_ref):
        acc = p_ref[0]
        for d in range(1, 8):
            acc = acc + p_ref[d]
        red_ref[...] = acc
        ex_ref[...] = acc[:N_DMOD_ROWS] + acc[N_DMOD_ROWS:2 * N_DMOD_ROWS]

    return pl.pallas_call(
        body, name="reduce_small",
        out_shape=[jax.ShapeDtypeStruct((rows, D), F32), jax.ShapeDtypeStruct((N_DMOD_ROWS, D), F32)],
        compiler_params=_cparams(32, 0),
    )(p_all)


def _place():
    return lax.axis_index("x"), lax.axis_index("y"), lax.axis_index("c")


def _other_chips(x, y):
    return [(1 - x, y), (x, 1 - y), (1 - x, 1 - y)]


def _sl(ref, axis, start, size):
    idx = [slice(None)] * len(ref.shape)
    idx[axis] = pl.ds(pl.multiple_of(start, 16), size)
    return ref.at[tuple(idx)]


def _rcopy(src, dst, send_sem, recv_sem, to):
    return pltpu.make_async_remote_copy(src_ref=src, dst_ref=dst, send_sem=send_sem, recv_sem=recv_sem,
                                        device_id=to, device_id_type=MESH)


def allgather_small(v, all_devices):
    rows, cols = v.shape
    flips = [(dx, dy, dc) for dx in (0, 1) for dy in (0, 1) for dc in (0, 1)
             if (dx, dy, dc) != (0, 0, 0) and (all_devices or dc == 0)]
    n_out = 8 if all_devices else 4

    def body(v_ref, o_ref, send_sems, recv_sems):
        x, y, c = _place()

        def slot(px, py, pc):
            return 4 * px + 2 * py + pc if all_devices else 2 * px + py

        peers = [(1 - x if dx else x, 1 - y if dy else y, 1 - c if dc else c) for dx, dy, dc in flips]
        sends = [_rcopy(v_ref, o_ref.at[slot(x, y, c)], send_sems.at[r], recv_sems.at[r], peer)
                 for r, peer in enumerate(peers)]
        for cp in sends:
            cp.start()
        o_ref[slot(x, y, c)] = v_ref[...]
        for r, peer in enumerate(peers):
            _rcopy(v_ref, o_ref.at[slot(*peer)], send_sems.at[r], recv_sems.at[r], peer).wait_recv()
        for cp in sends:
            cp.wait_send()

    vm = pl.BlockSpec(memory_space=pltpu.VMEM)
    return pl.pallas_call(
        body, name="allgather_small_all" if all_devices else "allgather_small_chips",
        in_specs=[vm], out_specs=vm,
        out_shape=jax.ShapeDtypeStruct((n_out, rows, cols), v.dtype),
        scratch_shapes=[pltpu.SemaphoreType.DMA((len(flips),)), pltpu.SemaphoreType.DMA((len(flips),))],
        compiler_params=pltpu.CompilerParams(vmem_limit_bytes=32 * 1024 * 1024),
    )(v)


HBM = pl.BlockSpec(memory_space=pltpu.HBM)
SEM = pl.BlockSpec(memory_space=pltpu.SEMAPHORE)
SPLIT_COPY = pltpu.CompilerParams(has_side_effects=pltpu.SideEffectType.DATAFLOW_SIDE_EFFECTING)


def _in_hbm(a):
    return pltpu.with_memory_space_constraint(a, pltpu.HBM)


def _window(ref, ax, chip):
    n = ref.shape[ax] // N_CHIPS
    return _sl(ref, ax, (2 * chip[0] + chip[1]) * n, n)


def _half(ref, ax, cc):
    ha = 3 - ax
    hs = ref.shape[ha] // 2
    return _sl(ref, ha, cc * hs, hs)


def gather_start(bufs, axes, tag):
    na = len(bufs)

    def body(*refs):
        ins = refs[:na]
        send_sems, recv_sems = refs[na], refs[na + 1]
        token = refs[2 * na + 2]
        x, y, c = _place()
        for a in range(na):
            mine = _half(_window(ins[a], axes[a], (x, y)), axes[a], c)
            for j, chip in enumerate(_other_chips(x, y)):
                _rcopy(mine, mine, send_sems.at[3 * a + j], recv_sems.at[3 * a + j], (*chip, c)).start()
        token[...] = jnp.zeros_like(token)

    dma = pltpu.SemaphoreType.DMA
    outs = pl.pallas_call(
        body, name="gather_start_" + tag,
        in_specs=[HBM] * na,
        out_specs=(SEM, SEM, *[HBM] * na, pl.BlockSpec(memory_space=pltpu.VMEM)),
        out_shape=(dma((3 * na,)), dma((3 * na,)), *[pltpu.HBM(b.shape, b.dtype) for b in bufs],
                   jax.ShapeDtypeStruct((8, 128), F32)),
        input_output_aliases={a: 2 + a for a in range(na)},
        compiler_params=SPLIT_COPY,
    )(*[_in_hbm(b) for b in bufs])
    return outs[0], outs[1], list(outs[2:2 + na]), outs[2 + na]


def gather_wait(send_sems, recv_sems, bufs, axes, after, tag):
    na = len(bufs)

    def body(*refs):
        ins = refs[:na]
        send_sems, recv_sems = refs[na], refs[na + 1]
        x, y, c = _place()
        for a in range(na):
            for j, chip in enumerate(_other_chips(x, y)):
                got = _half(_window(ins[a], axes[a], chip), axes[a], c)
                _rcopy(got, got, send_sems.at[3 * a + j], recv_sems.at[3 * a + j], (*chip, c)).wait_recv()
        for a in range(na):
            mine = _half(_window(ins[a], axes[a], (x, y)), axes[a], c)
            for j, chip in enumerate(_other_chips(x, y)):
                _rcopy(mine, mine, send_sems.at[3 * a + j], recv_sems.at[3 * a + j], (*chip, c)).wait_send()

    return pl.pallas_call(
        body, name="gather_wait_" + tag,
        in_specs=[HBM] * na + [SEM, SEM] + [ANY] * len(after),
        out_specs=[HBM] * na,
        out_shape=[pltpu.HBM(b.shape, b.dtype) for b in bufs],
        input_output_aliases={a: a for a in range(na)},
        compiler_params=SPLIT_COPY,
    )(*bufs, send_sems, recv_sems, *after)


def gather_forward(bufs, axes):
    na = len(bufs)

    def body(*refs):
        outs = refs[na:2 * na]
        send_sems, recv_sems = refs[2 * na:]
        x, y, c = _place()
        chips = _other_chips(x, y)
        passed = []
        for a in range(na):
            for j, chip in enumerate(chips):
                got = _half(_window(outs[a], axes[a], chip), axes[a], c)
                cp = _rcopy(got, got, send_sems.at[3 * a + j], recv_sems.at[3 * a + j], (x, y, 1 - c))
                cp.start()
                passed.append(cp)
        for a in range(na):
            for j, chip in enumerate(chips):
                got = _half(_window(outs[a], axes[a], chip), axes[a], 1 - c)
                _rcopy(got, got, send_sems.at[3 * a + j], recv_sems.at[3 * a + j], (x, y, 1 - c)).wait_recv()
        for cp in passed:
            cp.wait_send()

    dma = pltpu.SemaphoreType.DMA
    return pl.pallas_call(
        body, name="gather_forward",
        in_specs=[ANY] * na, out_specs=[ANY] * na,
        out_shape=[jax.ShapeDtypeStruct(b.shape, BF16) for b in bufs],
        input_output_aliases={a: a for a in range(na)},
        scratch_shapes=[dma((3 * na,)), dma((3 * na,))],
    )(*bufs)


def exchange_halves(grads, axes):
    na = len(grads)

    def hshape(g, ax):
        ha = 3 - ax
        return tuple(d // 2 if i == ha else d for i, d in enumerate(g.shape))

    def body(*refs):
        ins, outs = refs[:na], refs[na:2 * na]
        send_sems, recv_sems = refs[2 * na:]
        x, y, c = _place()
        cps = []
        for a in range(na):
            ha = 3 - axes[a]
            hs = ins[a].shape[ha] // 2
            cp = _rcopy(_sl(ins[a], ha, (1 - c) * hs, hs), outs[a], send_sems.at[a], recv_sems.at[a], (x, y, 1 - c))
            cp.start()
            cps.append(cp)
        for cp in cps:
            cp.wait_recv()
        for cp in cps:
            cp.wait_send()

    dma = pltpu.SemaphoreType.DMA
    return pl.pallas_call(
        body, name="exchange_halves",
        in_specs=[ANY] * na, out_specs=[ANY] * na,
        out_shape=[jax.ShapeDtypeStruct(hshape(g, ax), BF16) for g, ax in zip(grads, axes)],
        scratch_shapes=[dma((na,)), dma((na,))],
    )(*grads)


def scatter_start(halves, axes, tag):
    na = len(halves)

    def pshape(h, ax):
        return (N_CHIPS - 1,) + tuple(d // N_CHIPS if i == ax else d for i, d in enumerate(h.shape))

    def body(*refs):
        ins, lands = refs[:na], refs[na:2 * na]
        send_sems, recv_sems = refs[2 * na], refs[2 * na + 1]
        token = refs[4 * na + 2]
        x, y, c = _place()
        for a in range(na):
            for j, chip in enumerate(_other_chips(x, y)):
                _rcopy(_window(ins[a], axes[a], chip), lands[a].at[j],
                       send_sems.at[3 * a + j], recv_sems.at[3 * a + j], (*chip, c)).start()
        token[...] = jnp.zeros_like(token)

    dma = pltpu.SemaphoreType.DMA
    lands = [lax.empty(pshape(h, ax), BF16) for h, ax in zip(halves, axes)]
    outs = pl.pallas_call(
        body, name="scatter_start_" + tag,
        in_specs=[HBM] * (2 * na),
        out_specs=(SEM, SEM, *[HBM] * (2 * na), pl.BlockSpec(memory_space=pltpu.VMEM)),
        out_shape=(dma((3 * na,)), dma((3 * na,)), *[pltpu.HBM(b.shape, b.dtype) for b in halves + lands],
                   jax.ShapeDtypeStruct((8, 128), F32)),
        input_output_aliases={a: 2 + a for a in range(2 * na)},
        compiler_params=SPLIT_COPY,
    )(*[_in_hbm(b) for b in halves + lands])
    return outs[0], outs[1], list(outs[2:2 + na]), list(outs[2 + na:2 + 2 * na]), outs[2 + 2 * na]


def scatter_wait(send_sems, recv_sems, halves, lands, axes, after, tag):
    na = len(halves)

    def body(*refs):
        ins, lands = refs[:na], refs[na:2 * na]
        send_sems, recv_sems = refs[2 * na], refs[2 * na + 1]
        x, y, c = _place()
        for a in range(na):
            for j, chip in enumerate(_other_chips(x, y)):
                _rcopy(_window(ins[a], axes[a], chip), lands[a].at[j],
                       send_sems.at[3 * a + j], recv_sems.at[3 * a + j], (*chip, c)).wait_recv()
        for a in range(na):
            for j, chip in enumerate(_other_chips(x, y)):
                _rcopy(_window(ins[a], axes[a], chip), lands[a].at[j],
                       send_sems.at[3 * a + j], recv_sems.at[3 * a + j], (*chip, c)).wait_send()

    outs = pl.pallas_call(
        body, name="scatter_wait_" + tag,
        in_specs=[HBM] * (2 * na) + [SEM, SEM] + [ANY] * len(after),
        out_specs=[HBM] * (2 * na),
        out_shape=[pltpu.HBM(b.shape, b.dtype) for b in halves + lands],
        input_output_aliases={a: a for a in range(2 * na)},
        compiler_params=SPLIT_COPY,
    )(*halves, *lands, send_sems, recv_sems, *after)
    return list(outs[:na]), list(outs[na:])


def join_halves(gs, regions, axes):
    na = len(gs)

    def body(*refs):
        outs = refs[na:2 * na]
        send_sems, recv_sems = refs[2 * na:]
        x, y, c = _place()
        cps = []
        for a in range(na):
            ha = 3 - axes[a]
            hs = outs[a].shape[ha] // 2
            reg = outs[a].at[pl.ds(*regions[a])]
            mine = _sl(reg, ha, c * hs, hs)
            cp = _rcopy(mine, mine, send_sems.at[a], recv_sems.at[a], (x, y, 1 - c))
            cp.start()
            cps.append((cp, _sl(reg, ha, (1 - c) * hs, hs)))
        for a, (cp, theirs) in enumerate(cps):
            _rcopy(theirs, theirs, send_sems.at[a], recv_sems.at[a], (x, y, 1 - c)).wait_recv()
        for cp, _ in cps:
            cp.wait_send()

    dma = pltpu.SemaphoreType.DMA
    return pl.pallas_call(
        body, name="join_halves",
        in_specs=[ANY] * na, out_specs=[ANY] * na,
        out_shape=[jax.ShapeDtypeStruct(g.shape, F32) for g in gs],
        input_output_aliases={a: a for a in range(na)},
        scratch_shapes=[dma((na,)), dma((na,))],
    )(*gs)


def _tile2(r, c, itemsize, limit):
    bc = c
    while bc > 1536:
        bc //= 2
    assert c % bc == 0 and bc % 128 == 0
    br = r
    while br * bc * itemsize > limit and br % 32 == 0:
        br //= 2
    assert r % br == 0 and br % 16 == 0
    return br, bc


def add_my_half(g, theirs, ax, cc):
    ha = 3 - ax
    nl, r, c = theirs.shape
    br, bc = _tile2(r, c, 2, 2 << 20)
    nrb, ncb = r // br, c // bc

    def body(cc_ref, g_ref, t_ref, o_ref):
        o_ref[...] = (g_ref[...].astype(F32) + t_ref[...].astype(F32)).astype(BF16)

    def gmap(l, i, j, cc_ref):
        return (l, cc_ref[0] * nrb + i, j) if ha == 1 else (l, i, cc_ref[0] * ncb + j)

    blk = pl.BlockSpec((1, br, bc), lambda l, i, j, cc_ref: (l, i, j))
    return pl.pallas_call(
        body, name="add_my_half",
        grid_spec=pltpu.PrefetchScalarGridSpec(
            num_scalar_prefetch=1, grid=(nl, nrb, ncb),
            in_specs=[pl.BlockSpec((1, br, bc), gmap), blk], out_specs=blk),
        out_shape=_mid(theirs.shape, BF16),
        compiler_params=_cparams(32, 3),
    )(cc, *_pin(g, theirs))


def sum_chips(parts, pair, gstack, l0, ax, where):
    _, n, r, c = parts.shape
    ha = 3 - ax
    br, bc = _tile2(r, c, 2, 1 << 20)
    nrb, ncb = r // br, c // bc

    def body(w_ref, p_ref, own_ref, g_ref, o_ref):
        acc = own_ref[...].astype(F32)
        for q in range(N_CHIPS - 1):
            acc = acc + p_ref[q].astype(F32)
        o_ref[...] = acc

    def own_map(l, i, j, w_ref):
        return (l, w_ref[0] * nrb + i, j) if ax == 1 else (l, i, w_ref[0] * ncb + j)

    def out_map(l, i, j, w_ref):
        return (l0 + l, w_ref[1] * nrb + i, j) if ha == 1 else (l0 + l, i, w_ref[1] * ncb + j)

    return pl.pallas_call(
        body, name="sum_chips",
        grid_spec=pltpu.PrefetchScalarGridSpec(
            num_scalar_prefetch=1, grid=(n, nrb, ncb),
            in_specs=[pl.BlockSpec((N_CHIPS - 1, 1, br, bc), lambda l, i, j, w_ref: (0, l, i, j)),
                      pl.BlockSpec((1, br, bc), own_map), ANY],
            out_specs=pl.BlockSpec((1, br, bc), out_map)),
        out_shape=jax.ShapeDtypeStruct(gstack.shape, F32),
        input_output_aliases={3: 0},
        compiler_params=_cparams(32, 3),
    )(where, *_pin(parts, pair, gstack))


BIG = ("w_ffn_up", "w_ffn_down", "attn_w_qkv", "attn_w_o", "conv_w_in", "conv_w_out")
BIG_AXIS = {"w_ffn_up": 2, "w_ffn_down": 1, "attn_w_qkv": 2, "attn_w_o": 1, "conv_w_in": 2, "conv_w_out": 1}
WEIGHTS = ("norm_gain", "w_ada", "b_ada", "w_ffn_up", "w_ffn_down", "attn_w_qkv", "attn_b_qkv", "attn_q_gain",
           "attn_k_gain", "attn_sinks", "attn_w_o", "attn_b_o", "conv_w_in", "conv_w", "conv_w_out")
N_SMALL_ROWS = 112


def _stack3(a):
    return a.reshape((-1,) + a.shape[-2:])


def _head_matrices():
    lane = jnp.arange(QK_DIM)
    head = lane // HEAD_DIM
    col = jnp.arange(128)
    g1 = jnp.where(head[:, None] == col[None, :], 1.0 / HEAD_DIM, 0.0).astype(BF16)
    g2 = jnp.where(col[:, None] == head[None, :], 1.0, 0.0).astype(BF16)
    fold = lane % HEAD_DIM + jnp.where(head >= N_HEADS, HEAD_DIM, 0)
    gsel = jnp.where(fold[:, None] == col[None, :], 1.0, 0.0).astype(BF16)
    return g1, g2, gsel


def _pad_cols(a, n):
    return jnp.pad(a, ((0, 0), (0, n - a.shape[1])))


def kernel(x, c, positions, norm_gain, w_ada, b_ada, w_ffn_up, w_ffn_down, attn_w_qkv, attn_b_qkv, attn_q_gain, attn_k_gain, attn_sinks, attn_w_o, attn_b_o, conv_w_in, conv_w, conv_w_out, loss_target, m_norm_gain, m_w_ada, m_b_ada, m_w_ffn_up, m_w_ffn_down, m_attn_w_qkv, m_attn_b_qkv, m_attn_q_gain, m_attn_k_gain, m_attn_sinks, m_attn_w_o, m_attn_b_o, m_conv_w_in, m_conv_w, m_conv_w_out, v_norm_gain, v_w_ada, v_b_ada, v_w_ffn_up, v_w_ffn_down, v_attn_w_qkv, v_attn_b_qkv, v_attn_q_gain, v_attn_k_gain, v_attn_sinks, v_attn_w_o, v_attn_b_o, v_conv_w_in, v_conv_w, v_conv_w_out):
    w = dict(norm_gain=norm_gain, w_ada=w_ada, b_ada=b_ada, w_ffn_up=w_ffn_up, w_ffn_down=w_ffn_down,
             attn_w_qkv=attn_w_qkv, attn_b_qkv=attn_b_qkv, attn_q_gain=attn_q_gain, attn_k_gain=attn_k_gain,
             attn_sinks=attn_sinks, attn_w_o=attn_w_o, attn_b_o=attn_b_o, conv_w_in=conv_w_in, conv_w=conv_w,
             conv_w_out=conv_w_out)
    mom = dict(norm_gain=m_norm_gain, w_ada=m_w_ada, b_ada=m_b_ada, w_ffn_up=m_w_ffn_up, w_ffn_down=m_w_ffn_down,
               attn_w_qkv=m_attn_w_qkv, attn_b_qkv=m_attn_b_qkv, attn_q_gain=m_attn_q_gain,
               attn_k_gain=m_attn_k_gain, attn_sinks=m_attn_sinks, attn_w_o=m_attn_w_o, attn_b_o=m_attn_b_o,
               conv_w_in=m_conv_w_in, conv_w=m_conv_w, conv_w_out=m_conv_w_out)
    var = dict(norm_gain=v_norm_gain, w_ada=v_w_ada, b_ada=v_b_ada, w_ffn_up=v_w_ffn_up, w_ffn_down=v_w_ffn_down,
               attn_w_qkv=v_attn_w_qkv, attn_b_qkv=v_attn_b_qkv, attn_q_gain=v_attn_q_gain,
               attn_k_gain=v_attn_k_gain, attn_sinks=v_attn_sinks, attn_w_o=v_attn_w_o, attn_b_o=v_attn_b_o,
               conv_w_in=v_conv_w_in, conv_w=v_conv_w, conv_w_out=v_conv_w_out)

    xi, yi, ci = _place()
    chip = 2 * xi + yi
    dev = 4 * xi + 2 * yi + ci
    nex, s_len, _ = x.shape
    t = nex * s_len
    n_attn, n_conv = attn_w_qkv.shape[0], conv_w_in.shape[0]
    axes = [BIG_AXIS[n] for n in BIG]

    c_all = allgather_small(jnp.pad(c, ((0, 8 - nex), (0, 0))), True)[:, :nex].reshape(8 * nex, D)
    ada_cols = w_ada.shape[2]
    modp = ada_fwd(c_all, w_ada)
    modg = allgather_small(modp.reshape(DEPTH * 8 * nex, ada_cols), False)
    modg = lax.dynamic_slice_in_dim(modg.reshape(N_CHIPS, DEPTH, 8 * nex, ada_cols), dev * nex, nex, axis=2)
    modg = modg.transpose(1, 2, 0, 3).reshape(DEPTH, nex, 9 * D)
    mod = add_bias(modg, b_ada.reshape(DEPTH, 1, 9 * D)).reshape(DEPTH, nex, 9, D)

    small = jnp.concatenate([norm_gain.reshape(DEPTH * 3, -1), conv_w.reshape(n_conv * 3, -1)], axis=0)
    small = jnp.pad(small, ((0, -small.shape[0] % 8), (0, 0)))
    small = allgather_small(small, False).transpose(1, 0, 2).reshape(small.shape[0], D)
    gain_full = small[:DEPTH * 3].reshape(DEPTH, 3, D)
    convw_full = small[DEPTH * 3:DEPTH * 3 + n_conv * 3].reshape(n_conv, 3, D)

    chip_arr = chip.reshape(1).astype(jnp.int32)
    where = jnp.stack([chip, ci]).astype(jnp.int32)
    stacks = [_stack3(w[n]) for n in BIG]

    def mixer(i):
        return (2, 3) if i % 2 == 0 else (4, 5)

    groups = [[(0, 0, 1), (1, 0, 1)], [(mixer(0)[0], 0, 1), (mixer(0)[1], 0, 1), (0, 1, 1), (1, 1, 1)]]
    groups += [[(0, 2 * i, 2), (1, 2 * i, 2), (mixer(i)[0], i // 2, 1), (mixer(i)[1], i // 2, 1)]
               for i in range(1, DEPTH)]
    gaxes = [[axes[b] for b, _, _ in grp] for grp in groups]
    where_is = {(b, l0 + k): (g, a, k) for g, grp in enumerate(groups) for a, (b, l0, n) in enumerate(grp)
                for k in range(n)}
    in_flight, token = [], (mod, small)
    for g, grp in enumerate(groups):
        bufs = [cast_into_window(stacks[b], l0, n, axes[b], chip_arr, token) for b, l0, n in grp]
        ssem, rsem, bufs, tok = gather_start(bufs, gaxes[g], f"g{g}")
        in_flight.append((ssem, rsem, bufs))
        token = (tok,)

    invf = ROPE_THETA ** (-jnp.arange(0, HEAD_DIM, 2, dtype=F32) / HEAD_DIM)
    cos, sin = rope_tables(positions.reshape(t, 1), jnp.tile(invf, 4).reshape(1, 128))
    g1, g2, gsel = _head_matrices()
    gqk = [jnp.concatenate([jnp.tile(attn_q_gain[j], N_HEADS), jnp.tile(attn_k_gain[j], N_KV)]).reshape(1, QK_DIM)
           for j in range(n_attn)]
    zero_bias = jnp.zeros((1, D), F32)

    xs = x.reshape(t, D)
    saved, ready = [], {}

    def weight(b, l, after):
        g, a, k = where_is[(b, l)]
        if g not in ready:
            ssem, rsem, bufs = in_flight[g]
            ready[g] = gather_forward(gather_wait(ssem, rsem, bufs, gaxes[g], after, f"g{g}"), gaxes[g])
        return ready[g][a], k

    for i in range(DEPTH):
        j = i // 2
        gn, md = gain_full[i], mod[i]
        x0 = xs
        wup, k = weight(0, 2 * i, token if i == 0 else (xs,))
        wdn, _ = weight(1, 2 * i, ())
        xs, u1, f1 = ffn_fwd(x0, gn[0:1], md, wup, wdn, k, 0)
        x1 = xs
        wmi, k = weight(mixer(i)[0], j, (xs,))
        wmo, _ = weight(mixer(i)[1], j, ())
        if i % 2 == 0:
            raw, qr, kr = qkv_fwd(x1, gn[1:2], md, wmi, k, attn_b_qkv[j:j + 1], gqk[j], cos, sin, g1, g2)
            o, lse = attn_fwd(qr, kr, raw, attn_sinks[j:j + 1])
            xs, ymix = proj_res(x1, o, wmo, k, attn_b_o[j:j + 1], md)
            mix = (raw, qr, kr, o, lse)
        else:
            z = lin_fwd(x1, gn[1:2], md, wmi, k, 1)
            p = conv_fwd(z, convw_full[j])
            xs, ymix = proj_res(x1, p, wmo, k, zero_bias, md)
            mix = (z, p)
        x2 = xs
        wup, k = weight(0, 2 * i + 1, (xs,))
        wdn, _ = weight(1, 2 * i + 1, ())
        xs, u3, f3 = ffn_fwd(x2, gn[2:3], md, wup, wdn, k, 2)
        saved.append((x0, u1, f1, x1, mix, ymix, x2, u3, f3))
    dy, lpart = loss_grad(xs, loss_target.reshape(t, D))

    cc = ci.reshape(1).astype(jnp.int32)
    gshard = [lax.empty(s.shape, F32) for s in stacks]
    upd = [[lax.empty(s.shape, F32) for _ in range(3)] for s in stacks]
    mstacks = [_stack3(mom[n]) for n in BIG]
    vstacks = [_stack3(var[n]) for n in BIG]

    def finish(g, copies, after):
        ssem, rsem, pair, lands = copies
        pair, lands = scatter_wait(ssem, rsem, pair, lands, gaxes[g], after, f"g{g}")
        grp = groups[g]
        for a, (b, l0, n) in enumerate(grp):
            gshard[b] = sum_chips(lands[a], pair[a], gshard[b], l0, axes[b], where)
        joined = join_halves([gshard[b] for b, _, _ in grp], [(l0, n) for _, l0, n in grp], gaxes[g])
        for (b, l0, n), gj in zip(grp, joined):
            gshard[b] = gj
            upd[b] = adamw_layers(stacks[b], gj, mstacks[b], vstacks[b], upd[b], l0, n)

    ggrad = {g: [lax.empty(buf.shape, BF16) for buf in ready[g]] for g in range(len(groups))}
    missing = {g: sum(n for _, _, n in grp) for g, grp in enumerate(groups)}
    state = dict(flight=None, token=())

    def put(b, l, lhs, rhs, bm, bn):
        g, a, k = where_is[(b, l)]
        ggrad[g][a] = wgrad(ggrad[g][a], k, lhs, rhs, bm, bn)
        missing[g] -= 1
        if missing[g] == 0:
            theirs = exchange_halves(ggrad[g], gaxes[g])
            pair = [add_my_half(gr, th, ax, cc) for gr, th, ax in zip(ggrad[g], theirs, gaxes[g])]
            ssem, rsem, pair, lands, tok = scatter_start(pair, gaxes[g], f"g{g}")
            if state["flight"] is not None:
                finish(*state["flight"], (tok,))
            state.update(flight=(g, (ssem, rsem, pair, lands)), token=(tok,))

    dmod = [None] * DEPTH
    dgain = [None] * DEPTH
    db_qkv, dqk_gain, dsinks, db_o, dconv_w = ([None] * n_attn, [None] * n_attn, [None] * n_attn,
                                                [None] * n_attn, [None] * n_conv)
    for i in reversed(range(DEPTH)):
        j = i // 2
        gn, md = gain_full[i], mod[i]
        x0, u1, f1, x1, mix, ymix, x2, u3, f3 = saved[i]
        (wup, k), (wdn, _) = weight(0, 2 * i + 1, ()), weight(1, 2 * i + 1, ())
        du, a, df, sg3 = ffn_bwd_act(dy, u3, f3, md, wdn, k, 2, after=state["token"])
        dy, h, se3, sa3 = lin_bwd(dy, du, wup, k, x2, gn[2:3], md, 2, False)
        put(0, 2 * i + 1, h, du, D // 2, D_FF)
        put(1, 2 * i + 1, a, df, D_FF // 2, D)
        (wmi, k), (wmo, _) = weight(mixer(i)[0], j, ()), weight(mixer(i)[1], j, ())
        if i % 2 == 0:
            raw, qr, kr, o, lse = mix
            dyy, do, sp, sb = proj_res_bwd(dy, ymix, wmo, k, md)
            put(mixer(i)[1], j, o, dyy, D, D)
            dq, dk, dv, dsinks[j] = attn_bwd(qr, kr, raw, attn_sinks[j:j + 1], o, do, lse)
            dz, dqk_gain[j] = qkv_bwd_pre(dq, dk, dv, raw, gqk[j], cos, sin, g1, g2, gsel)
            dy, h, se2, sa2, db_qkv[j] = lin_bwd(dy, dz, wmi, k, x1, gn[1:2], md, 1, True)
            put(mixer(i)[0], j, h, dz, D, QKV_DIM)
            db_o[j] = sb
        else:
            z, p = mix
            dyy, dp, sp, _ = proj_res_bwd(dy, ymix, wmo, k, md)
            put(mixer(i)[1], j, p, dyy, D, D)
            dz, dconv_w[j] = conv_bwd(z, convw_full[j], dp)
            dy, h, se2, sa2 = lin_bwd(dy, dz, wmi, k, x1, gn[1:2], md, 1, False)
            put(mixer(i)[0], j, h, dz, D, 1536)
        (wup, k), (wdn, _) = weight(0, 2 * i, ()), weight(1, 2 * i, ())
        du, a, df, sg1 = ffn_bwd_act(dy, u1, f1, md, wdn, k, 0, after=state["token"])
        dy, h, se1, sa1 = lin_bwd(dy, du, wup, k, x0, gn[0:1], md, 0, False)
        put(0, 2 * i, h, du, D // 2, D_FF)
        put(1, 2 * i, a, df, D_FF // 2, D)
        dmod[i] = jnp.stack([se1[:, 0], se1[:, 1], sg1[:, 0], se2[:, 0], se2[:, 1], sp[:, 0],
                             se3[:, 0], se3[:, 1], sg3[:, 0]], axis=1)
        dgain[i] = jnp.stack([sa1[0], sa2[0], sa3[0]], axis=0)
    grad_x = dy.reshape(x.shape)

    dmod_ex = jnp.stack(dmod, axis=1).reshape(nex, DEPTH * 9, D)
    dmod_ex = jnp.pad(dmod_ex, ((0, 0), (0, N_DMOD_ROWS - DEPTH * 9), (0, 0))).reshape(nex * N_DMOD_ROWS, D)
    misc = jnp.concatenate([dqk_gain[jj][0] for jj in range(n_attn)]
                           + [jnp.pad(dsinks[jj][0], (0, 128 - N_HEADS)) for jj in range(n_attn)]
                           + [lpart[0]])
    rows = [dmod_ex,
            jnp.concatenate(dgain, axis=0), jnp.zeros((4, D), F32),
            jnp.concatenate([_pad_cols(db_qkv[jj][0:1], 2 * D).reshape(2, D) for jj in range(n_attn)], axis=0),
            jnp.concatenate([db_o[jj][0:1] for jj in range(n_attn)], axis=0),
            jnp.concatenate([dconv_w[jj][0:3] for jj in range(n_conv)], axis=0),
            jnp.pad(misc, (0, D - misc.shape[0])).reshape(1, D)]
    packed = jnp.concatenate(rows, axis=0)
    packed = jnp.pad(packed, ((0, N_SMALL_ROWS - packed.shape[0]), (0, 0)))
    p_all = allgather_small(packed, True)
    red, exsum = reduce_small(p_all)

    r0 = nex * N_DMOD_ROWS
    grads = {}
    grads["b_ada"] = exsum[:DEPTH * 9].reshape(DEPTH, 9 * D)
    grads["norm_gain"] = lax.dynamic_slice_in_dim(red[r0:r0 + 12].reshape(DEPTH, 3, D), chip * (D // N_CHIPS),
                                                  D // N_CHIPS, axis=2)
    r1 = r0 + 16
    grads["attn_b_qkv"] = red[r1:r1 + 2 * n_attn].reshape(n_attn, 2 * D)[:, :QKV_DIM]
    r2 = r1 + 2 * n_attn
    grads["attn_b_o"] = red[r2:r2 + n_attn]
    r3 = r2 + n_attn
    grads["conv_w"] = lax.dynamic_slice_in_dim(red[r3:r3 + 3 * n_conv].reshape(n_conv, 3, D), chip * (D // N_CHIPS),
                                               D // N_CHIPS, axis=2)
    mrow = red[r3 + 3 * n_conv]
    grads["attn_q_gain"] = jnp.stack([mrow[128 * jj:128 * jj + HEAD_DIM] for jj in range(n_attn)])
    grads["attn_k_gain"] = jnp.stack([mrow[128 * jj + HEAD_DIM:128 * jj + 128] for jj in range(n_attn)])
    grads["attn_sinks"] = jnp.stack([mrow[128 * (n_attn + jj):128 * (n_attn + jj) + N_HEADS] for jj in range(n_attn)])
    loss = mrow[128 * 2 * n_attn]

    dm_all = p_all[:, :r0].reshape(8, nex, N_DMOD_ROWS, D)[:, :, :DEPTH * 9].reshape(8 * nex, DEPTH, 9 * D)
    dm_mine = lax.dynamic_slice_in_dim(dm_all.transpose(1, 0, 2), chip * ada_cols, ada_cols, axis=2)
    g_ada, d_ada, nm_ada, nv_ada = ada_bwd_adam(c_all, dm_mine, w_ada, m_w_ada, v_w_ada)

    delta, new_m, new_v = {}, {}, {}
    for n in WEIGHTS:
        if n == "w_ada":
            grads[n], delta[n], new_m[n], new_v[n] = g_ada, d_ada, nm_ada, nv_ada
        elif n not in BIG:
            delta[n], new_m[n], new_v[n] = adamw(w[n], grads[n], mom[n], var[n])
    finish(*state["flight"], (delta["conv_w"], d_ada))
    for b, n in enumerate(BIG):
        grads[n] = gshard[b].reshape(w[n].shape)
        delta[n], new_m[n], new_v[n] = (u.reshape(w[n].shape) for u in upd[b])

    return (loss, grad_x, *[grads[n] for n in WEIGHTS], *[delta[n] for n in WEIGHTS],
            *[new_m[n] for n in WEIGHTS], *[new_v[n] for n in WEIGHTS])
```

```python
import functools

import jax
import jax.numpy as jnp
from jax import lax
from jax.experimental import pallas as pl
from jax.experimental.pallas import tpu as pltpu

F32 = jnp.float32
BF16 = jnp.bfloat16

D = 1024
D_FF = 2816
N_HEADS = 16
N_KV = 4
HEAD_DIM = 64
GROUP = N_HEADS // N_KV
QK_DIM = (N_HEADS + N_KV) * HEAD_DIM
QKV_DIM = QK_DIM + N_KV * HEAD_DIM
BLOCK = 128
ROPE_THETA = 10000.0
EPS = 1e-6
DEPTH = 4
N_CHIPS = 4

ADAM_LR = 0.001
ADAM_B1 = 0.9
ADAM_B2 = 0.999
ADAM_EPS = 1e-08
ADAM_WD = 0.01
ADAM_STEP = 10

V7X_VMEM_BYTES = 64 * 1024 * 1024
V7X_MXU_DIM = 256
FF_CHUNKS = ((0, 1536), (1536, D_FF))
assert all((hi - lo) % V7X_MXU_DIM == 0 for lo, hi in FF_CHUNKS)
MESH = pl.DeviceIdType.MESH
ANY = pl.BlockSpec(memory_space=pl.ANY)


def _cparams(vmem_mb, n_grid):
    assert vmem_mb * 1024 * 1024 <= V7X_VMEM_BYTES
    return pltpu.CompilerParams(vmem_limit_bytes=vmem_mb * 1024 * 1024,
                                dimension_semantics=("arbitrary",) * n_grid)


def _resident(shape):
    nd = len(shape)
    return pl.BlockSpec(shape, lambda *_: (0,) * nd, pipeline_mode=pl.Buffered(1))


def _layer(w, l):
    return pl.BlockSpec((None,) + w.shape[1:], lambda *_: (l, 0, 0), pipeline_mode=pl.Buffered(1))


PIN_BYTES = 1 << 20


def _pin(*args):
    return [pltpu.with_memory_space_constraint(a, pltpu.HBM) if a.size * a.dtype.itemsize >= PIN_BYTES else a
            for a in args]


def _mid(shape, dtype):
    n = 1
    for d in shape:
        n *= d
    if n * jnp.dtype(dtype).itemsize >= PIN_BYTES:
        return pltpu.HBM(tuple(shape), dtype)
    return jax.ShapeDtypeStruct(tuple(shape), dtype)


def _dot(a, b):
    return jnp.dot(a, b, preferred_element_type=F32)


def _dot_nt(a, b):
    return lax.dot_general(a, b, (((1,), (1,)), ((), ())), preferred_element_type=F32)


def _dot_tn(a, b):
    return lax.dot_general(a, b, (((0,), (0,)), ((), ())), preferred_element_type=F32)


def _dot_hilo(a, g):
    hi = a.astype(BF16)
    lo = (a - hi.astype(F32)).astype(BF16)
    return _dot(hi, g) + _dot(lo, g)


def _sigmoid(x):
    return 0.5 * jnp.tanh(0.5 * x) + 0.5


def _colsum(a):
    return jnp.sum(a, axis=0, keepdims=True)


def _norm_mod(x, gain, sc, sh):
    r = lax.rsqrt(jnp.mean(x * x, axis=-1, keepdims=True) + EPS)
    n = x * r * gain
    return r, n, n * (1.0 + sc) + sh


def _mod_rows(mod_ref, s):
    return (mod_ref[0, 3 * s:3 * s + 1, :], mod_ref[0, 3 * s + 1:3 * s + 2, :], mod_ref[0, 3 * s + 2:3 * s + 3, :])


def ffn_fwd(x, gain, mod, wup, wdn, l, s, tm=512, after=()):
    t = x.shape[0]
    tpe = t // tm // 2

    def body(x_ref, gain_ref, mod_ref, wup_ref, wdn_ref, *rest):
        xo_ref, u_ref, f_ref = rest[len(after):]
        xv = x_ref[...]
        sh, sc, g = _mod_rows(mod_ref, s)
        _, _, h = _norm_mod(xv, gain_ref[...], sc, sh)
        hb = h.astype(BF16)
        acc = jnp.zeros((tm, D), F32)
        for lo, hi in FF_CHUNKS:
            gate = _dot(hb, wup_ref[:, lo:hi])
            up = _dot(hb, wup_ref[:, D_FF + lo:D_FF + hi])
            u_ref[:, lo:hi] = gate.astype(BF16)
            u_ref[:, D_FF + lo:D_FF + hi] = up.astype(BF16)
            a = (gate * _sigmoid(gate) * up).astype(BF16)
            acc = acc + _dot(a, wdn_ref[lo:hi, :])
        f_ref[...] = acc.astype(BF16)
        xo_ref[...] = xv + 0.5 * g * acc

    return pl.pallas_call(
        body, name="ffn_fwd", grid=(t // tm,),
        in_specs=[pl.BlockSpec((tm, D), lambda i: (i, 0)),
                  pl.BlockSpec((1, D), lambda i: (0, 0)),
                  pl.BlockSpec((1, 9, D), lambda i: (i // tpe, 0, 0)),
                  _layer(wup, l), _layer(wdn, l)] + [ANY] * len(after),
        out_specs=[pl.BlockSpec((tm, D), lambda i: (i, 0)),
                   pl.BlockSpec((tm, 2 * D_FF), lambda i: (i, 0)),
                   pl.BlockSpec((tm, D), lambda i: (i, 0))],
        out_shape=[_mid((t, D), F32), _mid((t, 2 * D_FF), BF16), _mid((t, D), BF16)],
        compiler_params=_cparams(60, 1),
    )(*_pin(x, gain, mod, wup, wdn, *after))


def ffn_bwd(dy, u, f, x, gain, mod, wup, wdn, l, s, tm=256, after=()):
    t = dy.shape[0]
    tpe = t // tm // 2

    def body(dy_ref, u_ref, f_ref, x_ref, gain_ref, mod_ref, wup_ref, wdn_ref, *rest):
        dx_ref, du_ref, a_ref, df_ref, h_ref, se_ref, sa_ref = rest[len(after):]
        i = pl.program_id(0)
        dyv, xv, gain_v = dy_ref[...], x_ref[...], gain_ref[...]
        sh, sc, g = _mod_rows(mod_ref, s)
        r, nrm, h = _norm_mod(xv, gain_v, sc, sh)
        h_ref[...] = h.astype(BF16)
        dfb = (0.5 * g * dyv).astype(BF16)
        df_ref[...] = dfb
        dh = jnp.zeros((tm, D), F32)
        for lo, hi in FF_CHUNKS:
            da = _dot_nt(dfb, wdn_ref[lo:hi, :])
            gate = u_ref[:, lo:hi].astype(F32)
            up = u_ref[:, D_FF + lo:D_FF + hi].astype(F32)
            sg = _sigmoid(gate)
            silu = gate * sg
            a_ref[:, lo:hi] = (silu * up).astype(BF16)
            dgate = (da * up * (sg + silu * (1.0 - sg))).astype(BF16)
            dup = (da * silu).astype(BF16)
            du_ref[:, lo:hi] = dgate
            du_ref[:, D_FF + lo:D_FF + hi] = dup
            dh = dh + _dot_nt(dgate, wup_ref[:, lo:hi]) + _dot_nt(dup, wup_ref[:, D_FF + lo:D_FF + hi])
        dn = dh * (1.0 + sc)
        dxr = dn * gain_v
        m = jnp.mean(dxr * xv, axis=-1, keepdims=True)
        dx_ref[...] = dyv + r * dxr - xv * (r * r * r) * m

        @pl.when(i % tpe == 0)
        def _():
            se_ref[...] = jnp.zeros_like(se_ref)

        @pl.when(i == 0)
        def _():
            sa_ref[...] = jnp.zeros_like(sa_ref)

        se_ref[0, 0:1, :] += _colsum(dh)
        se_ref[0, 1:2, :] += _colsum(dh * nrm)
        se_ref[0, 2:3, :] += _colsum(0.5 * dyv * f_ref[...].astype(F32))
        sa_ref[0:1, :] += _colsum(dn * xv * r)

    row = lambda w_: pl.BlockSpec((tm, w_), lambda i: (i, 0))
    return pl.pallas_call(
        body, name="ffn_bwd", grid=(t // tm,),
        in_specs=[row(D), row(2 * D_FF), row(D), row(D),
                  pl.BlockSpec((1, D), lambda i: (0, 0)),
                  pl.BlockSpec((1, 9, D), lambda i: (i // tpe, 0, 0)),
                  _layer(wup, l), _layer(wdn, l)] + [ANY] * len(after),
        out_specs=[row(D), row(2 * D_FF), row(D_FF), row(D), row(D),
                   pl.BlockSpec((1, 8, D), lambda i: (i // tpe, 0, 0)), pl.BlockSpec((8, D), lambda i: (0, 0))],
        out_shape=[_mid((t, D), F32), _mid((t, 2 * D_FF), BF16), _mid((t, D_FF), BF16), _mid((t, D), BF16),
                   _mid((t, D), BF16), _mid((2, 8, D), F32), _mid((8, D), F32)],
        compiler_params=_cparams(60, 1),
    )(*_pin(dy, u, f, x, gain, mod, wup, wdn, *after))


def lin_bwd(dy, dz, w, l, x, gain, mod, s, want_db, tm=512):
    t = dy.shape[0]
    n = w.shape[2]
    tpe = t // tm // 2

    def body(dy_ref, dz_ref, w_ref, x_ref, gain_ref, mod_ref, dx_ref, h_ref, se_ref, sa_ref, *db_ref):
        i = pl.program_id(0)
        xv = x_ref[...]
        gain_v = gain_ref[...]
        sh, sc, _ = _mod_rows(mod_ref, s)
        r, nrm, h = _norm_mod(xv, gain_v, sc, sh)
        h_ref[...] = h.astype(BF16)
        dh = _dot_nt(dz_ref[...], w_ref[...])
        dn = dh * (1.0 + sc)
        dxr = dn * gain_v
        m = jnp.mean(dxr * xv, axis=-1, keepdims=True)
        dx_ref[...] = dy_ref[...] + r * dxr - xv * (r * r * r) * m

        @pl.when(i % tpe == 0)
        def _():
            se_ref[...] = jnp.zeros_like(se_ref)

        @pl.when(i == 0)
        def _():
            sa_ref[...] = jnp.zeros_like(sa_ref)
            if want_db:
                db_ref[0][...] = jnp.zeros_like(db_ref[0])

        se_ref[0, 0:1, :] += _colsum(dh)
        se_ref[0, 1:2, :] += _colsum(dh * nrm)
        sa_ref[0:1, :] += _colsum(dn * xv * r)
        if want_db:
            db_ref[0][0:1, :] += _colsum(dz_ref[...].astype(F32))

    out_specs = [pl.BlockSpec((tm, D), lambda i: (i, 0)), pl.BlockSpec((tm, D), lambda i: (i, 0)),
                 pl.BlockSpec((1, 8, D), lambda i: (i // tpe, 0, 0)), pl.BlockSpec((8, D), lambda i: (0, 0))]
    out_shape = [_mid((t, D), F32), _mid((t, D), BF16),
                 jax.ShapeDtypeStruct((2, 8, D), F32), jax.ShapeDtypeStruct((8, D), F32)]
    if want_db:
        out_specs.append(pl.BlockSpec((8, n), lambda i: (0, 0)))
        out_shape.append(jax.ShapeDtypeStruct((8, n), F32))
    return pl.pallas_call(
        body, name="lin_bwd", grid=(t // tm,),
        in_specs=[pl.BlockSpec((tm, D), lambda i: (i, 0)),
                  pl.BlockSpec((tm, n), lambda i: (i, 0)),
                  _layer(w, l),
                  pl.BlockSpec((tm, D), lambda i: (i, 0)),
                  pl.BlockSpec((1, D), lambda i: (0, 0)),
                  pl.BlockSpec((1, 9, D), lambda i: (i // tpe, 0, 0))],
        out_specs=out_specs, out_shape=out_shape,
        compiler_params=_cparams(56, 1),
    )(*_pin(dy, dz, w, x, gain, mod))


def wgrad(gstack, l, a, b, bm, bn, bt=1024):
    t, m = a.shape
    n = b.shape[1]
    nt = t // bt

    def body(g_ref, a_ref, b_ref, o_ref, acc_ref):
        k = pl.program_id(2)

        @pl.when(k == 0)
        def _():
            acc_ref[...] = jnp.zeros_like(acc_ref)

        acc_ref[...] += _dot_tn(a_ref[...], b_ref[...])

        @pl.when(k == nt - 1)
        def _():
            o_ref[...] = acc_ref[...].astype(BF16)

    return pl.pallas_call(
        body, name="wgrad", grid=(m // bm, n // bn, nt),
        in_specs=[ANY, pl.BlockSpec((bt, bm), lambda i, j, k: (k, i)),
                  pl.BlockSpec((bt, bn), lambda i, j, k: (k, j))],
        out_specs=pl.BlockSpec((None, bm, bn), lambda i, j, k: (l, i, j)),
        out_shape=_mid(gstack.shape, BF16),
        input_output_aliases={0: 0},
        scratch_shapes=[pltpu.VMEM((bm, bn), F32)],
        compiler_params=_cparams(56, 3),
    )(*_pin(gstack, a, b))


def proj_res(x, o, w, l, b, mod, tm=512):
    t = x.shape[0]
    tpe = t // tm // 2

    def body(x_ref, o_ref, w_ref, b_ref, mod_ref, xo_ref, y_ref):
        _, _, g = _mod_rows(mod_ref, 1)
        y = _dot(o_ref[...], w_ref[...]) + b_ref[...]
        y_ref[...] = y.astype(BF16)
        xo_ref[...] = x_ref[...] + g * y

    return pl.pallas_call(
        body, name="proj_res", grid=(t // tm,),
        in_specs=[pl.BlockSpec((tm, D), lambda i: (i, 0)), pl.BlockSpec((tm, D), lambda i: (i, 0)),
                  _layer(w, l), pl.BlockSpec((1, D), lambda i: (0, 0)),
                  pl.BlockSpec((1, 9, D), lambda i: (i // tpe, 0, 0))],
        out_specs=[pl.BlockSpec((tm, D), lambda i: (i, 0)), pl.BlockSpec((tm, D), lambda i: (i, 0))],
        out_shape=[_mid((t, D), F32), _mid((t, D), BF16)],
        compiler_params=_cparams(32, 1),
    )(*_pin(x, o, w, b, mod))


def proj_res_bwd(dy, y, w, l, mod, tm=512):
    t = dy.shape[0]
    tpe = t // tm // 2

    def body(dy_ref, y_ref, w_ref, mod_ref, dyy_ref, do_ref, se_ref, sa_ref):
        i = pl.program_id(0)
        _, _, g = _mod_rows(mod_ref, 1)
        dyv = dy_ref[...]
        dyy = g * dyv
        dyb = dyy.astype(BF16)
        dyy_ref[...] = dyb
        do_ref[...] = _dot_nt(dyb, w_ref[...]).astype(BF16)

        @pl.when(i % tpe == 0)
        def _():
            se_ref[...] = jnp.zeros_like(se_ref)

        @pl.when(i == 0)
        def _():
            sa_ref[...] = jnp.zeros_like(sa_ref)

        se_ref[0, 0:1, :] += _colsum(dyv * y_ref[...].astype(F32))
        sa_ref[0:1, :] += _colsum(dyy)

    return pl.pallas_call(
        body, name="proj_res_bwd", grid=(t // tm,),
        in_specs=[pl.BlockSpec((tm, D), lambda i: (i, 0)), pl.BlockSpec((tm, D), lambda i: (i, 0)),
                  _layer(w, l), pl.BlockSpec((1, 9, D), lambda i: (i // tpe, 0, 0))],
        out_specs=[pl.BlockSpec((tm, D), lambda i: (i, 0)), pl.BlockSpec((tm, D), lambda i: (i, 0)),
                   pl.BlockSpec((1, 8, D), lambda i: (i // tpe, 0, 0)), pl.BlockSpec((8, D), lambda i: (0, 0))],
        out_shape=[_mid((t, D), BF16), _mid((t, D), BF16), _mid((2, 8, D), F32), _mid((8, D), F32)],
        compiler_params=_cparams(32, 1),
    )(*_pin(dy, y, w, mod))


def lin_fwd(x, gain, mod, w, l, s, tm=512):
    t = x.shape[0]
    n = w.shape[2]
    tpe = t // tm // 2

    def body(x_ref, gain_ref, mod_ref, w_ref, z_ref):
        sh, sc, _ = _mod_rows(mod_ref, s)
        _, _, h = _norm_mod(x_ref[...], gain_ref[...], sc, sh)
        z_ref[...] = _dot(h.astype(BF16), w_ref[...]).astype(BF16)

    return pl.pallas_call(
        body, name="lin_fwd", grid=(t // tm,),
        in_specs=[pl.BlockSpec((tm, D), lambda i: (i, 0)), pl.BlockSpec((1, D), lambda i: (0, 0)),
                  pl.BlockSpec((1, 9, D), lambda i: (i // tpe, 0, 0)), _layer(w, l)],
        out_specs=pl.BlockSpec((tm, n), lambda i: (i, 0)),
        out_shape=_mid((t, n), BF16),
        compiler_params=_cparams(40, 1),
    )(*_pin(x, gain, mod, w))


def rope_tables(pos, invf):
    t = pos.shape[0]
    tm = 1024

    def body(pos_ref, invf_ref, c_ref, s_ref):
        ang = pos_ref[...].astype(F32) * invf_ref[...]
        lane = lax.broadcasted_iota(jnp.int32, (tm, 128), 1)
        sign = jnp.where(lane % HEAD_DIM < HEAD_DIM // 2, -1.0, 1.0)
        c_ref[...] = jnp.cos(ang)
        s_ref[...] = sign * jnp.sin(ang)

    return pl.pallas_call(
        body, name="rope_tables", grid=(t // tm,),
        in_specs=[pl.BlockSpec((tm, 1), lambda i: (i, 0)), pl.BlockSpec((1, 128), lambda i: (0, 0))],
        out_specs=[pl.BlockSpec((tm, 128), lambda i: (i, 0))] * 2,
        out_shape=[jax.ShapeDtypeStruct((t, 128), F32)] * 2,
        compiler_params=_cparams(16, 1),
    )(pos, invf)


def _swap_halves(v):
    lane = lax.broadcasted_iota(jnp.int32, v.shape, 1)
    return jnp.where(lane % HEAD_DIM < HEAD_DIM // 2, pltpu.roll(v, 128 - HEAD_DIM // 2, 1), pltpu.roll(v, HEAD_DIM // 2, 1))


def _rope(v, cos, sin):
    return jnp.concatenate(
        [v[:, j:j + 128] * cos + _swap_halves(v[:, j:j + 128]) * sin for j in range(0, v.shape[1], 128)], axis=1)


def _rope_t(dv, cos, sin):
    return jnp.concatenate(
        [dv[:, j:j + 128] * cos + _swap_halves(dv[:, j:j + 128] * sin) for j in range(0, dv.shape[1], 128)], axis=1)


def _head_stats(qk, g1, g2):
    rinv = lax.rsqrt(_dot_hilo(qk * qk, g1) + EPS)
    return rinv, _dot_hilo(rinv, g2)


def qkv_fwd(x, gain, mod, w, l, b, gqk, cos, sin, g1, g2, tm=256):
    t = x.shape[0]
    tpe = t // tm // 2

    def body(x_ref, gain_ref, mod_ref, w_ref, b_ref, gqk_ref, c_ref, s_ref, g1_ref, g2_ref, raw_ref, q_ref, k_ref):
        sh, sc, _ = _mod_rows(mod_ref, 1)
        _, _, h = _norm_mod(x_ref[...], gain_ref[...], sc, sh)
        qkv = _dot(h.astype(BF16), w_ref[...]) + b_ref[...]
        raw_ref[...] = qkv.astype(BF16)
        qk = qkv[:, :QK_DIM]
        _, rb = _head_stats(qk, g1_ref[...], g2_ref[...])
        qr = _rope(qk * rb * gqk_ref[...], c_ref[...], s_ref[...])
        q_ref[...] = qr[:, :D].astype(BF16)
        k_ref[...] = qr[:, D:].astype(BF16)

    return pl.pallas_call(
        body, name="qkv_fwd", grid=(t // tm,),
        in_specs=[pl.BlockSpec((tm, D), lambda i: (i, 0)), pl.BlockSpec((1, D), lambda i: (0, 0)),
                  pl.BlockSpec((1, 9, D), lambda i: (i // tpe, 0, 0)), _layer(w, l),
                  pl.BlockSpec((1, QKV_DIM), lambda i: (0, 0)), pl.BlockSpec((1, QK_DIM), lambda i: (0, 0)),
                  pl.BlockSpec((tm, 128), lambda i: (i, 0)), pl.BlockSpec((tm, 128), lambda i: (i, 0)),
                  _resident((QK_DIM, 128)), _resident((128, QK_DIM))],
        out_specs=[pl.BlockSpec((tm, QKV_DIM), lambda i: (i, 0)), pl.BlockSpec((tm, D), lambda i: (i, 0)),
                   pl.BlockSpec((tm, N_KV * HEAD_DIM), lambda i: (i, 0))],
        out_shape=[_mid((t, QKV_DIM), BF16), _mid((t, D), BF16), _mid((t, N_KV * HEAD_DIM), BF16)],
        compiler_params=_cparams(40, 1),
    )(*_pin(x, gain, mod, w, b, gqk, cos, sin, g1, g2))


def qkv_bwd_pre(dq, dk, dv, raw, gqk, cos, sin, g1, g2, gsel, tm=256):
    t = dq.shape[0]

    def body(dq_ref, dk_ref, dv_ref, raw_ref, gqk_ref, c_ref, s_ref, g1_ref, g2_ref, gsel_ref, dz_ref, sa_ref):
        i = pl.program_id(0)
        dqk = jnp.concatenate([dq_ref[...].astype(F32), dk_ref[...]], axis=1)
        dqn = _rope_t(dqk, c_ref[...], s_ref[...])
        qk = raw_ref[:, :QK_DIM].astype(F32)
        g1v, g2v = g1_ref[...], g2_ref[...]
        rinv, rb = _head_stats(qk, g1v, g2v)
        dgq = jnp.broadcast_to(_colsum(dqn * qk * rb), (8, QK_DIM))
        dyh = dqn * gqk_ref[...]
        mh = _dot_hilo(dyh * qk, g1v)
        mb = _dot_hilo(mh * rinv * rinv * rinv, g2v)
        dz_ref[:, :QK_DIM] = (rb * dyh - qk * mb).astype(BF16)
        dz_ref[:, QK_DIM:] = dv_ref[...].astype(BF16)

        @pl.when(i == 0)
        def _():
            sa_ref[...] = jnp.zeros_like(sa_ref)

        sa_ref[...] += _dot_hilo(dgq, gsel_ref[...])

    kvw = N_KV * HEAD_DIM
    return pl.pallas_call(
        body, name="qkv_bwd_pre", grid=(t // tm,),
        in_specs=[pl.BlockSpec((tm, D), lambda i: (i, 0)), pl.BlockSpec((tm, kvw), lambda i: (i, 0)),
                  pl.BlockSpec((tm, kvw), lambda i: (i, 0)), pl.BlockSpec((tm, QKV_DIM), lambda i: (i, 0)),
                  pl.BlockSpec((1, QK_DIM), lambda i: (0, 0)),
                  pl.BlockSpec((tm, 128), lambda i: (i, 0)), pl.BlockSpec((tm, 128), lambda i: (i, 0)),
                  _resident((QK_DIM, 128)), _resident((128, QK_DIM)), _resident((QK_DIM, 128))],
        out_specs=[pl.BlockSpec((tm, QKV_DIM), lambda i: (i, 0)), pl.BlockSpec((8, 128), lambda i: (0, 0))],
        out_shape=[_mid((t, QKV_DIM), BF16), _mid((8, 128), F32)],
        compiler_params=_cparams(40, 1),
    )(*_pin(dq, dk, dv, raw, gqk, cos, sin, g1, g2, gsel))


def _band_mask(n):
    row = lax.broadcasted_iota(jnp.int32, (GROUP * BLOCK, 2 * BLOCK), 0) % BLOCK
    col = lax.broadcasted_iota(jnp.int32, (GROUP * BLOCK, 2 * BLOCK), 1)
    rel = row + BLOCK - col
    return (rel >= 0) & (rel < BLOCK) & ((col >= BLOCK) | (n > 0))


def _stack_heads(v, g):
    base = g * GROUP * HEAD_DIM
    return jnp.concatenate([v[:, base + j * HEAD_DIM:base + (j + 1) * HEAD_DIM] for j in range(GROUP)], axis=0)


def _kv_cat(prev, cur, g):
    return jnp.concatenate([prev[:, g * HEAD_DIM:(g + 1) * HEAD_DIM], cur[:, g * HEAD_DIM:(g + 1) * HEAD_DIM]], axis=0)


def _sink_col(sink_ref, g):
    return jnp.concatenate([jnp.full((BLOCK, 1), sink_ref[0, g * GROUP + j], F32) for j in range(GROUP)], axis=0)


def _attn_specs(nb):
    kvw = N_KV * HEAD_DIM
    vcol = QK_DIM // kvw
    cur = lambda e, n: (e * nb + n, 0)
    prev = lambda e, n: (e * nb + jnp.maximum(n - 1, 0), 0)
    return [pl.BlockSpec((BLOCK, D), cur),
            pl.BlockSpec((BLOCK, kvw), cur), pl.BlockSpec((BLOCK, kvw), prev),
            pl.BlockSpec((BLOCK, kvw), lambda e, n: (e * nb + n, vcol)),
            pl.BlockSpec((BLOCK, kvw), lambda e, n: (e * nb + jnp.maximum(n - 1, 0), vcol)),
            pl.BlockSpec(memory_space=pltpu.SMEM)]


def attn_fwd(q, k, raw, sinks):
    t = q.shape[0]
    nb = t // 2 // BLOCK

    def body(q_ref, kc_ref, kp_ref, vc_ref, vp_ref, sink_ref, o_ref, lse_ref):
        n = pl.program_id(1)
        qv = q_ref[...]
        kc, kp, vc, vp = kc_ref[...], kp_ref[...], vc_ref[...], vp_ref[...]
        mask = _band_mask(n)
        outs, lses = [], []
        for g in range(N_KV):
            kk, vv = _kv_cat(kp, kc, g), _kv_cat(vp, vc, g)
            s = jnp.where(mask, _dot_nt(_stack_heads(qv, g), kk) * (HEAD_DIM ** -0.5), -1e30)
            sink = _sink_col(sink_ref, g)
            m = jnp.maximum(jnp.max(s, axis=1, keepdims=True), sink)
            p = jnp.exp(s - m)
            l = jnp.sum(p, axis=1, keepdims=True) + jnp.exp(sink - m)
            o = _dot(p.astype(BF16), vv) / l
            lse = m + jnp.log(l)
            for j in range(GROUP):
                outs.append(o[j * BLOCK:(j + 1) * BLOCK, :])
                lses.append(lse[j * BLOCK:(j + 1) * BLOCK, :])
        o_ref[...] = jnp.concatenate(outs, axis=1).astype(BF16)
        lse_ref[...] = jnp.concatenate(lses, axis=1)

    cur = lambda e, n: (e * nb + n, 0)
    return pl.pallas_call(
        body, name="attn_fwd", grid=(2, nb),
        in_specs=_attn_specs(nb),
        out_specs=[pl.BlockSpec((BLOCK, D), cur), pl.BlockSpec((BLOCK, N_HEADS), cur)],
        out_shape=[_mid((t, D), BF16), _mid((t, N_HEADS), F32)],
        compiler_params=_cparams(32, 2),
    )(*_pin(q, k, k, raw, raw), sinks)


def attn_bwd(q, k, raw, sinks, o, do, lse):
    t = q.shape[0]
    s_len = t // 2
    nb = s_len // BLOCK
    kvw = N_KV * HEAD_DIM

    def body(q_ref, kc_ref, kp_ref, vc_ref, vp_ref, sink_ref, o_ref, do_ref, lse_ref, dq_ref, dk_ref, dv_ref, ds_ref):
        n = pl.program_id(1)

        @pl.when(n == 0)
        def _():
            dk_ref[...] = jnp.zeros_like(dk_ref)
            dv_ref[...] = jnp.zeros_like(dv_ref)

        @pl.when((n == 0) & (pl.program_id(0) == 0))
        def _():
            ds_ref[...] = jnp.zeros_like(ds_ref)

        qv, ov, dov, lsev = q_ref[...], o_ref[...], do_ref[...], lse_ref[...]
        kc, kp, vc, vp = kc_ref[...], kp_ref[...], vc_ref[...], vp_ref[...]
        mask = _band_mask(n)
        dqs, dks, dvs, dsk = [], [], [], []
        for g in range(N_KV):
            kk, vv = _kv_cat(kp, kc, g), _kv_cat(vp, vc, g)
            qg, og, dog = _stack_heads(qv, g), _stack_heads(ov, g), _stack_heads(dov, g)
            lse = jnp.concatenate([lsev[:, g * GROUP + j:g * GROUP + j + 1] for j in range(GROUP)], axis=0)
            s = jnp.where(mask, _dot_nt(qg, kk) * (HEAD_DIM ** -0.5), -1e30)
            p = jnp.exp(s - lse)
            dd = _dot_hilo(dog.astype(F32) * og.astype(F32), jnp.ones((HEAD_DIM, 128), BF16))[:, :1]
            ds = (p * (_dot_nt(dog, vv) - dd) * (HEAD_DIM ** -0.5)).astype(BF16)
            dqg = _dot(ds, kk)
            dks.append(_dot_tn(ds, qg))
            dvs.append(_dot_tn(p.astype(BF16), dog))
            wsink = jnp.exp(_sink_col(sink_ref, g) - lse) * dd
            for j in range(GROUP):
                dqs.append(dqg[j * BLOCK:(j + 1) * BLOCK, :])
                dsk.append(wsink[j * BLOCK:(j + 1) * BLOCK, :])
        dq_ref[...] = jnp.concatenate(dqs, axis=1).astype(BF16)
        dkk = jnp.concatenate(dks, axis=1)
        dvv = jnp.concatenate(dvs, axis=1)
        prev0 = pl.multiple_of(jnp.maximum(n - 1, 0) * BLOCK, BLOCK)
        cur0 = pl.multiple_of(n * BLOCK, BLOCK)
        dk_ref[pl.ds(prev0, BLOCK), :] += dkk[:BLOCK]
        dv_ref[pl.ds(prev0, BLOCK), :] += dvv[:BLOCK]
        dk_ref[pl.ds(cur0, BLOCK), :] += dkk[BLOCK:]
        dv_ref[pl.ds(cur0, BLOCK), :] += dvv[BLOCK:]
        ds_ref[0:1, :] -= _colsum(jnp.concatenate(dsk, axis=1))

    cur = lambda e, n: (e * nb + n, 0)
    return pl.pallas_call(
        body, name="attn_bwd", grid=(2, nb),
        in_specs=_attn_specs(nb) + [pl.BlockSpec((BLOCK, D), cur), pl.BlockSpec((BLOCK, D), cur),
                                    pl.BlockSpec((BLOCK, N_HEADS), cur)],
        out_specs=[pl.BlockSpec((BLOCK, D), cur), pl.BlockSpec((s_len, kvw), lambda e, n: (e, 0)),
                   pl.BlockSpec((s_len, kvw), lambda e, n: (e, 0)), pl.BlockSpec((8, N_HEADS), lambda e, n: (0, 0))],
        out_shape=[_mid((t, D), BF16), _mid((t, kvw), F32), _mid((t, kvw), F32), _mid((8, N_HEADS), F32)],
        compiler_params=_cparams(32, 2),
    )(*_pin(q, k, k, raw, raw), sinks, *_pin(o, do, lse))


CONV_COLS = 256


def _conv_specs(s_len):
    nct = D // CONV_COLS
    return [pl.BlockSpec((s_len, CONV_COLS), lambda j, e, *_: (e, j)),
            pl.BlockSpec((s_len, CONV_COLS), lambda j, e, *_: (e, nct + j)),
            pl.BlockSpec((s_len, CONV_COLS), lambda j, e, *_: (e, 2 * nct + j)),
            pl.BlockSpec((3, CONV_COLS), lambda j, e, *_: (0, j))]


def _conv_taps(gc, v, w, s_len):
    u = gc * v
    row = lax.broadcasted_iota(jnp.int32, u.shape, 0)
    u1 = jnp.where(row >= 1, pltpu.roll(u, 1, 0), 0.0)
    u2 = jnp.where(row >= 2, pltpu.roll(u, 2, 0), 0.0)
    return u, u1, u2, w[2:3, :] * u + w[1:2, :] * u1 + w[0:1, :] * u2


def conv_fwd(z, w):
    t = z.shape[0]
    s_len = t // 2

    def body(gb_ref, gc_ref, v_ref, w_ref, p_ref):
        _, _, _, conv = _conv_taps(gc_ref[...].astype(F32), v_ref[...].astype(F32), w_ref[...], s_len)
        p_ref[...] = (gb_ref[...].astype(F32) * conv).astype(BF16)

    return pl.pallas_call(
        body, name="conv_fwd", grid=(D // CONV_COLS, 2),
        in_specs=_conv_specs(s_len),
        out_specs=pl.BlockSpec((s_len, CONV_COLS), lambda j, e: (e, j)),
        out_shape=_mid((t, D), BF16),
        compiler_params=_cparams(40, 2),
    )(*_pin(z, z, z, w))


def conv_bwd(z, w, dp):
    t = z.shape[0]
    s_len = t // 2
    nct = D // CONV_COLS

    def body(gb_ref, gc_ref, v_ref, w_ref, dp_ref, dz_ref, dw_ref, parts_ref):
        e, part = pl.program_id(1), pl.program_id(2)

        @pl.when(part == 0)
        def _():
            gc, v, wv = gc_ref[...].astype(F32), v_ref[...].astype(F32), w_ref[...]
            u, u1, u2, conv = _conv_taps(gc, v, wv, s_len)
            dpv = dp_ref[...].astype(F32)
            parts_ref[0] = (dpv * conv).astype(BF16)
            dc = dpv * gb_ref[...].astype(F32)
            row = lax.broadcasted_iota(jnp.int32, dc.shape, 0)
            dc1 = jnp.where(row <= s_len - 2, pltpu.roll(dc, s_len - 1, 0), 0.0)
            dc2 = jnp.where(row <= s_len - 3, pltpu.roll(dc, s_len - 2, 0), 0.0)
            du = wv[2:3, :] * dc + wv[1:2, :] * dc1 + wv[0:1, :] * dc2
            parts_ref[1] = (du * v).astype(BF16)
            parts_ref[2] = (du * gc).astype(BF16)

            @pl.when(e == 0)
            def _():
                dw_ref[...] = jnp.zeros_like(dw_ref)

            dw_ref[0:1, :] += _colsum(dc * u2)
            dw_ref[1:2, :] += _colsum(dc * u1)
            dw_ref[2:3, :] += _colsum(dc * u)

        dz_ref[...] = parts_ref[part]

    return pl.pallas_call(
        body, name="conv_bwd", grid=(nct, 2, 3),
        in_specs=_conv_specs(s_len) + [pl.BlockSpec((s_len, CONV_COLS), lambda j, e, part: (e, j))],
        out_specs=[pl.BlockSpec((s_len, CONV_COLS), lambda j, e, part: (e, part * nct + j)),
                   pl.BlockSpec((8, CONV_COLS), lambda j, e, part: (0, j))],
        out_shape=[_mid((t, 3 * D), BF16), jax.ShapeDtypeStruct((8, D), F32)],
        scratch_shapes=[pltpu.VMEM((3, s_len, CONV_COLS), BF16)],
        compiler_params=_cparams(48, 3),
    )(*_pin(z, z, z, w, dp))


def loss_grad(y, tgt, tm=512):
    t = y.shape[0]

    def body(y_ref, t_ref, dy_ref, l_ref):
        i = pl.program_id(0)
        d = y_ref[...] - t_ref[...]
        dy_ref[...] = d * (1.0 / D)

        @pl.when(i == 0)
        def _():
            l_ref[...] = jnp.zeros_like(l_ref)

        l_ref[...] += 0.5 / D * jnp.sum(d * d)

    return pl.pallas_call(
        body, name="loss_grad", grid=(t // tm,),
        in_specs=[pl.BlockSpec((tm, D), lambda i: (i, 0))] * 2,
        out_specs=[pl.BlockSpec((tm, D), lambda i: (i, 0)), pl.BlockSpec((8, 128), lambda i: (0, 0))],
        out_shape=[_mid((t, D), F32), _mid((8, 128), F32)],
        compiler_params=_cparams(32, 1),
    )(*_pin(y, tgt))


ADA_COLS = 384


def ada_fwd(c_all, w):
    nl, _, n = w.shape
    nex = c_all.shape[0]

    def body(c_ref, w_ref, o_ref):
        cv = c_ref[...]
        ca = (cv * jax.nn.sigmoid(cv)).astype(BF16)
        o_ref[0] = _dot(ca, w_ref[0].astype(BF16))

    return pl.pallas_call(
        body, name="ada_fwd", grid=(nl, n // ADA_COLS),
        in_specs=[pl.BlockSpec((nex, D), lambda l, j: (0, 0)), pl.BlockSpec((1, D, ADA_COLS), lambda l, j: (l, 0, j))],
        out_specs=pl.BlockSpec((1, nex, ADA_COLS), lambda l, j: (l, 0, j)),
        out_shape=jax.ShapeDtypeStruct((nl, nex, n), F32),
        compiler_params=_cparams(32, 2),
    )(*_pin(c_all, w))


def _adam_math(w, g, m, v):
    m = ADAM_B1 * m + (1.0 - ADAM_B1) * g
    v = ADAM_B2 * v + (1.0 - ADAM_B2) * (g * g)
    m_hat = m / (1.0 - ADAM_B1 ** ADAM_STEP)
    v_hat = v / (1.0 - ADAM_B2 ** ADAM_STEP)
    return -ADAM_LR * (m_hat / (jnp.sqrt(v_hat) + ADAM_EPS) + ADAM_WD * w), m, v


def ada_bwd_adam(c_all, dm, w, m, v):
    nl, _, n = w.shape
    nex = c_all.shape[0]

    def body(c_ref, dm_ref, w_ref, m_ref, v_ref, g_ref, d_ref, mo_ref, vo_ref):
        cv = c_ref[...]
        ca = (cv * jax.nn.sigmoid(cv)).astype(BF16)
        g = _dot_tn(ca, dm_ref[0].astype(BF16))
        g_ref[0] = g
        d_ref[0], mo_ref[0], vo_ref[0] = _adam_math(w_ref[0], g, m_ref[0], v_ref[0])

    wspec = pl.BlockSpec((1, D, ADA_COLS), lambda l, j: (l, 0, j))
    return pl.pallas_call(
        body, name="ada_bwd_adam", grid=(nl, n // ADA_COLS),
        in_specs=[pl.BlockSpec((nex, D), lambda l, j: (0, 0)), pl.BlockSpec((1, nex, ADA_COLS), lambda l, j: (l, 0, j)),
                  wspec, wspec, wspec],
        out_specs=[wspec] * 4,
        out_shape=[jax.ShapeDtypeStruct(w.shape, F32)] * 4,
        compiler_params=_cparams(40, 2),
    )(*_pin(c_all, dm, w, m, v))


def adamw(w, g, m, v):
    shape = w.shape
    cols = shape[-1]
    rows = w.size // cols
    args = [a.reshape(rows, cols) for a in (w, g, m, v)]
    tr = rows
    while tr * cols * 4 > (1 << 20) and tr % 16 == 0:
        tr //= 2

    def body(w_ref, g_ref, m_ref, v_ref, d_ref, mo_ref, vo_ref):
        d_ref[...], mo_ref[...], vo_ref[...] = _adam_math(w_ref[...], g_ref[...], m_ref[...], v_ref[...])

    spec = pl.BlockSpec((tr, cols), lambda i: (i, 0))
    outs = pl.pallas_call(
        body, name="adamw", grid=(rows // tr,),
        in_specs=[spec] * 4, out_specs=[spec] * 3,
        out_shape=[jax.ShapeDtypeStruct((rows, cols), F32)] * 3,
        compiler_params=_cparams(32, 1),
    )(*args)
    return [o.reshape(shape) for o in outs]


def adamw_layers(w, g, m, v, prev, l0, n):
    _, r, c = w.shape
    tr = r
    while tr * c * 4 > (2 << 20) and tr % 16 == 0:
        tr //= 2

    def body(w_ref, g_ref, m_ref, v_ref, pd_ref, pm_ref, pv_ref, d_ref, mo_ref, vo_ref):
        d_ref[...], mo_ref[...], vo_ref[...] = _adam_math(w_ref[...], g_ref[...], m_ref[...], v_ref[...])

    spec = pl.BlockSpec((1, tr, c), lambda l, i: (l0 + l, i, 0))
    return pl.pallas_call(
        body, name="adamw_layers", grid=(n, r // tr),
        in_specs=[spec] * 4 + [ANY] * 3, out_specs=[spec] * 3,
        out_shape=[jax.ShapeDtypeStruct(w.shape, F32)] * 3,
        input_output_aliases={4: 0, 5: 1, 6: 2},
        compiler_params=_cparams(40, 2),
    )(*_pin(w, g, m, v, *prev))


def cast_into_window(w, l0, n, ax, chip, after=()):
    _, r, c = w.shape
    tr = r
    while tr * c * 4 > (4 << 20) and tr % 32 == 0:
        tr //= 2
    nrb = r // tr
    full = (n, r * N_CHIPS, c) if ax == 1 else (n, r, c * N_CHIPS)

    def body(chip_ref, w_ref, *rest):
        o_ref = rest[len(after)]
        o_ref[...] = w_ref[...].astype(BF16)

    def omap(l, i, chip_ref):
        return (l, chip_ref[0] * nrb + i, 0) if ax == 1 else (l, i, chip_ref[0])

    return pl.pallas_call(
        body, name="cast_into_window",
        grid_spec=pltpu.PrefetchScalarGridSpec(
            num_scalar_prefetch=1, grid=(n, nrb),
            in_specs=[pl.BlockSpec((1, tr, c), lambda l, i, chip_ref: (l0 + l, i, 0))] + [ANY] * len(after),
            out_specs=pl.BlockSpec((1, tr, c), omap)),
        out_shape=_mid(full, BF16), compiler_params=_cparams(32, 2),
    )(chip, *_pin(w, *after))


def add_bias(a, b):
    def body(a_ref, b_ref, o_ref):
        o_ref[...] = a_ref[...] + b_ref[...]

    return pl.pallas_call(body, name="add_bias", out_shape=jax.ShapeDtypeStruct(a.shape, F32))(a, b)


N_DMOD_ROWS = 40


def reduce_small(p_all):
    rows = p_all.shape[1]

    def body(p_ref, red_ref, ex_ref):
        acc = p_ref[0]
        for d in range(1, 8):
            acc = acc + p_ref[d]
        red_ref[...] = acc
        ex_ref[...] = acc[:N_DMOD_ROWS] + acc[N_DMOD_ROWS:2 * N_DMOD_ROWS]

    return pl.pallas_call(
        body, name="reduce_small",
        out_shape=[jax.ShapeDtypeStruct((rows, D), F32), jax.ShapeDtypeStruct((N_DMOD_ROWS, D), F32)],
        compiler_params=_cparams(32, 0),
    )(p_all)


def _place():
    return lax.axis_index("x"), lax.axis_index("y"), lax.axis_index("c")


def _other_chips(x, y):
    return [(1 - x, y), (x, 1 - y), (1 - x, 1 - y)]


def _sl(ref, axis, start, size):
    idx = [slice(None)] * len(ref.shape)
    idx[axis] = pl.ds(pl.multiple_of(start, 16), size)
    return ref.at[tuple(idx)]


def _rcopy(src, dst, send_sem, recv_sem, to):
    return pltpu.make_async_remote_copy(src_ref=src, dst_ref=dst, send_sem=send_sem, recv_sem=recv_sem,
                                        device_id=to, device_id_type=MESH)


def allgather_small(v, all_devices):
    rows, cols = v.shape
    flips = [(dx, dy, dc) for dx in (0, 1) for dy in (0, 1) for dc in (0, 1)
             if (dx, dy, dc) != (0, 0, 0) and (all_devices or dc == 0)]
    n_out = 8 if all_devices else 4

    def body(v_ref, o_ref, send_sems, recv_sems):
        x, y, c = _place()

        def slot(px, py, pc):
            return 4 * px + 2 * py + pc if all_devices else 2 * px + py

        peers = [(1 - x if dx else x, 1 - y if dy else y, 1 - c if dc else c) for dx, dy, dc in flips]
        sends = [_rcopy(v_ref, o_ref.at[slot(x, y, c)], send_sems.at[r], recv_sems.at[r], peer)
                 for r, peer in enumerate(peers)]
        for cp in sends:
            cp.start()
        o_ref[slot(x, y, c)] = v_ref[...]
        for r, peer in enumerate(peers):
            _rcopy(v_ref, o_ref.at[slot(*peer)], send_sems.at[r], recv_sems.at[r], peer).wait_recv()
        for cp in sends:
            cp.wait_send()

    vm = pl.BlockSpec(memory_space=pltpu.VMEM)
    return pl.pallas_call(
        body, name="allgather_small_all" if all_devices else "allgather_small_chips",
        in_specs=[vm], out_specs=vm,
        out_shape=jax.ShapeDtypeStruct((n_out, rows, cols), v.dtype),
        scratch_shapes=[pltpu.SemaphoreType.DMA((len(flips),)), pltpu.SemaphoreType.DMA((len(flips),))],
        compiler_params=pltpu.CompilerParams(vmem_limit_bytes=32 * 1024 * 1024),
    )(v)


HBM = pl.BlockSpec(memory_space=pltpu.HBM)
SEM = pl.BlockSpec(memory_space=pltpu.SEMAPHORE)
SPLIT_COPY = pltpu.CompilerParams(has_side_effects=pltpu.SideEffectType.DATAFLOW_SIDE_EFFECTING)


def _in_hbm(a):
    return pltpu.with_memory_space_constraint(a, pltpu.HBM)


def _window(ref, ax, chip):
    n = ref.shape[ax] // N_CHIPS
    return _sl(ref, ax, (2 * chip[0] + chip[1]) * n, n)


def _half(ref, ax, cc):
    ha = 3 - ax
    hs = ref.shape[ha] // 2
    return _sl(ref, ha, cc * hs, hs)


def gather_start(bufs, axes, tag):
    na = len(bufs)

    def body(*refs):
        ins = refs[:na]
        send_sems, recv_sems = refs[na], refs[na + 1]
        token = refs[2 * na + 2]
        x, y, c = _place()
        for a in range(na):
            mine = _half(_window(ins[a], axes[a], (x, y)), axes[a], c)
            for j, chip in enumerate(_other_chips(x, y)):
                _rcopy(mine, mine, send_sems.at[3 * a + j], recv_sems.at[3 * a + j], (*chip, c)).start()
        token[...] = jnp.zeros_like(token)

    dma = pltpu.SemaphoreType.DMA
    outs = pl.pallas_call(
        body, name="gather_start_" + tag,
        in_specs=[HBM] * na,
        out_specs=(SEM, SEM, *[HBM] * na, pl.BlockSpec(memory_space=pltpu.VMEM)),
        out_shape=(dma((3 * na,)), dma((3 * na,)), *[pltpu.HBM(b.shape, b.dtype) for b in bufs],
                   jax.ShapeDtypeStruct((8, 128), F32)),
        input_output_aliases={a: 2 + a for a in range(na)},
        compiler_params=SPLIT_COPY,
    )(*[_in_hbm(b) for b in bufs])
    return outs[0], outs[1], list(outs[2:2 + na]), outs[2 + na]


def gather_wait(send_sems, recv_sems, bufs, axes, after, tag):
    na = len(bufs)

    def body(*refs):
        ins = refs[:na]
        send_sems, recv_sems = refs[na], refs[na + 1]
        x, y, c = _place()
        for a in range(na):
            for j, chip in enumerate(_other_chips(x, y)):
                got = _half(_window(ins[a], axes[a], chip), axes[a], c)
                _rcopy(got, got, send_sems.at[3 * a + j], recv_sems.at[3 * a + j], (*chip, c)).wait_recv()
        for a in range(na):
            mine = _half(_window(ins[a], axes[a], (x, y)), axes[a], c)
            for j, chip in enumerate(_other_chips(x, y)):
                _rcopy(mine, mine, send_sems.at[3 * a + j], recv_sems.at[3 * a + j], (*chip, c)).wait_send()

    return pl.pallas_call(
        body, name="gather_wait_" + tag,
        in_specs=[HBM] * na + [SEM, SEM] + [ANY] * len(after),
        out_specs=[HBM] * na,
        out_shape=[pltpu.HBM(b.shape, b.dtype) for b in bufs],
        input_output_aliases={a: a for a in range(na)},
        compiler_params=SPLIT_COPY,
    )(*bufs, send_sems, recv_sems, *after)


def gather_forward(bufs, axes):
    na = len(bufs)

    def body(*refs):
        outs = refs[na:2 * na]
        send_sems, recv_sems = refs[2 * na:]
        x, y, c = _place()
        chips = _other_chips(x, y)
        passed = []
        for a in range(na):
            for j, chip in enumerate(chips):
                got = _half(_window(outs[a], axes[a], chip), axes[a], c)
                cp = _rcopy(got, got, send_sems.at[3 * a + j], recv_sems.at[3 * a + j], (x, y, 1 - c))
                cp.start()
                passed.append(cp)
        for a in range(na):
            for j, chip in enumerate(chips):
                got = _half(_window(outs[a], axes[a], chip), axes[a], 1 - c)
                _rcopy(got, got, send_sems.at[3 * a + j], recv_sems.at[3 * a + j], (x, y, 1 - c)).wait_recv()
        for cp in passed:
            cp.wait_send()

    dma = pltpu.SemaphoreType.DMA
    return pl.pallas_call(
        body, name="gather_forward",
        in_specs=[ANY] * na, out_specs=[ANY] * na,
        out_shape=[jax.ShapeDtypeStruct(b.shape, BF16) for b in bufs],
        input_output_aliases={a: a for a in range(na)},
        scratch_shapes=[dma((3 * na,)), dma((3 * na,))],
    )(*bufs)


def exchange_halves(grads, axes):
    na = len(grads)

    def hshape(g, ax):
        ha = 3 - ax
        return tuple(d // 2 if i == ha else d for i, d in enumerate(g.shape))

    def body(*refs):
        ins, outs = refs[:na], refs[na:2 * na]
        send_sems, recv_sems = refs[2 * na:]
        x, y, c = _place()
        cps = []
        for a in range(na):
            ha = 3 - axes[a]
            hs = ins[a].shape[ha] // 2
            cp = _rcopy(_sl(ins[a], ha, (1 - c) * hs, hs), outs[a], send_sems.at[a], recv_sems.at[a], (x, y, 1 - c))
            cp.start()
            cps.append(cp)
        for cp in cps:
            cp.wait_recv()
        for cp in cps:
            cp.wait_send()

    dma = pltpu.SemaphoreType.DMA
    return pl.pallas_call(
        body, name="exchange_halves",
        in_specs=[ANY] * na, out_specs=[ANY] * na,
        out_shape=[jax.ShapeDtypeStruct(hshape(g, ax), BF16) for g, ax in zip(grads, axes)],
        scratch_shapes=[dma((na,)), dma((na,))],
    )(*grads)


def scatter_start(halves, axes, tag):
    na = len(halves)

    def pshape(h, ax):
        return (N_CHIPS - 1,) + tuple(d // N_CHIPS if i == ax else d for i, d in enumerate(h.shape))

    def body(*refs):
        ins, lands = refs[:na], refs[na:2 * na]
        send_sems, recv_sems = refs[2 * na], refs[2 * na + 1]
        token = refs[4 * na + 2]
        x, y, c = _place()
        for a in range(na):
            for j, chip in enumerate(_other_chips(x, y)):
                _rcopy(_window(ins[a], axes[a], chip), lands[a].at[j],
                       send_sems.at[3 * a + j], recv_sems.at[3 * a + j], (*chip, c)).start()
        token[...] = jnp.zeros_like(token)

    dma = pltpu.SemaphoreType.DMA
    lands = [lax.empty(pshape(h, ax), BF16) for h, ax in zip(halves, axes)]
    outs = pl.pallas_call(
        body, name="scatter_start_" + tag,
        in_specs=[HBM] * (2 * na),
        out_specs=(SEM, SEM, *[HBM] * (2 * na), pl.BlockSpec(memory_space=pltpu.VMEM)),
        out_shape=(dma((3 * na,)), dma((3 * na,)), *[pltpu.HBM(b.shape, b.dtype) for b in halves + lands],
                   jax.ShapeDtypeStruct((8, 128), F32)),
        input_output_aliases={a: 2 + a for a in range(2 * na)},
        compiler_params=SPLIT_COPY,
    )(*[_in_hbm(b) for b in halves + lands])
    return outs[0], outs[1], list(outs[2:2 + na]), list(outs[2 + na:2 + 2 * na]), outs[2 + 2 * na]


def scatter_wait(send_sems, recv_sems, halves, lands, axes, after, tag):
    na = len(halves)

    def body(*refs):
        ins, lands = refs[:na], refs[na:2 * na]
        send_sems, recv_sems = refs[2 * na], refs[2 * na + 1]
        x, y, c = _place()
        for a in range(na):
            for j, chip in enumerate(_other_chips(x, y)):
                _rcopy(_window(ins[a], axes[a], chip), lands[a].at[j],
                       send_sems.at[3 * a + j], recv_sems.at[3 * a + j], (*chip, c)).wait_recv()
        for a in range(na):
            for j, chip in enumerate(_other_chips(x, y)):
                _rcopy(_window(ins[a], axes[a], chip), lands[a].at[j],
                       send_sems.at[3 * a + j], recv_sems.at[3 * a + j], (*chip, c)).wait_send()

    outs = pl.pallas_call(
        body, name="scatter_wait_" + tag,
        in_specs=[HBM] * (2 * na) + [SEM, SEM] + [ANY] * len(after),
        out_specs=[HBM] * (2 * na),
        out_shape=[pltpu.HBM(b.shape, b.dtype) for b in halves + lands],
        input_output_aliases={a: a for a in range(2 * na)},
        compiler_params=SPLIT_COPY,
    )(*halves, *lands, send_sems, recv_sems, *after)
    return list(outs[:na]), list(outs[na:])


def join_halves(gs, regions, axes):
    na = len(gs)

    def body(*refs):
        outs = refs[na:2 * na]
        send_sems, recv_sems = refs[2 * na:]
        x, y, c = _place()
        cps = []
        for a in range(na):
            ha = 3 - axes[a]
            hs = outs[a].shape[ha] // 2
            reg = outs[a].at[pl.ds(*regions[a])]
            mine = _sl(reg, ha, c * hs, hs)
            cp = _rcopy(mine, mine, send_sems.at[a], recv_sems.at[a], (x, y, 1 - c))
            cp.start()
            cps.append((cp, _sl(reg, ha, (1 - c) * hs, hs)))
        for a, (cp, theirs) in enumerate(cps):
            _rcopy(theirs, theirs, send_sems.at[a], recv_sems.at[a], (x, y, 1 - c)).wait_recv()
        for cp, _ in cps:
            cp.wait_send()

    dma = pltpu.SemaphoreType.DMA
    return pl.pallas_call(
        body, name="join_halves",
        in_specs=[ANY] * na, out_specs=[ANY] * na,
        out_shape=[jax.ShapeDtypeStruct(g.shape, F32) for g in gs],
        input_output_aliases={a: a for a in range(na)},
        scratch_shapes=[dma((na,)), dma((na,))],
    )(*gs)


def _tile2(r, c, itemsize, limit):
    bc = c
    while bc > 1536:
        bc //= 2
    assert c % bc == 0 and bc % 128 == 0
    br = r
    while br * bc * itemsize > limit and br % 32 == 0:
        br //= 2
    assert r % br == 0 and br % 16 == 0
    return br, bc


def add_my_half(g, theirs, ax, cc):
    ha = 3 - ax
    nl, r, c = theirs.shape
    br, bc = _tile2(r, c, 2, 2 << 20)
    nrb, ncb = r // br, c // bc

    def body(cc_ref, g_ref, t_ref, o_ref):
        o_ref[...] = (g_ref[...].astype(F32) + t_ref[...].astype(F32)).astype(BF16)

    def gmap(l, i, j, cc_ref):
        return (l, cc_ref[0] * nrb + i, j) if ha == 1 else (l, i, cc_ref[0] * ncb + j)

    blk = pl.BlockSpec((1, br, bc), lambda l, i, j, cc_ref: (l, i, j))
    return pl.pallas_call(
        body, name="add_my_half",
        grid_spec=pltpu.PrefetchScalarGridSpec(
            num_scalar_prefetch=1, grid=(nl, nrb, ncb),
            in_specs=[pl.BlockSpec((1, br, bc), gmap), blk], out_specs=blk),
        out_shape=_mid(theirs.shape, BF16),
        compiler_params=_cparams(32, 3),
    )(cc, *_pin(g, theirs))


def sum_chips(parts, pair, gstack, l0, ax, where):
    _, n, r, c = parts.shape
    ha = 3 - ax
    br, bc = _tile2(r, c, 2, 1 << 20)
    nrb, ncb = r // br, c // bc

    def body(w_ref, p_ref, own_ref, g_ref, o_ref):
        acc = own_ref[...].astype(F32)
        for q in range(N_CHIPS - 1):
            acc = acc + p_ref[q].astype(F32)
        o_ref[...] = acc

    def own_map(l, i, j, w_ref):
        return (l, w_ref[0] * nrb + i, j) if ax == 1 else (l, i, w_ref[0] * ncb + j)

    def out_map(l, i, j, w_ref):
        return (l0 + l, w_ref[1] * nrb + i, j) if ha == 1 else (l0 + l, i, w_ref[1] * ncb + j)

    return pl.pallas_call(
        body, name="sum_chips",
        grid_spec=pltpu.PrefetchScalarGridSpec(
            num_scalar_prefetch=1, grid=(n, nrb, ncb),
            in_specs=[pl.BlockSpec((N_CHIPS - 1, 1, br, bc), lambda l, i, j, w_ref: (0, l, i, j)),
                      pl.BlockSpec((1, br, bc), own_map), ANY],
            out_specs=pl.BlockSpec((1, br, bc), out_map)),
        out_shape=jax.ShapeDtypeStruct(gstack.shape, F32),
        input_output_aliases={3: 0},
        compiler_params=_cparams(32, 3),
    )(where, *_pin(parts, pair, gstack))


BIG = ("w_ffn_up", "w_ffn_down", "attn_w_qkv", "attn_w_o", "conv_w_in", "conv_w_out")
BIG_AXIS = {"w_ffn_up": 2, "w_ffn_down": 1, "attn_w_qkv": 2, "attn_w_o": 1, "conv_w_in": 2, "conv_w_out": 1}
WEIGHTS = ("norm_gain", "w_ada", "b_ada", "w_ffn_up", "w_ffn_down", "attn_w_qkv", "attn_b_qkv", "attn_q_gain",
           "attn_k_gain", "attn_sinks", "attn_w_o", "attn_b_o", "conv_w_in", "conv_w", "conv_w_out")
N_SMALL_ROWS = 112


def _stack3(a):
    return a.reshape((-1,) + a.shape[-2:])


def _head_matrices():
    lane = jnp.arange(QK_DIM)
    head = lane // HEAD_DIM
    col = jnp.arange(128)
    g1 = jnp.where(head[:, None] == col[None, :], 1.0 / HEAD_DIM, 0.0).astype(BF16)
    g2 = jnp.where(col[:, None] == head[None, :], 1.0, 0.0).astype(BF16)
    fold = lane % HEAD_DIM + jnp.where(head >= N_HEADS, HEAD_DIM, 0)
    gsel = jnp.where(fold[:, None] == col[None, :], 1.0, 0.0).astype(BF16)
    return g1, g2, gsel


def _pad_cols(a, n):
    return jnp.pad(a, ((0, 0), (0, n - a.shape[1])))


def kernel(x, c, positions, norm_gain, w_ada, b_ada, w_ffn_up, w_ffn_down, attn_w_qkv, attn_b_qkv, attn_q_gain, attn_k_gain, attn_sinks, attn_w_o, attn_b_o, conv_w_in, conv_w, conv_w_out, loss_target, m_norm_gain, m_w_ada, m_b_ada, m_w_ffn_up, m_w_ffn_down, m_attn_w_qkv, m_attn_b_qkv, m_attn_q_gain, m_attn_k_gain, m_attn_sinks, m_attn_w_o, m_attn_b_o, m_conv_w_in, m_conv_w, m_conv_w_out, v_norm_gain, v_w_ada, v_b_ada, v_w_ffn_up, v_w_ffn_down, v_attn_w_qkv, v_attn_b_qkv, v_attn_q_gain, v_attn_k_gain, v_attn_sinks, v_attn_w_o, v_attn_b_o, v_conv_w_in, v_conv_w, v_conv_w_out):
    w = dict(norm_gain=norm_gain, w_ada=w_ada, b_ada=b_ada, w_ffn_up=w_ffn_up, w_ffn_down=w_ffn_down,
             attn_w_qkv=attn_w_qkv, attn_b_qkv=attn_b_qkv, attn_q_gain=attn_q_gain, attn_k_gain=attn_k_gain,
             attn_sinks=attn_sinks, attn_w_o=attn_w_o, attn_b_o=attn_b_o, conv_w_in=conv_w_in, conv_w=conv_w,
             conv_w_out=conv_w_out)
    mom = dict(norm_gain=m_norm_gain, w_ada=m_w_ada, b_ada=m_b_ada, w_ffn_up=m_w_ffn_up, w_ffn_down=m_w_ffn_down,
               attn_w_qkv=m_attn_w_qkv, attn_b_qkv=m_attn_b_qkv, attn_q_gain=m_attn_q_gain,
               attn_k_gain=m_attn_k_gain, attn_sinks=m_attn_sinks, attn_w_o=m_attn_w_o, attn_b_o=m_attn_b_o,
               conv_w_in=m_conv_w_in, conv_w=m_conv_w, conv_w_out=m_conv_w_out)
    var = dict(norm_gain=v_norm_gain, w_ada=v_w_ada, b_ada=v_b_ada, w_ffn_up=v_w_ffn_up, w_ffn_down=v_w_ffn_down,
               attn_w_qkv=v_attn_w_qkv, attn_b_qkv=v_attn_b_qkv, attn_q_gain=v_attn_q_gain,
               attn_k_gain=v_attn_k_gain, attn_sinks=v_attn_sinks, attn_w_o=v_attn_w_o, attn_b_o=v_attn_b_o,
               conv_w_in=v_conv_w_in, conv_w=v_conv_w, conv_w_out=v_conv_w_out)

    xi, yi, ci = _place()
    chip = 2 * xi + yi
    dev = 4 * xi + 2 * yi + ci
    nex, s_len, _ = x.shape
    t = nex * s_len
    n_attn, n_conv = attn_w_qkv.shape[0], conv_w_in.shape[0]
    axes = [BIG_AXIS[n] for n in BIG]

    c_all = allgather_small(jnp.pad(c, ((0, 8 - nex), (0, 0))), True)[:, :nex].reshape(8 * nex, D)
    ada_cols = w_ada.shape[2]
    modp = ada_fwd(c_all, w_ada)
    modg = allgather_small(modp.reshape(DEPTH * 8 * nex, ada_cols), False)
    modg = lax.dynamic_slice_in_dim(modg.reshape(N_CHIPS, DEPTH, 8 * nex, ada_cols), dev * nex, nex, axis=2)
    modg = modg.transpose(1, 2, 0, 3).reshape(DEPTH, nex, 9 * D)
    mod = add_bias(modg, b_ada.reshape(DEPTH, 1, 9 * D)).reshape(DEPTH, nex, 9, D)

    small = jnp.concatenate([norm_gain.reshape(DEPTH * 3, -1), conv_w.reshape(n_conv * 3, -1)], axis=0)
    small = jnp.pad(small, ((0, -small.shape[0] % 8), (0, 0)))
    small = allgather_small(small, False).transpose(1, 0, 2).reshape(small.shape[0], D)
    gain_full = small[:DEPTH * 3].reshape(DEPTH, 3, D)
    convw_full = small[DEPTH * 3:DEPTH * 3 + n_conv * 3].reshape(n_conv, 3, D)

    chip_arr = chip.reshape(1).astype(jnp.int32)
    where = jnp.stack([chip, ci]).astype(jnp.int32)
    stacks = [_stack3(w[n]) for n in BIG]

    def mixer(i):
        return (2, 3) if i % 2 == 0 else (4, 5)

    groups = [[(0, 0, 1), (1, 0, 1)], [(mixer(0)[0], 0, 1), (mixer(0)[1], 0, 1), (0, 1, 1), (1, 1, 1)]]
    groups += [[(0, 2 * i, 2), (1, 2 * i, 2), (mixer(i)[0], i // 2, 1), (mixer(i)[1], i // 2, 1)]
               for i in range(1, DEPTH)]
    gaxes = [[axes[b] for b, _, _ in grp] for grp in groups]
    where_is = {(b, l0 + k): (g, a, k) for g, grp in enumerate(groups) for a, (b, l0, n) in enumerate(grp)
                for k in range(n)}
    in_flight, token = [], (mod, small)
    for g, grp in enumerate(groups):
        bufs = [cast_into_window(stacks[b], l0, n, axes[b], chip_arr, token) for b, l0, n in grp]
        ssem, rsem, bufs, tok = gather_start(bufs, gaxes[g], f"g{g}")
        in_flight.append((ssem, rsem, bufs))
        token = (tok,)

    invf = ROPE_THETA ** (-jnp.arange(0, HEAD_DIM, 2, dtype=F32) / HEAD_DIM)
    cos, sin = rope_tables(positions.reshape(t, 1), jnp.tile(invf, 4).reshape(1, 128))
    g1, g2, gsel = _head_matrices()
    gqk = [jnp.concatenate([jnp.tile(attn_q_gain[j], N_HEADS), jnp.tile(attn_k_gain[j], N_KV)]).reshape(1, QK_DIM)
           for j in range(n_attn)]
    zero_bias = jnp.zeros((1, D), F32)

    xs = x.reshape(t, D)
    saved, ready = [], {}

    def weight(b, l, after):
        g, a, k = where_is[(b, l)]
        if g not in ready:
            ssem, rsem, bufs = in_flight[g]
            ready[g] = gather_forward(gather_wait(ssem, rsem, bufs, gaxes[g], after, f"g{g}"), gaxes[g])
        return ready[g][a], k

    for i in range(DEPTH):
        j = i // 2
        gn, md = gain_full[i], mod[i]
        x0 = xs
        wup, k = weight(0, 2 * i, token if i == 0 else (xs,))
        wdn, _ = weight(1, 2 * i, ())
        xs, u1, f1 = ffn_fwd(x0, gn[0:1], md, wup, wdn, k, 0)
        x1 = xs
        wmi, k = weight(mixer(i)[0], j, (xs,))
        wmo, _ = weight(mixer(i)[1], j, ())
        if i % 2 == 0:
            raw, qr, kr = qkv_fwd(x1, gn[1:2], md, wmi, k, attn_b_qkv[j:j + 1], gqk[j], cos, sin, g1, g2)
            o, lse = attn_fwd(qr, kr, raw, attn_sinks[j:j + 1])
            xs, ymix = proj_res(x1, o, wmo, k, attn_b_o[j:j + 1], md)
            mix = (raw, qr, kr, o, lse)
        else:
            z = lin_fwd(x1, gn[1:2], md, wmi, k, 1)
            p = conv_fwd(z, convw_full[j])
            xs, ymix = proj_res(x1, p, wmo, k, zero_bias, md)
            mix = (z, p)
        x2 = xs
        wup, k = weight(0, 2 * i + 1, (xs,))
        wdn, _ = weight(1, 2 * i + 1, ())
        xs, u3, f3 = ffn_fwd(x2, gn[2:3], md, wup, wdn, k, 2)
        saved.append((x0, u1, f1, x1, mix, ymix, x2, u3, f3))
    dy, lpart = loss_grad(xs, loss_target.reshape(t, D))

    cc = ci.reshape(1).astype(jnp.int32)
    gshard = [lax.empty(s.shape, F32) for s in stacks]
    upd = [[lax.empty(s.shape, F32) for _ in range(3)] for s in stacks]
    mstacks = [_stack3(mom[n]) for n in BIG]
    vstacks = [_stack3(var[n]) for n in BIG]

    def finish(g, copies, after):
        ssem, rsem, pair, lands = copies
        pair, lands = scatter_wait(ssem, rsem, pair, lands, gaxes[g], after, f"g{g}")
        grp = groups[g]
        for a, (b, l0, n) in enumerate(grp):
            gshard[b] = sum_chips(lands[a], pair[a], gshard[b], l0, axes[b], where)
        joined = join_halves([gshard[b] for b, _, _ in grp], [(l0, n) for _, l0, n in grp], gaxes[g])
        for (b, l0, n), gj in zip(grp, joined):
            gshard[b] = gj
            upd[b] = adamw_layers(stacks[b], gj, mstacks[b], vstacks[b], upd[b], l0, n)

    ggrad = {g: [lax.empty(buf.shape, BF16) for buf in ready[g]] for g in range(len(groups))}
    missing = {g: sum(n for _, _, n in grp) for g, grp in enumerate(groups)}
    state = dict(flight=None, token=())

    def put(b, l, lhs, rhs, bm, bn):
        g, a, k = where_is[(b, l)]
        ggrad[g][a] = wgrad(ggrad[g][a], k, lhs, rhs, bm, bn)
        missing[g] -= 1
        if missing[g] == 0:
            theirs = exchange_halves(ggrad[g], gaxes[g])
            pair = [add_my_half(gr, th, ax, cc) for gr, th, ax in zip(ggrad[g], theirs, gaxes[g])]
            ssem, rsem, pair, lands, tok = scatter_start(pair, gaxes[g], f"g{g}")
            if state["flight"] is not None:
                finish(*state["flight"], (tok,))
            state.update(flight=(g, (ssem, rsem, pair, lands)), token=(tok,))

    dmod = [None] * DEPTH
    dgain = [None] * DEPTH
    db_qkv, dqk_gain, dsinks, db_o, dconv_w = ([None] * n_attn, [None] * n_attn, [None] * n_attn,
                                                [None] * n_attn, [None] * n_conv)
    for i in reversed(range(DEPTH)):
        j = i // 2
        gn, md = gain_full[i], mod[i]
        x0, u1, f1, x1, mix, ymix, x2, u3, f3 = saved[i]
        (wup, k), (wdn, _) = weight(0, 2 * i + 1, ()), weight(1, 2 * i + 1, ())
        dy, du, a, df, h, se3, sa3 = ffn_bwd(dy, u3, f3, x2, gn[2:3], md, wup, wdn, k, 2, after=state["token"])
        put(0, 2 * i + 1, h, du, D, D_FF)
        put(1, 2 * i + 1, a, df, D_FF // 2, D)
        (wmi, k), (wmo, _) = weight(mixer(i)[0], j, ()), weight(mixer(i)[1], j, ())
        if i % 2 == 0:
            raw, qr, kr, o, lse = mix
            dyy, do, sp, sb = proj_res_bwd(dy, ymix, wmo, k, md)
            put(mixer(i)[1], j, o, dyy, D, D)
            dq, dk, dv, dsinks[j] = attn_bwd(qr, kr, raw, attn_sinks[j:j + 1], o, do, lse)
            dz, dqk_gain[j] = qkv_bwd_pre(dq, dk, dv, raw, gqk[j], cos, sin, g1, g2, gsel)
            dy, h, se2, sa2, db_qkv[j] = lin_bwd(dy, dz, wmi, k, x1, gn[1:2], md, 1, True)
            put(mixer(i)[0], j, h, dz, D, QKV_DIM)
            db_o[j] = sb
        else:
            z, p = mix
            dyy, dp, sp, _ = proj_res_bwd(dy, ymix, wmo, k, md)
            put(mixer(i)[1], j, p, dyy, D, D)
            dz, dconv_w[j] = conv_bwd(z, convw_full[j], dp)
            dy, h, se2, sa2 = lin_bwd(dy, dz, wmi, k, x1, gn[1:2], md, 1, False)
            put(mixer(i)[0], j, h, dz, D, 1536)
        (wup, k), (wdn, _) = weight(0, 2 * i, ()), weight(1, 2 * i, ())
        dy, du, a, df, h, se1, sa1 = ffn_bwd(dy, u1, f1, x0, gn[0:1], md, wup, wdn, k, 0, after=state["token"])
        put(0, 2 * i, h, du, D, D_FF)
        put(1, 2 * i, a, df, D_FF // 2, D)
        dmod[i] = jnp.stack([se1[:, 0], se1[:, 1], se1[:, 2], se2[:, 0], se2[:, 1], sp[:, 0],
                             se3[:, 0], se3[:, 1], se3[:, 2]], axis=1)
        dgain[i] = jnp.stack([sa1[0], sa2[0], sa3[0]], axis=0)
    grad_x = dy.reshape(x.shape)

    dmod_ex = jnp.stack(dmod, axis=1).reshape(nex, DEPTH * 9, D)
    dmod_ex = jnp.pad(dmod_ex, ((0, 0), (0, N_DMOD_ROWS - DEPTH * 9), (0, 0))).reshape(nex * N_DMOD_ROWS, D)
    misc = jnp.concatenate([dqk_gain[jj][0] for jj in range(n_attn)]
                           + [jnp.pad(dsinks[jj][0], (0, 128 - N_HEADS)) for jj in range(n_attn)]
                           + [lpart[0]])
    rows = [dmod_ex,
            jnp.concatenate(dgain, axis=0), jnp.zeros((4, D), F32),
            jnp.concatenate([_pad_cols(db_qkv[jj][0:1], 2 * D).reshape(2, D) for jj in range(n_attn)], axis=0),
            jnp.concatenate([db_o[jj][0:1] for jj in range(n_attn)], axis=0),
            jnp.concatenate([dconv_w[jj][0:3] for jj in range(n_conv)], axis=0),
            jnp.pad(misc, (0, D - misc.shape[0])).reshape(1, D)]
    packed = jnp.concatenate(rows, axis=0)
    packed = jnp.pad(packed, ((0, N_SMALL_ROWS - packed.shape[0]), (0, 0)))
    p_all = allgather_small(packed, True)
    red, exsum = reduce_small(p_all)

    r0 = nex * N_DMOD_ROWS
    grads = {}
    grads["b_ada"] = exsum[:DEPTH * 9].reshape(DEPTH, 9 * D)
    grads["norm_gain"] = lax.dynamic_slice_in_dim(red[r0:r0 + 12].reshape(DEPTH, 3, D), chip * (D // N_CHIPS),
                                                  D // N_CHIPS, axis=2)
    r1 = r0 + 16
    grads["attn_b_qkv"] = red[r1:r1 + 2 * n_attn].reshape(n_attn, 2 * D)[:, :QKV_DIM]
    r2 = r1 + 2 * n_attn
    grads["attn_b_o"] = red[r2:r2 + n_attn]
    r3 = r2 + n_attn
    grads["conv_w"] = lax.dynamic_slice_in_dim(red[r3:r3 + 3 * n_conv].reshape(n_conv, 3, D), chip * (D // N_CHIPS),
                                               D // N_CHIPS, axis=2)
    mrow = red[r3 + 3 * n_conv]
    grads["attn_q_gain"] = jnp.stack([mrow[128 * jj:128 * jj + HEAD_DIM] for jj in range(n_attn)])
    grads["attn_k_gain"] = jnp.stack([mrow[128 * jj + HEAD_DIM:128 * jj + 128] for jj in range(n_attn)])
    grads["attn_sinks"] = jnp.stack([mrow[128 * (n_attn + jj):128 * (n_attn + jj) + N_HEADS] for jj in range(n_attn)])
    loss = mrow[128 * 2 * n_attn]

    dm_all = p_all[:, :r0].reshape(8, nex, N_DMOD_ROWS, D)[:, :, :DEPTH * 9].reshape(8 * nex, DEPTH, 9 * D)
    dm_mine = lax.dynamic_slice_in_dim(dm_all.transpose(1, 0, 2), chip * ada_cols, ada_cols, axis=2)
    g_ada, d_ada, nm_ada, nv_ada = ada_bwd_adam(c_all, dm_mine, w_ada, m_w_ada, v_w_ada)

    delta, new_m, new_v = {}, {}, {}
    for n in WEIGHTS:
        if n == "w_ada":
            grads[n], delta[n], new_m[n], new_v[n] = g_ada, d_ada, nm_ada, nv_ada
        elif n not in BIG:
            delta[n], new_m[n], new_v[n] = adamw(w[n], grads[n], mom[n], var[n])
    finish(*state["flight"], (delta["conv_w"], d_ada))
    for b, n in enumerate(BIG):
        grads[n] = gshard[b].reshape(w[n].shape)
        delta[n], new_m[n], new_v[n] = (u.reshape(w[n].shape) for u in upd[b])

    return (loss, grad_x, *[grads[n] for n in WEIGHTS], *[delta[n] for n in WEIGHTS],
            *[new_m[n] for n in WEIGHTS], *[new_v[n] for n in WEIGHTS])
```

```python
import functools

import jax
import jax.numpy as jnp
from jax import lax
from jax.experimental import pallas as pl
from jax.experimental.pallas import tpu as pltpu

F32 = jnp.float32
BF16 = jnp.bfloat16

D = 1024
D_FF = 2816
N_HEADS = 16
N_KV = 4
HEAD_DIM = 64
GROUP = N_HEADS // N_KV
QK_DIM = (N_HEADS + N_KV) * HEAD_DIM
QKV_DIM = QK_DIM + N_KV * HEAD_DIM
BLOCK = 128
ROPE_THETA = 10000.0
EPS = 1e-6
DEPTH = 4
N_CHIPS = 4

ADAM_LR = 0.001
ADAM_B1 = 0.9
ADAM_B2 = 0.999
ADAM_EPS = 1e-08
ADAM_WD = 0.01
ADAM_STEP = 10

V7X_VMEM_BYTES = 64 * 1024 * 1024
V7X_MXU_DIM = 256
FF_CHUNKS = ((0, 1536), (1536, D_FF))
assert all((hi - lo) % V7X_MXU_DIM == 0 for lo, hi in FF_CHUNKS)
MESH = pl.DeviceIdType.MESH
ANY = pl.BlockSpec(memory_space=pl.ANY)


def _cparams(vmem_mb, n_grid):
    assert vmem_mb * 1024 * 1024 <= V7X_VMEM_BYTES
    return pltpu.CompilerParams(vmem_limit_bytes=vmem_mb * 1024 * 1024,
                                dimension_semantics=("arbitrary",) * n_grid)


def _resident(shape):
    nd = len(shape)
    return pl.BlockSpec(shape, lambda *_: (0,) * nd, pipeline_mode=pl.Buffered(1))


def _layer(w, l):
    return pl.BlockSpec((None,) + w.shape[1:], lambda *_: (l, 0, 0), pipeline_mode=pl.Buffered(1))


PIN_BYTES = 1 << 20


def _pin(*args):
    return [pltpu.with_memory_space_constraint(a, pltpu.HBM) if a.size * a.dtype.itemsize >= PIN_BYTES else a
            for a in args]


def _mid(shape, dtype):
    n = 1
    for d in shape:
        n *= d
    if n * jnp.dtype(dtype).itemsize >= PIN_BYTES:
        return pltpu.HBM(tuple(shape), dtype)
    return jax.ShapeDtypeStruct(tuple(shape), dtype)


def _dot(a, b):
    return jnp.dot(a, b, preferred_element_type=F32)


def _dot_nt(a, b):
    return lax.dot_general(a, b, (((1,), (1,)), ((), ())), preferred_element_type=F32)


def _dot_tn(a, b):
    return lax.dot_general(a, b, (((0,), (0,)), ((), ())), preferred_element_type=F32)


def _dot_hilo(a, g):
    hi = a.astype(BF16)
    lo = (a - hi.astype(F32)).astype(BF16)
    return _dot(hi, g) + _dot(lo, g)


def _sigmoid(x):
    return 0.5 * jnp.tanh(0.5 * x) + 0.5


def _colsum(a):
    return jnp.sum(a, axis=0, keepdims=True)


def _norm_mod(x, gain, sc, sh):
    r = lax.rsqrt(jnp.mean(x * x, axis=-1, keepdims=True) + EPS)
    n = x * r * gain
    return r, n, n * (1.0 + sc) + sh


def _mod_rows(mod_ref, s):
    return (mod_ref[0, 3 * s:3 * s + 1, :], mod_ref[0, 3 * s + 1:3 * s + 2, :], mod_ref[0, 3 * s + 2:3 * s + 3, :])


def ffn_fwd(x, gain, mod, wup, wdn, l, s, tm=512, after=()):
    t = x.shape[0]
    tpe = t // tm // 2

    def body(x_ref, gain_ref, mod_ref, wup_ref, wdn_ref, *rest):
        xo_ref, u_ref, f_ref = rest[len(after):]
        xv = x_ref[...]
        sh, sc, g = _mod_rows(mod_ref, s)
        _, _, h = _norm_mod(xv, gain_ref[...], sc, sh)
        hb = h.astype(BF16)
        acc = jnp.zeros((tm, D), F32)
        for lo, hi in FF_CHUNKS:
            gate = _dot(hb, wup_ref[:, lo:hi])
            up = _dot(hb, wup_ref[:, D_FF + lo:D_FF + hi])
            u_ref[:, lo:hi] = gate.astype(BF16)
            u_ref[:, D_FF + lo:D_FF + hi] = up.astype(BF16)
            a = (gate * _sigmoid(gate) * up).astype(BF16)
            acc = acc + _dot(a, wdn_ref[lo:hi, :])
        f_ref[...] = acc.astype(BF16)
        xo_ref[...] = xv + 0.5 * g * acc

    return pl.pallas_call(
        body, name="ffn_fwd", grid=(t // tm,),
        in_specs=[pl.BlockSpec((tm, D), lambda i: (i, 0)),
                  pl.BlockSpec((1, D), lambda i: (0, 0)),
                  pl.BlockSpec((1, 9, D), lambda i: (i // tpe, 0, 0)),
                  _layer(wup, l), _layer(wdn, l)] + [ANY] * len(after),
        out_specs=[pl.BlockSpec((tm, D), lambda i: (i, 0)),
                   pl.BlockSpec((tm, 2 * D_FF), lambda i: (i, 0)),
                   pl.BlockSpec((tm, D), lambda i: (i, 0))],
        out_shape=[_mid((t, D), F32), _mid((t, 2 * D_FF), BF16), _mid((t, D), BF16)],
        compiler_params=_cparams(60, 1),
    )(*_pin(x, gain, mod, wup, wdn, *after))


def ffn_bwd(dy, u, f, x, gain, mod, wup, wdn, l, s, tm=256, after=()):
    t = dy.shape[0]
    tpe = t // tm // 2

    def body(dy_ref, u_ref, f_ref, x_ref, gain_ref, mod_ref, wup_ref, wdn_ref, *rest):
        dx_ref, du_ref, a_ref, df_ref, h_ref, se_ref, sa_ref = rest[len(after):]
        i = pl.program_id(0)
        dyv, xv, gain_v = dy_ref[...], x_ref[...], gain_ref[...]
        sh, sc, g = _mod_rows(mod_ref, s)
        r, nrm, h = _norm_mod(xv, gain_v, sc, sh)
        h_ref[...] = h.astype(BF16)
        dfb = (0.5 * g * dyv).astype(BF16)
        df_ref[...] = dfb
        dh = jnp.zeros((tm, D), F32)
        for lo, hi in FF_CHUNKS:
            da = _dot_nt(dfb, wdn_ref[lo:hi, :])
            gate = u_ref[:, lo:hi].astype(F32)
            up = u_ref[:, D_FF + lo:D_FF + hi].astype(F32)
            sg = _sigmoid(gate)
            silu = gate * sg
            a_ref[:, lo:hi] = (silu * up).astype(BF16)
            dgate = (da * up * (sg + silu * (1.0 - sg))).astype(BF16)
            dup = (da * silu).astype(BF16)
            du_ref[:, lo:hi] = dgate
            du_ref[:, D_FF + lo:D_FF + hi] = dup
            dh = dh + _dot_nt(dgate, wup_ref[:, lo:hi]) + _dot_nt(dup, wup_ref[:, D_FF + lo:D_FF + hi])
        dn = dh * (1.0 + sc)
        dxr = dn * gain_v
        m = jnp.mean(dxr * xv, axis=-1, keepdims=True)
        dx_ref[...] = dyv + r * dxr - xv * (r * r * r) * m

        @pl.when(i % tpe == 0)
        def _():
            se_ref[...] = jnp.zeros_like(se_ref)

        @pl.when(i == 0)
        def _():
            sa_ref[...] = jnp.zeros_like(sa_ref)

        se_ref[0, 0:1, :] += _colsum(dh)
        se_ref[0, 1:2, :] += _colsum(dh * nrm)
        se_ref[0, 2:3, :] += _colsum(0.5 * dyv * f_ref[...].astype(F32))
        sa_ref[0:1, :] += _colsum(dn * xv * r)

    row = lambda w_: pl.BlockSpec((tm, w_), lambda i: (i, 0))
    return pl.pallas_call(
        body, name="ffn_bwd", grid=(t // tm,),
        in_specs=[row(D), row(2 * D_FF), row(D), row(D),
                  pl.BlockSpec((1, D), lambda i: (0, 0)),
                  pl.BlockSpec((1, 9, D), lambda i: (i // tpe, 0, 0)),
                  _layer(wup, l), _layer(wdn, l)] + [ANY] * len(after),
        out_specs=[row(D), row(2 * D_FF), row(D_FF), row(D), row(D),
                   pl.BlockSpec((1, 8, D), lambda i: (i // tpe, 0, 0)), pl.BlockSpec((8, D), lambda i: (0, 0))],
        out_shape=[_mid((t, D), F32), _mid((t, 2 * D_FF), BF16), _mid((t, D_FF), BF16), _mid((t, D), BF16),
                   _mid((t, D), BF16), _mid((2, 8, D), F32), _mid((8, D), F32)],
        compiler_params=_cparams(60, 1),
    )(*_pin(dy, u, f, x, gain, mod, wup, wdn, *after))


def lin_bwd(dy, dz, w, l, x, gain, mod, s, want_db, tm=512):
    t = dy.shape[0]
    n = w.shape[2]
    tpe = t // tm // 2

    def body(dy_ref, dz_ref, w_ref, x_ref, gain_ref, mod_ref, dx_ref, h_ref, se_ref, sa_ref, *db_ref):
        i = pl.program_id(0)
        xv = x_ref[...]
        gain_v = gain_ref[...]
        sh, sc, _ = _mod_rows(mod_ref, s)
        r, nrm, h = _norm_mod(xv, gain_v, sc, sh)
        h_ref[...] = h.astype(BF16)
        dh = _dot_nt(dz_ref[...], w_ref[...])
        dn = dh * (1.0 + sc)
        dxr = dn * gain_v
        m = jnp.mean(dxr * xv, axis=-1, keepdims=True)
        dx_ref[...] = dy_ref[...] + r * dxr - xv * (r * r * r) * m

        @pl.when(i % tpe == 0)
        def _():
            se_ref[...] = jnp.zeros_like(se_ref)

        @pl.when(i == 0)
        def _():
            sa_ref[...] = jnp.zeros_like(sa_ref)
            if want_db:
                db_ref[0][...] = jnp.zeros_like(db_ref[0])

        se_ref[0, 0:1, :] += _colsum(dh)
        se_ref[0, 1:2, :] += _colsum(dh * nrm)
        sa_ref[0:1, :] += _colsum(dn * xv * r)
        if want_db:
            db_ref[0][0:1, :] += _colsum(dz_ref[...].astype(F32))

    out_specs = [pl.BlockSpec((tm, D), lambda i: (i, 0)), pl.BlockSpec((tm, D), lambda i: (i, 0)),
                 pl.BlockSpec((1, 8, D), lambda i: (i // tpe, 0, 0)), pl.BlockSpec((8, D), lambda i: (0, 0))]
    out_shape = [_mid((t, D), F32), _mid((t, D), BF16),
                 jax.ShapeDtypeStruct((2, 8, D), F32), jax.ShapeDtypeStruct((8, D), F32)]
    if want_db:
        out_specs.append(pl.BlockSpec((8, n), lambda i: (0, 0)))
        out_shape.append(jax.ShapeDtypeStruct((8, n), F32))
    return pl.pallas_call(
        body, name="lin_bwd", grid=(t // tm,),
        in_specs=[pl.BlockSpec((tm, D), lambda i: (i, 0)),
                  pl.BlockSpec((tm, n), lambda i: (i, 0)),
                  _layer(w, l),
                  pl.BlockSpec((tm, D), lambda i: (i, 0)),
                  pl.BlockSpec((1, D), lambda i: (0, 0)),
                  pl.BlockSpec((1, 9, D), lambda i: (i // tpe, 0, 0))],
        out_specs=out_specs, out_shape=out_shape,
        compiler_params=_cparams(56, 1),
    )(*_pin(dy, dz, w, x, gain, mod))


def wgrad(gstack, l, a, b, bm, bn, bt=1024, after=()):
    t, m = a.shape
    n = b.shape[1]
    nt = t // bt

    def body(g_ref, a_ref, b_ref, *rest):
        o_ref, acc_ref = rest[len(after):]
        k = pl.program_id(2)

        @pl.when(k == 0)
        def _():
            acc_ref[...] = jnp.zeros_like(acc_ref)

        acc_ref[...] += _dot_tn(a_ref[...], b_ref[...])

        @pl.when(k == nt - 1)
        def _():
            o_ref[...] = acc_ref[...].astype(BF16)

    return pl.pallas_call(
        body, name="wgrad", grid=(m // bm, n // bn, nt),
        in_specs=[ANY, pl.BlockSpec((bt, bm), lambda i, j, k: (k, i)),
                  pl.BlockSpec((bt, bn), lambda i, j, k: (k, j))] + [ANY] * len(after),
        out_specs=pl.BlockSpec((None, bm, bn), lambda i, j, k: (l, i, j)),
        out_shape=_mid(gstack.shape, BF16),
        input_output_aliases={0: 0},
        scratch_shapes=[pltpu.VMEM((bm, bn), F32)],
        compiler_params=_cparams(56, 3),
    )(*_pin(gstack, a, b, *after))


def proj_res(x, o, w, l, b, mod, tm=512):
    t = x.shape[0]
    tpe = t // tm // 2

    def body(x_ref, o_ref, w_ref, b_ref, mod_ref, xo_ref, y_ref):
        _, _, g = _mod_rows(mod_ref, 1)
        y = _dot(o_ref[...], w_ref[...]) + b_ref[...]
        y_ref[...] = y.astype(BF16)
        xo_ref[...] = x_ref[...] + g * y

    return pl.pallas_call(
        body, name="proj_res", grid=(t // tm,),
        in_specs=[pl.BlockSpec((tm, D), lambda i: (i, 0)), pl.BlockSpec((tm, D), lambda i: (i, 0)),
                  _layer(w, l), pl.BlockSpec((1, D), lambda i: (0, 0)),
                  pl.BlockSpec((1, 9, D), lambda i: (i // tpe, 0, 0))],
        out_specs=[pl.BlockSpec((tm, D), lambda i: (i, 0)), pl.BlockSpec((tm, D), lambda i: (i, 0))],
        out_shape=[_mid((t, D), F32), _mid((t, D), BF16)],
        compiler_params=_cparams(32, 1),
    )(*_pin(x, o, w, b, mod))


def proj_res_bwd(dy, y, w, l, mod, tm=512):
    t = dy.shape[0]
    tpe = t // tm // 2

    def body(dy_ref, y_ref, w_ref, mod_ref, dyy_ref, do_ref, se_ref, sa_ref):
        i = pl.program_id(0)
        _, _, g = _mod_rows(mod_ref, 1)
        dyv = dy_ref[...]
        dyy = g * dyv
        dyb = dyy.astype(BF16)
        dyy_ref[...] = dyb
        do_ref[...] = _dot_nt(dyb, w_ref[...]).astype(BF16)

        @pl.when(i % tpe == 0)
        def _():
            se_ref[...] = jnp.zeros_like(se_ref)

        @pl.when(i == 0)
        def _():
            sa_ref[...] = jnp.zeros_like(sa_ref)

        se_ref[0, 0:1, :] += _colsum(dyv * y_ref[...].astype(F32))
        sa_ref[0:1, :] += _colsum(dyy)

    return pl.pallas_call(
        body, name="proj_res_bwd", grid=(t // tm,),
        in_specs=[pl.BlockSpec((tm, D), lambda i: (i, 0)), pl.BlockSpec((tm, D), lambda i: (i, 0)),
                  _layer(w, l), pl.BlockSpec((1, 9, D), lambda i: (i // tpe, 0, 0))],
        out_specs=[pl.BlockSpec((tm, D), lambda i: (i, 0)), pl.BlockSpec((tm, D), lambda i: (i, 0)),
                   pl.BlockSpec((1, 8, D), lambda i: (i // tpe, 0, 0)), pl.BlockSpec((8, D), lambda i: (0, 0))],
        out_shape=[_mid((t, D), BF16), _mid((t, D), BF16), _mid((2, 8, D), F32), _mid((8, D), F32)],
        compiler_params=_cparams(32, 1),
    )(*_pin(dy, y, w, mod))


def lin_fwd(x, gain, mod, w, l, s, tm=512):
    t = x.shape[0]
    n = w.shape[2]
    tpe = t // tm // 2

    def body(x_ref, gain_ref, mod_ref, w_ref, z_ref):
        sh, sc, _ = _mod_rows(mod_ref, s)
        _, _, h = _norm_mod(x_ref[...], gain_ref[...], sc, sh)
        z_ref[...] = _dot(h.astype(BF16), w_ref[...]).astype(BF16)

    return pl.pallas_call(
        body, name="lin_fwd", grid=(t // tm,),
        in_specs=[pl.BlockSpec((tm, D), lambda i: (i, 0)), pl.BlockSpec((1, D), lambda i: (0, 0)),
                  pl.BlockSpec((1, 9, D), lambda i: (i // tpe, 0, 0)), _layer(w, l)],
        out_specs=pl.BlockSpec((tm, n), lambda i: (i, 0)),
        out_shape=_mid((t, n), BF16),
        compiler_params=_cparams(40, 1),
    )(*_pin(x, gain, mod, w))


def rope_tables(pos, invf):
    t = pos.shape[0]
    tm = 1024

    def body(pos_ref, invf_ref, c_ref, s_ref):
        ang = pos_ref[...].astype(F32) * invf_ref[...]
        lane = lax.broadcasted_iota(jnp.int32, (tm, 128), 1)
        sign = jnp.where(lane % HEAD_DIM < HEAD_DIM // 2, -1.0, 1.0)
        c_ref[...] = jnp.cos(ang)
        s_ref[...] = sign * jnp.sin(ang)

    return pl.pallas_call(
        body, name="rope_tables", grid=(t // tm,),
        in_specs=[pl.BlockSpec((tm, 1), lambda i: (i, 0)), pl.BlockSpec((1, 128), lambda i: (0, 0))],
        out_specs=[pl.BlockSpec((tm, 128), lambda i: (i, 0))] * 2,
        out_shape=[jax.ShapeDtypeStruct((t, 128), F32)] * 2,
        compiler_params=_cparams(16, 1),
    )(pos, invf)


def _swap_halves(v):
    lane = lax.broadcasted_iota(jnp.int32, v.shape, 1)
    return jnp.where(lane % HEAD_DIM < HEAD_DIM // 2, pltpu.roll(v, 128 - HEAD_DIM // 2, 1), pltpu.roll(v, HEAD_DIM // 2, 1))


def _rope(v, cos, sin):
    return jnp.concatenate(
        [v[:, j:j + 128] * cos + _swap_halves(v[:, j:j + 128]) * sin for j in range(0, v.shape[1], 128)], axis=1)


def _rope_t(dv, cos, sin):
    return jnp.concatenate(
        [dv[:, j:j + 128] * cos + _swap_halves(dv[:, j:j + 128] * sin) for j in range(0, dv.shape[1], 128)], axis=1)


def _head_stats(qk, g1, g2):
    rinv = lax.rsqrt(_dot_hilo(qk * qk, g1) + EPS)
    return rinv, _dot_hilo(rinv, g2)


def qkv_fwd(x, gain, mod, w, l, b, gqk, cos, sin, g1, g2, tm=256):
    t = x.shape[0]
    tpe = t // tm // 2

    def body(x_ref, gain_ref, mod_ref, w_ref, b_ref, gqk_ref, c_ref, s_ref, g1_ref, g2_ref, raw_ref, q_ref, k_ref):
        sh, sc, _ = _mod_rows(mod_ref, 1)
        _, _, h = _norm_mod(x_ref[...], gain_ref[...], sc, sh)
        qkv = _dot(h.astype(BF16), w_ref[...]) + b_ref[...]
        raw_ref[...] = qkv.astype(BF16)
        qk = qkv[:, :QK_DIM]
        _, rb = _head_stats(qk, g1_ref[...], g2_ref[...])
        qr = _rope(qk * rb * gqk_ref[...], c_ref[...], s_ref[...])
        q_ref[...] = qr[:, :D].astype(BF16)
        k_ref[...] = qr[:, D:].astype(BF16)

    return pl.pallas_call(
        body, name="qkv_fwd", grid=(t // tm,),
        in_specs=[pl.BlockSpec((tm, D), lambda i: (i, 0)), pl.BlockSpec((1, D), lambda i: (0, 0)),
                  pl.BlockSpec((1, 9, D), lambda i: (i // tpe, 0, 0)), _layer(w, l),
                  pl.BlockSpec((1, QKV_DIM), lambda i: (0, 0)), pl.BlockSpec((1, QK_DIM), lambda i: (0, 0)),
                  pl.BlockSpec((tm, 128), lambda i: (i, 0)), pl.BlockSpec((tm, 128), lambda i: (i, 0)),
                  _resident((QK_DIM, 128)), _resident((128, QK_DIM))],
        out_specs=[pl.BlockSpec((tm, QKV_DIM), lambda i: (i, 0)), pl.BlockSpec((tm, D), lambda i: (i, 0)),
                   pl.BlockSpec((tm, N_KV * HEAD_DIM), lambda i: (i, 0))],
        out_shape=[_mid((t, QKV_DIM), BF16), _mid((t, D), BF16), _mid((t, N_KV * HEAD_DIM), BF16)],
        compiler_params=_cparams(40, 1),
    )(*_pin(x, gain, mod, w, b, gqk, cos, sin, g1, g2))


def qkv_bwd_pre(dq, dk, dv, raw, gqk, cos, sin, g1, g2, gsel, tm=256):
    t = dq.shape[0]

    def body(dq_ref, dk_ref, dv_ref, raw_ref, gqk_ref, c_ref, s_ref, g1_ref, g2_ref, gsel_ref, dz_ref, sa_ref):
        i = pl.program_id(0)
        dqk = jnp.concatenate([dq_ref[...].astype(F32), dk_ref[...]], axis=1)
        dqn = _rope_t(dqk, c_ref[...], s_ref[...])
        qk = raw_ref[:, :QK_DIM].astype(F32)
        g1v, g2v = g1_ref[...], g2_ref[...]
        rinv, rb = _head_stats(qk, g1v, g2v)
        dgq = jnp.broadcast_to(_colsum(dqn * qk * rb), (8, QK_DIM))
        dyh = dqn * gqk_ref[...]
        mh = _dot_hilo(dyh * qk, g1v)
        mb = _dot_hilo(mh * rinv * rinv * rinv, g2v)
        dz_ref[:, :QK_DIM] = (rb * dyh - qk * mb).astype(BF16)
        dz_ref[:, QK_DIM:] = dv_ref[...].astype(BF16)

        @pl.when(i == 0)
        def _():
            sa_ref[...] = jnp.zeros_like(sa_ref)

        sa_ref[...] += _dot_hilo(dgq, gsel_ref[...])

    kvw = N_KV * HEAD_DIM
    return pl.pallas_call(
        body, name="qkv_bwd_pre", grid=(t // tm,),
        in_specs=[pl.BlockSpec((tm, D), lambda i: (i, 0)), pl.BlockSpec((tm, kvw), lambda i: (i, 0)),
                  pl.BlockSpec((tm, kvw), lambda i: (i, 0)), pl.BlockSpec((tm, QKV_DIM), lambda i: (i, 0)),
                  pl.BlockSpec((1, QK_DIM), lambda i: (0, 0)),
                  pl.BlockSpec((tm, 128), lambda i: (i, 0)), pl.BlockSpec((tm, 128), lambda i: (i, 0)),
                  _resident((QK_DIM, 128)), _resident((128, QK_DIM)), _resident((QK_DIM, 128))],
        out_specs=[pl.BlockSpec((tm, QKV_DIM), lambda i: (i, 0)), pl.BlockSpec((8, 128), lambda i: (0, 0))],
        out_shape=[_mid((t, QKV_DIM), BF16), _mid((8, 128), F32)],
        compiler_params=_cparams(40, 1),
    )(*_pin(dq, dk, dv, raw, gqk, cos, sin, g1, g2, gsel))


def _band_mask(n):
    row = lax.broadcasted_iota(jnp.int32, (GROUP * BLOCK, 2 * BLOCK), 0) % BLOCK
    col = lax.broadcasted_iota(jnp.int32, (GROUP * BLOCK, 2 * BLOCK), 1)
    rel = row + BLOCK - col
    return (rel >= 0) & (rel < BLOCK) & ((col >= BLOCK) | (n > 0))


def _stack_heads(v, g):
    base = g * GROUP * HEAD_DIM
    return jnp.concatenate([v[:, base + j * HEAD_DIM:base + (j + 1) * HEAD_DIM] for j in range(GROUP)], axis=0)


def _kv_cat(prev, cur, g):
    return jnp.concatenate([prev[:, g * HEAD_DIM:(g + 1) * HEAD_DIM], cur[:, g * HEAD_DIM:(g + 1) * HEAD_DIM]], axis=0)


def _sink_col(sink_ref, g):
    return jnp.concatenate([jnp.full((BLOCK, 1), sink_ref[0, g * GROUP + j], F32) for j in range(GROUP)], axis=0)


def _attn_specs(nb):
    kvw = N_KV * HEAD_DIM
    vcol = QK_DIM // kvw
    cur = lambda e, n: (e * nb + n, 0)
    prev = lambda e, n: (e * nb + jnp.maximum(n - 1, 0), 0)
    return [pl.BlockSpec((BLOCK, D), cur),
            pl.BlockSpec((BLOCK, kvw), cur), pl.BlockSpec((BLOCK, kvw), prev),
            pl.BlockSpec((BLOCK, kvw), lambda e, n: (e * nb + n, vcol)),
            pl.BlockSpec((BLOCK, kvw), lambda e, n: (e * nb + jnp.maximum(n - 1, 0), vcol)),
            pl.BlockSpec(memory_space=pltpu.SMEM)]


def attn_fwd(q, k, raw, sinks):
    t = q.shape[0]
    nb = t // 2 // BLOCK

    def body(q_ref, kc_ref, kp_ref, vc_ref, vp_ref, sink_ref, o_ref, lse_ref):
        n = pl.program_id(1)
        qv = q_ref[...]
        kc, kp, vc, vp = kc_ref[...], kp_ref[...], vc_ref[...], vp_ref[...]
        mask = _band_mask(n)
        outs, lses = [], []
        for g in range(N_KV):
            kk, vv = _kv_cat(kp, kc, g), _kv_cat(vp, vc, g)
            s = jnp.where(mask, _dot_nt(_stack_heads(qv, g), kk) * (HEAD_DIM ** -0.5), -1e30)
            sink = _sink_col(sink_ref, g)
            m = jnp.maximum(jnp.max(s, axis=1, keepdims=True), sink)
            p = jnp.exp(s - m)
            l = jnp.sum(p, axis=1, keepdims=True) + jnp.exp(sink - m)
            o = _dot(p.astype(BF16), vv) / l
            lse = m + jnp.log(l)
            for j in range(GROUP):
                outs.append(o[j * BLOCK:(j + 1) * BLOCK, :])
                lses.append(lse[j * BLOCK:(j + 1) * BLOCK, :])
        o_ref[...] = jnp.concatenate(outs, axis=1).astype(BF16)
        lse_ref[...] = jnp.concatenate(lses, axis=1)

    cur = lambda e, n: (e * nb + n, 0)
    return pl.pallas_call(
        body, name="attn_fwd", grid=(2, nb),
        in_specs=_attn_specs(nb),
        out_specs=[pl.BlockSpec((BLOCK, D), cur), pl.BlockSpec((BLOCK, N_HEADS), cur)],
        out_shape=[_mid((t, D), BF16), _mid((t, N_HEADS), F32)],
        compiler_params=_cparams(32, 2),
    )(*_pin(q, k, k, raw, raw), sinks)


def attn_bwd(q, k, raw, sinks, o, do, lse):
    t = q.shape[0]
    s_len = t // 2
    nb = s_len // BLOCK
    kvw = N_KV * HEAD_DIM

    def body(q_ref, kc_ref, kp_ref, vc_ref, vp_ref, sink_ref, o_ref, do_ref, lse_ref, dq_ref, dk_ref, dv_ref, ds_ref):
        n = pl.program_id(1)

        @pl.when(n == 0)
        def _():
            dk_ref[...] = jnp.zeros_like(dk_ref)
            dv_ref[...] = jnp.zeros_like(dv_ref)

        @pl.when((n == 0) & (pl.program_id(0) == 0))
        def _():
            ds_ref[...] = jnp.zeros_like(ds_ref)

        qv, ov, dov, lsev = q_ref[...], o_ref[...], do_ref[...], lse_ref[...]
        kc, kp, vc, vp = kc_ref[...], kp_ref[...], vc_ref[...], vp_ref[...]
        mask = _band_mask(n)
        dqs, dks, dvs, dsk = [], [], [], []
        for g in range(N_KV):
            kk, vv = _kv_cat(kp, kc, g), _kv_cat(vp, vc, g)
            qg, og, dog = _stack_heads(qv, g), _stack_heads(ov, g), _stack_heads(dov, g)
            lse = jnp.concatenate([lsev[:, g * GROUP + j:g * GROUP + j + 1] for j in range(GROUP)], axis=0)
            s = jnp.where(mask, _dot_nt(qg, kk) * (HEAD_DIM ** -0.5), -1e30)
            p = jnp.exp(s - lse)
            dd = _dot_hilo(dog.astype(F32) * og.astype(F32), jnp.ones((HEAD_DIM, 128), BF16))[:, :1]
            ds = (p * (_dot_nt(dog, vv) - dd) * (HEAD_DIM ** -0.5)).astype(BF16)
            dqg = _dot(ds, kk)
            dks.append(_dot_tn(ds, qg))
            dvs.append(_dot_tn(p.astype(BF16), dog))
            wsink = jnp.exp(_sink_col(sink_ref, g) - lse) * dd
            for j in range(GROUP):
                dqs.append(dqg[j * BLOCK:(j + 1) * BLOCK, :])
                dsk.append(wsink[j * BLOCK:(j + 1) * BLOCK, :])
        dq_ref[...] = jnp.concatenate(dqs, axis=1).astype(BF16)
        dkk = jnp.concatenate(dks, axis=1)
        dvv = jnp.concatenate(dvs, axis=1)
        prev0 = pl.multiple_of(jnp.maximum(n - 1, 0) * BLOCK, BLOCK)
        cur0 = pl.multiple_of(n * BLOCK, BLOCK)
        dk_ref[pl.ds(prev0, BLOCK), :] += dkk[:BLOCK]
        dv_ref[pl.ds(prev0, BLOCK), :] += dvv[:BLOCK]
        dk_ref[pl.ds(cur0, BLOCK), :] += dkk[BLOCK:]
        dv_ref[pl.ds(cur0, BLOCK), :] += dvv[BLOCK:]
        ds_ref[0:1, :] -= _colsum(jnp.concatenate(dsk, axis=1))

    cur = lambda e, n: (e * nb + n, 0)
    return pl.pallas_call(
        body, name="attn_bwd", grid=(2, nb),
        in_specs=_attn_specs(nb) + [pl.BlockSpec((BLOCK, D), cur), pl.BlockSpec((BLOCK, D), cur),
                                    pl.BlockSpec((BLOCK, N_HEADS), cur)],
        out_specs=[pl.BlockSpec((BLOCK, D), cur), pl.BlockSpec((s_len, kvw), lambda e, n: (e, 0)),
                   pl.BlockSpec((s_len, kvw), lambda e, n: (e, 0)), pl.BlockSpec((8, N_HEADS), lambda e, n: (0, 0))],
        out_shape=[_mid((t, D), BF16), _mid((t, kvw), F32), _mid((t, kvw), F32), _mid((8, N_HEADS), F32)],
        compiler_params=_cparams(32, 2),
    )(*_pin(q, k, k, raw, raw), sinks, *_pin(o, do, lse))


CONV_COLS = 256


def _conv_specs(s_len):
    nct = D // CONV_COLS
    return [pl.BlockSpec((s_len, CONV_COLS), lambda j, e, *_: (e, j)),
            pl.BlockSpec((s_len, CONV_COLS), lambda j, e, *_: (e, nct + j)),
            pl.BlockSpec((s_len, CONV_COLS), lambda j, e, *_: (e, 2 * nct + j)),
            pl.BlockSpec((3, CONV_COLS), lambda j, e, *_: (0, j))]


def _conv_taps(gc, v, w, s_len):
    u = gc * v
    row = lax.broadcasted_iota(jnp.int32, u.shape, 0)
    u1 = jnp.where(row >= 1, pltpu.roll(u, 1, 0), 0.0)
    u2 = jnp.where(row >= 2, pltpu.roll(u, 2, 0), 0.0)
    return u, u1, u2, w[2:3, :] * u + w[1:2, :] * u1 + w[0:1, :] * u2


def conv_fwd(z, w):
    t = z.shape[0]
    s_len = t // 2

    def body(gb_ref, gc_ref, v_ref, w_ref, p_ref):
        _, _, _, conv = _conv_taps(gc_ref[...].astype(F32), v_ref[...].astype(F32), w_ref[...], s_len)
        p_ref[...] = (gb_ref[...].astype(F32) * conv).astype(BF16)

    return pl.pallas_call(
        body, name="conv_fwd", grid=(D // CONV_COLS, 2),
        in_specs=_conv_specs(s_len),
        out_specs=pl.BlockSpec((s_len, CONV_COLS), lambda j, e: (e, j)),
        out_shape=_mid((t, D), BF16),
        compiler_params=_cparams(40, 2),
    )(*_pin(z, z, z, w))


def conv_bwd(z, w, dp):
    t = z.shape[0]
    s_len = t // 2
    nct = D // CONV_COLS

    def body(gb_ref, gc_ref, v_ref, w_ref, dp_ref, dz_ref, dw_ref, parts_ref):
        e, part = pl.program_id(1), pl.program_id(2)

        @pl.when(part == 0)
        def _():
            gc, v, wv = gc_ref[...].astype(F32), v_ref[...].astype(F32), w_ref[...]
            u, u1, u2, conv = _conv_taps(gc, v, wv, s_len)
            dpv = dp_ref[...].astype(F32)
            parts_ref[0] = (dpv * conv).astype(BF16)
            dc = dpv * gb_ref[...].astype(F32)
            row = lax.broadcasted_iota(jnp.int32, dc.shape, 0)
            dc1 = jnp.where(row <= s_len - 2, pltpu.roll(dc, s_len - 1, 0), 0.0)
            dc2 = jnp.where(row <= s_len - 3, pltpu.roll(dc, s_len - 2, 0), 0.0)
            du = wv[2:3, :] * dc + wv[1:2, :] * dc1 + wv[0:1, :] * dc2
            parts_ref[1] = (du * v).astype(BF16)
            parts_ref[2] = (du * gc).astype(BF16)

            @pl.when(e == 0)
            def _():
                dw_ref[...] = jnp.zeros_like(dw_ref)

            dw_ref[0:1, :] += _colsum(dc * u2)
            dw_ref[1:2, :] += _colsum(dc * u1)
            dw_ref[2:3, :] += _colsum(dc * u)

        dz_ref[...] = parts_ref[part]

    return pl.pallas_call(
        body, name="conv_bwd", grid=(nct, 2, 3),
        in_specs=_conv_specs(s_len) + [pl.BlockSpec((s_len, CONV_COLS), lambda j, e, part: (e, j))],
        out_specs=[pl.BlockSpec((s_len, CONV_COLS), lambda j, e, part: (e, part * nct + j)),
                   pl.BlockSpec((8, CONV_COLS), lambda j, e, part: (0, j))],
        out_shape=[_mid((t, 3 * D), BF16), jax.ShapeDtypeStruct((8, D), F32)],
        scratch_shapes=[pltpu.VMEM((3, s_len, CONV_COLS), BF16)],
        compiler_params=_cparams(48, 3),
    )(*_pin(z, z, z, w, dp))


def loss_grad(y, tgt, tm=512):
    t = y.shape[0]

    def body(y_ref, t_ref, dy_ref, l_ref):
        i = pl.program_id(0)
        d = y_ref[...] - t_ref[...]
        dy_ref[...] = d * (1.0 / D)

        @pl.when(i == 0)
        def _():
            l_ref[...] = jnp.zeros_like(l_ref)

        l_ref[...] += 0.5 / D * jnp.sum(d * d)

    return pl.pallas_call(
        body, name="loss_grad", grid=(t // tm,),
        in_specs=[pl.BlockSpec((tm, D), lambda i: (i, 0))] * 2,
        out_specs=[pl.BlockSpec((tm, D), lambda i: (i, 0)), pl.BlockSpec((8, 128), lambda i: (0, 0))],
        out_shape=[_mid((t, D), F32), _mid((8, 128), F32)],
        compiler_params=_cparams(32, 1),
    )(*_pin(y, tgt))


ADA_COLS = 384


def ada_fwd(c_all, w):
    nl, _, n = w.shape
    nex = c_all.shape[0]

    def body(c_ref, w_ref, o_ref):
        cv = c_ref[...]
        ca = (cv * jax.nn.sigmoid(cv)).astype(BF16)
        o_ref[0] = _dot(ca, w_ref[0].astype(BF16))

    return pl.pallas_call(
        body, name="ada_fwd", grid=(nl, n // ADA_COLS),
        in_specs=[pl.BlockSpec((nex, D), lambda l, j: (0, 0)), pl.BlockSpec((1, D, ADA_COLS), lambda l, j: (l, 0, j))],
        out_specs=pl.BlockSpec((1, nex, ADA_COLS), lambda l, j: (l, 0, j)),
        out_shape=jax.ShapeDtypeStruct((nl, nex, n), F32),
        compiler_params=_cparams(32, 2),
    )(*_pin(c_all, w))


def _adam_math(w, g, m, v):
    m = ADAM_B1 * m + (1.0 - ADAM_B1) * g
    v = ADAM_B2 * v + (1.0 - ADAM_B2) * (g * g)
    m_hat = m / (1.0 - ADAM_B1 ** ADAM_STEP)
    v_hat = v / (1.0 - ADAM_B2 ** ADAM_STEP)
    return -ADAM_LR * (m_hat / (jnp.sqrt(v_hat) + ADAM_EPS) + ADAM_WD * w), m, v


def ada_bwd_adam(c_all, dm, w, m, v):
    nl, _, n = w.shape
    nex = c_all.shape[0]

    def body(c_ref, dm_ref, w_ref, m_ref, v_ref, g_ref, d_ref, mo_ref, vo_ref):
        cv = c_ref[...]
        ca = (cv * jax.nn.sigmoid(cv)).astype(BF16)
        g = _dot_tn(ca, dm_ref[0].astype(BF16))
        g_ref[0] = g
        d_ref[0], mo_ref[0], vo_ref[0] = _adam_math(w_ref[0], g, m_ref[0], v_ref[0])

    wspec = pl.BlockSpec((1, D, ADA_COLS), lambda l, j: (l, 0, j))
    return pl.pallas_call(
        body, name="ada_bwd_adam", grid=(nl, n // ADA_COLS),
        in_specs=[pl.BlockSpec((nex, D), lambda l, j: (0, 0)), pl.BlockSpec((1, nex, ADA_COLS), lambda l, j: (l, 0, j)),
                  wspec, wspec, wspec],
        out_specs=[wspec] * 4,
        out_shape=[jax.ShapeDtypeStruct(w.shape, F32)] * 4,
        compiler_params=_cparams(40, 2),
    )(*_pin(c_all, dm, w, m, v))


def adamw(w, g, m, v):
    shape = w.shape
    cols = shape[-1]
    rows = w.size // cols
    args = [a.reshape(rows, cols) for a in (w, g, m, v)]
    tr = rows
    while tr * cols * 4 > (1 << 20) and tr % 16 == 0:
        tr //= 2

    def body(w_ref, g_ref, m_ref, v_ref, d_ref, mo_ref, vo_ref):
        d_ref[...], mo_ref[...], vo_ref[...] = _adam_math(w_ref[...], g_ref[...], m_ref[...], v_ref[...])

    spec = pl.BlockSpec((tr, cols), lambda i: (i, 0))
    outs = pl.pallas_call(
        body, name="adamw", grid=(rows // tr,),
        in_specs=[spec] * 4, out_specs=[spec] * 3,
        out_shape=[jax.ShapeDtypeStruct((rows, cols), F32)] * 3,
        compiler_params=_cparams(32, 1),
    )(*args)
    return [o.reshape(shape) for o in outs]


def adamw_layers(w, g, m, v, prev, l0, n):
    _, r, c = w.shape
    tr = r
    while tr * c * 4 > (2 << 20) and tr % 16 == 0:
        tr //= 2

    def body(w_ref, g_ref, m_ref, v_ref, pd_ref, pm_ref, pv_ref, d_ref, mo_ref, vo_ref):
        d_ref[...], mo_ref[...], vo_ref[...] = _adam_math(w_ref[...], g_ref[...], m_ref[...], v_ref[...])

    spec = pl.BlockSpec((1, tr, c), lambda l, i: (l0 + l, i, 0))
    return pl.pallas_call(
        body, name="adamw_layers", grid=(n, r // tr),
        in_specs=[spec] * 4 + [ANY] * 3, out_specs=[spec] * 3,
        out_shape=[jax.ShapeDtypeStruct(w.shape, F32)] * 3,
        input_output_aliases={4: 0, 5: 1, 6: 2},
        compiler_params=_cparams(40, 2),
    )(*_pin(w, g, m, v, *prev))


def cast_into_window(w, l0, n, ax, chip, after=()):
    _, r, c = w.shape
    tr = r
    while tr * c * 4 > (4 << 20) and tr % 32 == 0:
        tr //= 2
    nrb = r // tr
    full = (n, r * N_CHIPS, c) if ax == 1 else (n, r, c * N_CHIPS)

    def body(chip_ref, w_ref, *rest):
        o_ref = rest[len(after)]
        o_ref[...] = w_ref[...].astype(BF16)

    def omap(l, i, chip_ref):
        return (l, chip_ref[0] * nrb + i, 0) if ax == 1 else (l, i, chip_ref[0])

    return pl.pallas_call(
        body, name="cast_into_window",
        grid_spec=pltpu.PrefetchScalarGridSpec(
            num_scalar_prefetch=1, grid=(n, nrb),
            in_specs=[pl.BlockSpec((1, tr, c), lambda l, i, chip_ref: (l0 + l, i, 0))] + [ANY] * len(after),
            out_specs=pl.BlockSpec((1, tr, c), omap)),
        out_shape=_mid(full, BF16), compiler_params=_cparams(32, 2),
    )(chip, *_pin(w, *after))


def add_bias(a, b):
    def body(a_ref, b_ref, o_ref):
        o_ref[...] = a_ref[...] + b_ref[...]

    return pl.pallas_call(body, name="add_bias", out_shape=jax.ShapeDtypeStruct(a.shape, F32))(a, b)


N_DMOD_ROWS = 40


def reduce_small(p_all):
    rows = p_all.shape[1]

    def body(p_ref, red_ref, ex_ref):
        acc = p_ref[0]
        for d in range(1, 8):
            acc = acc + p_ref[d]
        red_ref[...] = acc
        ex_ref[...] = acc[:N_DMOD_ROWS] + acc[N_DMOD_ROWS:2 * N_DMOD_ROWS]

    return pl.pallas_call(
        body, name="reduce_small",
        out_shape=[jax.ShapeDtypeStruct((rows, D), F32), jax.ShapeDtypeStruct((N_DMOD_ROWS, D), F32)],
        compiler_params=_cparams(32, 0),
    )(p_all)


def _place():
    return lax.axis_index("x"), lax.axis_index("y"), lax.axis_index("c")


def _other_chips(x, y):
    return [(1 - x, y), (x, 1 - y), (1 - x, 1 - y)]


def _sl(ref, axis, start, size):
    idx = [slice(None)] * len(ref.shape)
    idx[axis] = pl.ds(pl.multiple_of(start, 16), size)
    return ref.at[tuple(idx)]


def _rcopy(src, dst, send_sem, recv_sem, to):
    return pltpu.make_async_remote_copy(src_ref=src, dst_ref=dst, send_sem=send_sem, recv_sem=recv_sem,
                                        device_id=to, device_id_type=MESH)


def allgather_small(v, all_devices):
    rows, cols = v.shape
    flips = [(dx, dy, dc) for dx in (0, 1) for dy in (0, 1) for dc in (0, 1)
             if (dx, dy, dc) != (0, 0, 0) and (all_devices or dc == 0)]
    n_out = 8 if all_devices else 4

    def body(v_ref, o_ref, send_sems, recv_sems):
        x, y, c = _place()

        def slot(px, py, pc):
            return 4 * px + 2 * py + pc if all_devices else 2 * px + py

        peers = [(1 - x if dx else x, 1 - y if dy else y, 1 - c if dc else c) for dx, dy, dc in flips]
        sends = [_rcopy(v_ref, o_ref.at[slot(x, y, c)], send_sems.at[r], recv_sems.at[r], peer)
                 for r, peer in enumerate(peers)]
        for cp in sends:
            cp.start()
        o_ref[slot(x, y, c)] = v_ref[...]
        for r, peer in enumerate(peers):
            _rcopy(v_ref, o_ref.at[slot(*peer)], send_sems.at[r], recv_sems.at[r], peer).wait_recv()
        for cp in sends:
            cp.wait_send()

    vm = pl.BlockSpec(memory_space=pltpu.VMEM)
    return pl.pallas_call(
        body, name="allgather_small_all" if all_devices else "allgather_small_chips",
        in_specs=[vm], out_specs=vm,
        out_shape=jax.ShapeDtypeStruct((n_out, rows, cols), v.dtype),
        scratch_shapes=[pltpu.SemaphoreType.DMA((len(flips),)), pltpu.SemaphoreType.DMA((len(flips),))],
        compiler_params=pltpu.CompilerParams(vmem_limit_bytes=32 * 1024 * 1024),
    )(v)


HBM = pl.BlockSpec(memory_space=pltpu.HBM)
SEM = pl.BlockSpec(memory_space=pltpu.SEMAPHORE)
SPLIT_COPY = pltpu.CompilerParams(has_side_effects=pltpu.SideEffectType.DATAFLOW_SIDE_EFFECTING)


def _in_hbm(a):
    return pltpu.with_memory_space_constraint(a, pltpu.HBM)


def _window(ref, ax, chip):
    n = ref.shape[ax] // N_CHIPS
    return _sl(ref, ax, (2 * chip[0] + chip[1]) * n, n)


def _half(ref, ax, cc):
    ha = 3 - ax
    hs = ref.shape[ha] // 2
    return _sl(ref, ha, cc * hs, hs)


def gather_start(bufs, axes, tag):
    na = len(bufs)

    def body(*refs):
        ins = refs[:na]
        send_sems, recv_sems = refs[na], refs[na + 1]
        token = refs[2 * na + 2]
        x, y, c = _place()
        for a in range(na):
            mine = _half(_window(ins[a], axes[a], (x, y)), axes[a], c)
            for j, chip in enumerate(_other_chips(x, y)):
                _rcopy(mine, mine, send_sems.at[3 * a + j], recv_sems.at[3 * a + j], (*chip, c)).start()
        token[...] = jnp.zeros_like(token)

    dma = pltpu.SemaphoreType.DMA
    outs = pl.pallas_call(
        body, name="gather_start_" + tag,
        in_specs=[HBM] * na,
        out_specs=(SEM, SEM, *[HBM] * na, pl.BlockSpec(memory_space=pltpu.VMEM)),
        out_shape=(dma((3 * na,)), dma((3 * na,)), *[pltpu.HBM(b.shape, b.dtype) for b in bufs],
                   jax.ShapeDtypeStruct((8, 128), F32)),
        input_output_aliases={a: 2 + a for a in range(na)},
        compiler_params=SPLIT_COPY,
    )(*[_in_hbm(b) for b in bufs])
    return outs[0], outs[1], list(outs[2:2 + na]), outs[2 + na]


def gather_wait(send_sems, recv_sems, bufs, axes, after, tag):
    na = len(bufs)

    def body(*refs):
        ins = refs[:na]
        send_sems, recv_sems = refs[na], refs[na + 1]
        x, y, c = _place()
        for a in range(na):
            for j, chip in enumerate(_other_chips(x, y)):
                got = _half(_window(ins[a], axes[a], chip), axes[a], c)
                _rcopy(got, got, send_sems.at[3 * a + j], recv_sems.at[3 * a + j], (*chip, c)).wait_recv()
        for a in range(na):
            mine = _half(_window(ins[a], axes[a], (x, y)), axes[a], c)
            for j, chip in enumerate(_other_chips(x, y)):
                _rcopy(mine, mine, send_sems.at[3 * a + j], recv_sems.at[3 * a + j], (*chip, c)).wait_send()

    return pl.pallas_call(
        body, name="gather_wait_" + tag,
        in_specs=[HBM] * na + [SEM, SEM] + [ANY] * len(after),
        out_specs=[HBM] * na,
        out_shape=[pltpu.HBM(b.shape, b.dtype) for b in bufs],
        input_output_aliases={a: a for a in range(na)},
        compiler_params=SPLIT_COPY,
    )(*bufs, send_sems, recv_sems, *after)


def gather_forward(bufs, axes):
    na = len(bufs)

    def body(*refs):
        outs = refs[na:2 * na]
        send_sems, recv_sems = refs[2 * na:]
        x, y, c = _place()
        chips = _other_chips(x, y)
        passed = []
        for a in range(na):
            for j, chip in enumerate(chips):
                got = _half(_window(outs[a], axes[a], chip), axes[a], c)
                cp = _rcopy(got, got, send_sems.at[3 * a + j], recv_sems.at[3 * a + j], (x, y, 1 - c))
                cp.start()
                passed.append(cp)
        for a in range(na):
            for j, chip in enumerate(chips):
                got = _half(_window(outs[a], axes[a], chip), axes[a], 1 - c)
                _rcopy(got, got, send_sems.at[3 * a + j], recv_sems.at[3 * a + j], (x, y, 1 - c)).wait_recv()
        for cp in passed:
            cp.wait_send()

    dma = pltpu.SemaphoreType.DMA
    return pl.pallas_call(
        body, name="gather_forward",
        in_specs=[ANY] * na, out_specs=[ANY] * na,
        out_shape=[jax.ShapeDtypeStruct(b.shape, BF16) for b in bufs],
        input_output_aliases={a: a for a in range(na)},
        scratch_shapes=[dma((3 * na,)), dma((3 * na,))],
    )(*bufs)


def sibling_start(bufs, views, n_copies, tag):
    nb = len(bufs)

    def body(*refs):
        send_sems, recv_sems = refs[nb], refs[nb + 1]
        token = refs[2 * nb + 2]
        x, y, c = _place()
        for k, (src, dst, _) in enumerate(views(refs[:nb], c)):
            _rcopy(src, dst, send_sems.at[k], recv_sems.at[k], (x, y, 1 - c)).start()
        token[...] = jnp.zeros_like(token)

    dma = pltpu.SemaphoreType.DMA
    outs = pl.pallas_call(
        body, name="sibling_start_" + tag,
        in_specs=[HBM] * nb,
        out_specs=(SEM, SEM, *[HBM] * nb, pl.BlockSpec(memory_space=pltpu.VMEM)),
        out_shape=(dma((n_copies,)), dma((n_copies,)), *[pltpu.HBM(b.shape, b.dtype) for b in bufs],
                   jax.ShapeDtypeStruct((8, 128), F32)),
        input_output_aliases={a: 2 + a for a in range(nb)},
        compiler_params=SPLIT_COPY,
    )(*[_in_hbm(b) for b in bufs])
    return outs[0], outs[1], list(outs[2:2 + nb]), outs[2 + nb]


def sibling_wait(send_sems, recv_sems, bufs, views, after, tag):
    nb = len(bufs)

    def body(*refs):
        send_sems, recv_sems = refs[nb], refs[nb + 1]
        x, y, c = _place()
        trip = views(refs[:nb], c)
        for k, (src, _, got) in enumerate(trip):
            _rcopy(src, got, send_sems.at[k], recv_sems.at[k], (x, y, 1 - c)).wait_recv()
        for k, (src, dst, _) in enumerate(trip):
            _rcopy(src, dst, send_sems.at[k], recv_sems.at[k], (x, y, 1 - c)).wait_send()

    return list(pl.pallas_call(
        body, name="sibling_wait_" + tag,
        in_specs=[HBM] * nb + [SEM, SEM] + [ANY] * len(after),
        out_specs=[HBM] * nb,
        out_shape=[pltpu.HBM(b.shape, b.dtype) for b in bufs],
        input_output_aliases={a: a for a in range(nb)},
        compiler_params=SPLIT_COPY,
    )(*bufs, send_sems, recv_sems, *after))


def exchange_views(axes):
    na = len(axes)

    def views(refs, c):
        return [(_half(refs[a], axes[a], 1 - c), refs[na + a], refs[na + a]) for a in range(na)]

    return views


def join_views(regions, axes):
    def views(refs, c):
        out = []
        for a, ref in enumerate(refs):
            reg = ref.at[pl.ds(*regions[a])]
            mine = _half(reg, axes[a], c)
            out.append((mine, mine, _half(reg, axes[a], 1 - c)))
        return out

    return views


def exchange_halves(grads, axes):
    na = len(grads)

    def hshape(g, ax):
        ha = 3 - ax
        return tuple(d // 2 if i == ha else d for i, d in enumerate(g.shape))

    def body(*refs):
        ins, outs = refs[:na], refs[na:2 * na]
        send_sems, recv_sems = refs[2 * na:]
        x, y, c = _place()
        cps = []
        for a in range(na):
            ha = 3 - axes[a]
            hs = ins[a].shape[ha] // 2
            cp = _rcopy(_sl(ins[a], ha, (1 - c) * hs, hs), outs[a], send_sems.at[a], recv_sems.at[a], (x, y, 1 - c))
            cp.start()
            cps.append(cp)
        for cp in cps:
            cp.wait_recv()
        for cp in cps:
            cp.wait_send()

    dma = pltpu.SemaphoreType.DMA
    return pl.pallas_call(
        body, name="exchange_halves",
        in_specs=[ANY] * na, out_specs=[ANY] * na,
        out_shape=[jax.ShapeDtypeStruct(hshape(g, ax), BF16) for g, ax in zip(grads, axes)],
        scratch_shapes=[dma((na,)), dma((na,))],
    )(*grads)


def scatter_start(halves, axes, tag):
    na = len(halves)

    def pshape(h, ax):
        return (N_CHIPS - 1,) + tuple(d // N_CHIPS if i == ax else d for i, d in enumerate(h.shape))

    def body(*refs):
        ins, lands = refs[:na], refs[na:2 * na]
        send_sems, recv_sems = refs[2 * na], refs[2 * na + 1]
        token = refs[4 * na + 2]
        x, y, c = _place()
        for a in range(na):
            for j, chip in enumerate(_other_chips(x, y)):
                _rcopy(_window(ins[a], axes[a], chip), lands[a].at[j],
                       send_sems.at[3 * a + j], recv_sems.at[3 * a + j], (*chip, c)).start()
        token[...] = jnp.zeros_like(token)

    dma = pltpu.SemaphoreType.DMA
    lands = [lax.empty(pshape(h, ax), BF16) for h, ax in zip(halves, axes)]
    outs = pl.pallas_call(
        body, name="scatter_start_" + tag,
        in_specs=[HBM] * (2 * na),
        out_specs=(SEM, SEM, *[HBM] * (2 * na), pl.BlockSpec(memory_space=pltpu.VMEM)),
        out_shape=(dma((3 * na,)), dma((3 * na,)), *[pltpu.HBM(b.shape, b.dtype) for b in halves + lands],
                   jax.ShapeDtypeStruct((8, 128), F32)),
        input_output_aliases={a: 2 + a for a in range(2 * na)},
        compiler_params=SPLIT_COPY,
    )(*[_in_hbm(b) for b in halves + lands])
    return outs[0], outs[1], list(outs[2:2 + na]), list(outs[2 + na:2 + 2 * na]), outs[2 + 2 * na]


def scatter_wait(send_sems, recv_sems, halves, lands, axes, after, tag):
    na = len(halves)

    def body(*refs):
        ins, lands = refs[:na], refs[na:2 * na]
        send_sems, recv_sems = refs[2 * na], refs[2 * na + 1]
        x, y, c = _place()
        for a in range(na):
            for j, chip in enumerate(_other_chips(x, y)):
                _rcopy(_window(ins[a], axes[a], chip), lands[a].at[j],
                       send_sems.at[3 * a + j], recv_sems.at[3 * a + j], (*chip, c)).wait_recv()
        for a in range(na):
            for j, chip in enumerate(_other_chips(x, y)):
                _rcopy(_window(ins[a], axes[a], chip), lands[a].at[j],
                       send_sems.at[3 * a + j], recv_sems.at[3 * a + j], (*chip, c)).wait_send()

    outs = pl.pallas_call(
        body, name="scatter_wait_" + tag,
        in_specs=[HBM] * (2 * na) + [SEM, SEM] + [ANY] * len(after),
        out_specs=[HBM] * (2 * na),
        out_shape=[pltpu.HBM(b.shape, b.dtype) for b in halves + lands],
        input_output_aliases={a: a for a in range(2 * na)},
        compiler_params=SPLIT_COPY,
    )(*halves, *lands, send_sems, recv_sems, *after)
    return list(outs[:na]), list(outs[na:])


def join_halves(gs, regions, axes):
    na = len(gs)

    def body(*refs):
        outs = refs[na:2 * na]
        send_sems, recv_sems = refs[2 * na:]
        x, y, c = _place()
        cps = []
        for a in range(na):
            ha = 3 - axes[a]
            hs = outs[a].shape[ha] // 2
            reg = outs[a].at[pl.ds(*regions[a])]
            mine = _sl(reg, ha, c * hs, hs)
            cp = _rcopy(mine, mine, send_sems.at[a], recv_sems.at[a], (x, y, 1 - c))
            cp.start()
            cps.append((cp, _sl(reg, ha, (1 - c) * hs, hs)))
        for a, (cp, theirs) in enumerate(cps):
            _rcopy(theirs, theirs, send_sems.at[a], recv_sems.at[a], (x, y, 1 - c)).wait_recv()
        for cp, _ in cps:
            cp.wait_send()

    dma = pltpu.SemaphoreType.DMA
    return pl.pallas_call(
        body, name="join_halves",
        in_specs=[ANY] * na, out_specs=[ANY] * na,
        out_shape=[jax.ShapeDtypeStruct(g.shape, F32) for g in gs],
        input_output_aliases={a: a for a in range(na)},
        scratch_shapes=[dma((na,)), dma((na,))],
    )(*gs)


def _tile2(r, c, itemsize, limit):
    bc = c
    while bc > 1536:
        bc //= 2
    assert c % bc == 0 and bc % 128 == 0
    br = r
    while br * bc * itemsize > limit and br % 32 == 0:
        br //= 2
    assert r % br == 0 and br % 16 == 0
    return br, bc


def add_my_half(g, theirs, ax, cc):
    ha = 3 - ax
    nl, r, c = theirs.shape
    br, bc = _tile2(r, c, 2, 2 << 20)
    nrb, ncb = r // br, c // bc

    def body(cc_ref, g_ref, t_ref, o_ref):
        o_ref[...] = (g_ref[...].astype(F32) + t_ref[...].astype(F32)).astype(BF16)

    def gmap(l, i, j, cc_ref):
        return (l, cc_ref[0] * nrb + i, j) if ha == 1 else (l, i, cc_ref[0] * ncb + j)

    blk = pl.BlockSpec((1, br, bc), lambda l, i, j, cc_ref: (l, i, j))
    return pl.pallas_call(
        body, name="add_my_half",
        grid_spec=pltpu.PrefetchScalarGridSpec(
            num_scalar_prefetch=1, grid=(nl, nrb, ncb),
            in_specs=[pl.BlockSpec((1, br, bc), gmap), blk], out_specs=blk),
        out_shape=_mid(theirs.shape, BF16),
        compiler_params=_cparams(32, 3),
    )(cc, *_pin(g, theirs))


def sum_chips(parts, pair, gstack, l0, ax, where):
    _, n, r, c = parts.shape
    ha = 3 - ax
    br, bc = _tile2(r, c, 2, 1 << 20)
    nrb, ncb = r // br, c // bc

    def body(w_ref, p_ref, own_ref, g_ref, o_ref):
        acc = own_ref[...].astype(F32)
        for q in range(N_CHIPS - 1):
            acc = acc + p_ref[q].astype(F32)
        o_ref[...] = acc

    def own_map(l, i, j, w_ref):
        return (l, w_ref[0] * nrb + i, j) if ax == 1 else (l, i, w_ref[0] * ncb + j)

    def out_map(l, i, j, w_ref):
        return (l0 + l, w_ref[1] * nrb + i, j) if ha == 1 else (l0 + l, i, w_ref[1] * ncb + j)

    return pl.pallas_call(
        body, name="sum_chips",
        grid_spec=pltpu.PrefetchScalarGridSpec(
            num_scalar_prefetch=1, grid=(n, nrb, ncb),
            in_specs=[pl.BlockSpec((N_CHIPS - 1, 1, br, bc), lambda l, i, j, w_ref: (0, l, i, j)),
                      pl.BlockSpec((1, br, bc), own_map), ANY],
            out_specs=pl.BlockSpec((1, br, bc), out_map)),
        out_shape=jax.ShapeDtypeStruct(gstack.shape, F32),
        input_output_aliases={3: 0},
        compiler_params=_cparams(32, 3),
    )(where, *_pin(parts, pair, gstack))


BIG = ("w_ffn_up", "w_ffn_down", "attn_w_qkv", "attn_w_o", "conv_w_in", "conv_w_out")
BIG_AXIS = {"w_ffn_up": 2, "w_ffn_down": 1, "attn_w_qkv": 2, "attn_w_o": 1, "conv_w_in": 2, "conv_w_out": 1}
WEIGHTS = ("norm_gain", "w_ada", "b_ada", "w_ffn_up", "w_ffn_down", "attn_w_qkv", "attn_b_qkv", "attn_q_gain",
           "attn_k_gain", "attn_sinks", "attn_w_o", "attn_b_o", "conv_w_in", "conv_w", "conv_w_out")
N_SMALL_ROWS = 112


def _stack3(a):
    return a.reshape((-1,) + a.shape[-2:])


def _head_matrices():
    lane = jnp.arange(QK_DIM)
    head = lane // HEAD_DIM
    col = jnp.arange(128)
    g1 = jnp.where(head[:, None] == col[None, :], 1.0 / HEAD_DIM, 0.0).astype(BF16)
    g2 = jnp.where(col[:, None] == head[None, :], 1.0, 0.0).astype(BF16)
    fold = lane % HEAD_DIM + jnp.where(head >= N_HEADS, HEAD_DIM, 0)
    gsel = jnp.where(fold[:, None] == col[None, :], 1.0, 0.0).astype(BF16)
    return g1, g2, gsel


def _pad_cols(a, n):
    return jnp.pad(a, ((0, 0), (0, n - a.shape[1])))


def kernel(x, c, positions, norm_gain, w_ada, b_ada, w_ffn_up, w_ffn_down, attn_w_qkv, attn_b_qkv, attn_q_gain, attn_k_gain, attn_sinks, attn_w_o, attn_b_o, conv_w_in, conv_w, conv_w_out, loss_target, m_norm_gain, m_w_ada, m_b_ada, m_w_ffn_up, m_w_ffn_down, m_attn_w_qkv, m_attn_b_qkv, m_attn_q_gain, m_attn_k_gain, m_attn_sinks, m_attn_w_o, m_attn_b_o, m_conv_w_in, m_conv_w, m_conv_w_out, v_norm_gain, v_w_ada, v_b_ada, v_w_ffn_up, v_w_ffn_down, v_attn_w_qkv, v_attn_b_qkv, v_attn_q_gain, v_attn_k_gain, v_attn_sinks, v_attn_w_o, v_attn_b_o, v_conv_w_in, v_conv_w, v_conv_w_out):
    w = dict(norm_gain=norm_gain, w_ada=w_ada, b_ada=b_ada, w_ffn_up=w_ffn_up, w_ffn_down=w_ffn_down,
             attn_w_qkv=attn_w_qkv, attn_b_qkv=attn_b_qkv, attn_q_gain=attn_q_gain, attn_k_gain=attn_k_gain,
             attn_sinks=attn_sinks, attn_w_o=attn_w_o, attn_b_o=attn_b_o, conv_w_in=conv_w_in, conv_w=conv_w,
             conv_w_out=conv_w_out)
    mom = dict(norm_gain=m_norm_gain, w_ada=m_w_ada, b_ada=m_b_ada, w_ffn_up=m_w_ffn_up, w_ffn_down=m_w_ffn_down,
               attn_w_qkv=m_attn_w_qkv, attn_b_qkv=m_attn_b_qkv, attn_q_gain=m_attn_q_gain,
               attn_k_gain=m_attn_k_gain, attn_sinks=m_attn_sinks, attn_w_o=m_attn_w_o, attn_b_o=m_attn_b_o,
               conv_w_in=m_conv_w_in, conv_w=m_conv_w, conv_w_out=m_conv_w_out)
    var = dict(norm_gain=v_norm_gain, w_ada=v_w_ada, b_ada=v_b_ada, w_ffn_up=v_w_ffn_up, w_ffn_down=v_w_ffn_down,
               attn_w_qkv=v_attn_w_qkv, attn_b_qkv=v_attn_b_qkv, attn_q_gain=v_attn_q_gain,
               attn_k_gain=v_attn_k_gain, attn_sinks=v_attn_sinks, attn_w_o=v_attn_w_o, attn_b_o=v_attn_b_o,
               conv_w_in=v_conv_w_in, conv_w=v_conv_w, conv_w_out=v_conv_w_out)

    xi, yi, ci = _place()
    chip = 2 * xi + yi
    dev = 4 * xi + 2 * yi + ci
    nex, s_len, _ = x.shape
    t = nex * s_len
    n_attn, n_conv = attn_w_qkv.shape[0], conv_w_in.shape[0]
    axes = [BIG_AXIS[n] for n in BIG]

    c_all = allgather_small(jnp.pad(c, ((0, 8 - nex), (0, 0))), True)[:, :nex].reshape(8 * nex, D)
    ada_cols = w_ada.shape[2]
    modp = ada_fwd(c_all, w_ada)
    modg = allgather_small(modp.reshape(DEPTH * 8 * nex, ada_cols), False)
    modg = lax.dynamic_slice_in_dim(modg.reshape(N_CHIPS, DEPTH, 8 * nex, ada_cols), dev * nex, nex, axis=2)
    modg = modg.transpose(1, 2, 0, 3).reshape(DEPTH, nex, 9 * D)
    mod = add_bias(modg, b_ada.reshape(DEPTH, 1, 9 * D)).reshape(DEPTH, nex, 9, D)

    small = jnp.concatenate([norm_gain.reshape(DEPTH * 3, -1), conv_w.reshape(n_conv * 3, -1)], axis=0)
    small = jnp.pad(small, ((0, -small.shape[0] % 8), (0, 0)))
    small = allgather_small(small, False).transpose(1, 0, 2).reshape(small.shape[0], D)
    gain_full = small[:DEPTH * 3].reshape(DEPTH, 3, D)
    convw_full = small[DEPTH * 3:DEPTH * 3 + n_conv * 3].reshape(n_conv, 3, D)

    chip_arr = chip.reshape(1).astype(jnp.int32)
    where = jnp.stack([chip, ci]).astype(jnp.int32)
    stacks = [_stack3(w[n]) for n in BIG]

    def mixer(i):
        return (2, 3) if i % 2 == 0 else (4, 5)

    groups = [[(0, 0, 1), (1, 0, 1)], [(mixer(0)[0], 0, 1), (mixer(0)[1], 0, 1), (0, 1, 1), (1, 1, 1)]]
    groups += [[(0, 2 * i, 2), (1, 2 * i, 2), (mixer(i)[0], i // 2, 1), (mixer(i)[1], i // 2, 1)]
               for i in range(1, DEPTH)]
    gaxes = [[axes[b] for b, _, _ in grp] for grp in groups]
    where_is = {(b, l0 + k): (g, a, k) for g, grp in enumerate(groups) for a, (b, l0, n) in enumerate(grp)
                for k in range(n)}
    in_flight, token = [], (mod, small)
    for g, grp in enumerate(groups):
        bufs = [cast_into_window(stacks[b], l0, n, axes[b], chip_arr, token) for b, l0, n in grp]
        ssem, rsem, bufs, tok = gather_start(bufs, gaxes[g], f"g{g}")
        in_flight.append((ssem, rsem, bufs))
        token = (tok,)

    invf = ROPE_THETA ** (-jnp.arange(0, HEAD_DIM, 2, dtype=F32) / HEAD_DIM)
    cos, sin = rope_tables(positions.reshape(t, 1), jnp.tile(invf, 4).reshape(1, 128))
    g1, g2, gsel = _head_matrices()
    gqk = [jnp.concatenate([jnp.tile(attn_q_gain[j], N_HEADS), jnp.tile(attn_k_gain[j], N_KV)]).reshape(1, QK_DIM)
           for j in range(n_attn)]
    zero_bias = jnp.zeros((1, D), F32)

    xs = x.reshape(t, D)
    saved, ready = [], {}

    def weight(b, l, after):
        g, a, k = where_is[(b, l)]
        if g not in ready:
            ssem, rsem, bufs = in_flight[g]
            ready[g] = gather_forward(gather_wait(ssem, rsem, bufs, gaxes[g], after, f"g{g}"), gaxes[g])
        return ready[g][a], k

    for i in range(DEPTH):
        j = i // 2
        gn, md = gain_full[i], mod[i]
        x0 = xs
        wup, k = weight(0, 2 * i, token if i == 0 else (xs,))
        wdn, _ = weight(1, 2 * i, ())
        xs, u1, f1 = ffn_fwd(x0, gn[0:1], md, wup, wdn, k, 0)
        x1 = xs
        wmi, k = weight(mixer(i)[0], j, (xs,))
        wmo, _ = weight(mixer(i)[1], j, ())
        if i % 2 == 0:
            raw, qr, kr = qkv_fwd(x1, gn[1:2], md, wmi, k, attn_b_qkv[j:j + 1], gqk[j], cos, sin, g1, g2)
            o, lse = attn_fwd(qr, kr, raw, attn_sinks[j:j + 1])
            xs, ymix = proj_res(x1, o, wmo, k, attn_b_o[j:j + 1], md)
            mix = (raw, qr, kr, o, lse)
        else:
            z = lin_fwd(x1, gn[1:2], md, wmi, k, 1)
            p = conv_fwd(z, convw_full[j])
            xs, ymix = proj_res(x1, p, wmo, k, zero_bias, md)
            mix = (z, p)
        x2 = xs
        wup, k = weight(0, 2 * i + 1, (xs,))
        wdn, _ = weight(1, 2 * i + 1, ())
        xs, u3, f3 = ffn_fwd(x2, gn[2:3], md, wup, wdn, k, 2)
        saved.append((x0, u1, f1, x1, mix, ymix, x2, u3, f3))
    dy, lpart = loss_grad(xs, loss_target.reshape(t, D))

    cc = ci.reshape(1).astype(jnp.int32)
    gshard = [lax.empty(s.shape, F32) for s in stacks]
    upd = [[lax.empty(s.shape, F32) for _ in range(3)] for s in stacks]
    mstacks = [_stack3(mom[n]) for n in BIG]
    vstacks = [_stack3(var[n]) for n in BIG]

    ggrad = {g: [lax.empty(buf.shape, BF16) for buf in ready[g]] for g in range(len(groups))}
    missing = {g: sum(n for _, _, n in grp) for g, grp in enumerate(groups)}
    state = dict(to_sibling=None, on_ici=None, joining=None, token=())

    def half_shape(shape, ax):
        return tuple(d // 2 if i == 3 - ax else d for i, d in enumerate(shape))

    def advance(after, last=False):
        if state["joining"] is not None:
            g, ssem, rsem, bufs = state["joining"]
            grp = groups[g]
            regions = [(l0, n) for _, l0, n in grp]
            joined = sibling_wait(ssem, rsem, bufs, join_views(regions, gaxes[g]), after, f"join_g{g}")
            for (b, l0, n), gj in zip(grp, joined):
                gshard[b] = gj
                upd[b] = adamw_layers(stacks[b], gj, mstacks[b], vstacks[b], upd[b], l0, n)
            state["joining"] = None
        started = None
        if state["to_sibling"] is not None:
            g, ssem, rsem, bufs = state["to_sibling"]
            na = len(groups[g])
            bufs = sibling_wait(ssem, rsem, bufs, exchange_views(gaxes[g]), after, f"xchg_g{g}")
            pair = [add_my_half(gr, th, ax, cc) for gr, th, ax in zip(bufs[:na], bufs[na:], gaxes[g])]
            ssem, rsem, pair, lands, tok = scatter_start(pair, gaxes[g], f"g{g}")
            started = (g, ssem, rsem, pair, lands)
            state.update(to_sibling=None, token=(tok,))
        if state["on_ici"] is not None and (started is not None or last):
            g, ssem, rsem, pair, lands = state["on_ici"]
            pair, lands = scatter_wait(ssem, rsem, pair, lands, gaxes[g], state["token"] + after, f"g{g}")
            grp = groups[g]
            for a, (b, l0, n) in enumerate(grp):
                gshard[b] = sum_chips(lands[a], pair[a], gshard[b], l0, axes[b], where)
            regions = [(l0, n) for _, l0, n in grp]
            ssem, rsem, bufs, tok = sibling_start([gshard[b] for b, _, _ in grp], join_views(regions, gaxes[g]),
                                                  len(grp), f"join_g{g}")
            for (b, _, _), buf in zip(grp, bufs):
                gshard[b] = buf
            state.update(joining=(g, ssem, rsem, bufs), on_ici=None, token=(tok,))
        if started is not None:
            state["on_ici"] = started

    def put(b, l, lhs, rhs, bm, bn):
        g, a, k = where_is[(b, l)]
        ggrad[g][a] = wgrad(ggrad[g][a], k, lhs, rhs, bm, bn, after=state["token"])
        state["token"] = ()
        missing[g] -= 1
        if missing[g] == 0:
            lands = [lax.empty(half_shape(gr.shape, ax), BF16) for gr, ax in zip(ggrad[g], gaxes[g])]
            ssem, rsem, bufs, tok = sibling_start(ggrad[g] + lands, exchange_views(gaxes[g]), len(lands), f"xchg_g{g}")
            state.update(to_sibling=(g, ssem, rsem, bufs), token=(tok,))

    dmod = [None] * DEPTH
    dgain = [None] * DEPTH
    db_qkv, dqk_gain, dsinks, db_o, dconv_w = ([None] * n_attn, [None] * n_attn, [None] * n_attn,
                                                [None] * n_attn, [None] * n_conv)
    for i in reversed(range(DEPTH)):
        j = i // 2
        gn, md = gain_full[i], mod[i]
        x0, u1, f1, x1, mix, ymix, x2, u3, f3 = saved[i]
        (wup, k), (wdn, _) = weight(0, 2 * i + 1, ()), weight(1, 2 * i + 1, ())
        dy, du, a, df, h, se3, sa3 = ffn_bwd(dy, u3, f3, x2, gn[2:3], md, wup, wdn, k, 2, after=state["token"])
        advance((dy,))
        put(0, 2 * i + 1, h, du, D, D_FF)
        put(1, 2 * i + 1, a, df, D_FF // 2, D)
        (wmi, k), (wmo, _) = weight(mixer(i)[0], j, ()), weight(mixer(i)[1], j, ())
        if i % 2 == 0:
            raw, qr, kr, o, lse = mix
            dyy, do, sp, sb = proj_res_bwd(dy, ymix, wmo, k, md)
            put(mixer(i)[1], j, o, dyy, D, D)
            dq, dk, dv, dsinks[j] = attn_bwd(qr, kr, raw, attn_sinks[j:j + 1], o, do, lse)
            dz, dqk_gain[j] = qkv_bwd_pre(dq, dk, dv, raw, gqk[j], cos, sin, g1, g2, gsel)
            dy, h, se2, sa2, db_qkv[j] = lin_bwd(dy, dz, wmi, k, x1, gn[1:2], md, 1, True)
            put(mixer(i)[0], j, h, dz, D, QKV_DIM)
            db_o[j] = sb
        else:
            z, p = mix
            dyy, dp, sp, _ = proj_res_bwd(dy, ymix, wmo, k, md)
            put(mixer(i)[1], j, p, dyy, D, D)
            dz, dconv_w[j] = conv_bwd(z, convw_full[j], dp)
            dy, h, se2, sa2 = lin_bwd(dy, dz, wmi, k, x1, gn[1:2], md, 1, False)
            put(mixer(i)[0], j, h, dz, D, 1536)
        (wup, k), (wdn, _) = weight(0, 2 * i, ()), weight(1, 2 * i, ())
        dy, du, a, df, h, se1, sa1 = ffn_bwd(dy, u1, f1, x0, gn[0:1], md, wup, wdn, k, 0, after=state["token"])
        advance((dy,))
        put(0, 2 * i, h, du, D, D_FF)
        put(1, 2 * i, a, df, D_FF // 2, D)
        dmod[i] = jnp.stack([se1[:, 0], se1[:, 1], se1[:, 2], se2[:, 0], se2[:, 1], sp[:, 0],
                             se3[:, 0], se3[:, 1], se3[:, 2]], axis=1)
        dgain[i] = jnp.stack([sa1[0], sa2[0], sa3[0]], axis=0)
    grad_x = dy.reshape(x.shape)
    advance((dy,))

    dmod_ex = jnp.stack(dmod, axis=1).reshape(nex, DEPTH * 9, D)
    dmod_ex = jnp.pad(dmod_ex, ((0, 0), (0, N_DMOD_ROWS - DEPTH * 9), (0, 0))).reshape(nex * N_DMOD_ROWS, D)
    misc = jnp.concatenate([dqk_gain[jj][0] for jj in range(n_attn)]
                           + [jnp.pad(dsinks[jj][0], (0, 128 - N_HEADS)) for jj in range(n_attn)]
                           + [lpart[0]])
    rows = [dmod_ex,
            jnp.concatenate(dgain, axis=0), jnp.zeros((4, D), F32),
            jnp.concatenate([_pad_cols(db_qkv[jj][0:1], 2 * D).reshape(2, D) for jj in range(n_attn)], axis=0),
            jnp.concatenate([db_o[jj][0:1] for jj in range(n_attn)], axis=0),
            jnp.concatenate([dconv_w[jj][0:3] for jj in range(n_conv)], axis=0),
            jnp.pad(misc, (0, D - misc.shape[0])).reshape(1, D)]
    packed = jnp.concatenate(rows, axis=0)
    packed = jnp.pad(packed, ((0, N_SMALL_ROWS - packed.shape[0]), (0, 0)))
    p_all = allgather_small(packed, True)
    red, exsum = reduce_small(p_all)

    r0 = nex * N_DMOD_ROWS
    grads = {}
    grads["b_ada"] = exsum[:DEPTH * 9].reshape(DEPTH, 9 * D)
    grads["norm_gain"] = lax.dynamic_slice_in_dim(red[r0:r0 + 12].reshape(DEPTH, 3, D), chip * (D // N_CHIPS),
                                                  D // N_CHIPS, axis=2)
    r1 = r0 + 16
    grads["attn_b_qkv"] = red[r1:r1 + 2 * n_attn].reshape(n_attn, 2 * D)[:, :QKV_DIM]
    r2 = r1 + 2 * n_attn
    grads["attn_b_o"] = red[r2:r2 + n_attn]
    r3 = r2 + n_attn
    grads["conv_w"] = lax.dynamic_slice_in_dim(red[r3:r3 + 3 * n_conv].reshape(n_conv, 3, D), chip * (D // N_CHIPS),
                                               D // N_CHIPS, axis=2)
    mrow = red[r3 + 3 * n_conv]
    grads["attn_q_gain"] = jnp.stack([mrow[128 * jj:128 * jj + HEAD_DIM] for jj in range(n_attn)])
    grads["attn_k_gain"] = jnp.stack([mrow[128 * jj + HEAD_DIM:128 * jj + 128] for jj in range(n_attn)])
    grads["attn_sinks"] = jnp.stack([mrow[128 * (n_attn + jj):128 * (n_attn + jj) + N_HEADS] for jj in range(n_attn)])
    loss = mrow[128 * 2 * n_attn]

    dm_all = p_all[:, :r0].reshape(8, nex, N_DMOD_ROWS, D)[:, :, :DEPTH * 9].reshape(8 * nex, DEPTH, 9 * D)
    dm_mine = lax.dynamic_slice_in_dim(dm_all.transpose(1, 0, 2), chip * ada_cols, ada_cols, axis=2)
    g_ada, d_ada, nm_ada, nv_ada = ada_bwd_adam(c_all, dm_mine, w_ada, m_w_ada, v_w_ada)

    delta, new_m, new_v = {}, {}, {}
    for n in WEIGHTS:
        if n == "w_ada":
            grads[n], delta[n], new_m[n], new_v[n] = g_ada, d_ada, nm_ada, nv_ada
        elif n not in BIG:
            delta[n], new_m[n], new_v[n] = adamw(w[n], grads[n], mom[n], var[n])
    advance((delta["conv_w"], d_ada), last=True)
    advance(state["token"], last=True)
    assert all(state[k] is None for k in ("to_sibling", "on_ici", "joining"))
    for b, n in enumerate(BIG):
        grads[n] = gshard[b].reshape(w[n].shape)
        delta[n], new_m[n], new_v[n] = (u.reshape(w[n].shape) for u in upd[b])

    return (loss, grad_x, *[grads[n] for n in WEIGHTS], *[delta[n] for n in WEIGHTS],
            *[new_m[n] for n in WEIGHTS], *[new_v[n] for n in WEIGHTS])
```

```python
import functools

import jax
import jax.numpy as jnp
from jax import lax
from jax.experimental import pallas as pl
from jax.experimental.pallas import tpu as pltpu

F32 = jnp.float32
BF16 = jnp.bfloat16

D = 1024
D_FF = 2816
N_HEADS = 16
N_KV = 4
HEAD_DIM = 64
GROUP = N_HEADS // N_KV
QK_DIM = (N_HEADS + N_KV) * HEAD_DIM
QKV_DIM = QK_DIM + N_KV * HEAD_DIM
BLOCK = 128
ROPE_THETA = 10000.0
EPS = 1e-6
DEPTH = 4
N_CHIPS = 4

ADAM_LR = 0.001
ADAM_B1 = 0.9
ADAM_B2 = 0.999
ADAM_EPS = 1e-08
ADAM_WD = 0.01
ADAM_STEP = 10

V7X_VMEM_BYTES = 64 * 1024 * 1024
V7X_MXU_DIM = 256
FF_CHUNKS = ((0, 1536), (1536, D_FF))
assert all((hi - lo) % V7X_MXU_DIM == 0 for lo, hi in FF_CHUNKS)
MESH = pl.DeviceIdType.MESH
ANY = pl.BlockSpec(memory_space=pl.ANY)


def _cparams(vmem_mb, n_grid):
    assert vmem_mb * 1024 * 1024 <= V7X_VMEM_BYTES
    return pltpu.CompilerParams(vmem_limit_bytes=vmem_mb * 1024 * 1024,
                                dimension_semantics=("arbitrary",) * n_grid)


def _resident(shape):
    nd = len(shape)
    return pl.BlockSpec(shape, lambda *_: (0,) * nd, pipeline_mode=pl.Buffered(1))


def _layer(w, l):
    return pl.BlockSpec((None,) + w.shape[1:], lambda *_: (l, 0, 0), pipeline_mode=pl.Buffered(1))


PIN_BYTES = 1 << 20


def _pin(*args):
    return [pltpu.with_memory_space_constraint(a, pltpu.HBM) if a.size * a.dtype.itemsize >= PIN_BYTES else a
            for a in args]


def _mid(shape, dtype):
    n = 1
    for d in shape:
        n *= d
    if n * jnp.dtype(dtype).itemsize >= PIN_BYTES:
        return pltpu.HBM(tuple(shape), dtype)
    return jax.ShapeDtypeStruct(tuple(shape), dtype)


def _dot(a, b):
    return jnp.dot(a, b, preferred_element_type=F32)


def _dot_nt(a, b):
    return lax.dot_general(a, b, (((1,), (1,)), ((), ())), preferred_element_type=F32)


def _dot_tn(a, b):
    return lax.dot_general(a, b, (((0,), (0,)), ((), ())), preferred_element_type=F32)


def _dot_hilo(a, g):
    hi = a.astype(BF16)
    lo = (a - hi.astype(F32)).astype(BF16)
    return _dot(hi, g) + _dot(lo, g)


def _sigmoid(x):
    return 0.5 * jnp.tanh(0.5 * x) + 0.5


def _colsum(a):
    return jnp.sum(a, axis=0, keepdims=True)


def _norm_mod(x, gain, sc, sh):
    r = lax.rsqrt(jnp.mean(x * x, axis=-1, keepdims=True) + EPS)
    n = x * r * gain
    return r, n, n * (1.0 + sc) + sh


def _mod_rows(mod_ref, s):
    return (mod_ref[0, 3 * s:3 * s + 1, :], mod_ref[0, 3 * s + 1:3 * s + 2, :], mod_ref[0, 3 * s + 2:3 * s + 3, :])


def ffn_fwd(x, gain, mod, wup, wdn, l, s, tm=512, after=()):
    t = x.shape[0]
    tpe = t // tm // 2

    def body(x_ref, gain_ref, mod_ref, wup_ref, wdn_ref, *rest):
        xo_ref, u_ref, f_ref = rest[len(after):]
        xv = x_ref[...]
        sh, sc, g = _mod_rows(mod_ref, s)
        _, _, h = _norm_mod(xv, gain_ref[...], sc, sh)
        hb = h.astype(BF16)
        acc = jnp.zeros((tm, D), F32)
        for lo, hi in FF_CHUNKS:
            gate = _dot(hb, wup_ref[:, lo:hi])
            up = _dot(hb, wup_ref[:, D_FF + lo:D_FF + hi])
            u_ref[:, lo:hi] = gate.astype(BF16)
            u_ref[:, D_FF + lo:D_FF + hi] = up.astype(BF16)
            a = (gate * _sigmoid(gate) * up).astype(BF16)
            acc = acc + _dot(a, wdn_ref[lo:hi, :])
        f_ref[...] = acc.astype(BF16)
        xo_ref[...] = xv + 0.5 * g * acc

    return pl.pallas_call(
        body, name="ffn_fwd", grid=(t // tm,),
        in_specs=[pl.BlockSpec((tm, D), lambda i: (i, 0)),
                  pl.BlockSpec((1, D), lambda i: (0, 0)),
                  pl.BlockSpec((1, 9, D), lambda i: (i // tpe, 0, 0)),
                  _layer(wup, l), _layer(wdn, l)] + [ANY] * len(after),
        out_specs=[pl.BlockSpec((tm, D), lambda i: (i, 0)),
                   pl.BlockSpec((tm, 2 * D_FF), lambda i: (i, 0)),
                   pl.BlockSpec((tm, D), lambda i: (i, 0))],
        out_shape=[_mid((t, D), F32), _mid((t, 2 * D_FF), BF16), _mid((t, D), BF16)],
        compiler_params=_cparams(60, 1),
    )(*_pin(x, gain, mod, wup, wdn, *after))


def ffn_bwd(dy, u, f, x, gain, mod, wup, wdn, l, s, tm=256, after=()):
    t = dy.shape[0]
    tpe = t // tm // 2

    def body(dy_ref, u_ref, f_ref, x_ref, gain_ref, mod_ref, wup_ref, wdn_ref, *rest):
        dx_ref, du_ref, a_ref, df_ref, h_ref, se_ref, sa_ref = rest[len(after):]
        i = pl.program_id(0)
        dyv, xv, gain_v = dy_ref[...], x_ref[...], gain_ref[...]
        sh, sc, g = _mod_rows(mod_ref, s)
        r, nrm, h = _norm_mod(xv, gain_v, sc, sh)
        h_ref[...] = h.astype(BF16)
        dfb = (0.5 * g * dyv).astype(BF16)
        df_ref[...] = dfb
        dh = jnp.zeros((tm, D), F32)
        for lo, hi in FF_CHUNKS:
            da = _dot_nt(dfb, wdn_ref[lo:hi, :])
            gate = u_ref[:, lo:hi].astype(F32)
            up = u_ref[:, D_FF + lo:D_FF + hi].astype(F32)
            sg = _sigmoid(gate)
            silu = gate * sg
            a_ref[:, lo:hi] = (silu * up).astype(BF16)
            dgate = (da * up * (sg + silu * (1.0 - sg))).astype(BF16)
            dup = (da * silu).astype(BF16)
            du_ref[:, lo:hi] = dgate
            du_ref[:, D_FF + lo:D_FF + hi] = dup
            dh = dh + _dot_nt(dgate, wup_ref[:, lo:hi]) + _dot_nt(dup, wup_ref[:, D_FF + lo:D_FF + hi])
        dn = dh * (1.0 + sc)
        dxr = dn * gain_v
        m = jnp.mean(dxr * xv, axis=-1, keepdims=True)
        dx_ref[...] = dyv + r * dxr - xv * (r * r * r) * m

        @pl.when(i % tpe == 0)
        def _():
            se_ref[...] = jnp.zeros_like(se_ref)

        @pl.when(i == 0)
        def _():
            sa_ref[...] = jnp.zeros_like(sa_ref)

        se_ref[0, 0:1, :] += _colsum(dh)
        se_ref[0, 1:2, :] += _colsum(dh * nrm)
        se_ref[0, 2:3, :] += _colsum(0.5 * dyv * f_ref[...].astype(F32))
        sa_ref[0:1, :] += _colsum(dn * xv * r)

    row = lambda w_: pl.BlockSpec((tm, w_), lambda i: (i, 0))
    return pl.pallas_call(
        body, name="ffn_bwd", grid=(t // tm,),
        in_specs=[row(D), row(2 * D_FF), row(D), row(D),
                  pl.BlockSpec((1, D), lambda i: (0, 0)),
                  pl.BlockSpec((1, 9, D), lambda i: (i // tpe, 0, 0)),
                  _layer(wup, l), _layer(wdn, l)] + [ANY] * len(after),
        out_specs=[row(D), row(2 * D_FF), row(D_FF), row(D), row(D),
                   pl.BlockSpec((1, 8, D), lambda i: (i // tpe, 0, 0)), pl.BlockSpec((8, D), lambda i: (0, 0))],
        out_shape=[_mid((t, D), F32), _mid((t, 2 * D_FF), BF16), _mid((t, D_FF), BF16), _mid((t, D), BF16),
                   _mid((t, D), BF16), _mid((2, 8, D), F32), _mid((8, D), F32)],
        compiler_params=_cparams(60, 1),
    )(*_pin(dy, u, f, x, gain, mod, wup, wdn, *after))


def lin_bwd(dy, dz, w, l, x, gain, mod, s, want_db, tm=512):
    t = dy.shape[0]
    n = w.shape[2]
    tpe = t // tm // 2

    def body(dy_ref, dz_ref, w_ref, x_ref, gain_ref, mod_ref, dx_ref, h_ref, se_ref, sa_ref, *db_ref):
        i = pl.program_id(0)
        xv = x_ref[...]
        gain_v = gain_ref[...]
        sh, sc, _ = _mod_rows(mod_ref, s)
        r, nrm, h = _norm_mod(xv, gain_v, sc, sh)
        h_ref[...] = h.astype(BF16)
        dh = _dot_nt(dz_ref[...], w_ref[...])
        dn = dh * (1.0 + sc)
        dxr = dn * gain_v
        m = jnp.mean(dxr * xv, axis=-1, keepdims=True)
        dx_ref[...] = dy_ref[...] + r * dxr - xv * (r * r * r) * m

        @pl.when(i % tpe == 0)
        def _():
            se_ref[...] = jnp.zeros_like(se_ref)

        @pl.when(i == 0)
        def _():
            sa_ref[...] = jnp.zeros_like(sa_ref)
            if want_db:
                db_ref[0][...] = jnp.zeros_like(db_ref[0])

        se_ref[0, 0:1, :] += _colsum(dh)
        se_ref[0, 1:2, :] += _colsum(dh * nrm)
        sa_ref[0:1, :] += _colsum(dn * xv * r)
        if want_db:
            db_ref[0][0:1, :] += _colsum(dz_ref[...].astype(F32))

    out_specs = [pl.BlockSpec((tm, D), lambda i: (i, 0)), pl.BlockSpec((tm, D), lambda i: (i, 0)),
                 pl.BlockSpec((1, 8, D), lambda i: (i // tpe, 0, 0)), pl.BlockSpec((8, D), lambda i: (0, 0))]
    out_shape = [_mid((t, D), F32), _mid((t, D), BF16),
                 jax.ShapeDtypeStruct((2, 8, D), F32), jax.ShapeDtypeStruct((8, D), F32)]
    if want_db:
        out_specs.append(pl.BlockSpec((8, n), lambda i: (0, 0)))
        out_shape.append(jax.ShapeDtypeStruct((8, n), F32))
    return pl.pallas_call(
        body, name="lin_bwd", grid=(t // tm,),
        in_specs=[pl.BlockSpec((tm, D), lambda i: (i, 0)),
                  pl.BlockSpec((tm, n), lambda i: (i, 0)),
                  _layer(w, l),
                  pl.BlockSpec((tm, D), lambda i: (i, 0)),
                  pl.BlockSpec((1, D), lambda i: (0, 0)),
                  pl.BlockSpec((1, 9, D), lambda i: (i // tpe, 0, 0))],
        out_specs=out_specs, out_shape=out_shape,
        compiler_params=_cparams(56, 1),
    )(*_pin(dy, dz, w, x, gain, mod))


def wgrad(gstack, l, a, b, bm, bn, bt=1024, after=()):
    t, m = a.shape
    n = b.shape[1]
    nt = t // bt

    def body(g_ref, a_ref, b_ref, *rest):
        o_ref, acc_ref = rest[len(after):]
        k = pl.program_id(2)

        @pl.when(k == 0)
        def _():
            acc_ref[...] = jnp.zeros_like(acc_ref)

        acc_ref[...] += _dot_tn(a_ref[...], b_ref[...])

        @pl.when(k == nt - 1)
        def _():
            o_ref[...] = acc_ref[...].astype(BF16)

    return pl.pallas_call(
        body, name="wgrad", grid=(m // bm, n // bn, nt),
        in_specs=[ANY, pl.BlockSpec((bt, bm), lambda i, j, k: (k, i)),
                  pl.BlockSpec((bt, bn), lambda i, j, k: (k, j))] + [ANY] * len(after),
        out_specs=pl.BlockSpec((None, bm, bn), lambda i, j, k: (l, i, j)),
        out_shape=_mid(gstack.shape, BF16),
        input_output_aliases={0: 0},
        scratch_shapes=[pltpu.VMEM((bm, bn), F32)],
        compiler_params=_cparams(56, 3),
    )(*_pin(gstack, a, b, *after))


def proj_res(x, o, w, l, b, mod, tm=512):
    t = x.shape[0]
    tpe = t // tm // 2

    def body(x_ref, o_ref, w_ref, b_ref, mod_ref, xo_ref, y_ref):
        _, _, g = _mod_rows(mod_ref, 1)
        y = _dot(o_ref[...], w_ref[...]) + b_ref[...]
        y_ref[...] = y.astype(BF16)
        xo_ref[...] = x_ref[...] + g * y

    return pl.pallas_call(
        body, name="proj_res", grid=(t // tm,),
        in_specs=[pl.BlockSpec((tm, D), lambda i: (i, 0)), pl.BlockSpec((tm, D), lambda i: (i, 0)),
                  _layer(w, l), pl.BlockSpec((1, D), lambda i: (0, 0)),
                  pl.BlockSpec((1, 9, D), lambda i: (i // tpe, 0, 0))],
        out_specs=[pl.BlockSpec((tm, D), lambda i: (i, 0)), pl.BlockSpec((tm, D), lambda i: (i, 0))],
        out_shape=[_mid((t, D), F32), _mid((t, D), BF16)],
        compiler_params=_cparams(32, 1),
    )(*_pin(x, o, w, b, mod))


def proj_res_bwd(dy, y, w, l, mod, tm=512):
    t = dy.shape[0]
    tpe = t // tm // 2

    def body(dy_ref, y_ref, w_ref, mod_ref, dyy_ref, do_ref, se_ref, sa_ref):
        i = pl.program_id(0)
        _, _, g = _mod_rows(mod_ref, 1)
        dyv = dy_ref[...]
        dyy = g * dyv
        dyb = dyy.astype(BF16)
        dyy_ref[...] = dyb
        do_ref[...] = _dot_nt(dyb, w_ref[...]).astype(BF16)

        @pl.when(i % tpe == 0)
        def _():
            se_ref[...] = jnp.zeros_like(se_ref)

        @pl.when(i == 0)
        def _():
            sa_ref[...] = jnp.zeros_like(sa_ref)

        se_ref[0, 0:1, :] += _colsum(dyv * y_ref[...].astype(F32))
        sa_ref[0:1, :] += _colsum(dyy)

    return pl.pallas_call(
        body, name="proj_res_bwd", grid=(t // tm,),
        in_specs=[pl.BlockSpec((tm, D), lambda i: (i, 0)), pl.BlockSpec((tm, D), lambda i: (i, 0)),
                  _layer(w, l), pl.BlockSpec((1, 9, D), lambda i: (i // tpe, 0, 0))],
        out_specs=[pl.BlockSpec((tm, D), lambda i: (i, 0)), pl.BlockSpec((tm, D), lambda i: (i, 0)),
                   pl.BlockSpec((1, 8, D), lambda i: (i // tpe, 0, 0)), pl.BlockSpec((8, D), lambda i: (0, 0))],
        out_shape=[_mid((t, D), BF16), _mid((t, D), BF16), _mid((2, 8, D), F32), _mid((8, D), F32)],
        compiler_params=_cparams(32, 1),
    )(*_pin(dy, y, w, mod))


def lin_fwd(x, gain, mod, w, l, s, tm=512):
    t = x.shape[0]
    n = w.shape[2]
    tpe = t // tm // 2

    def body(x_ref, gain_ref, mod_ref, w_ref, z_ref):
        sh, sc, _ = _mod_rows(mod_ref, s)
        _, _, h = _norm_mod(x_ref[...], gain_ref[...], sc, sh)
        z_ref[...] = _dot(h.astype(BF16), w_ref[...]).astype(BF16)

    return pl.pallas_call(
        body, name="lin_fwd", grid=(t // tm,),
        in_specs=[pl.BlockSpec((tm, D), lambda i: (i, 0)), pl.BlockSpec((1, D), lambda i: (0, 0)),
                  pl.BlockSpec((1, 9, D), lambda i: (i // tpe, 0, 0)), _layer(w, l)],
        out_specs=pl.BlockSpec((tm, n), lambda i: (i, 0)),
        out_shape=_mid((t, n), BF16),
        compiler_params=_cparams(40, 1),
    )(*_pin(x, gain, mod, w))


def rope_tables(pos, invf):
    t = pos.shape[0]
    tm = 1024

    def body(pos_ref, invf_ref, c_ref, s_ref):
        ang = pos_ref[...].astype(F32) * invf_ref[...]
        lane = lax.broadcasted_iota(jnp.int32, (tm, 128), 1)
        sign = jnp.where(lane % HEAD_DIM < HEAD_DIM // 2, -1.0, 1.0)
        c_ref[...] = jnp.cos(ang)
        s_ref[...] = sign * jnp.sin(ang)

    return pl.pallas_call(
        body, name="rope_tables", grid=(t // tm,),
        in_specs=[pl.BlockSpec((tm, 1), lambda i: (i, 0)), pl.BlockSpec((1, 128), lambda i: (0, 0))],
        out_specs=[pl.BlockSpec((tm, 128), lambda i: (i, 0))] * 2,
        out_shape=[jax.ShapeDtypeStruct((t, 128), F32)] * 2,
        compiler_params=_cparams(16, 1),
    )(pos, invf)


def _swap_halves(v):
    lane = lax.broadcasted_iota(jnp.int32, v.shape, 1)
    return jnp.where(lane % HEAD_DIM < HEAD_DIM // 2, pltpu.roll(v, 128 - HEAD_DIM // 2, 1), pltpu.roll(v, HEAD_DIM // 2, 1))


def _rope(v, cos, sin):
    return jnp.concatenate(
        [v[:, j:j + 128] * cos + _swap_halves(v[:, j:j + 128]) * sin for j in range(0, v.shape[1], 128)], axis=1)


def _rope_t(dv, cos, sin):
    return jnp.concatenate(
        [dv[:, j:j + 128] * cos + _swap_halves(dv[:, j:j + 128] * sin) for j in range(0, dv.shape[1], 128)], axis=1)


def _head_stats(qk, g1, g2):
    rinv = lax.rsqrt(_dot_hilo(qk * qk, g1) + EPS)
    return rinv, _dot_hilo(rinv, g2)


def qkv_fwd(x, gain, mod, w, l, b, gqk, cos, sin, g1, g2, tm=256):
    t = x.shape[0]
    tpe = t // tm // 2

    def body(x_ref, gain_ref, mod_ref, w_ref, b_ref, gqk_ref, c_ref, s_ref, g1_ref, g2_ref, raw_ref, q_ref, k_ref):
        sh, sc, _ = _mod_rows(mod_ref, 1)
        _, _, h = _norm_mod(x_ref[...], gain_ref[...], sc, sh)
        qkv = _dot(h.astype(BF16), w_ref[...]) + b_ref[...]
        raw_ref[...] = qkv.astype(BF16)
        qk = qkv[:, :QK_DIM]
        _, rb = _head_stats(qk, g1_ref[...], g2_ref[...])
        qr = _rope(qk * rb * gqk_ref[...], c_ref[...], s_ref[...])
        q_ref[...] = qr[:, :D].astype(BF16)
        k_ref[...] = qr[:, D:].astype(BF16)

    return pl.pallas_call(
        body, name="qkv_fwd", grid=(t // tm,),
        in_specs=[pl.BlockSpec((tm, D), lambda i: (i, 0)), pl.BlockSpec((1, D), lambda i: (0, 0)),
                  pl.BlockSpec((1, 9, D), lambda i: (i // tpe, 0, 0)), _layer(w, l),
                  pl.BlockSpec((1, QKV_DIM), lambda i: (0, 0)), pl.BlockSpec((1, QK_DIM), lambda i: (0, 0)),
                  pl.BlockSpec((tm, 128), lambda i: (i, 0)), pl.BlockSpec((tm, 128), lambda i: (i, 0)),
                  _resident((QK_DIM, 128)), _resident((128, QK_DIM))],
        out_specs=[pl.BlockSpec((tm, QKV_DIM), lambda i: (i, 0)), pl.BlockSpec((tm, D), lambda i: (i, 0)),
                   pl.BlockSpec((tm, N_KV * HEAD_DIM), lambda i: (i, 0))],
        out_shape=[_mid((t, QKV_DIM), BF16), _mid((t, D), BF16), _mid((t, N_KV * HEAD_DIM), BF16)],
        compiler_params=_cparams(40, 1),
    )(*_pin(x, gain, mod, w, b, gqk, cos, sin, g1, g2))


def qkv_bwd_pre(dq, dk, dv, raw, gqk, cos, sin, g1, g2, gsel, tm=256):
    t = dq.shape[0]

    def body(dq_ref, dk_ref, dv_ref, raw_ref, gqk_ref, c_ref, s_ref, g1_ref, g2_ref, gsel_ref, dz_ref, sa_ref):
        i = pl.program_id(0)
        dqk = jnp.concatenate([dq_ref[...].astype(F32), dk_ref[...]], axis=1)
        dqn = _rope_t(dqk, c_ref[...], s_ref[...])
        qk = raw_ref[:, :QK_DIM].astype(F32)
        g1v, g2v = g1_ref[...], g2_ref[...]
        rinv, rb = _head_stats(qk, g1v, g2v)
        dgq = jnp.broadcast_to(_colsum(dqn * qk * rb), (8, QK_DIM))
        dyh = dqn * gqk_ref[...]
        mh = _dot_hilo(dyh * qk, g1v)
        mb = _dot_hilo(mh * rinv * rinv * rinv, g2v)
        dz_ref[:, :QK_DIM] = (rb * dyh - qk * mb).astype(BF16)
        dz_ref[:, QK_DIM:] = dv_ref[...].astype(BF16)

        @pl.when(i == 0)
        def _():
            sa_ref[...] = jnp.zeros_like(sa_ref)

        sa_ref[...] += _dot_hilo(dgq, gsel_ref[...])

    kvw = N_KV * HEAD_DIM
    return pl.pallas_call(
        body, name="qkv_bwd_pre", grid=(t // tm,),
        in_specs=[pl.BlockSpec((tm, D), lambda i: (i, 0)), pl.BlockSpec((tm, kvw), lambda i: (i, 0)),
                  pl.BlockSpec((tm, kvw), lambda i: (i, 0)), pl.BlockSpec((tm, QKV_DIM), lambda i: (i, 0)),
                  pl.BlockSpec((1, QK_DIM), lambda i: (0, 0)),
                  pl.BlockSpec((tm, 128), lambda i: (i, 0)), pl.BlockSpec((tm, 128), lambda i: (i, 0)),
                  _resident((QK_DIM, 128)), _resident((128, QK_DIM)), _resident((QK_DIM, 128))],
        out_specs=[pl.BlockSpec((tm, QKV_DIM), lambda i: (i, 0)), pl.BlockSpec((8, 128), lambda i: (0, 0))],
        out_shape=[_mid((t, QKV_DIM), BF16), _mid((8, 128), F32)],
        compiler_params=_cparams(40, 1),
    )(*_pin(dq, dk, dv, raw, gqk, cos, sin, g1, g2, gsel))


def _band_mask(n):
    row = lax.broadcasted_iota(jnp.int32, (GROUP * BLOCK, 2 * BLOCK), 0) % BLOCK
    col = lax.broadcasted_iota(jnp.int32, (GROUP * BLOCK, 2 * BLOCK), 1)
    rel = row + BLOCK - col
    return (rel >= 0) & (rel < BLOCK) & ((col >= BLOCK) | (n > 0))


def _stack_heads(v, g):
    base = g * GROUP * HEAD_DIM
    return jnp.concatenate([v[:, base + j * HEAD_DIM:base + (j + 1) * HEAD_DIM] for j in range(GROUP)], axis=0)


def _kv_cat(prev, cur, g):
    return jnp.concatenate([prev[:, g * HEAD_DIM:(g + 1) * HEAD_DIM], cur[:, g * HEAD_DIM:(g + 1) * HEAD_DIM]], axis=0)


def _sink_col(sink_ref, g):
    return jnp.concatenate([jnp.full((BLOCK, 1), sink_ref[0, g * GROUP + j], F32) for j in range(GROUP)], axis=0)


def _attn_specs(nb):
    kvw = N_KV * HEAD_DIM
    vcol = QK_DIM // kvw
    cur = lambda e, n: (e * nb + n, 0)
    prev = lambda e, n: (e * nb + jnp.maximum(n - 1, 0), 0)
    return [pl.BlockSpec((BLOCK, D), cur),
            pl.BlockSpec((BLOCK, kvw), cur), pl.BlockSpec((BLOCK, kvw), prev),
            pl.BlockSpec((BLOCK, kvw), lambda e, n: (e * nb + n, vcol)),
            pl.BlockSpec((BLOCK, kvw), lambda e, n: (e * nb + jnp.maximum(n - 1, 0), vcol)),
            pl.BlockSpec(memory_space=pltpu.SMEM)]


def attn_fwd(q, k, raw, sinks):
    t = q.shape[0]
    nb = t // 2 // BLOCK

    def body(q_ref, kc_ref, kp_ref, vc_ref, vp_ref, sink_ref, o_ref, lse_ref):
        n = pl.program_id(1)
        qv = q_ref[...]
        kc, kp, vc, vp = kc_ref[...], kp_ref[...], vc_ref[...], vp_ref[...]
        mask = _band_mask(n)
        outs, lses = [], []
        for g in range(N_KV):
            kk, vv = _kv_cat(kp, kc, g), _kv_cat(vp, vc, g)
            s = jnp.where(mask, _dot_nt(_stack_heads(qv, g), kk) * (HEAD_DIM ** -0.5), -1e30)
            sink = _sink_col(sink_ref, g)
            m = jnp.maximum(jnp.max(s, axis=1, keepdims=True), sink)
            p = jnp.exp(s - m)
            l = jnp.sum(p, axis=1, keepdims=True) + jnp.exp(sink - m)
            o = _dot(p.astype(BF16), vv) / l
            lse = m + jnp.log(l)
            for j in range(GROUP):
                outs.append(o[j * BLOCK:(j + 1) * BLOCK, :])
                lses.append(lse[j * BLOCK:(j + 1) * BLOCK, :])
        o_ref[...] = jnp.concatenate(outs, axis=1).astype(BF16)
        lse_ref[...] = jnp.concatenate(lses, axis=1)

    cur = lambda e, n: (e * nb + n, 0)
    return pl.pallas_call(
        body, name="attn_fwd", grid=(2, nb),
        in_specs=_attn_specs(nb),
        out_specs=[pl.BlockSpec((BLOCK, D), cur), pl.BlockSpec((BLOCK, N_HEADS), cur)],
        out_shape=[_mid((t, D), BF16), _mid((t, N_HEADS), F32)],
        compiler_params=_cparams(32, 2),
    )(*_pin(q, k, k, raw, raw), sinks)


def attn_bwd(q, k, raw, sinks, o, do, lse):
    t = q.shape[0]
    s_len = t // 2
    nb = s_len // BLOCK
    kvw = N_KV * HEAD_DIM

    def body(q_ref, kc_ref, kp_ref, vc_ref, vp_ref, sink_ref, o_ref, do_ref, lse_ref, dq_ref, dk_ref, dv_ref, ds_ref):
        n = pl.program_id(1)

        @pl.when(n == 0)
        def _():
            dk_ref[...] = jnp.zeros_like(dk_ref)
            dv_ref[...] = jnp.zeros_like(dv_ref)

        @pl.when((n == 0) & (pl.program_id(0) == 0))
        def _():
            ds_ref[...] = jnp.zeros_like(ds_ref)

        qv, ov, dov, lsev = q_ref[...], o_ref[...], do_ref[...], lse_ref[...]
        kc, kp, vc, vp = kc_ref[...], kp_ref[...], vc_ref[...], vp_ref[...]
        mask = _band_mask(n)
        dqs, dks, dvs, dsk = [], [], [], []
        for g in range(N_KV):
            kk, vv = _kv_cat(kp, kc, g), _kv_cat(vp, vc, g)
            qg, og, dog = _stack_heads(qv, g), _stack_heads(ov, g), _stack_heads(dov, g)
            lse = jnp.concatenate([lsev[:, g * GROUP + j:g * GROUP + j + 1] for j in range(GROUP)], axis=0)
            s = jnp.where(mask, _dot_nt(qg, kk) * (HEAD_DIM ** -0.5), -1e30)
            p = jnp.exp(s - lse)
            dd = _dot_hilo(dog.astype(F32) * og.astype(F32), jnp.ones((HEAD_DIM, 128), BF16))[:, :1]
            ds = (p * (_dot_nt(dog, vv) - dd) * (HEAD_DIM ** -0.5)).astype(BF16)
            dqg = _dot(ds, kk)
            dks.append(_dot_tn(ds, qg))
            dvs.append(_dot_tn(p.astype(BF16), dog))
            wsink = jnp.exp(_sink_col(sink_ref, g) - lse) * dd
            for j in range(GROUP):
                dqs.append(dqg[j * BLOCK:(j + 1) * BLOCK, :])
                dsk.append(wsink[j * BLOCK:(j + 1) * BLOCK, :])
        dq_ref[...] = jnp.concatenate(dqs, axis=1).astype(BF16)
        dkk = jnp.concatenate(dks, axis=1)
        dvv = jnp.concatenate(dvs, axis=1)
        prev0 = pl.multiple_of(jnp.maximum(n - 1, 0) * BLOCK, BLOCK)
        cur0 = pl.multiple_of(n * BLOCK, BLOCK)
        dk_ref[pl.ds(prev0, BLOCK), :] += dkk[:BLOCK]
        dv_ref[pl.ds(prev0, BLOCK), :] += dvv[:BLOCK]
        dk_ref[pl.ds(cur0, BLOCK), :] += dkk[BLOCK:]
        dv_ref[pl.ds(cur0, BLOCK), :] += dvv[BLOCK:]
        ds_ref[0:1, :] -= _colsum(jnp.concatenate(dsk, axis=1))

    cur = lambda e, n: (e * nb + n, 0)
    return pl.pallas_call(
        body, name="attn_bwd", grid=(2, nb),
        in_specs=_attn_specs(nb) + [pl.BlockSpec((BLOCK, D), cur), pl.BlockSpec((BLOCK, D), cur),
                                    pl.BlockSpec((BLOCK, N_HEADS), cur)],
        out_specs=[pl.BlockSpec((BLOCK, D), cur), pl.BlockSpec((s_len, kvw), lambda e, n: (e, 0)),
                   pl.BlockSpec((s_len, kvw), lambda e, n: (e, 0)), pl.BlockSpec((8, N_HEADS), lambda e, n: (0, 0))],
        out_shape=[_mid((t, D), BF16), _mid((t, kvw), F32), _mid((t, kvw), F32), _mid((8, N_HEADS), F32)],
        compiler_params=_cparams(32, 2),
    )(*_pin(q, k, k, raw, raw), sinks, *_pin(o, do, lse))


CONV_COLS = 256


def _conv_specs(s_len):
    nct = D // CONV_COLS
    return [pl.BlockSpec((s_len, CONV_COLS), lambda j, e, *_: (e, j)),
            pl.BlockSpec((s_len, CONV_COLS), lambda j, e, *_: (e, nct + j)),
            pl.BlockSpec((s_len, CONV_COLS), lambda j, e, *_: (e, 2 * nct + j)),
            pl.BlockSpec((3, CONV_COLS), lambda j, e, *_: (0, j))]


def _conv_taps(gc, v, w, s_len):
    u = gc * v
    row = lax.broadcasted_iota(jnp.int32, u.shape, 0)
    u1 = jnp.where(row >= 1, pltpu.roll(u, 1, 0), 0.0)
    u2 = jnp.where(row >= 2, pltpu.roll(u, 2, 0), 0.0)
    return u, u1, u2, w[2:3, :] * u + w[1:2, :] * u1 + w[0:1, :] * u2


def conv_fwd(z, w):
    t = z.shape[0]
    s_len = t // 2

    def body(gb_ref, gc_ref, v_ref, w_ref, p_ref):
        _, _, _, conv = _conv_taps(gc_ref[...].astype(F32), v_ref[...].astype(F32), w_ref[...], s_len)
        p_ref[...] = (gb_ref[...].astype(F32) * conv).astype(BF16)

    return pl.pallas_call(
        body, name="conv_fwd", grid=(D // CONV_COLS, 2),
        in_specs=_conv_specs(s_len),
        out_specs=pl.BlockSpec((s_len, CONV_COLS), lambda j, e: (e, j)),
        out_shape=_mid((t, D), BF16),
        compiler_params=_cparams(40, 2),
    )(*_pin(z, z, z, w))


def conv_bwd(z, w, dp):
    t = z.shape[0]
    s_len = t // 2
    nct = D // CONV_COLS

    def body(gb_ref, gc_ref, v_ref, w_ref, dp_ref, dz_ref, dw_ref, parts_ref, sems):
        j, e = pl.program_id(0), pl.program_id(1)
        gc, v, wv = gc_ref[...].astype(F32), v_ref[...].astype(F32), w_ref[...]
        u, u1, u2, conv = _conv_taps(gc, v, wv, s_len)
        dpv = dp_ref[...].astype(F32)
        parts_ref[0] = (dpv * conv).astype(BF16)
        dc = dpv * gb_ref[...].astype(F32)
        row = lax.broadcasted_iota(jnp.int32, dc.shape, 0)
        dc1 = jnp.where(row <= s_len - 2, pltpu.roll(dc, s_len - 1, 0), 0.0)
        dc2 = jnp.where(row <= s_len - 3, pltpu.roll(dc, s_len - 2, 0), 0.0)
        du = wv[2:3, :] * dc + wv[1:2, :] * dc1 + wv[0:1, :] * dc2
        parts_ref[1] = (du * v).astype(BF16)
        parts_ref[2] = (du * gc).astype(BF16)
        rows = pl.ds(pl.multiple_of(e * s_len, s_len), s_len)
        copies = [pltpu.make_async_copy(
            parts_ref.at[k], dz_ref.at[rows, pl.ds(pl.multiple_of((k * nct + j) * CONV_COLS, CONV_COLS), CONV_COLS)],
            sems.at[k]) for k in range(3)]
        for cp in copies:
            cp.start()

        @pl.when(e == 0)
        def _():
            dw_ref[...] = jnp.zeros_like(dw_ref)

        dw_ref[0:1, :] += _colsum(dc * u2)
        dw_ref[1:2, :] += _colsum(dc * u1)
        dw_ref[2:3, :] += _colsum(dc * u)
        for cp in copies:
            cp.wait()

    return pl.pallas_call(
        body, name="conv_bwd", grid=(nct, 2),
        in_specs=_conv_specs(s_len) + [pl.BlockSpec((s_len, CONV_COLS), lambda j, e: (e, j))],
        out_specs=[ANY, pl.BlockSpec((8, CONV_COLS), lambda j, e: (0, j))],
        out_shape=[_mid((t, 3 * D), BF16), jax.ShapeDtypeStruct((8, D), F32)],
        scratch_shapes=[pltpu.VMEM((3, s_len, CONV_COLS), BF16), pltpu.SemaphoreType.DMA((3,))],
        compiler_params=_cparams(48, 2),
    )(*_pin(z, z, z, w, dp))


def loss_grad(y, tgt, tm=512):
    t = y.shape[0]

    def body(y_ref, t_ref, dy_ref, l_ref):
        i = pl.program_id(0)
        d = y_ref[...] - t_ref[...]
        dy_ref[...] = d * (1.0 / D)

        @pl.when(i == 0)
        def _():
            l_ref[...] = jnp.zeros_like(l_ref)

        l_ref[...] += 0.5 / D * jnp.sum(d * d)

    return pl.pallas_call(
        body, name="loss_grad", grid=(t // tm,),
        in_specs=[pl.BlockSpec((tm, D), lambda i: (i, 0))] * 2,
        out_specs=[pl.BlockSpec((tm, D), lambda i: (i, 0)), pl.BlockSpec((8, 128), lambda i: (0, 0))],
        out_shape=[_mid((t, D), F32), _mid((8, 128), F32)],
        compiler_params=_cparams(32, 1),
    )(*_pin(y, tgt))


ADA_COLS = 384


def ada_fwd(c_all, w):
    nl, _, n = w.shape
    nex = c_all.shape[0]

    def body(c_ref, w_ref, o_ref):
        cv = c_ref[...]
        ca = (cv * jax.nn.sigmoid(cv)).astype(BF16)
        o_ref[0] = _dot(ca, w_ref[0].astype(BF16))

    return pl.pallas_call(
        body, name="ada_fwd", grid=(nl, n // ADA_COLS),
        in_specs=[pl.BlockSpec((nex, D), lambda l, j: (0, 0)), pl.BlockSpec((1, D, ADA_COLS), lambda l, j: (l, 0, j))],
        out_specs=pl.BlockSpec((1, nex, ADA_COLS), lambda l, j: (l, 0, j)),
        out_shape=jax.ShapeDtypeStruct((nl, nex, n), F32),
        compiler_params=_cparams(32, 2),
    )(*_pin(c_all, w))


def _adam_math(w, g, m, v):
    m = ADAM_B1 * m + (1.0 - ADAM_B1) * g
    v = ADAM_B2 * v + (1.0 - ADAM_B2) * (g * g)
    m_hat = m / (1.0 - ADAM_B1 ** ADAM_STEP)
    v_hat = v / (1.0 - ADAM_B2 ** ADAM_STEP)
    return -ADAM_LR * (m_hat / (jnp.sqrt(v_hat) + ADAM_EPS) + ADAM_WD * w), m, v


def ada_bwd_adam(c_all, dm, w, m, v):
    nl, _, n = w.shape
    nex = c_all.shape[0]

    def body(c_ref, dm_ref, w_ref, m_ref, v_ref, g_ref, d_ref, mo_ref, vo_ref):
        cv = c_ref[...]
        ca = (cv * jax.nn.sigmoid(cv)).astype(BF16)
        g = _dot_tn(ca, dm_ref[0].astype(BF16))
        g_ref[0] = g
        d_ref[0], mo_ref[0], vo_ref[0] = _adam_math(w_ref[0], g, m_ref[0], v_ref[0])

    wspec = pl.BlockSpec((1, D, ADA_COLS), lambda l, j: (l, 0, j))
    return pl.pallas_call(
        body, name="ada_bwd_adam", grid=(nl, n // ADA_COLS),
        in_specs=[pl.BlockSpec((nex, D), lambda l, j: (0, 0)), pl.BlockSpec((1, nex, ADA_COLS), lambda l, j: (l, 0, j)),
                  wspec, wspec, wspec],
        out_specs=[wspec] * 4,
        out_shape=[jax.ShapeDtypeStruct(w.shape, F32)] * 4,
        compiler_params=_cparams(40, 2),
    )(*_pin(c_all, dm, w, m, v))


def adamw(w, g, m, v):
    shape = w.shape
    cols = shape[-1]
    rows = w.size // cols
    args = [a.reshape(rows, cols) for a in (w, g, m, v)]
    tr = rows
    while tr * cols * 4 > (1 << 20) and tr % 16 == 0:
        tr //= 2

    def body(w_ref, g_ref, m_ref, v_ref, d_ref, mo_ref, vo_ref):
        d_ref[...], mo_ref[...], vo_ref[...] = _adam_math(w_ref[...], g_ref[...], m_ref[...], v_ref[...])

    spec = pl.BlockSpec((tr, cols), lambda i: (i, 0))
    outs = pl.pallas_call(
        body, name="adamw", grid=(rows // tr,),
        in_specs=[spec] * 4, out_specs=[spec] * 3,
        out_shape=[jax.ShapeDtypeStruct((rows, cols), F32)] * 3,
        compiler_params=_cparams(32, 1),
    )(*args)
    return [o.reshape(shape) for o in outs]


def adamw_layers(w, g, m, v, prev, l0, n):
    _, r, c = w.shape
    tr = r
    while tr * c * 4 > (2 << 20) and tr % 16 == 0:
        tr //= 2

    def body(w_ref, g_ref, m_ref, v_ref, pd_ref, pm_ref, pv_ref, d_ref, mo_ref, vo_ref):
        d_ref[...], mo_ref[...], vo_ref[...] = _adam_math(w_ref[...], g_ref[...], m_ref[...], v_ref[...])

    spec = pl.BlockSpec((1, tr, c), lambda l, i: (l0 + l, i, 0))
    return pl.pallas_call(
        body, name="adamw_layers", grid=(n, r // tr),
        in_specs=[spec] * 4 + [ANY] * 3, out_specs=[spec] * 3,
        out_shape=[jax.ShapeDtypeStruct(w.shape, F32)] * 3,
        input_output_aliases={4: 0, 5: 1, 6: 2},
        compiler_params=_cparams(40, 2),
    )(*_pin(w, g, m, v, *prev))


def cast_into_window(w, l0, n, ax, chip, after=()):
    _, r, c = w.shape
    tr = r
    while tr * c * 4 > (4 << 20) and tr % 32 == 0:
        tr //= 2
    nrb = r // tr
    full = (n, r * N_CHIPS, c) if ax == 1 else (n, r, c * N_CHIPS)

    def body(chip_ref, w_ref, *rest):
        o_ref = rest[len(after)]
        o_ref[...] = w_ref[...].astype(BF16)

    def omap(l, i, chip_ref):
        return (l, chip_ref[0] * nrb + i, 0) if ax == 1 else (l, i, chip_ref[0])

    return pl.pallas_call(
        body, name="cast_into_window",
        grid_spec=pltpu.PrefetchScalarGridSpec(
            num_scalar_prefetch=1, grid=(n, nrb),
            in_specs=[pl.BlockSpec((1, tr, c), lambda l, i, chip_ref: (l0 + l, i, 0))] + [ANY] * len(after),
            out_specs=pl.BlockSpec((1, tr, c), omap)),
        out_shape=_mid(full, BF16), compiler_params=_cparams(32, 2),
    )(chip, *_pin(w, *after))


def add_bias(a, b):
    def body(a_ref, b_ref, o_ref):
        o_ref[...] = a_ref[...] + b_ref[...]

    return pl.pallas_call(body, name="add_bias", out_shape=jax.ShapeDtypeStruct(a.shape, F32))(a, b)


N_DMOD_ROWS = 40


def reduce_small(p_all):
    rows = p_all.shape[1]

    def body(p_ref, red_ref, ex_ref):
        acc = p_ref[0]
        for d in range(1, 8):
            acc = acc + p_ref[d]
        red_ref[...] = acc
        ex_ref[...] = acc[:N_DMOD_ROWS] + acc[N_DMOD_ROWS:2 * N_DMOD_ROWS]

    return pl.pallas_call(
        body, name="reduce_small",
        out_shape=[jax.ShapeDtypeStruct((rows, D), F32), jax.ShapeDtypeStruct((N_DMOD_ROWS, D), F32)],
        compiler_params=_cparams(32, 0),
    )(p_all)


def _place():
    return lax.axis_index("x"), lax.axis_index("y"), lax.axis_index("c")


def _other_chips(x, y):
    return [(1 - x, y), (x, 1 - y), (1 - x, 1 - y)]


def _sl(ref, axis, start, size):
    idx = [slice(None)] * len(ref.shape)
    idx[axis] = pl.ds(pl.multiple_of(start, 16), size)
    return ref.at[tuple(idx)]


def _rcopy(src, dst, send_sem, recv_sem, to):
    return pltpu.make_async_remote_copy(src_ref=src, dst_ref=dst, send_sem=send_sem, recv_sem=recv_sem,
                                        device_id=to, device_id_type=MESH)


def allgather_small(v, all_devices):
    rows, cols = v.shape
    flips = [(dx, dy, dc) for dx in (0, 1) for dy in (0, 1) for dc in (0, 1)
             if (dx, dy, dc) != (0, 0, 0) and (all_devices or dc == 0)]
    n_out = 8 if all_devices else 4

    def body(v_ref, o_ref, send_sems, recv_sems):
        x, y, c = _place()

        def slot(px, py, pc):
            return 4 * px + 2 * py + pc if all_devices else 2 * px + py

        peers = [(1 - x if dx else x, 1 - y if dy else y, 1 - c if dc else c) for dx, dy, dc in flips]
        sends = [_rcopy(v_ref, o_ref.at[slot(x, y, c)], send_sems.at[r], recv_sems.at[r], peer)
                 for r, peer in enumerate(peers)]
        for cp in sends:
            cp.start()
        o_ref[slot(x, y, c)] = v_ref[...]
        for r, peer in enumerate(peers):
            _rcopy(v_ref, o_ref.at[slot(*peer)], send_sems.at[r], recv_sems.at[r], peer).wait_recv()
        for cp in sends:
            cp.wait_send()

    vm = pl.BlockSpec(memory_space=pltpu.VMEM)
    return pl.pallas_call(
        body, name="allgather_small_all" if all_devices else "allgather_small_chips",
        in_specs=[vm], out_specs=vm,
        out_shape=jax.ShapeDtypeStruct((n_out, rows, cols), v.dtype),
        scratch_shapes=[pltpu.SemaphoreType.DMA((len(flips),)), pltpu.SemaphoreType.DMA((len(flips),))],
        compiler_params=pltpu.CompilerParams(vmem_limit_bytes=32 * 1024 * 1024),
    )(v)


HBM = pl.BlockSpec(memory_space=pltpu.HBM)
SEM = pl.BlockSpec(memory_space=pltpu.SEMAPHORE)
SPLIT_COPY = pltpu.CompilerParams(has_side_effects=pltpu.SideEffectType.DATAFLOW_SIDE_EFFECTING)


def _in_hbm(a):
    return pltpu.with_memory_space_constraint(a, pltpu.HBM)


def _window(ref, ax, chip):
    n = ref.shape[ax] // N_CHIPS
    return _sl(ref, ax, (2 * chip[0] + chip[1]) * n, n)


def _half(ref, ax, cc):
    ha = 3 - ax
    hs = ref.shape[ha] // 2
    return _sl(ref, ha, cc * hs, hs)


def gather_start(bufs, axes, tag):
    na = len(bufs)

    def body(*refs):
        ins = refs[:na]
        send_sems, recv_sems = refs[na], refs[na + 1]
        token = refs[2 * na + 2]
        x, y, c = _place()
        for a in range(na):
            mine = _half(_window(ins[a], axes[a], (x, y)), axes[a], c)
            for j, chip in enumerate(_other_chips(x, y)):
                _rcopy(mine, mine, send_sems.at[3 * a + j], recv_sems.at[3 * a + j], (*chip, c)).start()
        token[...] = jnp.zeros_like(token)

    dma = pltpu.SemaphoreType.DMA
    outs = pl.pallas_call(
        body, name="gather_start_" + tag,
        in_specs=[HBM] * na,
        out_specs=(SEM, SEM, *[HBM] * na, pl.BlockSpec(memory_space=pltpu.VMEM)),
        out_shape=(dma((3 * na,)), dma((3 * na,)), *[pltpu.HBM(b.shape, b.dtype) for b in bufs],
                   jax.ShapeDtypeStruct((8, 128), F32)),
        input_output_aliases={a: 2 + a for a in range(na)},
        compiler_params=SPLIT_COPY,
    )(*[_in_hbm(b) for b in bufs])
    return outs[0], outs[1], list(outs[2:2 + na]), outs[2 + na]


def gather_wait(send_sems, recv_sems, bufs, axes, after, tag):
    na = len(bufs)

    def body(*refs):
        ins = refs[:na]
        send_sems, recv_sems = refs[na], refs[na + 1]
        x, y, c = _place()
        for a in range(na):
            for j, chip in enumerate(_other_chips(x, y)):
                got = _half(_window(ins[a], axes[a], chip), axes[a], c)
                _rcopy(got, got, send_sems.at[3 * a + j], recv_sems.at[3 * a + j], (*chip, c)).wait_recv()
        for a in range(na):
            mine = _half(_window(ins[a], axes[a], (x, y)), axes[a], c)
            for j, chip in enumerate(_other_chips(x, y)):
                _rcopy(mine, mine, send_sems.at[3 * a + j], recv_sems.at[3 * a + j], (*chip, c)).wait_send()

    return pl.pallas_call(
        body, name="gather_wait_" + tag,
        in_specs=[HBM] * na + [SEM, SEM] + [ANY] * len(after),
        out_specs=[HBM] * na,
        out_shape=[pltpu.HBM(b.shape, b.dtype) for b in bufs],
        input_output_aliases={a: a for a in range(na)},
        compiler_params=SPLIT_COPY,
    )(*bufs, send_sems, recv_sems, *after)


def gather_forward(bufs, axes):
    na = len(bufs)

    def body(*refs):
        outs = refs[na:2 * na]
        send_sems, recv_sems = refs[2 * na:]
        x, y, c = _place()
        chips = _other_chips(x, y)
        passed = []
        for a in range(na):
            for j, chip in enumerate(chips):
                got = _half(_window(outs[a], axes[a], chip), axes[a], c)
                cp = _rcopy(got, got, send_sems.at[3 * a + j], recv_sems.at[3 * a + j], (x, y, 1 - c))
                cp.start()
                passed.append(cp)
        for a in range(na):
            for j, chip in enumerate(chips):
                got = _half(_window(outs[a], axes[a], chip), axes[a], 1 - c)
                _rcopy(got, got, send_sems.at[3 * a + j], recv_sems.at[3 * a + j], (x, y, 1 - c)).wait_recv()
        for cp in passed:
            cp.wait_send()

    dma = pltpu.SemaphoreType.DMA
    return pl.pallas_call(
        body, name="gather_forward",
        in_specs=[ANY] * na, out_specs=[ANY] * na,
        out_shape=[jax.ShapeDtypeStruct(b.shape, BF16) for b in bufs],
        input_output_aliases={a: a for a in range(na)},
        scratch_shapes=[dma((3 * na,)), dma((3 * na,))],
    )(*bufs)


def sibling_start(bufs, views, n_copies, tag):
    nb = len(bufs)

    def body(*refs):
        send_sems, recv_sems = refs[nb], refs[nb + 1]
        token = refs[2 * nb + 2]
        x, y, c = _place()
        for k, (src, dst, _) in enumerate(views(refs[:nb], c)):
            _rcopy(src, dst, send_sems.at[k], recv_sems.at[k], (x, y, 1 - c)).start()
        token[...] = jnp.zeros_like(token)

    dma = pltpu.SemaphoreType.DMA
    outs = pl.pallas_call(
        body, name="sibling_start_" + tag,
        in_specs=[HBM] * nb,
        out_specs=(SEM, SEM, *[HBM] * nb, pl.BlockSpec(memory_space=pltpu.VMEM)),
        out_shape=(dma((n_copies,)), dma((n_copies,)), *[pltpu.HBM(b.shape, b.dtype) for b in bufs],
                   jax.ShapeDtypeStruct((8, 128), F32)),
        input_output_aliases={a: 2 + a for a in range(nb)},
        compiler_params=SPLIT_COPY,
    )(*[_in_hbm(b) for b in bufs])
    return outs[0], outs[1], list(outs[2:2 + nb]), outs[2 + nb]


def sibling_wait(send_sems, recv_sems, bufs, views, after, tag):
    nb = len(bufs)

    def body(*refs):
        send_sems, recv_sems = refs[nb], refs[nb + 1]
        x, y, c = _place()
        trip = views(refs[:nb], c)
        for k, (src, _, got) in enumerate(trip):
            _rcopy(src, got, send_sems.at[k], recv_sems.at[k], (x, y, 1 - c)).wait_recv()
        for k, (src, dst, _) in enumerate(trip):
            _rcopy(src, dst, send_sems.at[k], recv_sems.at[k], (x, y, 1 - c)).wait_send()

    return list(pl.pallas_call(
        body, name="sibling_wait_" + tag,
        in_specs=[HBM] * nb + [SEM, SEM] + [ANY] * len(after),
        out_specs=[HBM] * nb,
        out_shape=[pltpu.HBM(b.shape, b.dtype) for b in bufs],
        input_output_aliases={a: a for a in range(nb)},
        compiler_params=SPLIT_COPY,
    )(*bufs, send_sems, recv_sems, *after))


def exchange_views(axes):
    na = len(axes)

    def views(refs, c):
        return [(_half(refs[a], axes[a], 1 - c), refs[na + a], refs[na + a]) for a in range(na)]

    return views


def join_views(regions, axes):
    def views(refs, c):
        out = []
        for a, ref in enumerate(refs):
            reg = ref.at[pl.ds(*regions[a])]
            mine = _half(reg, axes[a], c)
            out.append((mine, mine, _half(reg, axes[a], 1 - c)))
        return out

    return views


def exchange_halves(grads, axes):
    na = len(grads)

    def hshape(g, ax):
        ha = 3 - ax
        return tuple(d // 2 if i == ha else d for i, d in enumerate(g.shape))

    def body(*refs):
        ins, outs = refs[:na], refs[na:2 * na]
        send_sems, recv_sems = refs[2 * na:]
        x, y, c = _place()
        cps = []
        for a in range(na):
            ha = 3 - axes[a]
            hs = ins[a].shape[ha] // 2
            cp = _rcopy(_sl(ins[a], ha, (1 - c) * hs, hs), outs[a], send_sems.at[a], recv_sems.at[a], (x, y, 1 - c))
            cp.start()
            cps.append(cp)
        for cp in cps:
            cp.wait_recv()
        for cp in cps:
            cp.wait_send()

    dma = pltpu.SemaphoreType.DMA
    return pl.pallas_call(
        body, name="exchange_halves",
        in_specs=[ANY] * na, out_specs=[ANY] * na,
        out_shape=[jax.ShapeDtypeStruct(hshape(g, ax), BF16) for g, ax in zip(grads, axes)],
        scratch_shapes=[dma((na,)), dma((na,))],
    )(*grads)


def scatter_start(halves, axes, tag):
    na = len(halves)

    def pshape(h, ax):
        return (N_CHIPS - 1,) + tuple(d // N_CHIPS if i == ax else d for i, d in enumerate(h.shape))

    def body(*refs):
        ins, lands = refs[:na], refs[na:2 * na]
        send_sems, recv_sems = refs[2 * na], refs[2 * na + 1]
        token = refs[4 * na + 2]
        x, y, c = _place()
        for a in range(na):
            for j, chip in enumerate(_other_chips(x, y)):
                _rcopy(_window(ins[a], axes[a], chip), lands[a].at[j],
                       send_sems.at[3 * a + j], recv_sems.at[3 * a + j], (*chip, c)).start()
        token[...] = jnp.zeros_like(token)

    dma = pltpu.SemaphoreType.DMA
    lands = [lax.empty(pshape(h, ax), BF16) for h, ax in zip(halves, axes)]
    outs = pl.pallas_call(
        body, name="scatter_start_" + tag,
        in_specs=[HBM] * (2 * na),
        out_specs=(SEM, SEM, *[HBM] * (2 * na), pl.BlockSpec(memory_space=pltpu.VMEM)),
        out_shape=(dma((3 * na,)), dma((3 * na,)), *[pltpu.HBM(b.shape, b.dtype) for b in halves + lands],
                   jax.ShapeDtypeStruct((8, 128), F32)),
        input_output_aliases={a: 2 + a for a in range(2 * na)},
        compiler_params=SPLIT_COPY,
    )(*[_in_hbm(b) for b in halves + lands])
    return outs[0], outs[1], list(outs[2:2 + na]), list(outs[2 + na:2 + 2 * na]), outs[2 + 2 * na]


def scatter_wait(send_sems, recv_sems, halves, lands, axes, after, tag):
    na = len(halves)

    def body(*refs):
        ins, lands = refs[:na], refs[na:2 * na]
        send_sems, recv_sems = refs[2 * na], refs[2 * na + 1]
        x, y, c = _place()
        for a in range(na):
            for j, chip in enumerate(_other_chips(x, y)):
                _rcopy(_window(ins[a], axes[a], chip), lands[a].at[j],
                       send_sems.at[3 * a + j], recv_sems.at[3 * a + j], (*chip, c)).wait_recv()
        for a in range(na):
            for j, chip in enumerate(_other_chips(x, y)):
                _rcopy(_window(ins[a], axes[a], chip), lands[a].at[j],
                       send_sems.at[3 * a + j], recv_sems.at[3 * a + j], (*chip, c)).wait_send()

    outs = pl.pallas_call(
        body, name="scatter_wait_" + tag,
        in_specs=[HBM] * (2 * na) + [SEM, SEM] + [ANY] * len(after),
        out_specs=[HBM] * (2 * na),
        out_shape=[pltpu.HBM(b.shape, b.dtype) for b in halves + lands],
        input_output_aliases={a: a for a in range(2 * na)},
        compiler_params=SPLIT_COPY,
    )(*halves, *lands, send_sems, recv_sems, *after)
    return list(outs[:na]), list(outs[na:])


def join_halves(gs, regions, axes):
    na = len(gs)

    def body(*refs):
        outs = refs[na:2 * na]
        send_sems, recv_sems = refs[2 * na:]
        x, y, c = _place()
        cps = []
        for a in range(na):
            ha = 3 - axes[a]
            hs = outs[a].shape[ha] // 2
            reg = outs[a].at[pl.ds(*regions[a])]
            mine = _sl(reg, ha, c * hs, hs)
            cp = _rcopy(mine, mine, send_sems.at[a], recv_sems.at[a], (x, y, 1 - c))
            cp.start()
            cps.append((cp, _sl(reg, ha, (1 - c) * hs, hs)))
        for a, (cp, theirs) in enumerate(cps):
            _rcopy(theirs, theirs, send_sems.at[a], recv_sems.at[a], (x, y, 1 - c)).wait_recv()
        for cp, _ in cps:
            cp.wait_send()

    dma = pltpu.SemaphoreType.DMA
    return pl.pallas_call(
        body, name="join_halves",
        in_specs=[ANY] * na, out_specs=[ANY] * na,
        out_shape=[jax.ShapeDtypeStruct(g.shape, F32) for g in gs],
        input_output_aliases={a: a for a in range(na)},
        scratch_shapes=[dma((na,)), dma((na,))],
    )(*gs)


def _tile2(r, c, itemsize, limit):
    bc = c
    while bc > 1536:
        bc //= 2
    assert c % bc == 0 and bc % 128 == 0
    br = r
    while br * bc * itemsize > limit and br % 32 == 0:
        br //= 2
    assert r % br == 0 and br % 16 == 0
    return br, bc


def add_my_half(g, theirs, ax, cc):
    ha = 3 - ax
    nl, r, c = theirs.shape
    br, bc = _tile2(r, c, 2, 2 << 20)
    nrb, ncb = r // br, c // bc

    def body(cc_ref, g_ref, t_ref, o_ref):
        o_ref[...] = (g_ref[...].astype(F32) + t_ref[...].astype(F32)).astype(BF16)

    def gmap(l, i, j, cc_ref):
        return (l, cc_ref[0] * nrb + i, j) if ha == 1 else (l, i, cc_ref[0] * ncb + j)

    blk = pl.BlockSpec((1, br, bc), lambda l, i, j, cc_ref: (l, i, j))
    return pl.pallas_call(
        body, name="add_my_half",
        grid_spec=pltpu.PrefetchScalarGridSpec(
            num_scalar_prefetch=1, grid=(nl, nrb, ncb),
            in_specs=[pl.BlockSpec((1, br, bc), gmap), blk], out_specs=blk),
        out_shape=_mid(theirs.shape, BF16),
        compiler_params=_cparams(32, 3),
    )(cc, *_pin(g, theirs))


def sum_chips(parts, pair, gstack, l0, ax, where):
    _, n, r, c = parts.shape
    ha = 3 - ax
    br, bc = _tile2(r, c, 2, 1 << 20)
    nrb, ncb = r // br, c // bc

    def body(w_ref, p_ref, own_ref, g_ref, o_ref):
        acc = own_ref[...].astype(F32)
        for q in range(N_CHIPS - 1):
            acc = acc + p_ref[q].astype(F32)
        o_ref[...] = acc

    def own_map(l, i, j, w_ref):
        return (l, w_ref[0] * nrb + i, j) if ax == 1 else (l, i, w_ref[0] * ncb + j)

    def out_map(l, i, j, w_ref):
        return (l0 + l, w_ref[1] * nrb + i, j) if ha == 1 else (l0 + l, i, w_ref[1] * ncb + j)

    return pl.pallas_call(
        body, name="sum_chips",
        grid_spec=pltpu.PrefetchScalarGridSpec(
            num_scalar_prefetch=1, grid=(n, nrb, ncb),
            in_specs=[pl.BlockSpec((N_CHIPS - 1, 1, br, bc), lambda l, i, j, w_ref: (0, l, i, j)),
                      pl.BlockSpec((1, br, bc), own_map), ANY],
            out_specs=pl.BlockSpec((1, br, bc), out_map)),
        out_shape=jax.ShapeDtypeStruct(gstack.shape, F32),
        input_output_aliases={3: 0},
        compiler_params=_cparams(32, 3),
    )(where, *_pin(parts, pair, gstack))


BIG = ("w_ffn_up", "w_ffn_down", "attn_w_qkv", "attn_w_o", "conv_w_in", "conv_w_out")
BIG_AXIS = {"w_ffn_up": 2, "w_ffn_down": 1, "attn_w_qkv": 2, "attn_w_o": 1, "conv_w_in": 2, "conv_w_out": 1}
WEIGHTS = ("norm_gain", "w_ada", "b_ada", "w_ffn_up", "w_ffn_down", "attn_w_qkv", "attn_b_qkv", "attn_q_gain",
           "attn_k_gain", "attn_sinks", "attn_w_o", "attn_b_o", "conv_w_in", "conv_w", "conv_w_out")
N_SMALL_ROWS = 112


def _stack3(a):
    return a.reshape((-1,) + a.shape[-2:])


def _head_matrices():
    lane = jnp.arange(QK_DIM)
    head = lane // HEAD_DIM
    col = jnp.arange(128)
    g1 = jnp.where(head[:, None] == col[None, :], 1.0 / HEAD_DIM, 0.0).astype(BF16)
    g2 = jnp.where(col[:, None] == head[None, :], 1.0, 0.0).astype(BF16)
    fold = lane % HEAD_DIM + jnp.where(head >= N_HEADS, HEAD_DIM, 0)
    gsel = jnp.where(fold[:, None] == col[None, :], 1.0, 0.0).astype(BF16)
    return g1, g2, gsel


def _pad_cols(a, n):
    return jnp.pad(a, ((0, 0), (0, n - a.shape[1])))


def kernel(x, c, positions, norm_gain, w_ada, b_ada, w_ffn_up, w_ffn_down, attn_w_qkv, attn_b_qkv, attn_q_gain, attn_k_gain, attn_sinks, attn_w_o, attn_b_o, conv_w_in, conv_w, conv_w_out, loss_target, m_norm_gain, m_w_ada, m_b_ada, m_w_ffn_up, m_w_ffn_down, m_attn_w_qkv, m_attn_b_qkv, m_attn_q_gain, m_attn_k_gain, m_attn_sinks, m_attn_w_o, m_attn_b_o, m_conv_w_in, m_conv_w, m_conv_w_out, v_norm_gain, v_w_ada, v_b_ada, v_w_ffn_up, v_w_ffn_down, v_attn_w_qkv, v_attn_b_qkv, v_attn_q_gain, v_attn_k_gain, v_attn_sinks, v_attn_w_o, v_attn_b_o, v_conv_w_in, v_conv_w, v_conv_w_out):
    w = dict(norm_gain=norm_gain, w_ada=w_ada, b_ada=b_ada, w_ffn_up=w_ffn_up, w_ffn_down=w_ffn_down,
             attn_w_qkv=attn_w_qkv, attn_b_qkv=attn_b_qkv, attn_q_gain=attn_q_gain, attn_k_gain=attn_k_gain,
             attn_sinks=attn_sinks, attn_w_o=attn_w_o, attn_b_o=attn_b_o, conv_w_in=conv_w_in, conv_w=conv_w,
             conv_w_out=conv_w_out)
    mom = dict(norm_gain=m_norm_gain, w_ada=m_w_ada, b_ada=m_b_ada, w_ffn_up=m_w_ffn_up, w_ffn_down=m_w_ffn_down,
               attn_w_qkv=m_attn_w_qkv, attn_b_qkv=m_attn_b_qkv, attn_q_gain=m_attn_q_gain,
               attn_k_gain=m_attn_k_gain, attn_sinks=m_attn_sinks, attn_w_o=m_attn_w_o, attn_b_o=m_attn_b_o,
               conv_w_in=m_conv_w_in, conv_w=m_conv_w, conv_w_out=m_conv_w_out)
    var = dict(norm_gain=v_norm_gain, w_ada=v_w_ada, b_ada=v_b_ada, w_ffn_up=v_w_ffn_up, w_ffn_down=v_w_ffn_down,
               attn_w_qkv=v_attn_w_qkv, attn_b_qkv=v_attn_b_qkv, attn_q_gain=v_attn_q_gain,
               attn_k_gain=v_attn_k_gain, attn_sinks=v_attn_sinks, attn_w_o=v_attn_w_o, attn_b_o=v_attn_b_o,
               conv_w_in=v_conv_w_in, conv_w=v_conv_w, conv_w_out=v_conv_w_out)

    xi, yi, ci = _place()
    chip = 2 * xi + yi
    dev = 4 * xi + 2 * yi + ci
    nex, s_len, _ = x.shape
    t = nex * s_len
    n_attn, n_conv = attn_w_qkv.shape[0], conv_w_in.shape[0]
    axes = [BIG_AXIS[n] for n in BIG]

    c_all = allgather_small(jnp.pad(c, ((0, 8 - nex), (0, 0))), True)[:, :nex].reshape(8 * nex, D)
    ada_cols = w_ada.shape[2]
    modp = ada_fwd(c_all, w_ada)
    modg = allgather_small(modp.reshape(DEPTH * 8 * nex, ada_cols), False)
    modg = lax.dynamic_slice_in_dim(modg.reshape(N_CHIPS, DEPTH, 8 * nex, ada_cols), dev * nex, nex, axis=2)
    modg = modg.transpose(1, 2, 0, 3).reshape(DEPTH, nex, 9 * D)
    mod = add_bias(modg, b_ada.reshape(DEPTH, 1, 9 * D)).reshape(DEPTH, nex, 9, D)

    small = jnp.concatenate([norm_gain.reshape(DEPTH * 3, -1), conv_w.reshape(n_conv * 3, -1)], axis=0)
    small = jnp.pad(small, ((0, -small.shape[0] % 8), (0, 0)))
    small = allgather_small(small, False).transpose(1, 0, 2).reshape(small.shape[0], D)
    gain_full = small[:DEPTH * 3].reshape(DEPTH, 3, D)
    convw_full = small[DEPTH * 3:DEPTH * 3 + n_conv * 3].reshape(n_conv, 3, D)

    chip_arr = chip.reshape(1).astype(jnp.int32)
    where = jnp.stack([chip, ci]).astype(jnp.int32)
    stacks = [_stack3(w[n]) for n in BIG]

    def mixer(i):
        return (2, 3) if i % 2 == 0 else (4, 5)

    groups = [[(0, 0, 1), (1, 0, 1)], [(mixer(0)[0], 0, 1), (mixer(0)[1], 0, 1), (0, 1, 1), (1, 1, 1)]]
    groups += [[(0, 2 * i, 2), (1, 2 * i, 2), (mixer(i)[0], i // 2, 1), (mixer(i)[1], i // 2, 1)]
               for i in range(1, DEPTH)]
    gaxes = [[axes[b] for b, _, _ in grp] for grp in groups]
    where_is = {(b, l0 + k): (g, a, k) for g, grp in enumerate(groups) for a, (b, l0, n) in enumerate(grp)
                for k in range(n)}
    in_flight, token = [], (mod, small)
    for g, grp in enumerate(groups):
        bufs = [cast_into_window(stacks[b], l0, n, axes[b], chip_arr, token) for b, l0, n in grp]
        ssem, rsem, bufs, tok = gather_start(bufs, gaxes[g], f"g{g}")
        in_flight.append((ssem, rsem, bufs))
        token = (tok,)

    invf = ROPE_THETA ** (-jnp.arange(0, HEAD_DIM, 2, dtype=F32) / HEAD_DIM)
    cos, sin = rope_tables(positions.reshape(t, 1), jnp.tile(invf, 4).reshape(1, 128))
    g1, g2, gsel = _head_matrices()
    gqk = [jnp.concatenate([jnp.tile(attn_q_gain[j], N_HEADS), jnp.tile(attn_k_gain[j], N_KV)]).reshape(1, QK_DIM)
           for j in range(n_attn)]
    zero_bias = jnp.zeros((1, D), F32)

    xs = x.reshape(t, D)
    saved, ready = [], {}

    def weight(b, l, after):
        g, a, k = where_is[(b, l)]
        if g not in ready:
            ssem, rsem, bufs = in_flight[g]
            ready[g] = gather_forward(gather_wait(ssem, rsem, bufs, gaxes[g], after, f"g{g}"), gaxes[g])
        return ready[g][a], k

    for i in range(DEPTH):
        j = i // 2
        gn, md = gain_full[i], mod[i]
        x0 = xs
        wup, k = weight(0, 2 * i, token if i == 0 else (xs,))
        wdn, _ = weight(1, 2 * i, ())
        xs, u1, f1 = ffn_fwd(x0, gn[0:1], md, wup, wdn, k, 0)
        x1 = xs
        wmi, k = weight(mixer(i)[0], j, (xs,))
        wmo, _ = weight(mixer(i)[1], j, ())
        if i % 2 == 0:
            raw, qr, kr = qkv_fwd(x1, gn[1:2], md, wmi, k, attn_b_qkv[j:j + 1], gqk[j], cos, sin, g1, g2)
            o, lse = attn_fwd(qr, kr, raw, attn_sinks[j:j + 1])
            xs, ymix = proj_res(x1, o, wmo, k, attn_b_o[j:j + 1], md)
            mix = (raw, qr, kr, o, lse)
        else:
            z = lin_fwd(x1, gn[1:2], md, wmi, k, 1)
            p = conv_fwd(z, convw_full[j])
            xs, ymix = proj_res(x1, p, wmo, k, zero_bias, md)
            mix = (z, p)
        x2 = xs
        wup, k = weight(0, 2 * i + 1, (xs,))
        wdn, _ = weight(1, 2 * i + 1, ())
        xs, u3, f3 = ffn_fwd(x2, gn[2:3], md, wup, wdn, k, 2)
        saved.append((x0, u1, f1, x1, mix, ymix, x2, u3, f3))
    dy, lpart = loss_grad(xs, loss_target.reshape(t, D))

    cc = ci.reshape(1).astype(jnp.int32)
    gshard = [lax.empty(s.shape, F32) for s in stacks]
    upd = [[lax.empty(s.shape, F32) for _ in range(3)] for s in stacks]
    mstacks = [_stack3(mom[n]) for n in BIG]
    vstacks = [_stack3(var[n]) for n in BIG]

    ggrad = {g: [lax.empty(buf.shape, BF16) for buf in ready[g]] for g in range(len(groups))}
    missing = {g: sum(n for _, _, n in grp) for g, grp in enumerate(groups)}
    state = dict(to_sibling=None, on_ici=None, joining=None, token=())

    def half_shape(shape, ax):
        return tuple(d // 2 if i == 3 - ax else d for i, d in enumerate(shape))

    def advance(after, last=False):
        if state["joining"] is not None:
            g, ssem, rsem, bufs = state["joining"]
            grp = groups[g]
            regions = [(l0, n) for _, l0, n in grp]
            joined = sibling_wait(ssem, rsem, bufs, join_views(regions, gaxes[g]), after, f"join_g{g}")
            for (b, l0, n), gj in zip(grp, joined):
                gshard[b] = gj
                upd[b] = adamw_layers(stacks[b], gj, mstacks[b], vstacks[b], upd[b], l0, n)
            state["joining"] = None
        started = None
        if state["to_sibling"] is not None:
            g, ssem, rsem, bufs = state["to_sibling"]
            na = len(groups[g])
            bufs = sibling_wait(ssem, rsem, bufs, exchange_views(gaxes[g]), after, f"xchg_g{g}")
            pair = [add_my_half(gr, th, ax, cc) for gr, th, ax in zip(bufs[:na], bufs[na:], gaxes[g])]
            ssem, rsem, pair, lands, tok = scatter_start(pair, gaxes[g], f"g{g}")
            started = (g, ssem, rsem, pair, lands)
            state.update(to_sibling=None, token=(tok,))
        if state["on_ici"] is not None and (started is not None or last):
            g, ssem, rsem, pair, lands = state["on_ici"]
            pair, lands = scatter_wait(ssem, rsem, pair, lands, gaxes[g], state["token"] + after, f"g{g}")
            grp = groups[g]
            for a, (b, l0, n) in enumerate(grp):
                gshard[b] = sum_chips(lands[a], pair[a], gshard[b], l0, axes[b], where)
            regions = [(l0, n) for _, l0, n in grp]
            ssem, rsem, bufs, tok = sibling_start([gshard[b] for b, _, _ in grp], join_views(regions, gaxes[g]),
                                                  len(grp), f"join_g{g}")
            for (b, _, _), buf in zip(grp, bufs):
                gshard[b] = buf
            state.update(joining=(g, ssem, rsem, bufs), on_ici=None, token=(tok,))
        if started is not None:
            state["on_ici"] = started

    def put(b, l, lhs, rhs, bm, bn):
        g, a, k = where_is[(b, l)]
        ggrad[g][a] = wgrad(ggrad[g][a], k, lhs, rhs, bm, bn, after=state["token"])
        state["token"] = ()
        missing[g] -= 1
        if missing[g] == 0:
            lands = [lax.empty(half_shape(gr.shape, ax), BF16) for gr, ax in zip(ggrad[g], gaxes[g])]
            ssem, rsem, bufs, tok = sibling_start(ggrad[g] + lands, exchange_views(gaxes[g]), len(lands), f"xchg_g{g}")
            state.update(to_sibling=(g, ssem, rsem, bufs), token=(tok,))

    dmod = [None] * DEPTH
    dgain = [None] * DEPTH
    db_qkv, dqk_gain, dsinks, db_o, dconv_w = ([None] * n_attn, [None] * n_attn, [None] * n_attn,
                                                [None] * n_attn, [None] * n_conv)
    for i in reversed(range(DEPTH)):
        j = i // 2
        gn, md = gain_full[i], mod[i]
        x0, u1, f1, x1, mix, ymix, x2, u3, f3 = saved[i]
        (wup, k), (wdn, _) = weight(0, 2 * i + 1, ()), weight(1, 2 * i + 1, ())
        dy, du, a, df, h, se3, sa3 = ffn_bwd(dy, u3, f3, x2, gn[2:3], md, wup, wdn, k, 2, after=state["token"])
        advance((dy,))
        put(0, 2 * i + 1, h, du, D, D_FF)
        put(1, 2 * i + 1, a, df, D_FF, D)
        (wmi, k), (wmo, _) = weight(mixer(i)[0], j, ()), weight(mixer(i)[1], j, ())
        if i % 2 == 0:
            raw, qr, kr, o, lse = mix
            dyy, do, sp, sb = proj_res_bwd(dy, ymix, wmo, k, md)
            put(mixer(i)[1], j, o, dyy, D, D)
            dq, dk, dv, dsinks[j] = attn_bwd(qr, kr, raw, attn_sinks[j:j + 1], o, do, lse)
            dz, dqk_gain[j] = qkv_bwd_pre(dq, dk, dv, raw, gqk[j], cos, sin, g1, g2, gsel)
            dy, h, se2, sa2, db_qkv[j] = lin_bwd(dy, dz, wmi, k, x1, gn[1:2], md, 1, True)
            put(mixer(i)[0], j, h, dz, D, QKV_DIM)
            db_o[j] = sb
        else:
            z, p = mix
            dyy, dp, sp, _ = proj_res_bwd(dy, ymix, wmo, k, md)
            put(mixer(i)[1], j, p, dyy, D, D)
            dz, dconv_w[j] = conv_bwd(z, convw_full[j], dp)
            dy, h, se2, sa2 = lin_bwd(dy, dz, wmi, k, x1, gn[1:2], md, 1, False)
            put(mixer(i)[0], j, h, dz, D, 1536)
        (wup, k), (wdn, _) = weight(0, 2 * i, ()), weight(1, 2 * i, ())
        dy, du, a, df, h, se1, sa1 = ffn_bwd(dy, u1, f1, x0, gn[0:1], md, wup, wdn, k, 0, after=state["token"])
        advance((dy,))
        put(0, 2 * i, h, du, D, D_FF)
        put(1, 2 * i, a, df, D_FF, D)
        dmod[i] = jnp.stack([se1[:, 0], se1[:, 1], se1[:, 2], se2[:, 0], se2[:, 1], sp[:, 0],
                             se3[:, 0], se3[:, 1], se3[:, 2]], axis=1)
        dgain[i] = jnp.stack([sa1[0], sa2[0], sa3[0]], axis=0)
    grad_x = dy.reshape(x.shape)
    advance((dy,))

    dmod_ex = jnp.stack(dmod, axis=1).reshape(nex, DEPTH * 9, D)
    dmod_ex = jnp.pad(dmod_ex, ((0, 0), (0, N_DMOD_ROWS - DEPTH * 9), (0, 0))).reshape(nex * N_DMOD_ROWS, D)
    misc = jnp.concatenate([dqk_gain[jj][0] for jj in range(n_attn)]
                           + [jnp.pad(dsinks[jj][0], (0, 128 - N_HEADS)) for jj in range(n_attn)]
                           + [lpart[0]])
    rows = [dmod_ex,
            jnp.concatenate(dgain, axis=0), jnp.zeros((4, D), F32),
            jnp.concatenate([_pad_cols(db_qkv[jj][0:1], 2 * D).reshape(2, D) for jj in range(n_attn)], axis=0),
            jnp.concatenate([db_o[jj][0:1] for jj in range(n_attn)], axis=0),
            jnp.concatenate([dconv_w[jj][0:3] for jj in range(n_conv)], axis=0),
            jnp.pad(misc, (0, D - misc.shape[0])).reshape(1, D)]
    packed = jnp.concatenate(rows, axis=0)
    packed = jnp.pad(packed, ((0, N_SMALL_ROWS - packed.shape[0]), (0, 0)))
    p_all = allgather_small(packed, True)
    red, exsum = reduce_small(p_all)

    r0 = nex * N_DMOD_ROWS
    grads = {}
    grads["b_ada"] = exsum[:DEPTH * 9].reshape(DEPTH, 9 * D)
    grads["norm_gain"] = lax.dynamic_slice_in_dim(red[r0:r0 + 12].reshape(DEPTH, 3, D), chip * (D // N_CHIPS),
                                                  D // N_CHIPS, axis=2)
    r1 = r0 + 16
    grads["attn_b_qkv"] = red[r1:r1 + 2 * n_attn].reshape(n_attn, 2 * D)[:, :QKV_DIM]
    r2 = r1 + 2 * n_attn
    grads["attn_b_o"] = red[r2:r2 + n_attn]
    r3 = r2 + n_attn
    grads["conv_w"] = lax.dynamic_slice_in_dim(red[r3:r3 + 3 * n_conv].reshape(n_conv, 3, D), chip * (D // N_CHIPS),
                                               D // N_CHIPS, axis=2)
    mrow = red[r3 + 3 * n_conv]
    grads["attn_q_gain"] = jnp.stack([mrow[128 * jj:128 * jj + HEAD_DIM] for jj in range(n_attn)])
    grads["attn_k_gain"] = jnp.stack([mrow[128 * jj + HEAD_DIM:128 * jj + 128] for jj in range(n_attn)])
    grads["attn_sinks"] = jnp.stack([mrow[128 * (n_attn + jj):128 * (n_attn + jj) + N_HEADS] for jj in range(n_attn)])
    loss = mrow[128 * 2 * n_attn]

    dm_all = p_all[:, :r0].reshape(8, nex, N_DMOD_ROWS, D)[:, :, :DEPTH * 9].reshape(8 * nex, DEPTH, 9 * D)
    dm_mine = lax.dynamic_slice_in_dim(dm_all.transpose(1, 0, 2), chip * ada_cols, ada_cols, axis=2)
    g_ada, d_ada, nm_ada, nv_ada = ada_bwd_adam(c_all, dm_mine, w_ada, m_w_ada, v_w_ada)

    delta, new_m, new_v = {}, {}, {}
    for n in WEIGHTS:
        if n == "w_ada":
            grads[n], delta[n], new_m[n], new_v[n] = g_ada, d_ada, nm_ada, nv_ada
        elif n not in BIG:
            delta[n], new_m[n], new_v[n] = adamw(w[n], grads[n], mom[n], var[n])
    advance((delta["conv_w"], d_ada), last=True)
    advance(state["token"], last=True)
    assert all(state[k] is None for k in ("to_sibling", "on_ici", "joining"))
    for b, n in enumerate(BIG):
        grads[n] = gshard[b].reshape(w[n].shape)
        delta[n], new_m[n], new_v[n] = (u.reshape(w[n].shape) for u in upd[b])

    return (loss, grad_x, *[grads[n] for n in WEIGHTS], *[delta[n] for n in WEIGHTS],
            *[new_m[n] for n in WEIGHTS], *[new_v[n] for n in WEIGHTS])
```

```python
import jax
import jax.numpy as jnp
from jax import lax
from jax.experimental import pallas as pl
from jax.experimental.pallas import tpu as pltpu

F32 = jnp.float32
BF16 = jnp.bfloat16

D = 1024
D_FF = 2816
N_HEADS = 16
N_KV = 4
HEAD_DIM = 64
GROUP = N_HEADS // N_KV
QK_DIM = (N_HEADS + N_KV) * HEAD_DIM
QKV_DIM = QK_DIM + N_KV * HEAD_DIM
BLOCK = 128
ROPE_THETA = 10000.0
EPS = 1e-6
DEPTH = 4
N_CHIPS = 4

ADAM_LR = 0.001
ADAM_B1 = 0.9
ADAM_B2 = 0.999
ADAM_EPS = 1e-08
ADAM_WD = 0.01
ADAM_STEP = 10

V7X_VMEM_BYTES = 64 * 1024 * 1024
V7X_MXU_DIM = 256
FF_CHUNKS = ((0, 1536), (1536, D_FF))
assert all((hi - lo) % V7X_MXU_DIM == 0 for lo, hi in FF_CHUNKS)
MESH = pl.DeviceIdType.MESH
ANY = pl.BlockSpec(memory_space=pl.ANY)


def _cparams(vmem_mb, n_grid):
    assert vmem_mb * 1024 * 1024 <= V7X_VMEM_BYTES
    return pltpu.CompilerParams(vmem_limit_bytes=vmem_mb * 1024 * 1024,
                                dimension_semantics=("arbitrary",) * n_grid)


def _resident(shape):
    nd = len(shape)
    return pl.BlockSpec(shape, lambda *_: (0,) * nd, pipeline_mode=pl.Buffered(1))


def _layer(w, l):
    return pl.BlockSpec((None,) + w.shape[1:], lambda *_: (l, 0, 0), pipeline_mode=pl.Buffered(1))


PIN_BYTES = 1 << 20


def _pin(*args):
    return [pltpu.with_memory_space_constraint(a, pltpu.HBM) if a.size * a.dtype.itemsize >= PIN_BYTES else a
            for a in args]


def _mid(shape, dtype):
    n = 1
    for d in shape:
        n *= d
    if n * jnp.dtype(dtype).itemsize >= PIN_BYTES:
        return pltpu.HBM(tuple(shape), dtype)
    return jax.ShapeDtypeStruct(tuple(shape), dtype)


def _dot(a, b):
    return jnp.dot(a, b, preferred_element_type=F32)


def _dot_nt(a, b):
    return lax.dot_general(a, b, (((1,), (1,)), ((), ())), preferred_element_type=F32)


def _dot_tn(a, b):
    return lax.dot_general(a, b, (((0,), (0,)), ((), ())), preferred_element_type=F32)


def _dot_hilo(a, g):
    hi = a.astype(BF16)
    lo = (a - hi.astype(F32)).astype(BF16)
    return _dot(hi, g) + _dot(lo, g)


def _sigmoid(x):
    return 0.5 * jnp.tanh(0.5 * x) + 0.5


def _colsum(a):
    return jnp.sum(a, axis=0, keepdims=True)


def _norm_mod(x, gain, sc, sh):
    r = lax.rsqrt(jnp.mean(x * x, axis=-1, keepdims=True) + EPS)
    n = x * r * gain
    return r, n, n * (1.0 + sc) + sh


def _mod_rows(mod_ref, s):
    return (mod_ref[0, 3 * s:3 * s + 1, :], mod_ref[0, 3 * s + 1:3 * s + 2, :], mod_ref[0, 3 * s + 2:3 * s + 3, :])


def ffn_fwd(x, gain, mod, wup, wdn, l, s, tm=512, after=()):
    t = x.shape[0]
    tpe = t // tm // 2

    def body(x_ref, gain_ref, mod_ref, wup_ref, wdn_ref, *rest):
        xo_ref, u_ref, f_ref = rest[len(after):]
        xv = x_ref[...]
        sh, sc, g = _mod_rows(mod_ref, s)
        _, _, h = _norm_mod(xv, gain_ref[...], sc, sh)
        hb = h.astype(BF16)
        acc = jnp.zeros((tm, D), F32)
        for lo, hi in FF_CHUNKS:
            gate = _dot(hb, wup_ref[:, lo:hi])
            up = _dot(hb, wup_ref[:, D_FF + lo:D_FF + hi])
            u_ref[:, lo:hi] = gate.astype(BF16)
            u_ref[:, D_FF + lo:D_FF + hi] = up.astype(BF16)
            a = (gate * _sigmoid(gate) * up).astype(BF16)
            acc = acc + _dot(a, wdn_ref[lo:hi, :])
        f_ref[...] = acc.astype(BF16)
        xo_ref[...] = xv + 0.5 * g * acc

    return pl.pallas_call(
        body, name="ffn_fwd", grid=(t // tm,),
        in_specs=[pl.BlockSpec((tm, D), lambda i: (i, 0)),
                  pl.BlockSpec((1, D), lambda i: (0, 0)),
                  pl.BlockSpec((1, 9, D), lambda i: (i // tpe, 0, 0)),
                  _layer(wup, l), _layer(wdn, l)] + [ANY] * len(after),
        out_specs=[pl.BlockSpec((tm, D), lambda i: (i, 0)),
                   pl.BlockSpec((tm, 2 * D_FF), lambda i: (i, 0)),
                   pl.BlockSpec((tm, D), lambda i: (i, 0))],
        out_shape=[_mid((t, D), F32), _mid((t, 2 * D_FF), BF16), _mid((t, D), BF16)],
        compiler_params=_cparams(60, 1),
    )(*_pin(x, gain, mod, wup, wdn, *after))


def ffn_bwd(dy, u, f, x, gain, mod, wup, wdn, l, s, tm=256, after=()):
    t = dy.shape[0]
    tpe = t // tm // 2

    def body(dy_ref, u_ref, f_ref, x_ref, gain_ref, mod_ref, wup_ref, wdn_ref, *rest):
        dx_ref, du_ref, a_ref, df_ref, h_ref, se_ref, sa_ref = rest[len(after):]
        i = pl.program_id(0)
        dyv, xv, gain_v = dy_ref[...], x_ref[...], gain_ref[...]
        sh, sc, g = _mod_rows(mod_ref, s)
        r, nrm, h = _norm_mod(xv, gain_v, sc, sh)
        h_ref[...] = h.astype(BF16)
        dfb = (0.5 * g * dyv).astype(BF16)
        df_ref[...] = dfb
        dh = jnp.zeros((tm, D), F32)
        for lo, hi in FF_CHUNKS:
            da = _dot_nt(dfb, wdn_ref[lo:hi, :])
            gate = u_ref[:, lo:hi].astype(F32)
            up = u_ref[:, D_FF + lo:D_FF + hi].astype(F32)
            sg = _sigmoid(gate)
            silu = gate * sg
            a_ref[:, lo:hi] = (silu * up).astype(BF16)
            dgate = (da * up * (sg + silu * (1.0 - sg))).astype(BF16)
            dup = (da * silu).astype(BF16)
            du_ref[:, lo:hi] = dgate
            du_ref[:, D_FF + lo:D_FF + hi] = dup
            dh = dh + _dot_nt(dgate, wup_ref[:, lo:hi]) + _dot_nt(dup, wup_ref[:, D_FF + lo:D_FF + hi])
        dn = dh * (1.0 + sc)
        dxr = dn * gain_v
        m = jnp.mean(dxr * xv, axis=-1, keepdims=True)
        dx_ref[...] = dyv + r * dxr - xv * (r * r * r) * m

        @pl.when(i % tpe == 0)
        def _():
            se_ref[...] = jnp.zeros_like(se_ref)

        @pl.when(i == 0)
        def _():
            sa_ref[...] = jnp.zeros_like(sa_ref)

        se_ref[0, 0:1, :] += _colsum(dh)
        se_ref[0, 1:2, :] += _colsum(dh * nrm)
        se_ref[0, 2:3, :] += _colsum(0.5 * dyv * f_ref[...].astype(F32))
        sa_ref[0:1, :] += _colsum(dn * xv * r)

    row = lambda w_: pl.BlockSpec((tm, w_), lambda i: (i, 0))
    return pl.pallas_call(
        body, name="ffn_bwd", grid=(t // tm,),
        in_specs=[row(D), row(2 * D_FF), row(D), row(D),
                  pl.BlockSpec((1, D), lambda i: (0, 0)),
                  pl.BlockSpec((1, 9, D), lambda i: (i // tpe, 0, 0)),
                  _layer(wup, l), _layer(wdn, l)] + [ANY] * len(after),
        out_specs=[row(D), row(2 * D_FF), row(D_FF), row(D), row(D),
                   pl.BlockSpec((1, 8, D), lambda i: (i // tpe, 0, 0)), pl.BlockSpec((8, D), lambda i: (0, 0))],
        out_shape=[_mid((t, D), F32), _mid((t, 2 * D_FF), BF16), _mid((t, D_FF), BF16), _mid((t, D), BF16),
                   _mid((t, D), BF16), _mid((2, 8, D), F32), _mid((8, D), F32)],
        compiler_params=_cparams(60, 1),
    )(*_pin(dy, u, f, x, gain, mod, wup, wdn, *after))


def lin_bwd(dy, dz, w, l, x, gain, mod, s, want_db, tm=512):
    t = dy.shape[0]
    n = w.shape[2]
    tpe = t // tm // 2

    def body(dy_ref, dz_ref, w_ref, x_ref, gain_ref, mod_ref, dx_ref, h_ref, se_ref, sa_ref, *db_ref):
        i = pl.program_id(0)
        xv = x_ref[...]
        gain_v = gain_ref[...]
        sh, sc, _ = _mod_rows(mod_ref, s)
        r, nrm, h = _norm_mod(xv, gain_v, sc, sh)
        h_ref[...] = h.astype(BF16)
        dh = _dot_nt(dz_ref[...], w_ref[...])
        dn = dh * (1.0 + sc)
        dxr = dn * gain_v
        m = jnp.mean(dxr * xv, axis=-1, keepdims=True)
        dx_ref[...] = dy_ref[...] + r * dxr - xv * (r * r * r) * m

        @pl.when(i % tpe == 0)
        def _():
            se_ref[...] = jnp.zeros_like(se_ref)

        @pl.when(i == 0)
        def _():
            sa_ref[...] = jnp.zeros_like(sa_ref)
            if want_db:
                db_ref[0][...] = jnp.zeros_like(db_ref[0])

        se_ref[0, 0:1, :] += _colsum(dh)
        se_ref[0, 1:2, :] += _colsum(dh * nrm)
        sa_ref[0:1, :] += _colsum(dn * xv * r)
        if want_db:
            db_ref[0][0:1, :] += _colsum(dz_ref[...].astype(F32))

    out_specs = [pl.BlockSpec((tm, D), lambda i: (i, 0)), pl.BlockSpec((tm, D), lambda i: (i, 0)),
                 pl.BlockSpec((1, 8, D), lambda i: (i // tpe, 0, 0)), pl.BlockSpec((8, D), lambda i: (0, 0))]
    out_shape = [_mid((t, D), F32), _mid((t, D), BF16),
                 jax.ShapeDtypeStruct((2, 8, D), F32), jax.ShapeDtypeStruct((8, D), F32)]
    if want_db:
        out_specs.append(pl.BlockSpec((8, n), lambda i: (0, 0)))
        out_shape.append(jax.ShapeDtypeStruct((8, n), F32))
    return pl.pallas_call(
        body, name="lin_bwd", grid=(t // tm,),
        in_specs=[pl.BlockSpec((tm, D), lambda i: (i, 0)),
                  pl.BlockSpec((tm, n), lambda i: (i, 0)),
                  _layer(w, l),
                  pl.BlockSpec((tm, D), lambda i: (i, 0)),
                  pl.BlockSpec((1, D), lambda i: (0, 0)),
                  pl.BlockSpec((1, 9, D), lambda i: (i // tpe, 0, 0))],
        out_specs=out_specs, out_shape=out_shape,
        compiler_params=_cparams(56, 1),
    )(*_pin(dy, dz, w, x, gain, mod))


def wgrad(gstack, l, a, b, bm, bn, bt=1024, after=()):
    t, m = a.shape
    n = b.shape[1]
    nt = t // bt

    def body(g_ref, a_ref, b_ref, *rest):
        o_ref, acc_ref = rest[len(after):]
        k = pl.program_id(2)

        @pl.when(k == 0)
        def _():
            acc_ref[...] = jnp.zeros_like(acc_ref)

        acc_ref[...] += _dot_tn(a_ref[...], b_ref[...])

        @pl.when(k == nt - 1)
        def _():
            o_ref[...] = acc_ref[...].astype(BF16)

    return pl.pallas_call(
        body, name="wgrad", grid=(m // bm, n // bn, nt),
        in_specs=[ANY, pl.BlockSpec((bt, bm), lambda i, j, k: (k, i)),
                  pl.BlockSpec((bt, bn), lambda i, j, k: (k, j))] + [ANY] * len(after),
        out_specs=pl.BlockSpec((None, bm, bn), lambda i, j, k: (l, i, j)),
        out_shape=_mid(gstack.shape, BF16),
        input_output_aliases={0: 0},
        scratch_shapes=[pltpu.VMEM((bm, bn), F32)],
        compiler_params=_cparams(56, 3),
    )(*_pin(gstack, a, b, *after))


def proj_res(x, o, w, l, b, mod, tm=1024):
    t = x.shape[0]
    tpe = t // tm // 2

    def body(x_ref, o_ref, w_ref, b_ref, mod_ref, xo_ref, y_ref):
        _, _, g = _mod_rows(mod_ref, 1)
        y = _dot(o_ref[...], w_ref[...]) + b_ref[...]
        y_ref[...] = y.astype(BF16)
        xo_ref[...] = x_ref[...] + g * y

    return pl.pallas_call(
        body, name="proj_res", grid=(t // tm,),
        in_specs=[pl.BlockSpec((tm, D), lambda i: (i, 0)), pl.BlockSpec((tm, D), lambda i: (i, 0)),
                  _layer(w, l), pl.BlockSpec((1, D), lambda i: (0, 0)),
                  pl.BlockSpec((1, 9, D), lambda i: (i // tpe, 0, 0))],
        out_specs=[pl.BlockSpec((tm, D), lambda i: (i, 0)), pl.BlockSpec((tm, D), lambda i: (i, 0))],
        out_shape=[_mid((t, D), F32), _mid((t, D), BF16)],
        compiler_params=_cparams(48, 1),
    )(*_pin(x, o, w, b, mod))


def proj_res_bwd(dy, y, w, l, mod, tm=1024):
    t = dy.shape[0]
    tpe = t // tm // 2

    def body(dy_ref, y_ref, w_ref, mod_ref, dyy_ref, do_ref, se_ref, sa_ref):
        i = pl.program_id(0)
        _, _, g = _mod_rows(mod_ref, 1)
        dyv = dy_ref[...]
        dyy = g * dyv
        dyb = dyy.astype(BF16)
        dyy_ref[...] = dyb
        do_ref[...] = _dot_nt(dyb, w_ref[...]).astype(BF16)

        @pl.when(i % tpe == 0)
        def _():
            se_ref[...] = jnp.zeros_like(se_ref)

        @pl.when(i == 0)
        def _():
            sa_ref[...] = jnp.zeros_like(sa_ref)

        se_ref[0, 0:1, :] += _colsum(dyv * y_ref[...].astype(F32))
        sa_ref[0:1, :] += _colsum(dyy)

    return pl.pallas_call(
        body, name="proj_res_bwd", grid=(t // tm,),
        in_specs=[pl.BlockSpec((tm, D), lambda i: (i, 0)), pl.BlockSpec((tm, D), lambda i: (i, 0)),
                  _layer(w, l), pl.BlockSpec((1, 9, D), lambda i: (i // tpe, 0, 0))],
        out_specs=[pl.BlockSpec((tm, D), lambda i: (i, 0)), pl.BlockSpec((tm, D), lambda i: (i, 0)),
                   pl.BlockSpec((1, 8, D), lambda i: (i // tpe, 0, 0)), pl.BlockSpec((8, D), lambda i: (0, 0))],
        out_shape=[_mid((t, D), BF16), _mid((t, D), BF16), _mid((2, 8, D), F32), _mid((8, D), F32)],
        compiler_params=_cparams(48, 1),
    )(*_pin(dy, y, w, mod))


def lin_fwd(x, gain, mod, w, l, s, tm=512):
    t = x.shape[0]
    n = w.shape[2]
    tpe = t // tm // 2

    def body(x_ref, gain_ref, mod_ref, w_ref, z_ref):
        sh, sc, _ = _mod_rows(mod_ref, s)
        _, _, h = _norm_mod(x_ref[...], gain_ref[...], sc, sh)
        z_ref[...] = _dot(h.astype(BF16), w_ref[...]).astype(BF16)

    return pl.pallas_call(
        body, name="lin_fwd", grid=(t // tm,),
        in_specs=[pl.BlockSpec((tm, D), lambda i: (i, 0)), pl.BlockSpec((1, D), lambda i: (0, 0)),
                  pl.BlockSpec((1, 9, D), lambda i: (i // tpe, 0, 0)), _layer(w, l)],
        out_specs=pl.BlockSpec((tm, n), lambda i: (i, 0)),
        out_shape=_mid((t, n), BF16),
        compiler_params=_cparams(40, 1),
    )(*_pin(x, gain, mod, w))


def rope_tables(pos, invf):
    t = pos.shape[0]
    tm = 1024

    def body(pos_ref, invf_ref, c_ref, s_ref):
        ang = pos_ref[...].astype(F32) * invf_ref[...]
        lane = lax.broadcasted_iota(jnp.int32, (tm, 128), 1)
        sign = jnp.where(lane % HEAD_DIM < HEAD_DIM // 2, -1.0, 1.0)
        c_ref[...] = jnp.cos(ang)
        s_ref[...] = sign * jnp.sin(ang)

    return pl.pallas_call(
        body, name="rope_tables", grid=(t // tm,),
        in_specs=[pl.BlockSpec((tm, 1), lambda i: (i, 0)), pl.BlockSpec((1, 128), lambda i: (0, 0))],
        out_specs=[pl.BlockSpec((tm, 128), lambda i: (i, 0))] * 2,
        out_shape=[jax.ShapeDtypeStruct((t, 128), F32)] * 2,
        compiler_params=_cparams(16, 1),
    )(pos, invf)


def _swap_halves(v):
    lane = lax.broadcasted_iota(jnp.int32, v.shape, 1)
    return jnp.where(lane % HEAD_DIM < HEAD_DIM // 2, pltpu.roll(v, 128 - HEAD_DIM // 2, 1), pltpu.roll(v, HEAD_DIM // 2, 1))


def _rope(v, cos, sin):
    return jnp.concatenate(
        [v[:, j:j + 128] * cos + _swap_halves(v[:, j:j + 128]) * sin for j in range(0, v.shape[1], 128)], axis=1)


def _rope_t(dv, cos, sin):
    return jnp.concatenate(
        [dv[:, j:j + 128] * cos + _swap_halves(dv[:, j:j + 128] * sin) for j in range(0, dv.shape[1], 128)], axis=1)


def _head_stats(qk, g1, g2):
    rinv = lax.rsqrt(_dot_hilo(qk * qk, g1) + EPS)
    return rinv, _dot_hilo(rinv, g2)


def qkv_fwd(x, gain, mod, w, l, b, gqk, cos, sin, g1, g2, tm=256):
    t = x.shape[0]
    tpe = t // tm // 2

    def body(x_ref, gain_ref, mod_ref, w_ref, b_ref, gqk_ref, c_ref, s_ref, g1_ref, g2_ref, raw_ref, q_ref, k_ref):
        sh, sc, _ = _mod_rows(mod_ref, 1)
        _, _, h = _norm_mod(x_ref[...], gain_ref[...], sc, sh)
        qkv = _dot(h.astype(BF16), w_ref[...]) + b_ref[...]
        raw_ref[...] = qkv.astype(BF16)
        qk = qkv[:, :QK_DIM]
        _, rb = _head_stats(qk, g1_ref[...], g2_ref[...])
        qr = _rope(qk * rb * gqk_ref[...], c_ref[...], s_ref[...])
        q_ref[...] = qr[:, :D].astype(BF16)
        k_ref[...] = qr[:, D:].astype(BF16)

    return pl.pallas_call(
        body, name="qkv_fwd", grid=(t // tm,),
        in_specs=[pl.BlockSpec((tm, D), lambda i: (i, 0)), pl.BlockSpec((1, D), lambda i: (0, 0)),
                  pl.BlockSpec((1, 9, D), lambda i: (i // tpe, 0, 0)), _layer(w, l),
                  pl.BlockSpec((1, QKV_DIM), lambda i: (0, 0)), pl.BlockSpec((1, QK_DIM), lambda i: (0, 0)),
                  pl.BlockSpec((tm, 128), lambda i: (i, 0)), pl.BlockSpec((tm, 128), lambda i: (i, 0)),
                  _resident((QK_DIM, 128)), _resident((128, QK_DIM))],
        out_specs=[pl.BlockSpec((tm, QKV_DIM), lambda i: (i, 0)), pl.BlockSpec((tm, D), lambda i: (i, 0)),
                   pl.BlockSpec((tm, N_KV * HEAD_DIM), lambda i: (i, 0))],
        out_shape=[_mid((t, QKV_DIM), BF16), _mid((t, D), BF16), _mid((t, N_KV * HEAD_DIM), BF16)],
        compiler_params=_cparams(40, 1),
    )(*_pin(x, gain, mod, w, b, gqk, cos, sin, g1, g2))


def qkv_bwd_pre(dq, dk, dv, raw, gqk, cos, sin, g1, g2, gsel, tm=256):
    t = dq.shape[0]

    def body(dq_ref, dk_ref, dv_ref, raw_ref, gqk_ref, c_ref, s_ref, g1_ref, g2_ref, gsel_ref, dz_ref, sa_ref):
        i = pl.program_id(0)
        dqk = jnp.concatenate([dq_ref[...].astype(F32), dk_ref[...]], axis=1)
        dqn = _rope_t(dqk, c_ref[...], s_ref[...])
        qk = raw_ref[:, :QK_DIM].astype(F32)
        g1v, g2v = g1_ref[...], g2_ref[...]
        rinv, rb = _head_stats(qk, g1v, g2v)
        dgq = jnp.broadcast_to(_colsum(dqn * qk * rb), (8, QK_DIM))
        dyh = dqn * gqk_ref[...]
        mh = _dot_hilo(dyh * qk, g1v)
        mb = _dot_hilo(mh * rinv * rinv * rinv, g2v)
        dz_ref[:, :QK_DIM] = (rb * dyh - qk * mb).astype(BF16)
        dz_ref[:, QK_DIM:] = dv_ref[...].astype(BF16)

        @pl.when(i == 0)
        def _():
            sa_ref[...] = jnp.zeros_like(sa_ref)

        sa_ref[...] += _dot_hilo(dgq, gsel_ref[...])

    kvw = N_KV * HEAD_DIM
    return pl.pallas_call(
        body, name="qkv_bwd_pre", grid=(t // tm,),
        in_specs=[pl.BlockSpec((tm, D), lambda i: (i, 0)), pl.BlockSpec((tm, kvw), lambda i: (i, 0)),
                  pl.BlockSpec((tm, kvw), lambda i: (i, 0)), pl.BlockSpec((tm, QKV_DIM), lambda i: (i, 0)),
                  pl.BlockSpec((1, QK_DIM), lambda i: (0, 0)),
                  pl.BlockSpec((tm, 128), lambda i: (i, 0)), pl.BlockSpec((tm, 128), lambda i: (i, 0)),
                  _resident((QK_DIM, 128)), _resident((128, QK_DIM)), _resident((QK_DIM, 128))],
        out_specs=[pl.BlockSpec((tm, QKV_DIM), lambda i: (i, 0)), pl.BlockSpec((8, 128), lambda i: (0, 0))],
        out_shape=[_mid((t, QKV_DIM), BF16), _mid((8, 128), F32)],
        compiler_params=_cparams(40, 1),
    )(*_pin(dq, dk, dv, raw, gqk, cos, sin, g1, g2, gsel))


def _band_mask(n):
    row = lax.broadcasted_iota(jnp.int32, (GROUP * BLOCK, 2 * BLOCK), 0) % BLOCK
    col = lax.broadcasted_iota(jnp.int32, (GROUP * BLOCK, 2 * BLOCK), 1)
    rel = row + BLOCK - col
    return (rel >= 0) & (rel < BLOCK) & ((col >= BLOCK) | (n > 0))


def _stack_heads(v, g):
    base = g * GROUP * HEAD_DIM
    return jnp.concatenate([v[:, base + j * HEAD_DIM:base + (j + 1) * HEAD_DIM] for j in range(GROUP)], axis=0)


def _kv_cat(prev, cur, g):
    return jnp.concatenate([prev[:, g * HEAD_DIM:(g + 1) * HEAD_DIM], cur[:, g * HEAD_DIM:(g + 1) * HEAD_DIM]], axis=0)


def _sink_col(sink_ref, g):
    return jnp.concatenate([jnp.full((BLOCK, 1), sink_ref[0, g * GROUP + j], F32) for j in range(GROUP)], axis=0)


def _attn_specs(nb):
    kvw = N_KV * HEAD_DIM
    vcol = QK_DIM // kvw
    cur = lambda e, n: (e * nb + n, 0)
    prev = lambda e, n: (e * nb + jnp.maximum(n - 1, 0), 0)
    return [pl.BlockSpec((BLOCK, D), cur),
            pl.BlockSpec((BLOCK, kvw), cur), pl.BlockSpec((BLOCK, kvw), prev),
            pl.BlockSpec((BLOCK, kvw), lambda e, n: (e * nb + n, vcol)),
            pl.BlockSpec((BLOCK, kvw), lambda e, n: (e * nb + jnp.maximum(n - 1, 0), vcol)),
            pl.BlockSpec(memory_space=pltpu.SMEM)]


def attn_fwd(q, k, raw, sinks):
    t = q.shape[0]
    nb = t // 2 // BLOCK

    def body(q_ref, kc_ref, kp_ref, vc_ref, vp_ref, sink_ref, o_ref, lse_ref):
        n = pl.program_id(1)
        qv = q_ref[...]
        kc, kp, vc, vp = kc_ref[...], kp_ref[...], vc_ref[...], vp_ref[...]
        mask = _band_mask(n)
        outs, lses = [], []
        for g in range(N_KV):
            kk, vv = _kv_cat(kp, kc, g), _kv_cat(vp, vc, g)
            s = jnp.where(mask, _dot_nt(_stack_heads(qv, g), kk) * (HEAD_DIM ** -0.5), -1e30)
            sink = _sink_col(sink_ref, g)
            m = jnp.maximum(jnp.max(s, axis=1, keepdims=True), sink)
            p = jnp.exp(s - m)
            l = jnp.sum(p, axis=1, keepdims=True) + jnp.exp(sink - m)
            o = _dot(p.astype(BF16), vv) / l
            lse = m + jnp.log(l)
            for j in range(GROUP):
                outs.append(o[j * BLOCK:(j + 1) * BLOCK, :])
                lses.append(lse[j * BLOCK:(j + 1) * BLOCK, :])
        o_ref[...] = jnp.concatenate(outs, axis=1).astype(BF16)
        lse_ref[...] = jnp.concatenate(lses, axis=1)

    cur = lambda e, n: (e * nb + n, 0)
    return pl.pallas_call(
        body, name="attn_fwd", grid=(2, nb),
        in_specs=_attn_specs(nb),
        out_specs=[pl.BlockSpec((BLOCK, D), cur), pl.BlockSpec((BLOCK, N_HEADS), cur)],
        out_shape=[_mid((t, D), BF16), _mid((t, N_HEADS), F32)],
        compiler_params=_cparams(32, 2),
    )(*_pin(q, k, k, raw, raw), sinks)


def attn_bwd(q, k, raw, sinks, o, do, lse):
    t = q.shape[0]
    s_len = t // 2
    nb = s_len // BLOCK
    kvw = N_KV * HEAD_DIM

    def body(q_ref, kc_ref, kp_ref, vc_ref, vp_ref, sink_ref, o_ref, do_ref, lse_ref, dq_ref, dk_ref, dv_ref, ds_ref):
        n = pl.program_id(1)

        @pl.when(n == 0)
        def _():
            dk_ref[...] = jnp.zeros_like(dk_ref)
            dv_ref[...] = jnp.zeros_like(dv_ref)

        @pl.when((n == 0) & (pl.program_id(0) == 0))
        def _():
            ds_ref[...] = jnp.zeros_like(ds_ref)

        qv, ov, dov, lsev = q_ref[...], o_ref[...], do_ref[...], lse_ref[...]
        kc, kp, vc, vp = kc_ref[...], kp_ref[...], vc_ref[...], vp_ref[...]
        mask = _band_mask(n)
        dqs, dks, dvs, dsk = [], [], [], []
        for g in range(N_KV):
            kk, vv = _kv_cat(kp, kc, g), _kv_cat(vp, vc, g)
            qg, og, dog = _stack_heads(qv, g), _stack_heads(ov, g), _stack_heads(dov, g)
            lse = jnp.concatenate([lsev[:, g * GROUP + j:g * GROUP + j + 1] for j in range(GROUP)], axis=0)
            s = jnp.where(mask, _dot_nt(qg, kk) * (HEAD_DIM ** -0.5), -1e30)
            p = jnp.exp(s - lse)
            dd = _dot_hilo(dog.astype(F32) * og.astype(F32), jnp.ones((HEAD_DIM, 128), BF16))[:, :1]
            ds = (p * (_dot_nt(dog, vv) - dd) * (HEAD_DIM ** -0.5)).astype(BF16)
            dqg = _dot(ds, kk)
            dks.append(_dot_tn(ds, qg))
            dvs.append(_dot_tn(p.astype(BF16), dog))
            wsink = jnp.exp(_sink_col(sink_ref, g) - lse) * dd
            for j in range(GROUP):
                dqs.append(dqg[j * BLOCK:(j + 1) * BLOCK, :])
                dsk.append(wsink[j * BLOCK:(j + 1) * BLOCK, :])
        dq_ref[...] = jnp.concatenate(dqs, axis=1).astype(BF16)
        dkk = jnp.concatenate(dks, axis=1)
        dvv = jnp.concatenate(dvs, axis=1)
        prev0 = pl.multiple_of(jnp.maximum(n - 1, 0) * BLOCK, BLOCK)
        cur0 = pl.multiple_of(n * BLOCK, BLOCK)
        dk_ref[pl.ds(prev0, BLOCK), :] += dkk[:BLOCK]
        dv_ref[pl.ds(prev0, BLOCK), :] += dvv[:BLOCK]
        dk_ref[pl.ds(cur0, BLOCK), :] += dkk[BLOCK:]
        dv_ref[pl.ds(cur0, BLOCK), :] += dvv[BLOCK:]
        ds_ref[0:1, :] -= _colsum(jnp.concatenate(dsk, axis=1))

    cur = lambda e, n: (e * nb + n, 0)
    return pl.pallas_call(
        body, name="attn_bwd", grid=(2, nb),
        in_specs=_attn_specs(nb) + [pl.BlockSpec((BLOCK, D), cur), pl.BlockSpec((BLOCK, D), cur),
                                    pl.BlockSpec((BLOCK, N_HEADS), cur)],
        out_specs=[pl.BlockSpec((BLOCK, D), cur), pl.BlockSpec((s_len, kvw), lambda e, n: (e, 0)),
                   pl.BlockSpec((s_len, kvw), lambda e, n: (e, 0)), pl.BlockSpec((8, N_HEADS), lambda e, n: (0, 0))],
        out_shape=[_mid((t, D), BF16), _mid((t, kvw), F32), _mid((t, kvw), F32), _mid((8, N_HEADS), F32)],
        compiler_params=_cparams(32, 2),
    )(*_pin(q, k, k, raw, raw), sinks, *_pin(o, do, lse))


CONV_COLS = 256


def _conv_specs(s_len):
    nct = D // CONV_COLS
    return [pl.BlockSpec((s_len, CONV_COLS), lambda j, e, *_: (e, j)),
            pl.BlockSpec((s_len, CONV_COLS), lambda j, e, *_: (e, nct + j)),
            pl.BlockSpec((s_len, CONV_COLS), lambda j, e, *_: (e, 2 * nct + j)),
            pl.BlockSpec((3, CONV_COLS), lambda j, e, *_: (0, j))]


def _conv_taps(gc, v, w, s_len):
    u = gc * v
    row = lax.broadcasted_iota(jnp.int32, u.shape, 0)
    u1 = jnp.where(row >= 1, pltpu.roll(u, 1, 0), 0.0)
    u2 = jnp.where(row >= 2, pltpu.roll(u, 2, 0), 0.0)
    return u, u1, u2, w[2:3, :] * u + w[1:2, :] * u1 + w[0:1, :] * u2


def conv_fwd(z, w):
    t = z.shape[0]
    s_len = t // 2

    def body(gb_ref, gc_ref, v_ref, w_ref, p_ref):
        _, _, _, conv = _conv_taps(gc_ref[...].astype(F32), v_ref[...].astype(F32), w_ref[...], s_len)
        p_ref[...] = (gb_ref[...].astype(F32) * conv).astype(BF16)

    return pl.pallas_call(
        body, name="conv_fwd", grid=(D // CONV_COLS, 2),
        in_specs=_conv_specs(s_len),
        out_specs=pl.BlockSpec((s_len, CONV_COLS), lambda j, e: (e, j)),
        out_shape=_mid((t, D), BF16),
        compiler_params=_cparams(40, 2),
    )(*_pin(z, z, z, w))


def conv_bwd(z, w, dp):
    t = z.shape[0]
    s_len = t // 2
    nct = D // CONV_COLS

    def body(gb_ref, gc_ref, v_ref, w_ref, dp_ref, dz_ref, dw_ref, parts_ref, sems):
        j, e = pl.program_id(0), pl.program_id(1)
        gc, v, wv = gc_ref[...].astype(F32), v_ref[...].astype(F32), w_ref[...]
        u, u1, u2, conv = _conv_taps(gc, v, wv, s_len)
        dpv = dp_ref[...].astype(F32)
        parts_ref[0] = (dpv * conv).astype(BF16)
        dc = dpv * gb_ref[...].astype(F32)
        row = lax.broadcasted_iota(jnp.int32, dc.shape, 0)
        dc1 = jnp.where(row <= s_len - 2, pltpu.roll(dc, s_len - 1, 0), 0.0)
        dc2 = jnp.where(row <= s_len - 3, pltpu.roll(dc, s_len - 2, 0), 0.0)
        du = wv[2:3, :] * dc + wv[1:2, :] * dc1 + wv[0:1, :] * dc2
        parts_ref[1] = (du * v).astype(BF16)
        parts_ref[2] = (du * gc).astype(BF16)
        rows = pl.ds(pl.multiple_of(e * s_len, s_len), s_len)
        copies = [pltpu.make_async_copy(
            parts_ref.at[k], dz_ref.at[rows, pl.ds(pl.multiple_of((k * nct + j) * CONV_COLS, CONV_COLS), CONV_COLS)],
            sems.at[k]) for k in range(3)]
        for cp in copies:
            cp.start()

        @pl.when(e == 0)
        def _():
            dw_ref[...] = jnp.zeros_like(dw_ref)

        dw_ref[0:1, :] += _colsum(dc * u2)
        dw_ref[1:2, :] += _colsum(dc * u1)
        dw_ref[2:3, :] += _colsum(dc * u)
        for cp in copies:
            cp.wait()

    return pl.pallas_call(
        body, name="conv_bwd", grid=(nct, 2),
        in_specs=_conv_specs(s_len) + [pl.BlockSpec((s_len, CONV_COLS), lambda j, e: (e, j))],
        out_specs=[ANY, pl.BlockSpec((8, CONV_COLS), lambda j, e: (0, j))],
        out_shape=[_mid((t, 3 * D), BF16), jax.ShapeDtypeStruct((8, D), F32)],
        scratch_shapes=[pltpu.VMEM((3, s_len, CONV_COLS), BF16), pltpu.SemaphoreType.DMA((3,))],
        compiler_params=_cparams(48, 2),
    )(*_pin(z, z, z, w, dp))


def loss_grad(y, tgt, tm=1024):
    t = y.shape[0]

    def body(y_ref, t_ref, dy_ref, l_ref):
        i = pl.program_id(0)
        d = y_ref[...] - t_ref[...]
        dy_ref[...] = d * (1.0 / D)

        @pl.when(i == 0)
        def _():
            l_ref[...] = jnp.zeros_like(l_ref)

        l_ref[...] += 0.5 / D * jnp.sum(d * d)

    return pl.pallas_call(
        body, name="loss_grad", grid=(t // tm,),
        in_specs=[pl.BlockSpec((tm, D), lambda i: (i, 0))] * 2,
        out_specs=[pl.BlockSpec((tm, D), lambda i: (i, 0)), pl.BlockSpec((8, 128), lambda i: (0, 0))],
        out_shape=[_mid((t, D), F32), _mid((8, 128), F32)],
        compiler_params=_cparams(40, 1),
    )(*_pin(y, tgt))


ADA_COLS = 384


def ada_fwd(c_all, w):
    nl, _, n = w.shape
    nex = c_all.shape[0]

    def body(c_ref, w_ref, o_ref):
        cv = c_ref[...]
        ca = (cv * jax.nn.sigmoid(cv)).astype(BF16)
        o_ref[0] = _dot(ca, w_ref[0].astype(BF16))

    return pl.pallas_call(
        body, name="ada_fwd", grid=(nl, n // ADA_COLS),
        in_specs=[pl.BlockSpec((nex, D), lambda l, j: (0, 0)), pl.BlockSpec((1, D, ADA_COLS), lambda l, j: (l, 0, j))],
        out_specs=pl.BlockSpec((1, nex, ADA_COLS), lambda l, j: (l, 0, j)),
        out_shape=jax.ShapeDtypeStruct((nl, nex, n), F32),
        compiler_params=_cparams(32, 2),
    )(*_pin(c_all, w))


def _adam_math(w, g, m, v):
    m = ADAM_B1 * m + (1.0 - ADAM_B1) * g
    v = ADAM_B2 * v + (1.0 - ADAM_B2) * (g * g)
    m_hat = m / (1.0 - ADAM_B1 ** ADAM_STEP)
    v_hat = v / (1.0 - ADAM_B2 ** ADAM_STEP)
    return -ADAM_LR * (m_hat / (jnp.sqrt(v_hat) + ADAM_EPS) + ADAM_WD * w), m, v


def ada_bwd_adam(c_all, dm, w, m, v):
    nl, _, n = w.shape
    nex = c_all.shape[0]

    def body(c_ref, dm_ref, w_ref, m_ref, v_ref, g_ref, d_ref, mo_ref, vo_ref):
        cv = c_ref[...]
        ca = (cv * jax.nn.sigmoid(cv)).astype(BF16)
        g = _dot_tn(ca, dm_ref[0].astype(BF16))
        g_ref[0] = g
        d_ref[0], mo_ref[0], vo_ref[0] = _adam_math(w_ref[0], g, m_ref[0], v_ref[0])

    wspec = pl.BlockSpec((1, D, ADA_COLS), lambda l, j: (l, 0, j))
    return pl.pallas_call(
        body, name="ada_bwd_adam", grid=(nl, n // ADA_COLS),
        in_specs=[pl.BlockSpec((nex, D), lambda l, j: (0, 0)), pl.BlockSpec((1, nex, ADA_COLS), lambda l, j: (l, 0, j)),
                  wspec, wspec, wspec],
        out_specs=[wspec] * 4,
        out_shape=[jax.ShapeDtypeStruct(w.shape, F32)] * 4,
        compiler_params=_cparams(40, 2),
    )(*_pin(c_all, dm, w, m, v))


def adamw(w, g, m, v):
    shape = w.shape
    cols = shape[-1]
    rows = w.size // cols
    args = [a.reshape(rows, cols) for a in (w, g, m, v)]
    tr = rows
    while tr * cols * 4 > (1 << 20) and tr % 16 == 0:
        tr //= 2

    def body(w_ref, g_ref, m_ref, v_ref, d_ref, mo_ref, vo_ref):
        d_ref[...], mo_ref[...], vo_ref[...] = _adam_math(w_ref[...], g_ref[...], m_ref[...], v_ref[...])

    spec = pl.BlockSpec((tr, cols), lambda i: (i, 0))
    outs = pl.pallas_call(
        body, name="adamw", grid=(rows // tr,),
        in_specs=[spec] * 4, out_specs=[spec] * 3,
        out_shape=[jax.ShapeDtypeStruct((rows, cols), F32)] * 3,
        compiler_params=_cparams(32, 1),
    )(*args)
    return [o.reshape(shape) for o in outs]


def adamw_layers(w, g, m, v, prev, l0, n):
    _, r, c = w.shape
    tr = r
    while tr * c * 4 > (2 << 20) and tr % 16 == 0:
        tr //= 2

    def body(w_ref, g_ref, m_ref, v_ref, pd_ref, pm_ref, pv_ref, d_ref, mo_ref, vo_ref):
        d_ref[...], mo_ref[...], vo_ref[...] = _adam_math(w_ref[...], g_ref[...], m_ref[...], v_ref[...])

    spec = pl.BlockSpec((1, tr, c), lambda l, i: (l0 + l, i, 0))
    return pl.pallas_call(
        body, name="adamw_layers", grid=(n, r // tr),
        in_specs=[spec] * 4 + [ANY] * 3, out_specs=[spec] * 3,
        out_shape=[jax.ShapeDtypeStruct(w.shape, F32)] * 3,
        input_output_aliases={4: 0, 5: 1, 6: 2},
        compiler_params=_cparams(40, 2),
    )(*_pin(w, g, m, v, *prev))


def cast_into_window(w, l0, n, ax, chip, after=()):
    _, r, c = w.shape
    tr = r
    while tr * c * 4 > (4 << 20) and tr % 32 == 0:
        tr //= 2
    nrb = r // tr
    full = (n, r * N_CHIPS, c) if ax == 1 else (n, r, c * N_CHIPS)

    def body(chip_ref, w_ref, *rest):
        o_ref = rest[len(after)]
        o_ref[...] = w_ref[...].astype(BF16)

    def omap(l, i, chip_ref):
        return (l, chip_ref[0] * nrb + i, 0) if ax == 1 else (l, i, chip_ref[0])

    return pl.pallas_call(
        body, name="cast_into_window",
        grid_spec=pltpu.PrefetchScalarGridSpec(
            num_scalar_prefetch=1, grid=(n, nrb),
            in_specs=[pl.BlockSpec((1, tr, c), lambda l, i, chip_ref: (l0 + l, i, 0))] + [ANY] * len(after),
            out_specs=pl.BlockSpec((1, tr, c), omap)),
        out_shape=_mid(full, BF16), compiler_params=_cparams(32, 2),
    )(chip, *_pin(w, *after))


def add_bias(a, b):
    def body(a_ref, b_ref, o_ref):
        o_ref[...] = a_ref[...] + b_ref[...]

    return pl.pallas_call(body, name="add_bias", out_shape=jax.ShapeDtypeStruct(a.shape, F32))(a, b)


N_DMOD_ROWS = 40


def reduce_small(p_all):
    rows = p_all.shape[1]

    def body(p_ref, red_ref, ex_ref):
        acc = p_ref[0]
        for d in range(1, 8):
            acc = acc + p_ref[d]
        red_ref[...] = acc
        ex_ref[...] = acc[:N_DMOD_ROWS] + acc[N_DMOD_ROWS:2 * N_DMOD_ROWS]

    return pl.pallas_call(
        body, name="reduce_small",
        out_shape=[jax.ShapeDtypeStruct((rows, D), F32), jax.ShapeDtypeStruct((N_DMOD_ROWS, D), F32)],
        compiler_params=_cparams(32, 0),
    )(p_all)


def _place():
    return lax.axis_index("x"), lax.axis_index("y"), lax.axis_index("c")


def _other_chips(x, y):
    return [(1 - x, y), (x, 1 - y), (1 - x, 1 - y)]


def _sl(ref, axis, start, size):
    idx = [slice(None)] * len(ref.shape)
    idx[axis] = pl.ds(pl.multiple_of(start, 16), size)
    return ref.at[tuple(idx)]


def _rcopy(src, dst, send_sem, recv_sem, to):
    return pltpu.make_async_remote_copy(src_ref=src, dst_ref=dst, send_sem=send_sem, recv_sem=recv_sem,
                                        device_id=to, device_id_type=MESH)


def allgather_small(v, all_devices):
    rows, cols = v.shape
    flips = [(dx, dy, dc) for dx in (0, 1) for dy in (0, 1) for dc in (0, 1)
             if (dx, dy, dc) != (0, 0, 0) and (all_devices or dc == 0)]
    n_out = 8 if all_devices else 4

    def body(v_ref, o_ref, send_sems, recv_sems):
        x, y, c = _place()

        def slot(px, py, pc):
            return 4 * px + 2 * py + pc if all_devices else 2 * px + py

        peers = [(1 - x if dx else x, 1 - y if dy else y, 1 - c if dc else c) for dx, dy, dc in flips]
        sends = [_rcopy(v_ref, o_ref.at[slot(x, y, c)], send_sems.at[r], recv_sems.at[r], peer)
                 for r, peer in enumerate(peers)]
        for cp in sends:
            cp.start()
        o_ref[slot(x, y, c)] = v_ref[...]
        for r, peer in enumerate(peers):
            _rcopy(v_ref, o_ref.at[slot(*peer)], send_sems.at[r], recv_sems.at[r], peer).wait_recv()
        for cp in sends:
            cp.wait_send()

    vm = pl.BlockSpec(memory_space=pltpu.VMEM)
    return pl.pallas_call(
        body, name="allgather_small_all" if all_devices else "allgather_small_chips",
        in_specs=[vm], out_specs=vm,
        out_shape=jax.ShapeDtypeStruct((n_out, rows, cols), v.dtype),
        scratch_shapes=[pltpu.SemaphoreType.DMA((len(flips),)), pltpu.SemaphoreType.DMA((len(flips),))],
        compiler_params=pltpu.CompilerParams(vmem_limit_bytes=32 * 1024 * 1024),
    )(v)


HBM = pl.BlockSpec(memory_space=pltpu.HBM)
SEM = pl.BlockSpec(memory_space=pltpu.SEMAPHORE)
SPLIT_COPY = pltpu.CompilerParams(has_side_effects=pltpu.SideEffectType.DATAFLOW_SIDE_EFFECTING)


def _in_hbm(a):
    return pltpu.with_memory_space_constraint(a, pltpu.HBM)


def _window(ref, ax, chip):
    n = ref.shape[ax] // N_CHIPS
    return _sl(ref, ax, (2 * chip[0] + chip[1]) * n, n)


def _half(ref, ax, cc):
    ha = 3 - ax
    hs = ref.shape[ha] // 2
    return _sl(ref, ha, cc * hs, hs)


def gather_start(bufs, axes, tag):
    na = len(bufs)

    def body(*refs):
        ins = refs[:na]
        send_sems, recv_sems = refs[na], refs[na + 1]
        token = refs[2 * na + 2]
        x, y, c = _place()
        for a in range(na):
            mine = _half(_window(ins[a], axes[a], (x, y)), axes[a], c)
            for j, chip in enumerate(_other_chips(x, y)):
                _rcopy(mine, mine, send_sems.at[3 * a + j], recv_sems.at[3 * a + j], (*chip, c)).start()
        token[...] = jnp.zeros_like(token)

    dma = pltpu.SemaphoreType.DMA
    outs = pl.pallas_call(
        body, name="gather_start_" + tag,
        in_specs=[HBM] * na,
        out_specs=(SEM, SEM, *[HBM] * na, pl.BlockSpec(memory_space=pltpu.VMEM)),
        out_shape=(dma((3 * na,)), dma((3 * na,)), *[pltpu.HBM(b.shape, b.dtype) for b in bufs],
                   jax.ShapeDtypeStruct((8, 128), F32)),
        input_output_aliases={a: 2 + a for a in range(na)},
        compiler_params=SPLIT_COPY,
    )(*[_in_hbm(b) for b in bufs])
    return outs[0], outs[1], list(outs[2:2 + na]), outs[2 + na]


def gather_wait(send_sems, recv_sems, bufs, axes, after, tag):
    na = len(bufs)

    def body(*refs):
        ins = refs[:na]
        send_sems, recv_sems = refs[na], refs[na + 1]
        x, y, c = _place()
        for a in range(na):
            for j, chip in enumerate(_other_chips(x, y)):
                got = _half(_window(ins[a], axes[a], chip), axes[a], c)
                _rcopy(got, got, send_sems.at[3 * a + j], recv_sems.at[3 * a + j], (*chip, c)).wait_recv()
        for a in range(na):
            mine = _half(_window(ins[a], axes[a], (x, y)), axes[a], c)
            for j, chip in enumerate(_other_chips(x, y)):
                _rcopy(mine, mine, send_sems.at[3 * a + j], recv_sems.at[3 * a + j], (*chip, c)).wait_send()

    return pl.pallas_call(
        body, name="gather_wait_" + tag,
        in_specs=[HBM] * na + [SEM, SEM] + [ANY] * len(after),
        out_specs=[HBM] * na,
        out_shape=[pltpu.HBM(b.shape, b.dtype) for b in bufs],
        input_output_aliases={a: a for a in range(na)},
        compiler_params=SPLIT_COPY,
    )(*bufs, send_sems, recv_sems, *after)


def gather_forward(bufs, axes):
    na = len(bufs)

    def body(*refs):
        outs = refs[na:2 * na]
        send_sems, recv_sems = refs[2 * na:]
        x, y, c = _place()
        chips = _other_chips(x, y)
        passed = []
        for a in range(na):
            for j, chip in enumerate(chips):
                got = _half(_window(outs[a], axes[a], chip), axes[a], c)
                cp = _rcopy(got, got, send_sems.at[3 * a + j], recv_sems.at[3 * a + j], (x, y, 1 - c))
                cp.start()
                passed.append(cp)
        for a in range(na):
            for j, chip in enumerate(chips):
                got = _half(_window(outs[a], axes[a], chip), axes[a], 1 - c)
                _rcopy(got, got, send_sems.at[3 * a + j], recv_sems.at[3 * a + j], (x, y, 1 - c)).wait_recv()
        for cp in passed:
            cp.wait_send()

    dma = pltpu.SemaphoreType.DMA
    return pl.pallas_call(
        body, name="gather_forward",
        in_specs=[ANY] * na, out_specs=[ANY] * na,
        out_shape=[jax.ShapeDtypeStruct(b.shape, BF16) for b in bufs],
        input_output_aliases={a: a for a in range(na)},
        scratch_shapes=[dma((3 * na,)), dma((3 * na,))],
    )(*bufs)


def sibling_start(bufs, views, n_copies, tag):
    nb = len(bufs)

    def body(*refs):
        send_sems, recv_sems = refs[nb], refs[nb + 1]
        token = refs[2 * nb + 2]
        x, y, c = _place()
        for k, (src, dst, _) in enumerate(views(refs[:nb], c)):
            _rcopy(src, dst, send_sems.at[k], recv_sems.at[k], (x, y, 1 - c)).start()
        token[...] = jnp.zeros_like(token)

    dma = pltpu.SemaphoreType.DMA
    outs = pl.pallas_call(
        body, name="sibling_start_" + tag,
        in_specs=[HBM] * nb,
        out_specs=(SEM, SEM, *[HBM] * nb, pl.BlockSpec(memory_space=pltpu.VMEM)),
        out_shape=(dma((n_copies,)), dma((n_copies,)), *[pltpu.HBM(b.shape, b.dtype) for b in bufs],
                   jax.ShapeDtypeStruct((8, 128), F32)),
        input_output_aliases={a: 2 + a for a in range(nb)},
        compiler_params=SPLIT_COPY,
    )(*[_in_hbm(b) for b in bufs])
    return outs[0], outs[1], list(outs[2:2 + nb]), outs[2 + nb]


def sibling_wait(send_sems, recv_sems, bufs, views, after, tag):
    nb = len(bufs)

    def body(*refs):
        send_sems, recv_sems = refs[nb], refs[nb + 1]
        x, y, c = _place()
        trip = views(refs[:nb], c)
        for k, (src, _, got) in enumerate(trip):
            _rcopy(src, got, send_sems.at[k], recv_sems.at[k], (x, y, 1 - c)).wait_recv()
        for k, (src, dst, _) in enumerate(trip):
            _rcopy(src, dst, send_sems.at[k], recv_sems.at[k], (x, y, 1 - c)).wait_send()

    return list(pl.pallas_call(
        body, name="sibling_wait_" + tag,
        in_specs=[HBM] * nb + [SEM, SEM] + [ANY] * len(after),
        out_specs=[HBM] * nb,
        out_shape=[pltpu.HBM(b.shape, b.dtype) for b in bufs],
        input_output_aliases={a: a for a in range(nb)},
        compiler_params=SPLIT_COPY,
    )(*bufs, send_sems, recv_sems, *after))


def exchange_views(axes):
    na = len(axes)

    def views(refs, c):
        return [(_half(refs[a], axes[a], 1 - c), refs[na + a], refs[na + a]) for a in range(na)]

    return views


def join_views(regions, axes):
    def views(refs, c):
        out = []
        for a, ref in enumerate(refs):
            reg = ref.at[pl.ds(*regions[a])]
            mine = _half(reg, axes[a], c)
            out.append((mine, mine, _half(reg, axes[a], 1 - c)))
        return out

    return views


def scatter_start(halves, axes, tag):
    na = len(halves)

    def pshape(h, ax):
        return (N_CHIPS - 1,) + tuple(d // N_CHIPS if i == ax else d for i, d in enumerate(h.shape))

    def body(*refs):
        ins, lands = refs[:na], refs[na:2 * na]
        send_sems, recv_sems = refs[2 * na], refs[2 * na + 1]
        token = refs[4 * na + 2]
        x, y, c = _place()
        for a in range(na):
            for j, chip in enumerate(_other_chips(x, y)):
                _rcopy(_window(ins[a], axes[a], chip), lands[a].at[j],
                       send_sems.at[3 * a + j], recv_sems.at[3 * a + j], (*chip, c)).start()
        token[...] = jnp.zeros_like(token)

    dma = pltpu.SemaphoreType.DMA
    lands = [lax.empty(pshape(h, ax), BF16) for h, ax in zip(halves, axes)]
    outs = pl.pallas_call(
        body, name="scatter_start_" + tag,
        in_specs=[HBM] * (2 * na),
        out_specs=(SEM, SEM, *[HBM] * (2 * na), pl.BlockSpec(memory_space=pltpu.VMEM)),
        out_shape=(dma((3 * na,)), dma((3 * na,)), *[pltpu.HBM(b.shape, b.dtype) for b in halves + lands],
                   jax.ShapeDtypeStruct((8, 128), F32)),
        input_output_aliases={a: 2 + a for a in range(2 * na)},
        compiler_params=SPLIT_COPY,
    )(*[_in_hbm(b) for b in halves + lands])
    return outs[0], outs[1], list(outs[2:2 + na]), list(outs[2 + na:2 + 2 * na]), outs[2 + 2 * na]


def scatter_wait(send_sems, recv_sems, halves, lands, axes, after, tag):
    na = len(halves)

    def body(*refs):
        ins, lands = refs[:na], refs[na:2 * na]
        send_sems, recv_sems = refs[2 * na], refs[2 * na + 1]
        x, y, c = _place()
        for a in range(na):
            for j, chip in enumerate(_other_chips(x, y)):
                _rcopy(_window(ins[a], axes[a], chip), lands[a].at[j],
                       send_sems.at[3 * a + j], recv_sems.at[3 * a + j], (*chip, c)).wait_recv()
        for a in range(na):
            for j, chip in enumerate(_other_chips(x, y)):
                _rcopy(_window(ins[a], axes[a], chip), lands[a].at[j],
                       send_sems.at[3 * a + j], recv_sems.at[3 * a + j], (*chip, c)).wait_send()

    outs = pl.pallas_call(
        body, name="scatter_wait_" + tag,
        in_specs=[HBM] * (2 * na) + [SEM, SEM] + [ANY] * len(after),
        out_specs=[HBM] * (2 * na),
        out_shape=[pltpu.HBM(b.shape, b.dtype) for b in halves + lands],
        input_output_aliases={a: a for a in range(2 * na)},
        compiler_params=SPLIT_COPY,
    )(*halves, *lands, send_sems, recv_sems, *after)
    return list(outs[:na]), list(outs[na:])


def _tile2(r, c, itemsize, limit):
    bc = c
    while bc > 1536:
        bc //= 2
    assert c % bc == 0 and bc % 128 == 0
    br = r
    while br * bc * itemsize > limit and br % 32 == 0:
        br //= 2
    assert r % br == 0 and br % 16 == 0
    return br, bc


def add_my_half(g, theirs, ax, cc):
    ha = 3 - ax
    nl, r, c = theirs.shape
    br, bc = _tile2(r, c, 2, 2 << 20)
    nrb, ncb = r // br, c // bc

    def body(cc_ref, g_ref, t_ref, o_ref):
        o_ref[...] = (g_ref[...].astype(F32) + t_ref[...].astype(F32)).astype(BF16)

    def gmap(l, i, j, cc_ref):
        return (l, cc_ref[0] * nrb + i, j) if ha == 1 else (l, i, cc_ref[0] * ncb + j)

    blk = pl.BlockSpec((1, br, bc), lambda l, i, j, cc_ref: (l, i, j))
    return pl.pallas_call(
        body, name="add_my_half",
        grid_spec=pltpu.PrefetchScalarGridSpec(
            num_scalar_prefetch=1, grid=(nl, nrb, ncb),
            in_specs=[pl.BlockSpec((1, br, bc), gmap), blk], out_specs=blk),
        out_shape=_mid(theirs.shape, BF16),
        compiler_params=_cparams(32, 3),
    )(cc, *_pin(g, theirs))


def sum_chips(parts, pair, gstack, l0, ax, where):
    _, n, r, c = parts.shape
    ha = 3 - ax
    br, bc = _tile2(r, c, 2, 1 << 20)
    nrb, ncb = r // br, c // bc

    def body(w_ref, p_ref, own_ref, g_ref, o_ref):
        acc = own_ref[...].astype(F32)
        for q in range(N_CHIPS - 1):
            acc = acc + p_ref[q].astype(F32)
        o_ref[...] = acc

    def own_map(l, i, j, w_ref):
        return (l, w_ref[0] * nrb + i, j) if ax == 1 else (l, i, w_ref[0] * ncb + j)

    def out_map(l, i, j, w_ref):
        return (l0 + l, w_ref[1] * nrb + i, j) if ha == 1 else (l0 + l, i, w_ref[1] * ncb + j)

    return pl.pallas_call(
        body, name="sum_chips",
        grid_spec=pltpu.PrefetchScalarGridSpec(
            num_scalar_prefetch=1, grid=(n, nrb, ncb),
            in_specs=[pl.BlockSpec((N_CHIPS - 1, 1, br, bc), lambda l, i, j, w_ref: (0, l, i, j)),
                      pl.BlockSpec((1, br, bc), own_map), ANY],
            out_specs=pl.BlockSpec((1, br, bc), out_map)),
        out_shape=jax.ShapeDtypeStruct(gstack.shape, F32),
        input_output_aliases={3: 0},
        compiler_params=_cparams(32, 3),
    )(where, *_pin(parts, pair, gstack))


BIG = ("w_ffn_up", "w_ffn_down", "attn_w_qkv", "attn_w_o", "conv_w_in", "conv_w_out")
BIG_AXIS = {"w_ffn_up": 2, "w_ffn_down": 1, "attn_w_qkv": 2, "attn_w_o": 1, "conv_w_in": 2, "conv_w_out": 1}
WEIGHTS = ("norm_gain", "w_ada", "b_ada", "w_ffn_up", "w_ffn_down", "attn_w_qkv", "attn_b_qkv", "attn_q_gain",
           "attn_k_gain", "attn_sinks", "attn_w_o", "attn_b_o", "conv_w_in", "conv_w", "conv_w_out")
N_SMALL_ROWS = 112


def _stack3(a):
    return a.reshape((-1,) + a.shape[-2:])


def _head_matrices():
    lane = jnp.arange(QK_DIM)
    head = lane // HEAD_DIM
    col = jnp.arange(128)
    g1 = jnp.where(head[:, None] == col[None, :], 1.0 / HEAD_DIM, 0.0).astype(BF16)
    g2 = jnp.where(col[:, None] == head[None, :], 1.0, 0.0).astype(BF16)
    fold = lane % HEAD_DIM + jnp.where(head >= N_HEADS, HEAD_DIM, 0)
    gsel = jnp.where(fold[:, None] == col[None, :], 1.0, 0.0).astype(BF16)
    return g1, g2, gsel


def _pad_cols(a, n):
    return jnp.pad(a, ((0, 0), (0, n - a.shape[1])))


def kernel(x, c, positions, norm_gain, w_ada, b_ada, w_ffn_up, w_ffn_down, attn_w_qkv, attn_b_qkv, attn_q_gain, attn_k_gain, attn_sinks, attn_w_o, attn_b_o, conv_w_in, conv_w, conv_w_out, loss_target, m_norm_gain, m_w_ada, m_b_ada, m_w_ffn_up, m_w_ffn_down, m_attn_w_qkv, m_attn_b_qkv, m_attn_q_gain, m_attn_k_gain, m_attn_sinks, m_attn_w_o, m_attn_b_o, m_conv_w_in, m_conv_w, m_conv_w_out, v_norm_gain, v_w_ada, v_b_ada, v_w_ffn_up, v_w_ffn_down, v_attn_w_qkv, v_attn_b_qkv, v_attn_q_gain, v_attn_k_gain, v_attn_sinks, v_attn_w_o, v_attn_b_o, v_conv_w_in, v_conv_w, v_conv_w_out):
    w = dict(norm_gain=norm_gain, w_ada=w_ada, b_ada=b_ada, w_ffn_up=w_ffn_up, w_ffn_down=w_ffn_down,
             attn_w_qkv=attn_w_qkv, attn_b_qkv=attn_b_qkv, attn_q_gain=attn_q_gain, attn_k_gain=attn_k_gain,
             attn_sinks=attn_sinks, attn_w_o=attn_w_o, attn_b_o=attn_b_o, conv_w_in=conv_w_in, conv_w=conv_w,
             conv_w_out=conv_w_out)
    mom = dict(norm_gain=m_norm_gain, w_ada=m_w_ada, b_ada=m_b_ada, w_ffn_up=m_w_ffn_up, w_ffn_down=m_w_ffn_down,
               attn_w_qkv=m_attn_w_qkv, attn_b_qkv=m_attn_b_qkv, attn_q_gain=m_attn_q_gain,
               attn_k_gain=m_attn_k_gain, attn_sinks=m_attn_sinks, attn_w_o=m_attn_w_o, attn_b_o=m_attn_b_o,
               conv_w_in=m_conv_w_in, conv_w=m_conv_w, conv_w_out=m_conv_w_out)
    var = dict(norm_gain=v_norm_gain, w_ada=v_w_ada, b_ada=v_b_ada, w_ffn_up=v_w_ffn_up, w_ffn_down=v_w_ffn_down,
               attn_w_qkv=v_attn_w_qkv, attn_b_qkv=v_attn_b_qkv, attn_q_gain=v_attn_q_gain,
               attn_k_gain=v_attn_k_gain, attn_sinks=v_attn_sinks, attn_w_o=v_attn_w_o, attn_b_o=v_attn_b_o,
               conv_w_in=v_conv_w_in, conv_w=v_conv_w, conv_w_out=v_conv_w_out)

    xi, yi, ci = _place()
    chip = 2 * xi + yi
    dev = 4 * xi + 2 * yi + ci
    nex, s_len, _ = x.shape
    t = nex * s_len
    n_attn, n_conv = attn_w_qkv.shape[0], conv_w_in.shape[0]
    axes = [BIG_AXIS[n] for n in BIG]

    c_all = allgather_small(jnp.pad(c, ((0, 8 - nex), (0, 0))), True)[:, :nex].reshape(8 * nex, D)
    ada_cols = w_ada.shape[2]
    modp = ada_fwd(c_all, w_ada)
    modg = allgather_small(modp.reshape(DEPTH * 8 * nex, ada_cols), False)
    modg = lax.dynamic_slice_in_dim(modg.reshape(N_CHIPS, DEPTH, 8 * nex, ada_cols), dev * nex, nex, axis=2)
    modg = modg.transpose(1, 2, 0, 3).reshape(DEPTH, nex, 9 * D)
    mod = add_bias(modg, b_ada.reshape(DEPTH, 1, 9 * D)).reshape(DEPTH, nex, 9, D)

    small = jnp.concatenate([norm_gain.reshape(DEPTH * 3, -1), conv_w.reshape(n_conv * 3, -1)], axis=0)
    small = jnp.pad(small, ((0, -small.shape[0] % 8), (0, 0)))
    small = allgather_small(small, False).transpose(1, 0, 2).reshape(small.shape[0], D)
    gain_full = small[:DEPTH * 3].reshape(DEPTH, 3, D)
    convw_full = small[DEPTH * 3:DEPTH * 3 + n_conv * 3].reshape(n_conv, 3, D)

    chip_arr = chip.reshape(1).astype(jnp.int32)
    where = jnp.stack([chip, ci]).astype(jnp.int32)
    stacks = [_stack3(w[n]) for n in BIG]

    def mixer(i):
        return (2, 3) if i % 2 == 0 else (4, 5)

    groups = [[(0, 0, 1), (1, 0, 1)], [(mixer(0)[0], 0, 1), (mixer(0)[1], 0, 1), (0, 1, 1), (1, 1, 1)]]
    groups += [[(0, 2 * i, 2), (1, 2 * i, 2), (mixer(i)[0], i // 2, 1), (mixer(i)[1], i // 2, 1)]
               for i in range(1, DEPTH)]
    gaxes = [[axes[b] for b, _, _ in grp] for grp in groups]
    where_is = {(b, l0 + k): (g, a, k) for g, grp in enumerate(groups) for a, (b, l0, n) in enumerate(grp)
                for k in range(n)}
    in_flight, token = [], (mod, small)
    for g, grp in enumerate(groups):
        bufs = [cast_into_window(stacks[b], l0, n, axes[b], chip_arr, token) for b, l0, n in grp]
        ssem, rsem, bufs, tok = gather_start(bufs, gaxes[g], f"g{g}")
        in_flight.append((ssem, rsem, bufs))
        token = (tok,)

    invf = ROPE_THETA ** (-jnp.arange(0, HEAD_DIM, 2, dtype=F32) / HEAD_DIM)
    cos, sin = rope_tables(positions.reshape(t, 1), jnp.tile(invf, 4).reshape(1, 128))
    g1, g2, gsel = _head_matrices()
    gqk = [jnp.concatenate([jnp.tile(attn_q_gain[j], N_HEADS), jnp.tile(attn_k_gain[j], N_KV)]).reshape(1, QK_DIM)
           for j in range(n_attn)]
    zero_bias = jnp.zeros((1, D), F32)

    xs = x.reshape(t, D)
    saved, ready = [], {}

    def weight(b, l, after):
        g, a, k = where_is[(b, l)]
        if g not in ready:
            ssem, rsem, bufs = in_flight[g]
            ready[g] = gather_forward(gather_wait(ssem, rsem, bufs, gaxes[g], after, f"g{g}"), gaxes[g])
        return ready[g][a], k

    for i in range(DEPTH):
        j = i // 2
        gn, md = gain_full[i], mod[i]
        x0 = xs
        wup, k = weight(0, 2 * i, token if i == 0 else (xs,))
        wdn, _ = weight(1, 2 * i, ())
        xs, u1, f1 = ffn_fwd(x0, gn[0:1], md, wup, wdn, k, 0)
        x1 = xs
        wmi, k = weight(mixer(i)[0], j, (xs,))
        wmo, _ = weight(mixer(i)[1], j, ())
        if i % 2 == 0:
            raw, qr, kr = qkv_fwd(x1, gn[1:2], md, wmi, k, attn_b_qkv[j:j + 1], gqk[j], cos, sin, g1, g2)
            o, lse = attn_fwd(qr, kr, raw, attn_sinks[j:j + 1])
            xs, ymix = proj_res(x1, o, wmo, k, attn_b_o[j:j + 1], md)
            mix = (raw, qr, kr, o, lse)
        else:
            z = lin_fwd(x1, gn[1:2], md, wmi, k, 1)
            p = conv_fwd(z, convw_full[j])
            xs, ymix = proj_res(x1, p, wmo, k, zero_bias, md)
            mix = (z, p)
        x2 = xs
        wup, k = weight(0, 2 * i + 1, (xs,))
        wdn, _ = weight(1, 2 * i + 1, ())
        xs, u3, f3 = ffn_fwd(x2, gn[2:3], md, wup, wdn, k, 2)
        saved.append((x0, u1, f1, x1, mix, ymix, x2, u3, f3))
    dy, lpart = loss_grad(xs, loss_target.reshape(t, D))

    cc = ci.reshape(1).astype(jnp.int32)
    gshard = [lax.empty(s.shape, F32) for s in stacks]
    upd = [[lax.empty(s.shape, F32) for _ in range(3)] for s in stacks]
    mstacks = [_stack3(mom[n]) for n in BIG]
    vstacks = [_stack3(var[n]) for n in BIG]

    ggrad = {g: [lax.empty(buf.shape, BF16) for buf in ready[g]] for g in range(len(groups))}
    missing = {g: sum(n for _, _, n in grp) for g, grp in enumerate(groups)}
    state = dict(to_sibling=None, on_ici=None, joining=None, token=())

    def half_shape(shape, ax):
        return tuple(d // 2 if i == 3 - ax else d for i, d in enumerate(shape))

    def advance(after, last=False):
        if state["joining"] is not None:
            g, ssem, rsem, bufs = state["joining"]
            grp = groups[g]
            regions = [(l0, n) for _, l0, n in grp]
            joined = sibling_wait(ssem, rsem, bufs, join_views(regions, gaxes[g]), after, f"join_g{g}")
            for (b, l0, n), gj in zip(grp, joined):
                gshard[b] = gj
                upd[b] = adamw_layers(stacks[b], gj, mstacks[b], vstacks[b], upd[b], l0, n)
            state["joining"] = None
        started = None
        if state["to_sibling"] is not None:
            g, ssem, rsem, bufs = state["to_sibling"]
            na = len(groups[g])
            bufs = sibling_wait(ssem, rsem, bufs, exchange_views(gaxes[g]), after, f"xchg_g{g}")
            pair = [add_my_half(gr, th, ax, cc) for gr, th, ax in zip(bufs[:na], bufs[na:], gaxes[g])]
            ssem, rsem, pair, lands, tok = scatter_start(pair, gaxes[g], f"g{g}")
            started = (g, ssem, rsem, pair, lands)
            state.update(to_sibling=None, token=(tok,))
        if state["on_ici"] is not None and (started is not None or last):
            g, ssem, rsem, pair, lands = state["on_ici"]
            pair, lands = scatter_wait(ssem, rsem, pair, lands, gaxes[g], state["token"] + after, f"g{g}")
            grp = groups[g]
            for a, (b, l0, n) in enumerate(grp):
                gshard[b] = sum_chips(lands[a], pair[a], gshard[b], l0, axes[b], where)
            regions = [(l0, n) for _, l0, n in grp]
            ssem, rsem, bufs, tok = sibling_start([gshard[b] for b, _, _ in grp], join_views(regions, gaxes[g]),
                                                  len(grp), f"join_g{g}")
            for (b, _, _), buf in zip(grp, bufs):
                gshard[b] = buf
            state.update(joining=(g, ssem, rsem, bufs), on_ici=None, token=(tok,))
        if started is not None:
            state["on_ici"] = started

    def put(b, l, lhs, rhs, bm, bn):
        g, a, k = where_is[(b, l)]
        ggrad[g][a] = wgrad(ggrad[g][a], k, lhs, rhs, bm, bn, after=state["token"])
        state["token"] = ()
        missing[g] -= 1
        if missing[g] == 0:
            lands = [lax.empty(half_shape(gr.shape, ax), BF16) for gr, ax in zip(ggrad[g], gaxes[g])]
            ssem, rsem, bufs, tok = sibling_start(ggrad[g] + lands, exchange_views(gaxes[g]), len(lands), f"xchg_g{g}")
            state.update(to_sibling=(g, ssem, rsem, bufs), token=(tok,))

    dmod = [None] * DEPTH
    dgain = [None] * DEPTH
    db_qkv, dqk_gain, dsinks, db_o, dconv_w = ([None] * n_attn, [None] * n_attn, [None] * n_attn,
                                                [None] * n_attn, [None] * n_conv)
    for i in reversed(range(DEPTH)):
        j = i // 2
        gn, md = gain_full[i], mod[i]
        x0, u1, f1, x1, mix, ymix, x2, u3, f3 = saved[i]
        (wup, k), (wdn, _) = weight(0, 2 * i + 1, ()), weight(1, 2 * i + 1, ())
        dy, du, a, df, h, se3, sa3 = ffn_bwd(dy, u3, f3, x2, gn[2:3], md, wup, wdn, k, 2, after=state["token"])
        advance((dy,))
        put(0, 2 * i + 1, h, du, D, D_FF)
        put(1, 2 * i + 1, a, df, D_FF, D)
        (wmi, k), (wmo, _) = weight(mixer(i)[0], j, ()), weight(mixer(i)[1], j, ())
        if i % 2 == 0:
            raw, qr, kr, o, lse = mix
            dyy, do, sp, sb = proj_res_bwd(dy, ymix, wmo, k, md)
            put(mixer(i)[1], j, o, dyy, D, D)
            dq, dk, dv, dsinks[j] = attn_bwd(qr, kr, raw, attn_sinks[j:j + 1], o, do, lse)
            dz, dqk_gain[j] = qkv_bwd_pre(dq, dk, dv, raw, gqk[j], cos, sin, g1, g2, gsel)
            dy, h, se2, sa2, db_qkv[j] = lin_bwd(dy, dz, wmi, k, x1, gn[1:2], md, 1, True)
            put(mixer(i)[0], j, h, dz, D, QKV_DIM)
            db_o[j] = sb
        else:
            z, p = mix
            dyy, dp, sp, _ = proj_res_bwd(dy, ymix, wmo, k, md)
            put(mixer(i)[1], j, p, dyy, D, D)
            dz, dconv_w[j] = conv_bwd(z, convw_full[j], dp)
            dy, h, se2, sa2 = lin_bwd(dy, dz, wmi, k, x1, gn[1:2], md, 1, False)
            put(mixer(i)[0], j, h, dz, D, 1536)
        (wup, k), (wdn, _) = weight(0, 2 * i, ()), weight(1, 2 * i, ())
        dy, du, a, df, h, se1, sa1 = ffn_bwd(dy, u1, f1, x0, gn[0:1], md, wup, wdn, k, 0, after=state["token"])
        advance((dy,))
        put(0, 2 * i, h, du, D, D_FF)
        put(1, 2 * i, a, df, D_FF, D)
        dmod[i] = jnp.stack([se1[:, 0], se1[:, 1], se1[:, 2], se2[:, 0], se2[:, 1], sp[:, 0],
                             se3[:, 0], se3[:, 1], se3[:, 2]], axis=1)
        dgain[i] = jnp.stack([sa1[0], sa2[0], sa3[0]], axis=0)
    grad_x = dy.reshape(x.shape)
    advance((dy,))

    dmod_ex = jnp.stack(dmod, axis=1).reshape(nex, DEPTH * 9, D)
    dmod_ex = jnp.pad(dmod_ex, ((0, 0), (0, N_DMOD_ROWS - DEPTH * 9), (0, 0))).reshape(nex * N_DMOD_ROWS, D)
    misc = jnp.concatenate([dqk_gain[jj][0] for jj in range(n_attn)]
                           + [jnp.pad(dsinks[jj][0], (0, 128 - N_HEADS)) for jj in range(n_attn)]
                           + [lpart[0]])
    rows = [dmod_ex,
            jnp.concatenate(dgain, axis=0), jnp.zeros((4, D), F32),
            jnp.concatenate([_pad_cols(db_qkv[jj][0:1], 2 * D).reshape(2, D) for jj in range(n_attn)], axis=0),
            jnp.concatenate([db_o[jj][0:1] for jj in range(n_attn)], axis=0),
            jnp.concatenate([dconv_w[jj][0:3] for jj in range(n_conv)], axis=0),
            jnp.pad(misc, (0, D - misc.shape[0])).reshape(1, D)]
    packed = jnp.concatenate(rows, axis=0)
    packed = jnp.pad(packed, ((0, N_SMALL_ROWS - packed.shape[0]), (0, 0)))
    p_all = allgather_small(packed, True)
    red, exsum = reduce_small(p_all)

    r0 = nex * N_DMOD_ROWS
    grads = {}
    grads["b_ada"] = exsum[:DEPTH * 9].reshape(DEPTH, 9 * D)
    grads["norm_gain"] = lax.dynamic_slice_in_dim(red[r0:r0 + 12].reshape(DEPTH, 3, D), chip * (D // N_CHIPS),
                                                  D // N_CHIPS, axis=2)
    r1 = r0 + 16
    grads["attn_b_qkv"] = red[r1:r1 + 2 * n_attn].reshape(n_attn, 2 * D)[:, :QKV_DIM]
    r2 = r1 + 2 * n_attn
    grads["attn_b_o"] = red[r2:r2 + n_attn]
    r3 = r2 + n_attn
    grads["conv_w"] = lax.dynamic_slice_in_dim(red[r3:r3 + 3 * n_conv].reshape(n_conv, 3, D), chip * (D // N_CHIPS),
                                               D // N_CHIPS, axis=2)
    mrow = red[r3 + 3 * n_conv]
    grads["attn_q_gain"] = jnp.stack([mrow[128 * jj:128 * jj + HEAD_DIM] for jj in range(n_attn)])
    grads["attn_k_gain"] = jnp.stack([mrow[128 * jj + HEAD_DIM:128 * jj + 128] for jj in range(n_attn)])
    grads["attn_sinks"] = jnp.stack([mrow[128 * (n_attn + jj):128 * (n_attn + jj) + N_HEADS] for jj in range(n_attn)])
    loss = mrow[128 * 2 * n_attn]

    dm_all = p_all[:, :r0].reshape(8, nex, N_DMOD_ROWS, D)[:, :, :DEPTH * 9].reshape(8 * nex, DEPTH, 9 * D)
    dm_mine = lax.dynamic_slice_in_dim(dm_all.transpose(1, 0, 2), chip * ada_cols, ada_cols, axis=2)
    g_ada, d_ada, nm_ada, nv_ada = ada_bwd_adam(c_all, dm_mine, w_ada, m_w_ada, v_w_ada)

    delta, new_m, new_v = {}, {}, {}
    for n in WEIGHTS:
        if n == "w_ada":
            grads[n], delta[n], new_m[n], new_v[n] = g_ada, d_ada, nm_ada, nv_ada
        elif n not in BIG:
            delta[n], new_m[n], new_v[n] = adamw(w[n], grads[n], mom[n], var[n])
    advance((delta["conv_w"], d_ada), last=True)
    advance(state["token"], last=True)
    assert all(state[k] is None for k in ("to_sibling", "on_ici", "joining"))
    for b, n in enumerate(BIG):
        grads[n] = gshard[b].reshape(w[n].shape)
        delta[n], new_m[n], new_v[n] = (u.reshape(w[n].shape) for u in upd[b])

    return (loss, grad_x, *[grads[n] for n in WEIGHTS], *[delta[n] for n in WEIGHTS],
            *[new_m[n] for n in WEIGHTS], *[new_v[n] for n in WEIGHTS])
```
